```python
import jax, jax.numpy as jnp
from jax import lax
import numpy as np

D_MODEL = 1024
BATCH = 8
SEQ = 2048
DEPTH = 1

CHUNK = 64
MEM_LEN = 256
CONV_DIM = D_MODEL
CONV_WIDTH = 3
SB_HEADS = 16
SB_HEAD_DIM = 64
SB_DIM = SB_HEADS * SB_HEAD_DIM
SB_BLOCK = 128
X_HEADS = 4
X_HEAD_DIM = 256
X_DIM = X_HEADS * X_HEAD_DIM
N_BRANCH = 3
D_FF = 4 * D_MODEL
EPS = 1e-6

IN_SPLITS = [CONV_DIM, CONV_DIM, CONV_DIM, SB_DIM, SB_DIM, SB_DIM, X_DIM, N_BRANCH * D_MODEL]
IN_COLS = int(sum(IN_SPLITS))

kernel_name = "hybrid_conv_stickbreaking_memxattn_block"


def rms_norm(x, g):
    xf = x.astype(jnp.float32)
    y = xf * lax.rsqrt(jnp.mean(xf * xf, axis=-1, keepdims=True) + EPS)
    return (y * g.astype(jnp.float32)).astype(x.dtype)


def causal_depthwise_conv(u, w):
    c = u.shape[-1]
    return lax.conv_general_dilated(
        u, w.astype(u.dtype)[:, None, :], window_strides=(1,),
        padding=[(CONV_WIDTH - 1, 0)],
        dimension_numbers=("NWC", "WIO", "NWC"),
        feature_group_count=c)


def stick_breaking_attention(q, k, v):
    seq = q.shape[2]
    scale = q.shape[-1] ** -0.5
    outs = []
    for start in range(0, seq, SB_BLOCK):
        end = start + SB_BLOCK
        qb = q[:, :, start:end]
        kb = k[:, :, :end]
        vb = v[:, :, :end]
        z = jnp.einsum("bhqd,bhkd->bhqk", qb, kb).astype(jnp.float32) * scale
        t_idx = start + jnp.arange(SB_BLOCK)[:, None]
        s_idx = jnp.arange(end)[None, :]
        past = s_idx < t_idx
        log_beta = jax.nn.log_sigmoid(z)
        log_1mb = jnp.where(past, jax.nn.log_sigmoid(-z), 0.0)
        between = lax.cumsum(log_1mb, axis=3, reverse=True) - log_1mb
        a = jnp.where(past, jnp.exp(log_beta + between), 0.0)
        outs.append(jnp.einsum("bhqk,bhkd->bhqd", a.astype(vb.dtype), vb))
    return jnp.concatenate(outs, axis=2)


def memory_cross_attention(xq, mem_n, w_mem_kv, q_norm_g, k_norm_g):
    b, s, _ = xq.shape
    m = mem_n.shape[1]
    q = xq.reshape(b, s, X_HEADS, X_HEAD_DIM)
    kv = mem_n @ w_mem_kv
    k, v = jnp.split(kv, 2, axis=-1)
    k = k.reshape(b, m, X_HEADS, X_HEAD_DIM)
    v = v.reshape(b, m, X_HEADS, X_HEAD_DIM)
    q = rms_norm(q, q_norm_g)
    k = rms_norm(k, k_norm_g)
    scores = jnp.einsum("bqhd,bkhd->bhqk", q, k).astype(jnp.float32) * (X_HEAD_DIM ** -0.5)
    p = jax.nn.softmax(scores, axis=-1).astype(v.dtype)
    o = jnp.einsum("bhqk,bkhd->bqhd", p, v)
    return o.reshape(b, s, X_DIM)


def _fwd_setup_inputs(seed: int = 0) -> dict:
    key = jax.random.key(seed)
    ks = jax.random.split(key, 20)
    f32 = jnp.float32

    def nrm(k, shape, fan_in):
        return jax.random.normal(k, shape, f32) * (fan_in ** -0.5)

    def gain(k, shape):
        return 1.0 + 0.02 * jax.random.normal(k, shape, f32)

    L = DEPTH
    return {
        "x": jax.random.normal(ks[0], (BATCH, SEQ, D_MODEL), f32),
        "mem": jax.random.normal(ks[1], (BATCH, MEM_LEN, D_MODEL), f32),
        "g_mix": gain(ks[2], (L, D_MODEL)),
        "g_mem": gain(ks[3], (L, D_MODEL)),
        "w_in": nrm(ks[4], (L, D_MODEL, IN_COLS), D_MODEL),
        "conv_w": nrm(ks[5], (L, CONV_WIDTH, CONV_DIM), CONV_WIDTH),
        "w_conv_out": nrm(ks[6], (L, CONV_DIM, D_MODEL), CONV_DIM),
        "w_sb_out": nrm(ks[7], (L, SB_DIM, D_MODEL), SB_DIM),
        "q_norm_g": gain(ks[8], (L, X_HEAD_DIM)),
        "k_norm_g": gain(ks[9], (L, X_HEAD_DIM)),
        "w_mem_kv": nrm(ks[10], (L, D_MODEL, 2 * X_DIM), D_MODEL),
        "w_x_out": nrm(ks[11], (L, X_DIM, D_MODEL), X_DIM),
        "w_out": nrm(ks[12], (L, D_MODEL, D_MODEL), D_MODEL),
        "g_mlp": gain(ks[13], (L, D_MODEL)),
        "w_up": nrm(ks[14], (L, D_MODEL, D_FF), D_MODEL),
        "w_down": nrm(ks[15], (L, D_FF, D_MODEL), D_FF),
    }


def _fwd_reference(x, mem, g_mix, g_mem, w_in, conv_w, w_conv_out, w_sb_out, q_norm_g,
              k_norm_g, w_mem_kv, w_x_out, w_out, g_mlp, w_up, w_down):
    b, s, d = x.shape
    split_idx = list(np.cumsum(IN_SPLITS)[:-1])
    for l in range(DEPTH):
        h = rms_norm(x, g_mix[l])
        proj = h @ w_in[l]
        c_h, c_b, c_c, sq, sk, sv, xq, gate_pre = jnp.split(proj, split_idx, axis=-1)

        y_conv = (c_b * causal_depthwise_conv(c_c * c_h, conv_w[l])) @ w_conv_out[l]

        to_heads = lambda t: t.reshape(b, s, SB_HEADS, SB_HEAD_DIM).transpose(0, 2, 1, 3)
        o_sb = stick_breaking_attention(to_heads(sq), to_heads(sk), to_heads(sv))
        y_sb = o_sb.transpose(0, 2, 1, 3).reshape(b, s, SB_DIM) @ w_sb_out[l]

        mem_n = rms_norm(mem, g_mem[l])
        y_x = memory_cross_attention(xq, mem_n, w_mem_kv[l], q_norm_g[l], k_norm_g[l]) @ w_x_out[l]

        gates = jax.nn.sigmoid(gate_pre.astype(jnp.float32)).astype(x.dtype)
        gates = gates.reshape(b, s, N_BRANCH, d)
        merged = gates[:, :, 0] * y_conv + gates[:, :, 1] * y_sb + gates[:, :, 2] * y_x
        x = x + merged @ w_out[l]

        h2 = rms_norm(x, g_mlp[l])
        x = x + jnp.square(jax.nn.relu(h2 @ w_up[l])) @ w_down[l]
    return x


import jax as _jax
import jax.numpy as _jnp

TWIN_FORMAT = 'train_step'
FWD_PARAMS = ['x', 'mem', 'g_mix', 'g_mem', 'w_in', 'conv_w', 'w_conv_out', 'w_sb_out', 'q_norm_g', 'k_norm_g', 'w_mem_kv', 'w_x_out', 'w_out', 'g_mlp', 'w_up', 'w_down']
TWIN_WEIGHTS = ['g_mix', 'g_mem', 'w_in', 'conv_w', 'w_conv_out', 'w_sb_out', 'q_norm_g', 'k_norm_g', 'w_mem_kv', 'w_x_out', 'w_out', 'g_mlp', 'w_up', 'w_down']
TWIN_DIFF_INPUT = 'x'
TWIN_INPUTS = ['x', 'mem', 'g_mix', 'g_mem', 'w_in', 'conv_w', 'w_conv_out', 'w_sb_out', 'q_norm_g', 'k_norm_g', 'w_mem_kv', 'w_x_out', 'w_out', 'g_mlp', 'w_up', 'w_down', 'loss_target', 'm_g_mix', 'm_g_mem', 'm_w_in', 'm_conv_w', 'm_w_conv_out', 'm_w_sb_out', 'm_q_norm_g', 'm_k_norm_g', 'm_w_mem_kv', 'm_w_x_out', 'm_w_out', 'm_g_mlp', 'm_w_up', 'm_w_down', 'v_g_mix', 'v_g_mem', 'v_w_in', 'v_conv_w', 'v_w_conv_out', 'v_w_sb_out', 'v_q_norm_g', 'v_k_norm_g', 'v_w_mem_kv', 'v_w_x_out', 'v_w_out', 'v_g_mlp', 'v_w_up', 'v_w_down']
TWIN_OUTPUTS = ['loss', 'grad_x', 'grad_g_mix', 'grad_g_mem', 'grad_w_in', 'grad_conv_w', 'grad_w_conv_out', 'grad_w_sb_out', 'grad_q_norm_g', 'grad_k_norm_g', 'grad_w_mem_kv', 'grad_w_x_out', 'grad_w_out', 'grad_g_mlp', 'grad_w_up', 'grad_w_down', 'delta_g_mix', 'delta_g_mem', 'delta_w_in', 'delta_conv_w', 'delta_w_conv_out', 'delta_w_sb_out', 'delta_q_norm_g', 'delta_k_norm_g', 'delta_w_mem_kv', 'delta_w_x_out', 'delta_w_out', 'delta_g_mlp', 'delta_w_up', 'delta_w_down', 'new_m_g_mix', 'new_m_g_mem', 'new_m_w_in', 'new_m_conv_w', 'new_m_w_conv_out', 'new_m_w_sb_out', 'new_m_q_norm_g', 'new_m_k_norm_g', 'new_m_w_mem_kv', 'new_m_w_x_out', 'new_m_w_out', 'new_m_g_mlp', 'new_m_w_up', 'new_m_w_down', 'new_v_g_mix', 'new_v_g_mem', 'new_v_w_in', 'new_v_conv_w', 'new_v_w_conv_out', 'new_v_w_sb_out', 'new_v_q_norm_g', 'new_v_k_norm_g', 'new_v_w_mem_kv', 'new_v_w_x_out', 'new_v_w_out', 'new_v_g_mlp', 'new_v_w_up', 'new_v_w_down']
TWIN_LEAF_KINDS = {'loss': 'loss', 'grad_x': 'grad_x', 'grad_g_mix': 'grad_w', 'grad_g_mem': 'grad_w', 'grad_w_in': 'grad_w', 'grad_conv_w': 'grad_w', 'grad_w_conv_out': 'grad_w', 'grad_w_sb_out': 'grad_w', 'grad_q_norm_g': 'grad_w', 'grad_k_norm_g': 'grad_w', 'grad_w_mem_kv': 'grad_w', 'grad_w_x_out': 'grad_w', 'grad_w_out': 'grad_w', 'grad_g_mlp': 'grad_w', 'grad_w_up': 'grad_w', 'grad_w_down': 'grad_w', 'delta_g_mix': 'delta_w', 'delta_g_mem': 'delta_w', 'delta_w_in': 'delta_w', 'delta_conv_w': 'delta_w', 'delta_w_conv_out': 'delta_w', 'delta_w_sb_out': 'delta_w', 'delta_q_norm_g': 'delta_w', 'delta_k_norm_g': 'delta_w', 'delta_w_mem_kv': 'delta_w', 'delta_w_x_out': 'delta_w', 'delta_w_out': 'delta_w', 'delta_g_mlp': 'delta_w', 'delta_w_up': 'delta_w', 'delta_w_down': 'delta_w', 'new_m_g_mix': 'new_m', 'new_m_g_mem': 'new_m', 'new_m_w_in': 'new_m', 'new_m_conv_w': 'new_m', 'new_m_w_conv_out': 'new_m', 'new_m_w_sb_out': 'new_m', 'new_m_q_norm_g': 'new_m', 'new_m_k_norm_g': 'new_m', 'new_m_w_mem_kv': 'new_m', 'new_m_w_x_out': 'new_m', 'new_m_w_out': 'new_m', 'new_m_g_mlp': 'new_m', 'new_m_w_up': 'new_m', 'new_m_w_down': 'new_m', 'new_v_g_mix': 'new_v', 'new_v_g_mem': 'new_v', 'new_v_w_in': 'new_v', 'new_v_conv_w': 'new_v', 'new_v_w_conv_out': 'new_v', 'new_v_w_sb_out': 'new_v', 'new_v_q_norm_g': 'new_v', 'new_v_k_norm_g': 'new_v', 'new_v_w_mem_kv': 'new_v', 'new_v_w_x_out': 'new_v', 'new_v_w_out': 'new_v', 'new_v_g_mlp': 'new_v', 'new_v_w_up': 'new_v', 'new_v_w_down': 'new_v'}


def _forward(args):
    return _fwd_reference(*[args[k] for k in FWD_PARAMS])


def _output_shape():
    out = _jax.eval_shape(lambda: _forward(_fwd_setup_inputs(0)))
    return out.shape, out.dtype

N_MICROBATCH = 1
ADAM_LR = 0.001
ADAM_B1 = 0.9
ADAM_B2 = 0.999
ADAM_EPS = 1e-08
ADAM_WD = 0.01
ADAM_STEP = 10
PER_EXAMPLE_BATCH_AXIS = {'x': 0, 'mem': 0, 'loss_target': 0}
SHARED_INPUTS = []
_WEIGHT_DTYPES = {'g_mix': _jnp.float32, 'g_mem': _jnp.float32, 'w_in': _jnp.float32, 'conv_w': _jnp.float32, 'w_conv_out': _jnp.float32, 'w_sb_out': _jnp.float32, 'q_norm_g': _jnp.float32, 'k_norm_g': _jnp.float32, 'w_mem_kv': _jnp.float32, 'w_x_out': _jnp.float32, 'w_out': _jnp.float32, 'g_mlp': _jnp.float32, 'w_up': _jnp.float32, 'w_down': _jnp.float32}
MOMENT_SCALE = {'g_mix': 1.705035e+01, 'g_mem': 1.929730e-01, 'w_in': 1.940052e-01, 'conv_w': 2.644654e+00, 'w_conv_out': 3.589678e-01, 'w_sb_out': 2.049393e-01, 'q_norm_g': 1.800320e-01, 'k_norm_g': 1.797950e-01, 'w_mem_kv': 1.383393e-01, 'w_x_out': 1.953917e-01, 'w_out': 5.047055e-01, 'g_mlp': 4.832180e+01, 'w_up': 4.721696e-01, 'w_down': 4.041272e+00}


def _to_microbatches(a, axis):
    t = _jnp.moveaxis(a, axis, 0)
    t = t.reshape((N_MICROBATCH, t.shape[0] // N_MICROBATCH) + t.shape[1:])
    return _jnp.moveaxis(t, 1, axis + 1)


def setup_inputs(seed: int = 0) -> dict:
    inp = _fwd_setup_inputs(seed)
    key = _jax.random.fold_in(_jax.random.key(seed), 7919)
    shape, _ = _output_shape()
    out = dict(inp)
    out["loss_target"] = _jax.random.normal(_jax.random.fold_in(key, 0), shape, _jnp.float32)
    for i, name in enumerate(TWIN_WEIGHTS):
        w = inp[name].astype(_jnp.float32)
        if MOMENT_SCALE is None:
            s = _jnp.sqrt(_jnp.mean(_jnp.square(w)) + 1e-30)
        else:
            s = MOMENT_SCALE[name]
        km, kv = _jax.random.split(_jax.random.fold_in(key, i + 1))
        out[name] = w
        out["m_" + name] = s * _jax.random.normal(km, w.shape, _jnp.float32)
        out["v_" + name] = (s * s) * _jax.random.uniform(kv, w.shape, _jnp.float32, 0.5, 1.5)
    if N_MICROBATCH > 1:
        for name, axis in PER_EXAMPLE_BATCH_AXIS.items():
            out[name] = _to_microbatches(out[name], axis)
    return {'x': out['x'], 'mem': out['mem'], 'g_mix': out['g_mix'], 'g_mem': out['g_mem'], 'w_in': out['w_in'], 'conv_w': out['conv_w'], 'w_conv_out': out['w_conv_out'], 'w_sb_out': out['w_sb_out'], 'q_norm_g': out['q_norm_g'], 'k_norm_g': out['k_norm_g'], 'w_mem_kv': out['w_mem_kv'], 'w_x_out': out['w_x_out'], 'w_out': out['w_out'], 'g_mlp': out['g_mlp'], 'w_up': out['w_up'], 'w_down': out['w_down'], 'loss_target': out['loss_target'], 'm_g_mix': out['m_g_mix'], 'm_g_mem': out['m_g_mem'], 'm_w_in': out['m_w_in'], 'm_conv_w': out['m_conv_w'], 'm_w_conv_out': out['m_w_conv_out'], 'm_w_sb_out': out['m_w_sb_out'], 'm_q_norm_g': out['m_q_norm_g'], 'm_k_norm_g': out['m_k_norm_g'], 'm_w_mem_kv': out['m_w_mem_kv'], 'm_w_x_out': out['m_w_x_out'], 'm_w_out': out['m_w_out'], 'm_g_mlp': out['m_g_mlp'], 'm_w_up': out['m_w_up'], 'm_w_down': out['m_w_down'], 'v_g_mix': out['v_g_mix'], 'v_g_mem': out['v_g_mem'], 'v_w_in': out['v_w_in'], 'v_conv_w': out['v_conv_w'], 'v_w_conv_out': out['v_w_conv_out'], 'v_w_sb_out': out['v_w_sb_out'], 'v_q_norm_g': out['v_q_norm_g'], 'v_k_norm_g': out['v_k_norm_g'], 'v_w_mem_kv': out['v_w_mem_kv'], 'v_w_x_out': out['v_w_x_out'], 'v_w_out': out['v_w_out'], 'v_g_mlp': out['v_g_mlp'], 'v_w_up': out['v_w_up'], 'v_w_down': out['v_w_down']}


def _loss(weights, diff, rest, loss_target):
    with _jax.named_scope("forward"):
        args = {**rest, TWIN_DIFF_INPUT: diff, **{k: w.astype(_WEIGHT_DTYPES[k]) for k, w in weights.items()}}
        y = _forward(args)
    with _jax.named_scope("loss_head"):
        err = _jnp.square(y.astype(_jnp.float32) - loss_target)
        return 0.5 * _jnp.sum(_jnp.mean(err, axis=-1)) if err.ndim else 0.5 * err


def _adamw(w, g, m, v):
    m = ADAM_B1 * m + (1.0 - ADAM_B1) * g
    v = ADAM_B2 * v + (1.0 - ADAM_B2) * _jnp.square(g)
    m_hat = m / (1.0 - ADAM_B1 ** ADAM_STEP)
    v_hat = v / (1.0 - ADAM_B2 ** ADAM_STEP)
    delta = -ADAM_LR * (m_hat / (_jnp.sqrt(v_hat) + ADAM_EPS) + ADAM_WD * w)
    return delta, m, v


def reference(x, mem, g_mix, g_mem, w_in, conv_w, w_conv_out, w_sb_out, q_norm_g, k_norm_g, w_mem_kv, w_x_out, w_out, g_mlp, w_up, w_down, loss_target, m_g_mix, m_g_mem, m_w_in, m_conv_w, m_w_conv_out, m_w_sb_out, m_q_norm_g, m_k_norm_g, m_w_mem_kv, m_w_x_out, m_w_out, m_g_mlp, m_w_up, m_w_down, v_g_mix, v_g_mem, v_w_in, v_conv_w, v_w_conv_out, v_w_sb_out, v_q_norm_g, v_k_norm_g, v_w_mem_kv, v_w_x_out, v_w_out, v_g_mlp, v_w_up, v_w_down):
    given = dict(x=x, mem=mem, g_mix=g_mix, g_mem=g_mem, w_in=w_in, conv_w=conv_w, w_conv_out=w_conv_out, w_sb_out=w_sb_out, q_norm_g=q_norm_g, k_norm_g=k_norm_g, w_mem_kv=w_mem_kv, w_x_out=w_x_out, w_out=w_out, g_mlp=g_mlp, w_up=w_up, w_down=w_down, loss_target=loss_target, m_g_mix=m_g_mix, m_g_mem=m_g_mem, m_w_in=m_w_in, m_conv_w=m_conv_w, m_w_conv_out=m_w_conv_out, m_w_sb_out=m_w_sb_out, m_q_norm_g=m_q_norm_g, m_k_norm_g=m_k_norm_g, m_w_mem_kv=m_w_mem_kv, m_w_x_out=m_w_x_out, m_w_out=m_w_out, m_g_mlp=m_g_mlp, m_w_up=m_w_up, m_w_down=m_w_down, v_g_mix=v_g_mix, v_g_mem=v_g_mem, v_w_in=v_w_in, v_conv_w=v_conv_w, v_w_conv_out=v_w_conv_out, v_w_sb_out=v_w_sb_out, v_q_norm_g=v_q_norm_g, v_k_norm_g=v_k_norm_g, v_w_mem_kv=v_w_mem_kv, v_w_x_out=v_w_x_out, v_w_out=v_w_out, v_g_mlp=v_g_mlp, v_w_up=v_w_up, v_w_down=v_w_down)
    weights = {n: given[n] for n in TWIN_WEIGHTS}
    shared = {n: given[n] for n in SHARED_INPUTS}
    per_example = {n: given[n] for n in ['x', 'mem']}
    grad_fn = _jax.value_and_grad(_loss, argnums=(0, 1))

    def one_microbatch(ex, loss_target):
        ex = dict(ex)
        diff = ex.pop(TWIN_DIFF_INPUT)
        return grad_fn(weights, diff, {**shared, **ex}, loss_target)

    if N_MICROBATCH == 1:
        loss, (grad_w, grad_x) = one_microbatch(per_example, given["loss_target"])
    else:
        def body(carry, xs):
            loss_sum, grad_sum = carry
            l_k, (gw_k, gx_k) = one_microbatch(xs[0], xs[1])
            with _jax.named_scope("update"):
                return (loss_sum + l_k, _jax.tree.map(_jnp.add, grad_sum, gw_k)), gx_k

        init = (_jnp.zeros((), _jnp.float32), _jax.tree.map(_jnp.zeros_like, weights))
        (loss, grad_w), grad_x = _jax.lax.scan(body, init, (per_example, given["loss_target"]))
    with _jax.named_scope("update"):
        delta_w, new_m, new_v = {}, {}, {}
        for n in TWIN_WEIGHTS:
            delta_w[n], new_m[n], new_v[n] = _adamw(weights[n], grad_w[n], given["m_" + n], given["v_" + n])
    return (loss, grad_x, *[grad_w[n] for n in TWIN_WEIGHTS], *[delta_w[n] for n in TWIN_WEIGHTS],
            *[new_m[n] for n in TWIN_WEIGHTS], *[new_v[n] for n in TWIN_WEIGHTS])
```

```python
import functools

import jax
import jax.numpy as jnp
from jax import lax
from jax.experimental import pallas as pl
from jax.experimental.pallas import tpu as pltpu

F32 = jnp.float32
BF16 = jnp.bfloat16
EPS = 1e-6
N_CHIP = 4
SB_HEAD_DIM = 64
X_HEAD_DIM = 256
LANES = 128
VMEM_LIMIT = 56 * 1024 * 1024
ADAM_LR, ADAM_B1, ADAM_B2, ADAM_EPS, ADAM_WD, ADAM_STEP = 0.001, 0.9, 0.999, 1e-8, 0.01, 10
MESH = pl.DeviceIdType.MESH


def _params(*sem):
    return pltpu.CompilerParams(dimension_semantics=sem, vmem_limit_bytes=VMEM_LIMIT)


def _tile(n, pref):
    if n <= pref:
        return n
    t = 1 << (pref.bit_length() - 1)
    while n % t:
        t //= 2
    return t


NN = (((1,), (0,)), ((), ()))
NT = (((1,), (1,)), ((), ()))
TN = (((0,), (0,)), ((), ()))


def _dot(a, b, dims):
    return lax.dot_general(a.astype(BF16), b.astype(BF16), dims, preferred_element_type=F32)


def mm_nn_shard(a, g, name, relu2=False):
    M, K = a.shape
    _, _, Ns = g.shape
    tm, tn = _tile(M, 1024), _tile(Ns, 512)
    nb = Ns // tn

    def body(a_ref, b_ref, *o_refs):
        acc = _dot(a_ref[...], b_ref[...], NN)
        o_refs[0][...] = acc
        if relu2:
            r = jnp.maximum(acc, 0.0)
            o_refs[1][...] = (r * r).astype(BF16)

    o_spec = pl.BlockSpec((tm, tn), lambda i, j: (i, j))
    shapes = [jax.ShapeDtypeStruct((M, N_CHIP * Ns), F32)]
    specs = [o_spec]
    if relu2:
        shapes.append(jax.ShapeDtypeStruct((M, N_CHIP * Ns), BF16))
        specs.append(o_spec)
    out = pl.pallas_call(
        body, grid=(M // tm, N_CHIP * nb), name=name,
        in_specs=[pl.BlockSpec((tm, K), lambda i, j: (i, 0)),
                  pl.BlockSpec((None, K, tn), lambda i, j: (j // nb, 0, j % nb))],
        out_specs=specs, out_shape=shapes, compiler_params=_params("parallel", "parallel"),
    )(a, g)
    return out if relu2 else out[0]


def mm_nn(a, w, name, res=None, out_dtype=F32):
    M, K = a.shape
    N = w.shape[1]
    tm, tn = _tile(M, 1024 if K <= 2048 else 512), _tile(N, 512)

    def body(a_ref, b_ref, *refs):
        acc = _dot(a_ref[...], b_ref[...], NN)
        if res is not None:
            acc = refs[0][...] + acc
        refs[-1][...] = acc.astype(out_dtype)

    o_spec = pl.BlockSpec((tm, tn), lambda i, j: (i, j))
    ins = [a, w] + ([res] if res is not None else [])
    return pl.pallas_call(
        body, grid=(M // tm, N // tn), name=name,
        in_specs=[pl.BlockSpec((tm, K), lambda i, j: (i, 0)), pl.BlockSpec((K, tn), lambda i, j: (0, j))]
        + ([o_spec] if res is not None else []),
        out_specs=o_spec, out_shape=jax.ShapeDtypeStruct((M, N), out_dtype),
        compiler_params=_params("parallel", "parallel"),
    )(*ins)


def mm_nt(a, w, name, up=None, out_dtype=F32):
    M, N = a.shape
    R = w.shape[0]
    tm, tr = _tile(M, 1024), _tile(R, 512)

    def body(a_ref, b_ref, *refs):
        acc = _dot(a_ref[...], b_ref[...], NT)
        if up is not None:
            acc = acc * (2.0 * jnp.maximum(refs[0][...], 0.0))
        refs[-1][...] = acc.astype(out_dtype)

    o_spec = pl.BlockSpec((tm, tr), lambda i, j: (i, j))
    ins = [a, w] + ([up] if up is not None else [])
    return pl.pallas_call(
        body, grid=(M // tm, R // tr), name=name,
        in_specs=[pl.BlockSpec((tm, N), lambda i, j: (i, 0)), pl.BlockSpec((tr, N), lambda i, j: (j, 0))]
        + ([o_spec] if up is not None else []),
        out_specs=o_spec, out_shape=jax.ShapeDtypeStruct((M, R), out_dtype),
        compiler_params=_params("parallel", "parallel"),
    )(*ins)


def mm_nt_shard(a, g, name, out_dtype=F32):
    M = a.shape[0]
    _, R, Ns = g.shape
    tm, tr, tk = _tile(M, 1024), _tile(R, 512), _tile(Ns, 1024)
    nb = Ns // tk
    nk = N_CHIP * nb

    def body(a_ref, b_ref, o_ref, acc_ref):
        k = pl.program_id(2)

        @pl.when(k == 0)
        def _():
            acc_ref[...] = jnp.zeros_like(acc_ref)

        acc_ref[...] += _dot(a_ref[...], b_ref[...], NT)

        @pl.when(k == nk - 1)
        def _():
            o_ref[...] = acc_ref[...].astype(out_dtype)

    return pl.pallas_call(
        body, grid=(M // tm, R // tr, nk), name=name,
        in_specs=[pl.BlockSpec((tm, tk), lambda i, j, k: (i, k)),
                  pl.BlockSpec((None, tr, tk), lambda i, j, k: (k // nb, j, k % nb))],
        out_specs=pl.BlockSpec((tm, tr), lambda i, j, k: (i, j)),
        out_shape=jax.ShapeDtypeStruct((M, R), out_dtype),
        scratch_shapes=[pltpu.VMEM((tm, tr), F32)],
        compiler_params=_params("parallel", "parallel", "arbitrary"),
    )(a, g)


def mm_tn(a, b, name, shard_out=False):
    S, M = a.shape
    N = b.shape[1]
    Ns = N // N_CHIP if shard_out else N
    tm, tn = _tile(M, 512), _tile(Ns, 512)
    nb = Ns // tn

    def body(a_ref, b_ref, o_ref):
        o_ref[...] = _dot(a_ref[...], b_ref[...], TN)

    if shard_out:
        o_spec = pl.BlockSpec((None, tm, tn), lambda i, j: (j // nb, i, j % nb))
        o_shape = jax.ShapeDtypeStruct((N_CHIP, M, Ns), F32)
    else:
        o_spec = pl.BlockSpec((tm, tn), lambda i, j: (i, j))
        o_shape = jax.ShapeDtypeStruct((M, N), F32)
    return pl.pallas_call(
        body, grid=(M // tm, N // tn), name=name,
        in_specs=[pl.BlockSpec((S, tm), lambda i, j: (0, i)), pl.BlockSpec((S, tn), lambda i, j: (0, j))],
        out_specs=o_spec, out_shape=o_shape, compiler_params=_params("parallel", "parallel"),
    )(a, b)


def rms_fwd(x, g, name):
    S, D = x.shape
    tm = _tile(S, 512)

    def body(x_ref, g_ref, h_ref):
        xv = x_ref[...]
        r = lax.rsqrt(jnp.mean(xv * xv, axis=-1, keepdims=True) + EPS)
        h_ref[...] = ((xv * r) * g_ref[...]).astype(BF16)

    return pl.pallas_call(
        body, grid=(S // tm,), name=name,
        in_specs=[pl.BlockSpec((tm, D), lambda i: (i, 0)), pl.BlockSpec((1, D), lambda i: (0, 0))],
        out_specs=pl.BlockSpec((tm, D), lambda i: (i, 0)),
        out_shape=jax.ShapeDtypeStruct((S, D), BF16), compiler_params=_params("parallel"),
    )(x, g)


def rms_bwd(x, g, dh, name, dres=None, want_dx=True):
    S, D = x.shape
    tm = _tile(S, 512)

    def body(x_ref, g_ref, dh_ref, *refs):
        i = pl.program_id(0)
        xv = x_ref[...]
        r = lax.rsqrt(jnp.mean(xv * xv, axis=-1, keepdims=True) + EPS)
        xn = xv * r
        dhv = dh_ref[...].astype(F32)
        gg_ref = refs[-1]

        @pl.when(i == 0)
        def _():
            gg_ref[...] = jnp.zeros_like(gg_ref)

        gg_ref[...] += jnp.sum(dhv * xn, axis=0, keepdims=True)
        if want_dx:
            dxn = dhv * g_ref[...]
            dx = r * (dxn - xn * jnp.mean(dxn * xn, axis=-1, keepdims=True))
            if dres is not None:
                dx = refs[0][...] + dx
            refs[-2][...] = dx

    row = pl.BlockSpec((tm, D), lambda i: (i, 0))
    vec = pl.BlockSpec((1, D), lambda i: (0, 0))
    ins, in_specs = [x, g, dh], [row, vec, row]
    if dres is not None:
        ins.append(dres)
        in_specs.append(row)
    shapes, specs = [jax.ShapeDtypeStruct((1, D), F32)], [vec]
    if want_dx:
        shapes.insert(0, jax.ShapeDtypeStruct((S, D), F32))
        specs.insert(0, row)
    out = pl.pallas_call(body, grid=(S // tm,), name=name, in_specs=in_specs, out_specs=specs,
                         out_shape=shapes, compiler_params=_params("arbitrary"))(*ins)
    return out if want_dx else out[0]


def _shift_down(u, k, row):
    return jnp.where(row >= k, pltpu.roll(u, k, axis=0), 0.0)


def _shift_up(u, k, row):
    S = u.shape[0]
    return jnp.where(row < S - k, pltpu.roll(u, S - k, axis=0), 0.0)


def _conv_specs(S, D, tc):
    nb = D // tc
    col = lambda o: pl.BlockSpec((S, tc), lambda j, o=o: (0, o * nb + j))
    return col, pl.BlockSpec((3, tc), lambda j: (0, j))


def conv_fwd(proj, conv_w, D, name):
    S = proj.shape[0]
    tc = _tile(D, 256)
    col, wspec = _conv_specs(S, D, tc)

    def body(ch_ref, cb_ref, cc_ref, w_ref, a_ref):
        row = lax.broadcasted_iota(jnp.int32, (S, tc), 0)
        u = cc_ref[...] * ch_ref[...]
        w = w_ref[...]
        cv = w[0:1, :] * _shift_down(u, 2, row) + w[1:2, :] * _shift_down(u, 1, row) + w[2:3, :] * u
        a_ref[...] = (cb_ref[...] * cv).astype(BF16)

    return pl.pallas_call(
        body, grid=(D // tc,), name=name, in_specs=[col(0), col(1), col(2), wspec],
        out_specs=pl.BlockSpec((S, tc), lambda j: (0, j)),
        out_shape=jax.ShapeDtypeStruct((S, D), BF16), compiler_params=_params("parallel"),
    )(proj, proj, proj, conv_w)


def conv_bwd(proj, conv_w, da, D, name):
    S = proj.shape[0]
    tc = _tile(D, 256)
    col, wspec = _conv_specs(S, D, tc)
    blk = pl.BlockSpec((S, tc), lambda j: (0, j))

    def body(ch_ref, cb_ref, cc_ref, w_ref, da_ref, dch_ref, dcb_ref, dcc_ref, gw_ref):
        row = lax.broadcasted_iota(jnp.int32, (S, tc), 0)
        ch, cb, cc, dav = ch_ref[...], cb_ref[...], cc_ref[...], da_ref[...]
        w = w_ref[...]
        u = cc * ch
        u1, u2 = _shift_down(u, 1, row), _shift_down(u, 2, row)
        cv = w[0:1, :] * u2 + w[1:2, :] * u1 + w[2:3, :] * u
        dcb_ref[...] = (dav * cv).astype(BF16)
        dcv = dav * cb
        gw_ref[0:1, :] = jnp.sum(dcv * u2, axis=0, keepdims=True)
        gw_ref[1:2, :] = jnp.sum(dcv * u1, axis=0, keepdims=True)
        gw_ref[2:3, :] = jnp.sum(dcv * u, axis=0, keepdims=True)
        du = w[2:3, :] * dcv + w[1:2, :] * _shift_up(dcv, 1, row) + w[0:1, :] * _shift_up(dcv, 2, row)
        dcc_ref[...] = (du * ch).astype(BF16)
        dch_ref[...] = (du * cc).astype(BF16)

    act = jax.ShapeDtypeStruct((S, D), BF16)
    return pl.pallas_call(
        body, grid=(D // tc,), name=name, in_specs=[col(0), col(1), col(2), wspec, blk],
        out_specs=[blk, blk, blk, wspec], out_shape=[act, act, act, jax.ShapeDtypeStruct((3, D), F32)],
        compiler_params=_params("parallel"),
    )(proj, proj, proj, conv_w, da)


SB_BQ = 128
SB_BK = 128


def _sb_consts(bq):
    lane = lax.broadcasted_iota(jnp.int32, (bq, LANES), 1)
    r = lax.broadcasted_iota(jnp.int32, (SB_BK, 2 * SB_BK), 0)
    c = lax.broadcasted_iota(jnp.int32, (SB_BK, 2 * SB_BK), 1)
    tri_rev = jnp.where((c >= SB_BK) | (r > c), 1.0, 0.0).astype(BF16)
    tri_fwd = jnp.where((c >= SB_BK) | (r < c), 1.0, 0.0).astype(BF16)
    return lane, tri_rev, tri_fwd


def _cumsum2(v, tri):
    hi = v.astype(BF16)
    lo = (v - hi.astype(F32)).astype(BF16)
    return (lax.dot_general(hi, tri, NN, preferred_element_type=F32)
            + lax.dot_general(lo, tri, NN, preferred_element_type=F32))


def _sb_logits(qh, kb, past):
    z = lax.dot_general(qh, kb, NT, preferred_element_type=F32)
    sp = jnp.log(1.0 + jnp.exp(-jnp.abs(z)))
    l = jnp.minimum(z, 0.0) - sp
    m = l - z
    if past is not None:
        m = jnp.where(past, m, 0.0)
    return l, m


def _sb_specs(S, D, bq):
    npair = D // LANES
    qspec = pl.BlockSpec((bq, LANES), lambda p, i: (i, 3 * npair + p))
    kspec = pl.BlockSpec((S, LANES), lambda p, i: (0, 4 * npair + p))
    vspec = pl.BlockSpec((S, LANES), lambda p, i: (0, 5 * npair + p))
    return npair, qspec, kspec, vspec


def sb_fwd(proj, D, name):
    S = proj.shape[0]
    bq = SB_BQ
    nd = bq // SB_BK
    npair, qspec, kspec, vspec = _sb_specs(S, D, bq)
    scale = SB_HEAD_DIM ** -0.5

    def body(q_ref, k_ref, v_ref, o_ref, kb_ref, vb_ref):
        i = pl.program_id(1)

        @pl.when(i == 0)
        def _():
            kb_ref[...] = k_ref[...].astype(BF16)
            vb_ref[...] = v_ref[...].astype(BF16)

        lane, tri_rev, _ = _sb_consts(bq)
        qv = q_ref[...] * scale
        qh = [jnp.where(lane < SB_HEAD_DIM, qv, 0.0).astype(BF16),
              jnp.where(lane >= SB_HEAD_DIM, qv, 0.0).astype(BF16)]
        trow = i * bq + lax.broadcasted_iota(jnp.int32, (bq, SB_BK), 0)
        scol = lax.broadcasted_iota(jnp.int32, (bq, SB_BK), 1)

        def step(j, carry, masked):
            ks = pl.multiple_of(j * SB_BK, SB_BK)
            kb = kb_ref[pl.ds(ks, SB_BK), :]
            vb = vb_ref[pl.ds(ks, SB_BK), :]
            past = (ks + scol) < trow if masked else None
            new = []
            for h in range(2):
                c, acc = carry[h]
                l, m = _sb_logits(qh[h], kb, past)
                cs = _cumsum2(m, tri_rev)
                a = jnp.exp(l + cs[:, :SB_BK] + c)
                if masked:
                    a = jnp.where(past, a, 0.0)
                acc = acc + lax.dot_general(a.astype(BF16), vb, NN, preferred_element_type=F32)
                new.append((c + cs[:, SB_BK:], acc))
            return tuple(new)

        zero = jnp.zeros((bq, LANES), F32)
        carry = ((zero, zero), (zero, zero))
        for d in range(nd - 1, -1, -1):
            carry = step(i * nd + d, carry, True)
        carry = lax.fori_loop(0, i * nd, lambda jj, cr: step(i * nd - 1 - jj, cr, False), carry)
        o_ref[...] = jnp.where(lane < SB_HEAD_DIM, carry[0][1], carry[1][1])

    return pl.pallas_call(
        body, grid=(npair, S // bq), name=name, in_specs=[qspec, kspec, vspec],
        out_specs=pl.BlockSpec((bq, LANES), lambda p, i: (i, p)),
        out_shape=jax.ShapeDtypeStruct((S, D), F32),
        scratch_shapes=[pltpu.VMEM((S, LANES), BF16), pltpu.VMEM((S, LANES), BF16)],
        compiler_params=_params("parallel", "arbitrary"),
    )(proj, proj, proj)


def sb_bwd(proj, do, D, name):
    S = proj.shape[0]
    bq = SB_BQ
    nd = bq // SB_BK
    nkb = S // SB_BK
    npair, qspec, kspec, vspec = _sb_specs(S, D, bq)
    scale = SB_HEAD_DIM ** -0.5

    def body(q_ref, k_ref, v_ref, do_ref, dq_ref, dk_ref, dv_ref,
             kb_ref, vb_ref, dk_acc, dv_acc, g_scr, b_scr, a_scr):
        i = pl.program_id(1)

        @pl.when(i == 0)
        def _():
            kb_ref[...] = k_ref[...].astype(BF16)
            vb_ref[...] = v_ref[...].astype(BF16)
            dk_acc[...] = jnp.zeros_like(dk_acc)
            dv_acc[...] = jnp.zeros_like(dv_acc)

        lane, tri_rev, tri_fwd = _sb_consts(bq)
        qv = q_ref[...] * scale
        dov = do_ref[...]
        qh = [jnp.where(lane < SB_HEAD_DIM, qv, 0.0).astype(BF16),
              jnp.where(lane >= SB_HEAD_DIM, qv, 0.0).astype(BF16)]
        doh = [jnp.where(lane < SB_HEAD_DIM, dov, 0.0).astype(BF16),
               jnp.where(lane >= SB_HEAD_DIM, dov, 0.0).astype(BF16)]
        trow = i * bq + lax.broadcasted_iota(jnp.int32, (bq, SB_BK), 0)
        scol = lax.broadcasted_iota(jnp.int32, (bq, SB_BK), 1)

        def sweep1(j, carry, masked):
            ks = pl.multiple_of(j * SB_BK, SB_BK)
            kb = kb_ref[pl.ds(ks, SB_BK), :]
            vb = vb_ref[pl.ds(ks, SB_BK), :]
            past = (ks + scol) < trow if masked else None
            new = []
            for h in range(2):
                l, m = _sb_logits(qh[h], kb, past)
                cs = _cumsum2(m, tri_rev)
                a = jnp.exp(l + cs[:, :SB_BK] + carry[h])
                if masked:
                    a = jnp.where(past, a, 0.0)
                da = lax.dot_general(doh[h], vb, NT, preferred_element_type=F32)
                g_scr[h, j] = da * a
                b_scr[h, j] = jnp.exp(l)
                a_scr[h, j] = a.astype(BF16)
                new.append(carry[h] + cs[:, SB_BK:])
            return tuple(new)

        zero = jnp.zeros((bq, LANES), F32)
        carry = (zero, zero)
        for d in range(nd - 1, -1, -1):
            carry = sweep1(i * nd + d, carry, True)
        lax.fori_loop(0, i * nd, lambda jj, cr: sweep1(i * nd - 1 - jj, cr, False), carry)

        def sweep2(j, carry, masked):
            ks = pl.multiple_of(j * SB_BK, SB_BK)
            kb = kb_ref[pl.ds(ks, SB_BK), :]
            past = (ks + scol) < trow if masked else None
            new = []
            dk_j = jnp.zeros((SB_BK, LANES), F32)
            dv_j = jnp.zeros((SB_BK, LANES), F32)
            for h in range(2):
                pc, dq = carry[h]
                g = g_scr[h, j]
                beta = b_scr[h, j]
                gs = _cumsum2(g, tri_fwd)
                dz = g - beta * (g + pc + gs[:, :SB_BK])
                if masked:
                    dz = jnp.where(past, dz, 0.0)
                dzb = dz.astype(BF16)
                dq = dq + lax.dot_general(dzb, kb, NN, preferred_element_type=F32)
                dk_j = dk_j + lax.dot_general(dzb, qh[h], TN, preferred_element_type=F32)
                dv_j = dv_j + lax.dot_general(a_scr[h, j], doh[h], TN, preferred_element_type=F32)
                new.append((pc + gs[:, SB_BK:], dq))
            dk_acc[pl.ds(ks, SB_BK), :] += dk_j
            dv_acc[pl.ds(ks, SB_BK), :] += dv_j
            return tuple(new)

        carry = ((zero, zero), (zero, zero))
        carry = lax.fori_loop(0, i * nd, lambda j, cr: sweep2(j, cr, False), carry)
        for d in range(nd):
            carry = sweep2(i * nd + d, carry, True)
        dq_ref[...] = (jnp.where(lane < SB_HEAD_DIM, carry[0][1], carry[1][1]) * scale).astype(BF16)

        @pl.when(i == pl.num_programs(1) - 1)
        def _():
            dk_ref[...] = dk_acc[...].astype(BF16)
            dv_ref[...] = dv_acc[...].astype(BF16)

    full = pl.BlockSpec((S, LANES), lambda p, i: (0, p))
    blk = pl.BlockSpec((bq, LANES), lambda p, i: (i, p))
    act = jax.ShapeDtypeStruct((S, D), BF16)
    return pl.pallas_call(
        body, grid=(npair, S // bq), name=name, in_specs=[qspec, kspec, vspec, blk],
        out_specs=[blk, full, full], out_shape=[act, act, act],
        scratch_shapes=[pltpu.VMEM((S, LANES), BF16), pltpu.VMEM((S, LANES), BF16),
                        pltpu.VMEM((S, LANES), F32), pltpu.VMEM((S, LANES), F32),
                        pltpu.VMEM((2, nkb, bq, SB_BK), F32), pltpu.VMEM((2, nkb, bq, SB_BK), F32),
                        pltpu.VMEM((2, nkb, bq, SB_BK), BF16)],
        compiler_params=_params("parallel", "arbitrary"),
    )(proj, proj, proj, do)


def _rms_rows(v):
    r = lax.rsqrt(jnp.mean(v * v, axis=-1, keepdims=True) + EPS)
    return v * r, r


def _xa_specs(S, D, M, tq):
    nh = D // X_HEAD_DIM
    qspec = pl.BlockSpec((tq, X_HEAD_DIM), lambda h, i: (i, 6 * nh + h))
    kspec = pl.BlockSpec((M, X_HEAD_DIM), lambda h, i: (0, h))
    vspec = pl.BlockSpec((M, X_HEAD_DIM), lambda h, i: (0, nh + h))
    gspec = pl.BlockSpec((1, X_HEAD_DIM), lambda h, i: (0, 0))
    return nh, qspec, kspec, vspec, gspec


def xa_fwd(proj, kv, gq, gk, D, name):
    S, M = proj.shape[0], kv.shape[0]
    tq = _tile(S, 512)
    nh, qspec, kspec, vspec, gspec = _xa_specs(S, D, M, tq)
    scale = X_HEAD_DIM ** -0.5

    def body(q_ref, k_ref, v_ref, gq_ref, gk_ref, o_ref):
        qn = _rms_rows(q_ref[...])[0] * gq_ref[...]
        kn = _rms_rows(k_ref[...])[0] * gk_ref[...]
        s = _dot(qn, kn, NT) * scale
        e = jnp.exp(s - jnp.max(s, axis=-1, keepdims=True))
        p = e / jnp.sum(e, axis=-1, keepdims=True)
        o_ref[...] = _dot(p, v_ref[...], NN)

    return pl.pallas_call(
        body, grid=(nh, S // tq), name=name, in_specs=[qspec, kspec, vspec, gspec, gspec],
        out_specs=pl.BlockSpec((tq, X_HEAD_DIM), lambda h, i: (i, h)),
        out_shape=jax.ShapeDtypeStruct((S, D), F32), compiler_params=_params("parallel", "parallel"),
    )(proj, kv, kv, gq, gk)


def xa_bwd(proj, kv, gq, gk, do, D, name):
    S, M = proj.shape[0], kv.shape[0]
    tq = _tile(S, 512)
    nh, qspec, kspec, vspec, gspec = _xa_specs(S, D, M, tq)
    scale = X_HEAD_DIM ** -0.5

    def body(q_ref, k_ref, v_ref, gq_ref, gk_ref, do_ref, dq_ref, dk_ref, dv_ref, ggq_ref, ggk_ref,
             dkn_acc, dv_acc):
        h, i = pl.program_id(0), pl.program_id(1)

        @pl.when((h == 0) & (i == 0))
        def _():
            ggq_ref[...] = jnp.zeros_like(ggq_ref)
            ggk_ref[...] = jnp.zeros_like(ggk_ref)

        @pl.when(i == 0)
        def _():
            dkn_acc[...] = jnp.zeros_like(dkn_acc)
            dv_acc[...] = jnp.zeros_like(dv_acc)

        gq, gk = gq_ref[...], gk_ref[...]
        qhat, rq = _rms_rows(q_ref[...])
        khat, rk = _rms_rows(k_ref[...])
        qn, kn = qhat * gq, khat * gk
        s = _dot(qn, kn, NT) * scale
        e = jnp.exp(s - jnp.max(s, axis=-1, keepdims=True))
        p = e / jnp.sum(e, axis=-1, keepdims=True)
        dov = do_ref[...]
        dv_acc[...] += _dot(p, dov, TN)
        dp = _dot(dov, v_ref[...], NT)
        ds = (p * (dp - jnp.sum(dp * p, axis=-1, keepdims=True))) * scale
        dqn = _dot(ds, kn, NN)
        dkn_acc[...] += _dot(ds, qn, TN)
        ggq_ref[...] += jnp.sum(dqn * qhat, axis=0, keepdims=True)
        dqh = dqn * gq
        dq_ref[...] = (rq * (dqh - qhat * jnp.mean(dqh * qhat, axis=-1, keepdims=True))).astype(BF16)

        @pl.when(i == pl.num_programs(1) - 1)
        def _():
            dkn = dkn_acc[...]
            ggk_ref[...] += jnp.sum(dkn * khat, axis=0, keepdims=True)
            dkh = dkn * gk
            dk_ref[...] = (rk * (dkh - khat * jnp.mean(dkh * khat, axis=-1, keepdims=True))).astype(BF16)
            dv_ref[...] = dv_acc[...].astype(BF16)

    blk = pl.BlockSpec((tq, X_HEAD_DIM), lambda h, i: (i, h))
    kv_shape = jax.ShapeDtypeStruct((M, 2 * D), BF16)
    gshape = jax.ShapeDtypeStruct((1, X_HEAD_DIM), F32)
    dq, dk, dv, ggq, ggk = pl.pallas_call(
        body, grid=(nh, S // tq), name=name, in_specs=[qspec, kspec, vspec, gspec, gspec, blk],
        out_specs=[blk, kspec, vspec, gspec, gspec],
        out_shape=[jax.ShapeDtypeStruct((S, D), BF16), kv_shape, kv_shape, gshape, gshape],
        scratch_shapes=[pltpu.VMEM((M, X_HEAD_DIM), F32), pltpu.VMEM((M, X_HEAD_DIM), F32)],
        compiler_params=_params("arbitrary", "arbitrary"),
    )(proj, kv, kv, gq, gk, do)
    d_kv = jnp.concatenate([dk[:, :D], dv[:, D:]], axis=1)
    return dq, d_kv, ggq, ggk


def _gate_specs(S, D, tm):
    row = pl.BlockSpec((tm, D), lambda i: (i, 0))
    gate = lambda b: pl.BlockSpec((tm, D), lambda i, b=b: (i, 7 + b))
    return row, gate


def merge_fwd(proj, ys, D, name):
    S = proj.shape[0]
    tm = _tile(S, 256)
    row, gate = _gate_specs(S, D, tm)

    def body(g0, g1, g2, y0, y1, y2, o_ref):
        acc = jax.nn.sigmoid(g0[...]) * y0[...]
        acc = acc + jax.nn.sigmoid(g1[...]) * y1[...]
        acc = acc + jax.nn.sigmoid(g2[...]) * y2[...]
        o_ref[...] = acc.astype(BF16)

    return pl.pallas_call(
        body, grid=(S // tm,), name=name, in_specs=[gate(0), gate(1), gate(2), row, row, row],
        out_specs=row, out_shape=jax.ShapeDtypeStruct((S, D), BF16), compiler_params=_params("parallel"),
    )(proj, proj, proj, *ys)


def merge_bwd(proj, ys, dm, D, name):
    S = proj.shape[0]
    tm = _tile(S, 256)
    row, gate = _gate_specs(S, D, tm)

    def body(g0, g1, g2, y0, y1, y2, dm_ref, d0, d1, d2, dg_ref):
        dmv = dm_ref[...]
        for b, (g_ref, y_ref, d_ref) in enumerate(((g0, y0, d0), (g1, y1, d1), (g2, y2, d2))):
            s = jax.nn.sigmoid(g_ref[...])
            d_ref[...] = (dmv * s).astype(BF16)
            dg_ref[:, b * D:(b + 1) * D] = ((dmv * y_ref[...]) * (s * (1.0 - s))).astype(BF16)

    act = jax.ShapeDtypeStruct((S, D), BF16)
    return pl.pallas_call(
        body, grid=(S // tm,), name=name, in_specs=[gate(0), gate(1), gate(2), row, row, row, row],
        out_specs=[row, row, row, pl.BlockSpec((tm, 3 * D), lambda i: (i, 0))],
        out_shape=[act, act, act, jax.ShapeDtypeStruct((S, 3 * D), BF16)], compiler_params=_params("parallel"),
    )(proj, proj, proj, *ys, dm)


def loss_head(y, target, name):
    S, D = y.shape
    tm = _tile(S, 512)

    def body(y_ref, t_ref, dy_ref, l_ref):
        @pl.when(pl.program_id(0) == 0)
        def _():
            l_ref[...] = jnp.zeros_like(l_ref)

        e = y_ref[...] - t_ref[...]
        dy_ref[...] = e * (1.0 / D)
        l_ref[...] += jnp.sum(e * e, axis=0, keepdims=True)

    row = pl.BlockSpec((tm, D), lambda i: (i, 0))
    vec = pl.BlockSpec((1, D), lambda i: (0, 0))
    return pl.pallas_call(
        body, grid=(S // tm,), name=name, in_specs=[row, row], out_specs=[row, vec],
        out_shape=[jax.ShapeDtypeStruct((S, D), F32), jax.ShapeDtypeStruct((1, D), F32)],
        compiler_params=_params("arbitrary"),
    )(y, target)


def _rows2d(a):
    return a.reshape(-1, a.shape[-1])


def _ew_call(fn, ins, out_dtypes, name):
    R, C = ins[0].shape
    tr = _tile(R, max(8, (1 << 19) // C))
    spec = pl.BlockSpec((tr, C), lambda i: (i, 0))

    def body(*refs):
        outs = fn(*[r[...] for r in refs[:len(ins)]])
        for o_ref, o in zip(refs[len(ins):], outs):
            o_ref[...] = o.astype(o_ref.dtype)

    return pl.pallas_call(
        body, grid=(R // tr,), name=name, in_specs=[spec] * len(ins), out_specs=[spec] * len(out_dtypes),
        out_shape=[jax.ShapeDtypeStruct((R, C), d) for d in out_dtypes], compiler_params=_params("parallel"),
    )(*ins)


def adamw(w, g, m, v, name):
    def fn(w, g, m, v):
        m = ADAM_B1 * m + (1.0 - ADAM_B1) * g
        v = ADAM_B2 * v + (1.0 - ADAM_B2) * (g * g)
        m_hat = m / (1.0 - ADAM_B1 ** ADAM_STEP)
        v_hat = v / (1.0 - ADAM_B2 ** ADAM_STEP)
        return -ADAM_LR * (m_hat / (jnp.sqrt(v_hat) + ADAM_EPS) + ADAM_WD * w), m, v

    shp = w.shape
    outs = _ew_call(fn, [_rows2d(a) for a in (w, g, m, v)], [F32, F32, F32], name)
    return [o.reshape(shp) for o in outs]


ANY = pl.BlockSpec(memory_space=pl.ANY)
CHIP_FLIPS = ((1, 0), (0, 1), (1, 1))


def _place():
    return lax.axis_index("x"), lax.axis_index("y"), lax.axis_index("c")


def _flip(v, f):
    return 1 - v if f else v


def gather_weights(shards, small, name):
    n = len(shards)

    def body(*refs):
        ins, small_in = refs[:n], refs[n]
        outs, small_out = refs[n + 1:2 * n + 1], refs[2 * n + 1]
        ici_send, ici_recv, d2d_send, d2d_recv, loc_sem, sm_send, sm_recv = refs[2 * n + 2:]
        x, y, c = _place()
        mine = 2 * x + y
        sibling = (x, y, 1 - c)
        chips = [(_flip(x, fx), _flip(y, fy)) for fx, fy in CHIP_FLIPS]

        local = []
        for a in range(n):
            cp = pltpu.make_async_copy(ins[a], outs[a].at[mine], loc_sem.at[a])
            cp.start()
            local.append(cp)
        cp = pltpu.make_async_copy(small_in, small_out.at[mine], loc_sem.at[n])
        cp.start()
        local.append(cp)

        def half(a, chip_idx, core):
            hr = ins[a].shape[0] // 2
            return outs[a].at[chip_idx, pl.ds(core * hr, hr), :]

        sends = []
        for j, (px, py) in enumerate(chips):
            cp = pltpu.make_async_remote_copy(
                src_ref=small_in, dst_ref=small_out.at[mine], send_sem=sm_send.at[j], recv_sem=sm_recv.at[j],
                device_id=(px, py, c), device_id_type=MESH)
            cp.start()
            sends.append(cp)
        for a in range(n):
            hr = ins[a].shape[0] // 2
            for j, (px, py) in enumerate(chips):
                cp = pltpu.make_async_remote_copy(
                    src_ref=ins[a].at[pl.ds(c * hr, hr), :], dst_ref=half(a, mine, c),
                    send_sem=ici_send.at[a, j], recv_sem=ici_recv.at[a, j],
                    device_id=(px, py, c), device_id_type=MESH)
                cp.start()
                sends.append(cp)
        for a in range(n):
            for j, (px, py) in enumerate(chips):
                src = half(a, 2 * px + py, c)
                pltpu.make_async_remote_copy(
                    src_ref=src, dst_ref=src, send_sem=ici_send.at[a, j], recv_sem=ici_recv.at[a, j],
                    device_id=(px, py, c), device_id_type=MESH).wait_recv()
                cp = pltpu.make_async_remote_copy(
                    src_ref=src, dst_ref=src, send_sem=d2d_send.at[a, j], recv_sem=d2d_recv.at[a, j],
                    device_id=sibling, device_id_type=MESH)
                cp.start()
                sends.append(cp)
        for a in range(n):
            for j, (px, py) in enumerate(chips):
                dst = half(a, 2 * px + py, 1 - c)
                pltpu.make_async_remote_copy(
                    src_ref=dst, dst_ref=dst, send_sem=d2d_send.at[a, j], recv_sem=d2d_recv.at[a, j],
                    device_id=sibling, device_id_type=MESH).wait_recv()
        for j, (px, py) in enumerate(chips):
            dst = small_out.at[2 * px + py]
            pltpu.make_async_remote_copy(
                src_ref=dst, dst_ref=dst, send_sem=sm_send.at[j], recv_sem=sm_recv.at[j],
                device_id=(px, py, c), device_id_type=MESH).wait_recv()
        for cp in sends:
            cp.wait_send()
        for cp in local:
            cp.wait()

    out_shape = [jax.ShapeDtypeStruct((N_CHIP,) + s.shape, s.dtype) for s in shards]
    out_shape.append(jax.ShapeDtypeStruct((N_CHIP,) + small.shape, small.dtype))
    dma = pltpu.SemaphoreType.DMA
    return pl.pallas_call(
        body, name=name, in_specs=[ANY] * (n + 1), out_specs=[ANY] * (n + 1), out_shape=out_shape,
        scratch_shapes=[dma((n, 3)), dma((n, 3)), dma((n, 3)), dma((n, 3)), dma((n + 1,)), dma((3,)), dma((3,))],
    )(*shards, small)


def exchange_halves(grads, name):
    n = len(grads)

    def body(*refs):
        ins = refs[:n]
        own, got = refs[n:2 * n], refs[2 * n:3 * n]
        send_sem, recv_sem, loc_sem = refs[3 * n:]
        x, y, c = _place()
        sibling = (x, y, 1 - c)
        copies = []
        for a in range(n):
            hr = ins[a].shape[1] // 2
            cp = pltpu.make_async_remote_copy(
                src_ref=ins[a].at[:, pl.ds((1 - c) * hr, hr), :], dst_ref=got[a],
                send_sem=send_sem.at[a], recv_sem=recv_sem.at[a], device_id=sibling, device_id_type=MESH)
            cp.start()
            copies.append(cp)
        local = []
        for a in range(n):
            hr = ins[a].shape[1] // 2
            cp = pltpu.make_async_copy(ins[a].at[:, pl.ds(c * hr, hr), :], own[a], loc_sem.at[a])
            cp.start()
            local.append(cp)
        for cp in copies:
            cp.wait()
        for cp in local:
            cp.wait()

    halves = [jax.ShapeDtypeStruct((N_CHIP, g.shape[1] // 2, g.shape[2]), g.dtype) for g in grads]
    dma = pltpu.SemaphoreType.DMA
    out = pl.pallas_call(
        body, name=name, in_specs=[ANY] * n, out_specs=[ANY] * (2 * n), out_shape=halves + halves,
        scratch_shapes=[dma((n,)), dma((n,)), dma((n,))],
    )(*grads)
    return out[:n], out[n:]


def scatter_partials(parts16, parts32, name):
    n = len(parts16)

    def body(*refs):
        p16, p32 = refs[:n], refs[n:2 * n]
        own, got = refs[2 * n:3 * n], refs[3 * n:4 * n]
        send_sem, recv_sem, loc_sem = refs[4 * n:]
        x, y, c = _place()
        mine = 2 * x + y
        chips = [(_flip(x, fx), _flip(y, fy)) for fx, fy in CHIP_FLIPS]
        copies = []
        for a in range(n):
            for j, (px, py) in enumerate(chips):
                cp = pltpu.make_async_remote_copy(
                    src_ref=p16[a].at[2 * px + py], dst_ref=got[a].at[j],
                    send_sem=send_sem.at[a, j], recv_sem=recv_sem.at[a, j],
                    device_id=(px, py, c), device_id_type=MESH)
                cp.start()
                copies.append(cp)
        local = []
        for a in range(n):
            cp = pltpu.make_async_copy(p32[a].at[mine], own[a], loc_sem.at[a])
            cp.start()
            local.append(cp)
        for cp in copies:
            cp.wait()
        for cp in local:
            cp.wait()

    dma = pltpu.SemaphoreType.DMA
    out = pl.pallas_call(
        body, name=name, in_specs=[ANY] * (2 * n), out_specs=[ANY] * (2 * n),
        out_shape=[jax.ShapeDtypeStruct(p.shape[1:], F32) for p in parts32]
        + [jax.ShapeDtypeStruct((3,) + p.shape[1:], BF16) for p in parts16],
        scratch_shapes=[dma((n, 3)), dma((n, 3)), dma((n,))],
    )(*parts16, *parts32)
    return out[:n], out[n:]


def join_halves(halves, name):
    n = len(halves)

    def body(*refs):
        ins, outs = refs[:n], refs[n:2 * n]
        send_sem, recv_sem, loc_sem = refs[2 * n:]
        x, y, c = _place()
        sibling = (x, y, 1 - c)
        copies, local = [], []
        for a in range(n):
            hr = ins[a].shape[0]
            dst = outs[a].at[pl.ds(c * hr, hr), :]
            cp = pltpu.make_async_remote_copy(
                src_ref=ins[a], dst_ref=dst, send_sem=send_sem.at[a], recv_sem=recv_sem.at[a],
                device_id=sibling, device_id_type=MESH)
            cp.start()
            copies.append(cp)
            cp = pltpu.make_async_copy(ins[a], dst, loc_sem.at[a])
            cp.start()
            local.append(cp)
        for cp in copies:
            cp.wait()
        for cp in local:
            cp.wait()

    dma = pltpu.SemaphoreType.DMA
    return pl.pallas_call(
        body, name=name, in_specs=[ANY] * n, out_specs=[ANY] * n,
        out_shape=[jax.ShapeDtypeStruct((2 * h.shape[0], h.shape[1]), F32) for h in halves],
        scratch_shapes=[dma((n,)), dma((n,)), dma((n,))],
    )(*halves)


def allreduce_small(block, name):
    R, C = block.shape

    def body(in_ref, out_ref, slots, send_sem, recv_sem):
        x, y, c = _place()
        me = 4 * x + 2 * y + c
        slots[me] = in_ref[...]
        copies = []
        for r in range(1, 8):
            fx, fy, fc = (r >> 2) & 1, (r >> 1) & 1, r & 1
            cp = pltpu.make_async_remote_copy(
                src_ref=in_ref, dst_ref=slots.at[me], send_sem=send_sem.at[r - 1], recv_sem=recv_sem.at[r - 1],
                device_id=(_flip(x, fx), _flip(y, fy), _flip(c, fc)), device_id_type=MESH)
            cp.start()
            copies.append(cp)
        for cp in copies:
            cp.wait()
        acc = slots[0]
        for d in range(1, 8):
            acc = acc + slots[d]
        out_ref[...] = acc

    vm = pl.BlockSpec(memory_space=pltpu.VMEM)
    dma = pltpu.SemaphoreType.DMA
    return pl.pallas_call(
        body, name=name, in_specs=[vm], out_specs=vm, out_shape=jax.ShapeDtypeStruct((R, C), F32),
        scratch_shapes=[pltpu.VMEM((8, R, C), F32), dma((7,)), dma((7,))],
    )(block)


def local_step(x, mem, target, g_mix, g_mem, q_norm_g, k_norm_g, g_mlp, conv_w,
               w_in, w_conv_out, w_sb_out, w_mem_kv, w_x_out, w_out, w_up, w_down):
    S, D = x.shape
    h = rms_fwd(x, g_mix, "rms_mix")
    proj = mm_nn_shard(h, w_in, "proj")
    a_conv = conv_fwd(proj, conv_w, D, "conv_fwd")
    o_sb = sb_fwd(proj, D, "sb_fwd")
    mem_n = rms_fwd(mem, g_mem, "rms_mem")
    kv = mm_nn_shard(mem_n, w_mem_kv, "kv")
    o_x = xa_fwd(proj, kv, q_norm_g, k_norm_g, D, "xa_fwd")
    ys = [mm_nn(a_conv, w_conv_out, "y_conv"), mm_nn(o_sb, w_sb_out, "y_sb"), mm_nn(o_x, w_x_out, "y_x")]
    merged = merge_fwd(proj, ys, D, "merge_fwd")
    x1 = mm_nn(merged, w_out, "x1", res=x)
    h2 = rms_fwd(x1, g_mlp, "rms_mlp")
    up, act = mm_nn_shard(h2, w_up, "up", relu2=True)
    x2 = mm_nn(act, w_down, "x2", res=x1)
    dy, loss_cols = loss_head(x2, target, "loss_head")
    d_up = mm_nt(dy, w_down, "d_up", up=up, out_dtype=BF16)
    g = {"w_down": mm_tn(act, dy, "g_w_down")}
    g["w_up"] = mm_tn(h2, d_up, "g_w_up", shard_out=True)
    dh2 = mm_nt_shard(d_up, w_up, "dh2")
    dx1, g["g_mlp"] = rms_bwd(x1, g_mlp, dh2, "rms_mlp_bwd", dres=dy)
    g["w_out"] = mm_tn(merged, dx1, "g_w_out")
    dm = mm_nt(dx1, w_out, "d_merged")
    dy_c, dy_s, dy_x, d_gate = merge_bwd(proj, ys, dm, D, "merge_bwd")
    g["w_conv_out"] = mm_tn(a_conv, dy_c, "g_w_conv_out")
    g["w_sb_out"] = mm_tn(o_sb, dy_s, "g_w_sb_out")
    g["w_x_out"] = mm_tn(o_x, dy_x, "g_w_x_out")
    d_ch, d_cb, d_cc, g["conv_w"] = conv_bwd(proj, conv_w, mm_nt(dy_c, w_conv_out, "d_a_conv"), D, "conv_bwd")
    dq, dk, dv = sb_bwd(proj, mm_nt(dy_s, w_sb_out, "d_o_sb"), D, "sb_bwd")
    d_xq, d_kv, g["q_norm_g"], g["k_norm_g"] = xa_bwd(
        proj, kv, q_norm_g, k_norm_g, mm_nt(dy_x, w_x_out, "d_o_x"), D, "xa_bwd")
    g["w_mem_kv"] = mm_tn(mem_n, d_kv, "g_w_mem_kv", shard_out=True)
    g["g_mem"] = rms_bwd(mem, g_mem, mm_nt_shard(d_kv, w_mem_kv, "d_mem_n"), "rms_mem_bwd", want_dx=False)
    d_proj = jnp.concatenate([d_ch, d_cb, d_cc, dq, dk, dv, d_xq, d_gate], axis=1)
    g["w_in"] = mm_tn(h, d_proj, "g_w_in", shard_out=True)
    dh = mm_nt_shard(d_proj, w_in, "dh")
    grad_x, g["g_mix"] = rms_bwd(x, g_mix, dh, "rms_mix_bwd", dres=dx1)
    return loss_cols, grad_x, g


BIG = ("w_in", "w_conv_out", "w_sb_out", "w_mem_kv", "w_x_out", "w_out", "w_up", "w_down")
COL_SHARDED = ("w_in", "w_mem_kv", "w_up")
WEIGHTS = ("g_mix", "g_mem", "w_in", "conv_w", "w_conv_out", "w_sb_out", "q_norm_g", "k_norm_g",
           "w_mem_kv", "w_x_out", "w_out", "g_mlp", "w_up", "w_down")


def _pack_small(D, g_mix, g_mem, g_mlp, q_norm_g, k_norm_g, conv_w, last):
    qk = jnp.concatenate([q_norm_g, k_norm_g, jnp.zeros((1, D - 2 * X_HEAD_DIM), F32)], axis=1)
    cw = jnp.pad(conv_w, ((0, 0), (0, D - conv_w.shape[1])))
    return jnp.concatenate([g_mix, g_mem, g_mlp, qk, cw, last], axis=0)


def kernel(x, mem, g_mix, g_mem, w_in, conv_w, w_conv_out, w_sb_out, q_norm_g, k_norm_g, w_mem_kv, w_x_out, w_out, g_mlp, w_up, w_down, loss_target, m_g_mix, m_g_mem, m_w_in, m_conv_w, m_w_conv_out, m_w_sb_out, m_q_norm_g, m_k_norm_g, m_w_mem_kv, m_w_x_out, m_w_out, m_g_mlp, m_w_up, m_w_down, v_g_mix, v_g_mem, v_w_in, v_conv_w, v_w_conv_out, v_w_sb_out, v_q_norm_g, v_k_norm_g, v_w_mem_kv, v_w_x_out, v_w_out, v_g_mlp, v_w_up, v_w_down):
    S, D = x.shape[1], x.shape[2]
    w = dict(g_mix=g_mix, g_mem=g_mem, w_in=w_in, conv_w=conv_w, w_conv_out=w_conv_out, w_sb_out=w_sb_out,
             q_norm_g=q_norm_g, k_norm_g=k_norm_g, w_mem_kv=w_mem_kv, w_x_out=w_x_out, w_out=w_out,
             g_mlp=g_mlp, w_up=w_up, w_down=w_down)
    m = dict(g_mix=m_g_mix, g_mem=m_g_mem, w_in=m_w_in, conv_w=m_conv_w, w_conv_out=m_w_conv_out,
             w_sb_out=m_w_sb_out, q_norm_g=m_q_norm_g, k_norm_g=m_k_norm_g, w_mem_kv=m_w_mem_kv,
             w_x_out=m_w_x_out, w_out=m_w_out, g_mlp=m_g_mlp, w_up=m_w_up, w_down=m_w_down)
    v = dict(g_mix=v_g_mix, g_mem=v_g_mem, w_in=v_w_in, conv_w=v_conv_w, w_conv_out=v_w_conv_out,
             w_sb_out=v_w_sb_out, q_norm_g=v_q_norm_g, k_norm_g=v_k_norm_g, w_mem_kv=v_w_mem_kv,
             w_x_out=v_w_x_out, w_out=v_w_out, g_mlp=v_g_mlp, w_up=v_w_up, w_down=v_w_down)
    chip = 2 * lax.axis_index("x") + lax.axis_index("y")
    cs = conv_w.shape[2]

    shards16 = [_ew_call(lambda a: (a,), [w[k][0]], [BF16], "cast_" + k)[0] for k in BIG]
    cw_block = jnp.pad(conv_w[0], ((0, 5), (0, 0)))
    *full16, cw_all = gather_weights(shards16, cw_block, "gather_weights")
    full = {}
    for k, a in zip(BIG, full16):
        full[k] = a if k in COL_SHARDED else a.reshape(-1, a.shape[-1])
    conv_full = jnp.concatenate([cw_all[p, :3] for p in range(N_CHIP)], axis=1)

    loss_cols, grad_x, g = local_step(
        x[0], mem[0], loss_target[0], g_mix, g_mem, q_norm_g, k_norm_g, g_mlp, conv_full,
        full["w_in"], full["w_conv_out"], full["w_sb_out"], full["w_mem_kv"], full["w_x_out"], full["w_out"],
        full["w_up"], full["w_down"])

    small = allreduce_small(
        _pack_small(D, g["g_mix"], g["g_mem"], g["g_mlp"], g["q_norm_g"], g["k_norm_g"], g["conv_w"], loss_cols),
        "allreduce_small")
    loss = (0.5 / D) * jnp.sum(small[7])
    gsum = {"g_mix": small[0:1], "g_mem": small[1:2], "g_mlp": small[2:3],
            "q_norm_g": small[3:4, :X_HEAD_DIM], "k_norm_g": small[3:4, X_HEAD_DIM:2 * X_HEAD_DIM],
            "conv_w": lax.dynamic_slice(small[4:7], (0, chip * cs), (3, cs))[None]}

    g4 = [g[k] if k in COL_SHARDED else g[k].reshape(N_CHIP, -1, g[k].shape[-1]) for k in BIG]
    own, got = exchange_halves(g4, "exchange_halves")
    p16, p32 = [], []
    for k, a, b in zip(BIG, own, got):
        s32, s16 = _ew_call(lambda a, b: (a + b, a + b), [_rows2d(a), _rows2d(b)], [F32, BF16], "pair_sum_" + k)
        p32.append(s32.reshape(a.shape))
        p16.append(s16.reshape(a.shape))
    own, got = scatter_partials(p16, p32, "scatter_partials")
    halves = [
        _ew_call(lambda a, b, c, d: (((a + b.astype(F32)) + c.astype(F32)) + d.astype(F32),),
                 [a, b[0], b[1], b[2]], [F32], "chip_sum_" + k)[0]
        for k, a, b in zip(BIG, own, got)]
    for k, a in zip(BIG, join_halves(halves, "join_halves")):
        gsum[k] = a[None]

    small_names = ("g_mix", "g_mem", "g_mlp", "q_norm_g", "k_norm_g", "conv_w")
    zero_row = jnp.zeros((1, D), F32)
    packed = [_pack_small(D, *[t[k] if k != "conv_w" else t[k][0] for k in small_names], zero_row)
              for t in (w, gsum, m, v)]
    sm = adamw(*packed, "adamw_small")
    delta, new_m, new_v = {}, {}, {}
    for t, block in zip((delta, new_m, new_v), sm):
        t["g_mix"], t["g_mem"], t["g_mlp"] = block[0:1], block[1:2], block[2:3]
        t["q_norm_g"], t["k_norm_g"] = block[3:4, :X_HEAD_DIM], block[3:4, X_HEAD_DIM:2 * X_HEAD_DIM]
        t["conv_w"] = block[4:7, :cs][None]
    for k in BIG:
        delta[k], new_m[k], new_v[k] = adamw(w[k], gsum[k], m[k], v[k], "adamw_" + k)

    return (loss, grad_x[None], *[gsum[k] for k in WEIGHTS], *[delta[k] for k in WEIGHTS],
            *[new_m[k] for k in WEIGHTS], *[new_v[k] for k in WEIGHTS])
```

```python
import functools

import jax
import jax.numpy as jnp
from jax import lax
from jax.experimental import pallas as pl
from jax.experimental.pallas import tpu as pltpu

F32 = jnp.float32
BF16 = jnp.bfloat16
EPS = 1e-6
N_CHIP = 4
SB_HEAD_DIM = 64
X_HEAD_DIM = 256
LANES = 128
VMEM_LIMIT = 56 * 1024 * 1024
ADAM_LR, ADAM_B1, ADAM_B2, ADAM_EPS, ADAM_WD, ADAM_STEP = 0.001, 0.9, 0.999, 1e-8, 0.01, 10
MESH = pl.DeviceIdType.MESH


def _params(*sem):
    return pltpu.CompilerParams(dimension_semantics=sem, vmem_limit_bytes=VMEM_LIMIT)


def _tile(n, pref):
    if n <= pref:
        return n
    t = 1 << (pref.bit_length() - 1)
    while n % t:
        t //= 2
    return t


NN = (((1,), (0,)), ((), ()))
NT = (((1,), (1,)), ((), ()))
TN = (((0,), (0,)), ((), ()))


def _dot(a, b, dims):
    return lax.dot_general(a.astype(BF16), b.astype(BF16), dims, preferred_element_type=F32)


def mm_nn_shard(a, g, name, relu2=False):
    M, K = a.shape
    _, _, Ns = g.shape
    tm, tn = _tile(M, 1024), _tile(Ns, 512)
    nb = Ns // tn

    def body(a_ref, b_ref, *o_refs):
        acc = _dot(a_ref[...], b_ref[...], NN)
        o_refs[0][...] = acc
        if relu2:
            r = jnp.maximum(acc, 0.0)
            o_refs[1][...] = (r * r).astype(BF16)

    o_spec = pl.BlockSpec((tm, tn), lambda i, j: (i, j))
    shapes = [jax.ShapeDtypeStruct((M, N_CHIP * Ns), F32)]
    specs = [o_spec]
    if relu2:
        shapes.append(jax.ShapeDtypeStruct((M, N_CHIP * Ns), BF16))
        specs.append(o_spec)
    out = pl.pallas_call(
        body, grid=(M // tm, N_CHIP * nb), name=name,
        in_specs=[pl.BlockSpec((tm, K), lambda i, j: (i, 0)),
                  pl.BlockSpec((None, K, tn), lambda i, j: (j // nb, 0, j % nb))],
        out_specs=specs, out_shape=shapes, compiler_params=_params("parallel", "parallel"),
    )(a, g)
    return out if relu2 else out[0]


def mm_nn(a, w, name, res=None, out_dtype=F32):
    M, K = a.shape
    N = w.shape[1]
    tm, tn = _tile(M, 1024 if K <= 2048 else 512), _tile(N, 512)

    def body(a_ref, b_ref, *refs):
        acc = _dot(a_ref[...], b_ref[...], NN)
        if res is not None:
            acc = refs[0][...] + acc
        refs[-1][...] = acc.astype(out_dtype)

    o_spec = pl.BlockSpec((tm, tn), lambda i, j: (i, j))
    ins = [a, w] + ([res] if res is not None else [])
    return pl.pallas_call(
        body, grid=(M // tm, N // tn), name=name,
        in_specs=[pl.BlockSpec((tm, K), lambda i, j: (i, 0)), pl.BlockSpec((K, tn), lambda i, j: (0, j))]
        + ([o_spec] if res is not None else []),
        out_specs=o_spec, out_shape=jax.ShapeDtypeStruct((M, N), out_dtype),
        compiler_params=_params("parallel", "parallel"),
    )(*ins)


def mm_nt(a, w, name, up=None, out_dtype=F32):
    M, N = a.shape
    R = w.shape[0]
    tm, tr = _tile(M, 1024), _tile(R, 512)

    def body(a_ref, b_ref, *refs):
        acc = _dot(a_ref[...], b_ref[...], NT)
        if up is not None:
            acc = acc * (2.0 * jnp.maximum(refs[0][...], 0.0))
        refs[-1][...] = acc.astype(out_dtype)

    o_spec = pl.BlockSpec((tm, tr), lambda i, j: (i, j))
    ins = [a, w] + ([up] if up is not None else [])
    return pl.pallas_call(
        body, grid=(M // tm, R // tr), name=name,
        in_specs=[pl.BlockSpec((tm, N), lambda i, j: (i, 0)), pl.BlockSpec((tr, N), lambda i, j: (j, 0))]
        + ([o_spec] if up is not None else []),
        out_specs=o_spec, out_shape=jax.ShapeDtypeStruct((M, R), out_dtype),
        compiler_params=_params("parallel", "parallel"),
    )(*ins)


def mm_nt_shard(a, g, name, out_dtype=F32):
    M = a.shape[0]
    _, R, Ns = g.shape
    tm, tr, tk = _tile(M, 1024), _tile(R, 512), _tile(Ns, 1024)
    nb = Ns // tk
    nk = N_CHIP * nb

    def body(a_ref, b_ref, o_ref, acc_ref):
        k = pl.program_id(2)

        @pl.when(k == 0)
        def _():
            acc_ref[...] = jnp.zeros_like(acc_ref)

        acc_ref[...] += _dot(a_ref[...], b_ref[...], NT)

        @pl.when(k == nk - 1)
        def _():
            o_ref[...] = acc_ref[...].astype(out_dtype)

    return pl.pallas_call(
        body, grid=(M // tm, R // tr, nk), name=name,
        in_specs=[pl.BlockSpec((tm, tk), lambda i, j, k: (i, k)),
                  pl.BlockSpec((None, tr, tk), lambda i, j, k: (k // nb, j, k % nb))],
        out_specs=pl.BlockSpec((tm, tr), lambda i, j, k: (i, j)),
        out_shape=jax.ShapeDtypeStruct((M, R), out_dtype),
        scratch_shapes=[pltpu.VMEM((tm, tr), F32)],
        compiler_params=_params("parallel", "parallel", "arbitrary"),
    )(a, g)


def mm_tn(a, b, name, shard_out=False):
    S, M = a.shape
    N = b.shape[1]
    Ns = N // N_CHIP if shard_out else N
    tm, tn = _tile(M, 512), _tile(Ns, 512)
    nb = Ns // tn

    def body(a_ref, b_ref, o_ref):
        o_ref[...] = _dot(a_ref[...], b_ref[...], TN)

    if shard_out:
        o_spec = pl.BlockSpec((None, tm, tn), lambda i, j: (j // nb, i, j % nb))
        o_shape = jax.ShapeDtypeStruct((N_CHIP, M, Ns), F32)
    else:
        o_spec = pl.BlockSpec((tm, tn), lambda i, j: (i, j))
        o_shape = jax.ShapeDtypeStruct((M, N), F32)
    return pl.pallas_call(
        body, grid=(M // tm, N // tn), name=name,
        in_specs=[pl.BlockSpec((S, tm), lambda i, j: (0, i)), pl.BlockSpec((S, tn), lambda i, j: (0, j))],
        out_specs=o_spec, out_shape=o_shape, compiler_params=_params("parallel", "parallel"),
    )(a, b)


def rms_fwd(x, g, name):
    S, D = x.shape
    tm = _tile(S, 512)

    def body(x_ref, g_ref, h_ref):
        xv = x_ref[...]
        r = lax.rsqrt(jnp.mean(xv * xv, axis=-1, keepdims=True) + EPS)
        h_ref[...] = ((xv * r) * g_ref[...]).astype(BF16)

    return pl.pallas_call(
        body, grid=(S // tm,), name=name,
        in_specs=[pl.BlockSpec((tm, D), lambda i: (i, 0)), pl.BlockSpec((1, D), lambda i: (0, 0))],
        out_specs=pl.BlockSpec((tm, D), lambda i: (i, 0)),
        out_shape=jax.ShapeDtypeStruct((S, D), BF16), compiler_params=_params("parallel"),
    )(x, g)


def rms_bwd(x, g, dh, name, dres=None, want_dx=True):
    S, D = x.shape
    tm = _tile(S, 512)

    def body(x_ref, g_ref, dh_ref, *refs):
        i = pl.program_id(0)
        xv = x_ref[...]
        r = lax.rsqrt(jnp.mean(xv * xv, axis=-1, keepdims=True) + EPS)
        xn = xv * r
        dhv = dh_ref[...].astype(F32)
        gg_ref = refs[-1]

        @pl.when(i == 0)
        def _():
            gg_ref[...] = jnp.zeros_like(gg_ref)

        gg_ref[...] += jnp.sum(dhv * xn, axis=0, keepdims=True)
        if want_dx:
            dxn = dhv * g_ref[...]
            dx = r * (dxn - xn * jnp.mean(dxn * xn, axis=-1, keepdims=True))
            if dres is not None:
                dx = refs[0][...] + dx
            refs[-2][...] = dx

    row = pl.BlockSpec((tm, D), lambda i: (i, 0))
    vec = pl.BlockSpec((1, D), lambda i: (0, 0))
    ins, in_specs = [x, g, dh], [row, vec, row]
    if dres is not None:
        ins.append(dres)
        in_specs.append(row)
    shapes, specs = [jax.ShapeDtypeStruct((1, D), F32)], [vec]
    if want_dx:
        shapes.insert(0, jax.ShapeDtypeStruct((S, D), F32))
        specs.insert(0, row)
    out = pl.pallas_call(body, grid=(S // tm,), name=name, in_specs=in_specs, out_specs=specs,
                         out_shape=shapes, compiler_params=_params("arbitrary"))(*ins)
    return out if want_dx else out[0]


def _shift_down(u, k, row):
    return jnp.where(row >= k, pltpu.roll(u, k, axis=0), 0.0)


def _shift_up(u, k, row):
    S = u.shape[0]
    return jnp.where(row < S - k, pltpu.roll(u, S - k, axis=0), 0.0)


def _conv_specs(S, D, tc):
    nb = D // tc
    col = lambda o: pl.BlockSpec((S, tc), lambda j, o=o: (0, o * nb + j))
    return col, pl.BlockSpec((3, tc), lambda j: (0, j))


def conv_fwd(proj, conv_w, D, name):
    S = proj.shape[0]
    tc = _tile(D, 256)
    col, wspec = _conv_specs(S, D, tc)

    def body(ch_ref, cb_ref, cc_ref, w_ref, a_ref):
        row = lax.broadcasted_iota(jnp.int32, (S, tc), 0)
        u = cc_ref[...] * ch_ref[...]
        w = w_ref[...]
        cv = w[0:1, :] * _shift_down(u, 2, row) + w[1:2, :] * _shift_down(u, 1, row) + w[2:3, :] * u
        a_ref[...] = (cb_ref[...] * cv).astype(BF16)

    return pl.pallas_call(
        body, grid=(D // tc,), name=name, in_specs=[col(0), col(1), col(2), wspec],
        out_specs=pl.BlockSpec((S, tc), lambda j: (0, j)),
        out_shape=jax.ShapeDtypeStruct((S, D), BF16), compiler_params=_params("parallel"),
    )(proj, proj, proj, conv_w)


def conv_bwd(proj, conv_w, da, D, name):
    S = proj.shape[0]
    tc = _tile(D, 256)
    col, wspec = _conv_specs(S, D, tc)
    blk = pl.BlockSpec((S, tc), lambda j: (0, j))

    def body(ch_ref, cb_ref, cc_ref, w_ref, da_ref, dch_ref, dcb_ref, dcc_ref, gw_ref):
        row = lax.broadcasted_iota(jnp.int32, (S, tc), 0)
        ch, cb, cc, dav = ch_ref[...], cb_ref[...], cc_ref[...], da_ref[...]
        w = w_ref[...]
        u = cc * ch
        u1, u2 = _shift_down(u, 1, row), _shift_down(u, 2, row)
        cv = w[0:1, :] * u2 + w[1:2, :] * u1 + w[2:3, :] * u
        dcb_ref[...] = (dav * cv).astype(BF16)
        dcv = dav * cb
        gw_ref[0:1, :] = jnp.sum(dcv * u2, axis=0, keepdims=True)
        gw_ref[1:2, :] = jnp.sum(dcv * u1, axis=0, keepdims=True)
        gw_ref[2:3, :] = jnp.sum(dcv * u, axis=0, keepdims=True)
        du = w[2:3, :] * dcv + w[1:2, :] * _shift_up(dcv, 1, row) + w[0:1, :] * _shift_up(dcv, 2, row)
        dcc_ref[...] = (du * ch).astype(BF16)
        dch_ref[...] = (du * cc).astype(BF16)

    act = jax.ShapeDtypeStruct((S, D), BF16)
    return pl.pallas_call(
        body, grid=(D // tc,), name=name, in_specs=[col(0), col(1), col(2), wspec, blk],
        out_specs=[blk, blk, blk, wspec], out_shape=[act, act, act, jax.ShapeDtypeStruct((3, D), F32)],
        compiler_params=_params("parallel"),
    )(proj, proj, proj, conv_w, da)


SB_BQ = 128
SB_BK = 128
SB_GROUP = 4


def _sb_consts(bq):
    lane = lax.broadcasted_iota(jnp.int32, (bq, LANES), 1)
    r = lax.broadcasted_iota(jnp.int32, (SB_BK, 2 * SB_BK), 0)
    c = lax.broadcasted_iota(jnp.int32, (SB_BK, 2 * SB_BK), 1)
    tri_rev = jnp.where((c >= SB_BK) | (r > c), 1.0, 0.0).astype(BF16)
    tri_fwd = jnp.where((c >= SB_BK) | (r < c), 1.0, 0.0).astype(BF16)
    return lane, tri_rev, tri_fwd


def _cumsum2(v, tri):
    hi = v.astype(BF16)
    lo = (v - hi.astype(F32)).astype(BF16)
    return (lax.dot_general(hi, tri, NN, preferred_element_type=F32)
            + lax.dot_general(lo, tri, NN, preferred_element_type=F32))


def _sb_logits(qh, kb, past):
    z = lax.dot_general(qh, kb, NT, preferred_element_type=F32)
    sp = jnp.log(1.0 + jnp.exp(-jnp.abs(z)))
    l = jnp.minimum(z, 0.0) - sp
    m = l - z
    if past is not None:
        m = jnp.where(past, m, 0.0)
    return l, m


def _sb_specs(S, D, bq):
    npair = D // LANES
    qspec = pl.BlockSpec((bq, LANES), lambda p, i: (i, 3 * npair + p))
    kspec = pl.BlockSpec((S, LANES), lambda p, i: (0, 4 * npair + p))
    vspec = pl.BlockSpec((S, LANES), lambda p, i: (0, 5 * npair + p))
    return npair, qspec, kspec, vspec


def sb_fwd(proj, D, name):
    S = proj.shape[0]
    bq = SB_BQ
    nd = bq // SB_BK
    npair, qspec, kspec, vspec = _sb_specs(S, D, bq)
    scale = SB_HEAD_DIM ** -0.5

    def body(q_ref, k_ref, v_ref, o_ref, kb_ref, vb_ref):
        i = pl.program_id(1)

        @pl.when(i == 0)
        def _():
            kb_ref[...] = k_ref[...].astype(BF16)
            vb_ref[...] = v_ref[...].astype(BF16)

        lane, tri_rev, _ = _sb_consts(bq)
        qv = q_ref[...] * scale
        qh = [jnp.where(lane < SB_HEAD_DIM, qv, 0.0).astype(BF16),
              jnp.where(lane >= SB_HEAD_DIM, qv, 0.0).astype(BF16)]
        trow = i * bq + lax.broadcasted_iota(jnp.int32, (bq, SB_BK), 0)
        scol = lax.broadcasted_iota(jnp.int32, (bq, SB_BK), 1)

        def steps(j0, carry, n, masked):
            pre = []
            for t in range(n):
                ks = pl.multiple_of((j0 - t) * SB_BK, SB_BK)
                kb = kb_ref[pl.ds(ks, SB_BK), :]
                past = (ks + scol) < trow if masked else None
                for h in range(2):
                    l, m = _sb_logits(qh[h], kb, past)
                    pre.append((l, _cumsum2(m, tri_rev), past, ks))
            carry = list(carry)
            for t in range(n):
                for h in range(2):
                    l, cs, past, ks = pre[2 * t + h]
                    c, acc = carry[h]
                    a = jnp.exp(l + cs[:, :SB_BK] + c)
                    if masked:
                        a = jnp.where(past, a, 0.0)
                    vb = vb_ref[pl.ds(ks, SB_BK), :]
                    acc = acc + lax.dot_general(a.astype(BF16), vb, NN, preferred_element_type=F32)
                    carry[h] = (c + cs[:, SB_BK:], acc)
            return tuple(carry)

        zero = jnp.zeros((bq, LANES), F32)
        carry = ((zero, zero), (zero, zero))
        for d in range(nd - 1, -1, -1):
            carry = steps(i * nd + d, carry, 1, True)
        older = i * nd
        groups = older // SB_GROUP
        carry = lax.fori_loop(
            0, groups, lambda t, cr: steps(older - 1 - t * SB_GROUP, cr, SB_GROUP, False), carry)
        rest = older - groups * SB_GROUP
        carry = lax.fori_loop(0, rest, lambda t, cr: steps(rest - 1 - t, cr, 1, False), carry)
        o_ref[...] = jnp.where(lane < SB_HEAD_DIM, carry[0][1], carry[1][1])

    return pl.pallas_call(
        body, grid=(npair, S // bq), name=name, in_specs=[qspec, kspec, vspec],
        out_specs=pl.BlockSpec((bq, LANES), lambda p, i: (i, p)),
        out_shape=jax.ShapeDtypeStruct((S, D), F32),
        scratch_shapes=[pltpu.VMEM((S, LANES), BF16), pltpu.VMEM((S, LANES), BF16)],
        compiler_params=_params("parallel", "arbitrary"),
    )(proj, proj, proj)


def sb_bwd(proj, do, D, name):
    S = proj.shape[0]
    bq = SB_BQ
    nd = bq // SB_BK
    nkb = S // SB_BK
    npair, qspec, kspec, vspec = _sb_specs(S, D, bq)
    scale = SB_HEAD_DIM ** -0.5

    def body(q_ref, k_ref, v_ref, do_ref, dq_ref, dk_ref, dv_ref,
             kb_ref, vb_ref, dk_acc, dv_acc, g_scr, b_scr, a_scr):
        i = pl.program_id(1)

        @pl.when(i == 0)
        def _():
            kb_ref[...] = k_ref[...].astype(BF16)
            vb_ref[...] = v_ref[...].astype(BF16)
            dk_acc[...] = jnp.zeros_like(dk_acc)
            dv_acc[...] = jnp.zeros_like(dv_acc)

        lane, tri_rev, tri_fwd = _sb_consts(bq)
        qv = q_ref[...] * scale
        dov = do_ref[...]
        qh = [jnp.where(lane < SB_HEAD_DIM, qv, 0.0).astype(BF16),
              jnp.where(lane >= SB_HEAD_DIM, qv, 0.0).astype(BF16)]
        doh = [jnp.where(lane < SB_HEAD_DIM, dov, 0.0).astype(BF16),
               jnp.where(lane >= SB_HEAD_DIM, dov, 0.0).astype(BF16)]
        trow = i * bq + lax.broadcasted_iota(jnp.int32, (bq, SB_BK), 0)
        scol = lax.broadcasted_iota(jnp.int32, (bq, SB_BK), 1)

        def sweep1(j0, carry, n, masked):
            pre = []
            for t in range(n):
                j = j0 - t
                ks = pl.multiple_of(j * SB_BK, SB_BK)
                kb = kb_ref[pl.ds(ks, SB_BK), :]
                vb = vb_ref[pl.ds(ks, SB_BK), :]
                past = (ks + scol) < trow if masked else None
                for h in range(2):
                    l, m = _sb_logits(qh[h], kb, past)
                    da = lax.dot_general(doh[h], vb, NT, preferred_element_type=F32)
                    b_scr[h, j] = jnp.exp(l)
                    pre.append((l, _cumsum2(m, tri_rev), da, past, j))
            carry = list(carry)
            for t in range(n):
                for h in range(2):
                    l, cs, da, past, j = pre[2 * t + h]
                    a = jnp.exp(l + cs[:, :SB_BK] + carry[h])
                    if masked:
                        a = jnp.where(past, a, 0.0)
                    g_scr[h, j] = da * a
                    a_scr[h, j] = a.astype(BF16)
                    carry[h] = carry[h] + cs[:, SB_BK:]
            return tuple(carry)

        zero = jnp.zeros((bq, LANES), F32)
        older = i * nd
        groups = older // SB_GROUP
        rest = older - groups * SB_GROUP
        carry = (zero, zero)
        for d in range(nd - 1, -1, -1):
            carry = sweep1(i * nd + d, carry, 1, True)
        carry = lax.fori_loop(
            0, groups, lambda t, cr: sweep1(older - 1 - t * SB_GROUP, cr, SB_GROUP, False), carry)
        lax.fori_loop(0, rest, lambda t, cr: sweep1(rest - 1 - t, cr, 1, False), carry)

        def sweep2(j0, carry, n, masked):
            pre = []
            for t in range(n):
                j = j0 + t
                for h in range(2):
                    g = g_scr[h, j]
                    pre.append((g, _cumsum2(g, tri_fwd), j))
            carry = list(carry)
            for t in range(n):
                j = j0 + t
                ks = pl.multiple_of(j * SB_BK, SB_BK)
                kb = kb_ref[pl.ds(ks, SB_BK), :]
                past = (ks + scol) < trow if masked else None
                dk_j = jnp.zeros((SB_BK, LANES), F32)
                dv_j = jnp.zeros((SB_BK, LANES), F32)
                for h in range(2):
                    g, gs, _ = pre[2 * t + h]
                    pc, dq = carry[h]
                    dz = g - b_scr[h, j] * (g + pc + gs[:, :SB_BK])
                    if masked:
                        dz = jnp.where(past, dz, 0.0)
                    dzb = dz.astype(BF16)
                    dq = dq + lax.dot_general(dzb, kb, NN, preferred_element_type=F32)
                    dk_j = dk_j + lax.dot_general(dzb, qh[h], TN, preferred_element_type=F32)
                    dv_j = dv_j + lax.dot_general(a_scr[h, j], doh[h], TN, preferred_element_type=F32)
                    carry[h] = (pc + gs[:, SB_BK:], dq)
                dk_acc[pl.ds(ks, SB_BK), :] += dk_j
                dv_acc[pl.ds(ks, SB_BK), :] += dv_j
            return tuple(carry)

        carry = ((zero, zero), (zero, zero))
        carry = lax.fori_loop(0, groups, lambda t, cr: sweep2(t * SB_GROUP, cr, SB_GROUP, False), carry)
        carry = lax.fori_loop(0, rest, lambda t, cr: sweep2(groups * SB_GROUP + t, cr, 1, False), carry)
        for d in range(nd):
            carry = sweep2(i * nd + d, carry, 1, True)
        dq_ref[...] = (jnp.where(lane < SB_HEAD_DIM, carry[0][1], carry[1][1]) * scale).astype(BF16)

        @pl.when(i == pl.num_programs(1) - 1)
        def _():
            dk_ref[...] = dk_acc[...].astype(BF16)
            dv_ref[...] = dv_acc[...].astype(BF16)

    full = pl.BlockSpec((S, LANES), lambda p, i: (0, p))
    blk = pl.BlockSpec((bq, LANES), lambda p, i: (i, p))
    act = jax.ShapeDtypeStruct((S, D), BF16)
    return pl.pallas_call(
        body, grid=(npair, S // bq), name=name, in_specs=[qspec, kspec, vspec, blk],
        out_specs=[blk, full, full], out_shape=[act, act, act],
        scratch_shapes=[pltpu.VMEM((S, LANES), BF16), pltpu.VMEM((S, LANES), BF16),
                        pltpu.VMEM((S, LANES), F32), pltpu.VMEM((S, LANES), F32),
                        pltpu.VMEM((2, nkb, bq, SB_BK), F32), pltpu.VMEM((2, nkb, bq, SB_BK), F32),
                        pltpu.VMEM((2, nkb, bq, SB_BK), BF16)],
        compiler_params=_params("parallel", "arbitrary"),
    )(proj, proj, proj, do)


def _rms_rows(v):
    r = lax.rsqrt(jnp.mean(v * v, axis=-1, keepdims=True) + EPS)
    return v * r, r


def _xa_specs(S, D, M, tq):
    nh = D // X_HEAD_DIM
    qspec = pl.BlockSpec((tq, X_HEAD_DIM), lambda h, i: (i, 6 * nh + h))
    kspec = pl.BlockSpec((M, X_HEAD_DIM), lambda h, i: (0, h))
    vspec = pl.BlockSpec((M, X_HEAD_DIM), lambda h, i: (0, nh + h))
    gspec = pl.BlockSpec((1, X_HEAD_DIM), lambda h, i: (0, 0))
    return nh, qspec, kspec, vspec, gspec


def xa_fwd(proj, kv, gq, gk, D, name):
    S, M = proj.shape[0], kv.shape[0]
    tq = _tile(S, 512)
    nh, qspec, kspec, vspec, gspec = _xa_specs(S, D, M, tq)
    scale = X_HEAD_DIM ** -0.5

    def body(q_ref, k_ref, v_ref, gq_ref, gk_ref, o_ref):
        qn = _rms_rows(q_ref[...])[0] * gq_ref[...]
        kn = _rms_rows(k_ref[...])[0] * gk_ref[...]
        s = _dot(qn, kn, NT) * scale
        e = jnp.exp(s - jnp.max(s, axis=-1, keepdims=True))
        p = e / jnp.sum(e, axis=-1, keepdims=True)
        o_ref[...] = _dot(p, v_ref[...], NN)

    return pl.pallas_call(
        body, grid=(nh, S // tq), name=name, in_specs=[qspec, kspec, vspec, gspec, gspec],
        out_specs=pl.BlockSpec((tq, X_HEAD_DIM), lambda h, i: (i, h)),
        out_shape=jax.ShapeDtypeStruct((S, D), F32), compiler_params=_params("parallel", "parallel"),
    )(proj, kv, kv, gq, gk)


def xa_bwd(proj, kv, gq, gk, do, D, name):
    S, M = proj.shape[0], kv.shape[0]
    tq = _tile(S, 512)
    nh, qspec, kspec, vspec, gspec = _xa_specs(S, D, M, tq)
    scale = X_HEAD_DIM ** -0.5

    def body(q_ref, k_ref, v_ref, gq_ref, gk_ref, do_ref, dq_ref, dk_ref, dv_ref, ggq_ref, ggk_ref,
             dkn_acc, dv_acc):
        h, i = pl.program_id(0), pl.program_id(1)

        @pl.when((h == 0) & (i == 0))
        def _():
            ggq_ref[...] = jnp.zeros_like(ggq_ref)
            ggk_ref[...] = jnp.zeros_like(ggk_ref)

        @pl.when(i == 0)
        def _():
            dkn_acc[...] = jnp.zeros_like(dkn_acc)
            dv_acc[...] = jnp.zeros_like(dv_acc)

        gq, gk = gq_ref[...], gk_ref[...]
        qhat, rq = _rms_rows(q_ref[...])
        khat, rk = _rms_rows(k_ref[...])
        qn, kn = qhat * gq, khat * gk
        s = _dot(qn, kn, NT) * scale
        e = jnp.exp(s - jnp.max(s, axis=-1, keepdims=True))
        p = e / jnp.sum(e, axis=-1, keepdims=True)
        dov = do_ref[...]
        dv_acc[...] += _dot(p, dov, TN)
        dp = _dot(dov, v_ref[...], NT)
        ds = (p * (dp - jnp.sum(dp * p, axis=-1, keepdims=True))) * scale
        dqn = _dot(ds, kn, NN)
        dkn_acc[...] += _dot(ds, qn, TN)
        ggq_ref[...] += jnp.sum(dqn * qhat, axis=0, keepdims=True)
        dqh = dqn * gq
        dq_ref[...] = (rq * (dqh - qhat * jnp.mean(dqh * qhat, axis=-1, keepdims=True))).astype(BF16)

        @pl.when(i == pl.num_programs(1) - 1)
        def _():
            dkn = dkn_acc[...]
            ggk_ref[...] += jnp.sum(dkn * khat, axis=0, keepdims=True)
            dkh = dkn * gk
            dk_ref[...] = (rk * (dkh - khat * jnp.mean(dkh * khat, axis=-1, keepdims=True))).astype(BF16)
            dv_ref[...] = dv_acc[...].astype(BF16)

    blk = pl.BlockSpec((tq, X_HEAD_DIM), lambda h, i: (i, h))
    kv_shape = jax.ShapeDtypeStruct((M, 2 * D), BF16)
    gshape = jax.ShapeDtypeStruct((1, X_HEAD_DIM), F32)
    dq, dk, dv, ggq, ggk = pl.pallas_call(
        body, grid=(nh, S // tq), name=name, in_specs=[qspec, kspec, vspec, gspec, gspec, blk],
        out_specs=[blk, kspec, vspec, gspec, gspec],
        out_shape=[jax.ShapeDtypeStruct((S, D), BF16), kv_shape, kv_shape, gshape, gshape],
        scratch_shapes=[pltpu.VMEM((M, X_HEAD_DIM), F32), pltpu.VMEM((M, X_HEAD_DIM), F32)],
        compiler_params=_params("arbitrary", "arbitrary"),
    )(proj, kv, kv, gq, gk, do)
    d_kv = jnp.concatenate([dk[:, :D], dv[:, D:]], axis=1)
    return dq, d_kv, ggq, ggk


def _gate_specs(S, D, tm):
    row = pl.BlockSpec((tm, D), lambda i: (i, 0))
    gate = lambda b: pl.BlockSpec((tm, D), lambda i, b=b: (i, 7 + b))
    return row, gate


def merge_fwd(proj, ys, D, name):
    S = proj.shape[0]
    tm = _tile(S, 256)
    row, gate = _gate_specs(S, D, tm)

    def body(g0, g1, g2, y0, y1, y2, o_ref):
        acc = jax.nn.sigmoid(g0[...]) * y0[...]
        acc = acc + jax.nn.sigmoid(g1[...]) * y1[...]
        acc = acc + jax.nn.sigmoid(g2[...]) * y2[...]
        o_ref[...] = acc.astype(BF16)

    return pl.pallas_call(
        body, grid=(S // tm,), name=name, in_specs=[gate(0), gate(1), gate(2), row, row, row],
        out_specs=row, out_shape=jax.ShapeDtypeStruct((S, D), BF16), compiler_params=_params("parallel"),
    )(proj, proj, proj, *ys)


def merge_bwd(proj, ys, dm, D, name):
    S = proj.shape[0]
    tm = _tile(S, 256)
    row, gate = _gate_specs(S, D, tm)

    def body(g0, g1, g2, y0, y1, y2, dm_ref, d0, d1, d2, dg_ref):
        dmv = dm_ref[...]
        for b, (g_ref, y_ref, d_ref) in enumerate(((g0, y0, d0), (g1, y1, d1), (g2, y2, d2))):
            s = jax.nn.sigmoid(g_ref[...])
            d_ref[...] = (dmv * s).astype(BF16)
            dg_ref[:, b * D:(b + 1) * D] = ((dmv * y_ref[...]) * (s * (1.0 - s))).astype(BF16)

    act = jax.ShapeDtypeStruct((S, D), BF16)
    return pl.pallas_call(
        body, grid=(S // tm,), name=name, in_specs=[gate(0), gate(1), gate(2), row, row, row, row],
        out_specs=[row, row, row, pl.BlockSpec((tm, 3 * D), lambda i: (i, 0))],
        out_shape=[act, act, act, jax.ShapeDtypeStruct((S, 3 * D), BF16)], compiler_params=_params("parallel"),
    )(proj, proj, proj, *ys, dm)


def loss_head(y, target, name):
    S, D = y.shape
    tm = _tile(S, 512)

    def body(y_ref, t_ref, dy_ref, l_ref):
        @pl.when(pl.program_id(0) == 0)
        def _():
            l_ref[...] = jnp.zeros_like(l_ref)

        e = y_ref[...] - t_ref[...]
        dy_ref[...] = e * (1.0 / D)
        l_ref[...] += jnp.sum(e * e, axis=0, keepdims=True)

    row = pl.BlockSpec((tm, D), lambda i: (i, 0))
    vec = pl.BlockSpec((1, D), lambda i: (0, 0))
    return pl.pallas_call(
        body, grid=(S // tm,), name=name, in_specs=[row, row], out_specs=[row, vec],
        out_shape=[jax.ShapeDtypeStruct((S, D), F32), jax.ShapeDtypeStruct((1, D), F32)],
        compiler_params=_params("arbitrary"),
    )(y, target)


def _rows2d(a):
    return a.reshape(-1, a.shape[-1])


def _ew_call(fn, ins, out_dtypes, name):
    R, C = ins[0].shape
    tr = _tile(R, max(8, (1 << 19) // C))
    spec = pl.BlockSpec((tr, C), lambda i: (i, 0))

    def body(*refs):
        outs = fn(*[r[...] for r in refs[:len(ins)]])
        for o_ref, o in zip(refs[len(ins):], outs):
            o_ref[...] = o.astype(o_ref.dtype)

    return pl.pallas_call(
        body, grid=(R // tr,), name=name, in_specs=[spec] * len(ins), out_specs=[spec] * len(out_dtypes),
        out_shape=[jax.ShapeDtypeStruct((R, C), d) for d in out_dtypes], compiler_params=_params("parallel"),
    )(*ins)


def adamw(w, g, m, v, name):
    def fn(w, g, m, v):
        m = ADAM_B1 * m + (1.0 - ADAM_B1) * g
        v = ADAM_B2 * v + (1.0 - ADAM_B2) * (g * g)
        m_hat = m / (1.0 - ADAM_B1 ** ADAM_STEP)
        v_hat = v / (1.0 - ADAM_B2 ** ADAM_STEP)
        return -ADAM_LR * (m_hat / (jnp.sqrt(v_hat) + ADAM_EPS) + ADAM_WD * w), m, v

    shp = w.shape
    outs = _ew_call(fn, [_rows2d(a) for a in (w, g, m, v)], [F32, F32, F32], name)
    return [o.reshape(shp) for o in outs]


def _placed_call(fn, place, grid, ins, in_specs, out_shape, out_specs, name):
    n = len(ins)

    def body(place_ref, *refs):
        outs = fn(*[r[...] for r in refs[:n]])
        for o_ref, o in zip(refs[n:], outs):
            o_ref[...] = o.astype(o_ref.dtype)

    return pl.pallas_call(
        body, name=name, out_shape=out_shape,
        grid_spec=pltpu.PrefetchScalarGridSpec(
            num_scalar_prefetch=1, grid=grid, in_specs=in_specs, out_specs=out_specs),
        compiler_params=_params(*["parallel"] * len(grid)),
    )(place, *ins)


def _row_tile(R, C):
    return _tile(R, max(16, (1 << 19) // C))


def cast_into_full(w, place, name):
    R, C = w.shape
    tr = _row_tile(R, C)
    return _placed_call(
        lambda a: (a,), place, (R // tr,), [w], [pl.BlockSpec((tr, C), lambda i, p: (i, 0))],
        [jax.ShapeDtypeStruct((N_CHIP, R, C), BF16)], [pl.BlockSpec((None, tr, C), lambda i, p: (p[0], i, 0))],
        name)[0]


def pair_sum(g4, got, place, name):
    _, hr, C = got.shape
    tr = _row_tile(hr, C)
    nb = hr // tr
    blk = pl.BlockSpec((None, tr, C), lambda s, i, p: (s, i, 0))
    return _placed_call(
        lambda a, b: (a + b, a + b), place, (N_CHIP, nb), [g4, got],
        [pl.BlockSpec((None, tr, C), lambda s, i, p: (s, p[1] * nb + i, 0)), blk],
        [jax.ShapeDtypeStruct(got.shape, F32), jax.ShapeDtypeStruct(got.shape, BF16)], [blk, blk], name)


def chip_sum(p32, got, place, name):
    _, H, C = p32.shape
    tr = _row_tile(H, C)
    nb = H // tr
    peer = lambda j: pl.BlockSpec((None, tr, C), lambda i, p, j=j: (j, i, 0))
    return _placed_call(
        lambda a, b, c, d: (((a + b.astype(F32)) + c.astype(F32)) + d.astype(F32),), place, (nb,),
        [p32, got, got, got], [pl.BlockSpec((None, tr, C), lambda i, p: (p[0], i, 0)), peer(0), peer(1), peer(2)],
        [jax.ShapeDtypeStruct((2 * H, C), F32)], [pl.BlockSpec((tr, C), lambda i, p: (p[1] * nb + i, 0))],
        name)[0]


ANY = pl.BlockSpec(memory_space=pl.ANY)
CHIP_FLIPS = ((1, 0), (0, 1), (1, 1))


def _place():
    return lax.axis_index("x"), lax.axis_index("y"), lax.axis_index("c")


def _flip(v, f):
    return 1 - v if f else v


def gather_weights(fulls, small, name):
    n = len(fulls)

    def body(*refs):
        small_in = refs[n]
        outs, small_out = refs[n + 1:2 * n + 1], refs[2 * n + 1]
        ici_send, ici_recv, d2d_send, d2d_recv, loc_sem, sm_send, sm_recv = refs[2 * n + 2:]
        x, y, c = _place()
        mine = 2 * x + y
        sibling = (x, y, 1 - c)
        chips = [(_flip(x, fx), _flip(y, fy)) for fx, fy in CHIP_FLIPS]

        local = [pltpu.make_async_copy(small_in, small_out.at[mine], loc_sem)]
        local[0].start()

        def half(a, chip_idx, core):
            hr = outs[a].shape[1] // 2
            return outs[a].at[chip_idx, pl.ds(core * hr, hr), :]

        sends = []
        for j, (px, py) in enumerate(chips):
            cp = pltpu.make_async_remote_copy(
                src_ref=small_in, dst_ref=small_out.at[mine], send_sem=sm_send.at[j], recv_sem=sm_recv.at[j],
                device_id=(px, py, c), device_id_type=MESH)
            cp.start()
            sends.append(cp)
        for a in range(n):
            for j, (px, py) in enumerate(chips):
                cp = pltpu.make_async_remote_copy(
                    src_ref=half(a, mine, c), dst_ref=half(a, mine, c),
                    send_sem=ici_send.at[a, j], recv_sem=ici_recv.at[a, j],
                    device_id=(px, py, c), device_id_type=MESH)
                cp.start()
                sends.append(cp)
        for a in range(n):
            for j, (px, py) in enumerate(chips):
                src = half(a, 2 * px + py, c)
                pltpu.make_async_remote_copy(
                    src_ref=src, dst_ref=src, send_sem=ici_send.at[a, j], recv_sem=ici_recv.at[a, j],
                    device_id=(px, py, c), device_id_type=MESH).wait_recv()
                cp = pltpu.make_async_remote_copy(
                    src_ref=src, dst_ref=src, send_sem=d2d_send.at[a, j], recv_sem=d2d_recv.at[a, j],
                    device_id=sibling, device_id_type=MESH)
                cp.start()
                sends.append(cp)
        for a in range(n):
            for j, (px, py) in enumerate(chips):
                dst = half(a, 2 * px + py, 1 - c)
                pltpu.make_async_remote_copy(
                    src_ref=dst, dst_ref=dst, send_sem=d2d_send.at[a, j], recv_sem=d2d_recv.at[a, j],
                    device_id=sibling, device_id_type=MESH).wait_recv()
        for j, (px, py) in enumerate(chips):
            dst = small_out.at[2 * px + py]
            pltpu.make_async_remote_copy(
                src_ref=dst, dst_ref=dst, send_sem=sm_send.at[j], recv_sem=sm_recv.at[j],
                device_id=(px, py, c), device_id_type=MESH).wait_recv()
        for cp in sends:
            cp.wait_send()
        for cp in local:
            cp.wait()

    out_shape = [jax.ShapeDtypeStruct(f.shape, f.dtype) for f in fulls]
    out_shape.append(jax.ShapeDtypeStruct((N_CHIP,) + small.shape, small.dtype))
    dma = pltpu.SemaphoreType.DMA
    return pl.pallas_call(
        body, name=name, in_specs=[ANY] * (n + 1), out_specs=[ANY] * (n + 1), out_shape=out_shape,
        input_output_aliases={a: a for a in range(n)},
        scratch_shapes=[dma((n, 3)), dma((n, 3)), dma((n, 3)), dma((n, 3)), dma, dma((3,)), dma((3,))],
    )(*fulls, small)


def exchange_halves(grads, name):
    n = len(grads)

    def body(*refs):
        ins, got = refs[:n], refs[n:2 * n]
        send_sem, recv_sem = refs[2 * n:]
        x, y, c = _place()
        copies = []
        for a in range(n):
            hr = ins[a].shape[1] // 2
            cp = pltpu.make_async_remote_copy(
                src_ref=ins[a].at[:, pl.ds((1 - c) * hr, hr), :], dst_ref=got[a],
                send_sem=send_sem.at[a], recv_sem=recv_sem.at[a], device_id=(x, y, 1 - c), device_id_type=MESH)
            cp.start()
            copies.append(cp)
        for cp in copies:
            cp.wait()

    dma = pltpu.SemaphoreType.DMA
    return pl.pallas_call(
        body, name=name, in_specs=[ANY] * n, out_specs=[ANY] * n,
        out_shape=[jax.ShapeDtypeStruct((N_CHIP, g.shape[1] // 2, g.shape[2]), g.dtype) for g in grads],
        scratch_shapes=[dma((n,)), dma((n,))],
    )(*grads)


def scatter_partials(parts16, name):
    n = len(parts16)

    def body(*refs):
        p16, got = refs[:n], refs[n:2 * n]
        send_sem, recv_sem = refs[2 * n:]
        x, y, c = _place()
        chips = [(_flip(x, fx), _flip(y, fy)) for fx, fy in CHIP_FLIPS]
        copies = []
        for a in range(n):
            for j, (px, py) in enumerate(chips):
                cp = pltpu.make_async_remote_copy(
                    src_ref=p16[a].at[2 * px + py], dst_ref=got[a].at[j],
                    send_sem=send_sem.at[a, j], recv_sem=recv_sem.at[a, j],
                    device_id=(px, py, c), device_id_type=MESH)
                cp.start()
                copies.append(cp)
        for cp in copies:
            cp.wait()

    dma = pltpu.SemaphoreType.DMA
    return pl.pallas_call(
        body, name=name, in_specs=[ANY] * n, out_specs=[ANY] * n,
        out_shape=[jax.ShapeDtypeStruct((3,) + p.shape[1:], BF16) for p in parts16],
        scratch_shapes=[dma((n, 3)), dma((n, 3))],
    )(*parts16)


def join_halves(fulls, name):
    n = len(fulls)

    def body(*refs):
        outs = refs[n:2 * n]
        send_sem, recv_sem = refs[2 * n:]
        x, y, c = _place()
        copies = []
        for a in range(n):
            hr = outs[a].shape[0] // 2
            half = outs[a].at[pl.ds(c * hr, hr), :]
            cp = pltpu.make_async_remote_copy(
                src_ref=half, dst_ref=half, send_sem=send_sem.at[a], recv_sem=recv_sem.at[a],
                device_id=(x, y, 1 - c), device_id_type=MESH)
            cp.start()
            copies.append(cp)
        for a, cp in enumerate(copies):
            hr = outs[a].shape[0] // 2
            theirs = outs[a].at[pl.ds((1 - c) * hr, hr), :]
            cp.wait_send()
            pltpu.make_async_remote_copy(
                src_ref=theirs, dst_ref=theirs, send_sem=send_sem.at[a], recv_sem=recv_sem.at[a],
                device_id=(x, y, 1 - c), device_id_type=MESH).wait_recv()

    dma = pltpu.SemaphoreType.DMA
    return pl.pallas_call(
        body, name=name, in_specs=[ANY] * n, out_specs=[ANY] * n,
        out_shape=[jax.ShapeDtypeStruct(f.shape, F32) for f in fulls],
        input_output_aliases={a: a for a in range(n)},
        scratch_shapes=[dma((n,)), dma((n,))],
    )(*fulls)


def allreduce_small(block, name):
    R, C = block.shape

    def body(in_ref, out_ref, slots, send_sem, recv_sem):
        x, y, c = _place()
        me = 4 * x + 2 * y + c
        slots[me] = in_ref[...]
        copies = []
        for r in range(1, 8):
            fx, fy, fc = (r >> 2) & 1, (r >> 1) & 1, r & 1
            cp = pltpu.make_async_remote_copy(
                src_ref=in_ref, dst_ref=slots.at[me], send_sem=send_sem.at[r - 1], recv_sem=recv_sem.at[r - 1],
                device_id=(_flip(x, fx), _flip(y, fy), _flip(c, fc)), device_id_type=MESH)
            cp.start()
            copies.append(cp)
        for cp in copies:
            cp.wait()
        acc = slots[0]
        for d in range(1, 8):
            acc = acc + slots[d]
        out_ref[...] = acc

    vm = pl.BlockSpec(memory_space=pltpu.VMEM)
    dma = pltpu.SemaphoreType.DMA
    return pl.pallas_call(
        body, name=name, in_specs=[vm], out_specs=vm, out_shape=jax.ShapeDtypeStruct((R, C), F32),
        scratch_shapes=[pltpu.VMEM((8, R, C), F32), dma((7,)), dma((7,))],
    )(block)


def local_step(x, mem, target, g_mix, g_mem, q_norm_g, k_norm_g, g_mlp, conv_w,
               w_in, w_conv_out, w_sb_out, w_mem_kv, w_x_out, w_out, w_up, w_down):
    S, D = x.shape
    h = rms_fwd(x, g_mix, "rms_mix")
    proj = mm_nn_shard(h, w_in, "proj")
    a_conv = conv_fwd(proj, conv_w, D, "conv_fwd")
    o_sb = sb_fwd(proj, D, "sb_fwd")
    mem_n = rms_fwd(mem, g_mem, "rms_mem")
    kv = mm_nn_shard(mem_n, w_mem_kv, "kv")
    o_x = xa_fwd(proj, kv, q_norm_g, k_norm_g, D, "xa_fwd")
    ys = [mm_nn(a_conv, w_conv_out, "y_conv"), mm_nn(o_sb, w_sb_out, "y_sb"), mm_nn(o_x, w_x_out, "y_x")]
    merged = merge_fwd(proj, ys, D, "merge_fwd")
    x1 = mm_nn(merged, w_out, "x1", res=x)
    h2 = rms_fwd(x1, g_mlp, "rms_mlp")
    up, act = mm_nn_shard(h2, w_up, "up", relu2=True)
    x2 = mm_nn(act, w_down, "x2", res=x1)
    dy, loss_cols = loss_head(x2, target, "loss_head")
    d_up = mm_nt(dy, w_down, "d_up", up=up, out_dtype=BF16)
    g = {"w_down": mm_tn(act, dy, "g_w_down")}
    g["w_up"] = mm_tn(h2, d_up, "g_w_up", shard_out=True)
    dh2 = mm_nt_shard(d_up, w_up, "dh2")
    dx1, g["g_mlp"] = rms_bwd(x1, g_mlp, dh2, "rms_mlp_bwd", dres=dy)
    g["w_out"] = mm_tn(merged, dx1, "g_w_out")
    dm = mm_nt(dx1, w_out, "d_merged")
    dy_c, dy_s, dy_x, d_gate = merge_bwd(proj, ys, dm, D, "merge_bwd")
    g["w_conv_out"] = mm_tn(a_conv, dy_c, "g_w_conv_out")
    g["w_sb_out"] = mm_tn(o_sb, dy_s, "g_w_sb_out")
    g["w_x_out"] = mm_tn(o_x, dy_x, "g_w_x_out")
    d_ch, d_cb, d_cc, g["conv_w"] = conv_bwd(proj, conv_w, mm_nt(dy_c, w_conv_out, "d_a_conv"), D, "conv_bwd")
    dq, dk, dv = sb_bwd(proj, mm_nt(dy_s, w_sb_out, "d_o_sb"), D, "sb_bwd")
    d_xq, d_kv, g["q_norm_g"], g["k_norm_g"] = xa_bwd(
        proj, kv, q_norm_g, k_norm_g, mm_nt(dy_x, w_x_out, "d_o_x"), D, "xa_bwd")
    g["w_mem_kv"] = mm_tn(mem_n, d_kv, "g_w_mem_kv", shard_out=True)
    g["g_mem"] = rms_bwd(mem, g_mem, mm_nt_shard(d_kv, w_mem_kv, "d_mem_n"), "rms_mem_bwd", want_dx=False)
    d_proj = jnp.concatenate([d_ch, d_cb, d_cc, dq, dk, dv, d_xq, d_gate], axis=1)
    g["w_in"] = mm_tn(h, d_proj, "g_w_in", shard_out=True)
    dh = mm_nt_shard(d_proj, w_in, "dh")
    grad_x, g["g_mix"] = rms_bwd(x, g_mix, dh, "rms_mix_bwd", dres=dx1)
    return loss_cols, grad_x, g


BIG = ("w_in", "w_conv_out", "w_sb_out", "w_mem_kv", "w_x_out", "w_out", "w_up", "w_down")
COL_SHARDED = ("w_in", "w_mem_kv", "w_up")
WEIGHTS = ("g_mix", "g_mem", "w_in", "conv_w", "w_conv_out", "w_sb_out", "q_norm_g", "k_norm_g",
           "w_mem_kv", "w_x_out", "w_out", "g_mlp", "w_up", "w_down")


def _pack_small(D, g_mix, g_mem, g_mlp, q_norm_g, k_norm_g, conv_w, last):
    qk = jnp.concatenate([q_norm_g, k_norm_g, jnp.zeros((1, D - 2 * X_HEAD_DIM), F32)], axis=1)
    cw = jnp.pad(conv_w, ((0, 0), (0, D - conv_w.shape[1])))
    return jnp.concatenate([g_mix, g_mem, g_mlp, qk, cw, last], axis=0)


def kernel(x, mem, g_mix, g_mem, w_in, conv_w, w_conv_out, w_sb_out, q_norm_g, k_norm_g, w_mem_kv, w_x_out, w_out, g_mlp, w_up, w_down, loss_target, m_g_mix, m_g_mem, m_w_in, m_conv_w, m_w_conv_out, m_w_sb_out, m_q_norm_g, m_k_norm_g, m_w_mem_kv, m_w_x_out, m_w_out, m_g_mlp, m_w_up, m_w_down, v_g_mix, v_g_mem, v_w_in, v_conv_w, v_w_conv_out, v_w_sb_out, v_q_norm_g, v_k_norm_g, v_w_mem_kv, v_w_x_out, v_w_out, v_g_mlp, v_w_up, v_w_down):
    S, D = x.shape[1], x.shape[2]
    w = dict(g_mix=g_mix, g_mem=g_mem, w_in=w_in, conv_w=conv_w, w_conv_out=w_conv_out, w_sb_out=w_sb_out,
             q_norm_g=q_norm_g, k_norm_g=k_norm_g, w_mem_kv=w_mem_kv, w_x_out=w_x_out, w_out=w_out,
             g_mlp=g_mlp, w_up=w_up, w_down=w_down)
    m = dict(g_mix=m_g_mix, g_mem=m_g_mem, w_in=m_w_in, conv_w=m_conv_w, w_conv_out=m_w_conv_out,
             w_sb_out=m_w_sb_out, q_norm_g=m_q_norm_g, k_norm_g=m_k_norm_g, w_mem_kv=m_w_mem_kv,
             w_x_out=m_w_x_out, w_out=m_w_out, g_mlp=m_g_mlp, w_up=m_w_up, w_down=m_w_down)
    v = dict(g_mix=v_g_mix, g_mem=v_g_mem, w_in=v_w_in, conv_w=v_conv_w, w_conv_out=v_w_conv_out,
             w_sb_out=v_w_sb_out, q_norm_g=v_q_norm_g, k_norm_g=v_k_norm_g, w_mem_kv=v_w_mem_kv,
             w_x_out=v_w_x_out, w_out=v_w_out, g_mlp=v_g_mlp, w_up=v_w_up, w_down=v_w_down)
    chip = 2 * lax.axis_index("x") + lax.axis_index("y")
    cs = conv_w.shape[2]

    place = jnp.stack([chip, lax.axis_index("c")]).astype(jnp.int32)
    mine16 = [cast_into_full(w[k][0], place, "cast_" + k) for k in BIG]
    cw_block = jnp.pad(conv_w[0], ((0, 5), (0, 0)))
    *full16, cw_all = gather_weights(mine16, cw_block, "gather_weights")
    full = {}
    for k, a in zip(BIG, full16):
        full[k] = a if k in COL_SHARDED else a.reshape(-1, a.shape[-1])
    conv_full = jnp.concatenate([cw_all[p, :3] for p in range(N_CHIP)], axis=1)

    loss_cols, grad_x, g = local_step(
        x[0], mem[0], loss_target[0], g_mix, g_mem, q_norm_g, k_norm_g, g_mlp, conv_full,
        full["w_in"], full["w_conv_out"], full["w_sb_out"], full["w_mem_kv"], full["w_x_out"], full["w_out"],
        full["w_up"], full["w_down"])

    small = allreduce_small(
        _pack_small(D, g["g_mix"], g["g_mem"], g["g_mlp"], g["q_norm_g"], g["k_norm_g"], g["conv_w"], loss_cols),
        "allreduce_small")
    loss = (0.5 / D) * jnp.sum(small[7])
    gsum = {"g_mix": small[0:1], "g_mem": small[1:2], "g_mlp": small[2:3],
            "q_norm_g": small[3:4, :X_HEAD_DIM], "k_norm_g": small[3:4, X_HEAD_DIM:2 * X_HEAD_DIM],
            "conv_w": lax.dynamic_slice(small[4:7], (0, chip * cs), (3, cs))[None]}

    g4 = [g[k] if k in COL_SHARDED else g[k].reshape(N_CHIP, -1, g[k].shape[-1]) for k in BIG]
    got = exchange_halves(g4, "exchange_halves")
    pairs = [pair_sum(a, b, place, "pair_sum_" + k) for k, a, b in zip(BIG, g4, got)]
    got = scatter_partials([p[1] for p in pairs], "scatter_partials")
    halves = [chip_sum(p[0], b, place, "chip_sum_" + k) for k, p, b in zip(BIG, pairs, got)]
    for k, a in zip(BIG, join_halves(halves, "join_halves")):
        gsum[k] = a[None]

    small_names = ("g_mix", "g_mem", "g_mlp", "q_norm_g", "k_norm_g", "conv_w")
    zero_row = jnp.zeros((1, D), F32)
    packed = [_pack_small(D, *[t[k] if k != "conv_w" else t[k][0] for k in small_names], zero_row)
              for t in (w, gsum, m, v)]
    sm = adamw(*packed, "adamw_small")
    delta, new_m, new_v = {}, {}, {}
    for t, block in zip((delta, new_m, new_v), sm):
        t["g_mix"], t["g_mem"], t["g_mlp"] = block[0:1], block[1:2], block[2:3]
        t["q_norm_g"], t["k_norm_g"] = block[3:4, :X_HEAD_DIM], block[3:4, X_HEAD_DIM:2 * X_HEAD_DIM]
        t["conv_w"] = block[4:7, :cs][None]
    for k in BIG:
        delta[k], new_m[k], new_v[k] = adamw(w[k], gsum[k], m[k], v[k], "adamw_" + k)

    return (loss, grad_x[None], *[gsum[k] for k in WEIGHTS], *[delta[k] for k in WEIGHTS],
            *[new_m[k] for k in WEIGHTS], *[new_v[k] for k in WEIGHTS])
```

```python
import functools

import jax
import jax.numpy as jnp
from jax import lax
from jax.experimental import pallas as pl
from jax.experimental.pallas import tpu as pltpu

F32 = jnp.float32
BF16 = jnp.bfloat16
EPS = 1e-6
N_CHIP = 4
SB_HEAD_DIM = 64
X_HEAD_DIM = 256
LANES = 128
VMEM_LIMIT = 56 * 1024 * 1024
ADAM_LR, ADAM_B1, ADAM_B2, ADAM_EPS, ADAM_WD, ADAM_STEP = 0.001, 0.9, 0.999, 1e-8, 0.01, 10
MESH = pl.DeviceIdType.MESH


def _params(*sem):
    return pltpu.CompilerParams(dimension_semantics=sem, vmem_limit_bytes=VMEM_LIMIT)


def _tile(n, pref):
    if n <= pref:
        return n
    t = 1 << (pref.bit_length() - 1)
    while n % t:
        t //= 2
    return t


NN = (((1,), (0,)), ((), ()))
NT = (((1,), (1,)), ((), ()))
TN = (((0,), (0,)), ((), ()))


def _dot(a, b, dims):
    return lax.dot_general(a.astype(BF16), b.astype(BF16), dims, preferred_element_type=F32)


def mm_nn_shard(a, g, name, relu2=False):
    M, K = a.shape
    _, _, Ns = g.shape
    tm, tn = _tile(M, 1024), _tile(Ns, 512)
    nb = Ns // tn

    def body(a_ref, b_ref, *o_refs):
        acc = _dot(a_ref[...], b_ref[...], NN)
        o_refs[0][...] = acc
        if relu2:
            r = jnp.maximum(acc, 0.0)
            o_refs[1][...] = (r * r).astype(BF16)

    o_spec = pl.BlockSpec((tm, tn), lambda i, j: (i, j))
    shapes = [jax.ShapeDtypeStruct((M, N_CHIP * Ns), F32)]
    specs = [o_spec]
    if relu2:
        shapes.append(jax.ShapeDtypeStruct((M, N_CHIP * Ns), BF16))
        specs.append(o_spec)
    out = pl.pallas_call(
        body, grid=(M // tm, N_CHIP * nb), name=name,
        in_specs=[pl.BlockSpec((tm, K), lambda i, j: (i, 0)),
                  pl.BlockSpec((None, K, tn), lambda i, j: (j // nb, 0, j % nb))],
        out_specs=specs, out_shape=shapes, compiler_params=_params("parallel", "parallel"),
    )(a, g)
    return out if relu2 else out[0]


def mm_nn(a, w, name, res=None, out_dtype=F32):
    M, K = a.shape
    N = w.shape[1]
    tm, tn = _tile(M, 1024 if K <= 2048 else 512), _tile(N, 512)

    def body(a_ref, b_ref, *refs):
        acc = _dot(a_ref[...], b_ref[...], NN)
        if res is not None:
            acc = refs[0][...] + acc
        refs[-1][...] = acc.astype(out_dtype)

    o_spec = pl.BlockSpec((tm, tn), lambda i, j: (i, j))
    ins = [a, w] + ([res] if res is not None else [])
    return pl.pallas_call(
        body, grid=(M // tm, N // tn), name=name,
        in_specs=[pl.BlockSpec((tm, K), lambda i, j: (i, 0)), pl.BlockSpec((K, tn), lambda i, j: (0, j))]
        + ([o_spec] if res is not None else []),
        out_specs=o_spec, out_shape=jax.ShapeDtypeStruct((M, N), out_dtype),
        compiler_params=_params("parallel", "parallel"),
    )(*ins)


def mm_nt(a, w, name, up=None, out_dtype=F32):
    M, N = a.shape
    R = w.shape[0]
    tm, tr = _tile(M, 1024), _tile(R, 512)

    def body(a_ref, b_ref, *refs):
        acc = _dot(a_ref[...], b_ref[...], NT)
        if up is not None:
            acc = acc * (2.0 * jnp.maximum(refs[0][...], 0.0))
        refs[-1][...] = acc.astype(out_dtype)

    o_spec = pl.BlockSpec((tm, tr), lambda i, j: (i, j))
    ins = [a, w] + ([up] if up is not None else [])
    return pl.pallas_call(
        body, grid=(M // tm, R // tr), name=name,
        in_specs=[pl.BlockSpec((tm, N), lambda i, j: (i, 0)), pl.BlockSpec((tr, N), lambda i, j: (j, 0))]
        + ([o_spec] if up is not None else []),
        out_specs=o_spec, out_shape=jax.ShapeDtypeStruct((M, R), out_dtype),
        compiler_params=_params("parallel", "parallel"),
    )(*ins)


def mm_nt_shard(a, g, name, out_dtype=F32):
    M = a.shape[0]
    _, R, Ns = g.shape
    tm, tr, tk = _tile(M, 1024), _tile(R, 512), _tile(Ns, 1024)
    nb = Ns // tk
    nk = N_CHIP * nb

    def body(a_ref, b_ref, o_ref, acc_ref):
        k = pl.program_id(2)

        @pl.when(k == 0)
        def _():
            acc_ref[...] = jnp.zeros_like(acc_ref)

        acc_ref[...] += _dot(a_ref[...], b_ref[...], NT)

        @pl.when(k == nk - 1)
        def _():
            o_ref[...] = acc_ref[...].astype(out_dtype)

    return pl.pallas_call(
        body, grid=(M // tm, R // tr, nk), name=name,
        in_specs=[pl.BlockSpec((tm, tk), lambda i, j, k: (i, k)),
                  pl.BlockSpec((None, tr, tk), lambda i, j, k: (k // nb, j, k % nb))],
        out_specs=pl.BlockSpec((tm, tr), lambda i, j, k: (i, j)),
        out_shape=jax.ShapeDtypeStruct((M, R), out_dtype),
        scratch_shapes=[pltpu.VMEM((tm, tr), F32)],
        compiler_params=_params("parallel", "parallel", "arbitrary"),
    )(a, g)


def mm_tn(a, b, name, shard_out=False):
    S, M = a.shape
    N = b.shape[1]
    Ns = N // N_CHIP if shard_out else N
    tm, tn = _tile(M, 512), _tile(Ns, 512)
    nb = Ns // tn

    def body(a_ref, b_ref, o_ref):
        o_ref[...] = _dot(a_ref[...], b_ref[...], TN)

    if shard_out:
        o_spec = pl.BlockSpec((None, tm, tn), lambda i, j: (j // nb, i, j % nb))
        o_shape = jax.ShapeDtypeStruct((N_CHIP, M, Ns), F32)
    else:
        o_spec = pl.BlockSpec((tm, tn), lambda i, j: (i, j))
        o_shape = jax.ShapeDtypeStruct((M, N), F32)
    return pl.pallas_call(
        body, grid=(M // tm, N // tn), name=name,
        in_specs=[pl.BlockSpec((S, tm), lambda i, j: (0, i)), pl.BlockSpec((S, tn), lambda i, j: (0, j))],
        out_specs=o_spec, out_shape=o_shape, compiler_params=_params("parallel", "parallel"),
    )(a, b)


def rms_fwd(x, g, name):
    S, D = x.shape
    tm = _tile(S, 512)

    def body(x_ref, g_ref, h_ref):
        xv = x_ref[...]
        r = lax.rsqrt(jnp.mean(xv * xv, axis=-1, keepdims=True) + EPS)
        h_ref[...] = ((xv * r) * g_ref[...]).astype(BF16)

    return pl.pallas_call(
        body, grid=(S // tm,), name=name,
        in_specs=[pl.BlockSpec((tm, D), lambda i: (i, 0)), pl.BlockSpec((1, D), lambda i: (0, 0))],
        out_specs=pl.BlockSpec((tm, D), lambda i: (i, 0)),
        out_shape=jax.ShapeDtypeStruct((S, D), BF16), compiler_params=_params("parallel"),
    )(x, g)


def rms_bwd(x, g, dh, name, dres=None, want_dx=True):
    S, D = x.shape
    tm = _tile(S, 512)

    def body(x_ref, g_ref, dh_ref, *refs):
        i = pl.program_id(0)
        xv = x_ref[...]
        r = lax.rsqrt(jnp.mean(xv * xv, axis=-1, keepdims=True) + EPS)
        xn = xv * r
        dhv = dh_ref[...].astype(F32)
        gg_ref = refs[-1]

        @pl.when(i == 0)
        def _():
            gg_ref[...] = jnp.zeros_like(gg_ref)

        gg_ref[...] += jnp.sum(dhv * xn, axis=0, keepdims=True)
        if want_dx:
            dxn = dhv * g_ref[...]
            dx = r * (dxn - xn * jnp.mean(dxn * xn, axis=-1, keepdims=True))
            if dres is not None:
                dx = refs[0][...] + dx
            refs[-2][...] = dx

    row = pl.BlockSpec((tm, D), lambda i: (i, 0))
    vec = pl.BlockSpec((1, D), lambda i: (0, 0))
    ins, in_specs = [x, g, dh], [row, vec, row]
    if dres is not None:
        ins.append(dres)
        in_specs.append(row)
    shapes, specs = [jax.ShapeDtypeStruct((1, D), F32)], [vec]
    if want_dx:
        shapes.insert(0, jax.ShapeDtypeStruct((S, D), F32))
        specs.insert(0, row)
    out = pl.pallas_call(body, grid=(S // tm,), name=name, in_specs=in_specs, out_specs=specs,
                         out_shape=shapes, compiler_params=_params("arbitrary"))(*ins)
    return out if want_dx else out[0]


def _shift_down(u, k, row):
    return jnp.where(row >= k, pltpu.roll(u, k, axis=0), 0.0)


def _shift_up(u, k, row):
    S = u.shape[0]
    return jnp.where(row < S - k, pltpu.roll(u, S - k, axis=0), 0.0)


def _conv_specs(S, D, tc):
    nb = D // tc
    col = lambda o: pl.BlockSpec((S, tc), lambda j, o=o: (0, o * nb + j))
    return col, pl.BlockSpec((3, tc), lambda j: (0, j))


def conv_fwd(proj, conv_w, D, name):
    S = proj.shape[0]
    tc = _tile(D, 256)
    col, wspec = _conv_specs(S, D, tc)

    def body(ch_ref, cb_ref, cc_ref, w_ref, a_ref):
        row = lax.broadcasted_iota(jnp.int32, (S, tc), 0)
        u = cc_ref[...] * ch_ref[...]
        w = w_ref[...]
        cv = w[0:1, :] * _shift_down(u, 2, row) + w[1:2, :] * _shift_down(u, 1, row) + w[2:3, :] * u
        a_ref[...] = (cb_ref[...] * cv).astype(BF16)

    return pl.pallas_call(
        body, grid=(D // tc,), name=name, in_specs=[col(0), col(1), col(2), wspec],
        out_specs=pl.BlockSpec((S, tc), lambda j: (0, j)),
        out_shape=jax.ShapeDtypeStruct((S, D), BF16), compiler_params=_params("parallel"),
    )(proj, proj, proj, conv_w)


def conv_bwd(proj, conv_w, da, D, name):
    S = proj.shape[0]
    tc = _tile(D, 256)
    col, wspec = _conv_specs(S, D, tc)
    blk = pl.BlockSpec((S, tc), lambda j: (0, j))

    def body(ch_ref, cb_ref, cc_ref, w_ref, da_ref, dch_ref, dcb_ref, dcc_ref, gw_ref):
        row = lax.broadcasted_iota(jnp.int32, (S, tc), 0)
        ch, cb, cc, dav = ch_ref[...], cb_ref[...], cc_ref[...], da_ref[...]
        w = w_ref[...]
        u = cc * ch
        u1, u2 = _shift_down(u, 1, row), _shift_down(u, 2, row)
        cv = w[0:1, :] * u2 + w[1:2, :] * u1 + w[2:3, :] * u
        dcb_ref[...] = (dav * cv).astype(BF16)
        dcv = dav * cb
        gw_ref[0:1, :] = jnp.sum(dcv * u2, axis=0, keepdims=True)
        gw_ref[1:2, :] = jnp.sum(dcv * u1, axis=0, keepdims=True)
        gw_ref[2:3, :] = jnp.sum(dcv * u, axis=0, keepdims=True)
        du = w[2:3, :] * dcv + w[1:2, :] * _shift_up(dcv, 1, row) + w[0:1, :] * _shift_up(dcv, 2, row)
        dcc_ref[...] = (du * ch).astype(BF16)
        dch_ref[...] = (du * cc).astype(BF16)

    act = jax.ShapeDtypeStruct((S, D), BF16)
    return pl.pallas_call(
        body, grid=(D // tc,), name=name, in_specs=[col(0), col(1), col(2), wspec, blk],
        out_specs=[blk, blk, blk, wspec], out_shape=[act, act, act, jax.ShapeDtypeStruct((3, D), F32)],
        compiler_params=_params("parallel"),
    )(proj, proj, proj, conv_w, da)


SB_BQ = 256
SB_BK = 128
SB_GROUP = 4


def _sb_consts(bq):
    lane = lax.broadcasted_iota(jnp.int32, (bq, LANES), 1)
    r = lax.broadcasted_iota(jnp.int32, (SB_BK, SB_BK), 0)
    c = lax.broadcasted_iota(jnp.int32, (SB_BK, SB_BK), 1)
    tri_rev = jnp.where(r > c, 1.0, 0.0).astype(BF16)
    tri_fwd = jnp.where(r < c, 1.0, 0.0).astype(BF16)
    return lane, tri_rev, tri_fwd


def _cumsum2(v, tri):
    hi = v.astype(BF16)
    lo = (v - hi.astype(F32)).astype(BF16)
    part = (lax.dot_general(hi, tri, NN, preferred_element_type=F32)
            + lax.dot_general(lo, tri, NN, preferred_element_type=F32))
    return part, jnp.sum(v, axis=1, keepdims=True)


def _sb_logits(z, past):
    sp = jnp.log(1.0 + jnp.exp(-jnp.abs(z)))
    l = jnp.minimum(z, 0.0) - sp
    m = l - z
    if past is not None:
        m = jnp.where(past, m, 0.0)
    return l, m


def _stack_heads(v, lane):
    return jnp.concatenate([jnp.where(lane < SB_HEAD_DIM, v, 0.0), jnp.where(lane >= SB_HEAD_DIM, v, 0.0)],
                           axis=0).astype(BF16)


def _unstack_heads(v, lane):
    bq = v.shape[0] // 2
    return jnp.where(lane < SB_HEAD_DIM, v[:bq], v[bq:])


def _sb_positions(i, bq):
    r = lax.broadcasted_iota(jnp.int32, (2 * bq, SB_BK), 0)
    trow = i * bq + jnp.where(r >= bq, r - bq, r)
    return trow, lax.broadcasted_iota(jnp.int32, (2 * bq, SB_BK), 1)


def _sb_specs(S, D, bq):
    npair = D // LANES
    qspec = pl.BlockSpec((bq, LANES), lambda p, i: (i, 3 * npair + p))
    kspec = pl.BlockSpec((S, LANES), lambda p, i: (0, 4 * npair + p))
    vspec = pl.BlockSpec((S, LANES), lambda p, i: (0, 5 * npair + p))
    return npair, qspec, kspec, vspec


def sb_fwd(proj, D, name):
    S = proj.shape[0]
    bq = SB_BQ
    nd = bq // SB_BK
    npair, qspec, kspec, vspec = _sb_specs(S, D, bq)
    scale = SB_HEAD_DIM ** -0.5

    def body(q_ref, k_ref, v_ref, o_ref, kb_ref, vb_ref):
        i = pl.program_id(1)

        @pl.when(i == 0)
        def _():
            kb_ref[...] = k_ref[...].astype(BF16)
            vb_ref[...] = v_ref[...].astype(BF16)

        lane, tri_rev, _ = _sb_consts(bq)
        qs = _stack_heads(q_ref[...] * scale, lane)
        trow, scol = _sb_positions(i, bq)

        def steps(j0, carry, n, masked):
            ks = [pl.multiple_of((j0 - t) * SB_BK, SB_BK) for t in range(n)]
            past = [(k + scol) < trow if masked else None for k in ks]
            zs = [lax.dot_general(qs, kb_ref[pl.ds(k, SB_BK), :], NT, preferred_element_type=F32) for k in ks]
            lm = [_sb_logits(z, p) for z, p in zip(zs, past)]
            cs = [_cumsum2(m, tri_rev) for _, m in lm]
            c, acc = carry
            for t in range(n):
                a = jnp.exp(lm[t][0] + (cs[t][0] + c))
                if masked:
                    a = jnp.where(past[t], a, 0.0)
                acc = acc + lax.dot_general(a.astype(BF16), vb_ref[pl.ds(ks[t], SB_BK), :], NN,
                                            preferred_element_type=F32)
                c = c + cs[t][1]
            return c, acc

        carry = (jnp.zeros((2 * bq, 1), F32), jnp.zeros((2 * bq, LANES), F32))
        carry = steps(i * nd + nd - 1, carry, nd, True)
        older = i * nd
        groups = older // SB_GROUP
        carry = lax.fori_loop(
            0, groups, lambda t, cr: steps(older - 1 - t * SB_GROUP, cr, SB_GROUP, False), carry)
        rest = older - groups * SB_GROUP
        carry = lax.fori_loop(0, rest // nd, lambda t, cr: steps(rest - 1 - t * nd, cr, nd, False), carry)
        o_ref[...] = _unstack_heads(carry[1], lane)

    return pl.pallas_call(
        body, grid=(npair, S // bq), name=name, in_specs=[qspec, kspec, vspec],
        out_specs=pl.BlockSpec((bq, LANES), lambda p, i: (i, p)),
        out_shape=jax.ShapeDtypeStruct((S, D), F32),
        scratch_shapes=[pltpu.VMEM((S, LANES), BF16), pltpu.VMEM((S, LANES), BF16)],
        compiler_params=_params("parallel", "arbitrary"),
    )(proj, proj, proj)


def sb_bwd(proj, do, D, name):
    S = proj.shape[0]
    bq = SB_BQ
    nd = bq // SB_BK
    nkb = S // SB_BK
    npair, qspec, kspec, vspec = _sb_specs(S, D, bq)
    scale = SB_HEAD_DIM ** -0.5

    def body(q_ref, k_ref, v_ref, do_ref, dq_ref, dk_ref, dv_ref,
             kb_ref, vb_ref, dk_acc, dv_acc, g_scr, b_scr, a_scr):
        i = pl.program_id(1)

        @pl.when(i == 0)
        def _():
            kb_ref[...] = k_ref[...].astype(BF16)
            vb_ref[...] = v_ref[...].astype(BF16)
            dk_acc[...] = jnp.zeros_like(dk_acc)
            dv_acc[...] = jnp.zeros_like(dv_acc)

        lane, tri_rev, tri_fwd = _sb_consts(bq)
        qs = _stack_heads(q_ref[...] * scale, lane)
        dos = _stack_heads(do_ref[...], lane)
        trow, scol = _sb_positions(i, bq)

        def sweep1(j0, c, n, masked):
            js = [j0 - t for t in range(n)]
            ks = [pl.multiple_of(j * SB_BK, SB_BK) for j in js]
            past = [(k + scol) < trow if masked else None for k in ks]
            zs = [lax.dot_general(qs, kb_ref[pl.ds(k, SB_BK), :], NT, preferred_element_type=F32) for k in ks]
            das = [lax.dot_general(dos, vb_ref[pl.ds(k, SB_BK), :], NT, preferred_element_type=F32) for k in ks]
            lm = [_sb_logits(z, p) for z, p in zip(zs, past)]
            cs = [_cumsum2(m, tri_rev) for _, m in lm]
            for t in range(n):
                b_scr[js[t]] = jnp.exp(lm[t][0])
            for t in range(n):
                a = jnp.exp(lm[t][0] + (cs[t][0] + c))
                if masked:
                    a = jnp.where(past[t], a, 0.0)
                g_scr[js[t]] = das[t] * a
                a_scr[js[t]] = a.astype(BF16)
                c = c + cs[t][1]
            return c

        older = i * nd
        groups = older // SB_GROUP
        rest = older - groups * SB_GROUP
        c = jnp.zeros((2 * bq, 1), F32)
        c = sweep1(i * nd + nd - 1, c, nd, True)
        c = lax.fori_loop(0, groups, lambda t, cr: sweep1(older - 1 - t * SB_GROUP, cr, SB_GROUP, False), c)
        lax.fori_loop(0, rest // nd, lambda t, cr: sweep1(rest - 1 - t * nd, cr, nd, False), c)

        def sweep2(j0, carry, n, masked):
            js = [j0 + t for t in range(n)]
            ks = [pl.multiple_of(j * SB_BK, SB_BK) for j in js]
            gv = [g_scr[j] for j in js]
            gs = [_cumsum2(g, tri_fwd) for g in gv]
            pc, dq = carry
            dzs = []
            for t in range(n):
                dz = gv[t] - b_scr[js[t]] * (gv[t] + (gs[t][0] + pc))
                if masked:
                    dz = jnp.where((ks[t] + scol) < trow, dz, 0.0)
                dzs.append(dz.astype(BF16))
                pc = pc + gs[t][1]
            for t in range(n):
                dq = dq + lax.dot_general(dzs[t], kb_ref[pl.ds(ks[t], SB_BK), :], NN, preferred_element_type=F32)
                dk_acc[pl.ds(ks[t], SB_BK), :] += lax.dot_general(dzs[t], qs, TN, preferred_element_type=F32)
                dv_acc[pl.ds(ks[t], SB_BK), :] += lax.dot_general(a_scr[js[t]], dos, TN, preferred_element_type=F32)
            return pc, dq

        carry = (jnp.zeros((2 * bq, 1), F32), jnp.zeros((2 * bq, LANES), F32))
        carry = lax.fori_loop(0, groups, lambda t, cr: sweep2(t * SB_GROUP, cr, SB_GROUP, False), carry)
        carry = lax.fori_loop(
            0, rest // nd, lambda t, cr: sweep2(groups * SB_GROUP + t * nd, cr, nd, False), carry)
        carry = sweep2(i * nd, carry, nd, True)
        dq_ref[...] = (_unstack_heads(carry[1], lane) * scale).astype(BF16)

        @pl.when(i == pl.num_programs(1) - 1)
        def _():
            dk_ref[...] = dk_acc[...].astype(BF16)
            dv_ref[...] = dv_acc[...].astype(BF16)

    full = pl.BlockSpec((S, LANES), lambda p, i: (0, p))
    blk = pl.BlockSpec((bq, LANES), lambda p, i: (i, p))
    act = jax.ShapeDtypeStruct((S, D), BF16)
    return pl.pallas_call(
        body, grid=(npair, S // bq), name=name, in_specs=[qspec, kspec, vspec, blk],
        out_specs=[blk, full, full], out_shape=[act, act, act],
        scratch_shapes=[pltpu.VMEM((S, LANES), BF16), pltpu.VMEM((S, LANES), BF16),
                        pltpu.VMEM((S, LANES), F32), pltpu.VMEM((S, LANES), F32),
                        pltpu.VMEM((nkb, 2 * bq, SB_BK), F32), pltpu.VMEM((nkb, 2 * bq, SB_BK), F32),
                        pltpu.VMEM((nkb, 2 * bq, SB_BK), BF16)],
        compiler_params=_params("parallel", "arbitrary"),
    )(proj, proj, proj, do)


def _rms_rows(v):
    r = lax.rsqrt(jnp.mean(v * v, axis=-1, keepdims=True) + EPS)
    return v * r, r


def _xa_specs(S, D, M, tq):
    nh = D // X_HEAD_DIM
    qspec = pl.BlockSpec((tq, X_HEAD_DIM), lambda h, i: (i, 6 * nh + h))
    kspec = pl.BlockSpec((M, X_HEAD_DIM), lambda h, i: (0, h))
    vspec = pl.BlockSpec((M, X_HEAD_DIM), lambda h, i: (0, nh + h))
    gspec = pl.BlockSpec((1, X_HEAD_DIM), lambda h, i: (0, 0))
    return nh, qspec, kspec, vspec, gspec


def xa_fwd(proj, kv, gq, gk, D, name):
    S, M = proj.shape[0], kv.shape[0]
    tq = _tile(S, 512)
    nh, qspec, kspec, vspec, gspec = _xa_specs(S, D, M, tq)
    scale = X_HEAD_DIM ** -0.5

    def body(q_ref, k_ref, v_ref, gq_ref, gk_ref, o_ref):
        qn = _rms_rows(q_ref[...])[0] * gq_ref[...]
        kn = _rms_rows(k_ref[...])[0] * gk_ref[...]
        s = _dot(qn, kn, NT) * scale
        e = jnp.exp(s - jnp.max(s, axis=-1, keepdims=True))
        p = e / jnp.sum(e, axis=-1, keepdims=True)
        o_ref[...] = _dot(p, v_ref[...], NN)

    return pl.pallas_call(
        body, grid=(nh, S // tq), name=name, in_specs=[qspec, kspec, vspec, gspec, gspec],
        out_specs=pl.BlockSpec((tq, X_HEAD_DIM), lambda h, i: (i, h)),
        out_shape=jax.ShapeDtypeStruct((S, D), F32), compiler_params=_params("parallel", "parallel"),
    )(proj, kv, kv, gq, gk)


def xa_bwd(proj, kv, gq, gk, do, D, name):
    S, M = proj.shape[0], kv.shape[0]
    tq = _tile(S, 512)
    nh, qspec, kspec, vspec, gspec = _xa_specs(S, D, M, tq)
    scale = X_HEAD_DIM ** -0.5

    def body(q_ref, k_ref, v_ref, gq_ref, gk_ref, do_ref, dq_ref, dk_ref, dv_ref, ggq_ref, ggk_ref,
             dkn_acc, dv_acc):
        h, i = pl.program_id(0), pl.program_id(1)

        @pl.when((h == 0) & (i == 0))
        def _():
            ggq_ref[...] = jnp.zeros_like(ggq_ref)
            ggk_ref[...] = jnp.zeros_like(ggk_ref)

        @pl.when(i == 0)
        def _():
            dkn_acc[...] = jnp.zeros_like(dkn_acc)
            dv_acc[...] = jnp.zeros_like(dv_acc)

        gq, gk = gq_ref[...], gk_ref[...]
        qhat, rq = _rms_rows(q_ref[...])
        khat, rk = _rms_rows(k_ref[...])
        qn, kn = qhat * gq, khat * gk
        s = _dot(qn, kn, NT) * scale
        e = jnp.exp(s - jnp.max(s, axis=-1, keepdims=True))
        p = e / jnp.sum(e, axis=-1, keepdims=True)
        dov = do_ref[...]
        dv_acc[...] += _dot(p, dov, TN)
        dp = _dot(dov, v_ref[...], NT)
        ds = (p * (dp - jnp.sum(dp * p, axis=-1, keepdims=True))) * scale
        dqn = _dot(ds, kn, NN)
        dkn_acc[...] += _dot(ds, qn, TN)
        ggq_ref[...] += jnp.sum(dqn * qhat, axis=0, keepdims=True)
        dqh = dqn * gq
        dq_ref[...] = (rq * (dqh - qhat * jnp.mean(dqh * qhat, axis=-1, keepdims=True))).astype(BF16)

        @pl.when(i == pl.num_programs(1) - 1)
        def _():
            dkn = dkn_acc[...]
            ggk_ref[...] += jnp.sum(dkn * khat, axis=0, keepdims=True)
            dkh = dkn * gk
            dk_ref[...] = (rk * (dkh - khat * jnp.mean(dkh * khat, axis=-1, keepdims=True))).astype(BF16)
            dv_ref[...] = dv_acc[...].astype(BF16)

    blk = pl.BlockSpec((tq, X_HEAD_DIM), lambda h, i: (i, h))
    kv_shape = jax.ShapeDtypeStruct((M, 2 * D), BF16)
    gshape = jax.ShapeDtypeStruct((1, X_HEAD_DIM), F32)
    dq, dk, dv, ggq, ggk = pl.pallas_call(
        body, grid=(nh, S // tq), name=name, in_specs=[qspec, kspec, vspec, gspec, gspec, blk],
        out_specs=[blk, kspec, vspec, gspec, gspec],
        out_shape=[jax.ShapeDtypeStruct((S, D), BF16), kv_shape, kv_shape, gshape, gshape],
        scratch_shapes=[pltpu.VMEM((M, X_HEAD_DIM), F32), pltpu.VMEM((M, X_HEAD_DIM), F32)],
        compiler_params=_params("arbitrary", "arbitrary"),
    )(proj, kv, kv, gq, gk, do)
    d_kv = jnp.concatenate([dk[:, :D], dv[:, D:]], axis=1)
    return dq, d_kv, ggq, ggk


def _gate_specs(S, D, tm):
    row = pl.BlockSpec((tm, D), lambda i: (i, 0))
    gate = lambda b: pl.BlockSpec((tm, D), lambda i, b=b: (i, 7 + b))
    return row, gate


def merge_fwd(proj, ys, D, name):
    S = proj.shape[0]
    tm = _tile(S, 256)
    row, gate = _gate_specs(S, D, tm)

    def body(g0, g1, g2, y0, y1, y2, o_ref):
        acc = jax.nn.sigmoid(g0[...]) * y0[...]
        acc = acc + jax.nn.sigmoid(g1[...]) * y1[...]
        acc = acc + jax.nn.sigmoid(g2[...]) * y2[...]
        o_ref[...] = acc.astype(BF16)

    return pl.pallas_call(
        body, grid=(S // tm,), name=name, in_specs=[gate(0), gate(1), gate(2), row, row, row],
        out_specs=row, out_shape=jax.ShapeDtypeStruct((S, D), BF16), compiler_params=_params("parallel"),
    )(proj, proj, proj, *ys)


def merge_bwd(proj, ys, dm, D, name):
    S = proj.shape[0]
    tm = _tile(S, 256)
    row, gate = _gate_specs(S, D, tm)

    def body(g0, g1, g2, y0, y1, y2, dm_ref, d0, d1, d2, dg_ref):
        dmv = dm_ref[...]
        for b, (g_ref, y_ref, d_ref) in enumerate(((g0, y0, d0), (g1, y1, d1), (g2, y2, d2))):
            s = jax.nn.sigmoid(g_ref[...])
            d_ref[...] = (dmv * s).astype(BF16)
            dg_ref[:, b * D:(b + 1) * D] = ((dmv * y_ref[...]) * (s * (1.0 - s))).astype(BF16)

    act = jax.ShapeDtypeStruct((S, D), BF16)
    return pl.pallas_call(
        body, grid=(S // tm,), name=name, in_specs=[gate(0), gate(1), gate(2), row, row, row, row],
        out_specs=[row, row, row, pl.BlockSpec((tm, 3 * D), lambda i: (i, 0))],
        out_shape=[act, act, act, jax.ShapeDtypeStruct((S, 3 * D), BF16)], compiler_params=_params("parallel"),
    )(proj, proj, proj, *ys, dm)


def loss_head(y, target, name):
    S, D = y.shape
    tm = _tile(S, 512)

    def body(y_ref, t_ref, dy_ref, l_ref):
        @pl.when(pl.program_id(0) == 0)
        def _():
            l_ref[...] = jnp.zeros_like(l_ref)

        e = y_ref[...] - t_ref[...]
        dy_ref[...] = e * (1.0 / D)
        l_ref[...] += jnp.sum(e * e, axis=0, keepdims=True)

    row = pl.BlockSpec((tm, D), lambda i: (i, 0))
    vec = pl.BlockSpec((1, D), lambda i: (0, 0))
    return pl.pallas_call(
        body, grid=(S // tm,), name=name, in_specs=[row, row], out_specs=[row, vec],
        out_shape=[jax.ShapeDtypeStruct((S, D), F32), jax.ShapeDtypeStruct((1, D), F32)],
        compiler_params=_params("arbitrary"),
    )(y, target)


def _rows2d(a):
    return a.reshape(-1, a.shape[-1])


def _ew_call(fn, ins, out_dtypes, name):
    R, C = ins[0].shape
    tr = _tile(R, max(8, (1 << 19) // C))
    spec = pl.BlockSpec((tr, C), lambda i: (i, 0))

    def body(*refs):
        outs = fn(*[r[...] for r in refs[:len(ins)]])
        for o_ref, o in zip(refs[len(ins):], outs):
            o_ref[...] = o.astype(o_ref.dtype)

    return pl.pallas_call(
        body, grid=(R // tr,), name=name, in_specs=[spec] * len(ins), out_specs=[spec] * len(out_dtypes),
        out_shape=[jax.ShapeDtypeStruct((R, C), d) for d in out_dtypes], compiler_params=_params("parallel"),
    )(*ins)


def adamw(w, g, m, v, name):
    def fn(w, g, m, v):
        m = ADAM_B1 * m + (1.0 - ADAM_B1) * g
        v = ADAM_B2 * v + (1.0 - ADAM_B2) * (g * g)
        m_hat = m / (1.0 - ADAM_B1 ** ADAM_STEP)
        v_hat = v / (1.0 - ADAM_B2 ** ADAM_STEP)
        return -ADAM_LR * (m_hat / (jnp.sqrt(v_hat) + ADAM_EPS) + ADAM_WD * w), m, v

    shp = w.shape
    outs = _ew_call(fn, [_rows2d(a) for a in (w, g, m, v)], [F32, F32, F32], name)
    return [o.reshape(shp) for o in outs]


def _placed_call(fn, place, grid, ins, in_specs, out_shape, out_specs, name):
    n = len(ins)

    def body(place_ref, *refs):
        outs = fn(*[r[...] for r in refs[:n]])
        for o_ref, o in zip(refs[n:], outs):
            o_ref[...] = o.astype(o_ref.dtype)

    return pl.pallas_call(
        body, name=name, out_shape=out_shape,
        grid_spec=pltpu.PrefetchScalarGridSpec(
            num_scalar_prefetch=1, grid=grid, in_specs=in_specs, out_specs=out_specs),
        compiler_params=_params(*["parallel"] * len(grid)),
    )(place, *ins)


def _row_tile(R, C):
    return _tile(R, max(16, (1 << 19) // C))


def cast_into_full(w, place, name):
    R, C = w.shape
    tr = _row_tile(R, C)
    return _placed_call(
        lambda a: (a,), place, (R // tr,), [w], [pl.BlockSpec((tr, C), lambda i, p: (i, 0))],
        [jax.ShapeDtypeStruct((N_CHIP, R, C), BF16)], [pl.BlockSpec((None, tr, C), lambda i, p: (p[0], i, 0))],
        name)[0]


def pair_sum(g4, got, place, name):
    _, hr, C = got.shape
    tr = _row_tile(hr, C)
    nb = hr // tr
    blk = pl.BlockSpec((None, tr, C), lambda s, i, p: (s, i, 0))
    return _placed_call(
        lambda a, b: (a + b, a + b), place, (N_CHIP, nb), [g4, got],
        [pl.BlockSpec((None, tr, C), lambda s, i, p: (s, p[1] * nb + i, 0)), blk],
        [jax.ShapeDtypeStruct(got.shape, F32), jax.ShapeDtypeStruct(got.shape, BF16)], [blk, blk], name)


def chip_sum(p32, got, place, name):
    _, H, C = p32.shape
    tr = _row_tile(H, C)
    nb = H // tr
    peer = lambda j: pl.BlockSpec((None, tr, C), lambda i, p, j=j: (j, i, 0))
    return _placed_call(
        lambda a, b, c, d: (((a + b.astype(F32)) + c.astype(F32)) + d.astype(F32),), place, (nb,),
        [p32, got, got, got], [pl.BlockSpec((None, tr, C), lambda i, p: (p[0], i, 0)), peer(0), peer(1), peer(2)],
        [jax.ShapeDtypeStruct((2 * H, C), F32)], [pl.BlockSpec((tr, C), lambda i, p: (p[1] * nb + i, 0))],
        name)[0]


ANY = pl.BlockSpec(memory_space=pl.ANY)
CHIP_FLIPS = ((1, 0), (0, 1), (1, 1))


def _place():
    return lax.axis_index("x"), lax.axis_index("y"), lax.axis_index("c")


def _flip(v, f):
    return 1 - v if f else v


def gather_weights(fulls, small, name):
    n = len(fulls)

    def body(*refs):
        small_in = refs[n]
        outs, small_out = refs[n + 1:2 * n + 1], refs[2 * n + 1]
        ici_send, ici_recv, d2d_send, d2d_recv, loc_sem, sm_send, sm_recv = refs[2 * n + 2:]
        x, y, c = _place()
        mine = 2 * x + y
        sibling = (x, y, 1 - c)
        chips = [(_flip(x, fx), _flip(y, fy)) for fx, fy in CHIP_FLIPS]

        local = [pltpu.make_async_copy(small_in, small_out.at[mine], loc_sem)]
        local[0].start()

        def half(a, chip_idx, core):
            hr = outs[a].shape[1] // 2
            return outs[a].at[chip_idx, pl.ds(core * hr, hr), :]

        sends = []
        for j, (px, py) in enumerate(chips):
            cp = pltpu.make_async_remote_copy(
                src_ref=small_in, dst_ref=small_out.at[mine], send_sem=sm_send.at[j], recv_sem=sm_recv.at[j],
                device_id=(px, py, c), device_id_type=MESH)
            cp.start()
            sends.append(cp)
        for a in range(n):
            for j, (px, py) in enumerate(chips):
                cp = pltpu.make_async_remote_copy(
                    src_ref=half(a, mine, c), dst_ref=half(a, mine, c),
                    send_sem=ici_send.at[a, j], recv_sem=ici_recv.at[a, j],
                    device_id=(px, py, c), device_id_type=MESH)
                cp.start()
                sends.append(cp)
        for a in range(n):
            for j, (px, py) in enumerate(chips):
                src = half(a, 2 * px + py, c)
                pltpu.make_async_remote_copy(
                    src_ref=src, dst_ref=src, send_sem=ici_send.at[a, j], recv_sem=ici_recv.at[a, j],
                    device_id=(px, py, c), device_id_type=MESH).wait_recv()
                cp = pltpu.make_async_remote_copy(
                    src_ref=src, dst_ref=src, send_sem=d2d_send.at[a, j], recv_sem=d2d_recv.at[a, j],
                    device_id=sibling, device_id_type=MESH)
                cp.start()
                sends.append(cp)
        for a in range(n):
            for j, (px, py) in enumerate(chips):
                dst = half(a, 2 * px + py, 1 - c)
                pltpu.make_async_remote_copy(
                    src_ref=dst, dst_ref=dst, send_sem=d2d_send.at[a, j], recv_sem=d2d_recv.at[a, j],
                    device_id=sibling, device_id_type=MESH).wait_recv()
        for j, (px, py) in enumerate(chips):
            dst = small_out.at[2 * px + py]
            pltpu.make_async_remote_copy(
                src_ref=dst, dst_ref=dst, send_sem=sm_send.at[j], recv_sem=sm_recv.at[j],
                device_id=(px, py, c), device_id_type=MESH).wait_recv()
        for cp in sends:
            cp.wait_send()
        for cp in local:
            cp.wait()

    out_shape = [jax.ShapeDtypeStruct(f.shape, f.dtype) for f in fulls]
    out_shape.append(jax.ShapeDtypeStruct((N_CHIP,) + small.shape, small.dtype))
    dma = pltpu.SemaphoreType.DMA
    return pl.pallas_call(
        body, name=name, in_specs=[ANY] * (n + 1), out_specs=[ANY] * (n + 1), out_shape=out_shape,
        input_output_aliases={a: a for a in range(n)},
        scratch_shapes=[dma((n, 3)), dma((n, 3)), dma((n, 3)), dma((n, 3)), dma, dma((3,)), dma((3,))],
    )(*fulls, small)


def exchange_halves(grads, name):
    n = len(grads)

    def body(*refs):
        ins, got = refs[:n], refs[n:2 * n]
        send_sem, recv_sem = refs[2 * n:]
        x, y, c = _place()
        copies = []
        for a in range(n):
            hr = ins[a].shape[1] // 2
            cp = pltpu.make_async_remote_copy(
                src_ref=ins[a].at[:, pl.ds((1 - c) * hr, hr), :], dst_ref=got[a],
                send_sem=send_sem.at[a], recv_sem=recv_sem.at[a], device_id=(x, y, 1 - c), device_id_type=MESH)
            cp.start()
            copies.append(cp)
        for cp in copies:
            cp.wait()

    dma = pltpu.SemaphoreType.DMA
    return pl.pallas_call(
        body, name=name, in_specs=[ANY] * n, out_specs=[ANY] * n,
        out_shape=[jax.ShapeDtypeStruct((N_CHIP, g.shape[1] // 2, g.shape[2]), g.dtype) for g in grads],
        scratch_shapes=[dma((n,)), dma((n,))],
    )(*grads)


def scatter_partials(parts16, name):
    n = len(parts16)

    def body(*refs):
        p16, got = refs[:n], refs[n:2 * n]
        send_sem, recv_sem = refs[2 * n:]
        x, y, c = _place()
        chips = [(_flip(x, fx), _flip(y, fy)) for fx, fy in CHIP_FLIPS]
        copies = []
        for a in range(n):
            for j, (px, py) in enumerate(chips):
                cp = pltpu.make_async_remote_copy(
                    src_ref=p16[a].at[2 * px + py], dst_ref=got[a].at[j],
                    send_sem=send_sem.at[a, j], recv_sem=recv_sem.at[a, j],
                    device_id=(px, py, c), device_id_type=MESH)
                cp.start()
                copies.append(cp)
        for cp in copies:
            cp.wait()

    dma = pltpu.SemaphoreType.DMA
    return pl.pallas_call(
        body, name=name, in_specs=[ANY] * n, out_specs=[ANY] * n,
        out_shape=[jax.ShapeDtypeStruct((3,) + p.shape[1:], BF16) for p in parts16],
        scratch_shapes=[dma((n, 3)), dma((n, 3))],
    )(*parts16)


def join_halves(fulls, name):
    n = len(fulls)

    def body(*refs):
        outs = refs[n:2 * n]
        send_sem, recv_sem = refs[2 * n:]
        x, y, c = _place()
        copies = []
        for a in range(n):
            hr = outs[a].shape[0] // 2
            half = outs[a].at[pl.ds(c * hr, hr), :]
            cp = pltpu.make_async_remote_copy(
                src_ref=half, dst_ref=half, send_sem=send_sem.at[a], recv_sem=recv_sem.at[a],
                device_id=(x, y, 1 - c), device_id_type=MESH)
            cp.start()
            copies.append(cp)
        for a, cp in enumerate(copies):
            hr = outs[a].shape[0] // 2
            theirs = outs[a].at[pl.ds((1 - c) * hr, hr), :]
            cp.wait_send()
            pltpu.make_async_remote_copy(
                src_ref=theirs, dst_ref=theirs, send_sem=send_sem.at[a], recv_sem=recv_sem.at[a],
                device_id=(x, y, 1 - c), device_id_type=MESH).wait_recv()

    dma = pltpu.SemaphoreType.DMA
    return pl.pallas_call(
        body, name=name, in_specs=[ANY] * n, out_specs=[ANY] * n,
        out_shape=[jax.ShapeDtypeStruct(f.shape, F32) for f in fulls],
        input_output_aliases={a: a for a in range(n)},
        scratch_shapes=[dma((n,)), dma((n,))],
    )(*fulls)


def allreduce_small(block, name):
    R, C = block.shape

    def body(in_ref, out_ref, slots, send_sem, recv_sem):
        x, y, c = _place()
        me = 4 * x + 2 * y + c
        slots[me] = in_ref[...]
        copies = []
        for r in range(1, 8):
            fx, fy, fc = (r >> 2) & 1, (r >> 1) & 1, r & 1
            cp = pltpu.make_async_remote_copy(
                src_ref=in_ref, dst_ref=slots.at[me], send_sem=send_sem.at[r - 1], recv_sem=recv_sem.at[r - 1],
                device_id=(_flip(x, fx), _flip(y, fy), _flip(c, fc)), device_id_type=MESH)
            cp.start()
            copies.append(cp)
        for cp in copies:
            cp.wait()
        acc = slots[0]
        for d in range(1, 8):
            acc = acc + slots[d]
        out_ref[...] = acc

    vm = pl.BlockSpec(memory_space=pltpu.VMEM)
    dma = pltpu.SemaphoreType.DMA
    return pl.pallas_call(
        body, name=name, in_specs=[vm], out_specs=vm, out_shape=jax.ShapeDtypeStruct((R, C), F32),
        scratch_shapes=[pltpu.VMEM((8, R, C), F32), dma((7,)), dma((7,))],
    )(block)


def local_step(x, mem, target, g_mix, g_mem, q_norm_g, k_norm_g, g_mlp, conv_w,
               w_in, w_conv_out, w_sb_out, w_mem_kv, w_x_out, w_out, w_up, w_down):
    S, D = x.shape
    h = rms_fwd(x, g_mix, "rms_mix")
    proj = mm_nn_shard(h, w_in, "proj")
    a_conv = conv_fwd(proj, conv_w, D, "conv_fwd")
    o_sb = sb_fwd(proj, D, "sb_fwd")
    mem_n = rms_fwd(mem, g_mem, "rms_mem")
    kv = mm_nn_shard(mem_n, w_mem_kv, "kv")
    o_x = xa_fwd(proj, kv, q_norm_g, k_norm_g, D, "xa_fwd")
    ys = [mm_nn(a_conv, w_conv_out, "y_conv"), mm_nn(o_sb, w_sb_out, "y_sb"), mm_nn(o_x, w_x_out, "y_x")]
    merged = merge_fwd(proj, ys, D, "merge_fwd")
    x1 = mm_nn(merged, w_out, "x1", res=x)
    h2 = rms_fwd(x1, g_mlp, "rms_mlp")
    up, act = mm_nn_shard(h2, w_up, "up", relu2=True)
    x2 = mm_nn(act, w_down, "x2", res=x1)
    dy, loss_cols = loss_head(x2, target, "loss_head")
    d_up = mm_nt(dy, w_down, "d_up", up=up, out_dtype=BF16)
    g = {"w_down": mm_tn(act, dy, "g_w_down")}
    g["w_up"] = mm_tn(h2, d_up, "g_w_up", shard_out=True)
    dh2 = mm_nt_shard(d_up, w_up, "dh2")
    dx1, g["g_mlp"] = rms_bwd(x1, g_mlp, dh2, "rms_mlp_bwd", dres=dy)
    g["w_out"] = mm_tn(merged, dx1, "g_w_out")
    dm = mm_nt(dx1, w_out, "d_merged")
    dy_c, dy_s, dy_x, d_gate = merge_bwd(proj, ys, dm, D, "merge_bwd")
    g["w_conv_out"] = mm_tn(a_conv, dy_c, "g_w_conv_out")
    g["w_sb_out"] = mm_tn(o_sb, dy_s, "g_w_sb_out")
    g["w_x_out"] = mm_tn(o_x, dy_x, "g_w_x_out")
    d_ch, d_cb, d_cc, g["conv_w"] = conv_bwd(proj, conv_w, mm_nt(dy_c, w_conv_out, "d_a_conv"), D, "conv_bwd")
    dq, dk, dv = sb_bwd(proj, mm_nt(dy_s, w_sb_out, "d_o_sb"), D, "sb_bwd")
    d_xq, d_kv, g["q_norm_g"], g["k_norm_g"] = xa_bwd(
        proj, kv, q_norm_g, k_norm_g, mm_nt(dy_x, w_x_out, "d_o_x"), D, "xa_bwd")
    g["w_mem_kv"] = mm_tn(mem_n, d_kv, "g_w_mem_kv", shard_out=True)
    g["g_mem"] = rms_bwd(mem, g_mem, mm_nt_shard(d_kv, w_mem_kv, "d_mem_n"), "rms_mem_bwd", want_dx=False)
    d_proj = jnp.concatenate([d_ch, d_cb, d_cc, dq, dk, dv, d_xq, d_gate], axis=1)
    g["w_in"] = mm_tn(h, d_proj, "g_w_in", shard_out=True)
    dh = mm_nt_shard(d_proj, w_in, "dh")
    grad_x, g["g_mix"] = rms_bwd(x, g_mix, dh, "rms_mix_bwd", dres=dx1)
    return loss_cols, grad_x, g


BIG = ("w_in", "w_conv_out", "w_sb_out", "w_mem_kv", "w_x_out", "w_out", "w_up", "w_down")
COL_SHARDED = ("w_in", "w_mem_kv", "w_up")
WEIGHTS = ("g_mix", "g_mem", "w_in", "conv_w", "w_conv_out", "w_sb_out", "q_norm_g", "k_norm_g",
           "w_mem_kv", "w_x_out", "w_out", "g_mlp", "w_up", "w_down")


def _pack_small(D, g_mix, g_mem, g_mlp, q_norm_g, k_norm_g, conv_w, last):
    qk = jnp.concatenate([q_norm_g, k_norm_g, jnp.zeros((1, D - 2 * X_HEAD_DIM), F32)], axis=1)
    cw = jnp.pad(conv_w, ((0, 0), (0, D - conv_w.shape[1])))
    return jnp.concatenate([g_mix, g_mem, g_mlp, qk, cw, last], axis=0)


def kernel(x, mem, g_mix, g_mem, w_in, conv_w, w_conv_out, w_sb_out, q_norm_g, k_norm_g, w_mem_kv, w_x_out, w_out, g_mlp, w_up, w_down, loss_target, m_g_mix, m_g_mem, m_w_in, m_conv_w, m_w_conv_out, m_w_sb_out, m_q_norm_g, m_k_norm_g, m_w_mem_kv, m_w_x_out, m_w_out, m_g_mlp, m_w_up, m_w_down, v_g_mix, v_g_mem, v_w_in, v_conv_w, v_w_conv_out, v_w_sb_out, v_q_norm_g, v_k_norm_g, v_w_mem_kv, v_w_x_out, v_w_out, v_g_mlp, v_w_up, v_w_down):
    S, D = x.shape[1], x.shape[2]
    w = dict(g_mix=g_mix, g_mem=g_mem, w_in=w_in, conv_w=conv_w, w_conv_out=w_conv_out, w_sb_out=w_sb_out,
             q_norm_g=q_norm_g, k_norm_g=k_norm_g, w_mem_kv=w_mem_kv, w_x_out=w_x_out, w_out=w_out,
             g_mlp=g_mlp, w_up=w_up, w_down=w_down)
    m = dict(g_mix=m_g_mix, g_mem=m_g_mem, w_in=m_w_in, conv_w=m_conv_w, w_conv_out=m_w_conv_out,
             w_sb_out=m_w_sb_out, q_norm_g=m_q_norm_g, k_norm_g=m_k_norm_g, w_mem_kv=m_w_mem_kv,
             w_x_out=m_w_x_out, w_out=m_w_out, g_mlp=m_g_mlp, w_up=m_w_up, w_down=m_w_down)
    v = dict(g_mix=v_g_mix, g_mem=v_g_mem, w_in=v_w_in, conv_w=v_conv_w, w_conv_out=v_w_conv_out,
             w_sb_out=v_w_sb_out, q_norm_g=v_q_norm_g, k_norm_g=v_k_norm_g, w_mem_kv=v_w_mem_kv,
             w_x_out=v_w_x_out, w_out=v_w_out, g_mlp=v_g_mlp, w_up=v_w_up, w_down=v_w_down)
    chip = 2 * lax.axis_index("x") + lax.axis_index("y")
    cs = conv_w.shape[2]

    place = jnp.stack([chip, lax.axis_index("c")]).astype(jnp.int32)
    mine16 = [cast_into_full(w[k][0], place, "cast_" + k) for k in BIG]
    cw_block = jnp.pad(conv_w[0], ((0, 5), (0, 0)))
    *full16, cw_all = gather_weights(mine16, cw_block, "gather_weights")
    full = {}
    for k, a in zip(BIG, full16):
        full[k] = a if k in COL_SHARDED else a.reshape(-1, a.shape[-1])
    conv_full = jnp.concatenate([cw_all[p, :3] for p in range(N_CHIP)], axis=1)

    loss_cols, grad_x, g = local_step(
        x[0], mem[0], loss_target[0], g_mix, g_mem, q_norm_g, k_norm_g, g_mlp, conv_full,
        full["w_in"], full["w_conv_out"], full["w_sb_out"], full["w_mem_kv"], full["w_x_out"], full["w_out"],
        full["w_up"], full["w_down"])

    small = allreduce_small(
        _pack_small(D, g["g_mix"], g["g_mem"], g["g_mlp"], g["q_norm_g"], g["k_norm_g"], g["conv_w"], loss_cols),
        "allreduce_small")
    loss = (0.5 / D) * jnp.sum(small[7])
    gsum = {"g_mix": small[0:1], "g_mem": small[1:2], "g_mlp": small[2:3],
            "q_norm_g": small[3:4, :X_HEAD_DIM], "k_norm_g": small[3:4, X_HEAD_DIM:2 * X_HEAD_DIM],
            "conv_w": lax.dynamic_slice(small[4:7], (0, chip * cs), (3, cs))[None]}

    g4 = [g[k] if k in COL_SHARDED else g[k].reshape(N_CHIP, -1, g[k].shape[-1]) for k in BIG]
    got = exchange_halves(g4, "exchange_halves")
    pairs = [pair_sum(a, b, place, "pair_sum_" + k) for k, a, b in zip(BIG, g4, got)]
    got = scatter_partials([p[1] for p in pairs], "scatter_partials")
    halves = [chip_sum(p[0], b, place, "chip_sum_" + k) for k, p, b in zip(BIG, pairs, got)]
    for k, a in zip(BIG, join_halves(halves, "join_halves")):
        gsum[k] = a[None]

    small_names = ("g_mix", "g_mem", "g_mlp", "q_norm_g", "k_norm_g", "conv_w")
    zero_row = jnp.zeros((1, D), F32)
    packed = [_pack_small(D, *[t[k] if k != "conv_w" else t[k][0] for k in small_names], zero_row)
              for t in (w, gsum, m, v)]
    sm = adamw(*packed, "adamw_small")
    delta, new_m, new_v = {}, {}, {}
    for t, block in zip((delta, new_m, new_v), sm):
        t["g_mix"], t["g_mem"], t["g_mlp"] = block[0:1], block[1:2], block[2:3]
        t["q_norm_g"], t["k_norm_g"] = block[3:4, :X_HEAD_DIM], block[3:4, X_HEAD_DIM:2 * X_HEAD_DIM]
        t["conv_w"] = block[4:7, :cs][None]
    for k in BIG:
        delta[k], new_m[k], new_v[k] = adamw(w[k], gsum[k], m[k], v[k], "adamw_" + k)

    return (loss, grad_x[None], *[gsum[k] for k in WEIGHTS], *[delta[k] for k in WEIGHTS],
            *[new_m[k] for k in WEIGHTS], *[new_v[k] for k in WEIGHTS])
```

```python
import functools

import jax
import jax.numpy as jnp
from jax import lax
from jax.experimental import pallas as pl
from jax.experimental.pallas import tpu as pltpu

F32 = jnp.float32
BF16 = jnp.bfloat16
EPS = 1e-6
N_CHIP = 4
SB_HEAD_DIM = 64
X_HEAD_DIM = 256
LANES = 128
VMEM_LIMIT = 56 * 1024 * 1024
ADAM_LR, ADAM_B1, ADAM_B2, ADAM_EPS, ADAM_WD, ADAM_STEP = 0.001, 0.9, 0.999, 1e-8, 0.01, 10
MESH = pl.DeviceIdType.MESH


def _params(*sem):
    return pltpu.CompilerParams(dimension_semantics=sem, vmem_limit_bytes=VMEM_LIMIT)


def _tile(n, pref):
    if n <= pref:
        return n
    t = 1 << (pref.bit_length() - 1)
    while n % t:
        t //= 2
    return t


NN = (((1,), (0,)), ((), ()))
NT = (((1,), (1,)), ((), ()))
TN = (((0,), (0,)), ((), ()))


def _dot(a, b, dims):
    return lax.dot_general(a.astype(BF16), b.astype(BF16), dims, preferred_element_type=F32)


def mm_nn_shard(a, g, name, relu2=False, dep=None):
    M, K = a.shape
    _, _, Ns = g.shape
    tm, tn = _tile(M, 1024), _tile(Ns, 512)
    nb = Ns // tn

    def body(a_ref, b_ref, *o_refs):
        o_refs = o_refs[len(deps):]
        acc = _dot(a_ref[...], b_ref[...], NN)
        o_refs[0][...] = acc
        if relu2:
            r = jnp.maximum(acc, 0.0)
            o_refs[1][...] = (r * r).astype(BF16)

    o_spec = pl.BlockSpec((tm, tn), lambda i, j: (i, j))
    shapes = [jax.ShapeDtypeStruct((M, N_CHIP * Ns), F32)]
    specs = [o_spec]
    if relu2:
        shapes.append(jax.ShapeDtypeStruct((M, N_CHIP * Ns), BF16))
        specs.append(o_spec)
    deps = [] if dep is None else [dep]
    out = pl.pallas_call(
        body, grid=(M // tm, N_CHIP * nb), name=name,
        in_specs=[pl.BlockSpec((tm, K), lambda i, j: (i, 0)),
                  pl.BlockSpec((None, K, tn), lambda i, j: (j // nb, 0, j % nb))] + [ANY] * len(deps),
        out_specs=specs, out_shape=shapes, compiler_params=_params("parallel", "parallel"),
    )(a, g, *deps)
    return out if relu2 else out[0]


def mm_nn(a, w, name, res=None, out_dtype=F32):
    M, K = a.shape
    N = w.shape[1]
    tm, tn = _tile(M, 1024 if K <= 2048 else 512), _tile(N, 512)

    def body(a_ref, b_ref, *refs):
        acc = _dot(a_ref[...], b_ref[...], NN)
        if res is not None:
            acc = refs[0][...] + acc
        refs[-1][...] = acc.astype(out_dtype)

    o_spec = pl.BlockSpec((tm, tn), lambda i, j: (i, j))
    ins = [a, w] + ([res] if res is not None else [])
    return pl.pallas_call(
        body, grid=(M // tm, N // tn), name=name,
        in_specs=[pl.BlockSpec((tm, K), lambda i, j: (i, 0)), pl.BlockSpec((K, tn), lambda i, j: (0, j))]
        + ([o_spec] if res is not None else []),
        out_specs=o_spec, out_shape=jax.ShapeDtypeStruct((M, N), out_dtype),
        compiler_params=_params("parallel", "parallel"),
    )(*ins)


def mm_nt(a, w, name, up=None, out_dtype=F32, dep=None):
    M, N = a.shape
    R = w.shape[0]
    tm, tr = _tile(M, 1024), _tile(R, 512)

    def body(a_ref, b_ref, *refs):
        acc = _dot(a_ref[...], b_ref[...], NT)
        if up is not None:
            acc = acc * (2.0 * jnp.maximum(refs[0][...], 0.0))
        refs[-1][...] = acc.astype(out_dtype)

    o_spec = pl.BlockSpec((tm, tr), lambda i, j: (i, j))
    ins = [a, w] + ([up] if up is not None else []) + ([dep] if dep is not None else [])
    return pl.pallas_call(
        body, grid=(M // tm, R // tr), name=name,
        in_specs=[pl.BlockSpec((tm, N), lambda i, j: (i, 0)), pl.BlockSpec((tr, N), lambda i, j: (j, 0))]
        + ([o_spec] if up is not None else []) + ([ANY] if dep is not None else []),
        out_specs=o_spec, out_shape=jax.ShapeDtypeStruct((M, R), out_dtype),
        compiler_params=_params("parallel", "parallel"),
    )(*ins)


def mm_nt_shard(a, g, name, out_dtype=F32):
    M = a.shape[0]
    _, R, Ns = g.shape
    tm, tr, tk = _tile(M, 1024), _tile(R, 512), _tile(Ns, 1024)
    nb = Ns // tk
    nk = N_CHIP * nb

    def body(a_ref, b_ref, o_ref, acc_ref):
        k = pl.program_id(2)

        @pl.when(k == 0)
        def _():
            acc_ref[...] = jnp.zeros_like(acc_ref)

        acc_ref[...] += _dot(a_ref[...], b_ref[...], NT)

        @pl.when(k == nk - 1)
        def _():
            o_ref[...] = acc_ref[...].astype(out_dtype)

    return pl.pallas_call(
        body, grid=(M // tm, R // tr, nk), name=name,
        in_specs=[pl.BlockSpec((tm, tk), lambda i, j, k: (i, k)),
                  pl.BlockSpec((None, tr, tk), lambda i, j, k: (k // nb, j, k % nb))],
        out_specs=pl.BlockSpec((tm, tr), lambda i, j, k: (i, j)),
        out_shape=jax.ShapeDtypeStruct((M, R), out_dtype),
        scratch_shapes=[pltpu.VMEM((tm, tr), F32)],
        compiler_params=_params("parallel", "parallel", "arbitrary"),
    )(a, g)


def mm_tn(a, b, name, shard_out=False):
    S, M = a.shape
    N = b.shape[1]
    Ns = N // N_CHIP if shard_out else N
    tm, tn = _tile(M, 512), _tile(Ns, 512)
    nb = Ns // tn

    def body(a_ref, b_ref, o_ref):
        o_ref[...] = _dot(a_ref[...], b_ref[...], TN)

    if shard_out:
        o_spec = pl.BlockSpec((None, tm, tn), lambda i, j: (j // nb, i, j % nb))
        o_shape = jax.ShapeDtypeStruct((N_CHIP, M, Ns), F32)
    else:
        o_spec = pl.BlockSpec((tm, tn), lambda i, j: (i, j))
        o_shape = jax.ShapeDtypeStruct((M, N), F32)
    return pl.pallas_call(
        body, grid=(M // tm, N // tn), name=name,
        in_specs=[pl.BlockSpec((S, tm), lambda i, j: (0, i)), pl.BlockSpec((S, tn), lambda i, j: (0, j))],
        out_specs=o_spec, out_shape=o_shape, compiler_params=_params("parallel", "parallel"),
    )(a, b)


def rms_fwd(x, g, name):
    S, D = x.shape
    tm = _tile(S, 512)

    def body(x_ref, g_ref, h_ref):
        xv = x_ref[...]
        r = lax.rsqrt(jnp.mean(xv * xv, axis=-1, keepdims=True) + EPS)
        h_ref[...] = ((xv * r) * g_ref[...]).astype(BF16)

    return pl.pallas_call(
        body, grid=(S // tm,), name=name,
        in_specs=[pl.BlockSpec((tm, D), lambda i: (i, 0)), pl.BlockSpec((1, D), lambda i: (0, 0))],
        out_specs=pl.BlockSpec((tm, D), lambda i: (i, 0)),
        out_shape=jax.ShapeDtypeStruct((S, D), BF16), compiler_params=_params("parallel"),
    )(x, g)


def rms_bwd(x, g, dh, name, dres=None, want_dx=True):
    S, D = x.shape
    tm = _tile(S, 512)

    def body(x_ref, g_ref, dh_ref, *refs):
        i = pl.program_id(0)
        xv = x_ref[...]
        r = lax.rsqrt(jnp.mean(xv * xv, axis=-1, keepdims=True) + EPS)
        xn = xv * r
        dhv = dh_ref[...].astype(F32)
        gg_ref = refs[-1]

        @pl.when(i == 0)
        def _():
            gg_ref[...] = jnp.zeros_like(gg_ref)

        gg_ref[...] += jnp.sum(dhv * xn, axis=0, keepdims=True)
        if want_dx:
            dxn = dhv * g_ref[...]
            dx = r * (dxn - xn * jnp.mean(dxn * xn, axis=-1, keepdims=True))
            if dres is not None:
                dx = refs[0][...] + dx
            refs[-2][...] = dx

    row = pl.BlockSpec((tm, D), lambda i: (i, 0))
    vec = pl.BlockSpec((1, D), lambda i: (0, 0))
    ins, in_specs = [x, g, dh], [row, vec, row]
    if dres is not None:
        ins.append(dres)
        in_specs.append(row)
    shapes, specs = [jax.ShapeDtypeStruct((1, D), F32)], [vec]
    if want_dx:
        shapes.insert(0, jax.ShapeDtypeStruct((S, D), F32))
        specs.insert(0, row)
    out = pl.pallas_call(body, grid=(S // tm,), name=name, in_specs=in_specs, out_specs=specs,
                         out_shape=shapes, compiler_params=_params("arbitrary"))(*ins)
    return out if want_dx else out[0]


def _shift_down(u, k, row):
    return jnp.where(row >= k, pltpu.roll(u, k, axis=0), 0.0)


def _shift_up(u, k, row):
    S = u.shape[0]
    return jnp.where(row < S - k, pltpu.roll(u, S - k, axis=0), 0.0)


def _conv_specs(S, D, tc):
    nb = D // tc
    col = lambda o: pl.BlockSpec((S, tc), lambda j, o=o: (0, o * nb + j))
    return col, pl.BlockSpec((3, tc), lambda j: (0, j))


def conv_fwd(proj, conv_w, D, name):
    S = proj.shape[0]
    tc = _tile(D, 256)
    col, wspec = _conv_specs(S, D, tc)

    def body(ch_ref, cb_ref, cc_ref, w_ref, a_ref):
        row = lax.broadcasted_iota(jnp.int32, (S, tc), 0)
        u = cc_ref[...] * ch_ref[...]
        w = w_ref[...]
        cv = w[0:1, :] * _shift_down(u, 2, row) + w[1:2, :] * _shift_down(u, 1, row) + w[2:3, :] * u
        a_ref[...] = (cb_ref[...] * cv).astype(BF16)

    return pl.pallas_call(
        body, grid=(D // tc,), name=name, in_specs=[col(0), col(1), col(2), wspec],
        out_specs=pl.BlockSpec((S, tc), lambda j: (0, j)),
        out_shape=jax.ShapeDtypeStruct((S, D), BF16), compiler_params=_params("parallel"),
    )(proj, proj, proj, conv_w)


def conv_bwd(proj, conv_w, da, D, name):
    S = proj.shape[0]
    tc = _tile(D, 256)
    col, wspec = _conv_specs(S, D, tc)
    blk = pl.BlockSpec((S, tc), lambda j: (0, j))

    def body(ch_ref, cb_ref, cc_ref, w_ref, da_ref, dch_ref, dcb_ref, dcc_ref, gw_ref):
        row = lax.broadcasted_iota(jnp.int32, (S, tc), 0)
        ch, cb, cc, dav = ch_ref[...], cb_ref[...], cc_ref[...], da_ref[...]
        w = w_ref[...]
        u = cc * ch
        u1, u2 = _shift_down(u, 1, row), _shift_down(u, 2, row)
        cv = w[0:1, :] * u2 + w[1:2, :] * u1 + w[2:3, :] * u
        dcb_ref[...] = (dav * cv).astype(BF16)
        dcv = dav * cb
        gw_ref[0:1, :] = jnp.sum(dcv * u2, axis=0, keepdims=True)
        gw_ref[1:2, :] = jnp.sum(dcv * u1, axis=0, keepdims=True)
        gw_ref[2:3, :] = jnp.sum(dcv * u, axis=0, keepdims=True)
        du = w[2:3, :] * dcv + w[1:2, :] * _shift_up(dcv, 1, row) + w[0:1, :] * _shift_up(dcv, 2, row)
        dcc_ref[...] = (du * ch).astype(BF16)
        dch_ref[...] = (du * cc).astype(BF16)

    act = jax.ShapeDtypeStruct((S, D), BF16)
    return pl.pallas_call(
        body, grid=(D // tc,), name=name, in_specs=[col(0), col(1), col(2), wspec, blk],
        out_specs=[blk, blk, blk, wspec], out_shape=[act, act, act, jax.ShapeDtypeStruct((3, D), F32)],
        compiler_params=_params("parallel"),
    )(proj, proj, proj, conv_w, da)


SB_BQ = 256
SB_BK = 128
SB_GROUP = 4


def _sb_consts(bq):
    lane = lax.broadcasted_iota(jnp.int32, (bq, LANES), 1)
    r = lax.broadcasted_iota(jnp.int32, (SB_BK, SB_BK), 0)
    c = lax.broadcasted_iota(jnp.int32, (SB_BK, SB_BK), 1)
    tri_rev = jnp.where(r > c, 1.0, 0.0).astype(BF16)
    tri_fwd = jnp.where(r < c, 1.0, 0.0).astype(BF16)
    return lane, tri_rev, tri_fwd


def _cumsum2(v, tri):
    hi = v.astype(BF16)
    lo = (v - hi.astype(F32)).astype(BF16)
    part = (lax.dot_general(hi, tri, NN, preferred_element_type=F32)
            + lax.dot_general(lo, tri, NN, preferred_element_type=F32))
    return part, jnp.sum(v, axis=1, keepdims=True)


def _sb_logits(z, past):
    sp = jnp.log(1.0 + jnp.exp(-jnp.abs(z)))
    l = jnp.minimum(z, 0.0) - sp
    m = l - z
    if past is not None:
        m = jnp.where(past, m, 0.0)
    return l, m


def _stack_heads(v, lane):
    return jnp.concatenate([jnp.where(lane < SB_HEAD_DIM, v, 0.0), jnp.where(lane >= SB_HEAD_DIM, v, 0.0)],
                           axis=0).astype(BF16)


def _unstack_heads(v, lane):
    bq = v.shape[0] // 2
    return jnp.where(lane < SB_HEAD_DIM, v[:bq], v[bq:])


def _sb_positions(i, bq):
    r = lax.broadcasted_iota(jnp.int32, (2 * bq, SB_BK), 0)
    trow = i * bq + jnp.where(r >= bq, r - bq, r)
    return trow, lax.broadcasted_iota(jnp.int32, (2 * bq, SB_BK), 1)


def _sb_specs(S, D, bq):
    npair = D // LANES
    qspec = pl.BlockSpec((bq, LANES), lambda p, i: (i, 3 * npair + p))
    kspec = pl.BlockSpec((S, LANES), lambda p, i: (0, 4 * npair + p))
    vspec = pl.BlockSpec((S, LANES), lambda p, i: (0, 5 * npair + p))
    return npair, qspec, kspec, vspec


def sb_fwd(proj, D, name):
    S = proj.shape[0]
    bq = SB_BQ
    nd = bq // SB_BK
    npair, qspec, kspec, vspec = _sb_specs(S, D, bq)
    scale = SB_HEAD_DIM ** -0.5

    def body(q_ref, k_ref, v_ref, o_ref, kb_ref, vb_ref):
        i = pl.program_id(1)

        @pl.when(i == 0)
        def _():
            kb_ref[...] = k_ref[...].astype(BF16)
            vb_ref[...] = v_ref[...].astype(BF16)

        lane, tri_rev, _ = _sb_consts(bq)
        qs = _stack_heads(q_ref[...] * scale, lane)
        trow, scol = _sb_positions(i, bq)

        def steps(j0, carry, n, masked):
            ks = [pl.multiple_of((j0 - t) * SB_BK, SB_BK) for t in range(n)]
            past = [(k + scol) < trow if masked else None for k in ks]
            zs = [lax.dot_general(qs, kb_ref[pl.ds(k, SB_BK), :], NT, preferred_element_type=F32) for k in ks]
            lm = [_sb_logits(z, p) for z, p in zip(zs, past)]
            cs = [_cumsum2(m, tri_rev) for _, m in lm]
            c, acc = carry
            for t in range(n):
                a = jnp.exp(lm[t][0] + (cs[t][0] + c))
                if masked:
                    a = jnp.where(past[t], a, 0.0)
                acc = acc + lax.dot_general(a.astype(BF16), vb_ref[pl.ds(ks[t], SB_BK), :], NN,
                                            preferred_element_type=F32)
                c = c + cs[t][1]
            return c, acc

        carry = (jnp.zeros((2 * bq, 1), F32), jnp.zeros((2 * bq, LANES), F32))
        carry = steps(i * nd + nd - 1, carry, nd, True)
        older = i * nd
        groups = older // SB_GROUP
        carry = lax.fori_loop(
            0, groups, lambda t, cr: steps(older - 1 - t * SB_GROUP, cr, SB_GROUP, False), carry)
        rest = older - groups * SB_GROUP
        carry = lax.fori_loop(0, rest // nd, lambda t, cr: steps(rest - 1 - t * nd, cr, nd, False), carry)
        o_ref[...] = _unstack_heads(carry[1], lane)

    return pl.pallas_call(
        body, grid=(npair, S // bq), name=name, in_specs=[qspec, kspec, vspec],
        out_specs=pl.BlockSpec((bq, LANES), lambda p, i: (i, p)),
        out_shape=jax.ShapeDtypeStruct((S, D), F32),
        scratch_shapes=[pltpu.VMEM((S, LANES), BF16), pltpu.VMEM((S, LANES), BF16)],
        compiler_params=_params("parallel", "arbitrary"),
    )(proj, proj, proj)


def sb_bwd(proj, do, D, name):
    S = proj.shape[0]
    bq = SB_BQ
    nd = bq // SB_BK
    nkb = S // SB_BK
    npair, qspec, kspec, vspec = _sb_specs(S, D, bq)
    scale = SB_HEAD_DIM ** -0.5

    def body(q_ref, k_ref, v_ref, do_ref, dq_ref, dk_ref, dv_ref,
             kb_ref, vb_ref, dk_acc, dv_acc, g_scr, b_scr, a_scr):
        i = pl.program_id(1)

        @pl.when(i == 0)
        def _():
            kb_ref[...] = k_ref[...].astype(BF16)
            vb_ref[...] = v_ref[...].astype(BF16)
            dk_acc[...] = jnp.zeros_like(dk_acc)
            dv_acc[...] = jnp.zeros_like(dv_acc)

        lane, tri_rev, tri_fwd = _sb_consts(bq)
        qs = _stack_heads(q_ref[...] * scale, lane)
        dos = _stack_heads(do_ref[...], lane)
        trow, scol = _sb_positions(i, bq)

        def sweep1(j0, c, n, masked):
            js = [j0 - t for t in range(n)]
            ks = [pl.multiple_of(j * SB_BK, SB_BK) for j in js]
            past = [(k + scol) < trow if masked else None for k in ks]
            zs = [lax.dot_general(qs, kb_ref[pl.ds(k, SB_BK), :], NT, preferred_element_type=F32) for k in ks]
            das = [lax.dot_general(dos, vb_ref[pl.ds(k, SB_BK), :], NT, preferred_element_type=F32) for k in ks]
            lm = [_sb_logits(z, p) for z, p in zip(zs, past)]
            cs = [_cumsum2(m, tri_rev) for _, m in lm]
            for t in range(n):
                b_scr[js[t]] = jnp.exp(lm[t][0])
            for t in range(n):
                a = jnp.exp(lm[t][0] + (cs[t][0] + c))
                if masked:
                    a = jnp.where(past[t], a, 0.0)
                g_scr[js[t]] = das[t] * a
                a_scr[js[t]] = a.astype(BF16)
                c = c + cs[t][1]
            return c

        older = i * nd
        groups = older // SB_GROUP
        rest = older - groups * SB_GROUP
        c = jnp.zeros((2 * bq, 1), F32)
        c = sweep1(i * nd + nd - 1, c, nd, True)
        c = lax.fori_loop(0, groups, lambda t, cr: sweep1(older - 1 - t * SB_GROUP, cr, SB_GROUP, False), c)
        lax.fori_loop(0, rest // nd, lambda t, cr: sweep1(rest - 1 - t * nd, cr, nd, False), c)

        def sweep2(j0, carry, n, masked):
            js = [j0 + t for t in range(n)]
            ks = [pl.multiple_of(j * SB_BK, SB_BK) for j in js]
            gv = [g_scr[j] for j in js]
            gs = [_cumsum2(g, tri_fwd) for g in gv]
            pc, dq = carry
            dzs = []
            for t in range(n):
                dz = gv[t] - b_scr[js[t]] * (gv[t] + (gs[t][0] + pc))
                if masked:
                    dz = jnp.where((ks[t] + scol) < trow, dz, 0.0)
                dzs.append(dz.astype(BF16))
                pc = pc + gs[t][1]
            for t in range(n):
                dq = dq + lax.dot_general(dzs[t], kb_ref[pl.ds(ks[t], SB_BK), :], NN, preferred_element_type=F32)
                dk_acc[pl.ds(ks[t], SB_BK), :] += lax.dot_general(dzs[t], qs, TN, preferred_element_type=F32)
                dv_acc[pl.ds(ks[t], SB_BK), :] += lax.dot_general(a_scr[js[t]], dos, TN, preferred_element_type=F32)
            return pc, dq

        carry = (jnp.zeros((2 * bq, 1), F32), jnp.zeros((2 * bq, LANES), F32))
        carry = lax.fori_loop(0, groups, lambda t, cr: sweep2(t * SB_GROUP, cr, SB_GROUP, False), carry)
        carry = lax.fori_loop(
            0, rest // nd, lambda t, cr: sweep2(groups * SB_GROUP + t * nd, cr, nd, False), carry)
        carry = sweep2(i * nd, carry, nd, True)
        dq_ref[...] = (_unstack_heads(carry[1], lane) * scale).astype(BF16)

        @pl.when(i == pl.num_programs(1) - 1)
        def _():
            dk_ref[...] = dk_acc[...].astype(BF16)
            dv_ref[...] = dv_acc[...].astype(BF16)

    full = pl.BlockSpec((S, LANES), lambda p, i: (0, p))
    blk = pl.BlockSpec((bq, LANES), lambda p, i: (i, p))
    act = jax.ShapeDtypeStruct((S, D), BF16)
    return pl.pallas_call(
        body, grid=(npair, S // bq), name=name, in_specs=[qspec, kspec, vspec, blk],
        out_specs=[blk, full, full], out_shape=[act, act, act],
        scratch_shapes=[pltpu.VMEM((S, LANES), BF16), pltpu.VMEM((S, LANES), BF16),
                        pltpu.VMEM((S, LANES), F32), pltpu.VMEM((S, LANES), F32),
                        pltpu.VMEM((nkb, 2 * bq, SB_BK), F32), pltpu.VMEM((nkb, 2 * bq, SB_BK), F32),
                        pltpu.VMEM((nkb, 2 * bq, SB_BK), BF16)],
        compiler_params=_params("parallel", "arbitrary"),
    )(proj, proj, proj, do)


def _rms_rows(v):
    r = lax.rsqrt(jnp.mean(v * v, axis=-1, keepdims=True) + EPS)
    return v * r, r


def _xa_specs(S, D, M, tq):
    nh = D // X_HEAD_DIM
    qspec = pl.BlockSpec((tq, X_HEAD_DIM), lambda h, i: (i, 6 * nh + h))
    kspec = pl.BlockSpec((M, X_HEAD_DIM), lambda h, i: (0, h))
    vspec = pl.BlockSpec((M, X_HEAD_DIM), lambda h, i: (0, nh + h))
    gspec = pl.BlockSpec((1, X_HEAD_DIM), lambda h, i: (0, 0))
    return nh, qspec, kspec, vspec, gspec


def xa_fwd(proj, kv, gq, gk, D, name):
    S, M = proj.shape[0], kv.shape[0]
    tq = _tile(S, 512)
    nh, qspec, kspec, vspec, gspec = _xa_specs(S, D, M, tq)
    scale = X_HEAD_DIM ** -0.5

    def body(q_ref, k_ref, v_ref, gq_ref, gk_ref, o_ref):
        qn = _rms_rows(q_ref[...])[0] * gq_ref[...]
        kn = _rms_rows(k_ref[...])[0] * gk_ref[...]
        s = _dot(qn, kn, NT) * scale
        e = jnp.exp(s - jnp.max(s, axis=-1, keepdims=True))
        p = e / jnp.sum(e, axis=-1, keepdims=True)
        o_ref[...] = _dot(p, v_ref[...], NN)

    return pl.pallas_call(
        body, grid=(nh, S // tq), name=name, in_specs=[qspec, kspec, vspec, gspec, gspec],
        out_specs=pl.BlockSpec((tq, X_HEAD_DIM), lambda h, i: (i, h)),
        out_shape=jax.ShapeDtypeStruct((S, D), F32), compiler_params=_params("parallel", "parallel"),
    )(proj, kv, kv, gq, gk)


def xa_bwd(proj, kv, gq, gk, do, D, name):
    S, M = proj.shape[0], kv.shape[0]
    tq = _tile(S, 512)
    nh, qspec, kspec, vspec, gspec = _xa_specs(S, D, M, tq)
    scale = X_HEAD_DIM ** -0.5

    def body(q_ref, k_ref, v_ref, gq_ref, gk_ref, do_ref, dq_ref, dk_ref, dv_ref, ggq_ref, ggk_ref,
             dkn_acc, dv_acc):
        h, i = pl.program_id(0), pl.program_id(1)

        @pl.when((h == 0) & (i == 0))
        def _():
            ggq_ref[...] = jnp.zeros_like(ggq_ref)
            ggk_ref[...] = jnp.zeros_like(ggk_ref)

        @pl.when(i == 0)
        def _():
            dkn_acc[...] = jnp.zeros_like(dkn_acc)
            dv_acc[...] = jnp.zeros_like(dv_acc)

        gq, gk = gq_ref[...], gk_ref[...]
        qhat, rq = _rms_rows(q_ref[...])
        khat, rk = _rms_rows(k_ref[...])
        qn, kn = qhat * gq, khat * gk
        s = _dot(qn, kn, NT) * scale
        e = jnp.exp(s - jnp.max(s, axis=-1, keepdims=True))
        p = e / jnp.sum(e, axis=-1, keepdims=True)
        dov = do_ref[...]
        dv_acc[...] += _dot(p, dov, TN)
        dp = _dot(dov, v_ref[...], NT)
        ds = (p * (dp - jnp.sum(dp * p, axis=-1, keepdims=True))) * scale
        dqn = _dot(ds, kn, NN)
        dkn_acc[...] += _dot(ds, qn, TN)
        ggq_ref[...] += jnp.sum(dqn * qhat, axis=0, keepdims=True)
        dqh = dqn * gq
        dq_ref[...] = (rq * (dqh - qhat * jnp.mean(dqh * qhat, axis=-1, keepdims=True))).astype(BF16)

        @pl.when(i == pl.num_programs(1) - 1)
        def _():
            dkn = dkn_acc[...]
            ggk_ref[...] += jnp.sum(dkn * khat, axis=0, keepdims=True)
            dkh = dkn * gk
            dk_ref[...] = (rk * (dkh - khat * jnp.mean(dkh * khat, axis=-1, keepdims=True))).astype(BF16)
            dv_ref[...] = dv_acc[...].astype(BF16)

    blk = pl.BlockSpec((tq, X_HEAD_DIM), lambda h, i: (i, h))
    kv_shape = jax.ShapeDtypeStruct((M, 2 * D), BF16)
    gshape = jax.ShapeDtypeStruct((1, X_HEAD_DIM), F32)
    dq, dk, dv, ggq, ggk = pl.pallas_call(
        body, grid=(nh, S // tq), name=name, in_specs=[qspec, kspec, vspec, gspec, gspec, blk],
        out_specs=[blk, kspec, vspec, gspec, gspec],
        out_shape=[jax.ShapeDtypeStruct((S, D), BF16), kv_shape, kv_shape, gshape, gshape],
        scratch_shapes=[pltpu.VMEM((M, X_HEAD_DIM), F32), pltpu.VMEM((M, X_HEAD_DIM), F32)],
        compiler_params=_params("arbitrary", "arbitrary"),
    )(proj, kv, kv, gq, gk, do)
    d_kv = jnp.concatenate([dk[:, :D], dv[:, D:]], axis=1)
    return dq, d_kv, ggq, ggk


def _gate_specs(S, D, tm):
    row = pl.BlockSpec((tm, D), lambda i: (i, 0))
    gate = lambda b: pl.BlockSpec((tm, D), lambda i, b=b: (i, 7 + b))
    return row, gate


def merge_fwd(proj, ys, D, name):
    S = proj.shape[0]
    tm = _tile(S, 256)
    row, gate = _gate_specs(S, D, tm)

    def body(g0, g1, g2, y0, y1, y2, o_ref):
        acc = jax.nn.sigmoid(g0[...]) * y0[...]
        acc = acc + jax.nn.sigmoid(g1[...]) * y1[...]
        acc = acc + jax.nn.sigmoid(g2[...]) * y2[...]
        o_ref[...] = acc.astype(BF16)

    return pl.pallas_call(
        body, grid=(S // tm,), name=name, in_specs=[gate(0), gate(1), gate(2), row, row, row],
        out_specs=row, out_shape=jax.ShapeDtypeStruct((S, D), BF16), compiler_params=_params("parallel"),
    )(proj, proj, proj, *ys)


def merge_bwd(proj, ys, dm, D, name):
    S = proj.shape[0]
    tm = _tile(S, 256)
    row, gate = _gate_specs(S, D, tm)

    def body(g0, g1, g2, y0, y1, y2, dm_ref, d0, d1, d2, dg_ref):
        dmv = dm_ref[...]
        for b, (g_ref, y_ref, d_ref) in enumerate(((g0, y0, d0), (g1, y1, d1), (g2, y2, d2))):
            s = jax.nn.sigmoid(g_ref[...])
            d_ref[...] = (dmv * s).astype(BF16)
            dg_ref[:, b * D:(b + 1) * D] = ((dmv * y_ref[...]) * (s * (1.0 - s))).astype(BF16)

    act = jax.ShapeDtypeStruct((S, D), BF16)
    return pl.pallas_call(
        body, grid=(S // tm,), name=name, in_specs=[gate(0), gate(1), gate(2), row, row, row, row],
        out_specs=[row, row, row, pl.BlockSpec((tm, 3 * D), lambda i: (i, 0))],
        out_shape=[act, act, act, jax.ShapeDtypeStruct((S, 3 * D), BF16)], compiler_params=_params("parallel"),
    )(proj, proj, proj, *ys, dm)


def loss_head(y, target, name):
    S, D = y.shape
    tm = _tile(S, 512)

    def body(y_ref, t_ref, dy_ref, l_ref):
        @pl.when(pl.program_id(0) == 0)
        def _():
            l_ref[...] = jnp.zeros_like(l_ref)

        e = y_ref[...] - t_ref[...]
        dy_ref[...] = e * (1.0 / D)
        l_ref[...] += jnp.sum(e * e, axis=0, keepdims=True)

    row = pl.BlockSpec((tm, D), lambda i: (i, 0))
    vec = pl.BlockSpec((1, D), lambda i: (0, 0))
    return pl.pallas_call(
        body, grid=(S // tm,), name=name, in_specs=[row, row], out_specs=[row, vec],
        out_shape=[jax.ShapeDtypeStruct((S, D), F32), jax.ShapeDtypeStruct((1, D), F32)],
        compiler_params=_params("arbitrary"),
    )(y, target)


def _rows2d(a):
    return a.reshape(-1, a.shape[-1])


def _ew_call(fn, ins, out_dtypes, name):
    R, C = ins[0].shape
    tr = _tile(R, max(8, (1 << 19) // C))
    spec = pl.BlockSpec((tr, C), lambda i: (i, 0))

    def body(*refs):
        outs = fn(*[r[...] for r in refs[:len(ins)]])
        for o_ref, o in zip(refs[len(ins):], outs):
            o_ref[...] = o.astype(o_ref.dtype)

    return pl.pallas_call(
        body, grid=(R // tr,), name=name, in_specs=[spec] * len(ins), out_specs=[spec] * len(out_dtypes),
        out_shape=[jax.ShapeDtypeStruct((R, C), d) for d in out_dtypes], compiler_params=_params("parallel"),
    )(*ins)


def adamw(w, g, m, v, name):
    def fn(w, g, m, v):
        m = ADAM_B1 * m + (1.0 - ADAM_B1) * g
        v = ADAM_B2 * v + (1.0 - ADAM_B2) * (g * g)
        m_hat = m / (1.0 - ADAM_B1 ** ADAM_STEP)
        v_hat = v / (1.0 - ADAM_B2 ** ADAM_STEP)
        return -ADAM_LR * (m_hat / (jnp.sqrt(v_hat) + ADAM_EPS) + ADAM_WD * w), m, v

    shp = w.shape
    outs = _ew_call(fn, [_rows2d(a) for a in (w, g, m, v)], [F32, F32, F32], name)
    return [o.reshape(shp) for o in outs]


def _placed_call(fn, place, grid, ins, in_specs, out_shape, out_specs, name):
    n = len(ins)

    def body(place_ref, *refs):
        outs = fn(*[r[...] for r in refs[:n]])
        for o_ref, o in zip(refs[n:], outs):
            o_ref[...] = o.astype(o_ref.dtype)

    return pl.pallas_call(
        body, name=name, out_shape=out_shape,
        grid_spec=pltpu.PrefetchScalarGridSpec(
            num_scalar_prefetch=1, grid=grid, in_specs=in_specs, out_specs=out_specs),
        compiler_params=_params(*["parallel"] * len(grid)),
    )(place, *ins)


def _row_tile(R, C):
    return _tile(R, max(16, (1 << 19) // C))


def cast_into_full(w, place, name):
    R, C = w.shape
    tr = _row_tile(R, C)
    return _placed_call(
        lambda a: (a,), place, (R // tr,), [w], [pl.BlockSpec((tr, C), lambda i, p: (i, 0))],
        [jax.ShapeDtypeStruct((N_CHIP, R, C), BF16)], [pl.BlockSpec((None, tr, C), lambda i, p: (p[0], i, 0))],
        name)[0]


def pair_sum(g4, got, place, name):
    _, hr, C = got.shape
    tr = _row_tile(hr, C)
    nb = hr // tr
    blk = pl.BlockSpec((None, tr, C), lambda s, i, p: (s, i, 0))
    return _placed_call(
        lambda a, b: (a + b, a + b), place, (N_CHIP, nb), [g4, got],
        [pl.BlockSpec((None, tr, C), lambda s, i, p: (s, p[1] * nb + i, 0)), blk],
        [jax.ShapeDtypeStruct(got.shape, F32), jax.ShapeDtypeStruct(got.shape, BF16)], [blk, blk], name)


def chip_sum(p32, got, place, name):
    _, H, C = p32.shape
    tr = _row_tile(H, C)
    nb = H // tr
    peer = lambda j: pl.BlockSpec((None, tr, C), lambda i, p, j=j: (j, i, 0))
    return _placed_call(
        lambda a, b, c, d: (((a + b.astype(F32)) + c.astype(F32)) + d.astype(F32),), place, (nb,),
        [p32, got, got, got], [pl.BlockSpec((None, tr, C), lambda i, p: (p[0], i, 0)), peer(0), peer(1), peer(2)],
        [jax.ShapeDtypeStruct((2 * H, C), F32)], [pl.BlockSpec((tr, C), lambda i, p: (p[1] * nb + i, 0))],
        name)[0]


ANY = pl.BlockSpec(memory_space=pl.ANY)
CHIP_FLIPS = ((1, 0), (0, 1), (1, 1))


def _place():
    return lax.axis_index("x"), lax.axis_index("y"), lax.axis_index("c")


def _flip(v, f):
    return 1 - v if f else v


def gather_weights(fulls, small, name):
    n = len(fulls)

    def body(*refs):
        small_in = refs[n]
        outs, small_out = refs[n + 1:2 * n + 1], refs[2 * n + 1]
        ici_send, ici_recv, d2d_send, d2d_recv, loc_sem, sm_send, sm_recv = refs[2 * n + 2:]
        x, y, c = _place()
        mine = 2 * x + y
        sibling = (x, y, 1 - c)
        chips = [(_flip(x, fx), _flip(y, fy)) for fx, fy in CHIP_FLIPS]

        local = [pltpu.make_async_copy(small_in, small_out.at[mine], loc_sem)]
        local[0].start()

        def half(a, chip_idx, core):
            hr = outs[a].shape[1] // 2
            return outs[a].at[chip_idx, pl.ds(core * hr, hr), :]

        sends = []
        for j, (px, py) in enumerate(chips):
            cp = pltpu.make_async_remote_copy(
                src_ref=small_in, dst_ref=small_out.at[mine], send_sem=sm_send.at[j], recv_sem=sm_recv.at[j],
                device_id=(px, py, c), device_id_type=MESH)
            cp.start()
            sends.append(cp)
        for a in range(n):
            for j, (px, py) in enumerate(chips):
                cp = pltpu.make_async_remote_copy(
                    src_ref=half(a, mine, c), dst_ref=half(a, mine, c),
                    send_sem=ici_send.at[a, j], recv_sem=ici_recv.at[a, j],
                    device_id=(px, py, c), device_id_type=MESH)
                cp.start()
                sends.append(cp)
        for a in range(n):
            for j, (px, py) in enumerate(chips):
                src = half(a, 2 * px + py, c)
                pltpu.make_async_remote_copy(
                    src_ref=src, dst_ref=src, send_sem=ici_send.at[a, j], recv_sem=ici_recv.at[a, j],
                    device_id=(px, py, c), device_id_type=MESH).wait_recv()
                cp = pltpu.make_async_remote_copy(
                    src_ref=src, dst_ref=src, send_sem=d2d_send.at[a, j], recv_sem=d2d_recv.at[a, j],
                    device_id=sibling, device_id_type=MESH)
                cp.start()
                sends.append(cp)
        for a in range(n):
            for j, (px, py) in enumerate(chips):
                dst = half(a, 2 * px + py, 1 - c)
                pltpu.make_async_remote_copy(
                    src_ref=dst, dst_ref=dst, send_sem=d2d_send.at[a, j], recv_sem=d2d_recv.at[a, j],
                    device_id=sibling, device_id_type=MESH).wait_recv()
        for j, (px, py) in enumerate(chips):
            dst = small_out.at[2 * px + py]
            pltpu.make_async_remote_copy(
                src_ref=dst, dst_ref=dst, send_sem=sm_send.at[j], recv_sem=sm_recv.at[j],
                device_id=(px, py, c), device_id_type=MESH).wait_recv()
        for cp in sends:
            cp.wait_send()
        for cp in local:
            cp.wait()

    out_shape = [jax.ShapeDtypeStruct(f.shape, f.dtype) for f in fulls]
    out_shape.append(jax.ShapeDtypeStruct((N_CHIP,) + small.shape, small.dtype))
    dma = pltpu.SemaphoreType.DMA
    return pl.pallas_call(
        body, name=name, in_specs=[ANY] * (n + 1), out_specs=[ANY] * (n + 1), out_shape=out_shape,
        input_output_aliases={a: a for a in range(n)},
        scratch_shapes=[dma((n, 3)), dma((n, 3)), dma((n, 3)), dma((n, 3)), dma, dma((3,)), dma((3,))],
    )(*fulls, small)


def exchange_halves(grads, name):
    n = len(grads)

    def body(*refs):
        ins, got = refs[:n], refs[n:2 * n]
        send_sem, recv_sem = refs[2 * n:]
        x, y, c = _place()
        copies = []
        for a in range(n):
            hr = ins[a].shape[1] // 2
            cp = pltpu.make_async_remote_copy(
                src_ref=ins[a].at[:, pl.ds((1 - c) * hr, hr), :], dst_ref=got[a],
                send_sem=send_sem.at[a], recv_sem=recv_sem.at[a], device_id=(x, y, 1 - c), device_id_type=MESH)
            cp.start()
            copies.append(cp)
        for cp in copies:
            cp.wait()

    dma = pltpu.SemaphoreType.DMA
    return pl.pallas_call(
        body, name=name, in_specs=[ANY] * n, out_specs=[ANY] * n,
        out_shape=[jax.ShapeDtypeStruct((N_CHIP, g.shape[1] // 2, g.shape[2]), g.dtype) for g in grads],
        scratch_shapes=[dma((n,)), dma((n,))],
    )(*grads)


def scatter_partials(parts16, name):
    n = len(parts16)

    def body(*refs):
        p16, got = refs[:n], refs[n:2 * n]
        send_sem, recv_sem = refs[2 * n:]
        x, y, c = _place()
        chips = [(_flip(x, fx), _flip(y, fy)) for fx, fy in CHIP_FLIPS]
        copies = []
        for a in range(n):
            for j, (px, py) in enumerate(chips):
                cp = pltpu.make_async_remote_copy(
                    src_ref=p16[a].at[2 * px + py], dst_ref=got[a].at[j],
                    send_sem=send_sem.at[a, j], recv_sem=recv_sem.at[a, j],
                    device_id=(px, py, c), device_id_type=MESH)
                cp.start()
                copies.append(cp)
        for cp in copies:
            cp.wait()

    dma = pltpu.SemaphoreType.DMA
    return pl.pallas_call(
        body, name=name, in_specs=[ANY] * n, out_specs=[ANY] * n,
        out_shape=[jax.ShapeDtypeStruct((3,) + p.shape[1:], BF16) for p in parts16],
        scratch_shapes=[dma((n, 3)), dma((n, 3))],
    )(*parts16)


def join_halves(fulls, name):
    n = len(fulls)

    def body(*refs):
        outs = refs[n:2 * n]
        send_sem, recv_sem = refs[2 * n:]
        x, y, c = _place()
        copies = []
        for a in range(n):
            hr = outs[a].shape[0] // 2
            half = outs[a].at[pl.ds(c * hr, hr), :]
            cp = pltpu.make_async_remote_copy(
                src_ref=half, dst_ref=half, send_sem=send_sem.at[a], recv_sem=recv_sem.at[a],
                device_id=(x, y, 1 - c), device_id_type=MESH)
            cp.start()
            copies.append(cp)
        for a, cp in enumerate(copies):
            hr = outs[a].shape[0] // 2
            theirs = outs[a].at[pl.ds((1 - c) * hr, hr), :]
            cp.wait_send()
            pltpu.make_async_remote_copy(
                src_ref=theirs, dst_ref=theirs, send_sem=send_sem.at[a], recv_sem=recv_sem.at[a],
                device_id=(x, y, 1 - c), device_id_type=MESH).wait_recv()

    dma = pltpu.SemaphoreType.DMA
    return pl.pallas_call(
        body, name=name, in_specs=[ANY] * n, out_specs=[ANY] * n,
        out_shape=[jax.ShapeDtypeStruct(f.shape, F32) for f in fulls],
        input_output_aliases={a: a for a in range(n)},
        scratch_shapes=[dma((n,)), dma((n,))],
    )(*fulls)


HBM = pl.BlockSpec(memory_space=pltpu.HBM)
SEM = pl.BlockSpec(memory_space=pltpu.SEMAPHORE)
EFFECT = pltpu.SideEffectType.DATAFLOW_SIDE_EFFECTING


def _in_hbm(a):
    return pltpu.with_memory_space_constraint(a, pltpu.HBM)


def _gather_half(ref, chip_idx, core):
    hr = ref.shape[1] // 2
    return ref.at[chip_idx, pl.ds(core * hr, hr), :]


def gather_start(fulls, dep, name):
    n = len(fulls)

    def body(*refs):
        send_sem, recv_sem = refs[n + 1], refs[n + 2]
        outs, token = refs[n + 3:2 * n + 3], refs[2 * n + 3]
        x, y, c = _place()
        mine = 2 * x + y
        for a in range(n):
            for j, (fx, fy) in enumerate(CHIP_FLIPS):
                half = _gather_half(outs[a], mine, c)
                pltpu.make_async_remote_copy(
                    src_ref=half, dst_ref=half, send_sem=send_sem.at[3 * a + j], recv_sem=recv_sem.at[3 * a + j],
                    device_id=(_flip(x, fx), _flip(y, fy), c), device_id_type=MESH).start()
        token[...] = jnp.zeros_like(token)

    dma = pltpu.SemaphoreType.DMA
    out = pl.pallas_call(
        body, name=name,
        out_shape=(dma((3 * n,)), dma((3 * n,)), *[pltpu.HBM(f.shape, f.dtype) for f in fulls],
                   jax.ShapeDtypeStruct((8, LANES), F32)),
        in_specs=[HBM] * n + [ANY], out_specs=(SEM, SEM, *[HBM] * n, pl.BlockSpec(memory_space=pltpu.VMEM)),
        input_output_aliases={a: 2 + a for a in range(n)},
        compiler_params=pltpu.CompilerParams(has_side_effects=EFFECT),
    )(*[_in_hbm(f) for f in fulls], dep)
    return out[0], out[1], out[2:2 + n], out[2 + n]


def gather_wait(send_sem, recv_sem, fulls, after, name):
    n = len(fulls)

    def body(*refs):
        outs = refs[n + 3:]
        send_sem, recv_sem = refs[n], refs[n + 1]
        x, y, c = _place()
        mine = 2 * x + y
        for a in range(n):
            for j, (fx, fy) in enumerate(CHIP_FLIPS):
                cp = pltpu.make_async_remote_copy(
                    src_ref=_gather_half(outs[a], mine, c),
                    dst_ref=_gather_half(outs[a], 2 * _flip(x, fx) + _flip(y, fy), c),
                    send_sem=send_sem.at[3 * a + j], recv_sem=recv_sem.at[3 * a + j],
                    device_id=(_flip(x, fx), _flip(y, fy), c), device_id_type=MESH)
                cp.wait_send()
                cp.wait_recv()

    return pl.pallas_call(
        body, name=name, out_shape=tuple(pltpu.HBM(f.shape, f.dtype) for f in fulls),
        in_specs=[HBM] * n + [SEM, SEM, ANY], out_specs=tuple([HBM] * n),
        input_output_aliases={a: a for a in range(n)},
        compiler_params=pltpu.CompilerParams(has_side_effects=EFFECT),
    )(*fulls, send_sem, recv_sem, after)


def gather_forward(fulls, name):
    n = len(fulls)

    def body(*refs):
        outs = refs[n:2 * n]
        send_sem, recv_sem = refs[2 * n:]
        x, y, c = _place()
        chips = [2 * _flip(x, fx) + _flip(y, fy) for fx, fy in CHIP_FLIPS]
        copies = []
        for a in range(n):
            for j, q in enumerate(chips):
                src = _gather_half(outs[a], q, c)
                cp = pltpu.make_async_remote_copy(
                    src_ref=src, dst_ref=src, send_sem=send_sem.at[a, j], recv_sem=recv_sem.at[a, j],
                    device_id=(x, y, 1 - c), device_id_type=MESH)
                cp.start()
                copies.append(cp)
        for a in range(n):
            for j, q in enumerate(chips):
                dst = _gather_half(outs[a], q, 1 - c)
                copies[3 * a + j].wait_send()
                pltpu.make_async_remote_copy(
                    src_ref=dst, dst_ref=dst, send_sem=send_sem.at[a, j], recv_sem=recv_sem.at[a, j],
                    device_id=(x, y, 1 - c), device_id_type=MESH).wait_recv()

    dma = pltpu.SemaphoreType.DMA
    return pl.pallas_call(
        body, name=name, in_specs=[ANY] * n, out_specs=[ANY] * n,
        out_shape=[jax.ShapeDtypeStruct(f.shape, f.dtype) for f in fulls],
        input_output_aliases={a: a for a in range(n)},
        scratch_shapes=[dma((n, 3)), dma((n, 3))],
    )(*fulls)


def scatter_start(parts16, dep, name):
    n = len(parts16)

    def body(*refs):
        send_sem, recv_sem = refs[2 * n + 1], refs[2 * n + 2]
        p16, land = refs[2 * n + 3:3 * n + 3], refs[3 * n + 3:4 * n + 3]
        token = refs[4 * n + 3]
        x, y, c = _place()
        for a in range(n):
            for j, (fx, fy) in enumerate(CHIP_FLIPS):
                px, py = _flip(x, fx), _flip(y, fy)
                pltpu.make_async_remote_copy(
                    src_ref=p16[a].at[2 * px + py], dst_ref=land[a].at[j],
                    send_sem=send_sem.at[3 * a + j], recv_sem=recv_sem.at[3 * a + j],
                    device_id=(px, py, c), device_id_type=MESH).start()
        token[...] = jnp.zeros_like(token)

    dma = pltpu.SemaphoreType.DMA
    lands = [lax.empty((3,) + p.shape[1:], BF16) for p in parts16]
    out = pl.pallas_call(
        body, name=name,
        out_shape=(dma((3 * n,)), dma((3 * n,)), *[pltpu.HBM(p.shape, p.dtype) for p in parts16],
                   *[pltpu.HBM(l.shape, l.dtype) for l in lands], jax.ShapeDtypeStruct((8, LANES), F32)),
        in_specs=[HBM] * (2 * n) + [ANY],
        out_specs=(SEM, SEM, *[HBM] * (2 * n), pl.BlockSpec(memory_space=pltpu.VMEM)),
        input_output_aliases={a: 2 + a for a in range(2 * n)},
        compiler_params=pltpu.CompilerParams(has_side_effects=EFFECT),
    )(*[_in_hbm(p) for p in parts16], *[_in_hbm(l) for l in lands], dep)
    return out[0], out[1], out[2:2 + n], out[2 + n:2 + 2 * n], out[2 + 2 * n]


def scatter_wait(send_sem, recv_sem, parts16, lands, after, name):
    n = len(parts16)

    def body(*refs):
        send_sem, recv_sem = refs[2 * n], refs[2 * n + 1]
        p16, land = refs[2 * n + 3:3 * n + 3], refs[3 * n + 3:4 * n + 3]
        x, y, c = _place()
        for a in range(n):
            for j, (fx, fy) in enumerate(CHIP_FLIPS):
                px, py = _flip(x, fx), _flip(y, fy)
                cp = pltpu.make_async_remote_copy(
                    src_ref=p16[a].at[2 * px + py], dst_ref=land[a].at[j],
                    send_sem=send_sem.at[3 * a + j], recv_sem=recv_sem.at[3 * a + j],
                    device_id=(px, py, c), device_id_type=MESH)
                cp.wait_send()
                cp.wait_recv()

    out = pl.pallas_call(
        body, name=name,
        out_shape=tuple(pltpu.HBM(p.shape, p.dtype) for p in list(parts16) + list(lands)),
        in_specs=[HBM] * (2 * n) + [SEM, SEM, ANY], out_specs=tuple([HBM] * (2 * n)),
        input_output_aliases={a: a for a in range(2 * n)},
        compiler_params=pltpu.CompilerParams(has_side_effects=EFFECT),
    )(*parts16, *lands, send_sem, recv_sem, after)
    return out[n:]


def allreduce_small(block, name):
    R, C = block.shape

    def body(in_ref, out_ref, slots, send_sem, recv_sem):
        x, y, c = _place()
        me = 4 * x + 2 * y + c
        slots[me] = in_ref[...]
        copies = []
        for r in range(1, 8):
            fx, fy, fc = (r >> 2) & 1, (r >> 1) & 1, r & 1
            cp = pltpu.make_async_remote_copy(
                src_ref=in_ref, dst_ref=slots.at[me], send_sem=send_sem.at[r - 1], recv_sem=recv_sem.at[r - 1],
                device_id=(_flip(x, fx), _flip(y, fy), _flip(c, fc)), device_id_type=MESH)
            cp.start()
            copies.append(cp)
        for cp in copies:
            cp.wait()
        acc = slots[0]
        for d in range(1, 8):
            acc = acc + slots[d]
        out_ref[...] = acc

    vm = pl.BlockSpec(memory_space=pltpu.VMEM)
    dma = pltpu.SemaphoreType.DMA
    return pl.pallas_call(
        body, name=name, in_specs=[vm], out_specs=vm, out_shape=jax.ShapeDtypeStruct((R, C), F32),
        scratch_shapes=[pltpu.VMEM((8, R, C), F32), dma((7,)), dma((7,))],
    )(block)


def local_step(x, mem, target, g_mix, g_mem, q_norm_g, k_norm_g, g_mlp, conv_w, w_in, w_in_dep,
               rest_weights, early_grads):
    S, D = x.shape
    h = rms_fwd(x, g_mix, "rms_mix")
    proj = mm_nn_shard(h, w_in, "proj", dep=w_in_dep)
    a_conv = conv_fwd(proj, conv_w, D, "conv_fwd")
    o_sb = sb_fwd(proj, D, "sb_fwd")
    w_conv_out, w_sb_out, w_mem_kv, w_x_out, w_out, w_up, w_down = rest_weights(o_sb)
    mem_n = rms_fwd(mem, g_mem, "rms_mem")
    kv = mm_nn_shard(mem_n, w_mem_kv, "kv")
    o_x = xa_fwd(proj, kv, q_norm_g, k_norm_g, D, "xa_fwd")
    ys = [mm_nn(a_conv, w_conv_out, "y_conv"), mm_nn(o_sb, w_sb_out, "y_sb"), mm_nn(o_x, w_x_out, "y_x")]
    merged = merge_fwd(proj, ys, D, "merge_fwd")
    x1 = mm_nn(merged, w_out, "x1", res=x)
    h2 = rms_fwd(x1, g_mlp, "rms_mlp")
    up, act = mm_nn_shard(h2, w_up, "up", relu2=True)
    x2 = mm_nn(act, w_down, "x2", res=x1)
    dy, loss_cols = loss_head(x2, target, "loss_head")
    d_up = mm_nt(dy, w_down, "d_up", up=up, out_dtype=BF16)
    g = {"w_down": mm_tn(act, dy, "g_w_down")}
    g["w_up"] = mm_tn(h2, d_up, "g_w_up", shard_out=True)
    dh2 = mm_nt_shard(d_up, w_up, "dh2")
    dx1, g["g_mlp"] = rms_bwd(x1, g_mlp, dh2, "rms_mlp_bwd", dres=dy)
    g["w_out"] = mm_tn(merged, dx1, "g_w_out")
    dm = mm_nt(dx1, w_out, "d_merged")
    dy_c, dy_s, dy_x, d_gate = merge_bwd(proj, ys, dm, D, "merge_bwd")
    g["w_conv_out"] = mm_tn(a_conv, dy_c, "g_w_conv_out")
    g["w_sb_out"] = mm_tn(o_sb, dy_s, "g_w_sb_out")
    g["w_x_out"] = mm_tn(o_x, dy_x, "g_w_x_out")
    d_xq, d_kv, g["q_norm_g"], g["k_norm_g"] = xa_bwd(
        proj, kv, q_norm_g, k_norm_g, mm_nt(dy_x, w_x_out, "d_o_x"), D, "xa_bwd")
    g["w_mem_kv"] = mm_tn(mem_n, d_kv, "g_w_mem_kv", shard_out=True)
    g["g_mem"] = rms_bwd(mem, g_mem, mm_nt_shard(d_kv, w_mem_kv, "d_mem_n"), "rms_mem_bwd", want_dx=False)
    dep = early_grads(g)
    d_a_conv = mm_nt(dy_c, w_conv_out, "d_a_conv", dep=dep)
    d_ch, d_cb, d_cc, g["conv_w"] = conv_bwd(proj, conv_w, d_a_conv, D, "conv_bwd")
    dq, dk, dv = sb_bwd(proj, mm_nt(dy_s, w_sb_out, "d_o_sb", dep=dep), D, "sb_bwd")
    d_proj = jnp.concatenate([d_ch, d_cb, d_cc, dq, dk, dv, d_xq, d_gate], axis=1)
    g["w_in"] = mm_tn(h, d_proj, "g_w_in", shard_out=True)
    dh = mm_nt_shard(d_proj, w_in, "dh")
    grad_x, g["g_mix"] = rms_bwd(x, g_mix, dh, "rms_mix_bwd", dres=dx1)
    return loss_cols, grad_x, g


BIG = ("w_in", "w_conv_out", "w_sb_out", "w_mem_kv", "w_x_out", "w_out", "w_up", "w_down")
REST = BIG[1:]
COL_SHARDED = ("w_in", "w_mem_kv", "w_up")
WEIGHTS = ("g_mix", "g_mem", "w_in", "conv_w", "w_conv_out", "w_sb_out", "q_norm_g", "k_norm_g",
           "w_mem_kv", "w_x_out", "w_out", "g_mlp", "w_up", "w_down")


def _pack_small(D, g_mix, g_mem, g_mlp, q_norm_g, k_norm_g, conv_w, last):
    qk = jnp.concatenate([q_norm_g, k_norm_g, jnp.zeros((1, D - 2 * X_HEAD_DIM), F32)], axis=1)
    cw = jnp.pad(conv_w, ((0, 0), (0, D - conv_w.shape[1])))
    return jnp.concatenate([g_mix, g_mem, g_mlp, qk, cw, last], axis=0)


def kernel(x, mem, g_mix, g_mem, w_in, conv_w, w_conv_out, w_sb_out, q_norm_g, k_norm_g, w_mem_kv, w_x_out, w_out, g_mlp, w_up, w_down, loss_target, m_g_mix, m_g_mem, m_w_in, m_conv_w, m_w_conv_out, m_w_sb_out, m_q_norm_g, m_k_norm_g, m_w_mem_kv, m_w_x_out, m_w_out, m_g_mlp, m_w_up, m_w_down, v_g_mix, v_g_mem, v_w_in, v_conv_w, v_w_conv_out, v_w_sb_out, v_q_norm_g, v_k_norm_g, v_w_mem_kv, v_w_x_out, v_w_out, v_g_mlp, v_w_up, v_w_down):
    S, D = x.shape[1], x.shape[2]
    w = dict(g_mix=g_mix, g_mem=g_mem, w_in=w_in, conv_w=conv_w, w_conv_out=w_conv_out, w_sb_out=w_sb_out,
             q_norm_g=q_norm_g, k_norm_g=k_norm_g, w_mem_kv=w_mem_kv, w_x_out=w_x_out, w_out=w_out,
             g_mlp=g_mlp, w_up=w_up, w_down=w_down)
    m = dict(g_mix=m_g_mix, g_mem=m_g_mem, w_in=m_w_in, conv_w=m_conv_w, w_conv_out=m_w_conv_out,
             w_sb_out=m_w_sb_out, q_norm_g=m_q_norm_g, k_norm_g=m_k_norm_g, w_mem_kv=m_w_mem_kv,
             w_x_out=m_w_x_out, w_out=m_w_out, g_mlp=m_g_mlp, w_up=m_w_up, w_down=m_w_down)
    v = dict(g_mix=v_g_mix, g_mem=v_g_mem, w_in=v_w_in, conv_w=v_conv_w, w_conv_out=v_w_conv_out,
             w_sb_out=v_w_sb_out, q_norm_g=v_q_norm_g, k_norm_g=v_k_norm_g, w_mem_kv=v_w_mem_kv,
             w_x_out=v_w_x_out, w_out=v_w_out, g_mlp=v_g_mlp, w_up=v_w_up, w_down=v_w_down)
    chip = 2 * lax.axis_index("x") + lax.axis_index("y")
    cs = conv_w.shape[2]

    place = jnp.stack([chip, lax.axis_index("c")]).astype(jnp.int32)
    mine16 = [cast_into_full(w[k][0], place, "cast_" + k) for k in BIG]
    cw_block = jnp.pad(conv_w[0], ((0, 5), (0, 0)))
    w_in_full, cw_all = gather_weights(mine16[:1], cw_block, "gather_weights")
    conv_full = jnp.concatenate([cw_all[p, :3] for p in range(N_CHIP)], axis=1)
    g_send, g_recv, flying, g_token = gather_start(mine16[1:], w_in_full, "gather_rest_start")

    def layout(k, a):
        return a if k in COL_SHARDED else a.reshape(-1, a.shape[-1])

    def rest_weights(after):
        landed = gather_wait(g_send, g_recv, flying, after, "gather_rest_wait")
        return [layout(k, a) for k, a in zip(REST, gather_forward(landed, "gather_rest_forward"))]

    def reduce_start(names, g, dep):
        g4 = [g[k] if k in COL_SHARDED else g[k].reshape(N_CHIP, -1, g[k].shape[-1]) for k in names]
        got = exchange_halves(g4, "exchange_halves_" + names[0])
        pairs = [pair_sum(a, b, place, "pair_sum_" + k) for k, a, b in zip(names, g4, got)]
        return pairs, scatter_start([p[1] for p in pairs], dep, "scatter_start_" + names[0])

    early = {}

    def early_grads(g):
        early["pairs"], early["fly"] = reduce_start(REST, g, g["g_mem"])
        return early["fly"][4]

    loss_cols, grad_x, g = local_step(
        x[0], mem[0], loss_target[0], g_mix, g_mem, q_norm_g, k_norm_g, g_mlp, conv_full,
        w_in_full, g_token, rest_weights, early_grads)

    small = allreduce_small(
        _pack_small(D, g["g_mix"], g["g_mem"], g["g_mlp"], g["q_norm_g"], g["k_norm_g"], g["conv_w"], loss_cols),
        "allreduce_small")
    loss = (0.5 / D) * jnp.sum(small[7])
    gsum = {"g_mix": small[0:1], "g_mem": small[1:2], "g_mlp": small[2:3],
            "q_norm_g": small[3:4, :X_HEAD_DIM], "k_norm_g": small[3:4, X_HEAD_DIM:2 * X_HEAD_DIM],
            "conv_w": lax.dynamic_slice(small[4:7], (0, chip * cs), (3, cs))[None]}

    send, recv, flying16, lands, _ = early["fly"]
    got_rest = scatter_wait(send, recv, flying16, lands, g["w_in"], "scatter_wait_rest")
    got_in = exchange_halves([g["w_in"]], "exchange_halves_w_in")[0]
    pair_in = pair_sum(g["w_in"], got_in, place, "pair_sum_w_in")
    pairs = [pair_in] + early["pairs"]
    got = list(scatter_partials([pair_in[1]], "scatter_partials_w_in")) + list(got_rest)
    halves = [chip_sum(p[0], b, place, "chip_sum_" + k) for k, p, b in zip(BIG, pairs, got)]
    for k, a in zip(BIG, join_halves(halves, "join_halves")):
        gsum[k] = a[None]

    small_names = ("g_mix", "g_mem", "g_mlp", "q_norm_g", "k_norm_g", "conv_w")
    zero_row = jnp.zeros((1, D), F32)
    packed = [_pack_small(D, *[t[k] if k != "conv_w" else t[k][0] for k in small_names], zero_row)
              for t in (w, gsum, m, v)]
    sm = adamw(*packed, "adamw_small")
    delta, new_m, new_v = {}, {}, {}
    for t, block in zip((delta, new_m, new_v), sm):
        t["g_mix"], t["g_mem"], t["g_mlp"] = block[0:1], block[1:2], block[2:3]
        t["q_norm_g"], t["k_norm_g"] = block[3:4, :X_HEAD_DIM], block[3:4, X_HEAD_DIM:2 * X_HEAD_DIM]
        t["conv_w"] = block[4:7, :cs][None]
    for k in BIG:
        delta[k], new_m[k], new_v[k] = adamw(w[k], gsum[k], m[k], v[k], "adamw_" + k)

    return (loss, grad_x[None], *[gsum[k] for k in WEIGHTS], *[delta[k] for k in WEIGHTS],
            *[new_m[k] for k in WEIGHTS], *[new_v[k] for k in WEIGHTS])
```

```python
import functools

import jax
import jax.numpy as jnp
from jax import lax
from jax.experimental import pallas as pl
from jax.experimental.pallas import tpu as pltpu

F32 = jnp.float32
BF16 = jnp.bfloat16
EPS = 1e-6
N_CHIP = 4
SB_HEAD_DIM = 64
X_HEAD_DIM = 256
LANES = 128
VMEM_LIMIT = 56 * 1024 * 1024
ADAM_LR, ADAM_B1, ADAM_B2, ADAM_EPS, ADAM_WD, ADAM_STEP = 0.001, 0.9, 0.999, 1e-8, 0.01, 10
MESH = pl.DeviceIdType.MESH


def _params(*sem):
    return pltpu.CompilerParams(dimension_semantics=sem, vmem_limit_bytes=VMEM_LIMIT)


def _tile(n, pref):
    if n <= pref:
        return n
    t = 1 << (pref.bit_length() - 1)
    while n % t:
        t //= 2
    return t


NN = (((1,), (0,)), ((), ()))
NT = (((1,), (1,)), ((), ()))
TN = (((0,), (0,)), ((), ()))


def _dot(a, b, dims):
    return lax.dot_general(a.astype(BF16), b.astype(BF16), dims, preferred_element_type=F32)


def mm_nn_shard(a, g, name, relu2=False, dep=None):
    M, K = a.shape
    _, _, Ns = g.shape
    tm, tn = _tile(M, 2048), _tile(Ns, 512)
    nb = Ns // tn

    def body(a_ref, b_ref, *o_refs):
        o_refs = o_refs[len(deps):]
        acc = _dot(a_ref[...], b_ref[...], NN)
        o_refs[0][...] = acc
        if relu2:
            r = jnp.maximum(acc, 0.0)
            o_refs[1][...] = (r * r).astype(BF16)

    o_spec = pl.BlockSpec((tm, tn), lambda i, j: (i, j))
    shapes = [jax.ShapeDtypeStruct((M, N_CHIP * Ns), F32)]
    specs = [o_spec]
    if relu2:
        shapes.append(jax.ShapeDtypeStruct((M, N_CHIP * Ns), BF16))
        specs.append(o_spec)
    deps = [] if dep is None else [dep]
    out = pl.pallas_call(
        body, grid=(M // tm, N_CHIP * nb), name=name,
        in_specs=[pl.BlockSpec((tm, K), lambda i, j: (i, 0)),
                  pl.BlockSpec((None, K, tn), lambda i, j: (j // nb, 0, j % nb))] + [ANY] * len(deps),
        out_specs=specs, out_shape=shapes, compiler_params=_params("parallel", "parallel"),
    )(a, g, *deps)
    return out if relu2 else out[0]


def mm_nn(a, w, name, res=None, out_dtype=F32):
    M, K = a.shape
    N = w.shape[1]
    tm, tn = _tile(M, 2048 if K <= 2048 else 1024), _tile(N, 512)

    def body(a_ref, b_ref, *refs):
        acc = _dot(a_ref[...], b_ref[...], NN)
        if res is not None:
            acc = refs[0][...] + acc
        refs[-1][...] = acc.astype(out_dtype)

    o_spec = pl.BlockSpec((tm, tn), lambda i, j: (i, j))
    ins = [a, w] + ([res] if res is not None else [])
    return pl.pallas_call(
        body, grid=(M // tm, N // tn), name=name,
        in_specs=[pl.BlockSpec((tm, K), lambda i, j: (i, 0)), pl.BlockSpec((K, tn), lambda i, j: (0, j))]
        + ([o_spec] if res is not None else []),
        out_specs=o_spec, out_shape=jax.ShapeDtypeStruct((M, N), out_dtype),
        compiler_params=_params("parallel", "parallel"),
    )(*ins)


def mm_nt(a, w, name, up=None, out_dtype=F32, dep=None):
    M, N = a.shape
    R = w.shape[0]
    tm, tr = _tile(M, 2048), _tile(R, 512)

    def body(a_ref, b_ref, *refs):
        acc = _dot(a_ref[...], b_ref[...], NT)
        if up is not None:
            acc = acc * (2.0 * jnp.maximum(refs[0][...], 0.0))
        refs[-1][...] = acc.astype(out_dtype)

    o_spec = pl.BlockSpec((tm, tr), lambda i, j: (i, j))
    ins = [a, w] + ([up] if up is not None else []) + ([dep] if dep is not None else [])
    return pl.pallas_call(
        body, grid=(M // tm, R // tr), name=name,
        in_specs=[pl.BlockSpec((tm, N), lambda i, j: (i, 0)), pl.BlockSpec((tr, N), lambda i, j: (j, 0))]
        + ([o_spec] if up is not None else []) + ([ANY] if dep is not None else []),
        out_specs=o_spec, out_shape=jax.ShapeDtypeStruct((M, R), out_dtype),
        compiler_params=_params("parallel", "parallel"),
    )(*ins)


def mm_nt_shard(a, g, name, out_dtype=F32):
    M = a.shape[0]
    _, R, Ns = g.shape
    tm, tr, tk = _tile(M, 1024), _tile(R, 1024), _tile(Ns, 2560)
    nb = Ns // tk
    nk = N_CHIP * nb

    def body(a_ref, b_ref, o_ref, acc_ref):
        k = pl.program_id(2)

        @pl.when(k == 0)
        def _():
            acc_ref[...] = jnp.zeros_like(acc_ref)

        acc_ref[...] += _dot(a_ref[...], b_ref[...], NT)

        @pl.when(k == nk - 1)
        def _():
            o_ref[...] = acc_ref[...].astype(out_dtype)

    return pl.pallas_call(
        body, grid=(M // tm, R // tr, nk), name=name,
        in_specs=[pl.BlockSpec((tm, tk), lambda i, j, k: (i, k)),
                  pl.BlockSpec((None, tr, tk), lambda i, j, k: (k // nb, j, k % nb))],
        out_specs=pl.BlockSpec((tm, tr), lambda i, j, k: (i, j)),
        out_shape=jax.ShapeDtypeStruct((M, R), out_dtype),
        scratch_shapes=[pltpu.VMEM((tm, tr), F32)],
        compiler_params=_params("parallel", "parallel", "arbitrary"),
    )(a, g)


def mm_tn(a, b, name, shard_out=False):
    S, M = a.shape
    N = b.shape[1]
    Ns = N // N_CHIP if shard_out else N
    tm, tn = _tile(M, 1024), _tile(Ns, 512)
    nb = Ns // tn

    def body(a_ref, b_ref, o_ref):
        o_ref[...] = _dot(a_ref[...], b_ref[...], TN)

    if shard_out:
        o_spec = pl.BlockSpec((None, tm, tn), lambda i, j: (j // nb, i, j % nb))
        o_shape = jax.ShapeDtypeStruct((N_CHIP, M, Ns), F32)
    else:
        o_spec = pl.BlockSpec((tm, tn), lambda i, j: (i, j))
        o_shape = jax.ShapeDtypeStruct((M, N), F32)
    return pl.pallas_call(
        body, grid=(M // tm, N // tn), name=name,
        in_specs=[pl.BlockSpec((S, tm), lambda i, j: (0, i)), pl.BlockSpec((S, tn), lambda i, j: (0, j))],
        out_specs=o_spec, out_shape=o_shape, compiler_params=_params("parallel", "parallel"),
    )(a, b)


def rms_fwd(x, g, name):
    S, D = x.shape
    tm = _tile(S, 512)

    def body(x_ref, g_ref, h_ref):
        xv = x_ref[...]
        r = lax.rsqrt(jnp.mean(xv * xv, axis=-1, keepdims=True) + EPS)
        h_ref[...] = ((xv * r) * g_ref[...]).astype(BF16)

    return pl.pallas_call(
        body, grid=(S // tm,), name=name,
        in_specs=[pl.BlockSpec((tm, D), lambda i: (i, 0)), pl.BlockSpec((1, D), lambda i: (0, 0))],
        out_specs=pl.BlockSpec((tm, D), lambda i: (i, 0)),
        out_shape=jax.ShapeDtypeStruct((S, D), BF16), compiler_params=_params("parallel"),
    )(x, g)


def rms_bwd(x, g, dh, name, dres=None, want_dx=True, want16=False):
    S, D = x.shape
    tm = _tile(S, 512)

    def body(x_ref, g_ref, dh_ref, *refs):
        i = pl.program_id(0)
        xv = x_ref[...]
        r = lax.rsqrt(jnp.mean(xv * xv, axis=-1, keepdims=True) + EPS)
        xn = xv * r
        dhv = dh_ref[...].astype(F32)
        gg_ref = refs[-1]

        @pl.when(i == 0)
        def _():
            gg_ref[...] = jnp.zeros_like(gg_ref)

        gg_ref[...] += jnp.sum(dhv * xn, axis=0, keepdims=True)
        if want_dx:
            dxn = dhv * g_ref[...]
            dx = r * (dxn - xn * jnp.mean(dxn * xn, axis=-1, keepdims=True))
            if dres is not None:
                dx = refs[0][...] + dx
            refs[-2][...] = dx.astype(refs[-2].dtype)
            if want16:
                refs[-3][...] = dx

    row = pl.BlockSpec((tm, D), lambda i: (i, 0))
    vec = pl.BlockSpec((1, D), lambda i: (0, 0))
    ins, in_specs = [x, g, dh], [row, vec, row]
    if dres is not None:
        ins.append(dres)
        in_specs.append(row)
    shapes, specs = [jax.ShapeDtypeStruct((1, D), F32)], [vec]
    if want_dx:
        if want16:
            shapes.insert(0, jax.ShapeDtypeStruct((S, D), BF16))
            specs.insert(0, row)
        shapes.insert(0, jax.ShapeDtypeStruct((S, D), F32))
        specs.insert(0, row)
    out = pl.pallas_call(body, grid=(S // tm,), name=name, in_specs=in_specs, out_specs=specs,
                         out_shape=shapes, compiler_params=_params("arbitrary"))(*ins)
    return out if want_dx else out[0]


def _shift_down(u, k, row):
    return jnp.where(row >= k, pltpu.roll(u, k, axis=0), 0.0)


def _shift_up(u, k, row):
    S = u.shape[0]
    return jnp.where(row < S - k, pltpu.roll(u, S - k, axis=0), 0.0)


def _conv_specs(S, D, tc):
    nb = D // tc
    col = lambda o: pl.BlockSpec((S, tc), lambda j, o=o: (0, o * nb + j))
    return col, pl.BlockSpec((3, tc), lambda j: (0, j))


def conv_fwd(proj, conv_w, D, name):
    S = proj.shape[0]
    tc = _tile(D, 256)
    col, wspec = _conv_specs(S, D, tc)

    def body(ch_ref, cb_ref, cc_ref, w_ref, a_ref):
        row = lax.broadcasted_iota(jnp.int32, (S, tc), 0)
        u = cc_ref[...] * ch_ref[...]
        w = w_ref[...]
        cv = w[0:1, :] * _shift_down(u, 2, row) + w[1:2, :] * _shift_down(u, 1, row) + w[2:3, :] * u
        a_ref[...] = (cb_ref[...] * cv).astype(BF16)

    return pl.pallas_call(
        body, grid=(D // tc,), name=name, in_specs=[col(0), col(1), col(2), wspec],
        out_specs=pl.BlockSpec((S, tc), lambda j: (0, j)),
        out_shape=jax.ShapeDtypeStruct((S, D), BF16), compiler_params=_params("parallel"),
    )(proj, proj, proj, conv_w)


def conv_bwd(proj, conv_w, da, D, name):
    S = proj.shape[0]
    tc = _tile(D, 256)
    col, wspec = _conv_specs(S, D, tc)
    blk = pl.BlockSpec((S, tc), lambda j: (0, j))

    def body(ch_ref, cb_ref, cc_ref, w_ref, da_ref, dch_ref, dcb_ref, dcc_ref, gw_ref):
        row = lax.broadcasted_iota(jnp.int32, (S, tc), 0)
        ch, cb, cc, dav = ch_ref[...], cb_ref[...], cc_ref[...], da_ref[...]
        w = w_ref[...]
        u = cc * ch
        u1, u2 = _shift_down(u, 1, row), _shift_down(u, 2, row)
        cv = w[0:1, :] * u2 + w[1:2, :] * u1 + w[2:3, :] * u
        dcb_ref[...] = (dav * cv).astype(BF16)
        dcv = dav * cb
        gw_ref[0:1, :] = jnp.sum(dcv * u2, axis=0, keepdims=True)
        gw_ref[1:2, :] = jnp.sum(dcv * u1, axis=0, keepdims=True)
        gw_ref[2:3, :] = jnp.sum(dcv * u, axis=0, keepdims=True)
        du = w[2:3, :] * dcv + w[1:2, :] * _shift_up(dcv, 1, row) + w[0:1, :] * _shift_up(dcv, 2, row)
        dcc_ref[...] = (du * ch).astype(BF16)
        dch_ref[...] = (du * cc).astype(BF16)

    act = jax.ShapeDtypeStruct((S, D), BF16)
    return pl.pallas_call(
        body, grid=(D // tc,), name=name, in_specs=[col(0), col(1), col(2), wspec, blk],
        out_specs=[blk, blk, blk, wspec], out_shape=[act, act, act, jax.ShapeDtypeStruct((3, D), F32)],
        compiler_params=_params("parallel"),
    )(proj, proj, proj, conv_w, da)


SB_BQ = 256
SB_BK = 128
SB_GROUP = 4


def _sb_consts(bq):
    lane = lax.broadcasted_iota(jnp.int32, (bq, LANES), 1)
    r = lax.broadcasted_iota(jnp.int32, (SB_BK, SB_BK), 0)
    c = lax.broadcasted_iota(jnp.int32, (SB_BK, SB_BK), 1)
    tri_rev = jnp.where(r > c, 1.0, 0.0).astype(BF16)
    tri_fwd = jnp.where(r < c, 1.0, 0.0).astype(BF16)
    return lane, tri_rev, tri_fwd


def _cumsum2(v, tri):
    hi = v.astype(BF16)
    lo = (v - hi.astype(F32)).astype(BF16)
    part = (lax.dot_general(hi, tri, NN, preferred_element_type=F32)
            + lax.dot_general(lo, tri, NN, preferred_element_type=F32))
    return part, jnp.sum(v, axis=1, keepdims=True)


def _sb_logits(z, past):
    sp = jnp.log(1.0 + jnp.exp(-jnp.abs(z)))
    l = jnp.minimum(z, 0.0) - sp
    m = l - z
    if past is not None:
        m = jnp.where(past, m, 0.0)
    return l, m


def _stack_heads(v, lane):
    return jnp.concatenate([jnp.where(lane < SB_HEAD_DIM, v, 0.0), jnp.where(lane >= SB_HEAD_DIM, v, 0.0)],
                           axis=0).astype(BF16)


def _unstack_heads(v, lane):
    bq = v.shape[0] // 2
    return jnp.where(lane < SB_HEAD_DIM, v[:bq], v[bq:])


def _sb_positions(i, bq):
    r = lax.broadcasted_iota(jnp.int32, (2 * bq, SB_BK), 0)
    trow = i * bq + jnp.where(r >= bq, r - bq, r)
    return trow, lax.broadcasted_iota(jnp.int32, (2 * bq, SB_BK), 1)


def _sb_specs(S, D, bq):
    npair = D // LANES
    qspec = pl.BlockSpec((bq, LANES), lambda p, i: (i, 3 * npair + p))
    kspec = pl.BlockSpec((S, LANES), lambda p, i: (0, 4 * npair + p))
    vspec = pl.BlockSpec((S, LANES), lambda p, i: (0, 5 * npair + p))
    return npair, qspec, kspec, vspec


def sb_fwd(proj, D, name):
    S = proj.shape[0]
    bq = SB_BQ
    nd = bq // SB_BK
    npair, qspec, kspec, vspec = _sb_specs(S, D, bq)
    scale = SB_HEAD_DIM ** -0.5

    def body(q_ref, k_ref, v_ref, o_ref, kb_ref, vb_ref):
        i = pl.program_id(1)

        @pl.when(i == 0)
        def _():
            kb_ref[...] = k_ref[...].astype(BF16)
            vb_ref[...] = v_ref[...].astype(BF16)

        lane, tri_rev, _ = _sb_consts(bq)
        qs = _stack_heads(q_ref[...] * scale, lane)
        trow, scol = _sb_positions(i, bq)

        def steps(j0, carry, n, masked):
            ks = [pl.multiple_of((j0 - t) * SB_BK, SB_BK) for t in range(n)]
            past = [(k + scol) < trow if masked else None for k in ks]
            zs = [lax.dot_general(qs, kb_ref[pl.ds(k, SB_BK), :], NT, preferred_element_type=F32) for k in ks]
            lm = [_sb_logits(z, p) for z, p in zip(zs, past)]
            cs = [_cumsum2(m, tri_rev) for _, m in lm]
            c, acc = carry
            for t in range(n):
                a = jnp.exp(lm[t][0] + (cs[t][0] + c))
                if masked:
                    a = jnp.where(past[t], a, 0.0)
                acc = acc + lax.dot_general(a.astype(BF16), vb_ref[pl.ds(ks[t], SB_BK), :], NN,
                                            preferred_element_type=F32)
                c = c + cs[t][1]
            return c, acc

        carry = (jnp.zeros((2 * bq, 1), F32), jnp.zeros((2 * bq, LANES), F32))
        carry = steps(i * nd + nd - 1, carry, nd, True)
        older = i * nd
        groups = older // SB_GROUP
        carry = lax.fori_loop(
            0, groups, lambda t, cr: steps(older - 1 - t * SB_GROUP, cr, SB_GROUP, False), carry)
        rest = older - groups * SB_GROUP
        carry = lax.fori_loop(0, rest // nd, lambda t, cr: steps(rest - 1 - t * nd, cr, nd, False), carry)
        o_ref[...] = _unstack_heads(carry[1], lane)

    return pl.pallas_call(
        body, grid=(npair, S // bq), name=name, in_specs=[qspec, kspec, vspec],
        out_specs=pl.BlockSpec((bq, LANES), lambda p, i: (i, p)),
        out_shape=jax.ShapeDtypeStruct((S, D), F32),
        scratch_shapes=[pltpu.VMEM((S, LANES), BF16), pltpu.VMEM((S, LANES), BF16)],
        compiler_params=_params("parallel", "arbitrary"),
    )(proj, proj, proj)


def sb_bwd(proj, do, D, name):
    S = proj.shape[0]
    bq = SB_BQ
    nd = bq // SB_BK
    nkb = S // SB_BK
    npair, qspec, kspec, vspec = _sb_specs(S, D, bq)
    scale = SB_HEAD_DIM ** -0.5

    def body(q_ref, k_ref, v_ref, do_ref, dq_ref, dk_ref, dv_ref,
             kb_ref, vb_ref, dk_acc, dv_acc, g_scr, b_scr, a_scr):
        i = pl.program_id(1)

        @pl.when(i == 0)
        def _():
            kb_ref[...] = k_ref[...].astype(BF16)
            vb_ref[...] = v_ref[...].astype(BF16)
            dk_acc[...] = jnp.zeros_like(dk_acc)
            dv_acc[...] = jnp.zeros_like(dv_acc)

        lane, tri_rev, tri_fwd = _sb_consts(bq)
        qs = _stack_heads(q_ref[...] * scale, lane)
        dos = _stack_heads(do_ref[...], lane)
        trow, scol = _sb_positions(i, bq)

        def sweep1(j0, c, n, masked):
            js = [j0 - t for t in range(n)]
            ks = [pl.multiple_of(j * SB_BK, SB_BK) for j in js]
            past = [(k + scol) < trow if masked else None for k in ks]
            zs = [lax.dot_general(qs, kb_ref[pl.ds(k, SB_BK), :], NT, preferred_element_type=F32) for k in ks]
            das = [lax.dot_general(dos, vb_ref[pl.ds(k, SB_BK), :], NT, preferred_element_type=F32) for k in ks]
            lm = [_sb_logits(z, p) for z, p in zip(zs, past)]
            cs = [_cumsum2(m, tri_rev) for _, m in lm]
            for t in range(n):
                b_scr[js[t]] = jnp.exp(lm[t][0])
            for t in range(n):
                a = jnp.exp(lm[t][0] + (cs[t][0] + c))
                if masked:
                    a = jnp.where(past[t], a, 0.0)
                g_scr[js[t]] = das[t] * a
                a_scr[js[t]] = a.astype(BF16)
                c = c + cs[t][1]
            return c

        older = i * nd
        groups = older // SB_GROUP
        rest = older - groups * SB_GROUP
        c = jnp.zeros((2 * bq, 1), F32)
        c = sweep1(i * nd + nd - 1, c, nd, True)
        c = lax.fori_loop(0, groups, lambda t, cr: sweep1(older - 1 - t * SB_GROUP, cr, SB_GROUP, False), c)
        lax.fori_loop(0, rest // nd, lambda t, cr: sweep1(rest - 1 - t * nd, cr, nd, False), c)

        def sweep2(j0, carry, n, masked):
            js = [j0 + t for t in range(n)]
            ks = [pl.multiple_of(j * SB_BK, SB_BK) for j in js]
            gv = [g_scr[j] for j in js]
            gs = [_cumsum2(g, tri_fwd) for g in gv]
            pc, dq = carry
            dzs = []
            for t in range(n):
                dz = gv[t] - b_scr[js[t]] * (gv[t] + (gs[t][0] + pc))
                if masked:
                    dz = jnp.where((ks[t] + scol) < trow, dz, 0.0)
                dzs.append(dz.astype(BF16))
                pc = pc + gs[t][1]
            for t in range(n):
                dq = dq + lax.dot_general(dzs[t], kb_ref[pl.ds(ks[t], SB_BK), :], NN, preferred_element_type=F32)
                dk_acc[pl.ds(ks[t], SB_BK), :] += lax.dot_general(dzs[t], qs, TN, preferred_element_type=F32)
                dv_acc[pl.ds(ks[t], SB_BK), :] += lax.dot_general(a_scr[js[t]], dos, TN, preferred_element_type=F32)
            return pc, dq

        carry = (jnp.zeros((2 * bq, 1), F32), jnp.zeros((2 * bq, LANES), F32))
        carry = lax.fori_loop(0, groups, lambda t, cr: sweep2(t * SB_GROUP, cr, SB_GROUP, False), carry)
        carry = lax.fori_loop(
            0, rest // nd, lambda t, cr: sweep2(groups * SB_GROUP + t * nd, cr, nd, False), carry)
        carry = sweep2(i * nd, carry, nd, True)
        dq_ref[...] = (_unstack_heads(carry[1], lane) * scale).astype(BF16)

        @pl.when(i == pl.num_programs(1) - 1)
        def _():
            dk_ref[...] = dk_acc[...].astype(BF16)
            dv_ref[...] = dv_acc[...].astype(BF16)

    full = pl.BlockSpec((S, LANES), lambda p, i: (0, p))
    blk = pl.BlockSpec((bq, LANES), lambda p, i: (i, p))
    act = jax.ShapeDtypeStruct((S, D), BF16)
    return pl.pallas_call(
        body, grid=(npair, S // bq), name=name, in_specs=[qspec, kspec, vspec, blk],
        out_specs=[blk, full, full], out_shape=[act, act, act],
        scratch_shapes=[pltpu.VMEM((S, LANES), BF16), pltpu.VMEM((S, LANES), BF16),
                        pltpu.VMEM((S, LANES), F32), pltpu.VMEM((S, LANES), F32),
                        pltpu.VMEM((nkb, 2 * bq, SB_BK), F32), pltpu.VMEM((nkb, 2 * bq, SB_BK), F32),
                        pltpu.VMEM((nkb, 2 * bq, SB_BK), BF16)],
        compiler_params=_params("parallel", "arbitrary"),
    )(proj, proj, proj, do)


def _rms_rows(v):
    r = lax.rsqrt(jnp.mean(v * v, axis=-1, keepdims=True) + EPS)
    return v * r, r


def _xa_specs(S, D, M, tq):
    nh = D // X_HEAD_DIM
    qspec = pl.BlockSpec((tq, X_HEAD_DIM), lambda h, i: (i, 6 * nh + h))
    kspec = pl.BlockSpec((M, X_HEAD_DIM), lambda h, i: (0, h))
    vspec = pl.BlockSpec((M, X_HEAD_DIM), lambda h, i: (0, nh + h))
    gspec = pl.BlockSpec((1, X_HEAD_DIM), lambda h, i: (0, 0))
    return nh, qspec, kspec, vspec, gspec


def xa_fwd(proj, kv, gq, gk, D, name):
    S, M = proj.shape[0], kv.shape[0]
    tq = _tile(S, 512)
    nh, qspec, kspec, vspec, gspec = _xa_specs(S, D, M, tq)
    scale = X_HEAD_DIM ** -0.5

    def body(q_ref, k_ref, v_ref, gq_ref, gk_ref, o_ref):
        qn = _rms_rows(q_ref[...])[0] * gq_ref[...]
        kn = _rms_rows(k_ref[...])[0] * gk_ref[...]
        s = _dot(qn, kn, NT) * scale
        e = jnp.exp(s - jnp.max(s, axis=-1, keepdims=True))
        p = e / jnp.sum(e, axis=-1, keepdims=True)
        o_ref[...] = _dot(p, v_ref[...], NN)

    return pl.pallas_call(
        body, grid=(nh, S // tq), name=name, in_specs=[qspec, kspec, vspec, gspec, gspec],
        out_specs=pl.BlockSpec((tq, X_HEAD_DIM), lambda h, i: (i, h)),
        out_shape=jax.ShapeDtypeStruct((S, D), F32), compiler_params=_params("parallel", "parallel"),
    )(proj, kv, kv, gq, gk)


def xa_bwd(proj, kv, gq, gk, do, D, name):
    S, M = proj.shape[0], kv.shape[0]
    tq = _tile(S, 512)
    nh, qspec, kspec, vspec, gspec = _xa_specs(S, D, M, tq)
    scale = X_HEAD_DIM ** -0.5

    def body(q_ref, k_ref, v_ref, gq_ref, gk_ref, do_ref, dq_ref, dk_ref, dv_ref, ggq_ref, ggk_ref,
             dkn_acc, dv_acc):
        h, i = pl.program_id(0), pl.program_id(1)

        @pl.when((h == 0) & (i == 0))
        def _():
            ggq_ref[...] = jnp.zeros_like(ggq_ref)
            ggk_ref[...] = jnp.zeros_like(ggk_ref)

        @pl.when(i == 0)
        def _():
            dkn_acc[...] = jnp.zeros_like(dkn_acc)
            dv_acc[...] = jnp.zeros_like(dv_acc)

        gq, gk = gq_ref[...], gk_ref[...]
        qhat, rq = _rms_rows(q_ref[...])
        khat, rk = _rms_rows(k_ref[...])
        qn, kn = qhat * gq, khat * gk
        s = _dot(qn, kn, NT) * scale
        e = jnp.exp(s - jnp.max(s, axis=-1, keepdims=True))
        p = e / jnp.sum(e, axis=-1, keepdims=True)
        dov = do_ref[...]
        dv_acc[...] += _dot(p, dov, TN)
        dp = _dot(dov, v_ref[...], NT)
        ds = (p * (dp - jnp.sum(dp * p, axis=-1, keepdims=True))) * scale
        dqn = _dot(ds, kn, NN)
        dkn_acc[...] += _dot(ds, qn, TN)
        ggq_ref[...] += jnp.sum(dqn * qhat, axis=0, keepdims=True)
        dqh = dqn * gq
        dq_ref[...] = (rq * (dqh - qhat * jnp.mean(dqh * qhat, axis=-1, keepdims=True))).astype(BF16)

        @pl.when(i == pl.num_programs(1) - 1)
        def _():
            dkn = dkn_acc[...]
            ggk_ref[...] += jnp.sum(dkn * khat, axis=0, keepdims=True)
            dkh = dkn * gk
            dk_ref[...] = (rk * (dkh - khat * jnp.mean(dkh * khat, axis=-1, keepdims=True))).astype(BF16)
            dv_ref[...] = dv_acc[...].astype(BF16)

    blk = pl.BlockSpec((tq, X_HEAD_DIM), lambda h, i: (i, h))
    kv_shape = jax.ShapeDtypeStruct((M, 2 * D), BF16)
    gshape = jax.ShapeDtypeStruct((1, X_HEAD_DIM), F32)
    dq, dk, dv, ggq, ggk = pl.pallas_call(
        body, grid=(nh, S // tq), name=name, in_specs=[qspec, kspec, vspec, gspec, gspec, blk],
        out_specs=[blk, kspec, vspec, gspec, gspec],
        out_shape=[jax.ShapeDtypeStruct((S, D), BF16), kv_shape, kv_shape, gshape, gshape],
        scratch_shapes=[pltpu.VMEM((M, X_HEAD_DIM), F32), pltpu.VMEM((M, X_HEAD_DIM), F32)],
        compiler_params=_params("arbitrary", "arbitrary"),
    )(proj, kv, kv, gq, gk, do)
    d_kv = jnp.concatenate([dk[:, :D], dv[:, D:]], axis=1)
    return dq, d_kv, ggq, ggk


def _gate_specs(S, D, tm):
    row = pl.BlockSpec((tm, D), lambda i: (i, 0))
    gate = lambda b: pl.BlockSpec((tm, D), lambda i, b=b: (i, 7 + b))
    return row, gate


def merge_fwd(proj, ys, D, name):
    S = proj.shape[0]
    tm = _tile(S, 256)
    row, gate = _gate_specs(S, D, tm)

    def body(g0, g1, g2, y0, y1, y2, o_ref):
        acc = jax.nn.sigmoid(g0[...]) * y0[...]
        acc = acc + jax.nn.sigmoid(g1[...]) * y1[...]
        acc = acc + jax.nn.sigmoid(g2[...]) * y2[...]
        o_ref[...] = acc.astype(BF16)

    return pl.pallas_call(
        body, grid=(S // tm,), name=name, in_specs=[gate(0), gate(1), gate(2), row, row, row],
        out_specs=row, out_shape=jax.ShapeDtypeStruct((S, D), BF16), compiler_params=_params("parallel"),
    )(proj, proj, proj, *ys)


def merge_bwd(proj, ys, dm, D, name):
    S = proj.shape[0]
    tm = _tile(S, 256)
    row, gate = _gate_specs(S, D, tm)

    def body(g0, g1, g2, y0, y1, y2, dm_ref, d0, d1, d2, dg_ref):
        dmv = dm_ref[...]
        for b, (g_ref, y_ref, d_ref) in enumerate(((g0, y0, d0), (g1, y1, d1), (g2, y2, d2))):
            s = jax.nn.sigmoid(g_ref[...])
            d_ref[...] = (dmv * s).astype(BF16)
            dg_ref[:, b * D:(b + 1) * D] = ((dmv * y_ref[...]) * (s * (1.0 - s))).astype(BF16)

    act = jax.ShapeDtypeStruct((S, D), BF16)
    return pl.pallas_call(
        body, grid=(S // tm,), name=name, in_specs=[gate(0), gate(1), gate(2), row, row, row, row],
        out_specs=[row, row, row, pl.BlockSpec((tm, 3 * D), lambda i: (i, 0))],
        out_shape=[act, act, act, jax.ShapeDtypeStruct((S, 3 * D), BF16)], compiler_params=_params("parallel"),
    )(proj, proj, proj, *ys, dm)


def loss_head(y, target, name):
    S, D = y.shape
    tm = _tile(S, 512)

    def body(y_ref, t_ref, dy_ref, dy16_ref, l_ref):
        @pl.when(pl.program_id(0) == 0)
        def _():
            l_ref[...] = jnp.zeros_like(l_ref)

        e = y_ref[...] - t_ref[...]
        dy = e * (1.0 / D)
        dy_ref[...] = dy
        dy16_ref[...] = dy.astype(BF16)
        l_ref[...] += jnp.sum(e * e, axis=0, keepdims=True)

    row = pl.BlockSpec((tm, D), lambda i: (i, 0))
    vec = pl.BlockSpec((1, D), lambda i: (0, 0))
    return pl.pallas_call(
        body, grid=(S // tm,), name=name, in_specs=[row, row], out_specs=[row, row, vec],
        out_shape=[jax.ShapeDtypeStruct((S, D), F32), jax.ShapeDtypeStruct((S, D), BF16),
                   jax.ShapeDtypeStruct((1, D), F32)],
        compiler_params=_params("arbitrary"),
    )(y, target)


def _rows2d(a):
    return a.reshape(-1, a.shape[-1])


def _ew_call(fn, ins, out_dtypes, name):
    R, C = ins[0].shape
    tr = _tile(R, max(8, (1 << 19) // C))
    spec = pl.BlockSpec((tr, C), lambda i: (i, 0))

    def body(*refs):
        outs = fn(*[r[...] for r in refs[:len(ins)]])
        for o_ref, o in zip(refs[len(ins):], outs):
            o_ref[...] = o.astype(o_ref.dtype)

    return pl.pallas_call(
        body, grid=(R // tr,), name=name, in_specs=[spec] * len(ins), out_specs=[spec] * len(out_dtypes),
        out_shape=[jax.ShapeDtypeStruct((R, C), d) for d in out_dtypes], compiler_params=_params("parallel"),
    )(*ins)


def adamw(w, g, m, v, name):
    def fn(w, g, m, v):
        m = ADAM_B1 * m + (1.0 - ADAM_B1) * g
        v = ADAM_B2 * v + (1.0 - ADAM_B2) * (g * g)
        m_hat = m / (1.0 - ADAM_B1 ** ADAM_STEP)
        v_hat = v / (1.0 - ADAM_B2 ** ADAM_STEP)
        return -ADAM_LR * (m_hat / (jnp.sqrt(v_hat) + ADAM_EPS) + ADAM_WD * w), m, v

    shp = w.shape
    outs = _ew_call(fn, [_rows2d(a) for a in (w, g, m, v)], [F32, F32, F32], name)
    return [o.reshape(shp) for o in outs]


def _placed_call(fn, place, grid, ins, in_specs, out_shape, out_specs, name, dep=None):
    n = len(ins)
    deps = [] if dep is None else [dep]

    def body(place_ref, *refs):
        outs = fn(*[r[...] for r in refs[:n]])
        for o_ref, o in zip(refs[n + len(deps):], outs):
            o_ref[...] = o.astype(o_ref.dtype)

    return pl.pallas_call(
        body, name=name, out_shape=out_shape,
        grid_spec=pltpu.PrefetchScalarGridSpec(
            num_scalar_prefetch=1, grid=grid, in_specs=list(in_specs) + [ANY] * len(deps), out_specs=out_specs),
        compiler_params=_params(*["parallel"] * len(grid)),
    )(place, *ins, *deps)


def _row_tile(R, C):
    return _tile(R, max(16, (1 << 19) // C))


def cast_into_full(w, place, name):
    R, C = w.shape
    tr = _row_tile(R, C)
    return _placed_call(
        lambda a: (a,), place, (R // tr,), [w], [pl.BlockSpec((tr, C), lambda i, p: (i, 0))],
        [jax.ShapeDtypeStruct((N_CHIP, R, C), BF16)], [pl.BlockSpec((None, tr, C), lambda i, p: (p[0], i, 0))],
        name)[0]


def pair_sum(g4, got, place, name):
    _, hr, C = got.shape
    tr = _row_tile(hr, C)
    nb = hr // tr
    blk = pl.BlockSpec((None, tr, C), lambda s, i, p: (s, i, 0))
    return _placed_call(
        lambda a, b: (a + b, a + b), place, (N_CHIP, nb), [g4, got],
        [pl.BlockSpec((None, tr, C), lambda s, i, p: (s, p[1] * nb + i, 0)), blk],
        [jax.ShapeDtypeStruct(got.shape, F32), jax.ShapeDtypeStruct(got.shape, BF16)], [blk, blk], name)


def chip_sum(p32, got, place, name, dep=None):
    _, H, C = p32.shape
    tr = _row_tile(H, C)
    nb = H // tr
    peer = lambda j: pl.BlockSpec((None, tr, C), lambda i, p, j=j: (j, i, 0))
    return _placed_call(
        lambda a, b, c, d: (((a + b.astype(F32)) + c.astype(F32)) + d.astype(F32),), place, (nb,),
        [p32, got, got, got], [pl.BlockSpec((None, tr, C), lambda i, p: (p[0], i, 0)), peer(0), peer(1), peer(2)],
        [jax.ShapeDtypeStruct((2 * H, C), F32)], [pl.BlockSpec((tr, C), lambda i, p: (p[1] * nb + i, 0))],
        name, dep=dep)[0]


ANY = pl.BlockSpec(memory_space=pl.ANY)
CHIP_FLIPS = ((1, 0), (0, 1), (1, 1))


def _place():
    return lax.axis_index("x"), lax.axis_index("y"), lax.axis_index("c")


def _flip(v, f):
    return 1 - v if f else v


def gather_weights(fulls, small, name):
    n = len(fulls)

    def body(*refs):
        small_in = refs[n]
        outs, small_out = refs[n + 1:2 * n + 1], refs[2 * n + 1]
        ici_send, ici_recv, d2d_send, d2d_recv, loc_sem, sm_send, sm_recv = refs[2 * n + 2:]
        x, y, c = _place()
        mine = 2 * x + y
        sibling = (x, y, 1 - c)
        chips = [(_flip(x, fx), _flip(y, fy)) for fx, fy in CHIP_FLIPS]

        local = [pltpu.make_async_copy(small_in, small_out.at[mine], loc_sem)]
        local[0].start()

        def half(a, chip_idx, core):
            hr = outs[a].shape[1] // 2
            return outs[a].at[chip_idx, pl.ds(core * hr, hr), :]

        sends = []
        for j, (px, py) in enumerate(chips):
            cp = pltpu.make_async_remote_copy(
                src_ref=small_in, dst_ref=small_out.at[mine], send_sem=sm_send.at[j], recv_sem=sm_recv.at[j],
                device_id=(px, py, c), device_id_type=MESH)
            cp.start()
            sends.append(cp)
        for a in range(n):
            for j, (px, py) in enumerate(chips):
                cp = pltpu.make_async_remote_copy(
                    src_ref=half(a, mine, c), dst_ref=half(a, mine, c),
                    send_sem=ici_send.at[a, j], recv_sem=ici_recv.at[a, j],
                    device_id=(px, py, c), device_id_type=MESH)
                cp.start()
                sends.append(cp)
        for a in range(n):
            for j, (px, py) in enumerate(chips):
                src = half(a, 2 * px + py, c)
                pltpu.make_async_remote_copy(
                    src_ref=src, dst_ref=src, send_sem=ici_send.at[a, j], recv_sem=ici_recv.at[a, j],
                    device_id=(px, py, c), device_id_type=MESH).wait_recv()
                cp = pltpu.make_async_remote_copy(
                    src_ref=src, dst_ref=src, send_sem=d2d_send.at[a, j], recv_sem=d2d_recv.at[a, j],
                    device_id=sibling, device_id_type=MESH)
                cp.start()
                sends.append(cp)
        for a in range(n):
            for j, (px, py) in enumerate(chips):
                dst = half(a, 2 * px + py, 1 - c)
                pltpu.make_async_remote_copy(
                    src_ref=dst, dst_ref=dst, send_sem=d2d_send.at[a, j], recv_sem=d2d_recv.at[a, j],
                    device_id=sibling, device_id_type=MESH).wait_recv()
        for j, (px, py) in enumerate(chips):
            dst = small_out.at[2 * px + py]
            pltpu.make_async_remote_copy(
                src_ref=dst, dst_ref=dst, send_sem=sm_send.at[j], recv_sem=sm_recv.at[j],
                device_id=(px, py, c), device_id_type=MESH).wait_recv()
        for cp in sends:
            cp.wait_send()
        for cp in local:
            cp.wait()

    out_shape = [jax.ShapeDtypeStruct(f.shape, f.dtype) for f in fulls]
    out_shape.append(jax.ShapeDtypeStruct((N_CHIP,) + small.shape, small.dtype))
    dma = pltpu.SemaphoreType.DMA
    return pl.pallas_call(
        body, name=name, in_specs=[ANY] * (n + 1), out_specs=[ANY] * (n + 1), out_shape=out_shape,
        input_output_aliases={a: a for a in range(n)},
        scratch_shapes=[dma((n, 3)), dma((n, 3)), dma((n, 3)), dma((n, 3)), dma, dma((3,)), dma((3,))],
    )(*fulls, small)


def exchange_halves(grads, name):
    n = len(grads)

    def body(*refs):
        ins, got = refs[:n], refs[n:2 * n]
        send_sem, recv_sem = refs[2 * n:]
        x, y, c = _place()
        copies = []
        for a in range(n):
            hr = ins[a].shape[1] // 2
            cp = pltpu.make_async_remote_copy(
                src_ref=ins[a].at[:, pl.ds((1 - c) * hr, hr), :], dst_ref=got[a],
                send_sem=send_sem.at[a], recv_sem=recv_sem.at[a], device_id=(x, y, 1 - c), device_id_type=MESH)
            cp.start()
            copies.append(cp)
        for cp in copies:
            cp.wait()

    dma = pltpu.SemaphoreType.DMA
    return pl.pallas_call(
        body, name=name, in_specs=[ANY] * n, out_specs=[ANY] * n,
        out_shape=[jax.ShapeDtypeStruct((N_CHIP, g.shape[1] // 2, g.shape[2]), g.dtype) for g in grads],
        scratch_shapes=[dma((n,)), dma((n,))],
    )(*grads)


def scatter_partials(parts16, name):
    n = len(parts16)

    def body(*refs):
        p16, got = refs[:n], refs[n:2 * n]
        send_sem, recv_sem = refs[2 * n:]
        x, y, c = _place()
        chips = [(_flip(x, fx), _flip(y, fy)) for fx, fy in CHIP_FLIPS]
        copies = []
        for a in range(n):
            for j, (px, py) in enumerate(chips):
                cp = pltpu.make_async_remote_copy(
                    src_ref=p16[a].at[2 * px + py], dst_ref=got[a].at[j],
                    send_sem=send_sem.at[a, j], recv_sem=recv_sem.at[a, j],
                    device_id=(px, py, c), device_id_type=MESH)
                cp.start()
                copies.append(cp)
        for cp in copies:
            cp.wait()

    dma = pltpu.SemaphoreType.DMA
    return pl.pallas_call(
        body, name=name, in_specs=[ANY] * n, out_specs=[ANY] * n,
        out_shape=[jax.ShapeDtypeStruct((3,) + p.shape[1:], BF16) for p in parts16],
        scratch_shapes=[dma((n, 3)), dma((n, 3))],
    )(*parts16)


def join_halves(fulls, name):
    n = len(fulls)

    def body(*refs):
        outs = refs[n:2 * n]
        send_sem, recv_sem = refs[2 * n:]
        x, y, c = _place()
        copies = []
        for a in range(n):
            hr = outs[a].shape[0] // 2
            half = outs[a].at[pl.ds(c * hr, hr), :]
            cp = pltpu.make_async_remote_copy(
                src_ref=half, dst_ref=half, send_sem=send_sem.at[a], recv_sem=recv_sem.at[a],
                device_id=(x, y, 1 - c), device_id_type=MESH)
            cp.start()
            copies.append(cp)
        for a, cp in enumerate(copies):
            hr = outs[a].shape[0] // 2
            theirs = outs[a].at[pl.ds((1 - c) * hr, hr), :]
            cp.wait_send()
            pltpu.make_async_remote_copy(
                src_ref=theirs, dst_ref=theirs, send_sem=send_sem.at[a], recv_sem=recv_sem.at[a],
                device_id=(x, y, 1 - c), device_id_type=MESH).wait_recv()

    dma = pltpu.SemaphoreType.DMA
    return pl.pallas_call(
        body, name=name, in_specs=[ANY] * n, out_specs=[ANY] * n,
        out_shape=[jax.ShapeDtypeStruct(f.shape, F32) for f in fulls],
        input_output_aliases={a: a for a in range(n)},
        scratch_shapes=[dma((n,)), dma((n,))],
    )(*fulls)


HBM = pl.BlockSpec(memory_space=pltpu.HBM)
SEM = pl.BlockSpec(memory_space=pltpu.SEMAPHORE)
EFFECT = pltpu.SideEffectType.DATAFLOW_SIDE_EFFECTING


def _in_hbm(a):
    return pltpu.with_memory_space_constraint(a, pltpu.HBM)


def _gather_half(ref, chip_idx, core):
    hr = ref.shape[1] // 2
    return ref.at[chip_idx, pl.ds(core * hr, hr), :]


def gather_start(fulls, dep, name):
    n = len(fulls)

    def body(*refs):
        send_sem, recv_sem = refs[n + 1], refs[n + 2]
        outs, token = refs[n + 3:2 * n + 3], refs[2 * n + 3]
        x, y, c = _place()
        mine = 2 * x + y
        for a in range(n):
            for j, (fx, fy) in enumerate(CHIP_FLIPS):
                half = _gather_half(outs[a], mine, c)
                pltpu.make_async_remote_copy(
                    src_ref=half, dst_ref=half, send_sem=send_sem.at[3 * a + j], recv_sem=recv_sem.at[3 * a + j],
                    device_id=(_flip(x, fx), _flip(y, fy), c), device_id_type=MESH).start()
        token[...] = jnp.zeros_like(token)

    dma = pltpu.SemaphoreType.DMA
    out = pl.pallas_call(
        body, name=name,
        out_shape=(dma((3 * n,)), dma((3 * n,)), *[pltpu.HBM(f.shape, f.dtype) for f in fulls],
                   jax.ShapeDtypeStruct((8, LANES), F32)),
        in_specs=[HBM] * n + [ANY], out_specs=(SEM, SEM, *[HBM] * n, pl.BlockSpec(memory_space=pltpu.VMEM)),
        input_output_aliases={a: 2 + a for a in range(n)},
        compiler_params=pltpu.CompilerParams(has_side_effects=EFFECT),
    )(*[_in_hbm(f) for f in fulls], dep)
    return out[0], out[1], out[2:2 + n], out[2 + n]


def gather_wait(send_sem, recv_sem, fulls, after, name):
    n = len(fulls)

    def body(*refs):
        outs = refs[n + 3:]
        send_sem, recv_sem = refs[n], refs[n + 1]
        x, y, c = _place()
        mine = 2 * x + y
        for a in range(n):
            for j, (fx, fy) in enumerate(CHIP_FLIPS):
                cp = pltpu.make_async_remote_copy(
                    src_ref=_gather_half(outs[a], mine, c),
                    dst_ref=_gather_half(outs[a], 2 * _flip(x, fx) + _flip(y, fy), c),
                    send_sem=send_sem.at[3 * a + j], recv_sem=recv_sem.at[3 * a + j],
                    device_id=(_flip(x, fx), _flip(y, fy), c), device_id_type=MESH)
                cp.wait_send()
                cp.wait_recv()

    return pl.pallas_call(
        body, name=name, out_shape=tuple(pltpu.HBM(f.shape, f.dtype) for f in fulls),
        in_specs=[HBM] * n + [SEM, SEM, ANY], out_specs=tuple([HBM] * n),
        input_output_aliases={a: a for a in range(n)},
        compiler_params=pltpu.CompilerParams(has_side_effects=EFFECT),
    )(*fulls, send_sem, recv_sem, after)


def gather_forward(fulls, name):
    n = len(fulls)

    def body(*refs):
        outs = refs[n:2 * n]
        send_sem, recv_sem = refs[2 * n:]
        x, y, c = _place()
        chips = [2 * _flip(x, fx) + _flip(y, fy) for fx, fy in CHIP_FLIPS]
        copies = []
        for a in range(n):
            for j, q in enumerate(chips):
                src = _gather_half(outs[a], q, c)
                cp = pltpu.make_async_remote_copy(
                    src_ref=src, dst_ref=src, send_sem=send_sem.at[a, j], recv_sem=recv_sem.at[a, j],
                    device_id=(x, y, 1 - c), device_id_type=MESH)
                cp.start()
                copies.append(cp)
        for a in range(n):
            for j, q in enumerate(chips):
                dst = _gather_half(outs[a], q, 1 - c)
                copies[3 * a + j].wait_send()
                pltpu.make_async_remote_copy(
                    src_ref=dst, dst_ref=dst, send_sem=send_sem.at[a, j], recv_sem=recv_sem.at[a, j],
                    device_id=(x, y, 1 - c), device_id_type=MESH).wait_recv()

    dma = pltpu.SemaphoreType.DMA
    return pl.pallas_call(
        body, name=name, in_specs=[ANY] * n, out_specs=[ANY] * n,
        out_shape=[jax.ShapeDtypeStruct(f.shape, f.dtype) for f in fulls],
        input_output_aliases={a: a for a in range(n)},
        scratch_shapes=[dma((n, 3)), dma((n, 3))],
    )(*fulls)


def scatter_start(parts16, dep, name):
    n = len(parts16)

    def body(*refs):
        send_sem, recv_sem = refs[2 * n + 1], refs[2 * n + 2]
        p16, land = refs[2 * n + 3:3 * n + 3], refs[3 * n + 3:4 * n + 3]
        token = refs[4 * n + 3]
        x, y, c = _place()
        for a in range(n):
            for j, (fx, fy) in enumerate(CHIP_FLIPS):
                px, py = _flip(x, fx), _flip(y, fy)
                pltpu.make_async_remote_copy(
                    src_ref=p16[a].at[2 * px + py], dst_ref=land[a].at[j],
                    send_sem=send_sem.at[3 * a + j], recv_sem=recv_sem.at[3 * a + j],
                    device_id=(px, py, c), device_id_type=MESH).start()
        token[...] = jnp.zeros_like(token)

    dma = pltpu.SemaphoreType.DMA
    lands = [lax.empty((3,) + p.shape[1:], BF16) for p in parts16]
    out = pl.pallas_call(
        body, name=name,
        out_shape=(dma((3 * n,)), dma((3 * n,)), *[pltpu.HBM(p.shape, p.dtype) for p in parts16],
                   *[pltpu.HBM(l.shape, l.dtype) for l in lands], jax.ShapeDtypeStruct((8, LANES), F32)),
        in_specs=[HBM] * (2 * n) + [ANY],
        out_specs=(SEM, SEM, *[HBM] * (2 * n), pl.BlockSpec(memory_space=pltpu.VMEM)),
        input_output_aliases={a: 2 + a for a in range(2 * n)},
        compiler_params=pltpu.CompilerParams(has_side_effects=EFFECT),
    )(*[_in_hbm(p) for p in parts16], *[_in_hbm(l) for l in lands], dep)
    return out[0], out[1], out[2:2 + n], out[2 + n:2 + 2 * n], out[2 + 2 * n]


def scatter_wait(send_sem, recv_sem, parts16, lands, after, name):
    n = len(parts16)
    m = 2 * n + 2 + len(after)

    def body(*refs):
        send_sem, recv_sem = refs[2 * n], refs[2 * n + 1]
        p16, land = refs[m:m + n], refs[m + n:m + 2 * n]
        x, y, c = _place()
        for a in range(n):
            for j, (fx, fy) in enumerate(CHIP_FLIPS):
                px, py = _flip(x, fx), _flip(y, fy)
                cp = pltpu.make_async_remote_copy(
                    src_ref=p16[a].at[2 * px + py], dst_ref=land[a].at[j],
                    send_sem=send_sem.at[3 * a + j], recv_sem=recv_sem.at[3 * a + j],
                    device_id=(px, py, c), device_id_type=MESH)
                cp.wait_send()
                cp.wait_recv()

    out = pl.pallas_call(
        body, name=name,
        out_shape=tuple(pltpu.HBM(p.shape, p.dtype) for p in list(parts16) + list(lands)),
        in_specs=[HBM] * (2 * n) + [SEM, SEM] + [ANY] * len(after), out_specs=tuple([HBM] * (2 * n)),
        input_output_aliases={a: a for a in range(2 * n)},
        compiler_params=pltpu.CompilerParams(has_side_effects=EFFECT),
    )(*parts16, *lands, send_sem, recv_sem, *after)
    return out[n:]


def allreduce_small(block, name):
    R, C = block.shape

    def body(in_ref, out_ref, slots, send_sem, recv_sem):
        x, y, c = _place()
        me = 4 * x + 2 * y + c
        slots[me] = in_ref[...]
        copies = []
        for r in range(1, 8):
            fx, fy, fc = (r >> 2) & 1, (r >> 1) & 1, r & 1
            cp = pltpu.make_async_remote_copy(
                src_ref=in_ref, dst_ref=slots.at[me], send_sem=send_sem.at[r - 1], recv_sem=recv_sem.at[r - 1],
                device_id=(_flip(x, fx), _flip(y, fy), _flip(c, fc)), device_id_type=MESH)
            cp.start()
            copies.append(cp)
        for cp in copies:
            cp.wait()
        acc = slots[0]
        for d in range(1, 8):
            acc = acc + slots[d]
        out_ref[...] = acc

    vm = pl.BlockSpec(memory_space=pltpu.VMEM)
    dma = pltpu.SemaphoreType.DMA
    return pl.pallas_call(
        body, name=name, in_specs=[vm], out_specs=vm, out_shape=jax.ShapeDtypeStruct((R, C), F32),
        scratch_shapes=[pltpu.VMEM((8, R, C), F32), dma((7,)), dma((7,))],
    )(block)


def local_step(x, mem, target, g_mix, g_mem, q_norm_g, k_norm_g, g_mlp, conv_w, w_in, w_in_dep,
               rest_weights, early_grads):
    S, D = x.shape
    h = rms_fwd(x, g_mix, "rms_mix")
    proj = mm_nn_shard(h, w_in, "proj", dep=w_in_dep)
    a_conv = conv_fwd(proj, conv_w, D, "conv_fwd")
    o_sb = sb_fwd(proj, D, "sb_fwd")
    w_conv_out, w_sb_out, w_mem_kv, w_x_out, w_out, w_up, w_down = rest_weights(o_sb)
    mem_n = rms_fwd(mem, g_mem, "rms_mem")
    kv = mm_nn_shard(mem_n, w_mem_kv, "kv")
    o_x = xa_fwd(proj, kv, q_norm_g, k_norm_g, D, "xa_fwd")
    ys = [mm_nn(a_conv, w_conv_out, "y_conv"), mm_nn(o_sb, w_sb_out, "y_sb"), mm_nn(o_x, w_x_out, "y_x")]
    merged = merge_fwd(proj, ys, D, "merge_fwd")
    x1 = mm_nn(merged, w_out, "x1", res=x)
    h2 = rms_fwd(x1, g_mlp, "rms_mlp")
    up, act = mm_nn_shard(h2, w_up, "up", relu2=True)
    x2 = mm_nn(act, w_down, "x2", res=x1)
    dy, dy16, loss_cols = loss_head(x2, target, "loss_head")
    d_up = mm_nt(dy16, w_down, "d_up", up=up, out_dtype=BF16)
    g = {"w_down": mm_tn(act, dy16, "g_w_down")}
    g["w_up"] = mm_tn(h2, d_up, "g_w_up", shard_out=True)
    dh2 = mm_nt_shard(d_up, w_up, "dh2")
    dx1, dx1_16, g["g_mlp"] = rms_bwd(x1, g_mlp, dh2, "rms_mlp_bwd", dres=dy, want16=True)
    g["w_out"] = mm_tn(merged, dx1_16, "g_w_out")
    dm = mm_nt(dx1_16, w_out, "d_merged")
    dy_c, dy_s, dy_x, d_gate = merge_bwd(proj, ys, dm, D, "merge_bwd")
    g["w_conv_out"] = mm_tn(a_conv, dy_c, "g_w_conv_out")
    g["w_sb_out"] = mm_tn(o_sb, dy_s, "g_w_sb_out")
    g["w_x_out"] = mm_tn(o_x, dy_x, "g_w_x_out")
    d_xq, d_kv, g["q_norm_g"], g["k_norm_g"] = xa_bwd(
        proj, kv, q_norm_g, k_norm_g, mm_nt(dy_x, w_x_out, "d_o_x"), D, "xa_bwd")
    g["w_mem_kv"] = mm_tn(mem_n, d_kv, "g_w_mem_kv", shard_out=True)
    g["g_mem"] = rms_bwd(mem, g_mem, mm_nt_shard(d_kv, w_mem_kv, "d_mem_n"), "rms_mem_bwd", want_dx=False)
    dep = early_grads(g)
    d_a_conv = mm_nt(dy_c, w_conv_out, "d_a_conv", dep=dep)
    d_ch, d_cb, d_cc, g["conv_w"] = conv_bwd(proj, conv_w, d_a_conv, D, "conv_bwd")
    dq, dk, dv = sb_bwd(proj, mm_nt(dy_s, w_sb_out, "d_o_sb", dep=dep), D, "sb_bwd")
    d_proj = jnp.concatenate([d_ch, d_cb, d_cc, dq, dk, dv, d_xq, d_gate], axis=1)
    g["w_in"] = mm_tn(h, d_proj, "g_w_in", shard_out=True)
    dh = mm_nt_shard(d_proj, w_in, "dh")
    grad_x, g["g_mix"] = rms_bwd(x, g_mix, dh, "rms_mix_bwd", dres=dx1)
    return loss_cols, grad_x, g


BIG = ("w_in", "w_conv_out", "w_sb_out", "w_mem_kv", "w_x_out", "w_out", "w_up", "w_down")
REST = BIG[1:]
COL_SHARDED = ("w_in", "w_mem_kv", "w_up")
WEIGHTS = ("g_mix", "g_mem", "w_in", "conv_w", "w_conv_out", "w_sb_out", "q_norm_g", "k_norm_g",
           "w_mem_kv", "w_x_out", "w_out", "g_mlp", "w_up", "w_down")


def _pack_small(D, g_mix, g_mem, g_mlp, q_norm_g, k_norm_g, conv_w, last):
    qk = jnp.concatenate([q_norm_g, k_norm_g, jnp.zeros((1, D - 2 * X_HEAD_DIM), F32)], axis=1)
    cw = jnp.pad(conv_w, ((0, 0), (0, D - conv_w.shape[1])))
    return jnp.concatenate([g_mix, g_mem, g_mlp, qk, cw, last], axis=0)


def kernel(x, mem, g_mix, g_mem, w_in, conv_w, w_conv_out, w_sb_out, q_norm_g, k_norm_g, w_mem_kv, w_x_out, w_out, g_mlp, w_up, w_down, loss_target, m_g_mix, m_g_mem, m_w_in, m_conv_w, m_w_conv_out, m_w_sb_out, m_q_norm_g, m_k_norm_g, m_w_mem_kv, m_w_x_out, m_w_out, m_g_mlp, m_w_up, m_w_down, v_g_mix, v_g_mem, v_w_in, v_conv_w, v_w_conv_out, v_w_sb_out, v_q_norm_g, v_k_norm_g, v_w_mem_kv, v_w_x_out, v_w_out, v_g_mlp, v_w_up, v_w_down):
    S, D = x.shape[1], x.shape[2]
    w = dict(g_mix=g_mix, g_mem=g_mem, w_in=w_in, conv_w=conv_w, w_conv_out=w_conv_out, w_sb_out=w_sb_out,
             q_norm_g=q_norm_g, k_norm_g=k_norm_g, w_mem_kv=w_mem_kv, w_x_out=w_x_out, w_out=w_out,
             g_mlp=g_mlp, w_up=w_up, w_down=w_down)
    m = dict(g_mix=m_g_mix, g_mem=m_g_mem, w_in=m_w_in, conv_w=m_conv_w, w_conv_out=m_w_conv_out,
             w_sb_out=m_w_sb_out, q_norm_g=m_q_norm_g, k_norm_g=m_k_norm_g, w_mem_kv=m_w_mem_kv,
             w_x_out=m_w_x_out, w_out=m_w_out, g_mlp=m_g_mlp, w_up=m_w_up, w_down=m_w_down)
    v = dict(g_mix=v_g_mix, g_mem=v_g_mem, w_in=v_w_in, conv_w=v_conv_w, w_conv_out=v_w_conv_out,
             w_sb_out=v_w_sb_out, q_norm_g=v_q_norm_g, k_norm_g=v_k_norm_g, w_mem_kv=v_w_mem_kv,
             w_x_out=v_w_x_out, w_out=v_w_out, g_mlp=v_g_mlp, w_up=v_w_up, w_down=v_w_down)
    chip = 2 * lax.axis_index("x") + lax.axis_index("y")
    cs = conv_w.shape[2]

    place = jnp.stack([chip, lax.axis_index("c")]).astype(jnp.int32)
    mine16 = [cast_into_full(w[k][0], place, "cast_" + k) for k in BIG]
    cw_block = jnp.pad(conv_w[0], ((0, 5), (0, 0)))
    w_in_full, cw_all = gather_weights(mine16[:1], cw_block, "gather_weights")
    conv_full = jnp.concatenate([cw_all[p, :3] for p in range(N_CHIP)], axis=1)
    g_send, g_recv, flying, g_token = gather_start(mine16[1:], w_in_full, "gather_rest_start")

    def layout(k, a):
        return a if k in COL_SHARDED else a.reshape(-1, a.shape[-1])

    def rest_weights(after):
        landed = gather_wait(g_send, g_recv, flying, after, "gather_rest_wait")
        return [layout(k, a) for k, a in zip(REST, gather_forward(landed, "gather_rest_forward"))]

    def reduce_start(names, g, dep):
        g4 = [g[k] if k in COL_SHARDED else g[k].reshape(N_CHIP, -1, g[k].shape[-1]) for k in names]
        got = exchange_halves(g4, "exchange_halves_" + names[0])
        pairs = [pair_sum(a, b, place, "pair_sum_" + k) for k, a, b in zip(names, g4, got)]
        return pairs, scatter_start([p[1] for p in pairs], dep, "scatter_start_" + names[0])

    early = {}

    def early_grads(g):
        early["pairs"], early["fly"] = reduce_start(REST, g, g["g_mem"])
        return early["fly"][4]

    loss_cols, grad_x, g = local_step(
        x[0], mem[0], loss_target[0], g_mix, g_mem, q_norm_g, k_norm_g, g_mlp, conv_full,
        w_in_full, g_token, rest_weights, early_grads)

    small = allreduce_small(
        _pack_small(D, g["g_mix"], g["g_mem"], g["g_mlp"], g["q_norm_g"], g["k_norm_g"], g["conv_w"], loss_cols),
        "allreduce_small")
    loss = (0.5 / D) * jnp.sum(small[7])
    gsum = {"g_mix": small[0:1], "g_mem": small[1:2], "g_mlp": small[2:3],
            "q_norm_g": small[3:4, :X_HEAD_DIM], "k_norm_g": small[3:4, X_HEAD_DIM:2 * X_HEAD_DIM],
            "conv_w": lax.dynamic_slice(small[4:7], (0, chip * cs), (3, cs))[None]}

    send, recv, flying16, lands, _ = early["fly"]
    got_rest = scatter_wait(send, recv, flying16, lands, [g["w_in"]], "scatter_wait_rest")
    got_in = exchange_halves([g["w_in"]], "exchange_halves_w_in")[0]
    pair_in = pair_sum(g["w_in"], got_in, place, "pair_sum_w_in")
    send, recv, flying16, lands, token = scatter_start([pair_in[1]], got_rest[0], "scatter_start_w_in")

    delta, new_m, new_v = {}, {}, {}
    halves = [chip_sum(p[0], b, place, "chip_sum_" + k, dep=token)
              for k, p, b in zip(REST, early["pairs"], got_rest)]
    for k, a in zip(REST, join_halves(halves, "join_halves_rest")):
        gsum[k] = a[None]
        delta[k], new_m[k], new_v[k] = adamw(w[k], gsum[k], m[k], v[k], "adamw_" + k)
    small_names = ("g_mix", "g_mem", "g_mlp", "q_norm_g", "k_norm_g", "conv_w")
    zero_row = jnp.zeros((1, D), F32)
    packed = [_pack_small(D, *[t[k] if k != "conv_w" else t[k][0] for k in small_names], zero_row)
              for t in (w, gsum, m, v)]
    sm = adamw(*packed, "adamw_small")
    for t, block in zip((delta, new_m, new_v), sm):
        t["g_mix"], t["g_mem"], t["g_mlp"] = block[0:1], block[1:2], block[2:3]
        t["q_norm_g"], t["k_norm_g"] = block[3:4, :X_HEAD_DIM], block[3:4, X_HEAD_DIM:2 * X_HEAD_DIM]
        t["conv_w"] = block[4:7, :cs][None]

    done = [new_v[k] for k in REST] + [sm[2]]
    got_in = scatter_wait(send, recv, flying16, lands, done, "scatter_wait_w_in")[0]
    half_in = chip_sum(pair_in[0], got_in, place, "chip_sum_w_in")
    gsum["w_in"] = join_halves([half_in], "join_halves_w_in")[0][None]
    delta["w_in"], new_m["w_in"], new_v["w_in"] = adamw(w["w_in"], gsum["w_in"], m["w_in"], v["w_in"], "adamw_w_in")

    return (loss, grad_x[None], *[gsum[k] for k in WEIGHTS], *[delta[k] for k in WEIGHTS],
            *[new_m[k] for k in WEIGHTS], *[new_v[k] for k in WEIGHTS])
```

```python
import functools

import jax
import jax.numpy as jnp
from jax import lax
from jax.experimental import pallas as pl
from jax.experimental.pallas import tpu as pltpu

F32 = jnp.float32
BF16 = jnp.bfloat16
EPS = 1e-6
N_CHIP = 4
SB_HEAD_DIM = 64
X_HEAD_DIM = 256
LANES = 128
VMEM_LIMIT = 56 * 1024 * 1024
ADAM_LR, ADAM_B1, ADAM_B2, ADAM_EPS, ADAM_WD, ADAM_STEP = 0.001, 0.9, 0.999, 1e-8, 0.01, 10
MESH = pl.DeviceIdType.MESH


def _params(*sem):
    return pltpu.CompilerParams(dimension_semantics=sem, vmem_limit_bytes=VMEM_LIMIT)


def _tile(n, pref):
    if n <= pref:
        return n
    t = 1 << (pref.bit_length() - 1)
    while n % t:
        t //= 2
    return t


NN = (((1,), (0,)), ((), ()))
NT = (((1,), (1,)), ((), ()))
TN = (((0,), (0,)), ((), ()))


def _dot(a, b, dims):
    return lax.dot_general(a.astype(BF16), b.astype(BF16), dims, preferred_element_type=F32)


def mm_nn_shard(a, g, name, relu2=False, dep=None):
    M, K = a.shape
    _, _, Ns = g.shape
    tm, tn = _tile(M, 2048), _tile(Ns, 512)
    nb = Ns // tn

    def body(a_ref, b_ref, *o_refs):
        o_refs = o_refs[len(deps):]
        acc = _dot(a_ref[...], b_ref[...], NN)
        o_refs[0][...] = acc
        if relu2:
            r = jnp.maximum(acc, 0.0)
            o_refs[1][...] = (r * r).astype(BF16)

    o_spec = pl.BlockSpec((tm, tn), lambda i, j: (i, j))
    shapes = [jax.ShapeDtypeStruct((M, N_CHIP * Ns), F32)]
    specs = [o_spec]
    if relu2:
        shapes.append(jax.ShapeDtypeStruct((M, N_CHIP * Ns), BF16))
        specs.append(o_spec)
    deps = [] if dep is None else [dep]
    out = pl.pallas_call(
        body, grid=(M // tm, N_CHIP * nb), name=name,
        in_specs=[pl.BlockSpec((tm, K), lambda i, j: (i, 0)),
                  pl.BlockSpec((None, K, tn), lambda i, j: (j // nb, 0, j % nb))] + [ANY] * len(deps),
        out_specs=specs, out_shape=shapes, compiler_params=_params("parallel", "parallel"),
    )(a, g, *deps)
    return out if relu2 else out[0]


def mm_nn(a, w, name, res=None, out_dtype=F32):
    M, K = a.shape
    N = w.shape[1]
    tm, tn = _tile(M, 2048 if K <= 2048 else 1024), _tile(N, 512)

    def body(a_ref, b_ref, *refs):
        acc = _dot(a_ref[...], b_ref[...], NN)
        if res is not None:
            acc = refs[0][...] + acc
        refs[-1][...] = acc.astype(out_dtype)

    o_spec = pl.BlockSpec((tm, tn), lambda i, j: (i, j))
    ins = [a, w] + ([res] if res is not None else [])
    return pl.pallas_call(
        body, grid=(M // tm, N // tn), name=name,
        in_specs=[pl.BlockSpec((tm, K), lambda i, j: (i, 0)), pl.BlockSpec((K, tn), lambda i, j: (0, j))]
        + ([o_spec] if res is not None else []),
        out_specs=o_spec, out_shape=jax.ShapeDtypeStruct((M, N), out_dtype),
        compiler_params=_params("parallel", "parallel"),
    )(*ins)


def mm_nt(a, w, name, up=None, out_dtype=F32, dep=None):
    M, N = a.shape
    R = w.shape[0]
    tm, tr = _tile(M, 2048), _tile(R, 512)

    def body(a_ref, b_ref, *refs):
        acc = _dot(a_ref[...], b_ref[...], NT)
        if up is not None:
            acc = acc * (2.0 * jnp.maximum(refs[0][...], 0.0))
        refs[-1][...] = acc.astype(out_dtype)

    o_spec = pl.BlockSpec((tm, tr), lambda i, j: (i, j))
    ins = [a, w] + ([up] if up is not None else []) + ([dep] if dep is not None else [])
    return pl.pallas_call(
        body, grid=(M // tm, R // tr), name=name,
        in_specs=[pl.BlockSpec((tm, N), lambda i, j: (i, 0)), pl.BlockSpec((tr, N), lambda i, j: (j, 0))]
        + ([o_spec] if up is not None else []) + ([ANY] if dep is not None else []),
        out_specs=o_spec, out_shape=jax.ShapeDtypeStruct((M, R), out_dtype),
        compiler_params=_params("parallel", "parallel"),
    )(*ins)


def mm_nt_shard(a, g, name, out_dtype=F32):
    M = a.shape[0]
    _, R, Ns = g.shape
    tm, tr, tk = _tile(M, 1024), _tile(R, 1024), _tile(Ns, 2560)
    nb = Ns // tk
    nk = N_CHIP * nb

    def body(a_ref, b_ref, o_ref, acc_ref):
        k = pl.program_id(2)

        @pl.when(k == 0)
        def _():
            acc_ref[...] = jnp.zeros_like(acc_ref)

        acc_ref[...] += _dot(a_ref[...], b_ref[...], NT)

        @pl.when(k == nk - 1)
        def _():
            o_ref[...] = acc_ref[...].astype(out_dtype)

    return pl.pallas_call(
        body, grid=(M // tm, R // tr, nk), name=name,
        in_specs=[pl.BlockSpec((tm, tk), lambda i, j, k: (i, k)),
                  pl.BlockSpec((None, tr, tk), lambda i, j, k: (k // nb, j, k % nb))],
        out_specs=pl.BlockSpec((tm, tr), lambda i, j, k: (i, j)),
        out_shape=jax.ShapeDtypeStruct((M, R), out_dtype),
        scratch_shapes=[pltpu.VMEM((tm, tr), F32)],
        compiler_params=_params("parallel", "parallel", "arbitrary"),
    )(a, g)


def mm_tn(a, b, name, shard_out=False):
    S, M = a.shape
    N = b.shape[1]
    Ns = N // N_CHIP if shard_out else N
    tm, tn = _tile(M, 1024), _tile(Ns, 512)
    nb = Ns // tn

    def body(a_ref, b_ref, o_ref):
        o_ref[...] = _dot(a_ref[...], b_ref[...], TN)

    if shard_out:
        o_spec = pl.BlockSpec((None, tm, tn), lambda i, j: (j // nb, i, j % nb))
        o_shape = jax.ShapeDtypeStruct((N_CHIP, M, Ns), F32)
    else:
        o_spec = pl.BlockSpec((tm, tn), lambda i, j: (i, j))
        o_shape = jax.ShapeDtypeStruct((M, N), F32)
    return pl.pallas_call(
        body, grid=(M // tm, N // tn), name=name,
        in_specs=[pl.BlockSpec((S, tm), lambda i, j: (0, i)), pl.BlockSpec((S, tn), lambda i, j: (0, j))],
        out_specs=o_spec, out_shape=o_shape, compiler_params=_params("parallel", "parallel"),
    )(a, b)


def rms_fwd(x, g, name, dep=None):
    S, D = x.shape
    tm = _tile(S, 512)
    deps = [] if dep is None else [dep]

    def body(x_ref, g_ref, *refs):
        xv = x_ref[...]
        r = lax.rsqrt(jnp.mean(xv * xv, axis=-1, keepdims=True) + EPS)
        refs[-1][...] = ((xv * r) * g_ref[...]).astype(BF16)

    return pl.pallas_call(
        body, grid=(S // tm,), name=name,
        in_specs=[pl.BlockSpec((tm, D), lambda i: (i, 0)), pl.BlockSpec((1, D), lambda i: (0, 0))]
        + [ANY] * len(deps),
        out_specs=pl.BlockSpec((tm, D), lambda i: (i, 0)),
        out_shape=jax.ShapeDtypeStruct((S, D), BF16), compiler_params=_params("parallel"),
    )(x, g, *deps)


def rms_bwd(x, g, dh, name, dres=None, want_dx=True, want16=False):
    S, D = x.shape
    tm = _tile(S, 512)

    def body(x_ref, g_ref, dh_ref, *refs):
        i = pl.program_id(0)
        xv = x_ref[...]
        r = lax.rsqrt(jnp.mean(xv * xv, axis=-1, keepdims=True) + EPS)
        xn = xv * r
        dhv = dh_ref[...].astype(F32)
        gg_ref = refs[-1]

        @pl.when(i == 0)
        def _():
            gg_ref[...] = jnp.zeros_like(gg_ref)

        gg_ref[...] += jnp.sum(dhv * xn, axis=0, keepdims=True)
        if want_dx:
            dxn = dhv * g_ref[...]
            dx = r * (dxn - xn * jnp.mean(dxn * xn, axis=-1, keepdims=True))
            if dres is not None:
                dx = refs[0][...] + dx
            refs[-2][...] = dx.astype(refs[-2].dtype)
            if want16:
                refs[-3][...] = dx

    row = pl.BlockSpec((tm, D), lambda i: (i, 0))
    vec = pl.BlockSpec((1, D), lambda i: (0, 0))
    ins, in_specs = [x, g, dh], [row, vec, row]
    if dres is not None:
        ins.append(dres)
        in_specs.append(row)
    shapes, specs = [jax.ShapeDtypeStruct((1, D), F32)], [vec]
    if want_dx:
        if want16:
            shapes.insert(0, jax.ShapeDtypeStruct((S, D), BF16))
            specs.insert(0, row)
        shapes.insert(0, jax.ShapeDtypeStruct((S, D), F32))
        specs.insert(0, row)
    out = pl.pallas_call(body, grid=(S // tm,), name=name, in_specs=in_specs, out_specs=specs,
                         out_shape=shapes, compiler_params=_params("arbitrary"))(*ins)
    return out if want_dx else out[0]


def _shift_down(u, k, row):
    return jnp.where(row >= k, pltpu.roll(u, k, axis=0), 0.0)


def _shift_up(u, k, row):
    S = u.shape[0]
    return jnp.where(row < S - k, pltpu.roll(u, S - k, axis=0), 0.0)


def _conv_specs(S, D, tc):
    nb = D // tc
    col = lambda o: pl.BlockSpec((S, tc), lambda j, o=o: (0, o * nb + j))
    return col, pl.BlockSpec((3, tc), lambda j: (0, j))


def conv_fwd(proj, conv_w, D, name):
    S = proj.shape[0]
    tc = _tile(D, 256)
    col, wspec = _conv_specs(S, D, tc)

    def body(ch_ref, cb_ref, cc_ref, w_ref, a_ref):
        row = lax.broadcasted_iota(jnp.int32, (S, tc), 0)
        u = cc_ref[...] * ch_ref[...]
        w = w_ref[...]
        cv = w[0:1, :] * _shift_down(u, 2, row) + w[1:2, :] * _shift_down(u, 1, row) + w[2:3, :] * u
        a_ref[...] = (cb_ref[...] * cv).astype(BF16)

    return pl.pallas_call(
        body, grid=(D // tc,), name=name, in_specs=[col(0), col(1), col(2), wspec],
        out_specs=pl.BlockSpec((S, tc), lambda j: (0, j)),
        out_shape=jax.ShapeDtypeStruct((S, D), BF16), compiler_params=_params("parallel"),
    )(proj, proj, proj, conv_w)


def conv_bwd(proj, conv_w, da, D, name):
    S = proj.shape[0]
    tc = _tile(D, 256)
    col, wspec = _conv_specs(S, D, tc)
    blk = pl.BlockSpec((S, tc), lambda j: (0, j))

    def body(ch_ref, cb_ref, cc_ref, w_ref, da_ref, dch_ref, dcb_ref, dcc_ref, gw_ref):
        row = lax.broadcasted_iota(jnp.int32, (S, tc), 0)
        ch, cb, cc, dav = ch_ref[...], cb_ref[...], cc_ref[...], da_ref[...]
        w = w_ref[...]
        u = cc * ch
        u1, u2 = _shift_down(u, 1, row), _shift_down(u, 2, row)
        cv = w[0:1, :] * u2 + w[1:2, :] * u1 + w[2:3, :] * u
        dcb_ref[...] = (dav * cv).astype(BF16)
        dcv = dav * cb
        gw_ref[0:1, :] = jnp.sum(dcv * u2, axis=0, keepdims=True)
        gw_ref[1:2, :] = jnp.sum(dcv * u1, axis=0, keepdims=True)
        gw_ref[2:3, :] = jnp.sum(dcv * u, axis=0, keepdims=True)
        du = w[2:3, :] * dcv + w[1:2, :] * _shift_up(dcv, 1, row) + w[0:1, :] * _shift_up(dcv, 2, row)
        dcc_ref[...] = (du * ch).astype(BF16)
        dch_ref[...] = (du * cc).astype(BF16)

    act = jax.ShapeDtypeStruct((S, D), BF16)
    return pl.pallas_call(
        body, grid=(D // tc,), name=name, in_specs=[col(0), col(1), col(2), wspec, blk],
        out_specs=[blk, blk, blk, wspec], out_shape=[act, act, act, jax.ShapeDtypeStruct((3, D), F32)],
        compiler_params=_params("parallel"),
    )(proj, proj, proj, conv_w, da)


SB_BQ_FWD = 512
SB_BQ_BWD = 256
SB_BK = 128
SB_GROUP = 4


def _sb_consts(bq):
    lane = lax.broadcasted_iota(jnp.int32, (bq, LANES), 1)
    r = lax.broadcasted_iota(jnp.int32, (SB_BK, SB_BK), 0)
    c = lax.broadcasted_iota(jnp.int32, (SB_BK, SB_BK), 1)
    tri_rev = jnp.where(r > c, 1.0, 0.0).astype(BF16)
    tri_fwd = jnp.where(r < c, 1.0, 0.0).astype(BF16)
    return lane, tri_rev, tri_fwd


def _cumsum2(v, tri):
    hi = v.astype(BF16)
    lo = (v - hi.astype(F32)).astype(BF16)
    part = (lax.dot_general(hi, tri, NN, preferred_element_type=F32)
            + lax.dot_general(lo, tri, NN, preferred_element_type=F32))
    return part, jnp.sum(v, axis=1, keepdims=True)


def _sb_logits(z, past):
    sp = jnp.log(1.0 + jnp.exp(-jnp.abs(z)))
    l = jnp.minimum(z, 0.0) - sp
    m = l - z
    if past is not None:
        m = jnp.where(past, m, 0.0)
    return l, m


def _stack_heads(v, lane):
    return jnp.concatenate([jnp.where(lane < SB_HEAD_DIM, v, 0.0), jnp.where(lane >= SB_HEAD_DIM, v, 0.0)],
                           axis=0).astype(BF16)


def _unstack_heads(v, lane):
    bq = v.shape[0] // 2
    return jnp.where(lane < SB_HEAD_DIM, v[:bq], v[bq:])


def _sb_positions(i, bq):
    r = lax.broadcasted_iota(jnp.int32, (2 * bq, SB_BK), 0)
    trow = i * bq + jnp.where(r >= bq, r - bq, r)
    return trow, lax.broadcasted_iota(jnp.int32, (2 * bq, SB_BK), 1)


def _sb_specs(S, D, bq):
    npair = D // LANES
    qspec = pl.BlockSpec((bq, LANES), lambda p, i: (i, 3 * npair + p))
    kspec = pl.BlockSpec((S, LANES), lambda p, i: (0, 4 * npair + p))
    vspec = pl.BlockSpec((S, LANES), lambda p, i: (0, 5 * npair + p))
    return npair, qspec, kspec, vspec


def sb_fwd(proj, D, name):
    S = proj.shape[0]
    bq = min(SB_BQ_FWD, S)
    nd = bq // SB_BK
    npair, qspec, kspec, vspec = _sb_specs(S, D, bq)
    scale = SB_HEAD_DIM ** -0.5

    def body(q_ref, k_ref, v_ref, o_ref, kb_ref, vb_ref):
        i = pl.program_id(1)

        @pl.when(i == 0)
        def _():
            kb_ref[...] = k_ref[...].astype(BF16)
            vb_ref[...] = v_ref[...].astype(BF16)

        lane, tri_rev, _ = _sb_consts(bq)
        qs = _stack_heads(q_ref[...] * scale, lane)
        trow, scol = _sb_positions(i, bq)

        def steps(j0, carry, n, masked):
            ks = [pl.multiple_of((j0 - t) * SB_BK, SB_BK) for t in range(n)]
            past = [(k + scol) < trow if masked else None for k in ks]
            zs = [lax.dot_general(qs, kb_ref[pl.ds(k, SB_BK), :], NT, preferred_element_type=F32) for k in ks]
            lm = [_sb_logits(z, p) for z, p in zip(zs, past)]
            cs = [_cumsum2(m, tri_rev) for _, m in lm]
            c, acc = carry
            for t in range(n):
                a = jnp.exp(lm[t][0] + (cs[t][0] + c))
                if masked:
                    a = jnp.where(past[t], a, 0.0)
                acc = acc + lax.dot_general(a.astype(BF16), vb_ref[pl.ds(ks[t], SB_BK), :], NN,
                                            preferred_element_type=F32)
                c = c + cs[t][1]
            return c, acc

        carry = (jnp.zeros((2 * bq, 1), F32), jnp.zeros((2 * bq, LANES), F32))
        carry = steps(i * nd + nd - 1, carry, nd, True)
        older = i * nd
        groups = older // SB_GROUP
        carry = lax.fori_loop(
            0, groups, lambda t, cr: steps(older - 1 - t * SB_GROUP, cr, SB_GROUP, False), carry)
        rest = older - groups * SB_GROUP
        carry = lax.fori_loop(0, rest // nd, lambda t, cr: steps(rest - 1 - t * nd, cr, nd, False), carry)
        o_ref[...] = _unstack_heads(carry[1], lane)

    return pl.pallas_call(
        body, grid=(npair, S // bq), name=name, in_specs=[qspec, kspec, vspec],
        out_specs=pl.BlockSpec((bq, LANES), lambda p, i: (i, p)),
        out_shape=jax.ShapeDtypeStruct((S, D), F32),
        scratch_shapes=[pltpu.VMEM((S, LANES), BF16), pltpu.VMEM((S, LANES), BF16)],
        compiler_params=_params("parallel", "arbitrary"),
    )(proj, proj, proj)


def sb_bwd(proj, do, D, name, dep=None):
    S = proj.shape[0]
    bq = min(SB_BQ_BWD, S)
    nd = bq // SB_BK
    nkb = S // SB_BK
    npair, qspec, kspec, vspec = _sb_specs(S, D, bq)
    scale = SB_HEAD_DIM ** -0.5

    def body(q_ref, k_ref, v_ref, do_ref, *refs):
        dq_ref, dk_ref, dv_ref, kb_ref, vb_ref, dk_acc, dv_acc, g_scr, b_scr, a_scr = refs[len(deps):]
        i = pl.program_id(1)

        @pl.when(i == 0)
        def _():
            kb_ref[...] = k_ref[...].astype(BF16)
            vb_ref[...] = v_ref[...].astype(BF16)
            dk_acc[...] = jnp.zeros_like(dk_acc)
            dv_acc[...] = jnp.zeros_like(dv_acc)

        lane, tri_rev, tri_fwd = _sb_consts(bq)
        qs = _stack_heads(q_ref[...] * scale, lane)
        dos = _stack_heads(do_ref[...], lane)
        qs_t, dos_t = qs.T, dos.T
        trow, scol = _sb_positions(i, bq)

        def sweep1(j0, c, n, masked):
            js = [j0 - t for t in range(n)]
            ks = [pl.multiple_of(j * SB_BK, SB_BK) for j in js]
            past = [(k + scol) < trow if masked else None for k in ks]
            zs = [lax.dot_general(qs, kb_ref[pl.ds(k, SB_BK), :], NT, preferred_element_type=F32) for k in ks]
            das = [lax.dot_general(dos, vb_ref[pl.ds(k, SB_BK), :], NT, preferred_element_type=F32) for k in ks]
            lm = [_sb_logits(z, p) for z, p in zip(zs, past)]
            cs = [_cumsum2(m, tri_rev) for _, m in lm]
            for t in range(n):
                b_scr[js[t]] = jnp.exp(lm[t][0])
            for t in range(n):
                a = jnp.exp(lm[t][0] + (cs[t][0] + c))
                if masked:
                    a = jnp.where(past[t], a, 0.0)
                g_scr[js[t]] = das[t] * a
                a_scr[js[t]] = a.astype(BF16)
                c = c + cs[t][1]
            return c

        older = i * nd
        groups = older // SB_GROUP
        rest = older - groups * SB_GROUP
        c = jnp.zeros((2 * bq, 1), F32)
        c = sweep1(i * nd + nd - 1, c, nd, True)
        c = lax.fori_loop(0, groups, lambda t, cr: sweep1(older - 1 - t * SB_GROUP, cr, SB_GROUP, False), c)
        lax.fori_loop(0, rest // nd, lambda t, cr: sweep1(rest - 1 - t * nd, cr, nd, False), c)

        def sweep2(j0, carry, n, masked):
            js = [j0 + t for t in range(n)]
            ks = [pl.multiple_of(j * SB_BK, SB_BK) for j in js]
            gv = [g_scr[j] for j in js]
            gs = [_cumsum2(g, tri_fwd) for g in gv]
            pc, dq = carry
            dzs = []
            for t in range(n):
                dz = gv[t] - b_scr[js[t]] * (gv[t] + (gs[t][0] + pc))
                if masked:
                    dz = jnp.where((ks[t] + scol) < trow, dz, 0.0)
                dzs.append(dz.astype(BF16))
                pc = pc + gs[t][1]
            for t in range(n):
                dq = dq + lax.dot_general(dzs[t], kb_ref[pl.ds(ks[t], SB_BK), :], NN, preferred_element_type=F32)
                dk_acc[js[t]] += lax.dot_general(qs_t, dzs[t], NN, preferred_element_type=F32)
                dv_acc[js[t]] += lax.dot_general(dos_t, a_scr[js[t]], NN, preferred_element_type=F32)
            return pc, dq

        carry = (jnp.zeros((2 * bq, 1), F32), jnp.zeros((2 * bq, LANES), F32))
        carry = lax.fori_loop(0, groups, lambda t, cr: sweep2(t * SB_GROUP, cr, SB_GROUP, False), carry)
        carry = lax.fori_loop(
            0, rest // nd, lambda t, cr: sweep2(groups * SB_GROUP + t * nd, cr, nd, False), carry)
        carry = sweep2(i * nd, carry, nd, True)
        dq_ref[...] = (_unstack_heads(carry[1], lane) * scale).astype(BF16)

        @pl.when(i == pl.num_programs(1) - 1)
        def _():
            for j in range(nkb):
                dk_ref[j * SB_BK:(j + 1) * SB_BK, :] = dk_acc[j].T.astype(BF16)
                dv_ref[j * SB_BK:(j + 1) * SB_BK, :] = dv_acc[j].T.astype(BF16)

    deps = [] if dep is None else [dep]
    full = pl.BlockSpec((S, LANES), lambda p, i: (0, p))
    blk = pl.BlockSpec((bq, LANES), lambda p, i: (i, p))
    act = jax.ShapeDtypeStruct((S, D), BF16)
    return pl.pallas_call(
        body, grid=(npair, S // bq), name=name, in_specs=[qspec, kspec, vspec, blk] + [ANY] * len(deps),
        out_specs=[blk, full, full], out_shape=[act, act, act],
        scratch_shapes=[pltpu.VMEM((S, LANES), BF16), pltpu.VMEM((S, LANES), BF16),
                        pltpu.VMEM((nkb, LANES, SB_BK), F32), pltpu.VMEM((nkb, LANES, SB_BK), F32),
                        pltpu.VMEM((nkb, 2 * bq, SB_BK), F32), pltpu.VMEM((nkb, 2 * bq, SB_BK), F32),
                        pltpu.VMEM((nkb, 2 * bq, SB_BK), BF16)],
        compiler_params=_params("parallel", "arbitrary"),
    )(proj, proj, proj, do, *deps)


def _rms_rows(v):
    r = lax.rsqrt(jnp.mean(v * v, axis=-1, keepdims=True) + EPS)
    return v * r, r


def _xa_specs(S, D, M, tq):
    nh = D // X_HEAD_DIM
    qspec = pl.BlockSpec((tq, X_HEAD_DIM), lambda h, i: (i, 6 * nh + h))
    kspec = pl.BlockSpec((M, X_HEAD_DIM), lambda h, i: (0, h))
    vspec = pl.BlockSpec((M, X_HEAD_DIM), lambda h, i: (0, nh + h))
    gspec = pl.BlockSpec((1, X_HEAD_DIM), lambda h, i: (0, 0))
    return nh, qspec, kspec, vspec, gspec


def xa_fwd(proj, kv, gq, gk, D, name):
    S, M = proj.shape[0], kv.shape[0]
    tq = _tile(S, 512)
    nh, qspec, kspec, vspec, gspec = _xa_specs(S, D, M, tq)
    scale = X_HEAD_DIM ** -0.5

    def body(q_ref, k_ref, v_ref, gq_ref, gk_ref, o_ref):
        qn = _rms_rows(q_ref[...])[0] * gq_ref[...]
        kn = _rms_rows(k_ref[...])[0] * gk_ref[...]
        s = _dot(qn, kn, NT) * scale
        e = jnp.exp(s - jnp.max(s, axis=-1, keepdims=True))
        p = e / jnp.sum(e, axis=-1, keepdims=True)
        o_ref[...] = _dot(p, v_ref[...], NN)

    return pl.pallas_call(
        body, grid=(nh, S // tq), name=name, in_specs=[qspec, kspec, vspec, gspec, gspec],
        out_specs=pl.BlockSpec((tq, X_HEAD_DIM), lambda h, i: (i, h)),
        out_shape=jax.ShapeDtypeStruct((S, D), F32), compiler_params=_params("parallel", "parallel"),
    )(proj, kv, kv, gq, gk)


def xa_bwd(proj, kv, gq, gk, do, D, name):
    S, M = proj.shape[0], kv.shape[0]
    tq = _tile(S, 512)
    nh, qspec, kspec, vspec, gspec = _xa_specs(S, D, M, tq)
    scale = X_HEAD_DIM ** -0.5

    def body(q_ref, k_ref, v_ref, gq_ref, gk_ref, do_ref, dq_ref, dk_ref, dv_ref, ggq_ref, ggk_ref,
             dkn_acc, dv_acc):
        h, i = pl.program_id(0), pl.program_id(1)

        @pl.when((h == 0) & (i == 0))
        def _():
            ggq_ref[...] = jnp.zeros_like(ggq_ref)
            ggk_ref[...] = jnp.zeros_like(ggk_ref)

        @pl.when(i == 0)
        def _():
            dkn_acc[...] = jnp.zeros_like(dkn_acc)
            dv_acc[...] = jnp.zeros_like(dv_acc)

        gq, gk = gq_ref[...], gk_ref[...]
        qhat, rq = _rms_rows(q_ref[...])
        khat, rk = _rms_rows(k_ref[...])
        qn, kn = qhat * gq, khat * gk
        s = _dot(qn, kn, NT) * scale
        e = jnp.exp(s - jnp.max(s, axis=-1, keepdims=True))
        p = e / jnp.sum(e, axis=-1, keepdims=True)
        dov = do_ref[...]
        dv_acc[...] += _dot(p, dov, TN)
        dp = _dot(dov, v_ref[...], NT)
        ds = (p * (dp - jnp.sum(dp * p, axis=-1, keepdims=True))) * scale
        dqn = _dot(ds, kn, NN)
        dkn_acc[...] += _dot(ds, qn, TN)
        ggq_ref[...] += jnp.sum(dqn * qhat, axis=0, keepdims=True)
        dqh = dqn * gq
        dq_ref[...] = (rq * (dqh - qhat * jnp.mean(dqh * qhat, axis=-1, keepdims=True))).astype(BF16)

        @pl.when(i == pl.num_programs(1) - 1)
        def _():
            dkn = dkn_acc[...]
            ggk_ref[...] += jnp.sum(dkn * khat, axis=0, keepdims=True)
            dkh = dkn * gk
            dk_ref[...] = (rk * (dkh - khat * jnp.mean(dkh * khat, axis=-1, keepdims=True))).astype(BF16)
            dv_ref[...] = dv_acc[...].astype(BF16)

    blk = pl.BlockSpec((tq, X_HEAD_DIM), lambda h, i: (i, h))
    kv_shape = jax.ShapeDtypeStruct((M, 2 * D), BF16)
    gshape = jax.ShapeDtypeStruct((1, X_HEAD_DIM), F32)
    dq, dk, dv, ggq, ggk = pl.pallas_call(
        body, grid=(nh, S // tq), name=name, in_specs=[qspec, kspec, vspec, gspec, gspec, blk],
        out_specs=[blk, kspec, vspec, gspec, gspec],
        out_shape=[jax.ShapeDtypeStruct((S, D), BF16), kv_shape, kv_shape, gshape, gshape],
        scratch_shapes=[pltpu.VMEM((M, X_HEAD_DIM), F32), pltpu.VMEM((M, X_HEAD_DIM), F32)],
        compiler_params=_params("arbitrary", "arbitrary"),
    )(proj, kv, kv, gq, gk, do)
    d_kv = jnp.concatenate([dk[:, :D], dv[:, D:]], axis=1)
    return dq, d_kv, ggq, ggk


def _gate_specs(S, D, tm):
    row = pl.BlockSpec((tm, D), lambda i: (i, 0))
    gate = lambda b: pl.BlockSpec((tm, D), lambda i, b=b: (i, 7 + b))
    return row, gate


def merge_fwd(proj, ys, D, name):
    S = proj.shape[0]
    tm = _tile(S, 256)
    row, gate = _gate_specs(S, D, tm)

    def body(g0, g1, g2, y0, y1, y2, o_ref):
        acc = jax.nn.sigmoid(g0[...]) * y0[...]
        acc = acc + jax.nn.sigmoid(g1[...]) * y1[...]
        acc = acc + jax.nn.sigmoid(g2[...]) * y2[...]
        o_ref[...] = acc.astype(BF16)

    return pl.pallas_call(
        body, grid=(S // tm,), name=name, in_specs=[gate(0), gate(1), gate(2), row, row, row],
        out_specs=row, out_shape=jax.ShapeDtypeStruct((S, D), BF16), compiler_params=_params("parallel"),
    )(proj, proj, proj, *ys)


def merge_bwd(proj, ys, dm, D, name):
    S = proj.shape[0]
    tm = _tile(S, 256)
    row, gate = _gate_specs(S, D, tm)

    def body(g0, g1, g2, y0, y1, y2, dm_ref, d0, d1, d2, dg_ref):
        dmv = dm_ref[...]
        for b, (g_ref, y_ref, d_ref) in enumerate(((g0, y0, d0), (g1, y1, d1), (g2, y2, d2))):
            s = jax.nn.sigmoid(g_ref[...])
            d_ref[...] = (dmv * s).astype(BF16)
            dg_ref[:, b * D:(b + 1) * D] = ((dmv * y_ref[...]) * (s * (1.0 - s))).astype(BF16)

    act = jax.ShapeDtypeStruct((S, D), BF16)
    return pl.pallas_call(
        body, grid=(S // tm,), name=name, in_specs=[gate(0), gate(1), gate(2), row, row, row, row],
        out_specs=[row, row, row, pl.BlockSpec((tm, 3 * D), lambda i: (i, 0))],
        out_shape=[act, act, act, jax.ShapeDtypeStruct((S, 3 * D), BF16)], compiler_params=_params("parallel"),
    )(proj, proj, proj, *ys, dm)


def loss_head(y, target, name):
    S, D = y.shape
    tm = _tile(S, 512)

    def body(y_ref, t_ref, dy_ref, dy16_ref, l_ref):
        @pl.when(pl.program_id(0) == 0)
        def _():
            l_ref[...] = jnp.zeros_like(l_ref)

        e = y_ref[...] - t_ref[...]
        dy = e * (1.0 / D)
        dy_ref[...] = dy
        dy16_ref[...] = dy.astype(BF16)
        l_ref[...] += jnp.sum(e * e, axis=0, keepdims=True)

    row = pl.BlockSpec((tm, D), lambda i: (i, 0))
    vec = pl.BlockSpec((1, D), lambda i: (0, 0))
    return pl.pallas_call(
        body, grid=(S // tm,), name=name, in_specs=[row, row], out_specs=[row, row, vec],
        out_shape=[jax.ShapeDtypeStruct((S, D), F32), jax.ShapeDtypeStruct((S, D), BF16),
                   jax.ShapeDtypeStruct((1, D), F32)],
        compiler_params=_params("arbitrary"),
    )(y, target)


def _rows2d(a):
    return a.reshape(-1, a.shape[-1])


def _ew_call(fn, ins, out_dtypes, name):
    R, C = ins[0].shape
    tr = _tile(R, max(8, (1 << 19) // C))
    spec = pl.BlockSpec((tr, C), lambda i: (i, 0))

    def body(*refs):
        outs = fn(*[r[...] for r in refs[:len(ins)]])
        for o_ref, o in zip(refs[len(ins):], outs):
            o_ref[...] = o.astype(o_ref.dtype)

    return pl.pallas_call(
        body, grid=(R // tr,), name=name, in_specs=[spec] * len(ins), out_specs=[spec] * len(out_dtypes),
        out_shape=[jax.ShapeDtypeStruct((R, C), d) for d in out_dtypes], compiler_params=_params("parallel"),
    )(*ins)


def adamw(w, g, m, v, name):
    def fn(w, g, m, v):
        m = ADAM_B1 * m + (1.0 - ADAM_B1) * g
        v = ADAM_B2 * v + (1.0 - ADAM_B2) * (g * g)
        m_hat = m / (1.0 - ADAM_B1 ** ADAM_STEP)
        v_hat = v / (1.0 - ADAM_B2 ** ADAM_STEP)
        return -ADAM_LR * (m_hat / (jnp.sqrt(v_hat) + ADAM_EPS) + ADAM_WD * w), m, v

    shp = w.shape
    outs = _ew_call(fn, [_rows2d(a) for a in (w, g, m, v)], [F32, F32, F32], name)
    return [o.reshape(shp) for o in outs]


def _placed_call(fn, place, grid, ins, in_specs, out_shape, out_specs, name, dep=None):
    n = len(ins)
    deps = [] if dep is None else [dep]

    def body(place_ref, *refs):
        outs = fn(*[r[...] for r in refs[:n]])
        for o_ref, o in zip(refs[n + len(deps):], outs):
            o_ref[...] = o.astype(o_ref.dtype)

    return pl.pallas_call(
        body, name=name, out_shape=out_shape,
        grid_spec=pltpu.PrefetchScalarGridSpec(
            num_scalar_prefetch=1, grid=grid, in_specs=list(in_specs) + [ANY] * len(deps), out_specs=out_specs),
        compiler_params=_params(*["parallel"] * len(grid)),
    )(place, *ins, *deps)


def _row_tile(R, C):
    return _tile(R, max(16, (1 << 19) // C))


def cast_into_full(w, place, name, dep=None):
    R, C = w.shape
    tr = _row_tile(R, C)
    return _placed_call(
        lambda a: (a,), place, (R // tr,), [w], [pl.BlockSpec((tr, C), lambda i, p: (i, 0))],
        [jax.ShapeDtypeStruct((N_CHIP, R, C), BF16)], [pl.BlockSpec((None, tr, C), lambda i, p: (p[0], i, 0))],
        name, dep=dep)[0]


def pair_sum(g4, got, place, name):
    _, hr, C = got.shape
    tr = _row_tile(hr, C)
    nb = hr // tr
    blk = pl.BlockSpec((None, tr, C), lambda s, i, p: (s, i, 0))
    return _placed_call(
        lambda a, b: (a + b, a + b), place, (N_CHIP, nb), [g4, got],
        [pl.BlockSpec((None, tr, C), lambda s, i, p: (s, p[1] * nb + i, 0)), blk],
        [jax.ShapeDtypeStruct(got.shape, F32), jax.ShapeDtypeStruct(got.shape, BF16)], [blk, blk], name)


def chip_sum(p32, got, place, name, dep=None):
    _, H, C = p32.shape
    tr = _row_tile(H, C)
    nb = H // tr
    peer = lambda j: pl.BlockSpec((None, tr, C), lambda i, p, j=j: (j, i, 0))
    return _placed_call(
        lambda a, b, c, d: (((a + b.astype(F32)) + c.astype(F32)) + d.astype(F32),), place, (nb,),
        [p32, got, got, got], [pl.BlockSpec((None, tr, C), lambda i, p: (p[0], i, 0)), peer(0), peer(1), peer(2)],
        [jax.ShapeDtypeStruct((2 * H, C), F32)], [pl.BlockSpec((tr, C), lambda i, p: (p[1] * nb + i, 0))],
        name, dep=dep)[0]


ANY = pl.BlockSpec(memory_space=pl.ANY)
CHIP_FLIPS = ((1, 0), (0, 1), (1, 1))


def _place():
    return lax.axis_index("x"), lax.axis_index("y"), lax.axis_index("c")


def _flip(v, f):
    return 1 - v if f else v


def exchange_halves(grads, name):
    n = len(grads)

    def body(*refs):
        ins, got = refs[:n], refs[n:2 * n]
        send_sem, recv_sem = refs[2 * n:]
        x, y, c = _place()
        copies = []
        for a in range(n):
            hr = ins[a].shape[1] // 2
            cp = pltpu.make_async_remote_copy(
                src_ref=ins[a].at[:, pl.ds((1 - c) * hr, hr), :], dst_ref=got[a],
                send_sem=send_sem.at[a], recv_sem=recv_sem.at[a], device_id=(x, y, 1 - c), device_id_type=MESH)
            cp.start()
            copies.append(cp)
        for cp in copies:
            cp.wait()

    dma = pltpu.SemaphoreType.DMA
    return pl.pallas_call(
        body, name=name, in_specs=[ANY] * n, out_specs=[ANY] * n,
        out_shape=[jax.ShapeDtypeStruct((N_CHIP, g.shape[1] // 2, g.shape[2]), g.dtype) for g in grads],
        scratch_shapes=[dma((n,)), dma((n,))],
    )(*grads)


def join_halves(fulls, name):
    n = len(fulls)

    def body(*refs):
        outs = refs[n:2 * n]
        send_sem, recv_sem = refs[2 * n:]
        x, y, c = _place()
        copies = []
        for a in range(n):
            hr = outs[a].shape[0] // 2
            half = outs[a].at[pl.ds(c * hr, hr), :]
            cp = pltpu.make_async_remote_copy(
                src_ref=half, dst_ref=half, send_sem=send_sem.at[a], recv_sem=recv_sem.at[a],
                device_id=(x, y, 1 - c), device_id_type=MESH)
            cp.start()
            copies.append(cp)
        for a, cp in enumerate(copies):
            hr = outs[a].shape[0] // 2
            theirs = outs[a].at[pl.ds((1 - c) * hr, hr), :]
            cp.wait_send()
            pltpu.make_async_remote_copy(
                src_ref=theirs, dst_ref=theirs, send_sem=send_sem.at[a], recv_sem=recv_sem.at[a],
                device_id=(x, y, 1 - c), device_id_type=MESH).wait_recv()

    dma = pltpu.SemaphoreType.DMA
    return pl.pallas_call(
        body, name=name, in_specs=[ANY] * n, out_specs=[ANY] * n,
        out_shape=[jax.ShapeDtypeStruct(f.shape, F32) for f in fulls],
        input_output_aliases={a: a for a in range(n)},
        scratch_shapes=[dma((n,)), dma((n,))],
    )(*fulls)


HBM = pl.BlockSpec(memory_space=pltpu.HBM)
SEM = pl.BlockSpec(memory_space=pltpu.SEMAPHORE)
EFFECT = pltpu.SideEffectType.DATAFLOW_SIDE_EFFECTING


def _in_hbm(a):
    return pltpu.with_memory_space_constraint(a, pltpu.HBM)


def _gather_half(ref, chip_idx, core):
    hr = ref.shape[1] // 2
    return ref.at[chip_idx, pl.ds(core * hr, hr), :]


def gather_forward(fulls, small, name):
    n = len(fulls)

    def body(*refs):
        small_in = refs[n]
        outs, small_out = refs[n + 1:2 * n + 1], refs[2 * n + 1]
        send_sem, recv_sem, sm_send, sm_recv, loc_sem = refs[2 * n + 2:]
        x, y, c = _place()
        mine = 2 * x + y
        chips = [(_flip(x, fx), _flip(y, fy)) for fx, fy in CHIP_FLIPS]
        local = pltpu.make_async_copy(small_in, small_out.at[mine], loc_sem)
        local.start()
        copies = []
        for j, (px, py) in enumerate(chips):
            cp = pltpu.make_async_remote_copy(
                src_ref=small_in, dst_ref=small_out.at[mine], send_sem=sm_send.at[j], recv_sem=sm_recv.at[j],
                device_id=(px, py, c), device_id_type=MESH)
            cp.start()
            copies.append(cp)
        for a in range(n):
            for j, (px, py) in enumerate(chips):
                src = _gather_half(outs[a], 2 * px + py, c)
                cp = pltpu.make_async_remote_copy(
                    src_ref=src, dst_ref=src, send_sem=send_sem.at[3 * a + j], recv_sem=recv_sem.at[3 * a + j],
                    device_id=(x, y, 1 - c), device_id_type=MESH)
                cp.start()
                copies.append(cp)
        for a in range(n):
            for j, (px, py) in enumerate(chips):
                dst = _gather_half(outs[a], 2 * px + py, 1 - c)
                pltpu.make_async_remote_copy(
                    src_ref=dst, dst_ref=dst, send_sem=send_sem.at[3 * a + j], recv_sem=recv_sem.at[3 * a + j],
                    device_id=(x, y, 1 - c), device_id_type=MESH).wait_recv()
        for j, (px, py) in enumerate(chips):
            dst = small_out.at[2 * px + py]
            pltpu.make_async_remote_copy(
                src_ref=dst, dst_ref=dst, send_sem=sm_send.at[j], recv_sem=sm_recv.at[j],
                device_id=(px, py, c), device_id_type=MESH).wait_recv()
        for cp in copies:
            cp.wait_send()
        local.wait()

    dma = pltpu.SemaphoreType.DMA
    out = pl.pallas_call(
        body, name=name, in_specs=[ANY] * (n + 1), out_specs=[ANY] * (n + 1),
        out_shape=[jax.ShapeDtypeStruct(f.shape, f.dtype) for f in fulls]
        + [jax.ShapeDtypeStruct((N_CHIP,) + small.shape, small.dtype)],
        input_output_aliases={a: a for a in range(n)},
        scratch_shapes=[dma((3 * n,)), dma((3 * n,)), dma((3,)), dma((3,)), dma],
    )(*fulls, small)
    return out[:n], out[n]


def _gather_plan(fulls, lands):
    x, y, c = _place()
    mine = 2 * x + y
    return [(_gather_half(f, mine, c), _gather_half(f, mine, c), (_flip(x, fx), _flip(y, fy), c))
            for f in fulls for fx, fy in CHIP_FLIPS]


def _scatter_plan(parts, lands):
    x, y, c = _place()
    plan = []
    for p, l in zip(parts, lands):
        for j, (fx, fy) in enumerate(CHIP_FLIPS):
            px, py = _flip(x, fx), _flip(y, fy)
            plan.append((p.at[2 * px + py], l.at[j], (px, py, c)))
    return plan


def _scatter_lands(parts):
    return [(3,) + p.shape[1:] for p in parts]


def _exchange_plan(grads, lands):
    x, y, c = _place()
    plan = []
    for g, l in zip(grads, lands):
        hr = g.shape[1] // 2
        plan.append((g.at[:, pl.ds((1 - c) * hr, hr), :], l, (x, y, 1 - c)))
    return plan


def _exchange_lands(grads):
    return [(N_CHIP, g.shape[1] // 2, g.shape[2]) for g in grads]


def split_start(plan, copies, srcs, land_shapes, deps, name):
    n, m = len(srcs), len(land_shapes)
    lands = [lax.empty(s, srcs[0].dtype) for s in land_shapes]

    def body(*refs):
        k0 = n + m + len(deps)
        send_sem, recv_sem = refs[k0], refs[k0 + 1]
        thru, token = refs[k0 + 2:k0 + 2 + n + m], refs[k0 + 2 + n + m]
        for k, (src, dst, dev) in enumerate(plan(thru[:n], thru[n:])):
            pltpu.make_async_remote_copy(src_ref=src, dst_ref=dst, send_sem=send_sem.at[k], recv_sem=recv_sem.at[k],
                                         device_id=dev, device_id_type=MESH).start()
        token[...] = jnp.zeros_like(token)

    dma = pltpu.SemaphoreType.DMA
    arrays = list(srcs) + lands
    out = pl.pallas_call(
        body, name=name,
        out_shape=(dma((copies,)), dma((copies,)), *[pltpu.HBM(a.shape, a.dtype) for a in arrays],
                   jax.ShapeDtypeStruct((8, LANES), F32)),
        in_specs=[HBM] * (n + m) + [ANY] * len(deps),
        out_specs=(SEM, SEM, *[HBM] * (n + m), pl.BlockSpec(memory_space=pltpu.VMEM)),
        input_output_aliases={a: 2 + a for a in range(n + m)},
        compiler_params=pltpu.CompilerParams(has_side_effects=EFFECT),
    )(*[_in_hbm(a) for a in arrays], *deps)
    return (out[0], out[1], list(out[2:2 + n]), list(out[2 + n:2 + n + m])), out[2 + n + m]


def split_wait(plan, handle, after, name):
    send_sem, recv_sem, srcs, lands = handle
    n, m = len(srcs), len(lands)

    def body(*refs):
        send_sem, recv_sem = refs[n + m], refs[n + m + 1]
        thru = refs[n + m + 2 + len(after):]
        for k, (src, dst, dev) in enumerate(plan(thru[:n], thru[n:])):
            cp = pltpu.make_async_remote_copy(src_ref=src, dst_ref=dst, send_sem=send_sem.at[k],
                                              recv_sem=recv_sem.at[k], device_id=dev, device_id_type=MESH)
            cp.wait_send()
            cp.wait_recv()

    arrays = list(srcs) + list(lands)
    out = pl.pallas_call(
        body, name=name, out_shape=tuple(pltpu.HBM(a.shape, a.dtype) for a in arrays),
        in_specs=[HBM] * (n + m) + [SEM, SEM] + [ANY] * len(after), out_specs=tuple([HBM] * (n + m)),
        input_output_aliases={a: a for a in range(n + m)},
        compiler_params=pltpu.CompilerParams(has_side_effects=EFFECT),
    )(*arrays, send_sem, recv_sem, *after)
    return list(out[:n]), list(out[n:])


def allreduce_small(block, name):
    R, C = block.shape

    def body(in_ref, out_ref, slots, send_sem, recv_sem):
        x, y, c = _place()
        me = 4 * x + 2 * y + c
        slots[me] = in_ref[...]
        copies = []
        for r in range(1, 8):
            fx, fy, fc = (r >> 2) & 1, (r >> 1) & 1, r & 1
            cp = pltpu.make_async_remote_copy(
                src_ref=in_ref, dst_ref=slots.at[me], send_sem=send_sem.at[r - 1], recv_sem=recv_sem.at[r - 1],
                device_id=(_flip(x, fx), _flip(y, fy), _flip(c, fc)), device_id_type=MESH)
            cp.start()
            copies.append(cp)
        for cp in copies:
            cp.wait()
        acc = slots[0]
        for d in range(1, 8):
            acc = acc + slots[d]
        out_ref[...] = acc

    vm = pl.BlockSpec(memory_space=pltpu.VMEM)
    dma = pltpu.SemaphoreType.DMA
    return pl.pallas_call(
        body, name=name, in_specs=[vm], out_specs=vm, out_shape=jax.ShapeDtypeStruct((R, C), F32),
        scratch_shapes=[pltpu.VMEM((8, R, C), F32), dma((7,)), dma((7,))],
    )(block)


def local_step(x, mem, target, g_mix, g_mem, q_norm_g, k_norm_g, g_mlp, conv_w, h, mem_n, w_in, w_in_dep,
               rest_weights, early_grads, mid_grads):
    S, D = x.shape
    proj = mm_nn_shard(h, w_in, "proj", dep=w_in_dep)
    a_conv = conv_fwd(proj, conv_w, D, "conv_fwd")
    o_sb = sb_fwd(proj, D, "sb_fwd")
    w_conv_out, w_sb_out, w_mem_kv, w_x_out, w_out, w_up, w_down = rest_weights(o_sb)
    kv = mm_nn_shard(mem_n, w_mem_kv, "kv")
    o_x = xa_fwd(proj, kv, q_norm_g, k_norm_g, D, "xa_fwd")
    ys = [mm_nn(a_conv, w_conv_out, "y_conv"), mm_nn(o_sb, w_sb_out, "y_sb"), mm_nn(o_x, w_x_out, "y_x")]
    merged = merge_fwd(proj, ys, D, "merge_fwd")
    x1 = mm_nn(merged, w_out, "x1", res=x)
    h2 = rms_fwd(x1, g_mlp, "rms_mlp")
    up, act = mm_nn_shard(h2, w_up, "up", relu2=True)
    x2 = mm_nn(act, w_down, "x2", res=x1)
    dy, dy16, loss_cols = loss_head(x2, target, "loss_head")
    d_up = mm_nt(dy16, w_down, "d_up", up=up, out_dtype=BF16)
    g = {"w_down": mm_tn(act, dy16, "g_w_down")}
    g["w_up"] = mm_tn(h2, d_up, "g_w_up", shard_out=True)
    dh2 = mm_nt_shard(d_up, w_up, "dh2")
    dx1, dx1_16, g["g_mlp"] = rms_bwd(x1, g_mlp, dh2, "rms_mlp_bwd", dres=dy, want16=True)
    g["w_out"] = mm_tn(merged, dx1_16, "g_w_out")
    dm = mm_nt(dx1_16, w_out, "d_merged")
    dy_c, dy_s, dy_x, d_gate = merge_bwd(proj, ys, dm, D, "merge_bwd")
    g["w_conv_out"] = mm_tn(a_conv, dy_c, "g_w_conv_out")
    g["w_sb_out"] = mm_tn(o_sb, dy_s, "g_w_sb_out")
    g["w_x_out"] = mm_tn(o_x, dy_x, "g_w_x_out")
    d_xq, d_kv, g["q_norm_g"], g["k_norm_g"] = xa_bwd(
        proj, kv, q_norm_g, k_norm_g, mm_nt(dy_x, w_x_out, "d_o_x"), D, "xa_bwd")
    g["w_mem_kv"] = mm_tn(mem_n, d_kv, "g_w_mem_kv", shard_out=True)
    g["g_mem"] = rms_bwd(mem, g_mem, mm_nt_shard(d_kv, w_mem_kv, "d_mem_n"), "rms_mem_bwd", want_dx=False)
    dep = early_grads(g)
    d_a_conv = mm_nt(dy_c, w_conv_out, "d_a_conv", dep=dep)
    d_ch, d_cb, d_cc, g["conv_w"] = conv_bwd(proj, conv_w, d_a_conv, D, "conv_bwd")
    d_o_sb = mm_nt(dy_s, w_sb_out, "d_o_sb", dep=dep)
    dq, dk, dv = sb_bwd(proj, d_o_sb, D, "sb_bwd", dep=mid_grads([d_ch, d_o_sb]))
    d_proj = jnp.concatenate([d_ch, d_cb, d_cc, dq, dk, dv, d_xq, d_gate], axis=1)
    g["w_in"] = mm_tn(h, d_proj, "g_w_in", shard_out=True)
    dh = mm_nt_shard(d_proj, w_in, "dh")
    grad_x, g["g_mix"] = rms_bwd(x, g_mix, dh, "rms_mix_bwd", dres=dx1)
    return loss_cols, grad_x, g


BIG = ("w_in", "w_conv_out", "w_sb_out", "w_mem_kv", "w_x_out", "w_out", "w_up", "w_down")
REST = BIG[1:]
COL_SHARDED = ("w_in", "w_mem_kv", "w_up")
WEIGHTS = ("g_mix", "g_mem", "w_in", "conv_w", "w_conv_out", "w_sb_out", "q_norm_g", "k_norm_g",
           "w_mem_kv", "w_x_out", "w_out", "g_mlp", "w_up", "w_down")


def _pack_small(D, g_mix, g_mem, g_mlp, q_norm_g, k_norm_g, conv_w, last):
    qk = jnp.concatenate([q_norm_g, k_norm_g, jnp.zeros((1, D - 2 * X_HEAD_DIM), F32)], axis=1)
    cw = jnp.pad(conv_w, ((0, 0), (0, D - conv_w.shape[1])))
    return jnp.concatenate([g_mix, g_mem, g_mlp, qk, cw, last], axis=0)


def kernel(x, mem, g_mix, g_mem, w_in, conv_w, w_conv_out, w_sb_out, q_norm_g, k_norm_g, w_mem_kv, w_x_out, w_out, g_mlp, w_up, w_down, loss_target, m_g_mix, m_g_mem, m_w_in, m_conv_w, m_w_conv_out, m_w_sb_out, m_q_norm_g, m_k_norm_g, m_w_mem_kv, m_w_x_out, m_w_out, m_g_mlp, m_w_up, m_w_down, v_g_mix, v_g_mem, v_w_in, v_conv_w, v_w_conv_out, v_w_sb_out, v_q_norm_g, v_k_norm_g, v_w_mem_kv, v_w_x_out, v_w_out, v_g_mlp, v_w_up, v_w_down):
    S, D = x.shape[1], x.shape[2]
    w = dict(g_mix=g_mix, g_mem=g_mem, w_in=w_in, conv_w=conv_w, w_conv_out=w_conv_out, w_sb_out=w_sb_out,
             q_norm_g=q_norm_g, k_norm_g=k_norm_g, w_mem_kv=w_mem_kv, w_x_out=w_x_out, w_out=w_out,
             g_mlp=g_mlp, w_up=w_up, w_down=w_down)
    m = dict(g_mix=m_g_mix, g_mem=m_g_mem, w_in=m_w_in, conv_w=m_conv_w, w_conv_out=m_w_conv_out,
             w_sb_out=m_w_sb_out, q_norm_g=m_q_norm_g, k_norm_g=m_k_norm_g, w_mem_kv=m_w_mem_kv,
             w_x_out=m_w_x_out, w_out=m_w_out, g_mlp=m_g_mlp, w_up=m_w_up, w_down=m_w_down)
    v = dict(g_mix=v_g_mix, g_mem=v_g_mem, w_in=v_w_in, conv_w=v_conv_w, w_conv_out=v_w_conv_out,
             w_sb_out=v_w_sb_out, q_norm_g=v_q_norm_g, k_norm_g=v_k_norm_g, w_mem_kv=v_w_mem_kv,
             w_x_out=v_w_x_out, w_out=v_w_out, g_mlp=v_g_mlp, w_up=v_w_up, w_down=v_w_down)
    chip = 2 * lax.axis_index("x") + lax.axis_index("y")
    cs = conv_w.shape[2]

    place = jnp.stack([chip, lax.axis_index("c")]).astype(jnp.int32)
    cw_block = jnp.pad(conv_w[0], ((0, 5), (0, 0)))
    handle, token = split_start(_gather_plan, 3, [cast_into_full(w["w_in"][0], place, "cast_w_in")], [], [],
                                "gather_w_in_start")
    rest16 = [cast_into_full(w[k][0], place, "cast_" + k, dep=token) for k in REST]
    h = rms_fwd(x[0], g_mix, "rms_mix", dep=token)
    mem_n = rms_fwd(mem[0], g_mem, "rms_mem", dep=token)
    landed, _ = split_wait(_gather_plan, handle, [*rest16, h, mem_n], "gather_w_in_wait")
    (w_in_full,), cw_all = gather_forward(landed, cw_block, "gather_w_in_forward")
    conv_full = jnp.concatenate([cw_all[p, :3] for p in range(N_CHIP)], axis=1)
    rest_handle, rest_token = split_start(_gather_plan, 3 * len(REST), rest16, [], [w_in_full], "gather_rest_start")

    def layout(k, a):
        return a if k in COL_SHARDED else a.reshape(-1, a.shape[-1])

    def rest_weights(after):
        landed, _ = split_wait(_gather_plan, rest_handle, [after], "gather_rest_wait")
        return [layout(k, a) for k, a in zip(REST, gather_forward(landed, cw_block, "gather_rest_forward")[0])]

    def blocks(k, a):
        return a if k in COL_SHARDED else a.reshape(N_CHIP, -1, a.shape[-1])

    early = {}

    def early_grads(g):
        early["g4"] = [blocks(k, g[k]) for k in REST]
        early["swap"], token = split_start(_exchange_plan, len(REST), early["g4"], _exchange_lands(early["g4"]),
                                           [g["g_mem"]], "exchange_rest_start")
        return token

    def mid_grads(after):
        _, got = split_wait(_exchange_plan, early["swap"], after, "exchange_rest_wait")
        early["pairs"] = [pair_sum(a, b, place, "pair_sum_" + k) for k, a, b in zip(REST, early["g4"], got)]
        p16 = [p[1] for p in early["pairs"]]
        early["fly"], token = split_start(_scatter_plan, 3 * len(REST), p16, _scatter_lands(p16), [],
                                          "scatter_rest_start")
        return token

    loss_cols, grad_x, g = local_step(
        x[0], mem[0], loss_target[0], g_mix, g_mem, q_norm_g, k_norm_g, g_mlp, conv_full, h, mem_n,
        w_in_full, rest_token, rest_weights, early_grads, mid_grads)

    small = allreduce_small(
        _pack_small(D, g["g_mix"], g["g_mem"], g["g_mlp"], g["q_norm_g"], g["k_norm_g"], g["conv_w"], loss_cols),
        "allreduce_small")
    loss = (0.5 / D) * jnp.sum(small[7])
    gsum = {"g_mix": small[0:1], "g_mem": small[1:2], "g_mlp": small[2:3],
            "q_norm_g": small[3:4, :X_HEAD_DIM], "k_norm_g": small[3:4, X_HEAD_DIM:2 * X_HEAD_DIM],
            "conv_w": lax.dynamic_slice(small[4:7], (0, chip * cs), (3, cs))[None]}

    _, got_rest = split_wait(_scatter_plan, early["fly"], [g["w_in"]], "scatter_rest_wait")
    got_in = exchange_halves([g["w_in"]], "exchange_halves_w_in")[0]
    pair_in = pair_sum(g["w_in"], got_in, place, "pair_sum_w_in")
    fly_in, token = split_start(_scatter_plan, 3, [pair_in[1]], _scatter_lands([pair_in[1]]), [got_rest[0]],
                                "scatter_w_in_start")

    delta, new_m, new_v = {}, {}, {}
    halves = [chip_sum(p[0], b, place, "chip_sum_" + k, dep=token)
              for k, p, b in zip(REST, early["pairs"], got_rest)]
    for k, a in zip(REST, join_halves(halves, "join_halves_rest")):
        gsum[k] = a[None]
        delta[k], new_m[k], new_v[k] = adamw(w[k], gsum[k], m[k], v[k], "adamw_" + k)
    small_names = ("g_mix", "g_mem", "g_mlp", "q_norm_g", "k_norm_g", "conv_w")
    zero_row = jnp.zeros((1, D), F32)
    packed = [_pack_small(D, *[t[k] if k != "conv_w" else t[k][0] for k in small_names], zero_row)
              for t in (w, gsum, m, v)]
    sm = adamw(*packed, "adamw_small")
    for t, block in zip((delta, new_m, new_v), sm):
        t["g_mix"], t["g_mem"], t["g_mlp"] = block[0:1], block[1:2], block[2:3]
        t["q_norm_g"], t["k_norm_g"] = block[3:4, :X_HEAD_DIM], block[3:4, X_HEAD_DIM:2 * X_HEAD_DIM]
        t["conv_w"] = block[4:7, :cs][None]

    done = [new_v[k] for k in REST] + [sm[2]]
    got_in = split_wait(_scatter_plan, fly_in, done, "scatter_w_in_wait")[1][0]
    half_in = chip_sum(pair_in[0], got_in, place, "chip_sum_w_in")
    gsum["w_in"] = join_halves([half_in], "join_halves_w_in")[0][None]
    delta["w_in"], new_m["w_in"], new_v["w_in"] = adamw(w["w_in"], gsum["w_in"], m["w_in"], v["w_in"], "adamw_w_in")

    return (loss, grad_x[None], *[gsum[k] for k in WEIGHTS], *[delta[k] for k in WEIGHTS],
            *[new_m[k] for k in WEIGHTS], *[new_v[k] for k in WEIGHTS])
```

```python
import functools

import jax
import jax.numpy as jnp
from jax import lax
from jax.experimental import pallas as pl
from jax.experimental.pallas import tpu as pltpu

F32 = jnp.float32
BF16 = jnp.bfloat16
EPS = 1e-6
N_CHIP = 4
SB_HEAD_DIM = 64
X_HEAD_DIM = 256
LANES = 128
VMEM_LIMIT = 56 * 1024 * 1024
ADAM_LR, ADAM_B1, ADAM_B2, ADAM_EPS, ADAM_WD, ADAM_STEP = 0.001, 0.9, 0.999, 1e-8, 0.01, 10
MESH = pl.DeviceIdType.MESH


def _params(*sem):
    return pltpu.CompilerParams(dimension_semantics=sem, vmem_limit_bytes=VMEM_LIMIT)


def _tile(n, pref):
    if n <= pref:
        return n
    t = 1 << (pref.bit_length() - 1)
    while n % t:
        t //= 2
    return t


NN = (((1,), (0,)), ((), ()))
NT = (((1,), (1,)), ((), ()))
TN = (((0,), (0,)), ((), ()))


def _dot(a, b, dims):
    return lax.dot_general(a.astype(BF16), b.astype(BF16), dims, preferred_element_type=F32)


def mm_nn_shard(a, g, name, relu2=False, dep=None):
    M, K = a.shape
    _, _, Ns = g.shape
    tm, tn = _tile(M, 2048), _tile(Ns, 512)
    nb = Ns // tn

    def body(a_ref, b_ref, *o_refs):
        o_refs = o_refs[len(deps):]
        acc = _dot(a_ref[...], b_ref[...], NN)
        o_refs[0][...] = acc
        if relu2:
            r = jnp.maximum(acc, 0.0)
            o_refs[1][...] = (r * r).astype(BF16)

    o_spec = pl.BlockSpec((tm, tn), lambda i, j: (i, j))
    shapes = [jax.ShapeDtypeStruct((M, N_CHIP * Ns), F32)]
    specs = [o_spec]
    if relu2:
        shapes.append(jax.ShapeDtypeStruct((M, N_CHIP * Ns), BF16))
        specs.append(o_spec)
    deps = [] if dep is None else [dep]
    out = pl.pallas_call(
        body, grid=(M // tm, N_CHIP * nb), name=name,
        in_specs=[pl.BlockSpec((tm, K), lambda i, j: (i, 0)),
                  pl.BlockSpec((None, K, tn), lambda i, j: (j // nb, 0, j % nb))] + [ANY] * len(deps),
        out_specs=specs, out_shape=shapes, compiler_params=_params("parallel", "parallel"),
    )(a, g, *deps)
    return out if relu2 else out[0]


def mm_nn(a, w, name, res=None, out_dtype=F32):
    M, K = a.shape
    N = w.shape[1]
    tm, tn = _tile(M, 2048 if K <= 2048 else 1024), _tile(N, 512)

    def body(a_ref, b_ref, *refs):
        acc = _dot(a_ref[...], b_ref[...], NN)
        if res is not None:
            acc = refs[0][...] + acc
        refs[-1][...] = acc.astype(out_dtype)

    o_spec = pl.BlockSpec((tm, tn), lambda i, j: (i, j))
    ins = [a, w] + ([res] if res is not None else [])
    return pl.pallas_call(
        body, grid=(M // tm, N // tn), name=name,
        in_specs=[pl.BlockSpec((tm, K), lambda i, j: (i, 0)), pl.BlockSpec((K, tn), lambda i, j: (0, j))]
        + ([o_spec] if res is not None else []),
        out_specs=o_spec, out_shape=jax.ShapeDtypeStruct((M, N), out_dtype),
        compiler_params=_params("parallel", "parallel"),
    )(*ins)


def mm_nt(a, w, name, up=None, out_dtype=F32, dep=None):
    M, N = a.shape
    R = w.shape[0]
    tm, tr = _tile(M, 2048), _tile(R, 512)

    def body(a_ref, b_ref, *refs):
        acc = _dot(a_ref[...], b_ref[...], NT)
        if up is not None:
            acc = acc * (2.0 * jnp.maximum(refs[0][...], 0.0))
        refs[-1][...] = acc.astype(out_dtype)

    o_spec = pl.BlockSpec((tm, tr), lambda i, j: (i, j))
    ins = [a, w] + ([up] if up is not None else []) + ([dep] if dep is not None else [])
    return pl.pallas_call(
        body, grid=(M // tm, R // tr), name=name,
        in_specs=[pl.BlockSpec((tm, N), lambda i, j: (i, 0)), pl.BlockSpec((tr, N), lambda i, j: (j, 0))]
        + ([o_spec] if up is not None else []) + ([ANY] if dep is not None else []),
        out_specs=o_spec, out_shape=jax.ShapeDtypeStruct((M, R), out_dtype),
        compiler_params=_params("parallel", "parallel"),
    )(*ins)


def mm_nt_shard(a, g, name, out_dtype=F32):
    M = a.shape[0]
    _, R, Ns = g.shape
    tm, tr, tk = _tile(M, 1024), _tile(R, 1024), _tile(Ns, 2560)
    nb = Ns // tk
    nk = N_CHIP * nb

    def body(a_ref, b_ref, o_ref, acc_ref):
        k = pl.program_id(2)

        @pl.when(k == 0)
        def _():
            acc_ref[...] = jnp.zeros_like(acc_ref)

        acc_ref[...] += _dot(a_ref[...], b_ref[...], NT)

        @pl.when(k == nk - 1)
        def _():
            o_ref[...] = acc_ref[...].astype(out_dtype)

    return pl.pallas_call(
        body, grid=(M // tm, R // tr, nk), name=name,
        in_specs=[pl.BlockSpec((tm, tk), lambda i, j, k: (i, k)),
                  pl.BlockSpec((None, tr, tk), lambda i, j, k: (k // nb, j, k % nb))],
        out_specs=pl.BlockSpec((tm, tr), lambda i, j, k: (i, j)),
        out_shape=jax.ShapeDtypeStruct((M, R), out_dtype),
        scratch_shapes=[pltpu.VMEM((tm, tr), F32)],
        compiler_params=_params("parallel", "parallel", "arbitrary"),
    )(a, g)


def mm_tn(a, b, name, shard_out=False):
    S, M = a.shape
    N = b.shape[1]
    Ns = N // N_CHIP if shard_out else N
    tm, tn = _tile(M, 1024), _tile(Ns, 512)
    nb = Ns // tn

    def body(a_ref, b_ref, o_ref):
        o_ref[...] = _dot(a_ref[...], b_ref[...], TN)

    if shard_out:
        o_spec = pl.BlockSpec((None, tm, tn), lambda i, j: (j // nb, i, j % nb))
        o_shape = jax.ShapeDtypeStruct((N_CHIP, M, Ns), F32)
    else:
        o_spec = pl.BlockSpec((tm, tn), lambda i, j: (i, j))
        o_shape = jax.ShapeDtypeStruct((M, N), F32)
    return pl.pallas_call(
        body, grid=(M // tm, N // tn), name=name,
        in_specs=[pl.BlockSpec((S, tm), lambda i, j: (0, i)), pl.BlockSpec((S, tn), lambda i, j: (0, j))],
        out_specs=o_spec, out_shape=o_shape, compiler_params=_params("parallel", "parallel"),
    )(a, b)


def rms_fwd(x, g, name, dep=None):
    S, D = x.shape
    tm = _tile(S, 512)
    deps = [] if dep is None else [dep]

    def body(x_ref, g_ref, *refs):
        xv = x_ref[...]
        r = lax.rsqrt(jnp.mean(xv * xv, axis=-1, keepdims=True) + EPS)
        refs[-1][...] = ((xv * r) * g_ref[...]).astype(BF16)

    return pl.pallas_call(
        body, grid=(S // tm,), name=name,
        in_specs=[pl.BlockSpec((tm, D), lambda i: (i, 0)), pl.BlockSpec((1, D), lambda i: (0, 0))]
        + [ANY] * len(deps),
        out_specs=pl.BlockSpec((tm, D), lambda i: (i, 0)),
        out_shape=jax.ShapeDtypeStruct((S, D), BF16), compiler_params=_params("parallel"),
    )(x, g, *deps)


def rms_bwd(x, g, dh, name, dres=None, want_dx=True, want16=False):
    S, D = x.shape
    tm = _tile(S, 512)

    def body(x_ref, g_ref, dh_ref, *refs):
        i = pl.program_id(0)
        xv = x_ref[...]
        r = lax.rsqrt(jnp.mean(xv * xv, axis=-1, keepdims=True) + EPS)
        xn = xv * r
        dhv = dh_ref[...].astype(F32)
        gg_ref = refs[-1]

        @pl.when(i == 0)
        def _():
            gg_ref[...] = jnp.zeros_like(gg_ref)

        gg_ref[...] += jnp.sum(dhv * xn, axis=0, keepdims=True)
        if want_dx:
            dxn = dhv * g_ref[...]
            dx = r * (dxn - xn * jnp.mean(dxn * xn, axis=-1, keepdims=True))
            if dres is not None:
                dx = refs[0][...] + dx
            refs[-2][...] = dx.astype(refs[-2].dtype)
            if want16:
                refs[-3][...] = dx

    row = pl.BlockSpec((tm, D), lambda i: (i, 0))
    vec = pl.BlockSpec((1, D), lambda i: (0, 0))
    ins, in_specs = [x, g, dh], [row, vec, row]
    if dres is not None:
        ins.append(dres)
        in_specs.append(row)
    shapes, specs = [jax.ShapeDtypeStruct((1, D), F32)], [vec]
    if want_dx:
        if want16:
            shapes.insert(0, jax.ShapeDtypeStruct((S, D), BF16))
            specs.insert(0, row)
        shapes.insert(0, jax.ShapeDtypeStruct((S, D), F32))
        specs.insert(0, row)
    out = pl.pallas_call(body, grid=(S // tm,), name=name, in_specs=in_specs, out_specs=specs,
                         out_shape=shapes, compiler_params=_params("arbitrary"))(*ins)
    return out if want_dx else out[0]


def _shift_down(u, k, row):
    return jnp.where(row >= k, pltpu.roll(u, k, axis=0), 0.0)


def _shift_up(u, k, row):
    S = u.shape[0]
    return jnp.where(row < S - k, pltpu.roll(u, S - k, axis=0), 0.0)


def _conv_specs(S, D, tc):
    nb = D // tc
    col = lambda o: pl.BlockSpec((S, tc), lambda j, o=o: (0, o * nb + j))
    return col, pl.BlockSpec((3, tc), lambda j: (0, j))


def conv_fwd(proj, conv_w, D, name):
    S = proj.shape[0]
    tc = _tile(D, 256)
    col, wspec = _conv_specs(S, D, tc)

    def body(ch_ref, cb_ref, cc_ref, w_ref, a_ref):
        row = lax.broadcasted_iota(jnp.int32, (S, tc), 0)
        u = cc_ref[...] * ch_ref[...]
        w = w_ref[...]
        cv = w[0:1, :] * _shift_down(u, 2, row) + w[1:2, :] * _shift_down(u, 1, row) + w[2:3, :] * u
        a_ref[...] = (cb_ref[...] * cv).astype(BF16)

    return pl.pallas_call(
        body, grid=(D // tc,), name=name, in_specs=[col(0), col(1), col(2), wspec],
        out_specs=pl.BlockSpec((S, tc), lambda j: (0, j)),
        out_shape=jax.ShapeDtypeStruct((S, D), BF16), compiler_params=_params("parallel"),
    )(proj, proj, proj, conv_w)


def conv_bwd(proj, conv_w, da, D, name):
    S = proj.shape[0]
    tc = _tile(D, 256)
    col, wspec = _conv_specs(S, D, tc)
    blk = pl.BlockSpec((S, tc), lambda j: (0, j))

    def body(ch_ref, cb_ref, cc_ref, w_ref, da_ref, dch_ref, dcb_ref, dcc_ref, gw_ref):
        row = lax.broadcasted_iota(jnp.int32, (S, tc), 0)
        ch, cb, cc, dav = ch_ref[...], cb_ref[...], cc_ref[...], da_ref[...]
        w = w_ref[...]
        u = cc * ch
        u1, u2 = _shift_down(u, 1, row), _shift_down(u, 2, row)
        cv = w[0:1, :] * u2 + w[1:2, :] * u1 + w[2:3, :] * u
        dcb_ref[...] = (dav * cv).astype(BF16)
        dcv = dav * cb
        gw_ref[0:1, :] = jnp.sum(dcv * u2, axis=0, keepdims=True)
        gw_ref[1:2, :] = jnp.sum(dcv * u1, axis=0, keepdims=True)
        gw_ref[2:3, :] = jnp.sum(dcv * u, axis=0, keepdims=True)
        du = w[2:3, :] * dcv + w[1:2, :] * _shift_up(dcv, 1, row) + w[0:1, :] * _shift_up(dcv, 2, row)
        dcc_ref[...] = (du * ch).astype(BF16)
        dch_ref[...] = (du * cc).astype(BF16)

    act = jax.ShapeDtypeStruct((S, D), BF16)
    return pl.pallas_call(
        body, grid=(D // tc,), name=name, in_specs=[col(0), col(1), col(2), wspec, blk],
        out_specs=[blk, blk, blk, wspec], out_shape=[act, act, act, jax.ShapeDtypeStruct((3, D), F32)],
        compiler_params=_params("parallel"),
    )(proj, proj, proj, conv_w, da)


SB_BQ_FWD = 512
SB_BQ_BWD = 256
SB_BK = 128
SB_GROUP = 4


def _sb_consts(bq):
    lane = lax.broadcasted_iota(jnp.int32, (bq, LANES), 1)
    r = lax.broadcasted_iota(jnp.int32, (SB_BK, SB_BK), 0)
    c = lax.broadcasted_iota(jnp.int32, (SB_BK, SB_BK), 1)
    tri_rev = jnp.where(r > c, 1.0, 0.0).astype(BF16)
    tri_fwd = jnp.where(r < c, 1.0, 0.0).astype(BF16)
    return lane, tri_rev, tri_fwd


def _cumsum2(v, tri):
    hi = v.astype(BF16)
    lo = (v - hi.astype(F32)).astype(BF16)
    part = (lax.dot_general(hi, tri, NN, preferred_element_type=F32)
            + lax.dot_general(lo, tri, NN, preferred_element_type=F32))
    return part, jnp.sum(v, axis=1, keepdims=True)


def _sb_logits(z, past):
    sp = jnp.log(1.0 + jnp.exp(-jnp.abs(z)))
    l = jnp.minimum(z, 0.0) - sp
    m = l - z
    if past is not None:
        m = jnp.where(past, m, 0.0)
    return l, m


def _stack_heads(v, lane):
    return jnp.concatenate([jnp.where(lane < SB_HEAD_DIM, v, 0.0), jnp.where(lane >= SB_HEAD_DIM, v, 0.0)],
                           axis=0).astype(BF16)


def _unstack_heads(v, lane):
    bq = v.shape[0] // 2
    return jnp.where(lane < SB_HEAD_DIM, v[:bq], v[bq:])


def _sb_positions(i, bq):
    r = lax.broadcasted_iota(jnp.int32, (2 * bq, SB_BK), 0)
    trow = i * bq + jnp.where(r >= bq, r - bq, r)
    return trow, lax.broadcasted_iota(jnp.int32, (2 * bq, SB_BK), 1)


def _sb_specs(S, D, bq):
    npair = D // LANES
    qspec = pl.BlockSpec((bq, LANES), lambda p, i: (i, 3 * npair + p))
    kspec = pl.BlockSpec((S, LANES), lambda p, i: (0, 4 * npair + p))
    vspec = pl.BlockSpec((S, LANES), lambda p, i: (0, 5 * npair + p))
    return npair, qspec, kspec, vspec


def sb_fwd(proj, D, name):
    S = proj.shape[0]
    bq = min(SB_BQ_FWD, S)
    nd = bq // SB_BK
    npair, qspec, kspec, vspec = _sb_specs(S, D, bq)
    scale = SB_HEAD_DIM ** -0.5

    def body(q_ref, k_ref, v_ref, o_ref, kb_ref, vb_ref):
        i = pl.program_id(1)

        @pl.when(i == 0)
        def _():
            kb_ref[...] = k_ref[...].astype(BF16)
            vb_ref[...] = v_ref[...].astype(BF16)

        lane, tri_rev, _ = _sb_consts(bq)
        qs = _stack_heads(q_ref[...] * scale, lane)
        trow, scol = _sb_positions(i, bq)

        def steps(j0, carry, n, masked):
            ks = [pl.multiple_of((j0 - t) * SB_BK, SB_BK) for t in range(n)]
            past = [(k + scol) < trow if masked else None for k in ks]
            zs = [lax.dot_general(qs, kb_ref[pl.ds(k, SB_BK), :], NT, preferred_element_type=F32) for k in ks]
            lm = [_sb_logits(z, p) for z, p in zip(zs, past)]
            cs = [_cumsum2(m, tri_rev) for _, m in lm]
            c, acc = carry
            for t in range(n):
                a = jnp.exp(lm[t][0] + (cs[t][0] + c))
                if masked:
                    a = jnp.where(past[t], a, 0.0)
                acc = acc + lax.dot_general(a.astype(BF16), vb_ref[pl.ds(ks[t], SB_BK), :], NN,
                                            preferred_element_type=F32)
                c = c + cs[t][1]
            return c, acc

        carry = (jnp.zeros((2 * bq, 1), F32), jnp.zeros((2 * bq, LANES), F32))
        carry = steps(i * nd + nd - 1, carry, nd, True)
        older = i * nd
        groups = older // SB_GROUP
        carry = lax.fori_loop(
            0, groups, lambda t, cr: steps(older - 1 - t * SB_GROUP, cr, SB_GROUP, False), carry)
        rest = older - groups * SB_GROUP
        carry = lax.fori_loop(0, rest // nd, lambda t, cr: steps(rest - 1 - t * nd, cr, nd, False), carry)
        o_ref[...] = _unstack_heads(carry[1], lane)

    return pl.pallas_call(
        body, grid=(npair, S // bq), name=name, in_specs=[qspec, kspec, vspec],
        out_specs=pl.BlockSpec((bq, LANES), lambda p, i: (i, p)),
        out_shape=jax.ShapeDtypeStruct((S, D), F32),
        scratch_shapes=[pltpu.VMEM((S, LANES), BF16), pltpu.VMEM((S, LANES), BF16)],
        compiler_params=_params("parallel", "arbitrary"),
    )(proj, proj, proj)


def sb_bwd(proj, do, D, name, dep=None):
    S = proj.shape[0]
    bq = min(SB_BQ_BWD, S)
    nd = bq // SB_BK
    nkb = S // SB_BK
    npair, qspec, kspec, vspec = _sb_specs(S, D, bq)
    scale = SB_HEAD_DIM ** -0.5

    def body(q_ref, k_ref, v_ref, do_ref, *refs):
        dq_ref, dk_ref, dv_ref, kb_ref, vb_ref, dk_acc, dv_acc, g_scr, b_scr, a_scr = refs[len(deps):]
        i = pl.program_id(1)

        @pl.when(i == 0)
        def _():
            kb_ref[...] = k_ref[...].astype(BF16)
            vb_ref[...] = v_ref[...].astype(BF16)
            dk_acc[...] = jnp.zeros_like(dk_acc)
            dv_acc[...] = jnp.zeros_like(dv_acc)

        lane, tri_rev, tri_fwd = _sb_consts(bq)
        qs = _stack_heads(q_ref[...] * scale, lane)
        dos = _stack_heads(do_ref[...], lane)
        qs_t, dos_t = qs.T, dos.T
        trow, scol = _sb_positions(i, bq)

        def sweep1(j0, c, n, masked):
            js = [j0 - t for t in range(n)]
            ks = [pl.multiple_of(j * SB_BK, SB_BK) for j in js]
            past = [(k + scol) < trow if masked else None for k in ks]
            zs = [lax.dot_general(qs, kb_ref[pl.ds(k, SB_BK), :], NT, preferred_element_type=F32) for k in ks]
            das = [lax.dot_general(dos, vb_ref[pl.ds(k, SB_BK), :], NT, preferred_element_type=F32) for k in ks]
            lm = [_sb_logits(z, p) for z, p in zip(zs, past)]
            cs = [_cumsum2(m, tri_rev) for _, m in lm]
            for t in range(n):
                b_scr[js[t]] = jnp.exp(lm[t][0]).astype(BF16)
            for t in range(n):
                a = jnp.exp(lm[t][0] + (cs[t][0] + c))
                if masked:
                    a = jnp.where(past[t], a, 0.0)
                g_scr[js[t]] = (das[t] * a).astype(BF16)
                a_scr[js[t]] = a.astype(BF16)
                c = c + cs[t][1]
            return c

        older = i * nd
        groups = older // SB_GROUP
        rest = older - groups * SB_GROUP
        c = jnp.zeros((2 * bq, 1), F32)
        c = sweep1(i * nd + nd - 1, c, nd, True)
        c = lax.fori_loop(0, groups, lambda t, cr: sweep1(older - 1 - t * SB_GROUP, cr, SB_GROUP, False), c)
        lax.fori_loop(0, rest // nd, lambda t, cr: sweep1(rest - 1 - t * nd, cr, nd, False), c)

        def sweep2(j0, carry, n, masked):
            js = [j0 + t for t in range(n)]
            ks = [pl.multiple_of(j * SB_BK, SB_BK) for j in js]
            g16 = [g_scr[j] for j in js]
            gv = [g.astype(F32) for g in g16]
            gs = [(lax.dot_general(g, tri_fwd, NN, preferred_element_type=F32), jnp.sum(v, axis=1, keepdims=True))
                  for g, v in zip(g16, gv)]
            pc, dq = carry
            dzs = []
            for t in range(n):
                dz = gv[t] - b_scr[js[t]].astype(F32) * (gv[t] + (gs[t][0] + pc))
                if masked:
                    dz = jnp.where((ks[t] + scol) < trow, dz, 0.0)
                dzs.append(dz.astype(BF16))
                pc = pc + gs[t][1]
            for t in range(n):
                dq = dq + lax.dot_general(dzs[t], kb_ref[pl.ds(ks[t], SB_BK), :], NN, preferred_element_type=F32)
                dk_acc[js[t]] += lax.dot_general(qs_t, dzs[t], NN, preferred_element_type=F32)
                dv_acc[js[t]] += lax.dot_general(dos_t, a_scr[js[t]], NN, preferred_element_type=F32)
            return pc, dq

        carry = (jnp.zeros((2 * bq, 1), F32), jnp.zeros((2 * bq, LANES), F32))
        carry = lax.fori_loop(0, groups, lambda t, cr: sweep2(t * SB_GROUP, cr, SB_GROUP, False), carry)
        carry = lax.fori_loop(
            0, rest // nd, lambda t, cr: sweep2(groups * SB_GROUP + t * nd, cr, nd, False), carry)
        carry = sweep2(i * nd, carry, nd, True)
        dq_ref[...] = (_unstack_heads(carry[1], lane) * scale).astype(BF16)

        @pl.when(i == pl.num_programs(1) - 1)
        def _():
            for j in range(nkb):
                dk_ref[j * SB_BK:(j + 1) * SB_BK, :] = dk_acc[j].T.astype(BF16)
                dv_ref[j * SB_BK:(j + 1) * SB_BK, :] = dv_acc[j].T.astype(BF16)

    deps = [] if dep is None else [dep]
    full = pl.BlockSpec((S, LANES), lambda p, i: (0, p))
    blk = pl.BlockSpec((bq, LANES), lambda p, i: (i, p))
    act = jax.ShapeDtypeStruct((S, D), BF16)
    return pl.pallas_call(
        body, grid=(npair, S // bq), name=name, in_specs=[qspec, kspec, vspec, blk] + [ANY] * len(deps),
        out_specs=[blk, full, full], out_shape=[act, act, act],
        scratch_shapes=[pltpu.VMEM((S, LANES), BF16), pltpu.VMEM((S, LANES), BF16),
                        pltpu.VMEM((nkb, LANES, SB_BK), F32), pltpu.VMEM((nkb, LANES, SB_BK), F32),
                        pltpu.VMEM((nkb, 2 * bq, SB_BK), BF16), pltpu.VMEM((nkb, 2 * bq, SB_BK), BF16),
                        pltpu.VMEM((nkb, 2 * bq, SB_BK), BF16)],
        compiler_params=_params("parallel", "arbitrary"),
    )(proj, proj, proj, do, *deps)


def _rms_rows(v):
    r = lax.rsqrt(jnp.mean(v * v, axis=-1, keepdims=True) + EPS)
    return v * r, r


def _xa_specs(S, D, M, tq):
    nh = D // X_HEAD_DIM
    qspec = pl.BlockSpec((tq, X_HEAD_DIM), lambda h, i: (i, 6 * nh + h))
    kspec = pl.BlockSpec((M, X_HEAD_DIM), lambda h, i: (0, h))
    vspec = pl.BlockSpec((M, X_HEAD_DIM), lambda h, i: (0, nh + h))
    gspec = pl.BlockSpec((1, X_HEAD_DIM), lambda h, i: (0, 0))
    return nh, qspec, kspec, vspec, gspec


def xa_fwd(proj, kv, gq, gk, D, name):
    S, M = proj.shape[0], kv.shape[0]
    tq = _tile(S, 512)
    nh, qspec, kspec, vspec, gspec = _xa_specs(S, D, M, tq)
    scale = X_HEAD_DIM ** -0.5

    def body(q_ref, k_ref, v_ref, gq_ref, gk_ref, o_ref):
        qn = _rms_rows(q_ref[...])[0] * gq_ref[...]
        kn = _rms_rows(k_ref[...])[0] * gk_ref[...]
        s = _dot(qn, kn, NT) * scale
        e = jnp.exp(s - jnp.max(s, axis=-1, keepdims=True))
        p = e / jnp.sum(e, axis=-1, keepdims=True)
        o_ref[...] = _dot(p, v_ref[...], NN)

    return pl.pallas_call(
        body, grid=(nh, S // tq), name=name, in_specs=[qspec, kspec, vspec, gspec, gspec],
        out_specs=pl.BlockSpec((tq, X_HEAD_DIM), lambda h, i: (i, h)),
        out_shape=jax.ShapeDtypeStruct((S, D), F32), compiler_params=_params("parallel", "parallel"),
    )(proj, kv, kv, gq, gk)


def xa_bwd(proj, kv, gq, gk, do, D, name):
    S, M = proj.shape[0], kv.shape[0]
    tq = _tile(S, 512)
    nh, qspec, kspec, vspec, gspec = _xa_specs(S, D, M, tq)
    scale = X_HEAD_DIM ** -0.5

    def body(q_ref, k_ref, v_ref, gq_ref, gk_ref, do_ref, dq_ref, dk_ref, dv_ref, ggq_ref, ggk_ref,
             dkn_acc, dv_acc):
        h, i = pl.program_id(0), pl.program_id(1)

        @pl.when((h == 0) & (i == 0))
        def _():
            ggq_ref[...] = jnp.zeros_like(ggq_ref)
            ggk_ref[...] = jnp.zeros_like(ggk_ref)

        @pl.when(i == 0)
        def _():
            dkn_acc[...] = jnp.zeros_like(dkn_acc)
            dv_acc[...] = jnp.zeros_like(dv_acc)

        gq, gk = gq_ref[...], gk_ref[...]
        qhat, rq = _rms_rows(q_ref[...])
        khat, rk = _rms_rows(k_ref[...])
        qn, kn = qhat * gq, khat * gk
        s = _dot(qn, kn, NT) * scale
        e = jnp.exp(s - jnp.max(s, axis=-1, keepdims=True))
        p = e / jnp.sum(e, axis=-1, keepdims=True)
        dov = do_ref[...]
        dv_acc[...] += _dot(p, dov, TN)
        dp = _dot(dov, v_ref[...], NT)
        ds = (p * (dp - jnp.sum(dp * p, axis=-1, keepdims=True))) * scale
        dqn = _dot(ds, kn, NN)
        dkn_acc[...] += _dot(ds, qn, TN)
        ggq_ref[...] += jnp.sum(dqn * qhat, axis=0, keepdims=True)
        dqh = dqn * gq
        dq_ref[...] = (rq * (dqh - qhat * jnp.mean(dqh * qhat, axis=-1, keepdims=True))).astype(BF16)

        @pl.when(i == pl.num_programs(1) - 1)
        def _():
            dkn = dkn_acc[...]
            ggk_ref[...] += jnp.sum(dkn * khat, axis=0, keepdims=True)
            dkh = dkn * gk
            dk_ref[...] = (rk * (dkh - khat * jnp.mean(dkh * khat, axis=-1, keepdims=True))).astype(BF16)
            dv_ref[...] = dv_acc[...].astype(BF16)

    blk = pl.BlockSpec((tq, X_HEAD_DIM), lambda h, i: (i, h))
    kv_shape = jax.ShapeDtypeStruct((M, 2 * D), BF16)
    gshape = jax.ShapeDtypeStruct((1, X_HEAD_DIM), F32)
    dq, dk, dv, ggq, ggk = pl.pallas_call(
        body, grid=(nh, S // tq), name=name, in_specs=[qspec, kspec, vspec, gspec, gspec, blk],
        out_specs=[blk, kspec, vspec, gspec, gspec],
        out_shape=[jax.ShapeDtypeStruct((S, D), BF16), kv_shape, kv_shape, gshape, gshape],
        scratch_shapes=[pltpu.VMEM((M, X_HEAD_DIM), F32), pltpu.VMEM((M, X_HEAD_DIM), F32)],
        compiler_params=_params("arbitrary", "arbitrary"),
    )(proj, kv, kv, gq, gk, do)
    d_kv = jnp.concatenate([dk[:, :D], dv[:, D:]], axis=1)
    return dq, d_kv, ggq, ggk


def _gate_specs(S, D, tm):
    row = pl.BlockSpec((tm, D), lambda i: (i, 0))
    gate = lambda b: pl.BlockSpec((tm, D), lambda i, b=b: (i, 7 + b))
    return row, gate


def merge_fwd(proj, ys, D, name):
    S = proj.shape[0]
    tm = _tile(S, 256)
    row, gate = _gate_specs(S, D, tm)

    def body(g0, g1, g2, y0, y1, y2, o_ref):
        acc = jax.nn.sigmoid(g0[...]) * y0[...]
        acc = acc + jax.nn.sigmoid(g1[...]) * y1[...]
        acc = acc + jax.nn.sigmoid(g2[...]) * y2[...]
        o_ref[...] = acc.astype(BF16)

    return pl.pallas_call(
        body, grid=(S // tm,), name=name, in_specs=[gate(0), gate(1), gate(2), row, row, row],
        out_specs=row, out_shape=jax.ShapeDtypeStruct((S, D), BF16), compiler_params=_params("parallel"),
    )(proj, proj, proj, *ys)


def merge_bwd(proj, ys, dm, D, name):
    S = proj.shape[0]
    tm = _tile(S, 256)
    row, gate = _gate_specs(S, D, tm)

    def body(g0, g1, g2, y0, y1, y2, dm_ref, d0, d1, d2, dg_ref):
        dmv = dm_ref[...]
        for b, (g_ref, y_ref, d_ref) in enumerate(((g0, y0, d0), (g1, y1, d1), (g2, y2, d2))):
            s = jax.nn.sigmoid(g_ref[...])
            d_ref[...] = (dmv * s).astype(BF16)
            dg_ref[:, b * D:(b + 1) * D] = ((dmv * y_ref[...]) * (s * (1.0 - s))).astype(BF16)

    act = jax.ShapeDtypeStruct((S, D), BF16)
    return pl.pallas_call(
        body, grid=(S // tm,), name=name, in_specs=[gate(0), gate(1), gate(2), row, row, row, row],
        out_specs=[row, row, row, pl.BlockSpec((tm, 3 * D), lambda i: (i, 0))],
        out_shape=[act, act, act, jax.ShapeDtypeStruct((S, 3 * D), BF16)], compiler_params=_params("parallel"),
    )(proj, proj, proj, *ys, dm)


def loss_head(y, target, name):
    S, D = y.shape
    tm = _tile(S, 512)

    def body(y_ref, t_ref, dy_ref, dy16_ref, l_ref):
        @pl.when(pl.program_id(0) == 0)
        def _():
            l_ref[...] = jnp.zeros_like(l_ref)

        e = y_ref[...] - t_ref[...]
        dy = e * (1.0 / D)
        dy_ref[...] = dy
        dy16_ref[...] = dy.astype(BF16)
        l_ref[...] += jnp.sum(e * e, axis=0, keepdims=True)

    row = pl.BlockSpec((tm, D), lambda i: (i, 0))
    vec = pl.BlockSpec((1, D), lambda i: (0, 0))
    return pl.pallas_call(
        body, grid=(S // tm,), name=name, in_specs=[row, row], out_specs=[row, row, vec],
        out_shape=[jax.ShapeDtypeStruct((S, D), F32), jax.ShapeDtypeStruct((S, D), BF16),
                   jax.ShapeDtypeStruct((1, D), F32)],
        compiler_params=_params("arbitrary"),
    )(y, target)


def _rows2d(a):
    return a.reshape(-1, a.shape[-1])


def _ew_call(fn, ins, out_dtypes, name):
    R, C = ins[0].shape
    tr = _tile(R, max(8, (1 << 19) // C))
    spec = pl.BlockSpec((tr, C), lambda i: (i, 0))

    def body(*refs):
        outs = fn(*[r[...] for r in refs[:len(ins)]])
        for o_ref, o in zip(refs[len(ins):], outs):
            o_ref[...] = o.astype(o_ref.dtype)

    return pl.pallas_call(
        body, grid=(R // tr,), name=name, in_specs=[spec] * len(ins), out_specs=[spec] * len(out_dtypes),
        out_shape=[jax.ShapeDtypeStruct((R, C), d) for d in out_dtypes], compiler_params=_params("parallel"),
    )(*ins)


def adamw(w, g, m, v, name):
    def fn(w, g, m, v):
        m = ADAM_B1 * m + (1.0 - ADAM_B1) * g
        v = ADAM_B2 * v + (1.0 - ADAM_B2) * (g * g)
        m_hat = m / (1.0 - ADAM_B1 ** ADAM_STEP)
        v_hat = v / (1.0 - ADAM_B2 ** ADAM_STEP)
        return -ADAM_LR * (m_hat / (jnp.sqrt(v_hat) + ADAM_EPS) + ADAM_WD * w), m, v

    shp = w.shape
    outs = _ew_call(fn, [_rows2d(a) for a in (w, g, m, v)], [F32, F32, F32], name)
    return [o.reshape(shp) for o in outs]


def _placed_call(fn, place, grid, ins, in_specs, out_shape, out_specs, name, dep=None):
    n = len(ins)
    deps = [] if dep is None else [dep]

    def body(place_ref, *refs):
        outs = fn(*[r[...] for r in refs[:n]])
        for o_ref, o in zip(refs[n + len(deps):], outs):
            o_ref[...] = o.astype(o_ref.dtype)

    return pl.pallas_call(
        body, name=name, out_shape=out_shape,
        grid_spec=pltpu.PrefetchScalarGridSpec(
            num_scalar_prefetch=1, grid=grid, in_specs=list(in_specs) + [ANY] * len(deps), out_specs=out_specs),
        compiler_params=_params(*["parallel"] * len(grid)),
    )(place, *ins, *deps)


def _row_tile(R, C):
    return _tile(R, max(16, (1 << 19) // C))


def cast_into_full(w, place, name, dep=None):
    R, C = w.shape
    tr = _row_tile(R, C)
    return _placed_call(
        lambda a: (a,), place, (R // tr,), [w], [pl.BlockSpec((tr, C), lambda i, p: (i, 0))],
        [jax.ShapeDtypeStruct((N_CHIP, R, C), BF16)], [pl.BlockSpec((None, tr, C), lambda i, p: (p[0], i, 0))],
        name, dep=dep)[0]


def pair_sum(g4, got, place, name):
    _, hr, C = got.shape
    tr = _row_tile(hr, C)
    nb = hr // tr
    blk = pl.BlockSpec((None, tr, C), lambda s, i, p: (s, i, 0))
    return _placed_call(
        lambda a, b: (a + b,), place, (N_CHIP, nb), [g4, got],
        [pl.BlockSpec((None, tr, C), lambda s, i, p: (s, p[1] * nb + i, 0)), blk],
        [jax.ShapeDtypeStruct(got.shape, BF16)], [blk], name)[0]


def chip_sum(p32, got, place, name, dep=None):
    _, H, C = p32.shape
    tr = _row_tile(H, C)
    nb = H // tr
    peer = lambda j: pl.BlockSpec((None, tr, C), lambda i, p, j=j: (j, i, 0))
    return _placed_call(
        lambda a, b, c, d: (((a.astype(F32) + b.astype(F32)) + c.astype(F32)) + d.astype(F32),), place, (nb,),
        [p32, got, got, got], [pl.BlockSpec((None, tr, C), lambda i, p: (p[0], i, 0)), peer(0), peer(1), peer(2)],
        [jax.ShapeDtypeStruct((2 * H, C), F32)], [pl.BlockSpec((tr, C), lambda i, p: (p[1] * nb + i, 0))],
        name, dep=dep)[0]


ANY = pl.BlockSpec(memory_space=pl.ANY)
CHIP_FLIPS = ((1, 0), (0, 1), (1, 1))


def _place():
    return lax.axis_index("x"), lax.axis_index("y"), lax.axis_index("c")


def _flip(v, f):
    return 1 - v if f else v


def exchange_halves(grads, name):
    n = len(grads)

    def body(*refs):
        ins, got = refs[:n], refs[n:2 * n]
        send_sem, recv_sem = refs[2 * n:]
        x, y, c = _place()
        copies = []
        for a in range(n):
            hr = ins[a].shape[1] // 2
            cp = pltpu.make_async_remote_copy(
                src_ref=ins[a].at[:, pl.ds((1 - c) * hr, hr), :], dst_ref=got[a],
                send_sem=send_sem.at[a], recv_sem=recv_sem.at[a], device_id=(x, y, 1 - c), device_id_type=MESH)
            cp.start()
            copies.append(cp)
        for cp in copies:
            cp.wait()

    dma = pltpu.SemaphoreType.DMA
    return pl.pallas_call(
        body, name=name, in_specs=[ANY] * n, out_specs=[ANY] * n,
        out_shape=[jax.ShapeDtypeStruct((N_CHIP, g.shape[1] // 2, g.shape[2]), g.dtype) for g in grads],
        scratch_shapes=[dma((n,)), dma((n,))],
    )(*grads)


def join_halves(fulls, name):
    n = len(fulls)

    def body(*refs):
        outs = refs[n:2 * n]
        send_sem, recv_sem = refs[2 * n:]
        x, y, c = _place()
        copies = []
        for a in range(n):
            hr = outs[a].shape[0] // 2
            half = outs[a].at[pl.ds(c * hr, hr), :]
            cp = pltpu.make_async_remote_copy(
                src_ref=half, dst_ref=half, send_sem=send_sem.at[a], recv_sem=recv_sem.at[a],
                device_id=(x, y, 1 - c), device_id_type=MESH)
            cp.start()
            copies.append(cp)
        for a, cp in enumerate(copies):
            hr = outs[a].shape[0] // 2
            theirs = outs[a].at[pl.ds((1 - c) * hr, hr), :]
            cp.wait_send()
            pltpu.make_async_remote_copy(
                src_ref=theirs, dst_ref=theirs, send_sem=send_sem.at[a], recv_sem=recv_sem.at[a],
                device_id=(x, y, 1 - c), device_id_type=MESH).wait_recv()

    dma = pltpu.SemaphoreType.DMA
    return pl.pallas_call(
        body, name=name, in_specs=[ANY] * n, out_specs=[ANY] * n,
        out_shape=[jax.ShapeDtypeStruct(f.shape, F32) for f in fulls],
        input_output_aliases={a: a for a in range(n)},
        scratch_shapes=[dma((n,)), dma((n,))],
    )(*fulls)


HBM = pl.BlockSpec(memory_space=pltpu.HBM)
SEM = pl.BlockSpec(memory_space=pltpu.SEMAPHORE)
EFFECT = pltpu.SideEffectType.DATAFLOW_SIDE_EFFECTING


def _in_hbm(a):
    return pltpu.with_memory_space_constraint(a, pltpu.HBM)


def _gather_half(ref, chip_idx, core):
    hr = ref.shape[1] // 2
    return ref.at[chip_idx, pl.ds(core * hr, hr), :]


def gather_forward(fulls, small, name):
    n = len(fulls)

    def body(*refs):
        small_in = refs[n]
        outs, small_out = refs[n + 1:2 * n + 1], refs[2 * n + 1]
        send_sem, recv_sem, sm_send, sm_recv, loc_sem = refs[2 * n + 2:]
        x, y, c = _place()
        mine = 2 * x + y
        chips = [(_flip(x, fx), _flip(y, fy)) for fx, fy in CHIP_FLIPS]
        local = pltpu.make_async_copy(small_in, small_out.at[mine], loc_sem)
        local.start()
        copies = []
        for j, (px, py) in enumerate(chips):
            cp = pltpu.make_async_remote_copy(
                src_ref=small_in, dst_ref=small_out.at[mine], send_sem=sm_send.at[j], recv_sem=sm_recv.at[j],
                device_id=(px, py, c), device_id_type=MESH)
            cp.start()
            copies.append(cp)
        for a in range(n):
            for j, (px, py) in enumerate(chips):
                src = _gather_half(outs[a], 2 * px + py, c)
                cp = pltpu.make_async_remote_copy(
                    src_ref=src, dst_ref=src, send_sem=send_sem.at[3 * a + j], recv_sem=recv_sem.at[3 * a + j],
                    device_id=(x, y, 1 - c), device_id_type=MESH)
                cp.start()
                copies.append(cp)
        for a in range(n):
            for j, (px, py) in enumerate(chips):
                dst = _gather_half(outs[a], 2 * px + py, 1 - c)
                pltpu.make_async_remote_copy(
                    src_ref=dst, dst_ref=dst, send_sem=send_sem.at[3 * a + j], recv_sem=recv_sem.at[3 * a + j],
                    device_id=(x, y, 1 - c), device_id_type=MESH).wait_recv()
        for j, (px, py) in enumerate(chips):
            dst = small_out.at[2 * px + py]
            pltpu.make_async_remote_copy(
                src_ref=dst, dst_ref=dst, send_sem=sm_send.at[j], recv_sem=sm_recv.at[j],
                device_id=(px, py, c), device_id_type=MESH).wait_recv()
        for cp in copies:
            cp.wait_send()
        local.wait()

    dma = pltpu.SemaphoreType.DMA
    out = pl.pallas_call(
        body, name=name, in_specs=[ANY] * (n + 1), out_specs=[ANY] * (n + 1),
        out_shape=[jax.ShapeDtypeStruct(f.shape, f.dtype) for f in fulls]
        + [jax.ShapeDtypeStruct((N_CHIP,) + small.shape, small.dtype)],
        input_output_aliases={a: a for a in range(n)},
        scratch_shapes=[dma((3 * n,)), dma((3 * n,)), dma((3,)), dma((3,)), dma],
    )(*fulls, small)
    return out[:n], out[n]


def _gather_plan(fulls, lands):
    x, y, c = _place()
    mine = 2 * x + y
    return [(_gather_half(f, mine, c), _gather_half(f, mine, c), (_flip(x, fx), _flip(y, fy), c))
            for f in fulls for fx, fy in CHIP_FLIPS]


def _scatter_plan(parts, lands):
    x, y, c = _place()
    plan = []
    for p, l in zip(parts, lands):
        for j, (fx, fy) in enumerate(CHIP_FLIPS):
            px, py = _flip(x, fx), _flip(y, fy)
            plan.append((p.at[2 * px + py], l.at[j], (px, py, c)))
    return plan


def _scatter_lands(parts):
    return [(3,) + p.shape[1:] for p in parts]


def _exchange_plan(grads, lands):
    x, y, c = _place()
    plan = []
    for g, l in zip(grads, lands):
        hr = g.shape[1] // 2
        plan.append((g.at[:, pl.ds((1 - c) * hr, hr), :], l, (x, y, 1 - c)))
    return plan


def _exchange_lands(grads):
    return [(N_CHIP, g.shape[1] // 2, g.shape[2]) for g in grads]


def split_start(plan, copies, srcs, land_shapes, deps, name):
    n, m = len(srcs), len(land_shapes)
    lands = [lax.empty(s, srcs[0].dtype) for s in land_shapes]

    def body(*refs):
        k0 = n + m + len(deps)
        send_sem, recv_sem = refs[k0], refs[k0 + 1]
        thru, token = refs[k0 + 2:k0 + 2 + n + m], refs[k0 + 2 + n + m]
        for k, (src, dst, dev) in enumerate(plan(thru[:n], thru[n:])):
            pltpu.make_async_remote_copy(src_ref=src, dst_ref=dst, send_sem=send_sem.at[k], recv_sem=recv_sem.at[k],
                                         device_id=dev, device_id_type=MESH).start()
        token[...] = jnp.zeros_like(token)

    dma = pltpu.SemaphoreType.DMA
    arrays = list(srcs) + lands
    out = pl.pallas_call(
        body, name=name,
        out_shape=(dma((copies,)), dma((copies,)), *[pltpu.HBM(a.shape, a.dtype) for a in arrays],
                   jax.ShapeDtypeStruct((8, LANES), F32)),
        in_specs=[HBM] * (n + m) + [ANY] * len(deps),
        out_specs=(SEM, SEM, *[HBM] * (n + m), pl.BlockSpec(memory_space=pltpu.VMEM)),
        input_output_aliases={a: 2 + a for a in range(n + m)},
        compiler_params=pltpu.CompilerParams(has_side_effects=EFFECT),
    )(*[_in_hbm(a) for a in arrays], *deps)
    return (out[0], out[1], list(out[2:2 + n]), list(out[2 + n:2 + n + m])), out[2 + n + m]


def split_wait(plan, handle, after, name):
    send_sem, recv_sem, srcs, lands = handle
    n, m = len(srcs), len(lands)

    def body(*refs):
        send_sem, recv_sem = refs[n + m], refs[n + m + 1]
        thru = refs[n + m + 2 + len(after):]
        for k, (src, dst, dev) in enumerate(plan(thru[:n], thru[n:])):
            cp = pltpu.make_async_remote_copy(src_ref=src, dst_ref=dst, send_sem=send_sem.at[k],
                                              recv_sem=recv_sem.at[k], device_id=dev, device_id_type=MESH)
            cp.wait_send()
            cp.wait_recv()

    arrays = list(srcs) + list(lands)
    out = pl.pallas_call(
        body, name=name, out_shape=tuple(pltpu.HBM(a.shape, a.dtype) for a in arrays),
        in_specs=[HBM] * (n + m) + [SEM, SEM] + [ANY] * len(after), out_specs=tuple([HBM] * (n + m)),
        input_output_aliases={a: a for a in range(n + m)},
        compiler_params=pltpu.CompilerParams(has_side_effects=EFFECT),
    )(*arrays, send_sem, recv_sem, *after)
    return list(out[:n]), list(out[n:])


def allreduce_small(block, name):
    R, C = block.shape

    def body(in_ref, out_ref, slots, send_sem, recv_sem):
        x, y, c = _place()
        me = 4 * x + 2 * y + c
        slots[me] = in_ref[...]
        copies = []
        for r in range(1, 8):
            fx, fy, fc = (r >> 2) & 1, (r >> 1) & 1, r & 1
            cp = pltpu.make_async_remote_copy(
                src_ref=in_ref, dst_ref=slots.at[me], send_sem=send_sem.at[r - 1], recv_sem=recv_sem.at[r - 1],
                device_id=(_flip(x, fx), _flip(y, fy), _flip(c, fc)), device_id_type=MESH)
            cp.start()
            copies.append(cp)
        for cp in copies:
            cp.wait()
        acc = slots[0]
        for d in range(1, 8):
            acc = acc + slots[d]
        out_ref[...] = acc

    vm = pl.BlockSpec(memory_space=pltpu.VMEM)
    dma = pltpu.SemaphoreType.DMA
    return pl.pallas_call(
        body, name=name, in_specs=[vm], out_specs=vm, out_shape=jax.ShapeDtypeStruct((R, C), F32),
        scratch_shapes=[pltpu.VMEM((8, R, C), F32), dma((7,)), dma((7,))],
    )(block)


def local_step(x, mem, target, g_mix, g_mem, q_norm_g, k_norm_g, g_mlp, conv_w, h, mem_n, w_in, w_in_dep,
               rest_weights, early_grads, mid_grads):
    S, D = x.shape
    proj = mm_nn_shard(h, w_in, "proj", dep=w_in_dep)
    a_conv = conv_fwd(proj, conv_w, D, "conv_fwd")
    o_sb = sb_fwd(proj, D, "sb_fwd")
    w_conv_out, w_sb_out, w_mem_kv, w_x_out, w_out, w_up, w_down = rest_weights(o_sb)
    kv = mm_nn_shard(mem_n, w_mem_kv, "kv")
    o_x = xa_fwd(proj, kv, q_norm_g, k_norm_g, D, "xa_fwd")
    ys = [mm_nn(a_conv, w_conv_out, "y_conv"), mm_nn(o_sb, w_sb_out, "y_sb"), mm_nn(o_x, w_x_out, "y_x")]
    merged = merge_fwd(proj, ys, D, "merge_fwd")
    x1 = mm_nn(merged, w_out, "x1", res=x)
    h2 = rms_fwd(x1, g_mlp, "rms_mlp")
    up, act = mm_nn_shard(h2, w_up, "up", relu2=True)
    x2 = mm_nn(act, w_down, "x2", res=x1)
    dy, dy16, loss_cols = loss_head(x2, target, "loss_head")
    d_up = mm_nt(dy16, w_down, "d_up", up=up, out_dtype=BF16)
    g = {"w_down": mm_tn(act, dy16, "g_w_down")}
    g["w_up"] = mm_tn(h2, d_up, "g_w_up", shard_out=True)
    dh2 = mm_nt_shard(d_up, w_up, "dh2")
    dx1, dx1_16, g["g_mlp"] = rms_bwd(x1, g_mlp, dh2, "rms_mlp_bwd", dres=dy, want16=True)
    g["w_out"] = mm_tn(merged, dx1_16, "g_w_out")
    dm = mm_nt(dx1_16, w_out, "d_merged")
    dy_c, dy_s, dy_x, d_gate = merge_bwd(proj, ys, dm, D, "merge_bwd")
    g["w_conv_out"] = mm_tn(a_conv, dy_c, "g_w_conv_out")
    g["w_sb_out"] = mm_tn(o_sb, dy_s, "g_w_sb_out")
    g["w_x_out"] = mm_tn(o_x, dy_x, "g_w_x_out")
    d_xq, d_kv, g["q_norm_g"], g["k_norm_g"] = xa_bwd(
        proj, kv, q_norm_g, k_norm_g, mm_nt(dy_x, w_x_out, "d_o_x"), D, "xa_bwd")
    g["w_mem_kv"] = mm_tn(mem_n, d_kv, "g_w_mem_kv", shard_out=True)
    g["g_mem"] = rms_bwd(mem, g_mem, mm_nt_shard(d_kv, w_mem_kv, "d_mem_n"), "rms_mem_bwd", want_dx=False)
    dep = early_grads(g)
    d_a_conv = mm_nt(dy_c, w_conv_out, "d_a_conv", dep=dep)
    d_ch, d_cb, d_cc, g["conv_w"] = conv_bwd(proj, conv_w, d_a_conv, D, "conv_bwd")
    d_o_sb = mm_nt(dy_s, w_sb_out, "d_o_sb", dep=dep)
    dq, dk, dv = sb_bwd(proj, d_o_sb, D, "sb_bwd", dep=mid_grads([d_ch, d_o_sb]))
    d_proj = jnp.concatenate([d_ch, d_cb, d_cc, dq, dk, dv, d_xq, d_gate], axis=1)
    g["w_in"] = mm_tn(h, d_proj, "g_w_in", shard_out=True)
    dh = mm_nt_shard(d_proj, w_in, "dh")
    grad_x, g["g_mix"] = rms_bwd(x, g_mix, dh, "rms_mix_bwd", dres=dx1)
    return loss_cols, grad_x, g


BIG = ("w_in", "w_conv_out", "w_sb_out", "w_mem_kv", "w_x_out", "w_out", "w_up", "w_down")
REST = BIG[1:]
COL_SHARDED = ("w_in", "w_mem_kv", "w_up")
WEIGHTS = ("g_mix", "g_mem", "w_in", "conv_w", "w_conv_out", "w_sb_out", "q_norm_g", "k_norm_g",
           "w_mem_kv", "w_x_out", "w_out", "g_mlp", "w_up", "w_down")


def _pack_small(D, g_mix, g_mem, g_mlp, q_norm_g, k_norm_g, conv_w, last):
    qk = jnp.concatenate([q_norm_g, k_norm_g, jnp.zeros((1, D - 2 * X_HEAD_DIM), F32)], axis=1)
    cw = jnp.pad(conv_w, ((0, 0), (0, D - conv_w.shape[1])))
    return jnp.concatenate([g_mix, g_mem, g_mlp, qk, cw, last], axis=0)


def kernel(x, mem, g_mix, g_mem, w_in, conv_w, w_conv_out, w_sb_out, q_norm_g, k_norm_g, w_mem_kv, w_x_out, w_out, g_mlp, w_up, w_down, loss_target, m_g_mix, m_g_mem, m_w_in, m_conv_w, m_w_conv_out, m_w_sb_out, m_q_norm_g, m_k_norm_g, m_w_mem_kv, m_w_x_out, m_w_out, m_g_mlp, m_w_up, m_w_down, v_g_mix, v_g_mem, v_w_in, v_conv_w, v_w_conv_out, v_w_sb_out, v_q_norm_g, v_k_norm_g, v_w_mem_kv, v_w_x_out, v_w_out, v_g_mlp, v_w_up, v_w_down):
    S, D = x.shape[1], x.shape[2]
    w = dict(g_mix=g_mix, g_mem=g_mem, w_in=w_in, conv_w=conv_w, w_conv_out=w_conv_out, w_sb_out=w_sb_out,
             q_norm_g=q_norm_g, k_norm_g=k_norm_g, w_mem_kv=w_mem_kv, w_x_out=w_x_out, w_out=w_out,
             g_mlp=g_mlp, w_up=w_up, w_down=w_down)
    m = dict(g_mix=m_g_mix, g_mem=m_g_mem, w_in=m_w_in, conv_w=m_conv_w, w_conv_out=m_w_conv_out,
             w_sb_out=m_w_sb_out, q_norm_g=m_q_norm_g, k_norm_g=m_k_norm_g, w_mem_kv=m_w_mem_kv,
             w_x_out=m_w_x_out, w_out=m_w_out, g_mlp=m_g_mlp, w_up=m_w_up, w_down=m_w_down)
    v = dict(g_mix=v_g_mix, g_mem=v_g_mem, w_in=v_w_in, conv_w=v_conv_w, w_conv_out=v_w_conv_out,
             w_sb_out=v_w_sb_out, q_norm_g=v_q_norm_g, k_norm_g=v_k_norm_g, w_mem_kv=v_w_mem_kv,
             w_x_out=v_w_x_out, w_out=v_w_out, g_mlp=v_g_mlp, w_up=v_w_up, w_down=v_w_down)
    chip = 2 * lax.axis_index("x") + lax.axis_index("y")
    cs = conv_w.shape[2]

    place = jnp.stack([chip, lax.axis_index("c")]).astype(jnp.int32)
    cw_block = jnp.pad(conv_w[0], ((0, 5), (0, 0)))
    handle, token = split_start(_gather_plan, 3, [cast_into_full(w["w_in"][0], place, "cast_w_in")], [], [],
                                "gather_w_in_start")
    rest16 = [cast_into_full(w[k][0], place, "cast_" + k, dep=token) for k in REST]
    h = rms_fwd(x[0], g_mix, "rms_mix", dep=token)
    mem_n = rms_fwd(mem[0], g_mem, "rms_mem", dep=token)
    landed, _ = split_wait(_gather_plan, handle, [*rest16, h, mem_n], "gather_w_in_wait")
    (w_in_full,), cw_all = gather_forward(landed, cw_block, "gather_w_in_forward")
    conv_full = jnp.concatenate([cw_all[p, :3] for p in range(N_CHIP)], axis=1)
    rest_handle, rest_token = split_start(_gather_plan, 3 * len(REST), rest16, [], [w_in_full], "gather_rest_start")

    def layout(k, a):
        return a if k in COL_SHARDED else a.reshape(-1, a.shape[-1])

    def rest_weights(after):
        landed, _ = split_wait(_gather_plan, rest_handle, [after], "gather_rest_wait")
        return [layout(k, a) for k, a in zip(REST, gather_forward(landed, cw_block, "gather_rest_forward")[0])]

    def blocks(k, a):
        return a if k in COL_SHARDED else a.reshape(N_CHIP, -1, a.shape[-1])

    early = {}

    def early_grads(g):
        early["g4"] = [blocks(k, g[k]) for k in REST]
        early["swap"], token = split_start(_exchange_plan, len(REST), early["g4"], _exchange_lands(early["g4"]),
                                           [g["g_mem"]], "exchange_rest_start")
        return token

    def mid_grads(after):
        g4, got = split_wait(_exchange_plan, early["swap"], after, "exchange_rest_wait")
        p16 = [pair_sum(a, b, place, "pair_sum_" + k) for k, a, b in zip(REST, g4, got)]
        early["fly"], token = split_start(_scatter_plan, 3 * len(REST), p16, _scatter_lands(p16), [],
                                          "scatter_rest_start")
        return token

    loss_cols, grad_x, g = local_step(
        x[0], mem[0], loss_target[0], g_mix, g_mem, q_norm_g, k_norm_g, g_mlp, conv_full, h, mem_n,
        w_in_full, rest_token, rest_weights, early_grads, mid_grads)

    small = allreduce_small(
        _pack_small(D, g["g_mix"], g["g_mem"], g["g_mlp"], g["q_norm_g"], g["k_norm_g"], g["conv_w"], loss_cols),
        "allreduce_small")
    loss = (0.5 / D) * jnp.sum(small[7])
    gsum = {"g_mix": small[0:1], "g_mem": small[1:2], "g_mlp": small[2:3],
            "q_norm_g": small[3:4, :X_HEAD_DIM], "k_norm_g": small[3:4, X_HEAD_DIM:2 * X_HEAD_DIM],
            "conv_w": lax.dynamic_slice(small[4:7], (0, chip * cs), (3, cs))[None]}

    p16_rest, got_rest = split_wait(_scatter_plan, early["fly"], [g["w_in"]], "scatter_rest_wait")
    got_in = exchange_halves([g["w_in"]], "exchange_halves_w_in")[0]
    p16_in = [pair_sum(g["w_in"], got_in, place, "pair_sum_w_in")]
    fly_in, token = split_start(_scatter_plan, 3, p16_in, _scatter_lands(p16_in), [got_rest[0]],
                                "scatter_w_in_start")

    delta, new_m, new_v = {}, {}, {}
    halves = [chip_sum(p, b, place, "chip_sum_" + k, dep=token) for k, p, b in zip(REST, p16_rest, got_rest)]
    for k, a in zip(REST, join_halves(halves, "join_halves_rest")):
        gsum[k] = a[None]
        delta[k], new_m[k], new_v[k] = adamw(w[k], gsum[k], m[k], v[k], "adamw_" + k)
    small_names = ("g_mix", "g_mem", "g_mlp", "q_norm_g", "k_norm_g", "conv_w")
    zero_row = jnp.zeros((1, D), F32)
    packed = [_pack_small(D, *[t[k] if k != "conv_w" else t[k][0] for k in small_names], zero_row)
              for t in (w, gsum, m, v)]
    sm = adamw(*packed, "adamw_small")
    for t, block in zip((delta, new_m, new_v), sm):
        t["g_mix"], t["g_mem"], t["g_mlp"] = block[0:1], block[1:2], block[2:3]
        t["q_norm_g"], t["k_norm_g"] = block[3:4, :X_HEAD_DIM], block[3:4, X_HEAD_DIM:2 * X_HEAD_DIM]
        t["conv_w"] = block[4:7, :cs][None]

    done = [new_v[k] for k in REST] + [sm[2]]
    p16_in, got_in = split_wait(_scatter_plan, fly_in, done, "scatter_w_in_wait")
    half_in = chip_sum(p16_in[0], got_in[0], place, "chip_sum_w_in")
    gsum["w_in"] = join_halves([half_in], "join_halves_w_in")[0][None]
    delta["w_in"], new_m["w_in"], new_v["w_in"] = adamw(w["w_in"], gsum["w_in"], m["w_in"], v["w_in"], "adamw_w_in")

    return (loss, grad_x[None], *[gsum[k] for k in WEIGHTS], *[delta[k] for k in WEIGHTS],
            *[new_m[k] for k in WEIGHTS], *[new_v[k] for k in WEIGHTS])
```

```python
import functools

import jax
import jax.numpy as jnp
from jax import lax
from jax.experimental import pallas as pl
from jax.experimental.pallas import tpu as pltpu

F32 = jnp.float32
BF16 = jnp.bfloat16
EPS = 1e-6
N_CHIP = 4
SB_HEAD_DIM = 64
X_HEAD_DIM = 256
LANES = 128
VMEM_LIMIT = 56 * 1024 * 1024
ADAM_LR, ADAM_B1, ADAM_B2, ADAM_EPS, ADAM_WD, ADAM_STEP = 0.001, 0.9, 0.999, 1e-8, 0.01, 10
MESH = pl.DeviceIdType.MESH


def _params(*sem):
    return pltpu.CompilerParams(dimension_semantics=sem, vmem_limit_bytes=VMEM_LIMIT)


def _tile(n, pref):
    if n <= pref:
        return n
    t = 1 << (pref.bit_length() - 1)
    while n % t:
        t //= 2
    return t


NN = (((1,), (0,)), ((), ()))
NT = (((1,), (1,)), ((), ()))
TN = (((0,), (0,)), ((), ()))


def _dot(a, b, dims):
    return lax.dot_general(a.astype(BF16), b.astype(BF16), dims, preferred_element_type=F32)


def mm_nn_shard(a, g, name, relu2=False, dep=None):
    M, K = a.shape
    _, _, Ns = g.shape
    tm, tn = _tile(M, 2048), _tile(Ns, 512)
    nb = Ns // tn

    def body(a_ref, b_ref, *o_refs):
        o_refs = o_refs[len(deps):]
        acc = _dot(a_ref[...], b_ref[...], NN)
        o_refs[0][...] = acc
        if relu2:
            r = jnp.maximum(acc, 0.0)
            o_refs[1][...] = (r * r).astype(BF16)

    o_spec = pl.BlockSpec((tm, tn), lambda i, j: (i, j))
    shapes = [jax.ShapeDtypeStruct((M, N_CHIP * Ns), F32)]
    specs = [o_spec]
    if relu2:
        shapes.append(jax.ShapeDtypeStruct((M, N_CHIP * Ns), BF16))
        specs.append(o_spec)
    deps = [] if dep is None else [dep]
    out = pl.pallas_call(
        body, grid=(M // tm, N_CHIP * nb), name=name,
        in_specs=[pl.BlockSpec((tm, K), lambda i, j: (i, 0)),
                  pl.BlockSpec((None, K, tn), lambda i, j: (j // nb, 0, j % nb))] + [ANY] * len(deps),
        out_specs=specs, out_shape=shapes, compiler_params=_params("parallel", "parallel"),
    )(a, g, *deps)
    return out if relu2 else out[0]


def mm_nn(a, w, name, res=None, out_dtype=F32):
    M, K = a.shape
    N = w.shape[1]
    tm, tn = _tile(M, 2048 if K <= 2048 else 1024), _tile(N, 512)

    def body(a_ref, b_ref, *refs):
        acc = _dot(a_ref[...], b_ref[...], NN)
        if res is not None:
            acc = refs[0][...] + acc
        refs[-1][...] = acc.astype(out_dtype)

    o_spec = pl.BlockSpec((tm, tn), lambda i, j: (i, j))
    ins = [a, w] + ([res] if res is not None else [])
    return pl.pallas_call(
        body, grid=(M // tm, N // tn), name=name,
        in_specs=[pl.BlockSpec((tm, K), lambda i, j: (i, 0)), pl.BlockSpec((K, tn), lambda i, j: (0, j))]
        + ([o_spec] if res is not None else []),
        out_specs=o_spec, out_shape=jax.ShapeDtypeStruct((M, N), out_dtype),
        compiler_params=_params("parallel", "parallel"),
    )(*ins)


def mm_nt(a, w, name, up=None, out_dtype=F32, dep=None):
    M, N = a.shape
    R = w.shape[0]
    tm, tr = _tile(M, 2048), _tile(R, 512)

    def body(a_ref, b_ref, *refs):
        acc = _dot(a_ref[...], b_ref[...], NT)
        if up is not None:
            acc = acc * (2.0 * jnp.maximum(refs[0][...], 0.0))
        refs[-1][...] = acc.astype(out_dtype)

    o_spec = pl.BlockSpec((tm, tr), lambda i, j: (i, j))
    ins = [a, w] + ([up] if up is not None else []) + ([dep] if dep is not None else [])
    return pl.pallas_call(
        body, grid=(M // tm, R // tr), name=name,
        in_specs=[pl.BlockSpec((tm, N), lambda i, j: (i, 0)), pl.BlockSpec((tr, N), lambda i, j: (j, 0))]
        + ([o_spec] if up is not None else []) + ([ANY] if dep is not None else []),
        out_specs=o_spec, out_shape=jax.ShapeDtypeStruct((M, R), out_dtype),
        compiler_params=_params("parallel", "parallel"),
    )(*ins)


def mm_nt_shard(a, g, name, out_dtype=F32, dep=None):
    deps = [] if dep is None else [dep]
    M = a.shape[0]
    _, R, Ns = g.shape
    tm, tr, tk = _tile(M, 1024), _tile(R, 1024), _tile(Ns, 2560)
    nb = Ns // tk
    nk = N_CHIP * nb

    def body(a_ref, b_ref, *refs):
        o_ref, acc_ref = refs[len(deps):]
        k = pl.program_id(2)

        @pl.when(k == 0)
        def _():
            acc_ref[...] = jnp.zeros_like(acc_ref)

        acc_ref[...] += _dot(a_ref[...], b_ref[...], NT)

        @pl.when(k == nk - 1)
        def _():
            o_ref[...] = acc_ref[...].astype(out_dtype)

    return pl.pallas_call(
        body, grid=(M // tm, R // tr, nk), name=name,
        in_specs=[pl.BlockSpec((tm, tk), lambda i, j, k: (i, k)),
                  pl.BlockSpec((None, tr, tk), lambda i, j, k: (k // nb, j, k % nb))] + [ANY] * len(deps),
        out_specs=pl.BlockSpec((tm, tr), lambda i, j, k: (i, j)),
        out_shape=jax.ShapeDtypeStruct((M, R), out_dtype),
        scratch_shapes=[pltpu.VMEM((tm, tr), F32)],
        compiler_params=_params("parallel", "parallel", "arbitrary"),
    )(a, g, *deps)


def mm_tn(a, b, name, shard_out=False):
    S, M = a.shape
    N = b.shape[1]
    Ns = N // N_CHIP if shard_out else N
    tm, tn = _tile(M, 1024), _tile(Ns, 512)
    nb = Ns // tn

    def body(a_ref, b_ref, o_ref):
        o_ref[...] = _dot(a_ref[...], b_ref[...], TN)

    if shard_out:
        o_spec = pl.BlockSpec((None, tm, tn), lambda i, j: (j // nb, i, j % nb))
        o_shape = jax.ShapeDtypeStruct((N_CHIP, M, Ns), F32)
    else:
        o_spec = pl.BlockSpec((tm, tn), lambda i, j: (i, j))
        o_shape = jax.ShapeDtypeStruct((M, N), F32)
    return pl.pallas_call(
        body, grid=(M // tm, N // tn), name=name,
        in_specs=[pl.BlockSpec((S, tm), lambda i, j: (0, i)), pl.BlockSpec((S, tn), lambda i, j: (0, j))],
        out_specs=o_spec, out_shape=o_shape, compiler_params=_params("parallel", "parallel"),
    )(a, b)


def rms_fwd(x, g, name, dep=None):
    S, D = x.shape
    tm = _tile(S, 512)
    deps = [] if dep is None else [dep]

    def body(x_ref, g_ref, *refs):
        xv = x_ref[...]
        r = lax.rsqrt(jnp.mean(xv * xv, axis=-1, keepdims=True) + EPS)
        refs[-1][...] = ((xv * r) * g_ref[...]).astype(BF16)

    return pl.pallas_call(
        body, grid=(S // tm,), name=name,
        in_specs=[pl.BlockSpec((tm, D), lambda i: (i, 0)), pl.BlockSpec((1, D), lambda i: (0, 0))]
        + [ANY] * len(deps),
        out_specs=pl.BlockSpec((tm, D), lambda i: (i, 0)),
        out_shape=jax.ShapeDtypeStruct((S, D), BF16), compiler_params=_params("parallel"),
    )(x, g, *deps)


def rms_bwd(x, g, dh, name, dres=None, want_dx=True, want16=False, dep=None):
    S, D = x.shape
    tm = _tile(S, 512)

    def body(x_ref, g_ref, dh_ref, *refs):
        i = pl.program_id(0)
        xv = x_ref[...]
        r = lax.rsqrt(jnp.mean(xv * xv, axis=-1, keepdims=True) + EPS)
        xn = xv * r
        dhv = dh_ref[...].astype(F32)
        gg_ref = refs[-1]

        @pl.when(i == 0)
        def _():
            gg_ref[...] = jnp.zeros_like(gg_ref)

        gg_ref[...] += jnp.sum(dhv * xn, axis=0, keepdims=True)
        if want_dx:
            dxn = dhv * g_ref[...]
            dx = r * (dxn - xn * jnp.mean(dxn * xn, axis=-1, keepdims=True))
            if dres is not None:
                dx = refs[0][...] + dx
            refs[-2][...] = dx.astype(refs[-2].dtype)
            if want16:
                refs[-3][...] = dx

    row = pl.BlockSpec((tm, D), lambda i: (i, 0))
    vec = pl.BlockSpec((1, D), lambda i: (0, 0))
    ins, in_specs = [x, g, dh], [row, vec, row]
    if dres is not None:
        ins.append(dres)
        in_specs.append(row)
    if dep is not None:
        ins.append(dep)
        in_specs.append(ANY)
    shapes, specs = [jax.ShapeDtypeStruct((1, D), F32)], [vec]
    if want_dx:
        if want16:
            shapes.insert(0, jax.ShapeDtypeStruct((S, D), BF16))
            specs.insert(0, row)
        shapes.insert(0, jax.ShapeDtypeStruct((S, D), F32))
        specs.insert(0, row)
    out = pl.pallas_call(body, grid=(S // tm,), name=name, in_specs=in_specs, out_specs=specs,
                         out_shape=shapes, compiler_params=_params("arbitrary"))(*ins)
    return out if want_dx else out[0]


def _shift_down(u, k, row):
    return jnp.where(row >= k, pltpu.roll(u, k, axis=0), 0.0)


def _shift_up(u, k, row):
    S = u.shape[0]
    return jnp.where(row < S - k, pltpu.roll(u, S - k, axis=0), 0.0)


def _conv_specs(S, D, tc):
    nb = D // tc
    col = lambda o: pl.BlockSpec((S, tc), lambda j, o=o: (0, o * nb + j))
    return col, pl.BlockSpec((3, tc), lambda j: (0, j))


def conv_fwd(proj, conv_w, D, name):
    S = proj.shape[0]
    tc = _tile(D, 256)
    col, wspec = _conv_specs(S, D, tc)

    def body(ch_ref, cb_ref, cc_ref, w_ref, a_ref):
        row = lax.broadcasted_iota(jnp.int32, (S, tc), 0)
        u = cc_ref[...] * ch_ref[...]
        w = w_ref[...]
        cv = w[0:1, :] * _shift_down(u, 2, row) + w[1:2, :] * _shift_down(u, 1, row) + w[2:3, :] * u
        a_ref[...] = (cb_ref[...] * cv).astype(BF16)

    return pl.pallas_call(
        body, grid=(D // tc,), name=name, in_specs=[col(0), col(1), col(2), wspec],
        out_specs=pl.BlockSpec((S, tc), lambda j: (0, j)),
        out_shape=jax.ShapeDtypeStruct((S, D), BF16), compiler_params=_params("parallel"),
    )(proj, proj, proj, conv_w)


def conv_bwd(proj, conv_w, da, D, name):
    S = proj.shape[0]
    tc = _tile(D, 256)
    col, wspec = _conv_specs(S, D, tc)
    blk = pl.BlockSpec((S, tc), lambda j: (0, j))

    def body(ch_ref, cb_ref, cc_ref, w_ref, da_ref, dch_ref, dcb_ref, dcc_ref, gw_ref):
        row = lax.broadcasted_iota(jnp.int32, (S, tc), 0)
        ch, cb, cc, dav = ch_ref[...], cb_ref[...], cc_ref[...], da_ref[...]
        w = w_ref[...]
        u = cc * ch
        u1, u2 = _shift_down(u, 1, row), _shift_down(u, 2, row)
        cv = w[0:1, :] * u2 + w[1:2, :] * u1 + w[2:3, :] * u
        dcb_ref[...] = (dav * cv).astype(BF16)
        dcv = dav * cb
        gw_ref[0:1, :] = jnp.sum(dcv * u2, axis=0, keepdims=True)
        gw_ref[1:2, :] = jnp.sum(dcv * u1, axis=0, keepdims=True)
        gw_ref[2:3, :] = jnp.sum(dcv * u, axis=0, keepdims=True)
        du = w[2:3, :] * dcv + w[1:2, :] * _shift_up(dcv, 1, row) + w[0:1, :] * _shift_up(dcv, 2, row)
        dcc_ref[...] = (du * ch).astype(BF16)
        dch_ref[...] = (du * cc).astype(BF16)

    act = jax.ShapeDtypeStruct((S, D), BF16)
    return pl.pallas_call(
        body, grid=(D // tc,), name=name, in_specs=[col(0), col(1), col(2), wspec, blk],
        out_specs=[blk, blk, blk, wspec], out_shape=[act, act, act, jax.ShapeDtypeStruct((3, D), F32)],
        compiler_params=_params("parallel"),
    )(proj, proj, proj, conv_w, da)


SB_BQ_FWD = 512
SB_BQ_BWD = 256
SB_BK = 128
SB_GROUP = 4


def _sb_consts(bq):
    lane = lax.broadcasted_iota(jnp.int32, (bq, LANES), 1)
    r = lax.broadcasted_iota(jnp.int32, (SB_BK, SB_BK), 0)
    c = lax.broadcasted_iota(jnp.int32, (SB_BK, SB_BK), 1)
    tri_rev = jnp.where(r > c, 1.0, 0.0).astype(BF16)
    tri_fwd = jnp.where(r < c, 1.0, 0.0).astype(BF16)
    return lane, tri_rev, tri_fwd


def _cumsum2(v, tri):
    hi = v.astype(BF16)
    lo = (v - hi.astype(F32)).astype(BF16)
    part = (lax.dot_general(hi, tri, NN, preferred_element_type=F32)
            + lax.dot_general(lo, tri, NN, preferred_element_type=F32))
    return part, jnp.sum(v, axis=1, keepdims=True)


def _sb_logits(z, past):
    sp = jnp.log(1.0 + jnp.exp(-jnp.abs(z)))
    l = jnp.minimum(z, 0.0) - sp
    m = l - z
    if past is not None:
        m = jnp.where(past, m, 0.0)
    return l, m


def _stack_heads(v, lane):
    return jnp.concatenate([jnp.where(lane < SB_HEAD_DIM, v, 0.0), jnp.where(lane >= SB_HEAD_DIM, v, 0.0)],
                           axis=0).astype(BF16)


def _unstack_heads(v, lane):
    bq = v.shape[0] // 2
    return jnp.where(lane < SB_HEAD_DIM, v[:bq], v[bq:])


def _sb_positions(i, bq):
    r = lax.broadcasted_iota(jnp.int32, (2 * bq, SB_BK), 0)
    trow = i * bq + jnp.where(r >= bq, r - bq, r)
    return trow, lax.broadcasted_iota(jnp.int32, (2 * bq, SB_BK), 1)


def _sb_specs(S, D, bq):
    npair = D // LANES
    qspec = pl.BlockSpec((bq, LANES), lambda p, i: (i, 3 * npair + p))
    kspec = pl.BlockSpec((S, LANES), lambda p, i: (0, 4 * npair + p))
    vspec = pl.BlockSpec((S, LANES), lambda p, i: (0, 5 * npair + p))
    return npair, qspec, kspec, vspec


def sb_fwd(proj, D, name):
    S = proj.shape[0]
    bq = min(SB_BQ_FWD, S)
    nd = bq // SB_BK
    npair, qspec, kspec, vspec = _sb_specs(S, D, bq)
    scale = SB_HEAD_DIM ** -0.5

    def body(q_ref, k_ref, v_ref, o_ref, kb_ref, vb_ref):
        i = pl.program_id(1)

        @pl.when(i == 0)
        def _():
            kb_ref[...] = k_ref[...].astype(BF16)
            vb_ref[...] = v_ref[...].astype(BF16)

        lane, tri_rev, _ = _sb_consts(bq)
        qs = _stack_heads(q_ref[...] * scale, lane)
        trow, scol = _sb_positions(i, bq)

        def steps(j0, carry, n, masked):
            ks = [pl.multiple_of((j0 - t) * SB_BK, SB_BK) for t in range(n)]
            past = [(k + scol) < trow if masked else None for k in ks]
            zs = [lax.dot_general(qs, kb_ref[pl.ds(k, SB_BK), :], NT, preferred_element_type=F32) for k in ks]
            lm = [_sb_logits(z, p) for z, p in zip(zs, past)]
            cs = [_cumsum2(m, tri_rev) for _, m in lm]
            c, acc = carry
            for t in range(n):
                a = jnp.exp(lm[t][0] + (cs[t][0] + c))
                if masked:
                    a = jnp.where(past[t], a, 0.0)
                acc = acc + lax.dot_general(a.astype(BF16), vb_ref[pl.ds(ks[t], SB_BK), :], NN,
                                            preferred_element_type=F32)
                c = c + cs[t][1]
            return c, acc

        carry = (jnp.zeros((2 * bq, 1), F32), jnp.zeros((2 * bq, LANES), F32))
        carry = steps(i * nd + nd - 1, carry, nd, True)
        older = i * nd
        groups = older // SB_GROUP
        carry = lax.fori_loop(
            0, groups, lambda t, cr: steps(older - 1 - t * SB_GROUP, cr, SB_GROUP, False), carry)
        rest = older - groups * SB_GROUP
        carry = lax.fori_loop(0, rest // nd, lambda t, cr: steps(rest - 1 - t * nd, cr, nd, False), carry)
        o_ref[...] = _unstack_heads(carry[1], lane)

    return pl.pallas_call(
        body, grid=(npair, S // bq), name=name, in_specs=[qspec, kspec, vspec],
        out_specs=pl.BlockSpec((bq, LANES), lambda p, i: (i, p)),
        out_shape=jax.ShapeDtypeStruct((S, D), F32),
        scratch_shapes=[pltpu.VMEM((S, LANES), BF16), pltpu.VMEM((S, LANES), BF16)],
        compiler_params=_params("parallel", "arbitrary"),
    )(proj, proj, proj)


def sb_bwd(proj, do, D, name, dep=None):
    S = proj.shape[0]
    bq = min(SB_BQ_BWD, S)
    nd = bq // SB_BK
    nkb = S // SB_BK
    npair, qspec, kspec, vspec = _sb_specs(S, D, bq)
    scale = SB_HEAD_DIM ** -0.5

    def body(q_ref, k_ref, v_ref, do_ref, *refs):
        dq_ref, dk_ref, dv_ref, kb_ref, vb_ref, dk_acc, dv_acc, g_scr, b_scr, a_scr = refs[len(deps):]
        i = pl.program_id(1)

        @pl.when(i == 0)
        def _():
            kb_ref[...] = k_ref[...].astype(BF16)
            vb_ref[...] = v_ref[...].astype(BF16)
            dk_acc[...] = jnp.zeros_like(dk_acc)
            dv_acc[...] = jnp.zeros_like(dv_acc)

        lane, tri_rev, tri_fwd = _sb_consts(bq)
        qs = _stack_heads(q_ref[...] * scale, lane)
        dos = _stack_heads(do_ref[...], lane)
        qs_t, dos_t = qs.T, dos.T
        trow, scol = _sb_positions(i, bq)

        def sweep1(j0, c, n, masked):
            js = [j0 - t for t in range(n)]
            ks = [pl.multiple_of(j * SB_BK, SB_BK) for j in js]
            past = [(k + scol) < trow if masked else None for k in ks]
            zs = [lax.dot_general(qs, kb_ref[pl.ds(k, SB_BK), :], NT, preferred_element_type=F32) for k in ks]
            das = [lax.dot_general(dos, vb_ref[pl.ds(k, SB_BK), :], NT, preferred_element_type=F32) for k in ks]
            lm = [_sb_logits(z, p) for z, p in zip(zs, past)]
            cs = [_cumsum2(m, tri_rev) for _, m in lm]
            for t in range(n):
                b_scr[js[t]] = jnp.exp(lm[t][0]).astype(BF16)
            for t in range(n):
                a = jnp.exp(lm[t][0] + (cs[t][0] + c))
                if masked:
                    a = jnp.where(past[t], a, 0.0)
                g_scr[js[t]] = (das[t] * a).astype(BF16)
                a_scr[js[t]] = a.astype(BF16)
                c = c + cs[t][1]
            return c

        older = i * nd
        groups = older // SB_GROUP
        rest = older - groups * SB_GROUP
        c = jnp.zeros((2 * bq, 1), F32)
        c = sweep1(i * nd + nd - 1, c, nd, True)
        c = lax.fori_loop(0, groups, lambda t, cr: sweep1(older - 1 - t * SB_GROUP, cr, SB_GROUP, False), c)
        lax.fori_loop(0, rest // nd, lambda t, cr: sweep1(rest - 1 - t * nd, cr, nd, False), c)

        def sweep2(j0, carry, n, masked):
            js = [j0 + t for t in range(n)]
            ks = [pl.multiple_of(j * SB_BK, SB_BK) for j in js]
            g16 = [g_scr[j] for j in js]
            gv = [g.astype(F32) for g in g16]
            gs = [(lax.dot_general(g, tri_fwd, NN, preferred_element_type=F32), jnp.sum(v, axis=1, keepdims=True))
                  for g, v in zip(g16, gv)]
            pc, dq = carry
            dzs = []
            for t in range(n):
                dz = gv[t] - b_scr[js[t]].astype(F32) * (gv[t] + (gs[t][0] + pc))
                if masked:
                    dz = jnp.where((ks[t] + scol) < trow, dz, 0.0)
                dzs.append(dz.astype(BF16))
                pc = pc + gs[t][1]
            for t in range(n):
                dq = dq + lax.dot_general(dzs[t], kb_ref[pl.ds(ks[t], SB_BK), :], NN, preferred_element_type=F32)
                dk_acc[js[t]] += lax.dot_general(qs_t, dzs[t], NN, preferred_element_type=F32)
                dv_acc[js[t]] += lax.dot_general(dos_t, a_scr[js[t]], NN, preferred_element_type=F32)
            return pc, dq

        carry = (jnp.zeros((2 * bq, 1), F32), jnp.zeros((2 * bq, LANES), F32))
        carry = lax.fori_loop(0, groups, lambda t, cr: sweep2(t * SB_GROUP, cr, SB_GROUP, False), carry)
        carry = lax.fori_loop(
            0, rest // nd, lambda t, cr: sweep2(groups * SB_GROUP + t * nd, cr, nd, False), carry)
        carry = sweep2(i * nd, carry, nd, True)
        dq_ref[...] = (_unstack_heads(carry[1], lane) * scale).astype(BF16)

        @pl.when(i == pl.num_programs(1) - 1)
        def _():
            for j in range(nkb):
                dk_ref[j * SB_BK:(j + 1) * SB_BK, :] = dk_acc[j].T.astype(BF16)
                dv_ref[j * SB_BK:(j + 1) * SB_BK, :] = dv_acc[j].T.astype(BF16)

    deps = [] if dep is None else [dep]
    full = pl.BlockSpec((S, LANES), lambda p, i: (0, p))
    blk = pl.BlockSpec((bq, LANES), lambda p, i: (i, p))
    act = jax.ShapeDtypeStruct((S, D), BF16)
    return pl.pallas_call(
        body, grid=(npair, S // bq), name=name, in_specs=[qspec, kspec, vspec, blk] + [ANY] * len(deps),
        out_specs=[blk, full, full], out_shape=[act, act, act],
        scratch_shapes=[pltpu.VMEM((S, LANES), BF16), pltpu.VMEM((S, LANES), BF16),
                        pltpu.VMEM((nkb, LANES, SB_BK), F32), pltpu.VMEM((nkb, LANES, SB_BK), F32),
                        pltpu.VMEM((nkb, 2 * bq, SB_BK), BF16), pltpu.VMEM((nkb, 2 * bq, SB_BK), BF16),
                        pltpu.VMEM((nkb, 2 * bq, SB_BK), BF16)],
        compiler_params=_params("parallel", "arbitrary"),
    )(proj, proj, proj, do, *deps)


def _rms_rows(v):
    r = lax.rsqrt(jnp.mean(v * v, axis=-1, keepdims=True) + EPS)
    return v * r, r


def _xa_specs(S, D, M, tq):
    nh = D // X_HEAD_DIM
    qspec = pl.BlockSpec((tq, X_HEAD_DIM), lambda h, i: (i, 6 * nh + h))
    kspec = pl.BlockSpec((M, X_HEAD_DIM), lambda h, i: (0, h))
    vspec = pl.BlockSpec((M, X_HEAD_DIM), lambda h, i: (0, nh + h))
    gspec = pl.BlockSpec((1, X_HEAD_DIM), lambda h, i: (0, 0))
    return nh, qspec, kspec, vspec, gspec


def xa_fwd(proj, kv, gq, gk, D, name):
    S, M = proj.shape[0], kv.shape[0]
    tq = _tile(S, 512)
    nh, qspec, kspec, vspec, gspec = _xa_specs(S, D, M, tq)
    scale = X_HEAD_DIM ** -0.5

    def body(q_ref, k_ref, v_ref, gq_ref, gk_ref, o_ref):
        qn = _rms_rows(q_ref[...])[0] * gq_ref[...]
        kn = _rms_rows(k_ref[...])[0] * gk_ref[...]
        s = _dot(qn, kn, NT) * scale
        e = jnp.exp(s - jnp.max(s, axis=-1, keepdims=True))
        p = e / jnp.sum(e, axis=-1, keepdims=True)
        o_ref[...] = _dot(p, v_ref[...], NN)

    return pl.pallas_call(
        body, grid=(nh, S // tq), name=name, in_specs=[qspec, kspec, vspec, gspec, gspec],
        out_specs=pl.BlockSpec((tq, X_HEAD_DIM), lambda h, i: (i, h)),
        out_shape=jax.ShapeDtypeStruct((S, D), F32), compiler_params=_params("parallel", "parallel"),
    )(proj, kv, kv, gq, gk)


def xa_bwd(proj, kv, gq, gk, do, D, name):
    S, M = proj.shape[0], kv.shape[0]
    tq = _tile(S, 512)
    nh, qspec, kspec, vspec, gspec = _xa_specs(S, D, M, tq)
    scale = X_HEAD_DIM ** -0.5

    def body(q_ref, k_ref, v_ref, gq_ref, gk_ref, do_ref, dq_ref, dk_ref, dv_ref, ggq_ref, ggk_ref,
             dkn_acc, dv_acc):
        h, i = pl.program_id(0), pl.program_id(1)

        @pl.when((h == 0) & (i == 0))
        def _():
            ggq_ref[...] = jnp.zeros_like(ggq_ref)
            ggk_ref[...] = jnp.zeros_like(ggk_ref)

        @pl.when(i == 0)
        def _():
            dkn_acc[...] = jnp.zeros_like(dkn_acc)
            dv_acc[...] = jnp.zeros_like(dv_acc)

        gq, gk = gq_ref[...], gk_ref[...]
        qhat, rq = _rms_rows(q_ref[...])
        khat, rk = _rms_rows(k_ref[...])
        qn, kn = qhat * gq, khat * gk
        s = _dot(qn, kn, NT) * scale
        e = jnp.exp(s - jnp.max(s, axis=-1, keepdims=True))
        p = e / jnp.sum(e, axis=-1, keepdims=True)
        dov = do_ref[...]
        dv_acc[...] += _dot(p, dov, TN)
        dp = _dot(dov, v_ref[...], NT)
        ds = (p * (dp - jnp.sum(dp * p, axis=-1, keepdims=True))) * scale
        dqn = _dot(ds, kn, NN)
        dkn_acc[...] += _dot(ds, qn, TN)
        ggq_ref[...] += jnp.sum(dqn * qhat, axis=0, keepdims=True)
        dqh = dqn * gq
        dq_ref[...] = (rq * (dqh - qhat * jnp.mean(dqh * qhat, axis=-1, keepdims=True))).astype(BF16)

        @pl.when(i == pl.num_programs(1) - 1)
        def _():
            dkn = dkn_acc[...]
            ggk_ref[...] += jnp.sum(dkn * khat, axis=0, keepdims=True)
            dkh = dkn * gk
            dk_ref[...] = (rk * (dkh - khat * jnp.mean(dkh * khat, axis=-1, keepdims=True))).astype(BF16)
            dv_ref[...] = dv_acc[...].astype(BF16)

    blk = pl.BlockSpec((tq, X_HEAD_DIM), lambda h, i: (i, h))
    kv_shape = jax.ShapeDtypeStruct((M, 2 * D), BF16)
    gshape = jax.ShapeDtypeStruct((1, X_HEAD_DIM), F32)
    dq, dk, dv, ggq, ggk = pl.pallas_call(
        body, grid=(nh, S // tq), name=name, in_specs=[qspec, kspec, vspec, gspec, gspec, blk],
        out_specs=[blk, kspec, vspec, gspec, gspec],
        out_shape=[jax.ShapeDtypeStruct((S, D), BF16), kv_shape, kv_shape, gshape, gshape],
        scratch_shapes=[pltpu.VMEM((M, X_HEAD_DIM), F32), pltpu.VMEM((M, X_HEAD_DIM), F32)],
        compiler_params=_params("arbitrary", "arbitrary"),
    )(proj, kv, kv, gq, gk, do)
    d_kv = jnp.concatenate([dk[:, :D], dv[:, D:]], axis=1)
    return dq, d_kv, ggq, ggk


def _gate_specs(S, D, tm):
    row = pl.BlockSpec((tm, D), lambda i: (i, 0))
    gate = lambda b: pl.BlockSpec((tm, D), lambda i, b=b: (i, 7 + b))
    return row, gate


def merge_fwd(proj, ys, D, name):
    S = proj.shape[0]
    tm = _tile(S, 256)
    row, gate = _gate_specs(S, D, tm)

    def body(g0, g1, g2, y0, y1, y2, o_ref):
        acc = jax.nn.sigmoid(g0[...]) * y0[...]
        acc = acc + jax.nn.sigmoid(g1[...]) * y1[...]
        acc = acc + jax.nn.sigmoid(g2[...]) * y2[...]
        o_ref[...] = acc.astype(BF16)

    return pl.pallas_call(
        body, grid=(S // tm,), name=name, in_specs=[gate(0), gate(1), gate(2), row, row, row],
        out_specs=row, out_shape=jax.ShapeDtypeStruct((S, D), BF16), compiler_params=_params("parallel"),
    )(proj, proj, proj, *ys)


def merge_bwd(proj, ys, dm, D, name):
    S = proj.shape[0]
    tm = _tile(S, 256)
    row, gate = _gate_specs(S, D, tm)

    def body(g0, g1, g2, y0, y1, y2, dm_ref, d0, d1, d2, dg_ref):
        dmv = dm_ref[...]
        for b, (g_ref, y_ref, d_ref) in enumerate(((g0, y0, d0), (g1, y1, d1), (g2, y2, d2))):
            s = jax.nn.sigmoid(g_ref[...])
            d_ref[...] = (dmv * s).astype(BF16)
            dg_ref[:, b * D:(b + 1) * D] = ((dmv * y_ref[...]) * (s * (1.0 - s))).astype(BF16)

    act = jax.ShapeDtypeStruct((S, D), BF16)
    return pl.pallas_call(
        body, grid=(S // tm,), name=name, in_specs=[gate(0), gate(1), gate(2), row, row, row, row],
        out_specs=[row, row, row, pl.BlockSpec((tm, 3 * D), lambda i: (i, 0))],
        out_shape=[act, act, act, jax.ShapeDtypeStruct((S, 3 * D), BF16)], compiler_params=_params("parallel"),
    )(proj, proj, proj, *ys, dm)


def loss_head(y, target, name):
    S, D = y.shape
    tm = _tile(S, 512)

    def body(y_ref, t_ref, dy_ref, dy16_ref, l_ref):
        @pl.when(pl.program_id(0) == 0)
        def _():
            l_ref[...] = jnp.zeros_like(l_ref)

        e = y_ref[...] - t_ref[...]
        dy = e * (1.0 / D)
        dy_ref[...] = dy
        dy16_ref[...] = dy.astype(BF16)
        l_ref[...] += jnp.sum(e * e, axis=0, keepdims=True)

    row = pl.BlockSpec((tm, D), lambda i: (i, 0))
    vec = pl.BlockSpec((1, D), lambda i: (0, 0))
    return pl.pallas_call(
        body, grid=(S // tm,), name=name, in_specs=[row, row], out_specs=[row, row, vec],
        out_shape=[jax.ShapeDtypeStruct((S, D), F32), jax.ShapeDtypeStruct((S, D), BF16),
                   jax.ShapeDtypeStruct((1, D), F32)],
        compiler_params=_params("arbitrary"),
    )(y, target)


def _rows2d(a):
    return a.reshape(-1, a.shape[-1])


def _ew_call(fn, ins, out_dtypes, name):
    R, C = ins[0].shape
    tr = _tile(R, max(8, (1 << 19) // C))
    spec = pl.BlockSpec((tr, C), lambda i: (i, 0))

    def body(*refs):
        outs = fn(*[r[...] for r in refs[:len(ins)]])
        for o_ref, o in zip(refs[len(ins):], outs):
            o_ref[...] = o.astype(o_ref.dtype)

    return pl.pallas_call(
        body, grid=(R // tr,), name=name, in_specs=[spec] * len(ins), out_specs=[spec] * len(out_dtypes),
        out_shape=[jax.ShapeDtypeStruct((R, C), d) for d in out_dtypes], compiler_params=_params("parallel"),
    )(*ins)


def adamw(w, g, m, v, name):
    def fn(w, g, m, v):
        m = ADAM_B1 * m + (1.0 - ADAM_B1) * g
        v = ADAM_B2 * v + (1.0 - ADAM_B2) * (g * g)
        m_hat = m / (1.0 - ADAM_B1 ** ADAM_STEP)
        v_hat = v / (1.0 - ADAM_B2 ** ADAM_STEP)
        return -ADAM_LR * (m_hat / (jnp.sqrt(v_hat) + ADAM_EPS) + ADAM_WD * w), m, v

    shp = w.shape
    outs = _ew_call(fn, [_rows2d(a) for a in (w, g, m, v)], [F32, F32, F32], name)
    return [o.reshape(shp) for o in outs]


def _placed_call(fn, place, grid, ins, in_specs, out_shape, out_specs, name, dep=None):
    n = len(ins)
    deps = [] if dep is None else [dep]

    def body(place_ref, *refs):
        outs = fn(*[r[...] for r in refs[:n]])
        for o_ref, o in zip(refs[n + len(deps):], outs):
            o_ref[...] = o.astype(o_ref.dtype)

    return pl.pallas_call(
        body, name=name, out_shape=out_shape,
        grid_spec=pltpu.PrefetchScalarGridSpec(
            num_scalar_prefetch=1, grid=grid, in_specs=list(in_specs) + [ANY] * len(deps), out_specs=out_specs),
        compiler_params=_params(*["parallel"] * len(grid)),
    )(place, *ins, *deps)


def _row_tile(R, C):
    return _tile(R, max(16, (1 << 19) // C))


def cast_into_full(w, place, name, dep=None):
    R, C = w.shape
    tr = _row_tile(R, C)
    return _placed_call(
        lambda a: (a,), place, (R // tr,), [w], [pl.BlockSpec((tr, C), lambda i, p: (i, 0))],
        [jax.ShapeDtypeStruct((N_CHIP, R, C), BF16)], [pl.BlockSpec((None, tr, C), lambda i, p: (p[0], i, 0))],
        name, dep=dep)[0]


def pair_sum(g4, got, place, name):
    _, hr, C = got.shape
    tr = _row_tile(hr, C)
    nb = hr // tr
    blk = pl.BlockSpec((None, tr, C), lambda s, i, p: (s, i, 0))
    return _placed_call(
        lambda a, b: (a + b,), place, (N_CHIP, nb), [g4, got],
        [pl.BlockSpec((None, tr, C), lambda s, i, p: (s, p[1] * nb + i, 0)), blk],
        [jax.ShapeDtypeStruct(got.shape, BF16)], [blk], name)[0]


def chip_sum(p32, got, place, name, dep=None):
    _, H, C = p32.shape
    tr = _row_tile(H, C)
    nb = H // tr
    peer = lambda j: pl.BlockSpec((None, tr, C), lambda i, p, j=j: (j, i, 0))
    return _placed_call(
        lambda a, b, c, d: (((a.astype(F32) + b.astype(F32)) + c.astype(F32)) + d.astype(F32),), place, (nb,),
        [p32, got, got, got], [pl.BlockSpec((None, tr, C), lambda i, p: (p[0], i, 0)), peer(0), peer(1), peer(2)],
        [jax.ShapeDtypeStruct((2 * H, C), F32)], [pl.BlockSpec((tr, C), lambda i, p: (p[1] * nb + i, 0))],
        name, dep=dep)[0]


ANY = pl.BlockSpec(memory_space=pl.ANY)
CHIP_FLIPS = ((1, 0), (0, 1), (1, 1))


def _place():
    return lax.axis_index("x"), lax.axis_index("y"), lax.axis_index("c")


def _flip(v, f):
    return 1 - v if f else v


def join_halves(fulls, name):
    n = len(fulls)

    def body(*refs):
        outs = refs[n:2 * n]
        send_sem, recv_sem = refs[2 * n:]
        x, y, c = _place()
        copies = []
        for a in range(n):
            hr = outs[a].shape[0] // 2
            half = outs[a].at[pl.ds(c * hr, hr), :]
            cp = pltpu.make_async_remote_copy(
                src_ref=half, dst_ref=half, send_sem=send_sem.at[a], recv_sem=recv_sem.at[a],
                device_id=(x, y, 1 - c), device_id_type=MESH)
            cp.start()
            copies.append(cp)
        for a, cp in enumerate(copies):
            hr = outs[a].shape[0] // 2
            theirs = outs[a].at[pl.ds((1 - c) * hr, hr), :]
            cp.wait_send()
            pltpu.make_async_remote_copy(
                src_ref=theirs, dst_ref=theirs, send_sem=send_sem.at[a], recv_sem=recv_sem.at[a],
                device_id=(x, y, 1 - c), device_id_type=MESH).wait_recv()

    dma = pltpu.SemaphoreType.DMA
    return pl.pallas_call(
        body, name=name, in_specs=[ANY] * n, out_specs=[ANY] * n,
        out_shape=[jax.ShapeDtypeStruct(f.shape, F32) for f in fulls],
        input_output_aliases={a: a for a in range(n)},
        scratch_shapes=[dma((n,)), dma((n,))],
    )(*fulls)


HBM = pl.BlockSpec(memory_space=pltpu.HBM)
SEM = pl.BlockSpec(memory_space=pltpu.SEMAPHORE)
EFFECT = pltpu.SideEffectType.DATAFLOW_SIDE_EFFECTING


def _in_hbm(a):
    return pltpu.with_memory_space_constraint(a, pltpu.HBM)


def _gather_half(ref, chip_idx, core):
    hr = ref.shape[1] // 2
    return ref.at[chip_idx, pl.ds(core * hr, hr), :]


def gather_forward(fulls, small, name):
    n = len(fulls)

    def body(*refs):
        small_in = refs[n]
        outs, small_out = refs[n + 1:2 * n + 1], refs[2 * n + 1]
        send_sem, recv_sem, sm_send, sm_recv, loc_sem = refs[2 * n + 2:]
        x, y, c = _place()
        mine = 2 * x + y
        chips = [(_flip(x, fx), _flip(y, fy)) for fx, fy in CHIP_FLIPS]
        local = pltpu.make_async_copy(small_in, small_out.at[mine], loc_sem)
        local.start()
        copies = []
        for j, (px, py) in enumerate(chips):
            cp = pltpu.make_async_remote_copy(
                src_ref=small_in, dst_ref=small_out.at[mine], send_sem=sm_send.at[j], recv_sem=sm_recv.at[j],
                device_id=(px, py, c), device_id_type=MESH)
            cp.start()
            copies.append(cp)
        for a in range(n):
            for j, (px, py) in enumerate(chips):
                src = _gather_half(outs[a], 2 * px + py, c)
                cp = pltpu.make_async_remote_copy(
                    src_ref=src, dst_ref=src, send_sem=send_sem.at[3 * a + j], recv_sem=recv_sem.at[3 * a + j],
                    device_id=(x, y, 1 - c), device_id_type=MESH)
                cp.start()
                copies.append(cp)
        for a in range(n):
            for j, (px, py) in enumerate(chips):
                dst = _gather_half(outs[a], 2 * px + py, 1 - c)
                pltpu.make_async_remote_copy(
                    src_ref=dst, dst_ref=dst, send_sem=send_sem.at[3 * a + j], recv_sem=recv_sem.at[3 * a + j],
                    device_id=(x, y, 1 - c), device_id_type=MESH).wait_recv()
        for j, (px, py) in enumerate(chips):
            dst = small_out.at[2 * px + py]
            pltpu.make_async_remote_copy(
                src_ref=dst, dst_ref=dst, send_sem=sm_send.at[j], recv_sem=sm_recv.at[j],
                device_id=(px, py, c), device_id_type=MESH).wait_recv()
        for cp in copies:
            cp.wait_send()
        local.wait()

    dma = pltpu.SemaphoreType.DMA
    out = pl.pallas_call(
        body, name=name, in_specs=[ANY] * (n + 1), out_specs=[ANY] * (n + 1),
        out_shape=[jax.ShapeDtypeStruct(f.shape, f.dtype) for f in fulls]
        + [jax.ShapeDtypeStruct((N_CHIP,) + small.shape, small.dtype)],
        input_output_aliases={a: a for a in range(n)},
        scratch_shapes=[dma((3 * n,)), dma((3 * n,)), dma((3,)), dma((3,)), dma],
    )(*fulls, small)
    return out[:n], out[n]


def _gather_plan(fulls, lands):
    x, y, c = _place()
    mine = 2 * x + y
    return [(_gather_half(f, mine, c), _gather_half(f, mine, c), (_flip(x, fx), _flip(y, fy), c))
            for f in fulls for fx, fy in CHIP_FLIPS]


def _scatter_plan(parts, lands):
    x, y, c = _place()
    plan = []
    for p, l in zip(parts, lands):
        for j, (fx, fy) in enumerate(CHIP_FLIPS):
            px, py = _flip(x, fx), _flip(y, fy)
            plan.append((p.at[2 * px + py], l.at[j], (px, py, c)))
    return plan


def _scatter_lands(parts):
    return [(3,) + p.shape[1:] for p in parts]


def _exchange_plan(grads, lands):
    x, y, c = _place()
    plan = []
    for g, l in zip(grads, lands):
        hr = g.shape[1] // 2
        plan.append((g.at[:, pl.ds((1 - c) * hr, hr), :], l, (x, y, 1 - c)))
    return plan


def _exchange_lands(grads):
    return [(N_CHIP, g.shape[1] // 2, g.shape[2]) for g in grads]


def split_start(plan, copies, srcs, land_shapes, deps, name):
    n, m = len(srcs), len(land_shapes)
    lands = [lax.empty(s, srcs[0].dtype) for s in land_shapes]

    def body(*refs):
        k0 = n + m + len(deps)
        send_sem, recv_sem = refs[k0], refs[k0 + 1]
        thru, token = refs[k0 + 2:k0 + 2 + n + m], refs[k0 + 2 + n + m]
        for k, (src, dst, dev) in enumerate(plan(thru[:n], thru[n:])):
            pltpu.make_async_remote_copy(src_ref=src, dst_ref=dst, send_sem=send_sem.at[k], recv_sem=recv_sem.at[k],
                                         device_id=dev, device_id_type=MESH).start()
        token[...] = jnp.zeros_like(token)

    dma = pltpu.SemaphoreType.DMA
    arrays = list(srcs) + lands
    out = pl.pallas_call(
        body, name=name,
        out_shape=(dma((copies,)), dma((copies,)), *[pltpu.HBM(a.shape, a.dtype) for a in arrays],
                   jax.ShapeDtypeStruct((8, LANES), F32)),
        in_specs=[HBM] * (n + m) + [ANY] * len(deps),
        out_specs=(SEM, SEM, *[HBM] * (n + m), pl.BlockSpec(memory_space=pltpu.VMEM)),
        input_output_aliases={a: 2 + a for a in range(n + m)},
        compiler_params=pltpu.CompilerParams(has_side_effects=EFFECT),
    )(*[_in_hbm(a) for a in arrays], *deps)
    return (out[0], out[1], list(out[2:2 + n]), list(out[2 + n:2 + n + m])), out[2 + n + m]


def split_wait(plan, handle, after, name):
    send_sem, recv_sem, srcs, lands = handle
    n, m = len(srcs), len(lands)

    def body(*refs):
        send_sem, recv_sem = refs[n + m], refs[n + m + 1]
        thru = refs[n + m + 2 + len(after):]
        for k, (src, dst, dev) in enumerate(plan(thru[:n], thru[n:])):
            cp = pltpu.make_async_remote_copy(src_ref=src, dst_ref=dst, send_sem=send_sem.at[k],
                                              recv_sem=recv_sem.at[k], device_id=dev, device_id_type=MESH)
            cp.wait_send()
            cp.wait_recv()

    arrays = list(srcs) + list(lands)
    out = pl.pallas_call(
        body, name=name, out_shape=tuple(pltpu.HBM(a.shape, a.dtype) for a in arrays),
        in_specs=[HBM] * (n + m) + [SEM, SEM] + [ANY] * len(after), out_specs=tuple([HBM] * (n + m)),
        input_output_aliases={a: a for a in range(n + m)},
        compiler_params=pltpu.CompilerParams(has_side_effects=EFFECT),
    )(*arrays, send_sem, recv_sem, *after)
    return list(out[:n]), list(out[n:])


def allreduce_small(block, name):
    R, C = block.shape

    def body(in_ref, out_ref, slots, send_sem, recv_sem):
        x, y, c = _place()
        me = 4 * x + 2 * y + c
        slots[me] = in_ref[...]
        copies = []
        for r in range(1, 8):
            fx, fy, fc = (r >> 2) & 1, (r >> 1) & 1, r & 1
            cp = pltpu.make_async_remote_copy(
                src_ref=in_ref, dst_ref=slots.at[me], send_sem=send_sem.at[r - 1], recv_sem=recv_sem.at[r - 1],
                device_id=(_flip(x, fx), _flip(y, fy), _flip(c, fc)), device_id_type=MESH)
            cp.start()
            copies.append(cp)
        for cp in copies:
            cp.wait()
        acc = slots[0]
        for d in range(1, 8):
            acc = acc + slots[d]
        out_ref[...] = acc

    vm = pl.BlockSpec(memory_space=pltpu.VMEM)
    dma = pltpu.SemaphoreType.DMA
    return pl.pallas_call(
        body, name=name, in_specs=[vm], out_specs=vm, out_shape=jax.ShapeDtypeStruct((R, C), F32),
        scratch_shapes=[pltpu.VMEM((8, R, C), F32), dma((7,)), dma((7,))],
    )(block)


def local_step(x, mem, target, g_mix, g_mem, q_norm_g, k_norm_g, g_mlp, conv_w, h, mem_n, w_in, w_in_dep,
               rest_weights, early_grads, mid_grads, late_grads, last_grads):
    S, D = x.shape
    proj = mm_nn_shard(h, w_in, "proj", dep=w_in_dep)
    a_conv = conv_fwd(proj, conv_w, D, "conv_fwd")
    o_sb = sb_fwd(proj, D, "sb_fwd")
    w_conv_out, w_sb_out, w_mem_kv, w_x_out, w_out, w_up, w_down = rest_weights(o_sb)
    kv = mm_nn_shard(mem_n, w_mem_kv, "kv")
    o_x = xa_fwd(proj, kv, q_norm_g, k_norm_g, D, "xa_fwd")
    ys = [mm_nn(a_conv, w_conv_out, "y_conv"), mm_nn(o_sb, w_sb_out, "y_sb"), mm_nn(o_x, w_x_out, "y_x")]
    merged = merge_fwd(proj, ys, D, "merge_fwd")
    x1 = mm_nn(merged, w_out, "x1", res=x)
    h2 = rms_fwd(x1, g_mlp, "rms_mlp")
    up, act = mm_nn_shard(h2, w_up, "up", relu2=True)
    x2 = mm_nn(act, w_down, "x2", res=x1)
    dy, dy16, loss_cols = loss_head(x2, target, "loss_head")
    d_up = mm_nt(dy16, w_down, "d_up", up=up, out_dtype=BF16)
    g = {"w_down": mm_tn(act, dy16, "g_w_down")}
    g["w_up"] = mm_tn(h2, d_up, "g_w_up", shard_out=True)
    dh2 = mm_nt_shard(d_up, w_up, "dh2")
    dx1, dx1_16, g["g_mlp"] = rms_bwd(x1, g_mlp, dh2, "rms_mlp_bwd", dres=dy, want16=True)
    g["w_out"] = mm_tn(merged, dx1_16, "g_w_out")
    dm = mm_nt(dx1_16, w_out, "d_merged")
    dy_c, dy_s, dy_x, d_gate = merge_bwd(proj, ys, dm, D, "merge_bwd")
    g["w_conv_out"] = mm_tn(a_conv, dy_c, "g_w_conv_out")
    g["w_sb_out"] = mm_tn(o_sb, dy_s, "g_w_sb_out")
    g["w_x_out"] = mm_tn(o_x, dy_x, "g_w_x_out")
    d_xq, d_kv, g["q_norm_g"], g["k_norm_g"] = xa_bwd(
        proj, kv, q_norm_g, k_norm_g, mm_nt(dy_x, w_x_out, "d_o_x"), D, "xa_bwd")
    g["w_mem_kv"] = mm_tn(mem_n, d_kv, "g_w_mem_kv", shard_out=True)
    g["g_mem"] = rms_bwd(mem, g_mem, mm_nt_shard(d_kv, w_mem_kv, "d_mem_n"), "rms_mem_bwd", want_dx=False)
    dep = early_grads(g)
    d_a_conv = mm_nt(dy_c, w_conv_out, "d_a_conv", dep=dep)
    d_ch, d_cb, d_cc, g["conv_w"] = conv_bwd(proj, conv_w, d_a_conv, D, "conv_bwd")
    d_o_sb = mm_nt(dy_s, w_sb_out, "d_o_sb", dep=dep)
    dq, dk, dv = sb_bwd(proj, d_o_sb, D, "sb_bwd", dep=mid_grads([d_ch, d_o_sb]))
    d_proj = jnp.concatenate([d_ch, d_cb, d_cc, dq, dk, dv, d_xq, d_gate], axis=1)
    g["w_in"] = mm_tn(h, d_proj, "g_w_in", shard_out=True)
    dh = mm_nt_shard(d_proj, w_in, "dh", dep=late_grads(g["w_in"]))
    grad_x, g["g_mix"] = rms_bwd(x, g_mix, dh, "rms_mix_bwd", dres=dx1, dep=last_grads(dh))
    return loss_cols, grad_x, g


BIG = ("w_in", "w_conv_out", "w_sb_out", "w_mem_kv", "w_x_out", "w_out", "w_up", "w_down")
REST = BIG[1:]
COL_SHARDED = ("w_in", "w_mem_kv", "w_up")
WEIGHTS = ("g_mix", "g_mem", "w_in", "conv_w", "w_conv_out", "w_sb_out", "q_norm_g", "k_norm_g",
           "w_mem_kv", "w_x_out", "w_out", "g_mlp", "w_up", "w_down")


def _pack_small(D, g_mix, g_mem, g_mlp, q_norm_g, k_norm_g, conv_w, last):
    qk = jnp.concatenate([q_norm_g, k_norm_g, jnp.zeros((1, D - 2 * X_HEAD_DIM), F32)], axis=1)
    cw = jnp.pad(conv_w, ((0, 0), (0, D - conv_w.shape[1])))
    return jnp.concatenate([g_mix, g_mem, g_mlp, qk, cw, last], axis=0)


def kernel(x, mem, g_mix, g_mem, w_in, conv_w, w_conv_out, w_sb_out, q_norm_g, k_norm_g, w_mem_kv, w_x_out, w_out, g_mlp, w_up, w_down, loss_target, m_g_mix, m_g_mem, m_w_in, m_conv_w, m_w_conv_out, m_w_sb_out, m_q_norm_g, m_k_norm_g, m_w_mem_kv, m_w_x_out, m_w_out, m_g_mlp, m_w_up, m_w_down, v_g_mix, v_g_mem, v_w_in, v_conv_w, v_w_conv_out, v_w_sb_out, v_q_norm_g, v_k_norm_g, v_w_mem_kv, v_w_x_out, v_w_out, v_g_mlp, v_w_up, v_w_down):
    S, D = x.shape[1], x.shape[2]
    w = dict(g_mix=g_mix, g_mem=g_mem, w_in=w_in, conv_w=conv_w, w_conv_out=w_conv_out, w_sb_out=w_sb_out,
             q_norm_g=q_norm_g, k_norm_g=k_norm_g, w_mem_kv=w_mem_kv, w_x_out=w_x_out, w_out=w_out,
             g_mlp=g_mlp, w_up=w_up, w_down=w_down)
    m = dict(g_mix=m_g_mix, g_mem=m_g_mem, w_in=m_w_in, conv_w=m_conv_w, w_conv_out=m_w_conv_out,
             w_sb_out=m_w_sb_out, q_norm_g=m_q_norm_g, k_norm_g=m_k_norm_g, w_mem_kv=m_w_mem_kv,
             w_x_out=m_w_x_out, w_out=m_w_out, g_mlp=m_g_mlp, w_up=m_w_up, w_down=m_w_down)
    v = dict(g_mix=v_g_mix, g_mem=v_g_mem, w_in=v_w_in, conv_w=v_conv_w, w_conv_out=v_w_conv_out,
             w_sb_out=v_w_sb_out, q_norm_g=v_q_norm_g, k_norm_g=v_k_norm_g, w_mem_kv=v_w_mem_kv,
             w_x_out=v_w_x_out, w_out=v_w_out, g_mlp=v_g_mlp, w_up=v_w_up, w_down=v_w_down)
    chip = 2 * lax.axis_index("x") + lax.axis_index("y")
    cs = conv_w.shape[2]

    place = jnp.stack([chip, lax.axis_index("c")]).astype(jnp.int32)
    cw_block = jnp.pad(conv_w[0], ((0, 5), (0, 0)))
    handle, token = split_start(_gather_plan, 3, [cast_into_full(w["w_in"][0], place, "cast_w_in")], [], [],
                                "gather_w_in_start")
    rest16 = [cast_into_full(w[k][0], place, "cast_" + k, dep=token) for k in REST]
    h = rms_fwd(x[0], g_mix, "rms_mix", dep=token)
    mem_n = rms_fwd(mem[0], g_mem, "rms_mem", dep=token)
    landed, _ = split_wait(_gather_plan, handle, [*rest16, h, mem_n], "gather_w_in_wait")
    (w_in_full,), cw_all = gather_forward(landed, cw_block, "gather_w_in_forward")
    conv_full = jnp.concatenate([cw_all[p, :3] for p in range(N_CHIP)], axis=1)
    rest_handle, rest_token = split_start(_gather_plan, 3 * len(REST), rest16, [], [w_in_full], "gather_rest_start")

    def layout(k, a):
        return a if k in COL_SHARDED else a.reshape(-1, a.shape[-1])

    def rest_weights(after):
        landed, _ = split_wait(_gather_plan, rest_handle, [after], "gather_rest_wait")
        return [layout(k, a) for k, a in zip(REST, gather_forward(landed, cw_block, "gather_rest_forward")[0])]

    def blocks(k, a):
        return a if k in COL_SHARDED else a.reshape(N_CHIP, -1, a.shape[-1])

    early = {}

    def early_grads(g):
        early["g4"] = [blocks(k, g[k]) for k in REST]
        early["swap"], token = split_start(_exchange_plan, len(REST), early["g4"], _exchange_lands(early["g4"]),
                                           [g["g_mem"]], "exchange_rest_start")
        return token

    def mid_grads(after):
        g4, got = split_wait(_exchange_plan, early["swap"], after, "exchange_rest_wait")
        p16 = [pair_sum(a, b, place, "pair_sum_" + k) for k, a, b in zip(REST, g4, got)]
        early["fly"], token = split_start(_scatter_plan, 3 * len(REST), p16, _scatter_lands(p16), [],
                                          "scatter_rest_start")
        return token

    late = {}

    def late_grads(gw):
        late["swap"], token = split_start(_exchange_plan, 1, [gw], _exchange_lands([gw]), [], "exchange_w_in_start")
        return token

    def last_grads(after):
        (gw,), (got,) = split_wait(_exchange_plan, late["swap"], [after], "exchange_w_in_wait")
        p16 = [pair_sum(gw, got, place, "pair_sum_w_in")]
        late["fly"], token = split_start(_scatter_plan, 3, p16, _scatter_lands(p16), [], "scatter_w_in_start")
        return token

    loss_cols, grad_x, g = local_step(
        x[0], mem[0], loss_target[0], g_mix, g_mem, q_norm_g, k_norm_g, g_mlp, conv_full, h, mem_n,
        w_in_full, rest_token, rest_weights, early_grads, mid_grads, late_grads, last_grads)
    token = g["g_mix"]

    small = allreduce_small(
        _pack_small(D, g["g_mix"], g["g_mem"], g["g_mlp"], g["q_norm_g"], g["k_norm_g"], g["conv_w"], loss_cols),
        "allreduce_small")
    loss = (0.5 / D) * jnp.sum(small[7])
    gsum = {"g_mix": small[0:1], "g_mem": small[1:2], "g_mlp": small[2:3],
            "q_norm_g": small[3:4, :X_HEAD_DIM], "k_norm_g": small[3:4, X_HEAD_DIM:2 * X_HEAD_DIM],
            "conv_w": lax.dynamic_slice(small[4:7], (0, chip * cs), (3, cs))[None]}

    p16_rest, got_rest = split_wait(_scatter_plan, early["fly"], [token], "scatter_rest_wait")

    delta, new_m, new_v = {}, {}, {}
    halves = [chip_sum(p, b, place, "chip_sum_" + k, dep=token) for k, p, b in zip(REST, p16_rest, got_rest)]
    for k, a in zip(REST, join_halves(halves, "join_halves_rest")):
        gsum[k] = a[None]
        delta[k], new_m[k], new_v[k] = adamw(w[k], gsum[k], m[k], v[k], "adamw_" + k)
    small_names = ("g_mix", "g_mem", "g_mlp", "q_norm_g", "k_norm_g", "conv_w")
    zero_row = jnp.zeros((1, D), F32)
    packed = [_pack_small(D, *[t[k] if k != "conv_w" else t[k][0] for k in small_names], zero_row)
              for t in (w, gsum, m, v)]
    sm = adamw(*packed, "adamw_small")
    for t, block in zip((delta, new_m, new_v), sm):
        t["g_mix"], t["g_mem"], t["g_mlp"] = block[0:1], block[1:2], block[2:3]
        t["q_norm_g"], t["k_norm_g"] = block[3:4, :X_HEAD_DIM], block[3:4, X_HEAD_DIM:2 * X_HEAD_DIM]
        t["conv_w"] = block[4:7, :cs][None]

    done = [new_v[k] for k in REST] + [sm[2]]
    p16_in, got_in = split_wait(_scatter_plan, late["fly"], done, "scatter_w_in_wait")
    half_in = chip_sum(p16_in[0], got_in[0], place, "chip_sum_w_in")
    gsum["w_in"] = join_halves([half_in], "join_halves_w_in")[0][None]
    delta["w_in"], new_m["w_in"], new_v["w_in"] = adamw(w["w_in"], gsum["w_in"], m["w_in"], v["w_in"], "adamw_w_in")

    return (loss, grad_x[None], *[gsum[k] for k in WEIGHTS], *[delta[k] for k in WEIGHTS],
            *[new_m[k] for k in WEIGHTS], *[new_v[k] for k in WEIGHTS])
```

```python
import functools

import jax
import jax.numpy as jnp
from jax import lax
from jax.experimental import pallas as pl
from jax.experimental.pallas import tpu as pltpu

F32 = jnp.float32
BF16 = jnp.bfloat16
EPS = 1e-6
N_CHIP = 4
SB_HEAD_DIM = 64
X_HEAD_DIM = 256
LANES = 128
VMEM_LIMIT = 56 * 1024 * 1024
ADAM_LR, ADAM_B1, ADAM_B2, ADAM_EPS, ADAM_WD, ADAM_STEP = 0.001, 0.9, 0.999, 1e-8, 0.01, 10
MESH = pl.DeviceIdType.MESH


def _params(*sem):
    return pltpu.CompilerParams(dimension_semantics=sem, vmem_limit_bytes=VMEM_LIMIT)


def _tile(n, pref):
    if n <= pref:
        return n
    t = 1 << (pref.bit_length() - 1)
    while n % t:
        t //= 2
    return t


NN = (((1,), (0,)), ((), ()))
NT = (((1,), (1,)), ((), ()))
TN = (((0,), (0,)), ((), ()))


def _dot(a, b, dims):
    return lax.dot_general(a.astype(BF16), b.astype(BF16), dims, preferred_element_type=F32)


def mm_nn_shard(a, g, name, relu2=False, dep=None):
    M, K = a.shape
    _, _, Ns = g.shape
    tm, tn = _tile(M, 2048), _tile(Ns, 512)
    nb = Ns // tn

    def body(a_ref, b_ref, *o_refs):
        o_refs = o_refs[len(deps):]
        acc = _dot(a_ref[...], b_ref[...], NN)
        o_refs[0][...] = acc
        if relu2:
            r = jnp.maximum(acc, 0.0)
            o_refs[1][...] = (r * r).astype(BF16)

    o_spec = pl.BlockSpec((tm, tn), lambda i, j: (i, j))
    shapes = [jax.ShapeDtypeStruct((M, N_CHIP * Ns), F32)]
    specs = [o_spec]
    if relu2:
        shapes.append(jax.ShapeDtypeStruct((M, N_CHIP * Ns), BF16))
        specs.append(o_spec)
    deps = [] if dep is None else [dep]
    out = pl.pallas_call(
        body, grid=(M // tm, N_CHIP * nb), name=name,
        in_specs=[pl.BlockSpec((tm, K), lambda i, j: (i, 0)),
                  pl.BlockSpec((None, K, tn), lambda i, j: (j // nb, 0, j % nb))] + [ANY] * len(deps),
        out_specs=specs, out_shape=shapes, compiler_params=_params("parallel", "parallel"),
    )(a, g, *deps)
    return out if relu2 else out[0]


def mm_nn(a, w, name, res=None, out_dtype=F32):
    M, K = a.shape
    N = w.shape[1]
    tm, tn = _tile(M, 2048 if K <= 2048 else 1024), _tile(N, 512)

    def body(a_ref, b_ref, *refs):
        acc = _dot(a_ref[...], b_ref[...], NN)
        if res is not None:
            acc = refs[0][...] + acc
        refs[-1][...] = acc.astype(out_dtype)

    o_spec = pl.BlockSpec((tm, tn), lambda i, j: (i, j))
    ins = [a, w] + ([res] if res is not None else [])
    return pl.pallas_call(
        body, grid=(M // tm, N // tn), name=name,
        in_specs=[pl.BlockSpec((tm, K), lambda i, j: (i, 0)), pl.BlockSpec((K, tn), lambda i, j: (0, j))]
        + ([o_spec] if res is not None else []),
        out_specs=o_spec, out_shape=jax.ShapeDtypeStruct((M, N), out_dtype),
        compiler_params=_params("parallel", "parallel"),
    )(*ins)


def mm_nt(a, w, name, up=None, out_dtype=F32, dep=None):
    M, N = a.shape
    R = w.shape[0]
    tm, tr = _tile(M, 2048), _tile(R, 512)

    def body(a_ref, b_ref, *refs):
        acc = _dot(a_ref[...], b_ref[...], NT)
        if up is not None:
            acc = acc * (2.0 * jnp.maximum(refs[0][...], 0.0))
        refs[-1][...] = acc.astype(out_dtype)

    o_spec = pl.BlockSpec((tm, tr), lambda i, j: (i, j))
    ins = [a, w] + ([up] if up is not None else []) + ([dep] if dep is not None else [])
    return pl.pallas_call(
        body, grid=(M // tm, R // tr), name=name,
        in_specs=[pl.BlockSpec((tm, N), lambda i, j: (i, 0)), pl.BlockSpec((tr, N), lambda i, j: (j, 0))]
        + ([o_spec] if up is not None else []) + ([ANY] if dep is not None else []),
        out_specs=o_spec, out_shape=jax.ShapeDtypeStruct((M, R), out_dtype),
        compiler_params=_params("parallel", "parallel"),
    )(*ins)


def mm_nt_shard(a, g, name, out_dtype=F32, dep=None):
    deps = [] if dep is None else [dep]
    M = a.shape[0]
    _, R, Ns = g.shape
    tm, tr, tk = _tile(M, 1024), _tile(R, 1024), _tile(Ns, 2560)
    nb = Ns // tk
    nk = N_CHIP * nb

    def body(a_ref, b_ref, *refs):
        o_ref, acc_ref = refs[len(deps):]
        k = pl.program_id(2)

        @pl.when(k == 0)
        def _():
            acc_ref[...] = jnp.zeros_like(acc_ref)

        acc_ref[...] += _dot(a_ref[...], b_ref[...], NT)

        @pl.when(k == nk - 1)
        def _():
            o_ref[...] = acc_ref[...].astype(out_dtype)

    return pl.pallas_call(
        body, grid=(M // tm, R // tr, nk), name=name,
        in_specs=[pl.BlockSpec((tm, tk), lambda i, j, k: (i, k)),
                  pl.BlockSpec((None, tr, tk), lambda i, j, k: (k // nb, j, k % nb))] + [ANY] * len(deps),
        out_specs=pl.BlockSpec((tm, tr), lambda i, j, k: (i, j)),
        out_shape=jax.ShapeDtypeStruct((M, R), out_dtype),
        scratch_shapes=[pltpu.VMEM((tm, tr), F32)],
        compiler_params=_params("parallel", "parallel", "arbitrary"),
    )(a, g, *deps)


def mm_tn(a, b, name, shard_out=False):
    S, M = a.shape
    N = b.shape[1]
    Ns = N // N_CHIP if shard_out else N
    tm, tn = _tile(M, 1024), _tile(Ns, 512)
    nb = Ns // tn

    def body(a_ref, b_ref, o_ref):
        o_ref[...] = _dot(a_ref[...], b_ref[...], TN)

    if shard_out:
        o_spec = pl.BlockSpec((None, tm, tn), lambda i, j: (j // nb, i, j % nb))
        o_shape = jax.ShapeDtypeStruct((N_CHIP, M, Ns), F32)
    else:
        o_spec = pl.BlockSpec((tm, tn), lambda i, j: (i, j))
        o_shape = jax.ShapeDtypeStruct((M, N), F32)
    return pl.pallas_call(
        body, grid=(M // tm, N // tn), name=name,
        in_specs=[pl.BlockSpec((S, tm), lambda i, j: (0, i)), pl.BlockSpec((S, tn), lambda i, j: (0, j))],
        out_specs=o_spec, out_shape=o_shape, compiler_params=_params("parallel", "parallel"),
    )(a, b)


def rms_fwd(x, g, name, dep=None):
    S, D = x.shape
    tm = _tile(S, 512)
    deps = [] if dep is None else [dep]

    def body(x_ref, g_ref, *refs):
        xv = x_ref[...]
        r = lax.rsqrt(jnp.mean(xv * xv, axis=-1, keepdims=True) + EPS)
        refs[-1][...] = ((xv * r) * g_ref[...]).astype(BF16)

    return pl.pallas_call(
        body, grid=(S // tm,), name=name,
        in_specs=[pl.BlockSpec((tm, D), lambda i: (i, 0)), pl.BlockSpec((1, D), lambda i: (0, 0))]
        + [ANY] * len(deps),
        out_specs=pl.BlockSpec((tm, D), lambda i: (i, 0)),
        out_shape=jax.ShapeDtypeStruct((S, D), BF16), compiler_params=_params("parallel"),
    )(x, g, *deps)


def rms_bwd(x, g, dh, name, dres=None, want_dx=True, want16=False, dep=None):
    S, D = x.shape
    tm = _tile(S, 512)

    def body(x_ref, g_ref, dh_ref, *refs):
        i = pl.program_id(0)
        xv = x_ref[...]
        r = lax.rsqrt(jnp.mean(xv * xv, axis=-1, keepdims=True) + EPS)
        xn = xv * r
        dhv = dh_ref[...].astype(F32)
        gg_ref = refs[-1]

        @pl.when(i == 0)
        def _():
            gg_ref[...] = jnp.zeros_like(gg_ref)

        gg_ref[...] += jnp.sum(dhv * xn, axis=0, keepdims=True)
        if want_dx:
            dxn = dhv * g_ref[...]
            dx = r * (dxn - xn * jnp.mean(dxn * xn, axis=-1, keepdims=True))
            if dres is not None:
                dx = refs[0][...] + dx
            refs[-2][...] = dx.astype(refs[-2].dtype)
            if want16:
                refs[-3][...] = dx

    row = pl.BlockSpec((tm, D), lambda i: (i, 0))
    vec = pl.BlockSpec((1, D), lambda i: (0, 0))
    ins, in_specs = [x, g, dh], [row, vec, row]
    if dres is not None:
        ins.append(dres)
        in_specs.append(row)
    if dep is not None:
        ins.append(dep)
        in_specs.append(ANY)
    shapes, specs = [jax.ShapeDtypeStruct((1, D), F32)], [vec]
    if want_dx:
        if want16:
            shapes.insert(0, jax.ShapeDtypeStruct((S, D), BF16))
            specs.insert(0, row)
        shapes.insert(0, jax.ShapeDtypeStruct((S, D), F32))
        specs.insert(0, row)
    out = pl.pallas_call(body, grid=(S // tm,), name=name, in_specs=in_specs, out_specs=specs,
                         out_shape=shapes, compiler_params=_params("arbitrary"))(*ins)
    return out if want_dx else out[0]


def _shift_down(u, k, row):
    return jnp.where(row >= k, pltpu.roll(u, k, axis=0), 0.0)


def _shift_up(u, k, row):
    S = u.shape[0]
    return jnp.where(row < S - k, pltpu.roll(u, S - k, axis=0), 0.0)


def _conv_specs(S, D, tc):
    nb = D // tc
    col = lambda o: pl.BlockSpec((S, tc), lambda j, o=o: (0, o * nb + j))
    return col, pl.BlockSpec((3, tc), lambda j: (0, j))


def conv_fwd(proj, conv_w, D, name):
    S = proj.shape[0]
    tc = _tile(D, 256)
    col, wspec = _conv_specs(S, D, tc)

    def body(ch_ref, cb_ref, cc_ref, w_ref, a_ref):
        row = lax.broadcasted_iota(jnp.int32, (S, tc), 0)
        u = cc_ref[...] * ch_ref[...]
        w = w_ref[...]
        cv = w[0:1, :] * _shift_down(u, 2, row) + w[1:2, :] * _shift_down(u, 1, row) + w[2:3, :] * u
        a_ref[...] = (cb_ref[...] * cv).astype(BF16)

    return pl.pallas_call(
        body, grid=(D // tc,), name=name, in_specs=[col(0), col(1), col(2), wspec],
        out_specs=pl.BlockSpec((S, tc), lambda j: (0, j)),
        out_shape=jax.ShapeDtypeStruct((S, D), BF16), compiler_params=_params("parallel"),
    )(proj, proj, proj, conv_w)


def conv_bwd(proj, conv_w, da, D, name):
    S = proj.shape[0]
    tc = _tile(D, 256)
    col, wspec = _conv_specs(S, D, tc)
    blk = pl.BlockSpec((S, tc), lambda j: (0, j))

    def body(ch_ref, cb_ref, cc_ref, w_ref, da_ref, dch_ref, dcb_ref, dcc_ref, gw_ref):
        row = lax.broadcasted_iota(jnp.int32, (S, tc), 0)
        ch, cb, cc, dav = ch_ref[...], cb_ref[...], cc_ref[...], da_ref[...]
        w = w_ref[...]
        u = cc * ch
        u1, u2 = _shift_down(u, 1, row), _shift_down(u, 2, row)
        cv = w[0:1, :] * u2 + w[1:2, :] * u1 + w[2:3, :] * u
        dcb_ref[...] = (dav * cv).astype(BF16)
        dcv = dav * cb
        gw_ref[0:1, :] = jnp.sum(dcv * u2, axis=0, keepdims=True)
        gw_ref[1:2, :] = jnp.sum(dcv * u1, axis=0, keepdims=True)
        gw_ref[2:3, :] = jnp.sum(dcv * u, axis=0, keepdims=True)
        du = w[2:3, :] * dcv + w[1:2, :] * _shift_up(dcv, 1, row) + w[0:1, :] * _shift_up(dcv, 2, row)
        dcc_ref[...] = (du * ch).astype(BF16)
        dch_ref[...] = (du * cc).astype(BF16)

    act = jax.ShapeDtypeStruct((S, D), BF16)
    return pl.pallas_call(
        body, grid=(D // tc,), name=name, in_specs=[col(0), col(1), col(2), wspec, blk],
        out_specs=[blk, blk, blk, wspec], out_shape=[act, act, act, jax.ShapeDtypeStruct((3, D), F32)],
        compiler_params=_params("parallel"),
    )(proj, proj, proj, conv_w, da)


SB_BQ_FWD = 512
SB_BQ_BWD = 256
SB_BK = 128
SB_GROUP = 4


def _sb_consts(bq):
    lane = lax.broadcasted_iota(jnp.int32, (bq, LANES), 1)
    r = lax.broadcasted_iota(jnp.int32, (SB_BK, SB_BK), 0)
    c = lax.broadcasted_iota(jnp.int32, (SB_BK, SB_BK), 1)
    tri_rev = jnp.where(r > c, 1.0, 0.0).astype(BF16)
    tri_fwd = jnp.where(r < c, 1.0, 0.0).astype(BF16)
    return lane, tri_rev, tri_fwd


def _cumsum2(v, tri):
    hi = v.astype(BF16)
    lo = (v - hi.astype(F32)).astype(BF16)
    part = (lax.dot_general(hi, tri, NN, preferred_element_type=F32)
            + lax.dot_general(lo, tri, NN, preferred_element_type=F32))
    return part, jnp.sum(v, axis=1, keepdims=True)


def _sb_logits(z, past):
    sp = jnp.log(1.0 + jnp.exp(-jnp.abs(z)))
    l = jnp.minimum(z, 0.0) - sp
    m = l - z
    if past is not None:
        m = jnp.where(past, m, 0.0)
    return l, m


def _stack_heads(v, lane):
    return jnp.concatenate([jnp.where(lane < SB_HEAD_DIM, v, 0.0), jnp.where(lane >= SB_HEAD_DIM, v, 0.0)],
                           axis=0).astype(BF16)


def _unstack_heads(v, lane):
    bq = v.shape[0] // 2
    return jnp.where(lane < SB_HEAD_DIM, v[:bq], v[bq:])


def _sb_positions(i, bq):
    r = lax.broadcasted_iota(jnp.int32, (2 * bq, SB_BK), 0)
    trow = i * bq + jnp.where(r >= bq, r - bq, r)
    return trow, lax.broadcasted_iota(jnp.int32, (2 * bq, SB_BK), 1)


def _sb_specs(S, D, bq):
    npair = D // LANES
    qspec = pl.BlockSpec((bq, LANES), lambda p, i: (i, 3 * npair + p))
    kspec = pl.BlockSpec((S, LANES), lambda p, i: (0, 4 * npair + p))
    vspec = pl.BlockSpec((S, LANES), lambda p, i: (0, 5 * npair + p))
    return npair, qspec, kspec, vspec


def sb_fwd(proj, D, name):
    S = proj.shape[0]
    bq = min(SB_BQ_FWD, S)
    nd = bq // SB_BK
    npair, qspec, kspec, vspec = _sb_specs(S, D, bq)
    scale = SB_HEAD_DIM ** -0.5

    def body(q_ref, k_ref, v_ref, o_ref, kb_ref, vb_ref):
        i = pl.program_id(1)

        @pl.when(i == 0)
        def _():
            kb_ref[...] = k_ref[...].astype(BF16)
            vb_ref[...] = v_ref[...].astype(BF16)

        lane, tri_rev, _ = _sb_consts(bq)
        qs = _stack_heads(q_ref[...] * scale, lane)
        trow, scol = _sb_positions(i, bq)

        def steps(j0, carry, n, masked):
            ks = [pl.multiple_of((j0 - t) * SB_BK, SB_BK) for t in range(n)]
            past = [(k + scol) < trow if masked else None for k in ks]
            zs = [lax.dot_general(qs, kb_ref[pl.ds(k, SB_BK), :], NT, preferred_element_type=F32) for k in ks]
            lm = [_sb_logits(z, p) for z, p in zip(zs, past)]
            cs = [_cumsum2(m, tri_rev) for _, m in lm]
            c, acc = carry
            for t in range(n):
                a = jnp.exp(lm[t][0] + (cs[t][0] + c))
                if masked:
                    a = jnp.where(past[t], a, 0.0)
                acc = acc + lax.dot_general(a.astype(BF16), vb_ref[pl.ds(ks[t], SB_BK), :], NN,
                                            preferred_element_type=F32)
                c = c + cs[t][1]
            return c, acc

        carry = (jnp.zeros((2 * bq, 1), F32), jnp.zeros((2 * bq, LANES), F32))
        carry = steps(i * nd + nd - 1, carry, nd, True)
        older = i * nd
        groups = older // SB_GROUP
        carry = lax.fori_loop(
            0, groups, lambda t, cr: steps(older - 1 - t * SB_GROUP, cr, SB_GROUP, False), carry)
        rest = older - groups * SB_GROUP
        carry = lax.fori_loop(0, rest // nd, lambda t, cr: steps(rest - 1 - t * nd, cr, nd, False), carry)
        o_ref[...] = _unstack_heads(carry[1], lane)

    return pl.pallas_call(
        body, grid=(npair, S // bq), name=name, in_specs=[qspec, kspec, vspec],
        out_specs=pl.BlockSpec((bq, LANES), lambda p, i: (i, p)),
        out_shape=jax.ShapeDtypeStruct((S, D), F32),
        scratch_shapes=[pltpu.VMEM((S, LANES), BF16), pltpu.VMEM((S, LANES), BF16)],
        compiler_params=_params("parallel", "arbitrary"),
    )(proj, proj, proj)


def sb_bwd(proj, do, D, name, dep=None):
    S = proj.shape[0]
    bq = min(SB_BQ_BWD, S)
    nd = bq // SB_BK
    nkb = S // SB_BK
    npair, qspec, kspec, vspec = _sb_specs(S, D, bq)
    scale = SB_HEAD_DIM ** -0.5

    def body(q_ref, k_ref, v_ref, do_ref, *refs):
        dq_ref, dk_ref, dv_ref, kb_ref, vb_ref, dk_acc, dv_acc, g_scr, b_scr, a_scr = refs[len(deps):]
        i = pl.program_id(1)

        @pl.when(i == 0)
        def _():
            kb_ref[...] = k_ref[...].astype(BF16)
            vb_ref[...] = v_ref[...].astype(BF16)
            dk_acc[...] = jnp.zeros_like(dk_acc)
            dv_acc[...] = jnp.zeros_like(dv_acc)

        lane, tri_rev, tri_fwd = _sb_consts(bq)
        qs = _stack_heads(q_ref[...] * scale, lane)
        dos = _stack_heads(do_ref[...], lane)
        qs_t, dos_t = qs.T, dos.T
        trow, scol = _sb_positions(i, bq)

        def sweep1(j0, c, n, masked):
            js = [j0 - t for t in range(n)]
            ks = [pl.multiple_of(j * SB_BK, SB_BK) for j in js]
            past = [(k + scol) < trow if masked else None for k in ks]
            zs = [lax.dot_general(qs, kb_ref[pl.ds(k, SB_BK), :], NT, preferred_element_type=F32) for k in ks]
            das = [lax.dot_general(dos, vb_ref[pl.ds(k, SB_BK), :], NT, preferred_element_type=F32) for k in ks]
            lm = [_sb_logits(z, p) for z, p in zip(zs, past)]
            cs = [_cumsum2(m, tri_rev) for _, m in lm]
            for t in range(n):
                b_scr[js[t]] = jnp.exp(lm[t][0]).astype(BF16)
            for t in range(n):
                a = jnp.exp(lm[t][0] + (cs[t][0] + c))
                if masked:
                    a = jnp.where(past[t], a, 0.0)
                g_scr[js[t]] = (das[t] * a).astype(BF16)
                a_scr[js[t]] = a.astype(BF16)
                c = c + cs[t][1]
            return c

        older = i * nd
        groups = older // SB_GROUP
        rest = older - groups * SB_GROUP
        c = jnp.zeros((2 * bq, 1), F32)
        c = sweep1(i * nd + nd - 1, c, nd, True)
        c = lax.fori_loop(0, groups, lambda t, cr: sweep1(older - 1 - t * SB_GROUP, cr, SB_GROUP, False), c)
        lax.fori_loop(0, rest // nd, lambda t, cr: sweep1(rest - 1 - t * nd, cr, nd, False), c)

        def sweep2(j0, carry, n, masked):
            js = [j0 + t for t in range(n)]
            ks = [pl.multiple_of(j * SB_BK, SB_BK) for j in js]
            g16 = [g_scr[j] for j in js]
            gv = [g.astype(F32) for g in g16]
            gs = [(lax.dot_general(g, tri_fwd, NN, preferred_element_type=F32), jnp.sum(v, axis=1, keepdims=True))
                  for g, v in zip(g16, gv)]
            pc, dq = carry
            dzs = []
            for t in range(n):
                dz = gv[t] - b_scr[js[t]].astype(F32) * (gv[t] + (gs[t][0] + pc))
                if masked:
                    dz = jnp.where((ks[t] + scol) < trow, dz, 0.0)
                dzs.append(dz.astype(BF16))
                pc = pc + gs[t][1]
            for t in range(n):
                dq = dq + lax.dot_general(dzs[t], kb_ref[pl.ds(ks[t], SB_BK), :], NN, preferred_element_type=F32)
                dk_acc[js[t]] += lax.dot_general(qs_t, dzs[t], NN, preferred_element_type=F32)
                dv_acc[js[t]] += lax.dot_general(dos_t, a_scr[js[t]], NN, preferred_element_type=F32)
            return pc, dq

        carry = (jnp.zeros((2 * bq, 1), F32), jnp.zeros((2 * bq, LANES), F32))
        carry = lax.fori_loop(0, groups, lambda t, cr: sweep2(t * SB_GROUP, cr, SB_GROUP, False), carry)
        carry = lax.fori_loop(
            0, rest // nd, lambda t, cr: sweep2(groups * SB_GROUP + t * nd, cr, nd, False), carry)
        carry = sweep2(i * nd, carry, nd, True)
        dq_ref[...] = (_unstack_heads(carry[1], lane) * scale).astype(BF16)

        @pl.when(i == pl.num_programs(1) - 1)
        def _():
            for j in range(nkb):
                dk_ref[j * SB_BK:(j + 1) * SB_BK, :] = dk_acc[j].T.astype(BF16)
                dv_ref[j * SB_BK:(j + 1) * SB_BK, :] = dv_acc[j].T.astype(BF16)

    deps = [] if dep is None else [dep]
    full = pl.BlockSpec((S, LANES), lambda p, i: (0, p))
    blk = pl.BlockSpec((bq, LANES), lambda p, i: (i, p))
    act = jax.ShapeDtypeStruct((S, D), BF16)
    return pl.pallas_call(
        body, grid=(npair, S // bq), name=name, in_specs=[qspec, kspec, vspec, blk] + [ANY] * len(deps),
        out_specs=[blk, full, full], out_shape=[act, act, act],
        scratch_shapes=[pltpu.VMEM((S, LANES), BF16), pltpu.VMEM((S, LANES), BF16),
                        pltpu.VMEM((nkb, LANES, SB_BK), F32), pltpu.VMEM((nkb, LANES, SB_BK), F32),
                        pltpu.VMEM((nkb, 2 * bq, SB_BK), BF16), pltpu.VMEM((nkb, 2 * bq, SB_BK), BF16),
                        pltpu.VMEM((nkb, 2 * bq, SB_BK), BF16)],
        compiler_params=_params("parallel", "arbitrary"),
    )(proj, proj, proj, do, *deps)


def _rms_rows(v):
    r = lax.rsqrt(jnp.mean(v * v, axis=-1, keepdims=True) + EPS)
    return v * r, r


def _xa_specs(S, D, M, tq):
    nh = D // X_HEAD_DIM
    qspec = pl.BlockSpec((tq, X_HEAD_DIM), lambda h, i: (i, 6 * nh + h))
    kspec = pl.BlockSpec((M, X_HEAD_DIM), lambda h, i: (0, h))
    vspec = pl.BlockSpec((M, X_HEAD_DIM), lambda h, i: (0, nh + h))
    gspec = pl.BlockSpec((1, X_HEAD_DIM), lambda h, i: (0, 0))
    return nh, qspec, kspec, vspec, gspec


def xa_fwd(proj, kv, gq, gk, D, name):
    S, M = proj.shape[0], kv.shape[0]
    tq = _tile(S, 512)
    nh, qspec, kspec, vspec, gspec = _xa_specs(S, D, M, tq)
    scale = X_HEAD_DIM ** -0.5

    def body(q_ref, k_ref, v_ref, gq_ref, gk_ref, o_ref):
        qn = _rms_rows(q_ref[...])[0] * gq_ref[...]
        kn = _rms_rows(k_ref[...])[0] * gk_ref[...]
        s = _dot(qn, kn, NT) * scale
        e = jnp.exp(s - jnp.max(s, axis=-1, keepdims=True))
        p = e / jnp.sum(e, axis=-1, keepdims=True)
        o_ref[...] = _dot(p, v_ref[...], NN)

    return pl.pallas_call(
        body, grid=(nh, S // tq), name=name, in_specs=[qspec, kspec, vspec, gspec, gspec],
        out_specs=pl.BlockSpec((tq, X_HEAD_DIM), lambda h, i: (i, h)),
        out_shape=jax.ShapeDtypeStruct((S, D), F32), compiler_params=_params("parallel", "parallel"),
    )(proj, kv, kv, gq, gk)


def xa_bwd(proj, kv, gq, gk, do, D, name):
    S, M = proj.shape[0], kv.shape[0]
    tq = _tile(S, 512)
    nh, qspec, kspec, vspec, gspec = _xa_specs(S, D, M, tq)
    scale = X_HEAD_DIM ** -0.5

    def body(q_ref, k_ref, v_ref, gq_ref, gk_ref, do_ref, dq_ref, dk_ref, dv_ref, ggq_ref, ggk_ref,
             dkn_acc, dv_acc):
        h, i = pl.program_id(0), pl.program_id(1)

        @pl.when((h == 0) & (i == 0))
        def _():
            ggq_ref[...] = jnp.zeros_like(ggq_ref)
            ggk_ref[...] = jnp.zeros_like(ggk_ref)

        @pl.when(i == 0)
        def _():
            dkn_acc[...] = jnp.zeros_like(dkn_acc)
            dv_acc[...] = jnp.zeros_like(dv_acc)

        gq, gk = gq_ref[...], gk_ref[...]
        qhat, rq = _rms_rows(q_ref[...])
        khat, rk = _rms_rows(k_ref[...])
        qn, kn = qhat * gq, khat * gk
        s = _dot(qn, kn, NT) * scale
        e = jnp.exp(s - jnp.max(s, axis=-1, keepdims=True))
        p = e / jnp.sum(e, axis=-1, keepdims=True)
        dov = do_ref[...]
        dv_acc[...] += _dot(p, dov, TN)
        dp = _dot(dov, v_ref[...], NT)
        ds = (p * (dp - jnp.sum(dp * p, axis=-1, keepdims=True))) * scale
        dqn = _dot(ds, kn, NN)
        dkn_acc[...] += _dot(ds, qn, TN)
        ggq_ref[...] += jnp.sum(dqn * qhat, axis=0, keepdims=True)
        dqh = dqn * gq
        dq_ref[...] = (rq * (dqh - qhat * jnp.mean(dqh * qhat, axis=-1, keepdims=True))).astype(BF16)

        @pl.when(i == pl.num_programs(1) - 1)
        def _():
            dkn = dkn_acc[...]
            ggk_ref[...] += jnp.sum(dkn * khat, axis=0, keepdims=True)
            dkh = dkn * gk
            dk_ref[...] = (rk * (dkh - khat * jnp.mean(dkh * khat, axis=-1, keepdims=True))).astype(BF16)
            dv_ref[...] = dv_acc[...].astype(BF16)

    blk = pl.BlockSpec((tq, X_HEAD_DIM), lambda h, i: (i, h))
    kv_shape = jax.ShapeDtypeStruct((M, 2 * D), BF16)
    gshape = jax.ShapeDtypeStruct((1, X_HEAD_DIM), F32)
    dq, dk, dv, ggq, ggk = pl.pallas_call(
        body, grid=(nh, S // tq), name=name, in_specs=[qspec, kspec, vspec, gspec, gspec, blk],
        out_specs=[blk, kspec, vspec, gspec, gspec],
        out_shape=[jax.ShapeDtypeStruct((S, D), BF16), kv_shape, kv_shape, gshape, gshape],
        scratch_shapes=[pltpu.VMEM((M, X_HEAD_DIM), F32), pltpu.VMEM((M, X_HEAD_DIM), F32)],
        compiler_params=_params("arbitrary", "arbitrary"),
    )(proj, kv, kv, gq, gk, do)
    d_kv = jnp.concatenate([dk[:, :D], dv[:, D:]], axis=1)
    return dq, d_kv, ggq, ggk


def _gate_specs(S, D, tm):
    row = pl.BlockSpec((tm, D), lambda i: (i, 0))
    gate = lambda b: pl.BlockSpec((tm, D), lambda i, b=b: (i, 7 + b))
    return row, gate


def merge_fwd(proj, ys, D, name):
    S = proj.shape[0]
    tm = _tile(S, 256)
    row, gate = _gate_specs(S, D, tm)

    def body(g0, g1, g2, y0, y1, y2, o_ref):
        acc = jax.nn.sigmoid(g0[...]) * y0[...]
        acc = acc + jax.nn.sigmoid(g1[...]) * y1[...]
        acc = acc + jax.nn.sigmoid(g2[...]) * y2[...]
        o_ref[...] = acc.astype(BF16)

    return pl.pallas_call(
        body, grid=(S // tm,), name=name, in_specs=[gate(0), gate(1), gate(2), row, row, row],
        out_specs=row, out_shape=jax.ShapeDtypeStruct((S, D), BF16), compiler_params=_params("parallel"),
    )(proj, proj, proj, *ys)


def merge_bwd(proj, ys, dm, D, name):
    S = proj.shape[0]
    tm = _tile(S, 256)
    row, gate = _gate_specs(S, D, tm)

    def body(g0, g1, g2, y0, y1, y2, dm_ref, d0, d1, d2, dg_ref):
        dmv = dm_ref[...]
        for b, (g_ref, y_ref, d_ref) in enumerate(((g0, y0, d0), (g1, y1, d1), (g2, y2, d2))):
            s = jax.nn.sigmoid(g_ref[...])
            d_ref[...] = (dmv * s).astype(BF16)
            dg_ref[:, b * D:(b + 1) * D] = ((dmv * y_ref[...]) * (s * (1.0 - s))).astype(BF16)

    act = jax.ShapeDtypeStruct((S, D), BF16)
    return pl.pallas_call(
        body, grid=(S // tm,), name=name, in_specs=[gate(0), gate(1), gate(2), row, row, row, row],
        out_specs=[row, row, row, pl.BlockSpec((tm, 3 * D), lambda i: (i, 0))],
        out_shape=[act, act, act, jax.ShapeDtypeStruct((S, 3 * D), BF16)], compiler_params=_params("parallel"),
    )(proj, proj, proj, *ys, dm)


def loss_head(y, target, name):
    S, D = y.shape
    tm = _tile(S, 512)

    def body(y_ref, t_ref, dy_ref, dy16_ref, l_ref):
        @pl.when(pl.program_id(0) == 0)
        def _():
            l_ref[...] = jnp.zeros_like(l_ref)

        e = y_ref[...] - t_ref[...]
        dy = e * (1.0 / D)
        dy_ref[...] = dy
        dy16_ref[...] = dy.astype(BF16)
        l_ref[...] += jnp.sum(e * e, axis=0, keepdims=True)

    row = pl.BlockSpec((tm, D), lambda i: (i, 0))
    vec = pl.BlockSpec((1, D), lambda i: (0, 0))
    return pl.pallas_call(
        body, grid=(S // tm,), name=name, in_specs=[row, row], out_specs=[row, row, vec],
        out_shape=[jax.ShapeDtypeStruct((S, D), F32), jax.ShapeDtypeStruct((S, D), BF16),
                   jax.ShapeDtypeStruct((1, D), F32)],
        compiler_params=_params("arbitrary"),
    )(y, target)


def _rows2d(a):
    return a.reshape(-1, a.shape[-1])


def _ew_call(fn, ins, out_dtypes, name):
    R, C = ins[0].shape
    tr = _tile(R, max(8, (1 << 19) // C))
    spec = pl.BlockSpec((tr, C), lambda i: (i, 0))

    def body(*refs):
        outs = fn(*[r[...] for r in refs[:len(ins)]])
        for o_ref, o in zip(refs[len(ins):], outs):
            o_ref[...] = o.astype(o_ref.dtype)

    return pl.pallas_call(
        body, grid=(R // tr,), name=name, in_specs=[spec] * len(ins), out_specs=[spec] * len(out_dtypes),
        out_shape=[jax.ShapeDtypeStruct((R, C), d) for d in out_dtypes], compiler_params=_params("parallel"),
    )(*ins)


def adamw(w, g, m, v, name):
    def fn(w, g, m, v):
        m = ADAM_B1 * m + (1.0 - ADAM_B1) * g
        v = ADAM_B2 * v + (1.0 - ADAM_B2) * (g * g)
        m_hat = m / (1.0 - ADAM_B1 ** ADAM_STEP)
        v_hat = v / (1.0 - ADAM_B2 ** ADAM_STEP)
        return -ADAM_LR * (m_hat / (jnp.sqrt(v_hat) + ADAM_EPS) + ADAM_WD * w), m, v

    shp = w.shape
    outs = _ew_call(fn, [_rows2d(a) for a in (w, g, m, v)], [F32, F32, F32], name)
    return [o.reshape(shp) for o in outs]


def _placed_call(fn, place, grid, ins, in_specs, out_shape, out_specs, name, dep=None):
    n = len(ins)
    deps = [] if dep is None else [dep]

    def body(place_ref, *refs):
        outs = fn(*[r[...] for r in refs[:n]])
        for o_ref, o in zip(refs[n + len(deps):], outs):
            o_ref[...] = o.astype(o_ref.dtype)

    return pl.pallas_call(
        body, name=name, out_shape=out_shape,
        grid_spec=pltpu.PrefetchScalarGridSpec(
            num_scalar_prefetch=1, grid=grid, in_specs=list(in_specs) + [ANY] * len(deps), out_specs=out_specs),
        compiler_params=_params(*["parallel"] * len(grid)),
    )(place, *ins, *deps)


def _row_tile(R, C):
    return _tile(R, max(16, (1 << 19) // C))


def cast_into_full(w, place, name, dep=None):
    R, C = w.shape
    tr = _row_tile(R, C)
    return _placed_call(
        lambda a: (a,), place, (R // tr,), [w], [pl.BlockSpec((tr, C), lambda i, p: (i, 0))],
        [jax.ShapeDtypeStruct((N_CHIP, R, C), BF16)], [pl.BlockSpec((None, tr, C), lambda i, p: (p[0], i, 0))],
        name, dep=dep)[0]


def pair_sum(g4, got, place, name):
    _, hr, C = got.shape
    tr = _row_tile(hr, C)
    nb = hr // tr
    blk = pl.BlockSpec((None, tr, C), lambda s, i, p: (s, i, 0))
    return _placed_call(
        lambda a, b: (a + b,), place, (N_CHIP, nb), [g4, got],
        [pl.BlockSpec((None, tr, C), lambda s, i, p: (s, p[1] * nb + i, 0)), blk],
        [jax.ShapeDtypeStruct(got.shape, BF16)], [blk], name)[0]


def chip_sum(p32, got, place, name, dep=None):
    _, H, C = p32.shape
    tr = _row_tile(H, C)
    nb = H // tr
    peer = lambda j: pl.BlockSpec((None, tr, C), lambda i, p, j=j: (j, i, 0))
    return _placed_call(
        lambda a, b, c, d: (((a.astype(F32) + b.astype(F32)) + c.astype(F32)) + d.astype(F32),), place, (nb,),
        [p32, got, got, got], [pl.BlockSpec((None, tr, C), lambda i, p: (p[0], i, 0)), peer(0), peer(1), peer(2)],
        [jax.ShapeDtypeStruct((2 * H, C), F32)], [pl.BlockSpec((tr, C), lambda i, p: (p[1] * nb + i, 0))],
        name, dep=dep)[0]


ANY = pl.BlockSpec(memory_space=pl.ANY)
CHIP_FLIPS = ((1, 0), (0, 1), (1, 1))


def _place():
    return lax.axis_index("x"), lax.axis_index("y"), lax.axis_index("c")


def _flip(v, f):
    return 1 - v if f else v


def join_halves(fulls, name):
    n = len(fulls)

    def body(*refs):
        outs = refs[n:2 * n]
        send_sem, recv_sem = refs[2 * n:]
        x, y, c = _place()
        copies = []
        for a in range(n):
            hr = outs[a].shape[0] // 2
            half = outs[a].at[pl.ds(c * hr, hr), :]
            cp = pltpu.make_async_remote_copy(
                src_ref=half, dst_ref=half, send_sem=send_sem.at[a], recv_sem=recv_sem.at[a],
                device_id=(x, y, 1 - c), device_id_type=MESH)
            cp.start()
            copies.append(cp)
        for a, cp in enumerate(copies):
            hr = outs[a].shape[0] // 2
            theirs = outs[a].at[pl.ds((1 - c) * hr, hr), :]
            cp.wait_send()
            pltpu.make_async_remote_copy(
                src_ref=theirs, dst_ref=theirs, send_sem=send_sem.at[a], recv_sem=recv_sem.at[a],
                device_id=(x, y, 1 - c), device_id_type=MESH).wait_recv()

    dma = pltpu.SemaphoreType.DMA
    return pl.pallas_call(
        body, name=name, in_specs=[ANY] * n, out_specs=[ANY] * n,
        out_shape=[jax.ShapeDtypeStruct(f.shape, F32) for f in fulls],
        input_output_aliases={a: a for a in range(n)},
        scratch_shapes=[dma((n,)), dma((n,))],
    )(*fulls)


HBM = pl.BlockSpec(memory_space=pltpu.HBM)
SEM = pl.BlockSpec(memory_space=pltpu.SEMAPHORE)
EFFECT = pltpu.SideEffectType.DATAFLOW_SIDE_EFFECTING


def _in_hbm(a):
    return pltpu.with_memory_space_constraint(a, pltpu.HBM)


def _gather_half(ref, chip_idx, core):
    hr = ref.shape[1] // 2
    return ref.at[chip_idx, pl.ds(core * hr, hr), :]


def gather_forward(fulls, small, name):
    n = len(fulls)

    def body(*refs):
        small_in = refs[n]
        outs, small_out = refs[n + 1:2 * n + 1], refs[2 * n + 1]
        send_sem, recv_sem, sm_send, sm_recv, loc_sem = refs[2 * n + 2:]
        x, y, c = _place()
        mine = 2 * x + y
        chips = [(_flip(x, fx), _flip(y, fy)) for fx, fy in CHIP_FLIPS]
        local = pltpu.make_async_copy(small_in, small_out.at[mine], loc_sem)
        local.start()
        copies = []
        for j, (px, py) in enumerate(chips):
            cp = pltpu.make_async_remote_copy(
                src_ref=small_in, dst_ref=small_out.at[mine], send_sem=sm_send.at[j], recv_sem=sm_recv.at[j],
                device_id=(px, py, c), device_id_type=MESH)
            cp.start()
            copies.append(cp)
        for a in range(n):
            for j, (px, py) in enumerate(chips):
                src = _gather_half(outs[a], 2 * px + py, c)
                cp = pltpu.make_async_remote_copy(
                    src_ref=src, dst_ref=src, send_sem=send_sem.at[3 * a + j], recv_sem=recv_sem.at[3 * a + j],
                    device_id=(x, y, 1 - c), device_id_type=MESH)
                cp.start()
                copies.append(cp)
        for a in range(n):
            for j, (px, py) in enumerate(chips):
                dst = _gather_half(outs[a], 2 * px + py, 1 - c)
                pltpu.make_async_remote_copy(
                    src_ref=dst, dst_ref=dst, send_sem=send_sem.at[3 * a + j], recv_sem=recv_sem.at[3 * a + j],
                    device_id=(x, y, 1 - c), device_id_type=MESH).wait_recv()
        for j, (px, py) in enumerate(chips):
            dst = small_out.at[2 * px + py]
            pltpu.make_async_remote_copy(
                src_ref=dst, dst_ref=dst, send_sem=sm_send.at[j], recv_sem=sm_recv.at[j],
                device_id=(px, py, c), device_id_type=MESH).wait_recv()
        for cp in copies:
            cp.wait_send()
        local.wait()

    dma = pltpu.SemaphoreType.DMA
    out = pl.pallas_call(
        body, name=name, in_specs=[ANY] * (n + 1), out_specs=[ANY] * (n + 1),
        out_shape=[jax.ShapeDtypeStruct(f.shape, f.dtype) for f in fulls]
        + [jax.ShapeDtypeStruct((N_CHIP,) + small.shape, small.dtype)],
        input_output_aliases={a: a for a in range(n)},
        scratch_shapes=[dma((3 * n,)), dma((3 * n,)), dma((3,)), dma((3,)), dma],
    )(*fulls, small)
    return out[:n], out[n]


def _gather_plan(fulls, lands):
    x, y, c = _place()
    mine = 2 * x + y
    return [(_gather_half(f, mine, c), _gather_half(f, mine, c), (_flip(x, fx), _flip(y, fy), c))
            for f in fulls for fx, fy in CHIP_FLIPS]


def _scatter_plan(parts, lands):
    x, y, c = _place()
    plan = []
    for p, l in zip(parts, lands):
        for j, (fx, fy) in enumerate(CHIP_FLIPS):
            px, py = _flip(x, fx), _flip(y, fy)
            plan.append((p.at[2 * px + py], l.at[j], (px, py, c)))
    return plan


def _scatter_lands(parts):
    return [(3,) + p.shape[1:] for p in parts]


def _exchange_plan(grads, lands):
    x, y, c = _place()
    plan = []
    for g, l in zip(grads, lands):
        hr = g.shape[1] // 2
        plan.append((g.at[:, pl.ds((1 - c) * hr, hr), :], l, (x, y, 1 - c)))
    return plan


def _exchange_lands(grads):
    return [(N_CHIP, g.shape[1] // 2, g.shape[2]) for g in grads]


def split_start(plan, copies, srcs, land_shapes, deps, name):
    n, m = len(srcs), len(land_shapes)
    lands = [lax.empty(s, srcs[0].dtype) for s in land_shapes]

    def body(*refs):
        k0 = n + m + len(deps)
        send_sem, recv_sem = refs[k0], refs[k0 + 1]
        thru, token = refs[k0 + 2:k0 + 2 + n + m], refs[k0 + 2 + n + m]
        for k, (src, dst, dev) in enumerate(plan(thru[:n], thru[n:])):
            pltpu.make_async_remote_copy(src_ref=src, dst_ref=dst, send_sem=send_sem.at[k], recv_sem=recv_sem.at[k],
                                         device_id=dev, device_id_type=MESH).start()
        token[...] = jnp.zeros_like(token)

    dma = pltpu.SemaphoreType.DMA
    arrays = list(srcs) + lands
    out = pl.pallas_call(
        body, name=name,
        out_shape=(dma((copies,)), dma((copies,)), *[pltpu.HBM(a.shape, a.dtype) for a in arrays],
                   jax.ShapeDtypeStruct((8, LANES), F32)),
        in_specs=[HBM] * (n + m) + [ANY] * len(deps),
        out_specs=(SEM, SEM, *[HBM] * (n + m), pl.BlockSpec(memory_space=pltpu.VMEM)),
        input_output_aliases={a: 2 + a for a in range(n + m)},
        compiler_params=pltpu.CompilerParams(has_side_effects=EFFECT),
    )(*[_in_hbm(a) for a in arrays], *deps)
    return (out[0], out[1], list(out[2:2 + n]), list(out[2 + n:2 + n + m])), out[2 + n + m]


def split_wait(plan, handle, after, name):
    send_sem, recv_sem, srcs, lands = handle
    n, m = len(srcs), len(lands)

    def body(*refs):
        send_sem, recv_sem = refs[n + m], refs[n + m + 1]
        thru = refs[n + m + 2 + len(after):]
        for k, (src, dst, dev) in enumerate(plan(thru[:n], thru[n:])):
            cp = pltpu.make_async_remote_copy(src_ref=src, dst_ref=dst, send_sem=send_sem.at[k],
                                              recv_sem=recv_sem.at[k], device_id=dev, device_id_type=MESH)
            cp.wait_send()
            cp.wait_recv()

    arrays = list(srcs) + list(lands)
    out = pl.pallas_call(
        body, name=name, out_shape=tuple(pltpu.HBM(a.shape, a.dtype) for a in arrays),
        in_specs=[HBM] * (n + m) + [SEM, SEM] + [ANY] * len(after), out_specs=tuple([HBM] * (n + m)),
        input_output_aliases={a: a for a in range(n + m)},
        compiler_params=pltpu.CompilerParams(has_side_effects=EFFECT),
    )(*arrays, send_sem, recv_sem, *after)
    return list(out[:n]), list(out[n:])


def allreduce_small(block, name, dep):
    R, C = block.shape

    def body(in_ref, dep_ref, out_ref, slots, send_sem, recv_sem):
        x, y, c = _place()
        me = 4 * x + 2 * y + c
        slots[me] = in_ref[...]
        copies = []
        for r in range(1, 8):
            fx, fy, fc = (r >> 2) & 1, (r >> 1) & 1, r & 1
            cp = pltpu.make_async_remote_copy(
                src_ref=in_ref, dst_ref=slots.at[me], send_sem=send_sem.at[r - 1], recv_sem=recv_sem.at[r - 1],
                device_id=(_flip(x, fx), _flip(y, fy), _flip(c, fc)), device_id_type=MESH)
            cp.start()
            copies.append(cp)
        for cp in copies:
            cp.wait()
        acc = slots[0]
        for d in range(1, 8):
            acc = acc + slots[d]
        out_ref[...] = acc

    vm = pl.BlockSpec(memory_space=pltpu.VMEM)
    dma = pltpu.SemaphoreType.DMA
    return pl.pallas_call(
        body, name=name, in_specs=[vm, ANY], out_specs=vm, out_shape=jax.ShapeDtypeStruct((R, C), F32),
        scratch_shapes=[pltpu.VMEM((8, R, C), F32), dma((7,)), dma((7,))],
    )(block, dep)


def local_step(x, mem, target, g_mix, g_mem, q_norm_g, k_norm_g, g_mlp, conv_w, h, mem_n, w_in, w_in_dep,
               rest_weights, early_grads, mid_grads, late_grads, last_grads):
    S, D = x.shape
    proj = mm_nn_shard(h, w_in, "proj", dep=w_in_dep)
    a_conv = conv_fwd(proj, conv_w, D, "conv_fwd")
    o_sb = sb_fwd(proj, D, "sb_fwd")
    w_conv_out, w_sb_out, w_mem_kv, w_x_out, w_out, w_up, w_down = rest_weights(o_sb)
    kv = mm_nn_shard(mem_n, w_mem_kv, "kv")
    o_x = xa_fwd(proj, kv, q_norm_g, k_norm_g, D, "xa_fwd")
    ys = [mm_nn(a_conv, w_conv_out, "y_conv"), mm_nn(o_sb, w_sb_out, "y_sb"), mm_nn(o_x, w_x_out, "y_x")]
    merged = merge_fwd(proj, ys, D, "merge_fwd")
    x1 = mm_nn(merged, w_out, "x1", res=x)
    h2 = rms_fwd(x1, g_mlp, "rms_mlp")
    up, act = mm_nn_shard(h2, w_up, "up", relu2=True)
    x2 = mm_nn(act, w_down, "x2", res=x1)
    dy, dy16, loss_cols = loss_head(x2, target, "loss_head")
    d_up = mm_nt(dy16, w_down, "d_up", up=up, out_dtype=BF16)
    g = {"w_down": mm_tn(act, dy16, "g_w_down")}
    g["w_up"] = mm_tn(h2, d_up, "g_w_up", shard_out=True)
    dh2 = mm_nt_shard(d_up, w_up, "dh2")
    dx1, dx1_16, g["g_mlp"] = rms_bwd(x1, g_mlp, dh2, "rms_mlp_bwd", dres=dy, want16=True)
    g["w_out"] = mm_tn(merged, dx1_16, "g_w_out")
    dm = mm_nt(dx1_16, w_out, "d_merged")
    dy_c, dy_s, dy_x, d_gate = merge_bwd(proj, ys, dm, D, "merge_bwd")
    g["w_conv_out"] = mm_tn(a_conv, dy_c, "g_w_conv_out")
    g["w_sb_out"] = mm_tn(o_sb, dy_s, "g_w_sb_out")
    g["w_x_out"] = mm_tn(o_x, dy_x, "g_w_x_out")
    d_xq, d_kv, g["q_norm_g"], g["k_norm_g"] = xa_bwd(
        proj, kv, q_norm_g, k_norm_g, mm_nt(dy_x, w_x_out, "d_o_x"), D, "xa_bwd")
    g["w_mem_kv"] = mm_tn(mem_n, d_kv, "g_w_mem_kv", shard_out=True)
    g["g_mem"] = rms_bwd(mem, g_mem, mm_nt_shard(d_kv, w_mem_kv, "d_mem_n"), "rms_mem_bwd", want_dx=False)
    dep = early_grads(g)
    d_a_conv = mm_nt(dy_c, w_conv_out, "d_a_conv", dep=dep)
    d_ch, d_cb, d_cc, g["conv_w"] = conv_bwd(proj, conv_w, d_a_conv, D, "conv_bwd")
    d_o_sb = mm_nt(dy_s, w_sb_out, "d_o_sb", dep=dep)
    dq, dk, dv = sb_bwd(proj, d_o_sb, D, "sb_bwd", dep=mid_grads([d_ch, d_o_sb]))
    d_proj = jnp.concatenate([d_ch, d_cb, d_cc, dq, dk, dv, d_xq, d_gate], axis=1)
    g["w_in"] = mm_tn(h, d_proj, "g_w_in", shard_out=True)
    dh = mm_nt_shard(d_proj, w_in, "dh", dep=late_grads(g["w_in"]))
    grad_x, g["g_mix"] = rms_bwd(x, g_mix, dh, "rms_mix_bwd", dres=dx1, dep=last_grads(dh))
    return loss_cols, grad_x, g


BIG = ("w_in", "w_conv_out", "w_sb_out", "w_mem_kv", "w_x_out", "w_out", "w_up", "w_down")
REST = BIG[1:]
COL_SHARDED = ("w_in", "w_mem_kv", "w_up")
WEIGHTS = ("g_mix", "g_mem", "w_in", "conv_w", "w_conv_out", "w_sb_out", "q_norm_g", "k_norm_g",
           "w_mem_kv", "w_x_out", "w_out", "g_mlp", "w_up", "w_down")


def _pack_small(D, g_mix, g_mem, g_mlp, q_norm_g, k_norm_g, conv_w, last):
    qk = jnp.concatenate([q_norm_g, k_norm_g, jnp.zeros((1, D - 2 * X_HEAD_DIM), F32)], axis=1)
    cw = jnp.pad(conv_w, ((0, 0), (0, D - conv_w.shape[1])))
    return jnp.concatenate([g_mix, g_mem, g_mlp, qk, cw, last], axis=0)


def kernel(x, mem, g_mix, g_mem, w_in, conv_w, w_conv_out, w_sb_out, q_norm_g, k_norm_g, w_mem_kv, w_x_out, w_out, g_mlp, w_up, w_down, loss_target, m_g_mix, m_g_mem, m_w_in, m_conv_w, m_w_conv_out, m_w_sb_out, m_q_norm_g, m_k_norm_g, m_w_mem_kv, m_w_x_out, m_w_out, m_g_mlp, m_w_up, m_w_down, v_g_mix, v_g_mem, v_w_in, v_conv_w, v_w_conv_out, v_w_sb_out, v_q_norm_g, v_k_norm_g, v_w_mem_kv, v_w_x_out, v_w_out, v_g_mlp, v_w_up, v_w_down):
    S, D = x.shape[1], x.shape[2]
    w = dict(g_mix=g_mix, g_mem=g_mem, w_in=w_in, conv_w=conv_w, w_conv_out=w_conv_out, w_sb_out=w_sb_out,
             q_norm_g=q_norm_g, k_norm_g=k_norm_g, w_mem_kv=w_mem_kv, w_x_out=w_x_out, w_out=w_out,
             g_mlp=g_mlp, w_up=w_up, w_down=w_down)
    m = dict(g_mix=m_g_mix, g_mem=m_g_mem, w_in=m_w_in, conv_w=m_conv_w, w_conv_out=m_w_conv_out,
             w_sb_out=m_w_sb_out, q_norm_g=m_q_norm_g, k_norm_g=m_k_norm_g, w_mem_kv=m_w_mem_kv,
             w_x_out=m_w_x_out, w_out=m_w_out, g_mlp=m_g_mlp, w_up=m_w_up, w_down=m_w_down)
    v = dict(g_mix=v_g_mix, g_mem=v_g_mem, w_in=v_w_in, conv_w=v_conv_w, w_conv_out=v_w_conv_out,
             w_sb_out=v_w_sb_out, q_norm_g=v_q_norm_g, k_norm_g=v_k_norm_g, w_mem_kv=v_w_mem_kv,
             w_x_out=v_w_x_out, w_out=v_w_out, g_mlp=v_g_mlp, w_up=v_w_up, w_down=v_w_down)
    chip = 2 * lax.axis_index("x") + lax.axis_index("y")
    cs = conv_w.shape[2]

    place = jnp.stack([chip, lax.axis_index("c")]).astype(jnp.int32)
    cw_block = jnp.pad(conv_w[0], ((0, 5), (0, 0)))
    handle, token = split_start(_gather_plan, 3, [cast_into_full(w["w_in"][0], place, "cast_w_in")], [], [],
                                "gather_w_in_start")
    rest16 = [cast_into_full(w[k][0], place, "cast_" + k, dep=token) for k in REST]
    h = rms_fwd(x[0], g_mix, "rms_mix", dep=token)
    mem_n = rms_fwd(mem[0], g_mem, "rms_mem", dep=token)
    landed, _ = split_wait(_gather_plan, handle, [*rest16, h, mem_n], "gather_w_in_wait")
    (w_in_full,), cw_all = gather_forward(landed, cw_block, "gather_w_in_forward")
    conv_full = jnp.concatenate([cw_all[p, :3] for p in range(N_CHIP)], axis=1)
    rest_handle, rest_token = split_start(_gather_plan, 3 * len(REST), rest16, [], [w_in_full], "gather_rest_start")

    def layout(k, a):
        return a if k in COL_SHARDED else a.reshape(-1, a.shape[-1])

    def rest_weights(after):
        landed, _ = split_wait(_gather_plan, rest_handle, [after], "gather_rest_wait")
        return [layout(k, a) for k, a in zip(REST, gather_forward(landed, cw_block, "gather_rest_forward")[0])]

    def blocks(k, a):
        return a if k in COL_SHARDED else a.reshape(N_CHIP, -1, a.shape[-1])

    early = {}

    def early_grads(g):
        early["g4"] = [blocks(k, g[k]) for k in REST]
        early["swap"], token = split_start(_exchange_plan, len(REST), early["g4"], _exchange_lands(early["g4"]),
                                           [g["g_mem"]], "exchange_rest_start")
        return token

    def mid_grads(after):
        g4, got = split_wait(_exchange_plan, early["swap"], after, "exchange_rest_wait")
        p16 = [pair_sum(a, b, place, "pair_sum_" + k) for k, a, b in zip(REST, g4, got)]
        early["fly"], token = split_start(_scatter_plan, 3 * len(REST), p16, _scatter_lands(p16), [],
                                          "scatter_rest_start")
        return token

    late = {}

    def late_grads(gw):
        late["swap"], token = split_start(_exchange_plan, 1, [gw], _exchange_lands([gw]), [], "exchange_w_in_start")
        return token

    def last_grads(after):
        (gw,), (got,) = split_wait(_exchange_plan, late["swap"], [after], "exchange_w_in_wait")
        p16 = [pair_sum(gw, got, place, "pair_sum_w_in")]
        late["fly"], token = split_start(_scatter_plan, 3, p16, _scatter_lands(p16), [], "scatter_w_in_start")
        return token

    loss_cols, grad_x, g = local_step(
        x[0], mem[0], loss_target[0], g_mix, g_mem, q_norm_g, k_norm_g, g_mlp, conv_full, h, mem_n,
        w_in_full, rest_token, rest_weights, early_grads, mid_grads, late_grads, last_grads)
    token = g["g_mix"]

    p16_rest, got_rest = split_wait(_scatter_plan, early["fly"], [token], "scatter_rest_wait")
    gsum, delta, new_m, new_v = {}, {}, {}, {}
    halves = [chip_sum(p, b, place, "chip_sum_" + k, dep=token) for k, p, b in zip(REST, p16_rest, got_rest)]
    for k, a in zip(REST, join_halves(halves, "join_halves_rest")):
        gsum[k] = a[None]
        delta[k], new_m[k], new_v[k] = adamw(w[k], gsum[k], m[k], v[k], "adamw_" + k)
    p16_in, got_in = split_wait(_scatter_plan, late["fly"], [new_v[k] for k in REST], "scatter_w_in_wait")

    small = allreduce_small(
        _pack_small(D, g["g_mix"], g["g_mem"], g["g_mlp"], g["q_norm_g"], g["k_norm_g"], g["conv_w"], loss_cols),
        "allreduce_small", dep=got_in[0])
    loss = (0.5 / D) * jnp.sum(small[7])
    gsum.update({"g_mix": small[0:1], "g_mem": small[1:2], "g_mlp": small[2:3],
                 "q_norm_g": small[3:4, :X_HEAD_DIM], "k_norm_g": small[3:4, X_HEAD_DIM:2 * X_HEAD_DIM],
                 "conv_w": lax.dynamic_slice(small[4:7], (0, chip * cs), (3, cs))[None]})
    half_in = chip_sum(p16_in[0], got_in[0], place, "chip_sum_w_in")
    gsum["w_in"] = join_halves([half_in], "join_halves_w_in")[0][None]
    delta["w_in"], new_m["w_in"], new_v["w_in"] = adamw(w["w_in"], gsum["w_in"], m["w_in"], v["w_in"], "adamw_w_in")
    small_names = ("g_mix", "g_mem", "g_mlp", "q_norm_g", "k_norm_g", "conv_w")
    zero_row = jnp.zeros((1, D), F32)
    packed = [_pack_small(D, *[t[k] if k != "conv_w" else t[k][0] for k in small_names], zero_row)
              for t in (w, gsum, m, v)]
    sm = adamw(*packed, "adamw_small")
    for t, block in zip((delta, new_m, new_v), sm):
        t["g_mix"], t["g_mem"], t["g_mlp"] = block[0:1], block[1:2], block[2:3]
        t["q_norm_g"], t["k_norm_g"] = block[3:4, :X_HEAD_DIM], block[3:4, X_HEAD_DIM:2 * X_HEAD_DIM]
        t["conv_w"] = block[4:7, :cs][None]

    return (loss, grad_x[None], *[gsum[k] for k in WEIGHTS], *[delta[k] for k in WEIGHTS],
            *[new_m[k] for k in WEIGHTS], *[new_v[k] for k in WEIGHTS])
```

```python
import functools

import jax
import jax.numpy as jnp
from jax import lax
from jax.experimental import pallas as pl
from jax.experimental.pallas import tpu as pltpu

F32 = jnp.float32
BF16 = jnp.bfloat16
EPS = 1e-6
N_CHIP = 4
SB_HEAD_DIM = 64
X_HEAD_DIM = 256
LANES = 128
VMEM_LIMIT = 56 * 1024 * 1024
ADAM_LR, ADAM_B1, ADAM_B2, ADAM_EPS, ADAM_WD, ADAM_STEP = 0.001, 0.9, 0.999, 1e-8, 0.01, 10
MESH = pl.DeviceIdType.MESH


def _params(*sem):
    return pltpu.CompilerParams(dimension_semantics=sem, vmem_limit_bytes=VMEM_LIMIT)


def _tile(n, pref):
    if n <= pref:
        return n
    t = 1 << (pref.bit_length() - 1)
    while n % t:
        t //= 2
    return t


NN = (((1,), (0,)), ((), ()))
NT = (((1,), (1,)), ((), ()))
TN = (((0,), (0,)), ((), ()))


def _dot(a, b, dims):
    return lax.dot_general(a.astype(BF16), b.astype(BF16), dims, preferred_element_type=F32)


def mm_nn_shard(a, g, name, relu2=False, dep=None, out_dtype=F32):
    M, K = a.shape
    _, _, Ns = g.shape
    tm, tn = _tile(M, 2048), _tile(Ns, 512)
    nb = Ns // tn

    def body(a_ref, b_ref, *o_refs):
        o_refs = o_refs[len(deps):]
        acc = _dot(a_ref[...], b_ref[...], NN)
        if relu2:
            acc = jnp.maximum(acc, 0.0)
            o_refs[1][...] = (acc * acc).astype(BF16)
        o_refs[0][...] = acc.astype(o_refs[0].dtype)

    o_spec = pl.BlockSpec((tm, tn), lambda i, j: (i, j))
    shapes = [jax.ShapeDtypeStruct((M, N_CHIP * Ns), BF16 if relu2 else out_dtype)]
    specs = [o_spec]
    if relu2:
        shapes.append(jax.ShapeDtypeStruct((M, N_CHIP * Ns), BF16))
        specs.append(o_spec)
    deps = [] if dep is None else [dep]
    out = pl.pallas_call(
        body, grid=(M // tm, N_CHIP * nb), name=name,
        in_specs=[pl.BlockSpec((tm, K), lambda i, j: (i, 0)),
                  pl.BlockSpec((None, K, tn), lambda i, j: (j // nb, 0, j % nb))] + [ANY] * len(deps),
        out_specs=specs, out_shape=shapes, compiler_params=_params("parallel", "parallel"),
    )(a, g, *deps)
    return out if relu2 else out[0]


def mm_nn(a, w, name, res=None, out_dtype=F32):
    M, K = a.shape
    N = w.shape[1]
    tm, tn = _tile(M, 2048 if K <= 2048 else 1024), _tile(N, 512)

    def body(a_ref, b_ref, *refs):
        acc = _dot(a_ref[...], b_ref[...], NN)
        if res is not None:
            acc = refs[0][...] + acc
        refs[-1][...] = acc.astype(out_dtype)

    o_spec = pl.BlockSpec((tm, tn), lambda i, j: (i, j))
    ins = [a, w] + ([res] if res is not None else [])
    return pl.pallas_call(
        body, grid=(M // tm, N // tn), name=name,
        in_specs=[pl.BlockSpec((tm, K), lambda i, j: (i, 0)), pl.BlockSpec((K, tn), lambda i, j: (0, j))]
        + ([o_spec] if res is not None else []),
        out_specs=o_spec, out_shape=jax.ShapeDtypeStruct((M, N), out_dtype),
        compiler_params=_params("parallel", "parallel"),
    )(*ins)


def mm_nt(a, w, name, up=None, out_dtype=F32, dep=None):
    M, N = a.shape
    R = w.shape[0]
    tm, tr = _tile(M, 2048), _tile(R, 512)

    def body(a_ref, b_ref, *refs):
        acc = _dot(a_ref[...], b_ref[...], NT)
        if up is not None:
            acc = acc * (2.0 * jnp.maximum(refs[0][...].astype(F32), 0.0))
        refs[-1][...] = acc.astype(out_dtype)

    o_spec = pl.BlockSpec((tm, tr), lambda i, j: (i, j))
    ins = [a, w] + ([up] if up is not None else []) + ([dep] if dep is not None else [])
    return pl.pallas_call(
        body, grid=(M // tm, R // tr), name=name,
        in_specs=[pl.BlockSpec((tm, N), lambda i, j: (i, 0)), pl.BlockSpec((tr, N), lambda i, j: (j, 0))]
        + ([o_spec] if up is not None else []) + ([ANY] if dep is not None else []),
        out_specs=o_spec, out_shape=jax.ShapeDtypeStruct((M, R), out_dtype),
        compiler_params=_params("parallel", "parallel"),
    )(*ins)


def mm_nt_shard(a, g, name, out_dtype=F32, dep=None):
    deps = [] if dep is None else [dep]
    M = a.shape[0]
    _, R, Ns = g.shape
    tm, tr, tk = _tile(M, 1024), _tile(R, 1024), _tile(Ns, 2560)
    nb = Ns // tk
    nk = N_CHIP * nb

    def body(a_ref, b_ref, *refs):
        o_ref, acc_ref = refs[len(deps):]
        k = pl.program_id(2)

        @pl.when(k == 0)
        def _():
            acc_ref[...] = jnp.zeros_like(acc_ref)

        acc_ref[...] += _dot(a_ref[...], b_ref[...], NT)

        @pl.when(k == nk - 1)
        def _():
            o_ref[...] = acc_ref[...].astype(out_dtype)

    return pl.pallas_call(
        body, grid=(M // tm, R // tr, nk), name=name,
        in_specs=[pl.BlockSpec((tm, tk), lambda i, j, k: (i, k)),
                  pl.BlockSpec((None, tr, tk), lambda i, j, k: (k // nb, j, k % nb))] + [ANY] * len(deps),
        out_specs=pl.BlockSpec((tm, tr), lambda i, j, k: (i, j)),
        out_shape=jax.ShapeDtypeStruct((M, R), out_dtype),
        scratch_shapes=[pltpu.VMEM((tm, tr), F32)],
        compiler_params=_params("parallel", "parallel", "arbitrary"),
    )(a, g, *deps)


def mm_tn(a, b, name, shard_out=False):
    S, M = a.shape
    N = b.shape[1]
    Ns = N // N_CHIP if shard_out else N
    tm, tn = _tile(M, 1024), _tile(Ns, 512)
    nb = Ns // tn

    def body(a_ref, b_ref, o_ref):
        o_ref[...] = _dot(a_ref[...], b_ref[...], TN)

    if shard_out:
        o_spec = pl.BlockSpec((None, tm, tn), lambda i, j: (j // nb, i, j % nb))
        o_shape = jax.ShapeDtypeStruct((N_CHIP, M, Ns), F32)
    else:
        o_spec = pl.BlockSpec((tm, tn), lambda i, j: (i, j))
        o_shape = jax.ShapeDtypeStruct((M, N), F32)
    return pl.pallas_call(
        body, grid=(M // tm, N // tn), name=name,
        in_specs=[pl.BlockSpec((S, tm), lambda i, j: (0, i)), pl.BlockSpec((S, tn), lambda i, j: (0, j))],
        out_specs=o_spec, out_shape=o_shape, compiler_params=_params("parallel", "parallel"),
    )(a, b)


def rms_fwd(x, g, name, dep=None):
    S, D = x.shape
    tm = _tile(S, 512)
    deps = [] if dep is None else [dep]

    def body(x_ref, g_ref, *refs):
        xv = x_ref[...]
        r = lax.rsqrt(jnp.mean(xv * xv, axis=-1, keepdims=True) + EPS)
        refs[-1][...] = ((xv * r) * g_ref[...]).astype(BF16)

    return pl.pallas_call(
        body, grid=(S // tm,), name=name,
        in_specs=[pl.BlockSpec((tm, D), lambda i: (i, 0)), pl.BlockSpec((1, D), lambda i: (0, 0))]
        + [ANY] * len(deps),
        out_specs=pl.BlockSpec((tm, D), lambda i: (i, 0)),
        out_shape=jax.ShapeDtypeStruct((S, D), BF16), compiler_params=_params("parallel"),
    )(x, g, *deps)


def rms_bwd(x, g, dh, name, dres=None, want_dx=True, want16=False, dep=None):
    S, D = x.shape
    tm = _tile(S, 512)

    def body(x_ref, g_ref, dh_ref, *refs):
        i = pl.program_id(0)
        xv = x_ref[...]
        r = lax.rsqrt(jnp.mean(xv * xv, axis=-1, keepdims=True) + EPS)
        xn = xv * r
        dhv = dh_ref[...].astype(F32)
        gg_ref = refs[-1]

        @pl.when(i == 0)
        def _():
            gg_ref[...] = jnp.zeros_like(gg_ref)

        gg_ref[...] += jnp.sum(dhv * xn, axis=0, keepdims=True)
        if want_dx:
            dxn = dhv * g_ref[...]
            dx = r * (dxn - xn * jnp.mean(dxn * xn, axis=-1, keepdims=True))
            if dres is not None:
                dx = refs[0][...] + dx
            refs[-2][...] = dx.astype(refs[-2].dtype)
            if want16:
                refs[-3][...] = dx

    row = pl.BlockSpec((tm, D), lambda i: (i, 0))
    vec = pl.BlockSpec((1, D), lambda i: (0, 0))
    ins, in_specs = [x, g, dh], [row, vec, row]
    if dres is not None:
        ins.append(dres)
        in_specs.append(row)
    if dep is not None:
        ins.append(dep)
        in_specs.append(ANY)
    shapes, specs = [jax.ShapeDtypeStruct((1, D), F32)], [vec]
    if want_dx:
        if want16:
            shapes.insert(0, jax.ShapeDtypeStruct((S, D), BF16))
            specs.insert(0, row)
        shapes.insert(0, jax.ShapeDtypeStruct((S, D), F32))
        specs.insert(0, row)
    out = pl.pallas_call(body, grid=(S // tm,), name=name, in_specs=in_specs, out_specs=specs,
                         out_shape=shapes, compiler_params=_params("arbitrary"))(*ins)
    return out if want_dx else out[0]


def _shift_down(u, k, row):
    return jnp.where(row >= k, pltpu.roll(u, k, axis=0), 0.0)


def _shift_up(u, k, row):
    S = u.shape[0]
    return jnp.where(row < S - k, pltpu.roll(u, S - k, axis=0), 0.0)


def _conv_specs(S, D, tc):
    nb = D // tc
    col = lambda o: pl.BlockSpec((S, tc), lambda j, o=o: (0, o * nb + j))
    return col, pl.BlockSpec((3, tc), lambda j: (0, j))


def conv_fwd(proj, conv_w, D, name):
    S = proj.shape[0]
    tc = _tile(D, 256)
    col, wspec = _conv_specs(S, D, tc)

    def body(ch_ref, cb_ref, cc_ref, w_ref, a_ref):
        row = lax.broadcasted_iota(jnp.int32, (S, tc), 0)
        u = cc_ref[...].astype(F32) * ch_ref[...].astype(F32)
        w = w_ref[...]
        cv = w[0:1, :] * _shift_down(u, 2, row) + w[1:2, :] * _shift_down(u, 1, row) + w[2:3, :] * u
        a_ref[...] = (cb_ref[...].astype(F32) * cv).astype(BF16)

    return pl.pallas_call(
        body, grid=(D // tc,), name=name, in_specs=[col(0), col(1), col(2), wspec],
        out_specs=pl.BlockSpec((S, tc), lambda j: (0, j)),
        out_shape=jax.ShapeDtypeStruct((S, D), BF16), compiler_params=_params("parallel"),
    )(proj, proj, proj, conv_w)


def conv_bwd(proj, conv_w, da, D, name):
    S = proj.shape[0]
    tc = _tile(D, 256)
    col, wspec = _conv_specs(S, D, tc)
    blk = pl.BlockSpec((S, tc), lambda j: (0, j))

    def body(ch_ref, cb_ref, cc_ref, w_ref, da_ref, dch_ref, dcb_ref, dcc_ref, gw_ref):
        row = lax.broadcasted_iota(jnp.int32, (S, tc), 0)
        ch, cb, cc, dav = [r[...].astype(F32) for r in (ch_ref, cb_ref, cc_ref, da_ref)]
        w = w_ref[...]
        u = cc * ch
        u1, u2 = _shift_down(u, 1, row), _shift_down(u, 2, row)
        cv = w[0:1, :] * u2 + w[1:2, :] * u1 + w[2:3, :] * u
        dcb_ref[...] = (dav * cv).astype(BF16)
        dcv = dav * cb
        gw_ref[0:1, :] = jnp.sum(dcv * u2, axis=0, keepdims=True)
        gw_ref[1:2, :] = jnp.sum(dcv * u1, axis=0, keepdims=True)
        gw_ref[2:3, :] = jnp.sum(dcv * u, axis=0, keepdims=True)
        du = w[2:3, :] * dcv + w[1:2, :] * _shift_up(dcv, 1, row) + w[0:1, :] * _shift_up(dcv, 2, row)
        dcc_ref[...] = (du * ch).astype(BF16)
        dch_ref[...] = (du * cc).astype(BF16)

    act = jax.ShapeDtypeStruct((S, D), BF16)
    return pl.pallas_call(
        body, grid=(D // tc,), name=name, in_specs=[col(0), col(1), col(2), wspec, blk],
        out_specs=[blk, blk, blk, wspec], out_shape=[act, act, act, jax.ShapeDtypeStruct((3, D), F32)],
        compiler_params=_params("parallel"),
    )(proj, proj, proj, conv_w, da)


SB_BQ_FWD = 512
SB_BQ_BWD = 256
SB_BK = 128
SB_GROUP = 4


def _sb_consts(bq):
    lane = lax.broadcasted_iota(jnp.int32, (bq, LANES), 1)
    r = lax.broadcasted_iota(jnp.int32, (SB_BK, SB_BK), 0)
    c = lax.broadcasted_iota(jnp.int32, (SB_BK, SB_BK), 1)
    tri_rev = jnp.where(r > c, 1.0, 0.0).astype(BF16)
    tri_fwd = jnp.where(r < c, 1.0, 0.0).astype(BF16)
    return lane, tri_rev, tri_fwd


def _cumsum2(v, tri):
    hi = v.astype(BF16)
    lo = (v - hi.astype(F32)).astype(BF16)
    part = (lax.dot_general(hi, tri, NN, preferred_element_type=F32)
            + lax.dot_general(lo, tri, NN, preferred_element_type=F32))
    return part, jnp.sum(v, axis=1, keepdims=True)


def _sb_logits(z, past):
    sp = jnp.log(1.0 + jnp.exp(-jnp.abs(z)))
    l = jnp.minimum(z, 0.0) - sp
    m = l - z
    if past is not None:
        m = jnp.where(past, m, 0.0)
    return l, m


def _stack_heads(v, lane):
    return jnp.concatenate([jnp.where(lane < SB_HEAD_DIM, v, 0.0), jnp.where(lane >= SB_HEAD_DIM, v, 0.0)],
                           axis=0).astype(BF16)


def _unstack_heads(v, lane):
    bq = v.shape[0] // 2
    return jnp.where(lane < SB_HEAD_DIM, v[:bq], v[bq:])


def _sb_positions(i, bq):
    r = lax.broadcasted_iota(jnp.int32, (2 * bq, SB_BK), 0)
    trow = i * bq + jnp.where(r >= bq, r - bq, r)
    return trow, lax.broadcasted_iota(jnp.int32, (2 * bq, SB_BK), 1)


def _sb_specs(S, D, bq):
    npair = D // LANES
    qspec = pl.BlockSpec((bq, LANES), lambda p, i: (i, 3 * npair + p))
    kspec = pl.BlockSpec((S, LANES), lambda p, i: (0, 4 * npair + p))
    vspec = pl.BlockSpec((S, LANES), lambda p, i: (0, 5 * npair + p))
    return npair, qspec, kspec, vspec


def sb_fwd(proj, D, name):
    S = proj.shape[0]
    bq = min(SB_BQ_FWD, S)
    nd = bq // SB_BK
    npair, qspec, kspec, vspec = _sb_specs(S, D, bq)
    scale = SB_HEAD_DIM ** -0.5

    def body(q_ref, k_ref, v_ref, o_ref, kb_ref, vb_ref):
        i = pl.program_id(1)

        @pl.when(i == 0)
        def _():
            kb_ref[...] = k_ref[...].astype(BF16)
            vb_ref[...] = v_ref[...].astype(BF16)

        lane, tri_rev, _ = _sb_consts(bq)
        qs = _stack_heads(q_ref[...].astype(F32) * scale, lane)
        trow, scol = _sb_positions(i, bq)

        def steps(j0, carry, n, masked):
            ks = [pl.multiple_of((j0 - t) * SB_BK, SB_BK) for t in range(n)]
            past = [(k + scol) < trow if masked else None for k in ks]
            zs = [lax.dot_general(qs, kb_ref[pl.ds(k, SB_BK), :], NT, preferred_element_type=F32) for k in ks]
            lm = [_sb_logits(z, p) for z, p in zip(zs, past)]
            cs = [_cumsum2(m, tri_rev) for _, m in lm]
            c, acc = carry
            for t in range(n):
                a = jnp.exp(lm[t][0] + (cs[t][0] + c))
                if masked:
                    a = jnp.where(past[t], a, 0.0)
                acc = acc + lax.dot_general(a.astype(BF16), vb_ref[pl.ds(ks[t], SB_BK), :], NN,
                                            preferred_element_type=F32)
                c = c + cs[t][1]
            return c, acc

        carry = (jnp.zeros((2 * bq, 1), F32), jnp.zeros((2 * bq, LANES), F32))
        carry = steps(i * nd + nd - 1, carry, nd, True)
        older = i * nd
        groups = older // SB_GROUP
        carry = lax.fori_loop(
            0, groups, lambda t, cr: steps(older - 1 - t * SB_GROUP, cr, SB_GROUP, False), carry)
        rest = older - groups * SB_GROUP
        carry = lax.fori_loop(0, rest // nd, lambda t, cr: steps(rest - 1 - t * nd, cr, nd, False), carry)
        o_ref[...] = _unstack_heads(carry[1], lane)

    return pl.pallas_call(
        body, grid=(npair, S // bq), name=name, in_specs=[qspec, kspec, vspec],
        out_specs=pl.BlockSpec((bq, LANES), lambda p, i: (i, p)),
        out_shape=jax.ShapeDtypeStruct((S, D), F32),
        scratch_shapes=[pltpu.VMEM((S, LANES), BF16), pltpu.VMEM((S, LANES), BF16)],
        compiler_params=_params("parallel", "arbitrary"),
    )(proj, proj, proj)


def sb_bwd(proj, do, D, name, dep=None):
    S = proj.shape[0]
    bq = min(SB_BQ_BWD, S)
    nd = bq // SB_BK
    nkb = S // SB_BK
    npair, qspec, kspec, vspec = _sb_specs(S, D, bq)
    scale = SB_HEAD_DIM ** -0.5

    def body(q_ref, k_ref, v_ref, do_ref, *refs):
        dq_ref, dk_ref, dv_ref, kb_ref, vb_ref, dk_acc, dv_acc, g_scr, b_scr, a_scr = refs[len(deps):]
        i = pl.program_id(1)

        @pl.when(i == 0)
        def _():
            kb_ref[...] = k_ref[...].astype(BF16)
            vb_ref[...] = v_ref[...].astype(BF16)
            dk_acc[...] = jnp.zeros_like(dk_acc)
            dv_acc[...] = jnp.zeros_like(dv_acc)

        lane, tri_rev, tri_fwd = _sb_consts(bq)
        qs = _stack_heads(q_ref[...].astype(F32) * scale, lane)
        dos = _stack_heads(do_ref[...].astype(F32), lane)
        qs_t, dos_t = qs.T, dos.T
        trow, scol = _sb_positions(i, bq)

        def sweep1(j0, c, n, masked):
            js = [j0 - t for t in range(n)]
            ks = [pl.multiple_of(j * SB_BK, SB_BK) for j in js]
            past = [(k + scol) < trow if masked else None for k in ks]
            zs = [lax.dot_general(qs, kb_ref[pl.ds(k, SB_BK), :], NT, preferred_element_type=F32) for k in ks]
            das = [lax.dot_general(dos, vb_ref[pl.ds(k, SB_BK), :], NT, preferred_element_type=F32) for k in ks]
            lm = [_sb_logits(z, p) for z, p in zip(zs, past)]
            cs = [_cumsum2(m, tri_rev) for _, m in lm]
            for t in range(n):
                b_scr[js[t]] = jnp.exp(lm[t][0]).astype(BF16)
            for t in range(n):
                a = jnp.exp(lm[t][0] + (cs[t][0] + c))
                if masked:
                    a = jnp.where(past[t], a, 0.0)
                g_scr[js[t]] = (das[t] * a).astype(BF16)
                a_scr[js[t]] = a.astype(BF16)
                c = c + cs[t][1]
            return c

        older = i * nd
        groups = older // SB_GROUP
        rest = older - groups * SB_GROUP
        c = jnp.zeros((2 * bq, 1), F32)
        c = sweep1(i * nd + nd - 1, c, nd, True)
        c = lax.fori_loop(0, groups, lambda t, cr: sweep1(older - 1 - t * SB_GROUP, cr, SB_GROUP, False), c)
        lax.fori_loop(0, rest // nd, lambda t, cr: sweep1(rest - 1 - t * nd, cr, nd, False), c)

        def sweep2(j0, carry, n, masked):
            js = [j0 + t for t in range(n)]
            ks = [pl.multiple_of(j * SB_BK, SB_BK) for j in js]
            g16 = [g_scr[j] for j in js]
            gv = [g.astype(F32) for g in g16]
            gs = [(lax.dot_general(g, tri_fwd, NN, preferred_element_type=F32), jnp.sum(v, axis=1, keepdims=True))
                  for g, v in zip(g16, gv)]
            pc, dq = carry
            dzs = []
            for t in range(n):
                dz = gv[t] - b_scr[js[t]].astype(F32) * (gv[t] + (gs[t][0] + pc))
                if masked:
                    dz = jnp.where((ks[t] + scol) < trow, dz, 0.0)
                dzs.append(dz.astype(BF16))
                pc = pc + gs[t][1]
            for t in range(n):
                dq = dq + lax.dot_general(dzs[t], kb_ref[pl.ds(ks[t], SB_BK), :], NN, preferred_element_type=F32)
                dk_acc[js[t]] += lax.dot_general(qs_t, dzs[t], NN, preferred_element_type=F32)
                dv_acc[js[t]] += lax.dot_general(dos_t, a_scr[js[t]], NN, preferred_element_type=F32)
            return pc, dq

        carry = (jnp.zeros((2 * bq, 1), F32), jnp.zeros((2 * bq, LANES), F32))
        carry = lax.fori_loop(0, groups, lambda t, cr: sweep2(t * SB_GROUP, cr, SB_GROUP, False), carry)
        carry = lax.fori_loop(
            0, rest // nd, lambda t, cr: sweep2(groups * SB_GROUP + t * nd, cr, nd, False), carry)
        carry = sweep2(i * nd, carry, nd, True)
        dq_ref[...] = (_unstack_heads(carry[1], lane) * scale).astype(BF16)

        @pl.when(i == pl.num_programs(1) - 1)
        def _():
            for j in range(nkb):
                dk_ref[j * SB_BK:(j + 1) * SB_BK, :] = dk_acc[j].T.astype(BF16)
                dv_ref[j * SB_BK:(j + 1) * SB_BK, :] = dv_acc[j].T.astype(BF16)

    deps = [] if dep is None else [dep]
    full = pl.BlockSpec((S, LANES), lambda p, i: (0, p))
    blk = pl.BlockSpec((bq, LANES), lambda p, i: (i, p))
    act = jax.ShapeDtypeStruct((S, D), BF16)
    return pl.pallas_call(
        body, grid=(npair, S // bq), name=name, in_specs=[qspec, kspec, vspec, blk] + [ANY] * len(deps),
        out_specs=[blk, full, full], out_shape=[act, act, act],
        scratch_shapes=[pltpu.VMEM((S, LANES), BF16), pltpu.VMEM((S, LANES), BF16),
                        pltpu.VMEM((nkb, LANES, SB_BK), F32), pltpu.VMEM((nkb, LANES, SB_BK), F32),
                        pltpu.VMEM((nkb, 2 * bq, SB_BK), BF16), pltpu.VMEM((nkb, 2 * bq, SB_BK), BF16),
                        pltpu.VMEM((nkb, 2 * bq, SB_BK), BF16)],
        compiler_params=_params("parallel", "arbitrary"),
    )(proj, proj, proj, do, *deps)


def _rms_rows(v):
    r = lax.rsqrt(jnp.mean(v * v, axis=-1, keepdims=True) + EPS)
    return v * r, r


def _xa_specs(S, D, M, tq):
    nh = D // X_HEAD_DIM
    qspec = pl.BlockSpec((tq, X_HEAD_DIM), lambda h, i: (i, 6 * nh + h))
    kspec = pl.BlockSpec((M, X_HEAD_DIM), lambda h, i: (0, h))
    vspec = pl.BlockSpec((M, X_HEAD_DIM), lambda h, i: (0, nh + h))
    gspec = pl.BlockSpec((1, X_HEAD_DIM), lambda h, i: (0, 0))
    return nh, qspec, kspec, vspec, gspec


def xa_fwd(proj, kv, gq, gk, D, name):
    S, M = proj.shape[0], kv.shape[0]
    tq = _tile(S, 512)
    nh, qspec, kspec, vspec, gspec = _xa_specs(S, D, M, tq)
    scale = X_HEAD_DIM ** -0.5

    def body(q_ref, k_ref, v_ref, gq_ref, gk_ref, o_ref):
        qn = _rms_rows(q_ref[...].astype(F32))[0] * gq_ref[...]
        kn = _rms_rows(k_ref[...])[0] * gk_ref[...]
        s = _dot(qn, kn, NT) * scale
        e = jnp.exp(s - jnp.max(s, axis=-1, keepdims=True))
        p = e / jnp.sum(e, axis=-1, keepdims=True)
        o_ref[...] = _dot(p, v_ref[...], NN)

    return pl.pallas_call(
        body, grid=(nh, S // tq), name=name, in_specs=[qspec, kspec, vspec, gspec, gspec],
        out_specs=pl.BlockSpec((tq, X_HEAD_DIM), lambda h, i: (i, h)),
        out_shape=jax.ShapeDtypeStruct((S, D), F32), compiler_params=_params("parallel", "parallel"),
    )(proj, kv, kv, gq, gk)


def xa_bwd(proj, kv, gq, gk, do, D, name):
    S, M = proj.shape[0], kv.shape[0]
    tq = _tile(S, 512)
    nh, qspec, kspec, vspec, gspec = _xa_specs(S, D, M, tq)
    scale = X_HEAD_DIM ** -0.5

    def body(q_ref, k_ref, v_ref, gq_ref, gk_ref, do_ref, dq_ref, dk_ref, dv_ref, ggq_ref, ggk_ref,
             dkn_acc, dv_acc):
        h, i = pl.program_id(0), pl.program_id(1)

        @pl.when((h == 0) & (i == 0))
        def _():
            ggq_ref[...] = jnp.zeros_like(ggq_ref)
            ggk_ref[...] = jnp.zeros_like(ggk_ref)

        @pl.when(i == 0)
        def _():
            dkn_acc[...] = jnp.zeros_like(dkn_acc)
            dv_acc[...] = jnp.zeros_like(dv_acc)

        gq, gk = gq_ref[...], gk_ref[...]
        qhat, rq = _rms_rows(q_ref[...].astype(F32))
        khat, rk = _rms_rows(k_ref[...])
        qn, kn = qhat * gq, khat * gk
        s = _dot(qn, kn, NT) * scale
        e = jnp.exp(s - jnp.max(s, axis=-1, keepdims=True))
        p = e / jnp.sum(e, axis=-1, keepdims=True)
        dov = do_ref[...]
        dv_acc[...] += _dot(p, dov, TN)
        dp = _dot(dov, v_ref[...], NT)
        ds = (p * (dp - jnp.sum(dp * p, axis=-1, keepdims=True))) * scale
        dqn = _dot(ds, kn, NN)
        dkn_acc[...] += _dot(ds, qn, TN)
        ggq_ref[...] += jnp.sum(dqn * qhat, axis=0, keepdims=True)
        dqh = dqn * gq
        dq_ref[...] = (rq * (dqh - qhat * jnp.mean(dqh * qhat, axis=-1, keepdims=True))).astype(BF16)

        @pl.when(i == pl.num_programs(1) - 1)
        def _():
            dkn = dkn_acc[...]
            ggk_ref[...] += jnp.sum(dkn * khat, axis=0, keepdims=True)
            dkh = dkn * gk
            dk_ref[...] = (rk * (dkh - khat * jnp.mean(dkh * khat, axis=-1, keepdims=True))).astype(BF16)
            dv_ref[...] = dv_acc[...].astype(BF16)

    blk = pl.BlockSpec((tq, X_HEAD_DIM), lambda h, i: (i, h))
    kv_shape = jax.ShapeDtypeStruct((M, 2 * D), BF16)
    gshape = jax.ShapeDtypeStruct((1, X_HEAD_DIM), F32)
    dq, dk, dv, ggq, ggk = pl.pallas_call(
        body, grid=(nh, S // tq), name=name, in_specs=[qspec, kspec, vspec, gspec, gspec, blk],
        out_specs=[blk, kspec, vspec, gspec, gspec],
        out_shape=[jax.ShapeDtypeStruct((S, D), BF16), kv_shape, kv_shape, gshape, gshape],
        scratch_shapes=[pltpu.VMEM((M, X_HEAD_DIM), F32), pltpu.VMEM((M, X_HEAD_DIM), F32)],
        compiler_params=_params("arbitrary", "arbitrary"),
    )(proj, kv, kv, gq, gk, do)
    d_kv = jnp.concatenate([dk[:, :D], dv[:, D:]], axis=1)
    return dq, d_kv, ggq, ggk


def _gate_specs(S, D, tm):
    row = pl.BlockSpec((tm, D), lambda i: (i, 0))
    gate = lambda b: pl.BlockSpec((tm, D), lambda i, b=b: (i, 7 + b))
    return row, gate


def merge_fwd(proj, ys, D, name):
    S = proj.shape[0]
    tm = _tile(S, 256)
    row, gate = _gate_specs(S, D, tm)

    def body(g0, g1, g2, y0, y1, y2, o_ref):
        acc = jax.nn.sigmoid(g0[...].astype(F32)) * y0[...].astype(F32)
        acc = acc + jax.nn.sigmoid(g1[...].astype(F32)) * y1[...].astype(F32)
        acc = acc + jax.nn.sigmoid(g2[...].astype(F32)) * y2[...].astype(F32)
        o_ref[...] = acc.astype(BF16)

    return pl.pallas_call(
        body, grid=(S // tm,), name=name, in_specs=[gate(0), gate(1), gate(2), row, row, row],
        out_specs=row, out_shape=jax.ShapeDtypeStruct((S, D), BF16), compiler_params=_params("parallel"),
    )(proj, proj, proj, *ys)


def merge_bwd(proj, ys, dm, D, name):
    S = proj.shape[0]
    tm = _tile(S, 256)
    row, gate = _gate_specs(S, D, tm)

    def body(g0, g1, g2, y0, y1, y2, dm_ref, d0, d1, d2, dg_ref):
        dmv = dm_ref[...]
        for b, (g_ref, y_ref, d_ref) in enumerate(((g0, y0, d0), (g1, y1, d1), (g2, y2, d2))):
            s = jax.nn.sigmoid(g_ref[...].astype(F32))
            d_ref[...] = (dmv * s).astype(BF16)
            dg_ref[:, b * D:(b + 1) * D] = ((dmv * y_ref[...].astype(F32)) * (s * (1.0 - s))).astype(BF16)

    act = jax.ShapeDtypeStruct((S, D), BF16)
    return pl.pallas_call(
        body, grid=(S // tm,), name=name, in_specs=[gate(0), gate(1), gate(2), row, row, row, row],
        out_specs=[row, row, row, pl.BlockSpec((tm, 3 * D), lambda i: (i, 0))],
        out_shape=[act, act, act, jax.ShapeDtypeStruct((S, 3 * D), BF16)], compiler_params=_params("parallel"),
    )(proj, proj, proj, *ys, dm)


def loss_head(y, target, name):
    S, D = y.shape
    tm = _tile(S, 512)

    def body(y_ref, t_ref, dy_ref, dy16_ref, l_ref):
        @pl.when(pl.program_id(0) == 0)
        def _():
            l_ref[...] = jnp.zeros_like(l_ref)

        e = y_ref[...] - t_ref[...]
        dy = e * (1.0 / D)
        dy_ref[...] = dy
        dy16_ref[...] = dy.astype(BF16)
        l_ref[...] += jnp.sum(e * e, axis=0, keepdims=True)

    row = pl.BlockSpec((tm, D), lambda i: (i, 0))
    vec = pl.BlockSpec((1, D), lambda i: (0, 0))
    return pl.pallas_call(
        body, grid=(S // tm,), name=name, in_specs=[row, row], out_specs=[row, row, vec],
        out_shape=[jax.ShapeDtypeStruct((S, D), F32), jax.ShapeDtypeStruct((S, D), BF16),
                   jax.ShapeDtypeStruct((1, D), F32)],
        compiler_params=_params("arbitrary"),
    )(y, target)


def _rows2d(a):
    return a.reshape(-1, a.shape[-1])


def _ew_call(fn, ins, out_dtypes, name):
    R, C = ins[0].shape
    tr = _tile(R, max(8, (1 << 19) // C))
    spec = pl.BlockSpec((tr, C), lambda i: (i, 0))

    def body(*refs):
        outs = fn(*[r[...] for r in refs[:len(ins)]])
        for o_ref, o in zip(refs[len(ins):], outs):
            o_ref[...] = o.astype(o_ref.dtype)

    return pl.pallas_call(
        body, grid=(R // tr,), name=name, in_specs=[spec] * len(ins), out_specs=[spec] * len(out_dtypes),
        out_shape=[jax.ShapeDtypeStruct((R, C), d) for d in out_dtypes], compiler_params=_params("parallel"),
    )(*ins)


def adamw(w, g, m, v, name):
    def fn(w, g, m, v):
        m = ADAM_B1 * m + (1.0 - ADAM_B1) * g
        v = ADAM_B2 * v + (1.0 - ADAM_B2) * (g * g)
        m_hat = m / (1.0 - ADAM_B1 ** ADAM_STEP)
        v_hat = v / (1.0 - ADAM_B2 ** ADAM_STEP)
        return -ADAM_LR * (m_hat / (jnp.sqrt(v_hat) + ADAM_EPS) + ADAM_WD * w), m, v

    shp = w.shape
    outs = _ew_call(fn, [_rows2d(a) for a in (w, g, m, v)], [F32, F32, F32], name)
    return [o.reshape(shp) for o in outs]


def _placed_call(fn, place, grid, ins, in_specs, out_shape, out_specs, name, dep=None):
    n = len(ins)
    deps = [] if dep is None else [dep]

    def body(place_ref, *refs):
        outs = fn(*[r[...] for r in refs[:n]])
        for o_ref, o in zip(refs[n + len(deps):], outs):
            o_ref[...] = o.astype(o_ref.dtype)

    return pl.pallas_call(
        body, name=name, out_shape=out_shape,
        grid_spec=pltpu.PrefetchScalarGridSpec(
            num_scalar_prefetch=1, grid=grid, in_specs=list(in_specs) + [ANY] * len(deps), out_specs=out_specs),
        compiler_params=_params(*["parallel"] * len(grid)),
    )(place, *ins, *deps)


def _row_tile(R, C):
    return _tile(R, max(16, (1 << 19) // C))


def cast_into_full(w, place, name, dep=None):
    R, C = w.shape
    tr = _row_tile(R, C)
    return _placed_call(
        lambda a: (a,), place, (R // tr,), [w], [pl.BlockSpec((tr, C), lambda i, p: (i, 0))],
        [jax.ShapeDtypeStruct((N_CHIP, R, C), BF16)], [pl.BlockSpec((None, tr, C), lambda i, p: (p[0], i, 0))],
        name, dep=dep)[0]


def pair_sum(g4, got, place, name):
    _, hr, C = got.shape
    tr = _row_tile(hr, C)
    nb = hr // tr
    blk = pl.BlockSpec((None, tr, C), lambda s, i, p: (s, i, 0))
    return _placed_call(
        lambda a, b: (a + b,), place, (N_CHIP, nb), [g4, got],
        [pl.BlockSpec((None, tr, C), lambda s, i, p: (s, p[1] * nb + i, 0)), blk],
        [jax.ShapeDtypeStruct(got.shape, BF16)], [blk], name)[0]


def chip_sum(p32, got, place, name, dep=None):
    _, H, C = p32.shape
    tr = _row_tile(H, C)
    nb = H // tr
    peer = lambda j: pl.BlockSpec((None, tr, C), lambda i, p, j=j: (j, i, 0))
    return _placed_call(
        lambda a, b, c, d: (((a.astype(F32) + b.astype(F32)) + c.astype(F32)) + d.astype(F32),), place, (nb,),
        [p32, got, got, got], [pl.BlockSpec((None, tr, C), lambda i, p: (p[0], i, 0)), peer(0), peer(1), peer(2)],
        [jax.ShapeDtypeStruct((2 * H, C), F32)], [pl.BlockSpec((tr, C), lambda i, p: (p[1] * nb + i, 0))],
        name, dep=dep)[0]


ANY = pl.BlockSpec(memory_space=pl.ANY)
CHIP_FLIPS = ((1, 0), (0, 1), (1, 1))


def _place():
    return lax.axis_index("x"), lax.axis_index("y"), lax.axis_index("c")


def _flip(v, f):
    return 1 - v if f else v


def join_halves(fulls, name):
    n = len(fulls)

    def body(*refs):
        outs = refs[n:2 * n]
        send_sem, recv_sem = refs[2 * n:]
        x, y, c = _place()
        copies = []
        for a in range(n):
            hr = outs[a].shape[0] // 2
            half = outs[a].at[pl.ds(c * hr, hr), :]
            cp = pltpu.make_async_remote_copy(
                src_ref=half, dst_ref=half, send_sem=send_sem.at[a], recv_sem=recv_sem.at[a],
                device_id=(x, y, 1 - c), device_id_type=MESH)
            cp.start()
            copies.append(cp)
        for a, cp in enumerate(copies):
            hr = outs[a].shape[0] // 2
            theirs = outs[a].at[pl.ds((1 - c) * hr, hr), :]
            cp.wait_send()
            pltpu.make_async_remote_copy(
                src_ref=theirs, dst_ref=theirs, send_sem=send_sem.at[a], recv_sem=recv_sem.at[a],
                device_id=(x, y, 1 - c), device_id_type=MESH).wait_recv()

    dma = pltpu.SemaphoreType.DMA
    return pl.pallas_call(
        body, name=name, in_specs=[ANY] * n, out_specs=[ANY] * n,
        out_shape=[jax.ShapeDtypeStruct(f.shape, F32) for f in fulls],
        input_output_aliases={a: a for a in range(n)},
        scratch_shapes=[dma((n,)), dma((n,))],
    )(*fulls)


HBM = pl.BlockSpec(memory_space=pltpu.HBM)
SEM = pl.BlockSpec(memory_space=pltpu.SEMAPHORE)
EFFECT = pltpu.SideEffectType.DATAFLOW_SIDE_EFFECTING


def _in_hbm(a):
    return pltpu.with_memory_space_constraint(a, pltpu.HBM)


def _gather_half(ref, chip_idx, core):
    hr = ref.shape[1] // 2
    return ref.at[chip_idx, pl.ds(core * hr, hr), :]


def gather_forward(fulls, small, name):
    n = len(fulls)

    def body(*refs):
        small_in = refs[n]
        outs, small_out = refs[n + 1:2 * n + 1], refs[2 * n + 1]
        send_sem, recv_sem, sm_send, sm_recv, loc_sem = refs[2 * n + 2:]
        x, y, c = _place()
        mine = 2 * x + y
        chips = [(_flip(x, fx), _flip(y, fy)) for fx, fy in CHIP_FLIPS]
        local = pltpu.make_async_copy(small_in, small_out.at[mine], loc_sem)
        local.start()
        copies = []
        for j, (px, py) in enumerate(chips):
            cp = pltpu.make_async_remote_copy(
                src_ref=small_in, dst_ref=small_out.at[mine], send_sem=sm_send.at[j], recv_sem=sm_recv.at[j],
                device_id=(px, py, c), device_id_type=MESH)
            cp.start()
            copies.append(cp)
        for a in range(n):
            for j, (px, py) in enumerate(chips):
                src = _gather_half(outs[a], 2 * px + py, c)
                cp = pltpu.make_async_remote_copy(
                    src_ref=src, dst_ref=src, send_sem=send_sem.at[3 * a + j], recv_sem=recv_sem.at[3 * a + j],
                    device_id=(x, y, 1 - c), device_id_type=MESH)
                cp.start()
                copies.append(cp)
        for a in range(n):
            for j, (px, py) in enumerate(chips):
                dst = _gather_half(outs[a], 2 * px + py, 1 - c)
                pltpu.make_async_remote_copy(
                    src_ref=dst, dst_ref=dst, send_sem=send_sem.at[3 * a + j], recv_sem=recv_sem.at[3 * a + j],
                    device_id=(x, y, 1 - c), device_id_type=MESH).wait_recv()
        for j, (px, py) in enumerate(chips):
            dst = small_out.at[2 * px + py]
            pltpu.make_async_remote_copy(
                src_ref=dst, dst_ref=dst, send_sem=sm_send.at[j], recv_sem=sm_recv.at[j],
                device_id=(px, py, c), device_id_type=MESH).wait_recv()
        for cp in copies:
            cp.wait_send()
        local.wait()

    dma = pltpu.SemaphoreType.DMA
    out = pl.pallas_call(
        body, name=name, in_specs=[ANY] * (n + 1), out_specs=[ANY] * (n + 1),
        out_shape=[jax.ShapeDtypeStruct(f.shape, f.dtype) for f in fulls]
        + [jax.ShapeDtypeStruct((N_CHIP,) + small.shape, small.dtype)],
        input_output_aliases={a: a for a in range(n)},
        scratch_shapes=[dma((3 * n,)), dma((3 * n,)), dma((3,)), dma((3,)), dma],
    )(*fulls, small)
    return out[:n], out[n]


def _gather_plan(fulls, lands):
    x, y, c = _place()
    mine = 2 * x + y
    return [(_gather_half(f, mine, c), _gather_half(f, mine, c), (_flip(x, fx), _flip(y, fy), c))
            for f in fulls for fx, fy in CHIP_FLIPS]


def _scatter_plan(parts, lands):
    x, y, c = _place()
    plan = []
    for p, l in zip(parts, lands):
        for j, (fx, fy) in enumerate(CHIP_FLIPS):
            px, py = _flip(x, fx), _flip(y, fy)
            plan.append((p.at[2 * px + py], l.at[j], (px, py, c)))
    return plan


def _scatter_lands(parts):
    return [(3,) + p.shape[1:] for p in parts]


def _exchange_plan(grads, lands):
    x, y, c = _place()
    plan = []
    for g, l in zip(grads, lands):
        hr = g.shape[1] // 2
        plan.append((g.at[:, pl.ds((1 - c) * hr, hr), :], l, (x, y, 1 - c)))
    return plan


def _exchange_lands(grads):
    return [(N_CHIP, g.shape[1] // 2, g.shape[2]) for g in grads]


def split_start(plan, copies, srcs, land_shapes, deps, name):
    n, m = len(srcs), len(land_shapes)
    lands = [lax.empty(s, srcs[0].dtype) for s in land_shapes]

    def body(*refs):
        k0 = n + m + len(deps)
        send_sem, recv_sem = refs[k0], refs[k0 + 1]
        thru, token = refs[k0 + 2:k0 + 2 + n + m], refs[k0 + 2 + n + m]
        for k, (src, dst, dev) in enumerate(plan(thru[:n], thru[n:])):
            pltpu.make_async_remote_copy(src_ref=src, dst_ref=dst, send_sem=send_sem.at[k], recv_sem=recv_sem.at[k],
                                         device_id=dev, device_id_type=MESH).start()
        token[...] = jnp.zeros_like(token)

    dma = pltpu.SemaphoreType.DMA
    arrays = list(srcs) + lands
    out = pl.pallas_call(
        body, name=name,
        out_shape=(dma((copies,)), dma((copies,)), *[pltpu.HBM(a.shape, a.dtype) for a in arrays],
                   jax.ShapeDtypeStruct((8, LANES), F32)),
        in_specs=[HBM] * (n + m) + [ANY] * len(deps),
        out_specs=(SEM, SEM, *[HBM] * (n + m), pl.BlockSpec(memory_space=pltpu.VMEM)),
        input_output_aliases={a: 2 + a for a in range(n + m)},
        compiler_params=pltpu.CompilerParams(has_side_effects=EFFECT),
    )(*[_in_hbm(a) for a in arrays], *deps)
    return (out[0], out[1], list(out[2:2 + n]), list(out[2 + n:2 + n + m])), out[2 + n + m]


def split_wait(plan, handle, after, name):
    send_sem, recv_sem, srcs, lands = handle
    n, m = len(srcs), len(lands)

    def body(*refs):
        send_sem, recv_sem = refs[n + m], refs[n + m + 1]
        thru = refs[n + m + 2 + len(after):]
        for k, (src, dst, dev) in enumerate(plan(thru[:n], thru[n:])):
            cp = pltpu.make_async_remote_copy(src_ref=src, dst_ref=dst, send_sem=send_sem.at[k],
                                              recv_sem=recv_sem.at[k], device_id=dev, device_id_type=MESH)
            cp.wait_send()
            cp.wait_recv()

    arrays = list(srcs) + list(lands)
    out = pl.pallas_call(
        body, name=name, out_shape=tuple(pltpu.HBM(a.shape, a.dtype) for a in arrays),
        in_specs=[HBM] * (n + m) + [SEM, SEM] + [ANY] * len(after), out_specs=tuple([HBM] * (n + m)),
        input_output_aliases={a: a for a in range(n + m)},
        compiler_params=pltpu.CompilerParams(has_side_effects=EFFECT),
    )(*arrays, send_sem, recv_sem, *after)
    return list(out[:n]), list(out[n:])


def allreduce_small(block, name, dep):
    R, C = block.shape

    def body(in_ref, dep_ref, out_ref, slots, send_sem, recv_sem):
        x, y, c = _place()
        me = 4 * x + 2 * y + c
        slots[me] = in_ref[...]
        copies = []
        for r in range(1, 8):
            fx, fy, fc = (r >> 2) & 1, (r >> 1) & 1, r & 1
            cp = pltpu.make_async_remote_copy(
                src_ref=in_ref, dst_ref=slots.at[me], send_sem=send_sem.at[r - 1], recv_sem=recv_sem.at[r - 1],
                device_id=(_flip(x, fx), _flip(y, fy), _flip(c, fc)), device_id_type=MESH)
            cp.start()
            copies.append(cp)
        for cp in copies:
            cp.wait()
        acc = slots[0]
        for d in range(1, 8):
            acc = acc + slots[d]
        out_ref[...] = acc

    vm = pl.BlockSpec(memory_space=pltpu.VMEM)
    dma = pltpu.SemaphoreType.DMA
    return pl.pallas_call(
        body, name=name, in_specs=[vm, ANY], out_specs=vm, out_shape=jax.ShapeDtypeStruct((R, C), F32),
        scratch_shapes=[pltpu.VMEM((8, R, C), F32), dma((7,)), dma((7,))],
    )(block, dep)


def local_step(x, mem, target, g_mix, g_mem, q_norm_g, k_norm_g, g_mlp, conv_w, h, mem_n, w_in, w_in_dep,
               rest_weights, early_grads, mid_grads, late_grads, last_grads):
    S, D = x.shape
    proj = mm_nn_shard(h, w_in, "proj", dep=w_in_dep, out_dtype=BF16)
    a_conv = conv_fwd(proj, conv_w, D, "conv_fwd")
    o_sb = sb_fwd(proj, D, "sb_fwd")
    w_conv_out, w_sb_out, w_mem_kv, w_x_out, w_out, w_up, w_down = rest_weights(o_sb)
    kv = mm_nn_shard(mem_n, w_mem_kv, "kv")
    o_x = xa_fwd(proj, kv, q_norm_g, k_norm_g, D, "xa_fwd")
    ys = [mm_nn(a_conv, w_conv_out, "y_conv", out_dtype=BF16), mm_nn(o_sb, w_sb_out, "y_sb", out_dtype=BF16),
          mm_nn(o_x, w_x_out, "y_x", out_dtype=BF16)]
    merged = merge_fwd(proj, ys, D, "merge_fwd")
    x1 = mm_nn(merged, w_out, "x1", res=x)
    h2 = rms_fwd(x1, g_mlp, "rms_mlp")
    up, act = mm_nn_shard(h2, w_up, "up", relu2=True)
    x2 = mm_nn(act, w_down, "x2", res=x1)
    dy, dy16, loss_cols = loss_head(x2, target, "loss_head")
    d_up = mm_nt(dy16, w_down, "d_up", up=up, out_dtype=BF16)
    g = {"w_down": mm_tn(act, dy16, "g_w_down")}
    g["w_up"] = mm_tn(h2, d_up, "g_w_up", shard_out=True)
    dh2 = mm_nt_shard(d_up, w_up, "dh2")
    dx1, dx1_16, g["g_mlp"] = rms_bwd(x1, g_mlp, dh2, "rms_mlp_bwd", dres=dy, want16=True)
    g["w_out"] = mm_tn(merged, dx1_16, "g_w_out")
    dm = mm_nt(dx1_16, w_out, "d_merged")
    dy_c, dy_s, dy_x, d_gate = merge_bwd(proj, ys, dm, D, "merge_bwd")
    g["w_conv_out"] = mm_tn(a_conv, dy_c, "g_w_conv_out")
    g["w_sb_out"] = mm_tn(o_sb, dy_s, "g_w_sb_out")
    g["w_x_out"] = mm_tn(o_x, dy_x, "g_w_x_out")
    d_xq, d_kv, g["q_norm_g"], g["k_norm_g"] = xa_bwd(
        proj, kv, q_norm_g, k_norm_g, mm_nt(dy_x, w_x_out, "d_o_x", out_dtype=BF16), D, "xa_bwd")
    g["w_mem_kv"] = mm_tn(mem_n, d_kv, "g_w_mem_kv", shard_out=True)
    g["g_mem"] = rms_bwd(mem, g_mem, mm_nt_shard(d_kv, w_mem_kv, "d_mem_n"), "rms_mem_bwd", want_dx=False)
    dep = early_grads(g)
    d_a_conv = mm_nt(dy_c, w_conv_out, "d_a_conv", dep=dep, out_dtype=BF16)
    d_ch, d_cb, d_cc, g["conv_w"] = conv_bwd(proj, conv_w, d_a_conv, D, "conv_bwd")
    d_o_sb = mm_nt(dy_s, w_sb_out, "d_o_sb", dep=dep, out_dtype=BF16)
    dq, dk, dv = sb_bwd(proj, d_o_sb, D, "sb_bwd", dep=mid_grads([d_ch, d_o_sb]))
    d_proj = jnp.concatenate([d_ch, d_cb, d_cc, dq, dk, dv, d_xq, d_gate], axis=1)
    g["w_in"] = mm_tn(h, d_proj, "g_w_in", shard_out=True)
    dh = mm_nt_shard(d_proj, w_in, "dh", dep=late_grads(g["w_in"]))
    grad_x, g["g_mix"] = rms_bwd(x, g_mix, dh, "rms_mix_bwd", dres=dx1, dep=last_grads(dh))
    return loss_cols, grad_x, g


BIG = ("w_in", "w_conv_out", "w_sb_out", "w_mem_kv", "w_x_out", "w_out", "w_up", "w_down")
REST = BIG[1:]
COL_SHARDED = ("w_in", "w_mem_kv", "w_up")
WEIGHTS = ("g_mix", "g_mem", "w_in", "conv_w", "w_conv_out", "w_sb_out", "q_norm_g", "k_norm_g",
           "w_mem_kv", "w_x_out", "w_out", "g_mlp", "w_up", "w_down")


def _pack_small(D, g_mix, g_mem, g_mlp, q_norm_g, k_norm_g, conv_w, last):
    qk = jnp.concatenate([q_norm_g, k_norm_g, jnp.zeros((1, D - 2 * X_HEAD_DIM), F32)], axis=1)
    cw = jnp.pad(conv_w, ((0, 0), (0, D - conv_w.shape[1])))
    return jnp.concatenate([g_mix, g_mem, g_mlp, qk, cw, last], axis=0)


def kernel(x, mem, g_mix, g_mem, w_in, conv_w, w_conv_out, w_sb_out, q_norm_g, k_norm_g, w_mem_kv, w_x_out, w_out, g_mlp, w_up, w_down, loss_target, m_g_mix, m_g_mem, m_w_in, m_conv_w, m_w_conv_out, m_w_sb_out, m_q_norm_g, m_k_norm_g, m_w_mem_kv, m_w_x_out, m_w_out, m_g_mlp, m_w_up, m_w_down, v_g_mix, v_g_mem, v_w_in, v_conv_w, v_w_conv_out, v_w_sb_out, v_q_norm_g, v_k_norm_g, v_w_mem_kv, v_w_x_out, v_w_out, v_g_mlp, v_w_up, v_w_down):
    S, D = x.shape[1], x.shape[2]
    w = dict(g_mix=g_mix, g_mem=g_mem, w_in=w_in, conv_w=conv_w, w_conv_out=w_conv_out, w_sb_out=w_sb_out,
             q_norm_g=q_norm_g, k_norm_g=k_norm_g, w_mem_kv=w_mem_kv, w_x_out=w_x_out, w_out=w_out,
             g_mlp=g_mlp, w_up=w_up, w_down=w_down)
    m = dict(g_mix=m_g_mix, g_mem=m_g_mem, w_in=m_w_in, conv_w=m_conv_w, w_conv_out=m_w_conv_out,
             w_sb_out=m_w_sb_out, q_norm_g=m_q_norm_g, k_norm_g=m_k_norm_g, w_mem_kv=m_w_mem_kv,
             w_x_out=m_w_x_out, w_out=m_w_out, g_mlp=m_g_mlp, w_up=m_w_up, w_down=m_w_down)
    v = dict(g_mix=v_g_mix, g_mem=v_g_mem, w_in=v_w_in, conv_w=v_conv_w, w_conv_out=v_w_conv_out,
             w_sb_out=v_w_sb_out, q_norm_g=v_q_norm_g, k_norm_g=v_k_norm_g, w_mem_kv=v_w_mem_kv,
             w_x_out=v_w_x_out, w_out=v_w_out, g_mlp=v_g_mlp, w_up=v_w_up, w_down=v_w_down)
    chip = 2 * lax.axis_index("x") + lax.axis_index("y")
    cs = conv_w.shape[2]

    place = jnp.stack([chip, lax.axis_index("c")]).astype(jnp.int32)
    cw_block = jnp.pad(conv_w[0], ((0, 5), (0, 0)))
    handle, token = split_start(_gather_plan, 3, [cast_into_full(w["w_in"][0], place, "cast_w_in")], [], [],
                                "gather_w_in_start")
    rest16 = [cast_into_full(w[k][0], place, "cast_" + k, dep=token) for k in REST]
    h = rms_fwd(x[0], g_mix, "rms_mix", dep=token)
    mem_n = rms_fwd(mem[0], g_mem, "rms_mem", dep=token)
    landed, _ = split_wait(_gather_plan, handle, [*rest16, h, mem_n], "gather_w_in_wait")
    (w_in_full,), cw_all = gather_forward(landed, cw_block, "gather_w_in_forward")
    conv_full = jnp.concatenate([cw_all[p, :3] for p in range(N_CHIP)], axis=1)
    rest_handle, rest_token = split_start(_gather_plan, 3 * len(REST), rest16, [], [w_in_full], "gather_rest_start")

    def layout(k, a):
        return a if k in COL_SHARDED else a.reshape(-1, a.shape[-1])

    def rest_weights(after):
        landed, _ = split_wait(_gather_plan, rest_handle, [after], "gather_rest_wait")
        return [layout(k, a) for k, a in zip(REST, gather_forward(landed, cw_block, "gather_rest_forward")[0])]

    def blocks(k, a):
        return a if k in COL_SHARDED else a.reshape(N_CHIP, -1, a.shape[-1])

    early = {}

    def early_grads(g):
        early["g4"] = [blocks(k, g[k]) for k in REST]
        early["swap"], token = split_start(_exchange_plan, len(REST), early["g4"], _exchange_lands(early["g4"]),
                                           [g["g_mem"]], "exchange_rest_start")
        return token

    def mid_grads(after):
        g4, got = split_wait(_exchange_plan, early["swap"], after, "exchange_rest_wait")
        p16 = [pair_sum(a, b, place, "pair_sum_" + k) for k, a, b in zip(REST, g4, got)]
        early["fly"], token = split_start(_scatter_plan, 3 * len(REST), p16, _scatter_lands(p16), [],
                                          "scatter_rest_start")
        return token

    late = {}

    def late_grads(gw):
        late["swap"], token = split_start(_exchange_plan, 1, [gw], _exchange_lands([gw]), [], "exchange_w_in_start")
        return token

    def last_grads(after):
        (gw,), (got,) = split_wait(_exchange_plan, late["swap"], [after], "exchange_w_in_wait")
        p16 = [pair_sum(gw, got, place, "pair_sum_w_in")]
        late["fly"], token = split_start(_scatter_plan, 3, p16, _scatter_lands(p16), [], "scatter_w_in_start")
        return token

    loss_cols, grad_x, g = local_step(
        x[0], mem[0], loss_target[0], g_mix, g_mem, q_norm_g, k_norm_g, g_mlp, conv_full, h, mem_n,
        w_in_full, rest_token, rest_weights, early_grads, mid_grads, late_grads, last_grads)
    token = g["g_mix"]

    p16_rest, got_rest = split_wait(_scatter_plan, early["fly"], [token], "scatter_rest_wait")
    gsum, delta, new_m, new_v = {}, {}, {}, {}
    halves = [chip_sum(p, b, place, "chip_sum_" + k, dep=token) for k, p, b in zip(REST, p16_rest, got_rest)]
    for k, a in zip(REST, join_halves(halves, "join_halves_rest")):
        gsum[k] = a[None]
        delta[k], new_m[k], new_v[k] = adamw(w[k], gsum[k], m[k], v[k], "adamw_" + k)
    p16_in, got_in = split_wait(_scatter_plan, late["fly"], [new_v[k] for k in REST], "scatter_w_in_wait")

    small = allreduce_small(
        _pack_small(D, g["g_mix"], g["g_mem"], g["g_mlp"], g["q_norm_g"], g["k_norm_g"], g["conv_w"], loss_cols),
        "allreduce_small", dep=got_in[0])
    loss = (0.5 / D) * jnp.sum(small[7])
    gsum.update({"g_mix": small[0:1], "g_mem": small[1:2], "g_mlp": small[2:3],
                 "q_norm_g": small[3:4, :X_HEAD_DIM], "k_norm_g": small[3:4, X_HEAD_DIM:2 * X_HEAD_DIM],
                 "conv_w": lax.dynamic_slice(small[4:7], (0, chip * cs), (3, cs))[None]})
    half_in = chip_sum(p16_in[0], got_in[0], place, "chip_sum_w_in")
    gsum["w_in"] = join_halves([half_in], "join_halves_w_in")[0][None]
    delta["w_in"], new_m["w_in"], new_v["w_in"] = adamw(w["w_in"], gsum["w_in"], m["w_in"], v["w_in"], "adamw_w_in")
    small_names = ("g_mix", "g_mem", "g_mlp", "q_norm_g", "k_norm_g", "conv_w")
    zero_row = jnp.zeros((1, D), F32)
    packed = [_pack_small(D, *[t[k] if k != "conv_w" else t[k][0] for k in small_names], zero_row)
              for t in (w, gsum, m, v)]
    sm = adamw(*packed, "adamw_small")
    for t, block in zip((delta, new_m, new_v), sm):
        t["g_mix"], t["g_mem"], t["g_mlp"] = block[0:1], block[1:2], block[2:3]
        t["q_norm_g"], t["k_norm_g"] = block[3:4, :X_HEAD_DIM], block[3:4, X_HEAD_DIM:2 * X_HEAD_DIM]
        t["conv_w"] = block[4:7, :cs][None]

    return (loss, grad_x[None], *[gsum[k] for k in WEIGHTS], *[delta[k] for k in WEIGHTS],
            *[new_m[k] for k in WEIGHTS], *[new_v[k] for k in WEIGHTS])
```

```python
import functools

import jax
import jax.numpy as jnp
from jax import lax
from jax.experimental import pallas as pl
from jax.experimental.pallas import tpu as pltpu

F32 = jnp.float32
BF16 = jnp.bfloat16
EPS = 1e-6
N_CHIP = 4
SB_HEAD_DIM = 64
X_HEAD_DIM = 256
LANES = 128
VMEM_LIMIT = 56 * 1024 * 1024
ADAM_LR, ADAM_B1, ADAM_B2, ADAM_EPS, ADAM_WD, ADAM_STEP = 0.001, 0.9, 0.999, 1e-8, 0.01, 10
MESH = pl.DeviceIdType.MESH


def _params(*sem):
    return pltpu.CompilerParams(dimension_semantics=sem, vmem_limit_bytes=VMEM_LIMIT)


def _tile(n, pref):
    if n <= pref:
        return n
    t = 1 << (pref.bit_length() - 1)
    while n % t:
        t //= 2
    return t


NN = (((1,), (0,)), ((), ()))
NT = (((1,), (1,)), ((), ()))
TN = (((0,), (0,)), ((), ()))


def _dot(a, b, dims):
    return lax.dot_general(a.astype(BF16), b.astype(BF16), dims, preferred_element_type=F32)


def mm_nn_shard(a, g, name, relu2=False, dep=None, out_dtype=F32):
    M, K = a.shape
    _, _, Ns = g.shape
    tm, tn = _tile(M, 2048), _tile(Ns, 512)
    nb = Ns // tn

    def body(a_ref, b_ref, *o_refs):
        o_refs = o_refs[len(deps):]
        acc = _dot(a_ref[...], b_ref[...], NN)
        if relu2:
            acc = jnp.maximum(acc, 0.0)
            o_refs[1][...] = (acc * acc).astype(BF16)
        o_refs[0][...] = acc.astype(o_refs[0].dtype)

    o_spec = pl.BlockSpec((tm, tn), lambda i, j: (i, j))
    shapes = [jax.ShapeDtypeStruct((M, N_CHIP * Ns), BF16 if relu2 else out_dtype)]
    specs = [o_spec]
    if relu2:
        shapes.append(jax.ShapeDtypeStruct((M, N_CHIP * Ns), BF16))
        specs.append(o_spec)
    deps = [] if dep is None else [dep]
    out = pl.pallas_call(
        body, grid=(M // tm, N_CHIP * nb), name=name,
        in_specs=[pl.BlockSpec((tm, K), lambda i, j: (i, 0)),
                  pl.BlockSpec((None, K, tn), lambda i, j: (j // nb, 0, j % nb))] + [ANY] * len(deps),
        out_specs=specs, out_shape=shapes, compiler_params=_params("parallel", "parallel"),
    )(a, g, *deps)
    return out if relu2 else out[0]


def mm_nn(a, w, name, res=None, out_dtype=F32):
    M, K = a.shape
    N = w.shape[1]
    tm, tn = _tile(M, 2048 if K <= 2048 else 1024), _tile(N, 512)

    def body(a_ref, b_ref, *refs):
        acc = _dot(a_ref[...], b_ref[...], NN)
        if res is not None:
            acc = refs[0][...] + acc
        refs[-1][...] = acc.astype(out_dtype)

    o_spec = pl.BlockSpec((tm, tn), lambda i, j: (i, j))
    ins = [a, w] + ([res] if res is not None else [])
    return pl.pallas_call(
        body, grid=(M // tm, N // tn), name=name,
        in_specs=[pl.BlockSpec((tm, K), lambda i, j: (i, 0)), pl.BlockSpec((K, tn), lambda i, j: (0, j))]
        + ([o_spec] if res is not None else []),
        out_specs=o_spec, out_shape=jax.ShapeDtypeStruct((M, N), out_dtype),
        compiler_params=_params("parallel", "parallel"),
    )(*ins)


def mm_nt(a, w, name, up=None, out_dtype=F32, dep=None):
    M, N = a.shape
    R = w.shape[0]
    tm, tr = _tile(M, 2048), _tile(R, 512)

    def body(a_ref, b_ref, *refs):
        acc = _dot(a_ref[...], b_ref[...], NT)
        if up is not None:
            acc = acc * (2.0 * jnp.maximum(refs[0][...].astype(F32), 0.0))
        refs[-1][...] = acc.astype(out_dtype)

    o_spec = pl.BlockSpec((tm, tr), lambda i, j: (i, j))
    ins = [a, w] + ([up] if up is not None else []) + ([dep] if dep is not None else [])
    return pl.pallas_call(
        body, grid=(M // tm, R // tr), name=name,
        in_specs=[pl.BlockSpec((tm, N), lambda i, j: (i, 0)), pl.BlockSpec((tr, N), lambda i, j: (j, 0))]
        + ([o_spec] if up is not None else []) + ([ANY] if dep is not None else []),
        out_specs=o_spec, out_shape=jax.ShapeDtypeStruct((M, R), out_dtype),
        compiler_params=_params("parallel", "parallel"),
    )(*ins)


def mm_nt_shard(a, g, name, out_dtype=F32, dep=None):
    deps = [] if dep is None else [dep]
    M = a.shape[0]
    _, R, Ns = g.shape
    tm, tr, tk = _tile(M, 1024), _tile(R, 1024), _tile(Ns, 2560)
    nb = Ns // tk
    nk = N_CHIP * nb

    def body(a_ref, b_ref, *refs):
        o_ref, acc_ref = refs[len(deps):]
        k = pl.program_id(2)

        @pl.when(k == 0)
        def _():
            acc_ref[...] = jnp.zeros_like(acc_ref)

        acc_ref[...] += _dot(a_ref[...], b_ref[...], NT)

        @pl.when(k == nk - 1)
        def _():
            o_ref[...] = acc_ref[...].astype(out_dtype)

    return pl.pallas_call(
        body, grid=(M // tm, R // tr, nk), name=name,
        in_specs=[pl.BlockSpec((tm, tk), lambda i, j, k: (i, k)),
                  pl.BlockSpec((None, tr, tk), lambda i, j, k: (k // nb, j, k % nb))] + [ANY] * len(deps),
        out_specs=pl.BlockSpec((tm, tr), lambda i, j, k: (i, j)),
        out_shape=jax.ShapeDtypeStruct((M, R), out_dtype),
        scratch_shapes=[pltpu.VMEM((tm, tr), F32)],
        compiler_params=_params("parallel", "parallel", "arbitrary"),
    )(a, g, *deps)


def mm_tn(a, b, name, shard_out=False):
    S, M = a.shape
    N = b.shape[1]
    Ns = N // N_CHIP if shard_out else N
    tm, tn = _tile(M, 1024), _tile(Ns, 512)
    nb = Ns // tn

    def body(a_ref, b_ref, o_ref):
        o_ref[...] = _dot(a_ref[...], b_ref[...], TN)

    if shard_out:
        o_spec = pl.BlockSpec((None, tm, tn), lambda i, j: (j // nb, i, j % nb))
        o_shape = jax.ShapeDtypeStruct((N_CHIP, M, Ns), F32)
    else:
        o_spec = pl.BlockSpec((tm, tn), lambda i, j: (i, j))
        o_shape = jax.ShapeDtypeStruct((M, N), F32)
    return pl.pallas_call(
        body, grid=(M // tm, N // tn), name=name,
        in_specs=[pl.BlockSpec((S, tm), lambda i, j: (0, i)), pl.BlockSpec((S, tn), lambda i, j: (0, j))],
        out_specs=o_spec, out_shape=o_shape, compiler_params=_params("parallel", "parallel"),
    )(a, b)


def rms_fwd(x, g, name, dep=None):
    S, D = x.shape
    tm = _tile(S, 512)
    deps = [] if dep is None else [dep]

    def body(x_ref, g_ref, *refs):
        xv = x_ref[...]
        r = lax.rsqrt(jnp.mean(xv * xv, axis=-1, keepdims=True) + EPS)
        refs[-1][...] = ((xv * r) * g_ref[...]).astype(BF16)

    return pl.pallas_call(
        body, grid=(S // tm,), name=name,
        in_specs=[pl.BlockSpec((tm, D), lambda i: (i, 0)), pl.BlockSpec((1, D), lambda i: (0, 0))]
        + [ANY] * len(deps),
        out_specs=pl.BlockSpec((tm, D), lambda i: (i, 0)),
        out_shape=jax.ShapeDtypeStruct((S, D), BF16), compiler_params=_params("parallel"),
    )(x, g, *deps)


def rms_bwd(x, g, dh, name, dres=None, want_dx=True, want16=False, dep=None):
    S, D = x.shape
    tm = _tile(S, 512)

    def body(x_ref, g_ref, dh_ref, *refs):
        i = pl.program_id(0)
        xv = x_ref[...]
        r = lax.rsqrt(jnp.mean(xv * xv, axis=-1, keepdims=True) + EPS)
        xn = xv * r
        dhv = dh_ref[...].astype(F32)
        gg_ref = refs[-1]

        @pl.when(i == 0)
        def _():
            gg_ref[...] = jnp.zeros_like(gg_ref)

        gg_ref[...] += jnp.sum(dhv * xn, axis=0, keepdims=True)
        if want_dx:
            dxn = dhv * g_ref[...]
            dx = r * (dxn - xn * jnp.mean(dxn * xn, axis=-1, keepdims=True))
            if dres is not None:
                dx = refs[0][...] + dx
            refs[-2][...] = dx.astype(refs[-2].dtype)
            if want16:
                refs[-3][...] = dx

    row = pl.BlockSpec((tm, D), lambda i: (i, 0))
    vec = pl.BlockSpec((1, D), lambda i: (0, 0))
    ins, in_specs = [x, g, dh], [row, vec, row]
    if dres is not None:
        ins.append(dres)
        in_specs.append(row)
    if dep is not None:
        ins.append(dep)
        in_specs.append(ANY)
    shapes, specs = [jax.ShapeDtypeStruct((1, D), F32)], [vec]
    if want_dx:
        if want16:
            shapes.insert(0, jax.ShapeDtypeStruct((S, D), BF16))
            specs.insert(0, row)
        shapes.insert(0, jax.ShapeDtypeStruct((S, D), F32))
        specs.insert(0, row)
    out = pl.pallas_call(body, grid=(S // tm,), name=name, in_specs=in_specs, out_specs=specs,
                         out_shape=shapes, compiler_params=_params("arbitrary"))(*ins)
    return out if want_dx else out[0]


def _shift_down(u, k, row):
    return jnp.where(row >= k, pltpu.roll(u, k, axis=0), 0.0)


def _shift_up(u, k, row):
    S = u.shape[0]
    return jnp.where(row < S - k, pltpu.roll(u, S - k, axis=0), 0.0)


def _conv_specs(S, D, tc):
    nb = D // tc
    col = lambda o: pl.BlockSpec((S, tc), lambda j, o=o: (0, o * nb + j))
    return col, pl.BlockSpec((3, tc), lambda j: (0, j))


def conv_fwd(proj, conv_w, D, name):
    S = proj.shape[0]
    tc = _tile(D, 256)
    col, wspec = _conv_specs(S, D, tc)

    def body(ch_ref, cb_ref, cc_ref, w_ref, a_ref):
        row = lax.broadcasted_iota(jnp.int32, (S, tc), 0)
        u = cc_ref[...].astype(F32) * ch_ref[...].astype(F32)
        w = w_ref[...]
        cv = w[0:1, :] * _shift_down(u, 2, row) + w[1:2, :] * _shift_down(u, 1, row) + w[2:3, :] * u
        a_ref[...] = (cb_ref[...].astype(F32) * cv).astype(BF16)

    return pl.pallas_call(
        body, grid=(D // tc,), name=name, in_specs=[col(0), col(1), col(2), wspec],
        out_specs=pl.BlockSpec((S, tc), lambda j: (0, j)),
        out_shape=jax.ShapeDtypeStruct((S, D), BF16), compiler_params=_params("parallel"),
    )(proj, proj, proj, conv_w)


def conv_bwd(proj, conv_w, da, D, name):
    S = proj.shape[0]
    tc = _tile(D, 256)
    col, wspec = _conv_specs(S, D, tc)
    blk = pl.BlockSpec((S, tc), lambda j: (0, j))

    def body(ch_ref, cb_ref, cc_ref, w_ref, da_ref, dch_ref, dcb_ref, dcc_ref, gw_ref):
        row = lax.broadcasted_iota(jnp.int32, (S, tc), 0)
        ch, cb, cc, dav = [r[...].astype(F32) for r in (ch_ref, cb_ref, cc_ref, da_ref)]
        w = w_ref[...]
        u = cc * ch
        u1, u2 = _shift_down(u, 1, row), _shift_down(u, 2, row)
        cv = w[0:1, :] * u2 + w[1:2, :] * u1 + w[2:3, :] * u
        dcb_ref[...] = (dav * cv).astype(BF16)
        dcv = dav * cb
        gw_ref[0:1, :] = jnp.sum(dcv * u2, axis=0, keepdims=True)
        gw_ref[1:2, :] = jnp.sum(dcv * u1, axis=0, keepdims=True)
        gw_ref[2:3, :] = jnp.sum(dcv * u, axis=0, keepdims=True)
        du = w[2:3, :] * dcv + w[1:2, :] * _shift_up(dcv, 1, row) + w[0:1, :] * _shift_up(dcv, 2, row)
        dcc_ref[...] = (du * ch).astype(BF16)
        dch_ref[...] = (du * cc).astype(BF16)

    act = jax.ShapeDtypeStruct((S, D), BF16)
    return pl.pallas_call(
        body, grid=(D // tc,), name=name, in_specs=[col(0), col(1), col(2), wspec, blk],
        out_specs=[blk, blk, blk, wspec], out_shape=[act, act, act, jax.ShapeDtypeStruct((3, D), F32)],
        compiler_params=_params("parallel"),
    )(proj, proj, proj, conv_w, da)


SB_BQ_FWD = 512
SB_BQ_BWD = 256
SB_BK = 128
SB_GROUP = 4


def _sb_consts(bq):
    lane = lax.broadcasted_iota(jnp.int32, (bq, LANES), 1)
    r = lax.broadcasted_iota(jnp.int32, (SB_BK, SB_BK), 0)
    c = lax.broadcasted_iota(jnp.int32, (SB_BK, SB_BK), 1)
    tri_rev = jnp.where(r > c, 1.0, 0.0).astype(BF16)
    tri_fwd = jnp.where(r < c, 1.0, 0.0).astype(BF16)
    return lane, tri_rev, tri_fwd


def _cumsum2(v, tri):
    hi = v.astype(BF16)
    lo = (v - hi.astype(F32)).astype(BF16)
    part = (lax.dot_general(hi, tri, NN, preferred_element_type=F32)
            + lax.dot_general(lo, tri, NN, preferred_element_type=F32))
    return part, jnp.sum(v, axis=1, keepdims=True)


def _sb_logits(z, past):
    sp = jnp.log(1.0 + jnp.exp(-jnp.abs(z)))
    l = jnp.minimum(z, 0.0) - sp
    m = l - z
    if past is not None:
        m = jnp.where(past, m, 0.0)
    return l, m


def _stack_heads(v, lane):
    return jnp.concatenate([jnp.where(lane < SB_HEAD_DIM, v, 0.0), jnp.where(lane >= SB_HEAD_DIM, v, 0.0)],
                           axis=0).astype(BF16)


def _unstack_heads(v, lane):
    bq = v.shape[0] // 2
    return jnp.where(lane < SB_HEAD_DIM, v[:bq], v[bq:])


def _sb_positions(i, bq):
    r = lax.broadcasted_iota(jnp.int32, (2 * bq, SB_BK), 0)
    trow = i * bq + jnp.where(r >= bq, r - bq, r)
    return trow, lax.broadcasted_iota(jnp.int32, (2 * bq, SB_BK), 1)


def _sb_specs(S, D, bq):
    npair = D // LANES
    qspec = pl.BlockSpec((bq, LANES), lambda p, i: (i, 3 * npair + p))
    kspec = pl.BlockSpec((S, LANES), lambda p, i: (0, 4 * npair + p))
    vspec = pl.BlockSpec((S, LANES), lambda p, i: (0, 5 * npair + p))
    return npair, qspec, kspec, vspec


def sb_fwd(proj, D, name):
    S = proj.shape[0]
    bq = min(SB_BQ_FWD, S)
    nd = bq // SB_BK
    npair, qspec, kspec, vspec = _sb_specs(S, D, bq)
    scale = SB_HEAD_DIM ** -0.5

    def body(q_ref, k_ref, v_ref, o_ref, kb_ref, vb_ref):
        i = pl.program_id(1)

        @pl.when(i == 0)
        def _():
            kb_ref[...] = k_ref[...].astype(BF16)
            vb_ref[...] = v_ref[...].astype(BF16)

        lane, tri_rev, _ = _sb_consts(bq)
        qs = _stack_heads(q_ref[...].astype(F32) * scale, lane)
        trow, scol = _sb_positions(i, bq)

        def steps(j0, carry, n, masked):
            ks = [pl.multiple_of((j0 - t) * SB_BK, SB_BK) for t in range(n)]
            past = [(k + scol) < trow if masked else None for k in ks]
            zs = [lax.dot_general(qs, kb_ref[pl.ds(k, SB_BK), :], NT, preferred_element_type=F32) for k in ks]
            lm = [_sb_logits(z, p) for z, p in zip(zs, past)]
            cs = [_cumsum2(m, tri_rev) for _, m in lm]
            c, acc = carry
            for t in range(n):
                a = jnp.exp(lm[t][0] + (cs[t][0] + c))
                if masked:
                    a = jnp.where(past[t], a, 0.0)
                acc = acc + lax.dot_general(a.astype(BF16), vb_ref[pl.ds(ks[t], SB_BK), :], NN,
                                            preferred_element_type=F32)
                c = c + cs[t][1]
            return c, acc

        carry = (jnp.zeros((2 * bq, 1), F32), jnp.zeros((2 * bq, LANES), F32))
        carry = steps(i * nd + nd - 1, carry, nd, True)
        older = i * nd
        groups = older // SB_GROUP
        carry = lax.fori_loop(
            0, groups, lambda t, cr: steps(older - 1 - t * SB_GROUP, cr, SB_GROUP, False), carry)
        rest = older - groups * SB_GROUP
        carry = lax.fori_loop(0, rest // nd, lambda t, cr: steps(rest - 1 - t * nd, cr, nd, False), carry)
        o_ref[...] = _unstack_heads(carry[1], lane)

    return pl.pallas_call(
        body, grid=(npair, S // bq), name=name, in_specs=[qspec, kspec, vspec],
        out_specs=pl.BlockSpec((bq, LANES), lambda p, i: (i, p)),
        out_shape=jax.ShapeDtypeStruct((S, D), F32),
        scratch_shapes=[pltpu.VMEM((S, LANES), BF16), pltpu.VMEM((S, LANES), BF16)],
        compiler_params=_params("parallel", "arbitrary"),
    )(proj, proj, proj)


def sb_bwd(proj, do, D, name, dep=None):
    S = proj.shape[0]
    bq = min(SB_BQ_BWD, S)
    nd = bq // SB_BK
    nkb = S // SB_BK
    npair, qspec, kspec, vspec = _sb_specs(S, D, bq)
    scale = SB_HEAD_DIM ** -0.5

    def body(q_ref, k_ref, v_ref, do_ref, *refs):
        dq_ref, dk_ref, dv_ref, kb_ref, vb_ref, dk_acc, dv_acc, g_scr, b_scr, a_scr = refs[len(deps):]
        i = pl.program_id(1)

        @pl.when(i == 0)
        def _():
            kb_ref[...] = k_ref[...].astype(BF16)
            vb_ref[...] = v_ref[...].astype(BF16)
            dk_acc[...] = jnp.zeros_like(dk_acc)
            dv_acc[...] = jnp.zeros_like(dv_acc)

        lane, tri_rev, tri_fwd = _sb_consts(bq)
        qs = _stack_heads(q_ref[...].astype(F32) * scale, lane)
        dos = _stack_heads(do_ref[...].astype(F32), lane)
        qs_t, dos_t = qs.T, dos.T
        trow, scol = _sb_positions(i, bq)

        def sweep1(j0, c, n, masked):
            js = [j0 - t for t in range(n)]
            ks = [pl.multiple_of(j * SB_BK, SB_BK) for j in js]
            past = [(k + scol) < trow if masked else None for k in ks]
            zs = [lax.dot_general(qs, kb_ref[pl.ds(k, SB_BK), :], NT, preferred_element_type=F32) for k in ks]
            das = [lax.dot_general(dos, vb_ref[pl.ds(k, SB_BK), :], NT, preferred_element_type=F32) for k in ks]
            lm = [_sb_logits(z, p) for z, p in zip(zs, past)]
            cs = [_cumsum2(m, tri_rev) for _, m in lm]
            for t in range(n):
                b_scr[js[t]] = jnp.exp(lm[t][0]).astype(BF16)
            for t in range(n):
                a = jnp.exp(lm[t][0] + (cs[t][0] + c))
                if masked:
                    a = jnp.where(past[t], a, 0.0)
                g_scr[js[t]] = (das[t] * a).astype(BF16)
                a_scr[js[t]] = a.astype(BF16)
                c = c + cs[t][1]
            return c

        older = i * nd
        groups = older // SB_GROUP
        rest = older - groups * SB_GROUP
        c = jnp.zeros((2 * bq, 1), F32)
        c = sweep1(i * nd + nd - 1, c, nd, True)
        c = lax.fori_loop(0, groups, lambda t, cr: sweep1(older - 1 - t * SB_GROUP, cr, SB_GROUP, False), c)
        lax.fori_loop(0, rest // nd, lambda t, cr: sweep1(rest - 1 - t * nd, cr, nd, False), c)

        def sweep2(j0, carry, n, masked):
            js = [j0 + t for t in range(n)]
            ks = [pl.multiple_of(j * SB_BK, SB_BK) for j in js]
            g16 = [g_scr[j] for j in js]
            gv = [g.astype(F32) for g in g16]
            gs = [(lax.dot_general(g, tri_fwd, NN, preferred_element_type=F32), jnp.sum(v, axis=1, keepdims=True))
                  for g, v in zip(g16, gv)]
            pc, dq = carry
            dzs = []
            for t in range(n):
                dz = gv[t] - b_scr[js[t]].astype(F32) * (gv[t] + (gs[t][0] + pc))
                if masked:
                    dz = jnp.where((ks[t] + scol) < trow, dz, 0.0)
                dzs.append(dz.astype(BF16))
                pc = pc + gs[t][1]
            for t in range(n):
                dq = dq + lax.dot_general(dzs[t], kb_ref[pl.ds(ks[t], SB_BK), :], NN, preferred_element_type=F32)
                dk_acc[js[t]] += lax.dot_general(qs_t, dzs[t], NN, preferred_element_type=F32)
                dv_acc[js[t]] += lax.dot_general(dos_t, a_scr[js[t]], NN, preferred_element_type=F32)
            return pc, dq

        carry = (jnp.zeros((2 * bq, 1), F32), jnp.zeros((2 * bq, LANES), F32))
        carry = lax.fori_loop(0, groups, lambda t, cr: sweep2(t * SB_GROUP, cr, SB_GROUP, False), carry)
        carry = lax.fori_loop(
            0, rest // nd, lambda t, cr: sweep2(groups * SB_GROUP + t * nd, cr, nd, False), carry)
        carry = sweep2(i * nd, carry, nd, True)
        dq_ref[...] = (_unstack_heads(carry[1], lane) * scale).astype(BF16)

        @pl.when(i == pl.num_programs(1) - 1)
        def _():
            for j in range(nkb):
                dk_ref[j * SB_BK:(j + 1) * SB_BK, :] = dk_acc[j].T.astype(BF16)
                dv_ref[j * SB_BK:(j + 1) * SB_BK, :] = dv_acc[j].T.astype(BF16)

    deps = [] if dep is None else [dep]
    full = pl.BlockSpec((S, LANES), lambda p, i: (0, p))
    blk = pl.BlockSpec((bq, LANES), lambda p, i: (i, p))
    act = jax.ShapeDtypeStruct((S, D), BF16)
    return pl.pallas_call(
        body, grid=(npair, S // bq), name=name, in_specs=[qspec, kspec, vspec, blk] + [ANY] * len(deps),
        out_specs=[blk, full, full], out_shape=[act, act, act],
        scratch_shapes=[pltpu.VMEM((S, LANES), BF16), pltpu.VMEM((S, LANES), BF16),
                        pltpu.VMEM((nkb, LANES, SB_BK), F32), pltpu.VMEM((nkb, LANES, SB_BK), F32),
                        pltpu.VMEM((nkb, 2 * bq, SB_BK), BF16), pltpu.VMEM((nkb, 2 * bq, SB_BK), BF16),
                        pltpu.VMEM((nkb, 2 * bq, SB_BK), BF16)],
        compiler_params=_params("parallel", "arbitrary"),
    )(proj, proj, proj, do, *deps)


def _rms_rows(v):
    r = lax.rsqrt(jnp.mean(v * v, axis=-1, keepdims=True) + EPS)
    return v * r, r


def _xa_specs(S, D, M, tq):
    nh = D // X_HEAD_DIM
    qspec = pl.BlockSpec((tq, X_HEAD_DIM), lambda h, i: (i, 6 * nh + h))
    kspec = pl.BlockSpec((M, X_HEAD_DIM), lambda h, i: (0, h))
    vspec = pl.BlockSpec((M, X_HEAD_DIM), lambda h, i: (0, nh + h))
    gspec = pl.BlockSpec((1, X_HEAD_DIM), lambda h, i: (0, 0))
    return nh, qspec, kspec, vspec, gspec


def xa_fwd(proj, kv, gq, gk, D, name):
    S, M = proj.shape[0], kv.shape[0]
    tq = _tile(S, 512)
    nh, qspec, kspec, vspec, gspec = _xa_specs(S, D, M, tq)
    scale = X_HEAD_DIM ** -0.5

    def body(q_ref, k_ref, v_ref, gq_ref, gk_ref, o_ref):
        qn = _rms_rows(q_ref[...].astype(F32))[0] * gq_ref[...]
        kn = _rms_rows(k_ref[...])[0] * gk_ref[...]
        s = _dot(qn, kn, NT) * scale
        e = jnp.exp(s - jnp.max(s, axis=-1, keepdims=True))
        p = e / jnp.sum(e, axis=-1, keepdims=True)
        o_ref[...] = _dot(p, v_ref[...], NN)

    return pl.pallas_call(
        body, grid=(nh, S // tq), name=name, in_specs=[qspec, kspec, vspec, gspec, gspec],
        out_specs=pl.BlockSpec((tq, X_HEAD_DIM), lambda h, i: (i, h)),
        out_shape=jax.ShapeDtypeStruct((S, D), F32), compiler_params=_params("parallel", "parallel"),
    )(proj, kv, kv, gq, gk)


def xa_bwd(proj, kv, gq, gk, do, D, name):
    S, M = proj.shape[0], kv.shape[0]
    tq = _tile(S, 512)
    nh, qspec, kspec, vspec, gspec = _xa_specs(S, D, M, tq)
    scale = X_HEAD_DIM ** -0.5

    def body(q_ref, k_ref, v_ref, gq_ref, gk_ref, do_ref, dq_ref, dk_ref, dv_ref, ggq_ref, ggk_ref,
             dkn_acc, dv_acc):
        h, i = pl.program_id(0), pl.program_id(1)

        @pl.when((h == 0) & (i == 0))
        def _():
            ggq_ref[...] = jnp.zeros_like(ggq_ref)
            ggk_ref[...] = jnp.zeros_like(ggk_ref)

        @pl.when(i == 0)
        def _():
            dkn_acc[...] = jnp.zeros_like(dkn_acc)
            dv_acc[...] = jnp.zeros_like(dv_acc)

        gq, gk = gq_ref[...], gk_ref[...]
        qhat, rq = _rms_rows(q_ref[...].astype(F32))
        khat, rk = _rms_rows(k_ref[...])
        qn, kn = qhat * gq, khat * gk
        s = _dot(qn, kn, NT) * scale
        e = jnp.exp(s - jnp.max(s, axis=-1, keepdims=True))
        p = e / jnp.sum(e, axis=-1, keepdims=True)
        dov = do_ref[...]
        dv_acc[...] += _dot(p, dov, TN)
        dp = _dot(dov, v_ref[...], NT)
        ds = (p * (dp - jnp.sum(dp * p, axis=-1, keepdims=True))) * scale
        dqn = _dot(ds, kn, NN)
        dkn_acc[...] += _dot(ds, qn, TN)
        ggq_ref[...] += jnp.sum(dqn * qhat, axis=0, keepdims=True)
        dqh = dqn * gq
        dq_ref[...] = (rq * (dqh - qhat * jnp.mean(dqh * qhat, axis=-1, keepdims=True))).astype(BF16)

        @pl.when(i == pl.num_programs(1) - 1)
        def _():
            dkn = dkn_acc[...]
            ggk_ref[...] += jnp.sum(dkn * khat, axis=0, keepdims=True)
            dkh = dkn * gk
            dk_ref[...] = (rk * (dkh - khat * jnp.mean(dkh * khat, axis=-1, keepdims=True))).astype(BF16)
            dv_ref[...] = dv_acc[...].astype(BF16)

    blk = pl.BlockSpec((tq, X_HEAD_DIM), lambda h, i: (i, h))
    kv_shape = jax.ShapeDtypeStruct((M, 2 * D), BF16)
    gshape = jax.ShapeDtypeStruct((1, X_HEAD_DIM), F32)
    dq, dk, dv, ggq, ggk = pl.pallas_call(
        body, grid=(nh, S // tq), name=name, in_specs=[qspec, kspec, vspec, gspec, gspec, blk],
        out_specs=[blk, kspec, vspec, gspec, gspec],
        out_shape=[jax.ShapeDtypeStruct((S, D), BF16), kv_shape, kv_shape, gshape, gshape],
        scratch_shapes=[pltpu.VMEM((M, X_HEAD_DIM), F32), pltpu.VMEM((M, X_HEAD_DIM), F32)],
        compiler_params=_params("arbitrary", "arbitrary"),
    )(proj, kv, kv, gq, gk, do)
    d_kv = jnp.concatenate([dk[:, :D], dv[:, D:]], axis=1)
    return dq, d_kv, ggq, ggk


def _gate_specs(S, D, tm):
    row = pl.BlockSpec((tm, D), lambda i: (i, 0))
    gate = lambda b: pl.BlockSpec((tm, D), lambda i, b=b: (i, 7 + b))
    return row, gate


def merge_fwd(proj, ys, D, name):
    S = proj.shape[0]
    tm = _tile(S, 256)
    row, gate = _gate_specs(S, D, tm)

    def body(g0, g1, g2, y0, y1, y2, o_ref):
        acc = jax.nn.sigmoid(g0[...].astype(F32)) * y0[...].astype(F32)
        acc = acc + jax.nn.sigmoid(g1[...].astype(F32)) * y1[...].astype(F32)
        acc = acc + jax.nn.sigmoid(g2[...].astype(F32)) * y2[...].astype(F32)
        o_ref[...] = acc.astype(BF16)

    return pl.pallas_call(
        body, grid=(S // tm,), name=name, in_specs=[gate(0), gate(1), gate(2), row, row, row],
        out_specs=row, out_shape=jax.ShapeDtypeStruct((S, D), BF16), compiler_params=_params("parallel"),
    )(proj, proj, proj, *ys)


def merge_bwd(proj, ys, dm, D, name):
    S = proj.shape[0]
    tm = _tile(S, 256)
    row, gate = _gate_specs(S, D, tm)

    def body(g0, g1, g2, y0, y1, y2, dm_ref, d0, d1, d2, dg_ref):
        dmv = dm_ref[...]
        for b, (g_ref, y_ref, d_ref) in enumerate(((g0, y0, d0), (g1, y1, d1), (g2, y2, d2))):
            s = jax.nn.sigmoid(g_ref[...].astype(F32))
            d_ref[...] = (dmv * s).astype(BF16)
            dg_ref[:, b * D:(b + 1) * D] = ((dmv * y_ref[...].astype(F32)) * (s * (1.0 - s))).astype(BF16)

    act = jax.ShapeDtypeStruct((S, D), BF16)
    return pl.pallas_call(
        body, grid=(S // tm,), name=name, in_specs=[gate(0), gate(1), gate(2), row, row, row, row],
        out_specs=[row, row, row, pl.BlockSpec((tm, 3 * D), lambda i: (i, 0))],
        out_shape=[act, act, act, jax.ShapeDtypeStruct((S, 3 * D), BF16)], compiler_params=_params("parallel"),
    )(proj, proj, proj, *ys, dm)


def loss_head(y, target, name):
    S, D = y.shape
    tm = _tile(S, 512)

    def body(y_ref, t_ref, dy_ref, dy16_ref, l_ref):
        @pl.when(pl.program_id(0) == 0)
        def _():
            l_ref[...] = jnp.zeros_like(l_ref)

        e = y_ref[...] - t_ref[...]
        dy = e * (1.0 / D)
        dy_ref[...] = dy
        dy16_ref[...] = dy.astype(BF16)
        l_ref[...] += jnp.sum(e * e, axis=0, keepdims=True)

    row = pl.BlockSpec((tm, D), lambda i: (i, 0))
    vec = pl.BlockSpec((1, D), lambda i: (0, 0))
    return pl.pallas_call(
        body, grid=(S // tm,), name=name, in_specs=[row, row], out_specs=[row, row, vec],
        out_shape=[jax.ShapeDtypeStruct((S, D), F32), jax.ShapeDtypeStruct((S, D), BF16),
                   jax.ShapeDtypeStruct((1, D), F32)],
        compiler_params=_params("arbitrary"),
    )(y, target)


def _rows2d(a):
    return a.reshape(-1, a.shape[-1])


def _ew_call(fn, ins, out_dtypes, name):
    R, C = ins[0].shape
    tr = _tile(R, max(8, (1 << 19) // C))
    spec = pl.BlockSpec((tr, C), lambda i: (i, 0))

    def body(*refs):
        outs = fn(*[r[...] for r in refs[:len(ins)]])
        for o_ref, o in zip(refs[len(ins):], outs):
            o_ref[...] = o.astype(o_ref.dtype)

    return pl.pallas_call(
        body, grid=(R // tr,), name=name, in_specs=[spec] * len(ins), out_specs=[spec] * len(out_dtypes),
        out_shape=[jax.ShapeDtypeStruct((R, C), d) for d in out_dtypes], compiler_params=_params("parallel"),
    )(*ins)


def adamw(w, g, m, v, name):
    def fn(w, g, m, v):
        m = ADAM_B1 * m + (1.0 - ADAM_B1) * g
        v = ADAM_B2 * v + (1.0 - ADAM_B2) * (g * g)
        m_hat = m / (1.0 - ADAM_B1 ** ADAM_STEP)
        v_hat = v / (1.0 - ADAM_B2 ** ADAM_STEP)
        return -ADAM_LR * (m_hat / (jnp.sqrt(v_hat) + ADAM_EPS) + ADAM_WD * w), m, v

    shp = w.shape
    outs = _ew_call(fn, [_rows2d(a) for a in (w, g, m, v)], [F32, F32, F32], name)
    return [o.reshape(shp) for o in outs]


def _placed_call(fn, place, grid, ins, in_specs, out_shape, out_specs, name, dep=None):
    n = len(ins)
    deps = [] if dep is None else [dep]

    def body(place_ref, *refs):
        outs = fn(*[r[...] for r in refs[:n]])
        for o_ref, o in zip(refs[n + len(deps):], outs):
            o_ref[...] = o.astype(o_ref.dtype)

    return pl.pallas_call(
        body, name=name, out_shape=out_shape,
        grid_spec=pltpu.PrefetchScalarGridSpec(
            num_scalar_prefetch=1, grid=grid, in_specs=list(in_specs) + [ANY] * len(deps), out_specs=out_specs),
        compiler_params=_params(*["parallel"] * len(grid)),
    )(place, *ins, *deps)


def _row_tile(R, C):
    return _tile(R, max(16, (1 << 19) // C))


def cast_into_full(w, place, name, dep=None):
    R, C = w.shape
    tr = _row_tile(R, C)
    return _placed_call(
        lambda a: (a,), place, (R // tr,), [w], [pl.BlockSpec((tr, C), lambda i, p: (i, 0))],
        [jax.ShapeDtypeStruct((N_CHIP, R, C), BF16)], [pl.BlockSpec((None, tr, C), lambda i, p: (p[0], i, 0))],
        name, dep=dep)[0]


def pair_sum(g4, got, place, name):
    _, hr, C = got.shape
    tr = _row_tile(hr, C)
    nb = hr // tr
    blk = pl.BlockSpec((None, tr, C), lambda s, i, p: (s, i, 0))
    return _placed_call(
        lambda a, b: (a + b,), place, (N_CHIP, nb), [g4, got],
        [pl.BlockSpec((None, tr, C), lambda s, i, p: (s, p[1] * nb + i, 0)), blk],
        [jax.ShapeDtypeStruct(got.shape, BF16)], [blk], name)[0]


def chip_sum(p32, got, place, name, dep=None):
    _, H, C = p32.shape
    tr = _row_tile(H, C)
    nb = H // tr
    peer = lambda j: pl.BlockSpec((None, tr, C), lambda i, p, j=j: (j, i, 0))
    return _placed_call(
        lambda a, b, c, d: (((a.astype(F32) + b.astype(F32)) + c.astype(F32)) + d.astype(F32),), place, (nb,),
        [p32, got, got, got], [pl.BlockSpec((None, tr, C), lambda i, p: (p[0], i, 0)), peer(0), peer(1), peer(2)],
        [jax.ShapeDtypeStruct((2 * H, C), F32)], [pl.BlockSpec((tr, C), lambda i, p: (p[1] * nb + i, 0))],
        name, dep=dep)[0]


ANY = pl.BlockSpec(memory_space=pl.ANY)
CHIP_FLIPS = ((1, 0), (0, 1), (1, 1))


def _place():
    return lax.axis_index("x"), lax.axis_index("y"), lax.axis_index("c")


def _flip(v, f):
    return 1 - v if f else v


def join_halves(fulls, name):
    n = len(fulls)

    def body(*refs):
        outs = refs[n:2 * n]
        send_sem, recv_sem = refs[2 * n:]
        x, y, c = _place()
        copies = []
        for a in range(n):
            hr = outs[a].shape[0] // 2
            half = outs[a].at[pl.ds(c * hr, hr), :]
            cp = pltpu.make_async_remote_copy(
                src_ref=half, dst_ref=half, send_sem=send_sem.at[a], recv_sem=recv_sem.at[a],
                device_id=(x, y, 1 - c), device_id_type=MESH)
            cp.start()
            copies.append(cp)
        for a, cp in enumerate(copies):
            hr = outs[a].shape[0] // 2
            theirs = outs[a].at[pl.ds((1 - c) * hr, hr), :]
            cp.wait_send()
            pltpu.make_async_remote_copy(
                src_ref=theirs, dst_ref=theirs, send_sem=send_sem.at[a], recv_sem=recv_sem.at[a],
                device_id=(x, y, 1 - c), device_id_type=MESH).wait_recv()

    dma = pltpu.SemaphoreType.DMA
    return pl.pallas_call(
        body, name=name, in_specs=[ANY] * n, out_specs=[ANY] * n,
        out_shape=[jax.ShapeDtypeStruct(f.shape, F32) for f in fulls],
        input_output_aliases={a: a for a in range(n)},
        scratch_shapes=[dma((n,)), dma((n,))],
    )(*fulls)


HBM = pl.BlockSpec(memory_space=pltpu.HBM)
SEM = pl.BlockSpec(memory_space=pltpu.SEMAPHORE)
EFFECT = pltpu.SideEffectType.DATAFLOW_SIDE_EFFECTING


def _in_hbm(a):
    return pltpu.with_memory_space_constraint(a, pltpu.HBM)


def _gather_half(ref, chip_idx, core):
    hr = ref.shape[1] // 2
    return ref.at[chip_idx, pl.ds(core * hr, hr), :]


def gather_forward(fulls, small, name):
    n = len(fulls)

    def body(*refs):
        small_in = refs[n]
        outs, small_out = refs[n + 1:2 * n + 1], refs[2 * n + 1]
        send_sem, recv_sem, sm_send, sm_recv, loc_sem = refs[2 * n + 2:]
        x, y, c = _place()
        mine = 2 * x + y
        chips = [(_flip(x, fx), _flip(y, fy)) for fx, fy in CHIP_FLIPS]
        local = pltpu.make_async_copy(small_in, small_out.at[mine], loc_sem)
        local.start()
        copies = []
        for j, (px, py) in enumerate(chips):
            cp = pltpu.make_async_remote_copy(
                src_ref=small_in, dst_ref=small_out.at[mine], send_sem=sm_send.at[j], recv_sem=sm_recv.at[j],
                device_id=(px, py, c), device_id_type=MESH)
            cp.start()
            copies.append(cp)
        for a in range(n):
            for j, (px, py) in enumerate(chips):
                src = _gather_half(outs[a], 2 * px + py, c)
                cp = pltpu.make_async_remote_copy(
                    src_ref=src, dst_ref=src, send_sem=send_sem.at[3 * a + j], recv_sem=recv_sem.at[3 * a + j],
                    device_id=(x, y, 1 - c), device_id_type=MESH)
                cp.start()
                copies.append(cp)
        for a in range(n):
            for j, (px, py) in enumerate(chips):
                dst = _gather_half(outs[a], 2 * px + py, 1 - c)
                pltpu.make_async_remote_copy(
                    src_ref=dst, dst_ref=dst, send_sem=send_sem.at[3 * a + j], recv_sem=recv_sem.at[3 * a + j],
                    device_id=(x, y, 1 - c), device_id_type=MESH).wait_recv()
        for j, (px, py) in enumerate(chips):
            dst = small_out.at[2 * px + py]
            pltpu.make_async_remote_copy(
                src_ref=dst, dst_ref=dst, send_sem=sm_send.at[j], recv_sem=sm_recv.at[j],
                device_id=(px, py, c), device_id_type=MESH).wait_recv()
        for cp in copies:
            cp.wait_send()
        local.wait()

    dma = pltpu.SemaphoreType.DMA
    out = pl.pallas_call(
        body, name=name, in_specs=[ANY] * (n + 1), out_specs=[ANY] * (n + 1),
        out_shape=[jax.ShapeDtypeStruct(f.shape, f.dtype) for f in fulls]
        + [jax.ShapeDtypeStruct((N_CHIP,) + small.shape, small.dtype)],
        input_output_aliases={a: a for a in range(n)},
        scratch_shapes=[dma((3 * n,)), dma((3 * n,)), dma((3,)), dma((3,)), dma],
    )(*fulls, small)
    return out[:n], out[n]


def _gather_plan(fulls, lands):
    x, y, c = _place()
    mine = 2 * x + y
    return [(_gather_half(f, mine, c), _gather_half(f, mine, c), (_flip(x, fx), _flip(y, fy), c))
            for f in fulls for fx, fy in CHIP_FLIPS]


def _scatter_plan(parts, lands):
    x, y, c = _place()
    plan = []
    for p, l in zip(parts, lands):
        for j, (fx, fy) in enumerate(CHIP_FLIPS):
            px, py = _flip(x, fx), _flip(y, fy)
            plan.append((p.at[2 * px + py], l.at[j], (px, py, c)))
    return plan


def _scatter_lands(parts):
    return [(3,) + p.shape[1:] for p in parts]


def _exchange_plan(grads, lands):
    x, y, c = _place()
    plan = []
    for g, l in zip(grads, lands):
        hr = g.shape[1] // 2
        plan.append((g.at[:, pl.ds((1 - c) * hr, hr), :], l, (x, y, 1 - c)))
    return plan


def _exchange_lands(grads):
    return [(N_CHIP, g.shape[1] // 2, g.shape[2]) for g in grads]


def _forward_plan(fulls, lands):
    x, y, c = _place()
    return [(_gather_half(f, 2 * _flip(x, fx) + _flip(y, fy), c), _gather_half(f, 2 * _flip(x, fx) + _flip(y, fy), c),
             (x, y, 1 - c)) for f in fulls for fx, fy in CHIP_FLIPS]


def split_start(plan, copies, srcs, land_shapes, deps, name):
    n, m = len(srcs), len(land_shapes)
    lands = [lax.empty(s, srcs[0].dtype) for s in land_shapes]

    def body(*refs):
        k0 = n + m + len(deps)
        send_sem, recv_sem = refs[k0], refs[k0 + 1]
        thru, token = refs[k0 + 2:k0 + 2 + n + m], refs[k0 + 2 + n + m]
        for k, (src, dst, dev) in enumerate(plan(thru[:n], thru[n:])):
            pltpu.make_async_remote_copy(src_ref=src, dst_ref=dst, send_sem=send_sem.at[k], recv_sem=recv_sem.at[k],
                                         device_id=dev, device_id_type=MESH).start()
        token[...] = jnp.zeros_like(token)

    dma = pltpu.SemaphoreType.DMA
    arrays = list(srcs) + lands
    out = pl.pallas_call(
        body, name=name,
        out_shape=(dma((copies,)), dma((copies,)), *[pltpu.HBM(a.shape, a.dtype) for a in arrays],
                   jax.ShapeDtypeStruct((8, LANES), F32)),
        in_specs=[HBM] * (n + m) + [ANY] * len(deps),
        out_specs=(SEM, SEM, *[HBM] * (n + m), pl.BlockSpec(memory_space=pltpu.VMEM)),
        input_output_aliases={a: 2 + a for a in range(n + m)},
        compiler_params=pltpu.CompilerParams(has_side_effects=EFFECT),
    )(*[_in_hbm(a) for a in arrays], *deps)
    return (out[0], out[1], list(out[2:2 + n]), list(out[2 + n:2 + n + m])), out[2 + n + m]


def split_wait(plan, handle, after, name):
    send_sem, recv_sem, srcs, lands = handle
    n, m = len(srcs), len(lands)

    def body(*refs):
        send_sem, recv_sem = refs[n + m], refs[n + m + 1]
        thru = refs[n + m + 2 + len(after):]
        for k, (src, dst, dev) in enumerate(plan(thru[:n], thru[n:])):
            cp = pltpu.make_async_remote_copy(src_ref=src, dst_ref=dst, send_sem=send_sem.at[k],
                                              recv_sem=recv_sem.at[k], device_id=dev, device_id_type=MESH)
            cp.wait_send()
            cp.wait_recv()

    arrays = list(srcs) + list(lands)
    out = pl.pallas_call(
        body, name=name, out_shape=tuple(pltpu.HBM(a.shape, a.dtype) for a in arrays),
        in_specs=[HBM] * (n + m) + [SEM, SEM] + [ANY] * len(after), out_specs=tuple([HBM] * (n + m)),
        input_output_aliases={a: a for a in range(n + m)},
        compiler_params=pltpu.CompilerParams(has_side_effects=EFFECT),
    )(*arrays, send_sem, recv_sem, *after)
    return list(out[:n]), list(out[n:])


def allreduce_small(block, name, dep):
    R, C = block.shape

    def body(in_ref, dep_ref, out_ref, slots, send_sem, recv_sem):
        x, y, c = _place()
        me = 4 * x + 2 * y + c
        slots[me] = in_ref[...]
        copies = []
        for r in range(1, 8):
            fx, fy, fc = (r >> 2) & 1, (r >> 1) & 1, r & 1
            cp = pltpu.make_async_remote_copy(
                src_ref=in_ref, dst_ref=slots.at[me], send_sem=send_sem.at[r - 1], recv_sem=recv_sem.at[r - 1],
                device_id=(_flip(x, fx), _flip(y, fy), _flip(c, fc)), device_id_type=MESH)
            cp.start()
            copies.append(cp)
        for cp in copies:
            cp.wait()
        acc = slots[0]
        for d in range(1, 8):
            acc = acc + slots[d]
        out_ref[...] = acc

    vm = pl.BlockSpec(memory_space=pltpu.VMEM)
    dma = pltpu.SemaphoreType.DMA
    return pl.pallas_call(
        body, name=name, in_specs=[vm, ANY], out_specs=vm, out_shape=jax.ShapeDtypeStruct((R, C), F32),
        scratch_shapes=[pltpu.VMEM((8, R, C), F32), dma((7,)), dma((7,))],
    )(block, dep)


def local_step(x, mem, target, g_mix, g_mem, q_norm_g, k_norm_g, g_mlp, conv_w, h, mem_n, w_in, w_in_dep,
               rest_weights, early_grads, mid_grads, late_grads, last_grads):
    S, D = x.shape
    proj = mm_nn_shard(h, w_in, "proj", dep=w_in_dep, out_dtype=BF16)
    a_conv = conv_fwd(proj, conv_w, D, "conv_fwd")
    o_sb = sb_fwd(proj, D, "sb_fwd")
    w_conv_out, w_sb_out, w_mem_kv, w_x_out, w_out, mlp_dep, mlp_weights = rest_weights(o_sb)
    kv = mm_nn_shard(mem_n, w_mem_kv, "kv", dep=mlp_dep)
    o_x = xa_fwd(proj, kv, q_norm_g, k_norm_g, D, "xa_fwd")
    ys = [mm_nn(a_conv, w_conv_out, "y_conv", out_dtype=BF16), mm_nn(o_sb, w_sb_out, "y_sb", out_dtype=BF16),
          mm_nn(o_x, w_x_out, "y_x", out_dtype=BF16)]
    merged = merge_fwd(proj, ys, D, "merge_fwd")
    x1 = mm_nn(merged, w_out, "x1", res=x)
    h2 = rms_fwd(x1, g_mlp, "rms_mlp")
    w_up, w_down = mlp_weights(h2)
    up, act = mm_nn_shard(h2, w_up, "up", relu2=True)
    x2 = mm_nn(act, w_down, "x2", res=x1)
    dy, dy16, loss_cols = loss_head(x2, target, "loss_head")
    d_up = mm_nt(dy16, w_down, "d_up", up=up, out_dtype=BF16)
    g = {"w_down": mm_tn(act, dy16, "g_w_down")}
    g["w_up"] = mm_tn(h2, d_up, "g_w_up", shard_out=True)
    dh2 = mm_nt_shard(d_up, w_up, "dh2")
    dx1, dx1_16, g["g_mlp"] = rms_bwd(x1, g_mlp, dh2, "rms_mlp_bwd", dres=dy, want16=True)
    g["w_out"] = mm_tn(merged, dx1_16, "g_w_out")
    dm = mm_nt(dx1_16, w_out, "d_merged")
    dy_c, dy_s, dy_x, d_gate = merge_bwd(proj, ys, dm, D, "merge_bwd")
    g["w_conv_out"] = mm_tn(a_conv, dy_c, "g_w_conv_out")
    g["w_sb_out"] = mm_tn(o_sb, dy_s, "g_w_sb_out")
    g["w_x_out"] = mm_tn(o_x, dy_x, "g_w_x_out")
    d_xq, d_kv, g["q_norm_g"], g["k_norm_g"] = xa_bwd(
        proj, kv, q_norm_g, k_norm_g, mm_nt(dy_x, w_x_out, "d_o_x", out_dtype=BF16), D, "xa_bwd")
    g["w_mem_kv"] = mm_tn(mem_n, d_kv, "g_w_mem_kv", shard_out=True)
    g["g_mem"] = rms_bwd(mem, g_mem, mm_nt_shard(d_kv, w_mem_kv, "d_mem_n"), "rms_mem_bwd", want_dx=False)
    dep = early_grads(g)
    d_a_conv = mm_nt(dy_c, w_conv_out, "d_a_conv", dep=dep, out_dtype=BF16)
    d_ch, d_cb, d_cc, g["conv_w"] = conv_bwd(proj, conv_w, d_a_conv, D, "conv_bwd")
    d_o_sb = mm_nt(dy_s, w_sb_out, "d_o_sb", dep=dep, out_dtype=BF16)
    dq, dk, dv = sb_bwd(proj, d_o_sb, D, "sb_bwd", dep=mid_grads([d_ch, d_o_sb]))
    d_proj = jnp.concatenate([d_ch, d_cb, d_cc, dq, dk, dv, d_xq, d_gate], axis=1)
    g["w_in"] = mm_tn(h, d_proj, "g_w_in", shard_out=True)
    dh = mm_nt_shard(d_proj, w_in, "dh", dep=late_grads(g["w_in"]))
    grad_x, g["g_mix"] = rms_bwd(x, g_mix, dh, "rms_mix_bwd", dres=dx1, dep=last_grads(dh))
    return loss_cols, grad_x, g


BIG = ("w_in", "w_conv_out", "w_sb_out", "w_mem_kv", "w_x_out", "w_out", "w_up", "w_down")
REST = BIG[1:]
COL_SHARDED = ("w_in", "w_mem_kv", "w_up")
WEIGHTS = ("g_mix", "g_mem", "w_in", "conv_w", "w_conv_out", "w_sb_out", "q_norm_g", "k_norm_g",
           "w_mem_kv", "w_x_out", "w_out", "g_mlp", "w_up", "w_down")


def _pack_small(D, g_mix, g_mem, g_mlp, q_norm_g, k_norm_g, conv_w, last):
    qk = jnp.concatenate([q_norm_g, k_norm_g, jnp.zeros((1, D - 2 * X_HEAD_DIM), F32)], axis=1)
    cw = jnp.pad(conv_w, ((0, 0), (0, D - conv_w.shape[1])))
    return jnp.concatenate([g_mix, g_mem, g_mlp, qk, cw, last], axis=0)


def kernel(x, mem, g_mix, g_mem, w_in, conv_w, w_conv_out, w_sb_out, q_norm_g, k_norm_g, w_mem_kv, w_x_out, w_out, g_mlp, w_up, w_down, loss_target, m_g_mix, m_g_mem, m_w_in, m_conv_w, m_w_conv_out, m_w_sb_out, m_q_norm_g, m_k_norm_g, m_w_mem_kv, m_w_x_out, m_w_out, m_g_mlp, m_w_up, m_w_down, v_g_mix, v_g_mem, v_w_in, v_conv_w, v_w_conv_out, v_w_sb_out, v_q_norm_g, v_k_norm_g, v_w_mem_kv, v_w_x_out, v_w_out, v_g_mlp, v_w_up, v_w_down):
    S, D = x.shape[1], x.shape[2]
    w = dict(g_mix=g_mix, g_mem=g_mem, w_in=w_in, conv_w=conv_w, w_conv_out=w_conv_out, w_sb_out=w_sb_out,
             q_norm_g=q_norm_g, k_norm_g=k_norm_g, w_mem_kv=w_mem_kv, w_x_out=w_x_out, w_out=w_out,
             g_mlp=g_mlp, w_up=w_up, w_down=w_down)
    m = dict(g_mix=m_g_mix, g_mem=m_g_mem, w_in=m_w_in, conv_w=m_conv_w, w_conv_out=m_w_conv_out,
             w_sb_out=m_w_sb_out, q_norm_g=m_q_norm_g, k_norm_g=m_k_norm_g, w_mem_kv=m_w_mem_kv,
             w_x_out=m_w_x_out, w_out=m_w_out, g_mlp=m_g_mlp, w_up=m_w_up, w_down=m_w_down)
    v = dict(g_mix=v_g_mix, g_mem=v_g_mem, w_in=v_w_in, conv_w=v_conv_w, w_conv_out=v_w_conv_out,
             w_sb_out=v_w_sb_out, q_norm_g=v_q_norm_g, k_norm_g=v_k_norm_g, w_mem_kv=v_w_mem_kv,
             w_x_out=v_w_x_out, w_out=v_w_out, g_mlp=v_g_mlp, w_up=v_w_up, w_down=v_w_down)
    chip = 2 * lax.axis_index("x") + lax.axis_index("y")
    cs = conv_w.shape[2]

    place = jnp.stack([chip, lax.axis_index("c")]).astype(jnp.int32)
    cw_block = jnp.pad(conv_w[0], ((0, 5), (0, 0)))
    handle, token = split_start(_gather_plan, 3, [cast_into_full(w["w_in"][0], place, "cast_w_in")], [], [],
                                "gather_w_in_start")
    rest16 = [cast_into_full(w[k][0], place, "cast_" + k, dep=token) for k in REST]
    h = rms_fwd(x[0], g_mix, "rms_mix", dep=token)
    mem_n = rms_fwd(mem[0], g_mem, "rms_mem", dep=token)
    landed, _ = split_wait(_gather_plan, handle, [*rest16, h, mem_n], "gather_w_in_wait")
    (w_in_full,), cw_all = gather_forward(landed, cw_block, "gather_w_in_forward")
    conv_full = jnp.concatenate([cw_all[p, :3] for p in range(N_CHIP)], axis=1)
    rest_handle, rest_token = split_start(_gather_plan, 3 * len(REST), rest16, [], [w_in_full], "gather_rest_start")

    def layout(k, a):
        return a if k in COL_SHARDED else a.reshape(-1, a.shape[-1])

    def rest_weights(after):
        landed, _ = split_wait(_gather_plan, rest_handle, [after], "gather_rest_wait")
        first = gather_forward(landed[:-2], cw_block, "gather_rest_forward")[0]
        mlp_handle, token = split_start(_forward_plan, 6, landed[-2:], [], [first[0]], "forward_mlp_start")

        def mlp_weights(after):
            both, _ = split_wait(_forward_plan, mlp_handle, [after], "forward_mlp_wait")
            return [layout(k, a) for k, a in zip(REST[-2:], both)]

        return [layout(k, a) for k, a in zip(REST[:-2], first)] + [token, mlp_weights]

    def blocks(k, a):
        return a if k in COL_SHARDED else a.reshape(N_CHIP, -1, a.shape[-1])

    early = {}

    def early_grads(g):
        early["g4"] = [blocks(k, g[k]) for k in REST]
        early["swap"], token = split_start(_exchange_plan, len(REST), early["g4"], _exchange_lands(early["g4"]),
                                           [g["g_mem"]], "exchange_rest_start")
        return token

    def mid_grads(after):
        g4, got = split_wait(_exchange_plan, early["swap"], after, "exchange_rest_wait")
        p16 = [pair_sum(a, b, place, "pair_sum_" + k) for k, a, b in zip(REST, g4, got)]
        early["fly"], token = split_start(_scatter_plan, 3 * len(REST), p16, _scatter_lands(p16), [],
                                          "scatter_rest_start")
        return token

    late = {}

    def late_grads(gw):
        late["swap"], token = split_start(_exchange_plan, 1, [gw], _exchange_lands([gw]), [], "exchange_w_in_start")
        return token

    def last_grads(after):
        (gw,), (got,) = split_wait(_exchange_plan, late["swap"], [after], "exchange_w_in_wait")
        p16 = [pair_sum(gw, got, place, "pair_sum_w_in")]
        late["fly"], token = split_start(_scatter_plan, 3, p16, _scatter_lands(p16), [], "scatter_w_in_start")
        return token

    loss_cols, grad_x, g = local_step(
        x[0], mem[0], loss_target[0], g_mix, g_mem, q_norm_g, k_norm_g, g_mlp, conv_full, h, mem_n,
        w_in_full, rest_token, rest_weights, early_grads, mid_grads, late_grads, last_grads)
    token = g["g_mix"]

    p16_rest, got_rest = split_wait(_scatter_plan, early["fly"], [token], "scatter_rest_wait")
    gsum, delta, new_m, new_v = {}, {}, {}, {}
    halves = [chip_sum(p, b, place, "chip_sum_" + k, dep=token) for k, p, b in zip(REST, p16_rest, got_rest)]
    for k, a in zip(REST, join_halves(halves, "join_halves_rest")):
        gsum[k] = a[None]
        delta[k], new_m[k], new_v[k] = adamw(w[k], gsum[k], m[k], v[k], "adamw_" + k)
    p16_in, got_in = split_wait(_scatter_plan, late["fly"], [new_v[k] for k in REST], "scatter_w_in_wait")

    small = allreduce_small(
        _pack_small(D, g["g_mix"], g["g_mem"], g["g_mlp"], g["q_norm_g"], g["k_norm_g"], g["conv_w"], loss_cols),
        "allreduce_small", dep=got_in[0])
    loss = (0.5 / D) * jnp.sum(small[7])
    gsum.update({"g_mix": small[0:1], "g_mem": small[1:2], "g_mlp": small[2:3],
                 "q_norm_g": small[3:4, :X_HEAD_DIM], "k_norm_g": small[3:4, X_HEAD_DIM:2 * X_HEAD_DIM],
                 "conv_w": lax.dynamic_slice(small[4:7], (0, chip * cs), (3, cs))[None]})
    half_in = chip_sum(p16_in[0], got_in[0], place, "chip_sum_w_in")
    gsum["w_in"] = join_halves([half_in], "join_halves_w_in")[0][None]
    delta["w_in"], new_m["w_in"], new_v["w_in"] = adamw(w["w_in"], gsum["w_in"], m["w_in"], v["w_in"], "adamw_w_in")
    small_names = ("g_mix", "g_mem", "g_mlp", "q_norm_g", "k_norm_g", "conv_w")
    zero_row = jnp.zeros((1, D), F32)
    packed = [_pack_small(D, *[t[k] if k != "conv_w" else t[k][0] for k in small_names], zero_row)
              for t in (w, gsum, m, v)]
    sm = adamw(*packed, "adamw_small")
    for t, block in zip((delta, new_m, new_v), sm):
        t["g_mix"], t["g_mem"], t["g_mlp"] = block[0:1], block[1:2], block[2:3]
        t["q_norm_g"], t["k_norm_g"] = block[3:4, :X_HEAD_DIM], block[3:4, X_HEAD_DIM:2 * X_HEAD_DIM]
        t["conv_w"] = block[4:7, :cs][None]

    return (loss, grad_x[None], *[gsum[k] for k in WEIGHTS], *[delta[k] for k in WEIGHTS],
            *[new_m[k] for k in WEIGHTS], *[new_v[k] for k in WEIGHTS])
```

```python
import functools

import jax
import jax.numpy as jnp
from jax import lax
from jax.experimental import pallas as pl
from jax.experimental.pallas import tpu as pltpu

F32 = jnp.float32
BF16 = jnp.bfloat16
EPS = 1e-6
N_CHIP = 4
SB_HEAD_DIM = 64
X_HEAD_DIM = 256
LANES = 128
VMEM_LIMIT = 56 * 1024 * 1024
ADAM_LR, ADAM_B1, ADAM_B2, ADAM_EPS, ADAM_WD, ADAM_STEP = 0.001, 0.9, 0.999, 1e-8, 0.01, 10
MESH = pl.DeviceIdType.MESH


def _params(*sem):
    return pltpu.CompilerParams(dimension_semantics=sem, vmem_limit_bytes=VMEM_LIMIT)


def _tile(n, pref):
    if n <= pref:
        return n
    t = 1 << (pref.bit_length() - 1)
    while n % t:
        t //= 2
    return t


NN = (((1,), (0,)), ((), ()))
NT = (((1,), (1,)), ((), ()))
TN = (((0,), (0,)), ((), ()))


def _dot(a, b, dims):
    return lax.dot_general(a.astype(BF16), b.astype(BF16), dims, preferred_element_type=F32)


def mm_nn_shard(a, g, name, relu2=False, dep=None, out_dtype=F32):
    M, K = a.shape
    _, _, Ns = g.shape
    tm, tn = _tile(M, 2048), _tile(Ns, 512)
    nb = Ns // tn

    def body(a_ref, b_ref, *o_refs):
        o_refs = o_refs[len(deps):]
        acc = _dot(a_ref[...], b_ref[...], NN)
        if relu2:
            acc = jnp.maximum(acc, 0.0)
            o_refs[1][...] = (acc * acc).astype(BF16)
        o_refs[0][...] = acc.astype(o_refs[0].dtype)

    o_spec = pl.BlockSpec((tm, tn), lambda i, j: (i, j))
    shapes = [jax.ShapeDtypeStruct((M, N_CHIP * Ns), BF16 if relu2 else out_dtype)]
    specs = [o_spec]
    if relu2:
        shapes.append(jax.ShapeDtypeStruct((M, N_CHIP * Ns), BF16))
        specs.append(o_spec)
    deps = [] if dep is None else [dep]
    out = pl.pallas_call(
        body, grid=(M // tm, N_CHIP * nb), name=name,
        in_specs=[pl.BlockSpec((tm, K), lambda i, j: (i, 0)),
                  pl.BlockSpec((None, K, tn), lambda i, j: (j // nb, 0, j % nb))] + [ANY] * len(deps),
        out_specs=specs, out_shape=shapes, compiler_params=_params("parallel", "parallel"),
    )(a, g, *deps)
    return out if relu2 else out[0]


def mm_nn(a, w, name, res=None, out_dtype=F32):
    M, K = a.shape
    N = w.shape[1]
    tm, tn = _tile(M, 2048 if K <= 2048 else 1024), _tile(N, 512)

    def body(a_ref, b_ref, *refs):
        acc = _dot(a_ref[...], b_ref[...], NN)
        if res is not None:
            acc = refs[0][...] + acc
        refs[-1][...] = acc.astype(out_dtype)

    o_spec = pl.BlockSpec((tm, tn), lambda i, j: (i, j))
    ins = [a, w] + ([res] if res is not None else [])
    return pl.pallas_call(
        body, grid=(M // tm, N // tn), name=name,
        in_specs=[pl.BlockSpec((tm, K), lambda i, j: (i, 0)), pl.BlockSpec((K, tn), lambda i, j: (0, j))]
        + ([o_spec] if res is not None else []),
        out_specs=o_spec, out_shape=jax.ShapeDtypeStruct((M, N), out_dtype),
        compiler_params=_params("parallel", "parallel"),
    )(*ins)


def mm_nt(a, w, name, up=None, out_dtype=F32, dep=None):
    M, N = a.shape
    R = w.shape[0]
    tm, tr = _tile(M, 2048), _tile(R, 512)

    def body(a_ref, b_ref, *refs):
        acc = _dot(a_ref[...], b_ref[...], NT)
        if up is not None:
            acc = acc * (2.0 * jnp.maximum(refs[0][...].astype(F32), 0.0))
        refs[-1][...] = acc.astype(out_dtype)

    o_spec = pl.BlockSpec((tm, tr), lambda i, j: (i, j))
    ins = [a, w] + ([up] if up is not None else []) + ([dep] if dep is not None else [])
    return pl.pallas_call(
        body, grid=(M // tm, R // tr), name=name,
        in_specs=[pl.BlockSpec((tm, N), lambda i, j: (i, 0)), pl.BlockSpec((tr, N), lambda i, j: (j, 0))]
        + ([o_spec] if up is not None else []) + ([ANY] if dep is not None else []),
        out_specs=o_spec, out_shape=jax.ShapeDtypeStruct((M, R), out_dtype),
        compiler_params=_params("parallel", "parallel"),
    )(*ins)


def mm_nt_shard(a, g, name, out_dtype=F32, dep=None):
    deps = [] if dep is None else [dep]
    M = a.shape[0]
    _, R, Ns = g.shape
    tm, tr, tk = _tile(M, 1024), _tile(R, 1024), _tile(Ns, 2560)
    nb = Ns // tk
    nk = N_CHIP * nb

    def body(a_ref, b_ref, *refs):
        o_ref, acc_ref = refs[len(deps):]
        k = pl.program_id(2)

        @pl.when(k == 0)
        def _():
            acc_ref[...] = jnp.zeros_like(acc_ref)

        acc_ref[...] += _dot(a_ref[...], b_ref[...], NT)

        @pl.when(k == nk - 1)
        def _():
            o_ref[...] = acc_ref[...].astype(out_dtype)

    return pl.pallas_call(
        body, grid=(M // tm, R // tr, nk), name=name,
        in_specs=[pl.BlockSpec((tm, tk), lambda i, j, k: (i, k)),
                  pl.BlockSpec((None, tr, tk), lambda i, j, k: (k // nb, j, k % nb))] + [ANY] * len(deps),
        out_specs=pl.BlockSpec((tm, tr), lambda i, j, k: (i, j)),
        out_shape=jax.ShapeDtypeStruct((M, R), out_dtype),
        scratch_shapes=[pltpu.VMEM((tm, tr), F32)],
        compiler_params=_params("parallel", "parallel", "arbitrary"),
    )(a, g, *deps)


def mm_tn(a, b, name, shard_out=False):
    S, M = a.shape
    N = b.shape[1]
    Ns = N // N_CHIP if shard_out else N
    tm, tn = _tile(M, 1024), _tile(Ns, 512)
    nb = Ns // tn

    def body(a_ref, b_ref, o_ref):
        o_ref[...] = _dot(a_ref[...], b_ref[...], TN)

    if shard_out:
        o_spec = pl.BlockSpec((None, tm, tn), lambda i, j: (j // nb, i, j % nb))
        o_shape = jax.ShapeDtypeStruct((N_CHIP, M, Ns), F32)
    else:
        o_spec = pl.BlockSpec((tm, tn), lambda i, j: (i, j))
        o_shape = jax.ShapeDtypeStruct((M, N), F32)
    return pl.pallas_call(
        body, grid=(M // tm, N // tn), name=name,
        in_specs=[pl.BlockSpec((S, tm), lambda i, j: (0, i)), pl.BlockSpec((S, tn), lambda i, j: (0, j))],
        out_specs=o_spec, out_shape=o_shape, compiler_params=_params("parallel", "parallel"),
    )(a, b)


def rms_fwd(x, g, name, dep=None):
    S, D = x.shape
    tm = _tile(S, 512)
    deps = [] if dep is None else [dep]

    def body(x_ref, g_ref, *refs):
        xv = x_ref[...]
        r = lax.rsqrt(jnp.mean(xv * xv, axis=-1, keepdims=True) + EPS)
        refs[-1][...] = ((xv * r) * g_ref[...]).astype(BF16)

    return pl.pallas_call(
        body, grid=(S // tm,), name=name,
        in_specs=[pl.BlockSpec((tm, D), lambda i: (i, 0)), pl.BlockSpec((1, D), lambda i: (0, 0))]
        + [ANY] * len(deps),
        out_specs=pl.BlockSpec((tm, D), lambda i: (i, 0)),
        out_shape=jax.ShapeDtypeStruct((S, D), BF16), compiler_params=_params("parallel"),
    )(x, g, *deps)


def rms_bwd(x, g, dh, name, dres=None, want_dx=True, want16=False, dep=None):
    S, D = x.shape
    tm = _tile(S, 512)

    def body(x_ref, g_ref, dh_ref, *refs):
        i = pl.program_id(0)
        xv = x_ref[...]
        r = lax.rsqrt(jnp.mean(xv * xv, axis=-1, keepdims=True) + EPS)
        xn = xv * r
        dhv = dh_ref[...].astype(F32)
        gg_ref = refs[-1]

        @pl.when(i == 0)
        def _():
            gg_ref[...] = jnp.zeros_like(gg_ref)

        gg_ref[...] += jnp.sum(dhv * xn, axis=0, keepdims=True)
        if want_dx:
            dxn = dhv * g_ref[...]
            dx = r * (dxn - xn * jnp.mean(dxn * xn, axis=-1, keepdims=True))
            if dres is not None:
                dx = refs[0][...] + dx
            refs[-2][...] = dx.astype(refs[-2].dtype)
            if want16:
                refs[-3][...] = dx

    row = pl.BlockSpec((tm, D), lambda i: (i, 0))
    vec = pl.BlockSpec((1, D), lambda i: (0, 0))
    ins, in_specs = [x, g, dh], [row, vec, row]
    if dres is not None:
        ins.append(dres)
        in_specs.append(row)
    if dep is not None:
        ins.append(dep)
        in_specs.append(ANY)
    shapes, specs = [jax.ShapeDtypeStruct((1, D), F32)], [vec]
    if want_dx:
        if want16:
            shapes.insert(0, jax.ShapeDtypeStruct((S, D), BF16))
            specs.insert(0, row)
        shapes.insert(0, jax.ShapeDtypeStruct((S, D), F32))
        specs.insert(0, row)
    out = pl.pallas_call(body, grid=(S // tm,), name=name, in_specs=in_specs, out_specs=specs,
                         out_shape=shapes, compiler_params=_params("arbitrary"))(*ins)
    return out if want_dx else out[0]


def _shift_down(u, k, row):
    return jnp.where(row >= k, pltpu.roll(u, k, axis=0), 0.0)


def _shift_up(u, k, row):
    S = u.shape[0]
    return jnp.where(row < S - k, pltpu.roll(u, S - k, axis=0), 0.0)


def _conv_specs(S, D, tc):
    nb = D // tc
    col = lambda o: pl.BlockSpec((S, tc), lambda j, o=o: (0, o * nb + j))
    return col, pl.BlockSpec((3, tc), lambda j: (0, j))


def conv_fwd(proj, conv_w, D, name):
    S = proj.shape[0]
    tc = _tile(D, 256)
    col, wspec = _conv_specs(S, D, tc)

    def body(ch_ref, cb_ref, cc_ref, w_ref, a_ref):
        row = lax.broadcasted_iota(jnp.int32, (S, tc), 0)
        u = cc_ref[...].astype(F32) * ch_ref[...].astype(F32)
        w = w_ref[...]
        cv = w[0:1, :] * _shift_down(u, 2, row) + w[1:2, :] * _shift_down(u, 1, row) + w[2:3, :] * u
        a_ref[...] = (cb_ref[...].astype(F32) * cv).astype(BF16)

    return pl.pallas_call(
        body, grid=(D // tc,), name=name, in_specs=[col(0), col(1), col(2), wspec],
        out_specs=pl.BlockSpec((S, tc), lambda j: (0, j)),
        out_shape=jax.ShapeDtypeStruct((S, D), BF16), compiler_params=_params("parallel"),
    )(proj, proj, proj, conv_w)


def conv_bwd(proj, conv_w, da, D, name):
    S = proj.shape[0]
    tc = _tile(D, 256)
    col, wspec = _conv_specs(S, D, tc)
    blk = pl.BlockSpec((S, tc), lambda j: (0, j))

    def body(ch_ref, cb_ref, cc_ref, w_ref, da_ref, dch_ref, dcb_ref, dcc_ref, gw_ref):
        row = lax.broadcasted_iota(jnp.int32, (S, tc), 0)
        ch, cb, cc, dav = [r[...].astype(F32) for r in (ch_ref, cb_ref, cc_ref, da_ref)]
        w = w_ref[...]
        u = cc * ch
        u1, u2 = _shift_down(u, 1, row), _shift_down(u, 2, row)
        cv = w[0:1, :] * u2 + w[1:2, :] * u1 + w[2:3, :] * u
        dcb_ref[...] = (dav * cv).astype(BF16)
        dcv = dav * cb
        gw_ref[0:1, :] = jnp.sum(dcv * u2, axis=0, keepdims=True)
        gw_ref[1:2, :] = jnp.sum(dcv * u1, axis=0, keepdims=True)
        gw_ref[2:3, :] = jnp.sum(dcv * u, axis=0, keepdims=True)
        du = w[2:3, :] * dcv + w[1:2, :] * _shift_up(dcv, 1, row) + w[0:1, :] * _shift_up(dcv, 2, row)
        dcc_ref[...] = (du * ch).astype(BF16)
        dch_ref[...] = (du * cc).astype(BF16)

    act = jax.ShapeDtypeStruct((S, D), BF16)
    return pl.pallas_call(
        body, grid=(D // tc,), name=name, in_specs=[col(0), col(1), col(2), wspec, blk],
        out_specs=[blk, blk, blk, wspec], out_shape=[act, act, act, jax.ShapeDtypeStruct((3, D), F32)],
        compiler_params=_params("parallel"),
    )(proj, proj, proj, conv_w, da)


SB_BQ_FWD = 512
SB_BQ_BWD = 256
SB_BK = 128
SB_GROUP = 4


def _sb_consts(bq):
    lane = lax.broadcasted_iota(jnp.int32, (bq, LANES), 1)
    r = lax.broadcasted_iota(jnp.int32, (SB_BK, SB_BK), 0)
    c = lax.broadcasted_iota(jnp.int32, (SB_BK, SB_BK), 1)
    tri_rev = jnp.where(r > c, 1.0, 0.0).astype(BF16)
    tri_fwd = jnp.where(r < c, 1.0, 0.0).astype(BF16)
    return lane, tri_rev, tri_fwd


def _cumsum2(v, tri):
    hi = v.astype(BF16)
    lo = (v - hi.astype(F32)).astype(BF16)
    part = (lax.dot_general(hi, tri, NN, preferred_element_type=F32)
            + lax.dot_general(lo, tri, NN, preferred_element_type=F32))
    return part, jnp.sum(v, axis=1, keepdims=True)


def _sb_logits(z, past):
    sp = jnp.log(1.0 + jnp.exp(-jnp.abs(z)))
    l = jnp.minimum(z, 0.0) - sp
    m = l - z
    if past is not None:
        m = jnp.where(past, m, 0.0)
    return l, m


def _stack_heads(v, lane):
    return jnp.concatenate([jnp.where(lane < SB_HEAD_DIM, v, 0.0), jnp.where(lane >= SB_HEAD_DIM, v, 0.0)],
                           axis=0).astype(BF16)


def _unstack_heads(v, lane):
    bq = v.shape[0] // 2
    return jnp.where(lane < SB_HEAD_DIM, v[:bq], v[bq:])


def _sb_positions(i, bq):
    r = lax.broadcasted_iota(jnp.int32, (2 * bq, SB_BK), 0)
    trow = i * bq + jnp.where(r >= bq, r - bq, r)
    return trow, lax.broadcasted_iota(jnp.int32, (2 * bq, SB_BK), 1)


def _sb_specs(S, D, bq):
    npair = D // LANES
    qspec = pl.BlockSpec((bq, LANES), lambda p, i: (i, 3 * npair + p))
    kspec = pl.BlockSpec((S, LANES), lambda p, i: (0, 4 * npair + p))
    vspec = pl.BlockSpec((S, LANES), lambda p, i: (0, 5 * npair + p))
    return npair, qspec, kspec, vspec


def sb_fwd(proj, D, name):
    S = proj.shape[0]
    bq = min(SB_BQ_FWD, S)
    nd = bq // SB_BK
    npair, qspec, kspec, vspec = _sb_specs(S, D, bq)
    scale = SB_HEAD_DIM ** -0.5

    def body(q_ref, k_ref, v_ref, o_ref, kb_ref, vb_ref):
        i = pl.program_id(1)

        @pl.when(i == 0)
        def _():
            kb_ref[...] = k_ref[...].astype(BF16)
            vb_ref[...] = v_ref[...].astype(BF16)

        lane, tri_rev, _ = _sb_consts(bq)
        qs = _stack_heads(q_ref[...].astype(F32) * scale, lane)
        trow, scol = _sb_positions(i, bq)

        def steps(j0, carry, n, masked):
            ks = [pl.multiple_of((j0 - t) * SB_BK, SB_BK) for t in range(n)]
            past = [(k + scol) < trow if masked else None for k in ks]
            zs = [lax.dot_general(qs, kb_ref[pl.ds(k, SB_BK), :], NT, preferred_element_type=F32) for k in ks]
            lm = [_sb_logits(z, p) for z, p in zip(zs, past)]
            cs = [_cumsum2(m, tri_rev) for _, m in lm]
            c, acc = carry
            for t in range(n):
                a = jnp.exp(lm[t][0] + (cs[t][0] + c))
                if masked:
                    a = jnp.where(past[t], a, 0.0)
                acc = acc + lax.dot_general(a.astype(BF16), vb_ref[pl.ds(ks[t], SB_BK), :], NN,
                                            preferred_element_type=F32)
                c = c + cs[t][1]
            return c, acc

        carry = (jnp.zeros((2 * bq, 1), F32), jnp.zeros((2 * bq, LANES), F32))
        carry = steps(i * nd + nd - 1, carry, nd, True)
        older = i * nd
        groups = older // SB_GROUP
        carry = lax.fori_loop(
            0, groups, lambda t, cr: steps(older - 1 - t * SB_GROUP, cr, SB_GROUP, False), carry)
        rest = older - groups * SB_GROUP
        carry = lax.fori_loop(0, rest // nd, lambda t, cr: steps(rest - 1 - t * nd, cr, nd, False), carry)
        o_ref[...] = _unstack_heads(carry[1], lane)

    return pl.pallas_call(
        body, grid=(npair, S // bq), name=name, in_specs=[qspec, kspec, vspec],
        out_specs=pl.BlockSpec((bq, LANES), lambda p, i: (i, p)),
        out_shape=jax.ShapeDtypeStruct((S, D), F32),
        scratch_shapes=[pltpu.VMEM((S, LANES), BF16), pltpu.VMEM((S, LANES), BF16)],
        compiler_params=_params("parallel", "arbitrary"),
    )(proj, proj, proj)


def sb_bwd(proj, do, D, name, dep=None):
    S = proj.shape[0]
    bq = min(SB_BQ_BWD, S)
    nd = bq // SB_BK
    nkb = S // SB_BK
    npair, qspec, kspec, vspec = _sb_specs(S, D, bq)
    scale = SB_HEAD_DIM ** -0.5

    def body(q_ref, k_ref, v_ref, do_ref, *refs):
        dq_ref, dk_ref, dv_ref, kb_ref, vb_ref, dk_acc, dv_acc, g_scr, b_scr, a_scr = refs[len(deps):]
        i = pl.program_id(1)

        @pl.when(i == 0)
        def _():
            kb_ref[...] = k_ref[...].astype(BF16)
            vb_ref[...] = v_ref[...].astype(BF16)
            dk_acc[...] = jnp.zeros_like(dk_acc)
            dv_acc[...] = jnp.zeros_like(dv_acc)

        lane, tri_rev, tri_fwd = _sb_consts(bq)
        qs = _stack_heads(q_ref[...].astype(F32) * scale, lane)
        dos = _stack_heads(do_ref[...].astype(F32), lane)
        qs_t, dos_t = qs.T, dos.T
        trow, scol = _sb_positions(i, bq)

        def sweep1(j0, c, n, masked):
            js = [j0 - t for t in range(n)]
            ks = [pl.multiple_of(j * SB_BK, SB_BK) for j in js]
            past = [(k + scol) < trow if masked else None for k in ks]
            zs = [lax.dot_general(qs, kb_ref[pl.ds(k, SB_BK), :], NT, preferred_element_type=F32) for k in ks]
            das = [lax.dot_general(dos, vb_ref[pl.ds(k, SB_BK), :], NT, preferred_element_type=F32) for k in ks]
            lm = [_sb_logits(z, p) for z, p in zip(zs, past)]
            cs = [_cumsum2(m, tri_rev) for _, m in lm]
            for t in range(n):
                b_scr[js[t]] = jnp.exp(lm[t][0]).astype(BF16)
            for t in range(n):
                a = jnp.exp(lm[t][0] + (cs[t][0] + c))
                if masked:
                    a = jnp.where(past[t], a, 0.0)
                g_scr[js[t]] = (das[t] * a).astype(BF16)
                a_scr[js[t]] = a.astype(BF16)
                c = c + cs[t][1]
            return c

        older = i * nd
        groups = older // SB_GROUP
        rest = older - groups * SB_GROUP
        c = jnp.zeros((2 * bq, 1), F32)
        c = sweep1(i * nd + nd - 1, c, nd, True)
        c = lax.fori_loop(0, groups, lambda t, cr: sweep1(older - 1 - t * SB_GROUP, cr, SB_GROUP, False), c)
        lax.fori_loop(0, rest // nd, lambda t, cr: sweep1(rest - 1 - t * nd, cr, nd, False), c)

        def sweep2(j0, carry, n, masked):
            js = [j0 + t for t in range(n)]
            ks = [pl.multiple_of(j * SB_BK, SB_BK) for j in js]
            g16 = [g_scr[j] for j in js]
            gv = [g.astype(F32) for g in g16]
            gs = [(lax.dot_general(g, tri_fwd, NN, preferred_element_type=F32), jnp.sum(v, axis=1, keepdims=True))
                  for g, v in zip(g16, gv)]
            pc, dq = carry
            dzs = []
            for t in range(n):
                dz = gv[t] - b_scr[js[t]].astype(F32) * (gv[t] + (gs[t][0] + pc))
                if masked:
                    dz = jnp.where((ks[t] + scol) < trow, dz, 0.0)
                dzs.append(dz.astype(BF16))
                pc = pc + gs[t][1]
            for t in range(n):
                dq = dq + lax.dot_general(dzs[t], kb_ref[pl.ds(ks[t], SB_BK), :], NN, preferred_element_type=F32)
                dk_acc[js[t]] += lax.dot_general(qs_t, dzs[t], NN, preferred_element_type=F32)
                dv_acc[js[t]] += lax.dot_general(dos_t, a_scr[js[t]], NN, preferred_element_type=F32)
            return pc, dq

        carry = (jnp.zeros((2 * bq, 1), F32), jnp.zeros((2 * bq, LANES), F32))
        carry = lax.fori_loop(0, groups, lambda t, cr: sweep2(t * SB_GROUP, cr, SB_GROUP, False), carry)
        carry = lax.fori_loop(
            0, rest // nd, lambda t, cr: sweep2(groups * SB_GROUP + t * nd, cr, nd, False), carry)
        carry = sweep2(i * nd, carry, nd, True)
        dq_ref[...] = (_unstack_heads(carry[1], lane) * scale).astype(BF16)

        @pl.when(i == pl.num_programs(1) - 1)
        def _():
            for j in range(nkb):
                dk_ref[j * SB_BK:(j + 1) * SB_BK, :] = dk_acc[j].T.astype(BF16)
                dv_ref[j * SB_BK:(j + 1) * SB_BK, :] = dv_acc[j].T.astype(BF16)

    deps = [] if dep is None else [dep]
    full = pl.BlockSpec((S, LANES), lambda p, i: (0, p))
    blk = pl.BlockSpec((bq, LANES), lambda p, i: (i, p))
    act = jax.ShapeDtypeStruct((S, D), BF16)
    return pl.pallas_call(
        body, grid=(npair, S // bq), name=name, in_specs=[qspec, kspec, vspec, blk] + [ANY] * len(deps),
        out_specs=[blk, full, full], out_shape=[act, act, act],
        scratch_shapes=[pltpu.VMEM((S, LANES), BF16), pltpu.VMEM((S, LANES), BF16),
                        pltpu.VMEM((nkb, LANES, SB_BK), F32), pltpu.VMEM((nkb, LANES, SB_BK), F32),
                        pltpu.VMEM((nkb, 2 * bq, SB_BK), BF16), pltpu.VMEM((nkb, 2 * bq, SB_BK), BF16),
                        pltpu.VMEM((nkb, 2 * bq, SB_BK), BF16)],
        compiler_params=_params("parallel", "arbitrary"),
    )(proj, proj, proj, do, *deps)


def _rms_rows(v):
    r = lax.rsqrt(jnp.mean(v * v, axis=-1, keepdims=True) + EPS)
    return v * r, r


def _xa_specs(S, D, M, tq):
    nh = D // X_HEAD_DIM
    qspec = pl.BlockSpec((tq, X_HEAD_DIM), lambda h, i: (i, 6 * nh + h))
    kspec = pl.BlockSpec((M, X_HEAD_DIM), lambda h, i: (0, h))
    vspec = pl.BlockSpec((M, X_HEAD_DIM), lambda h, i: (0, nh + h))
    gspec = pl.BlockSpec((1, X_HEAD_DIM), lambda h, i: (0, 0))
    return nh, qspec, kspec, vspec, gspec


def xa_fwd(proj, kv, gq, gk, D, name):
    S, M = proj.shape[0], kv.shape[0]
    tq = _tile(S, 2048)
    nh, qspec, kspec, vspec, gspec = _xa_specs(S, D, M, tq)
    scale = X_HEAD_DIM ** -0.5

    def body(q_ref, k_ref, v_ref, gq_ref, gk_ref, o_ref):
        qn = _rms_rows(q_ref[...].astype(F32))[0] * gq_ref[...]
        kn = _rms_rows(k_ref[...])[0] * gk_ref[...]
        s = _dot(qn, kn, NT) * scale
        e = jnp.exp(s - jnp.max(s, axis=-1, keepdims=True))
        p = e / jnp.sum(e, axis=-1, keepdims=True)
        o_ref[...] = _dot(p, v_ref[...], NN)

    return pl.pallas_call(
        body, grid=(nh, S // tq), name=name, in_specs=[qspec, kspec, vspec, gspec, gspec],
        out_specs=pl.BlockSpec((tq, X_HEAD_DIM), lambda h, i: (i, h)),
        out_shape=jax.ShapeDtypeStruct((S, D), F32), compiler_params=_params("parallel", "parallel"),
    )(proj, kv, kv, gq, gk)


def xa_bwd(proj, kv, gq, gk, do, D, name):
    S, M = proj.shape[0], kv.shape[0]
    tq = _tile(S, 2048)
    nh, qspec, kspec, vspec, gspec = _xa_specs(S, D, M, tq)
    scale = X_HEAD_DIM ** -0.5

    def body(q_ref, k_ref, v_ref, gq_ref, gk_ref, do_ref, dq_ref, dk_ref, dv_ref, ggq_ref, ggk_ref,
             dkn_acc, dv_acc):
        h, i = pl.program_id(0), pl.program_id(1)

        @pl.when((h == 0) & (i == 0))
        def _():
            ggq_ref[...] = jnp.zeros_like(ggq_ref)
            ggk_ref[...] = jnp.zeros_like(ggk_ref)

        @pl.when(i == 0)
        def _():
            dkn_acc[...] = jnp.zeros_like(dkn_acc)
            dv_acc[...] = jnp.zeros_like(dv_acc)

        gq, gk = gq_ref[...], gk_ref[...]
        qhat, rq = _rms_rows(q_ref[...].astype(F32))
        khat, rk = _rms_rows(k_ref[...])
        qn, kn = qhat * gq, khat * gk
        s = _dot(qn, kn, NT) * scale
        e = jnp.exp(s - jnp.max(s, axis=-1, keepdims=True))
        p = e / jnp.sum(e, axis=-1, keepdims=True)
        dov = do_ref[...]
        dv_acc[...] += _dot(p, dov, TN)
        dp = _dot(dov, v_ref[...], NT)
        ds = (p * (dp - jnp.sum(dp * p, axis=-1, keepdims=True))) * scale
        dqn = _dot(ds, kn, NN)
        dkn_acc[...] += _dot(ds, qn, TN)
        ggq_ref[...] += jnp.sum(dqn * qhat, axis=0, keepdims=True)
        dqh = dqn * gq
        dq_ref[...] = (rq * (dqh - qhat * jnp.mean(dqh * qhat, axis=-1, keepdims=True))).astype(BF16)

        @pl.when(i == pl.num_programs(1) - 1)
        def _():
            dkn = dkn_acc[...]
            ggk_ref[...] += jnp.sum(dkn * khat, axis=0, keepdims=True)
            dkh = dkn * gk
            dk_ref[...] = (rk * (dkh - khat * jnp.mean(dkh * khat, axis=-1, keepdims=True))).astype(BF16)
            dv_ref[...] = dv_acc[...].astype(BF16)

    blk = pl.BlockSpec((tq, X_HEAD_DIM), lambda h, i: (i, h))
    kv_shape = jax.ShapeDtypeStruct((M, 2 * D), BF16)
    gshape = jax.ShapeDtypeStruct((1, X_HEAD_DIM), F32)
    dq, dk, dv, ggq, ggk = pl.pallas_call(
        body, grid=(nh, S // tq), name=name, in_specs=[qspec, kspec, vspec, gspec, gspec, blk],
        out_specs=[blk, kspec, vspec, gspec, gspec],
        out_shape=[jax.ShapeDtypeStruct((S, D), BF16), kv_shape, kv_shape, gshape, gshape],
        scratch_shapes=[pltpu.VMEM((M, X_HEAD_DIM), F32), pltpu.VMEM((M, X_HEAD_DIM), F32)],
        compiler_params=_params("arbitrary", "arbitrary"),
    )(proj, kv, kv, gq, gk, do)
    d_kv = jnp.concatenate([dk[:, :D], dv[:, D:]], axis=1)
    return dq, d_kv, ggq, ggk


def _gate_specs(S, D, tm):
    row = pl.BlockSpec((tm, D), lambda i: (i, 0))
    gate = lambda b: pl.BlockSpec((tm, D), lambda i, b=b: (i, 7 + b))
    return row, gate


def merge_fwd(proj, ys, D, name):
    S = proj.shape[0]
    tm = _tile(S, 256)
    row, gate = _gate_specs(S, D, tm)

    def body(g0, g1, g2, y0, y1, y2, o_ref):
        acc = jax.nn.sigmoid(g0[...].astype(F32)) * y0[...].astype(F32)
        acc = acc + jax.nn.sigmoid(g1[...].astype(F32)) * y1[...].astype(F32)
        acc = acc + jax.nn.sigmoid(g2[...].astype(F32)) * y2[...].astype(F32)
        o_ref[...] = acc.astype(BF16)

    return pl.pallas_call(
        body, grid=(S // tm,), name=name, in_specs=[gate(0), gate(1), gate(2), row, row, row],
        out_specs=row, out_shape=jax.ShapeDtypeStruct((S, D), BF16), compiler_params=_params("parallel"),
    )(proj, proj, proj, *ys)


def merge_bwd(proj, ys, dm, D, name):
    S = proj.shape[0]
    tm = _tile(S, 256)
    row, gate = _gate_specs(S, D, tm)

    def body(g0, g1, g2, y0, y1, y2, dm_ref, d0, d1, d2, dg_ref):
        dmv = dm_ref[...]
        for b, (g_ref, y_ref, d_ref) in enumerate(((g0, y0, d0), (g1, y1, d1), (g2, y2, d2))):
            s = jax.nn.sigmoid(g_ref[...].astype(F32))
            d_ref[...] = (dmv * s).astype(BF16)
            dg_ref[:, b * D:(b + 1) * D] = ((dmv * y_ref[...].astype(F32)) * (s * (1.0 - s))).astype(BF16)

    act = jax.ShapeDtypeStruct((S, D), BF16)
    return pl.pallas_call(
        body, grid=(S // tm,), name=name, in_specs=[gate(0), gate(1), gate(2), row, row, row, row],
        out_specs=[row, row, row, pl.BlockSpec((tm, 3 * D), lambda i: (i, 0))],
        out_shape=[act, act, act, jax.ShapeDtypeStruct((S, 3 * D), BF16)], compiler_params=_params("parallel"),
    )(proj, proj, proj, *ys, dm)


def loss_head(y, target, name):
    S, D = y.shape
    tm = _tile(S, 512)

    def body(y_ref, t_ref, dy_ref, dy16_ref, l_ref):
        @pl.when(pl.program_id(0) == 0)
        def _():
            l_ref[...] = jnp.zeros_like(l_ref)

        e = y_ref[...] - t_ref[...]
        dy = e * (1.0 / D)
        dy_ref[...] = dy
        dy16_ref[...] = dy.astype(BF16)
        l_ref[...] += jnp.sum(e * e, axis=0, keepdims=True)

    row = pl.BlockSpec((tm, D), lambda i: (i, 0))
    vec = pl.BlockSpec((1, D), lambda i: (0, 0))
    return pl.pallas_call(
        body, grid=(S // tm,), name=name, in_specs=[row, row], out_specs=[row, row, vec],
        out_shape=[jax.ShapeDtypeStruct((S, D), F32), jax.ShapeDtypeStruct((S, D), BF16),
                   jax.ShapeDtypeStruct((1, D), F32)],
        compiler_params=_params("arbitrary"),
    )(y, target)


def _rows2d(a):
    return a.reshape(-1, a.shape[-1])


def _ew_call(fn, ins, out_dtypes, name):
    R, C = ins[0].shape
    tr = _tile(R, max(8, (1 << 19) // C))
    spec = pl.BlockSpec((tr, C), lambda i: (i, 0))

    def body(*refs):
        outs = fn(*[r[...] for r in refs[:len(ins)]])
        for o_ref, o in zip(refs[len(ins):], outs):
            o_ref[...] = o.astype(o_ref.dtype)

    return pl.pallas_call(
        body, grid=(R // tr,), name=name, in_specs=[spec] * len(ins), out_specs=[spec] * len(out_dtypes),
        out_shape=[jax.ShapeDtypeStruct((R, C), d) for d in out_dtypes], compiler_params=_params("parallel"),
    )(*ins)


def adamw(w, g, m, v, name):
    def fn(w, g, m, v):
        m = ADAM_B1 * m + (1.0 - ADAM_B1) * g
        v = ADAM_B2 * v + (1.0 - ADAM_B2) * (g * g)
        m_hat = m / (1.0 - ADAM_B1 ** ADAM_STEP)
        v_hat = v / (1.0 - ADAM_B2 ** ADAM_STEP)
        return -ADAM_LR * (m_hat / (jnp.sqrt(v_hat) + ADAM_EPS) + ADAM_WD * w), m, v

    shp = w.shape
    outs = _ew_call(fn, [_rows2d(a) for a in (w, g, m, v)], [F32, F32, F32], name)
    return [o.reshape(shp) for o in outs]


def _placed_call(fn, place, grid, ins, in_specs, out_shape, out_specs, name, dep=None):
    n = len(ins)
    deps = [] if dep is None else [dep]

    def body(place_ref, *refs):
        outs = fn(*[r[...] for r in refs[:n]])
        for o_ref, o in zip(refs[n + len(deps):], outs):
            o_ref[...] = o.astype(o_ref.dtype)

    return pl.pallas_call(
        body, name=name, out_shape=out_shape,
        grid_spec=pltpu.PrefetchScalarGridSpec(
            num_scalar_prefetch=1, grid=grid, in_specs=list(in_specs) + [ANY] * len(deps), out_specs=out_specs),
        compiler_params=_params(*["parallel"] * len(grid)),
    )(place, *ins, *deps)


def _row_tile(R, C):
    return _tile(R, max(16, (1 << 19) // C))


def cast_into_full(w, place, name, dep=None):
    R, C = w.shape
    tr = _row_tile(R, C)
    return _placed_call(
        lambda a: (a,), place, (R // tr,), [w], [pl.BlockSpec((tr, C), lambda i, p: (i, 0))],
        [jax.ShapeDtypeStruct((N_CHIP, R, C), BF16)], [pl.BlockSpec((None, tr, C), lambda i, p: (p[0], i, 0))],
        name, dep=dep)[0]


def pair_sum(g4, got, place, name):
    _, hr, C = got.shape
    tr = _row_tile(hr, C)
    nb = hr // tr
    blk = pl.BlockSpec((None, tr, C), lambda s, i, p: (s, i, 0))
    return _placed_call(
        lambda a, b: (a + b,), place, (N_CHIP, nb), [g4, got],
        [pl.BlockSpec((None, tr, C), lambda s, i, p: (s, p[1] * nb + i, 0)), blk],
        [jax.ShapeDtypeStruct(got.shape, BF16)], [blk], name)[0]


def chip_sum(p32, got, place, name, dep=None):
    _, H, C = p32.shape
    tr = _row_tile(H, C)
    nb = H // tr
    peer = lambda j: pl.BlockSpec((None, tr, C), lambda i, p, j=j: (j, i, 0))
    return _placed_call(
        lambda a, b, c, d: (((a.astype(F32) + b.astype(F32)) + c.astype(F32)) + d.astype(F32),), place, (nb,),
        [p32, got, got, got], [pl.BlockSpec((None, tr, C), lambda i, p: (p[0], i, 0)), peer(0), peer(1), peer(2)],
        [jax.ShapeDtypeStruct((2 * H, C), F32)], [pl.BlockSpec((tr, C), lambda i, p: (p[1] * nb + i, 0))],
        name, dep=dep)[0]


ANY = pl.BlockSpec(memory_space=pl.ANY)
CHIP_FLIPS = ((1, 0), (0, 1), (1, 1))


def _place():
    return lax.axis_index("x"), lax.axis_index("y"), lax.axis_index("c")


def _flip(v, f):
    return 1 - v if f else v


def join_halves(fulls, name):
    n = len(fulls)

    def body(*refs):
        outs = refs[n:2 * n]
        send_sem, recv_sem = refs[2 * n:]
        x, y, c = _place()
        copies = []
        for a in range(n):
            hr = outs[a].shape[0] // 2
            half = outs[a].at[pl.ds(c * hr, hr), :]
            cp = pltpu.make_async_remote_copy(
                src_ref=half, dst_ref=half, send_sem=send_sem.at[a], recv_sem=recv_sem.at[a],
                device_id=(x, y, 1 - c), device_id_type=MESH)
            cp.start()
            copies.append(cp)
        for a, cp in enumerate(copies):
            hr = outs[a].shape[0] // 2
            theirs = outs[a].at[pl.ds((1 - c) * hr, hr), :]
            cp.wait_send()
            pltpu.make_async_remote_copy(
                src_ref=theirs, dst_ref=theirs, send_sem=send_sem.at[a], recv_sem=recv_sem.at[a],
                device_id=(x, y, 1 - c), device_id_type=MESH).wait_recv()

    dma = pltpu.SemaphoreType.DMA
    return pl.pallas_call(
        body, name=name, in_specs=[ANY] * n, out_specs=[ANY] * n,
        out_shape=[jax.ShapeDtypeStruct(f.shape, F32) for f in fulls],
        input_output_aliases={a: a for a in range(n)},
        scratch_shapes=[dma((n,)), dma((n,))],
    )(*fulls)


HBM = pl.BlockSpec(memory_space=pltpu.HBM)
SEM = pl.BlockSpec(memory_space=pltpu.SEMAPHORE)
EFFECT = pltpu.SideEffectType.DATAFLOW_SIDE_EFFECTING


def _in_hbm(a):
    return pltpu.with_memory_space_constraint(a, pltpu.HBM)


def _gather_half(ref, chip_idx, core):
    hr = ref.shape[1] // 2
    return ref.at[chip_idx, pl.ds(core * hr, hr), :]


def gather_forward(fulls, small, name):
    n = len(fulls)

    def body(*refs):
        small_in = refs[n]
        outs, small_out = refs[n + 1:2 * n + 1], refs[2 * n + 1]
        send_sem, recv_sem, sm_send, sm_recv, loc_sem = refs[2 * n + 2:]
        x, y, c = _place()
        mine = 2 * x + y
        chips = [(_flip(x, fx), _flip(y, fy)) for fx, fy in CHIP_FLIPS]
        local = pltpu.make_async_copy(small_in, small_out.at[mine], loc_sem)
        local.start()
        copies = []
        for j, (px, py) in enumerate(chips):
            cp = pltpu.make_async_remote_copy(
                src_ref=small_in, dst_ref=small_out.at[mine], send_sem=sm_send.at[j], recv_sem=sm_recv.at[j],
                device_id=(px, py, c), device_id_type=MESH)
            cp.start()
            copies.append(cp)
        for a in range(n):
            for j, (px, py) in enumerate(chips):
                src = _gather_half(outs[a], 2 * px + py, c)
                cp = pltpu.make_async_remote_copy(
                    src_ref=src, dst_ref=src, send_sem=send_sem.at[3 * a + j], recv_sem=recv_sem.at[3 * a + j],
                    device_id=(x, y, 1 - c), device_id_type=MESH)
                cp.start()
                copies.append(cp)
        for a in range(n):
            for j, (px, py) in enumerate(chips):
                dst = _gather_half(outs[a], 2 * px + py, 1 - c)
                pltpu.make_async_remote_copy(
                    src_ref=dst, dst_ref=dst, send_sem=send_sem.at[3 * a + j], recv_sem=recv_sem.at[3 * a + j],
                    device_id=(x, y, 1 - c), device_id_type=MESH).wait_recv()
        for j, (px, py) in enumerate(chips):
            dst = small_out.at[2 * px + py]
            pltpu.make_async_remote_copy(
                src_ref=dst, dst_ref=dst, send_sem=sm_send.at[j], recv_sem=sm_recv.at[j],
                device_id=(px, py, c), device_id_type=MESH).wait_recv()
        for cp in copies:
            cp.wait_send()
        local.wait()

    dma = pltpu.SemaphoreType.DMA
    out = pl.pallas_call(
        body, name=name, in_specs=[ANY] * (n + 1), out_specs=[ANY] * (n + 1),
        out_shape=[jax.ShapeDtypeStruct(f.shape, f.dtype) for f in fulls]
        + [jax.ShapeDtypeStruct((N_CHIP,) + small.shape, small.dtype)],
        input_output_aliases={a: a for a in range(n)},
        scratch_shapes=[dma((3 * n,)), dma((3 * n,)), dma((3,)), dma((3,)), dma],
    )(*fulls, small)
    return out[:n], out[n]


def _gather_plan(fulls, lands):
    x, y, c = _place()
    mine = 2 * x + y
    return [(_gather_half(f, mine, c), _gather_half(f, mine, c), (_flip(x, fx), _flip(y, fy), c))
            for f in fulls for fx, fy in CHIP_FLIPS]


def _scatter_plan(parts, lands):
    x, y, c = _place()
    plan = []
    for p, l in zip(parts, lands):
        for j, (fx, fy) in enumerate(CHIP_FLIPS):
            px, py = _flip(x, fx), _flip(y, fy)
            plan.append((p.at[2 * px + py], l.at[j], (px, py, c)))
    return plan


def _scatter_lands(parts):
    return [(3,) + p.shape[1:] for p in parts]


def _exchange_plan(grads, lands):
    x, y, c = _place()
    plan = []
    for g, l in zip(grads, lands):
        hr = g.shape[1] // 2
        plan.append((g.at[:, pl.ds((1 - c) * hr, hr), :], l, (x, y, 1 - c)))
    return plan


def _exchange_lands(grads):
    return [(N_CHIP, g.shape[1] // 2, g.shape[2]) for g in grads]


def _forward_plan(fulls, lands):
    x, y, c = _place()
    return [(_gather_half(f, 2 * _flip(x, fx) + _flip(y, fy), c), _gather_half(f, 2 * _flip(x, fx) + _flip(y, fy), c),
             (x, y, 1 - c)) for f in fulls for fx, fy in CHIP_FLIPS]


def split_start(plan, copies, srcs, land_shapes, deps, name):
    n, m = len(srcs), len(land_shapes)
    lands = [lax.empty(s, srcs[0].dtype) for s in land_shapes]

    def body(*refs):
        k0 = n + m + len(deps)
        send_sem, recv_sem = refs[k0], refs[k0 + 1]
        thru, token = refs[k0 + 2:k0 + 2 + n + m], refs[k0 + 2 + n + m]
        for k, (src, dst, dev) in enumerate(plan(thru[:n], thru[n:])):
            pltpu.make_async_remote_copy(src_ref=src, dst_ref=dst, send_sem=send_sem.at[k], recv_sem=recv_sem.at[k],
                                         device_id=dev, device_id_type=MESH).start()
        token[...] = jnp.zeros_like(token)

    dma = pltpu.SemaphoreType.DMA
    arrays = list(srcs) + lands
    out = pl.pallas_call(
        body, name=name,
        out_shape=(dma((copies,)), dma((copies,)), *[pltpu.HBM(a.shape, a.dtype) for a in arrays],
                   jax.ShapeDtypeStruct((8, LANES), F32)),
        in_specs=[HBM] * (n + m) + [ANY] * len(deps),
        out_specs=(SEM, SEM, *[HBM] * (n + m), pl.BlockSpec(memory_space=pltpu.VMEM)),
        input_output_aliases={a: 2 + a for a in range(n + m)},
        compiler_params=pltpu.CompilerParams(has_side_effects=EFFECT),
    )(*[_in_hbm(a) for a in arrays], *deps)
    return (out[0], out[1], list(out[2:2 + n]), list(out[2 + n:2 + n + m])), out[2 + n + m]


def split_wait(plan, handle, after, name):
    send_sem, recv_sem, srcs, lands = handle
    n, m = len(srcs), len(lands)

    def body(*refs):
        send_sem, recv_sem = refs[n + m], refs[n + m + 1]
        thru = refs[n + m + 2 + len(after):]
        for k, (src, dst, dev) in enumerate(plan(thru[:n], thru[n:])):
            cp = pltpu.make_async_remote_copy(src_ref=src, dst_ref=dst, send_sem=send_sem.at[k],
                                              recv_sem=recv_sem.at[k], device_id=dev, device_id_type=MESH)
            cp.wait_send()
            cp.wait_recv()

    arrays = list(srcs) + list(lands)
    out = pl.pallas_call(
        body, name=name, out_shape=tuple(pltpu.HBM(a.shape, a.dtype) for a in arrays),
        in_specs=[HBM] * (n + m) + [SEM, SEM] + [ANY] * len(after), out_specs=tuple([HBM] * (n + m)),
        input_output_aliases={a: a for a in range(n + m)},
        compiler_params=pltpu.CompilerParams(has_side_effects=EFFECT),
    )(*arrays, send_sem, recv_sem, *after)
    return list(out[:n]), list(out[n:])


def allreduce_small(block, name, dep):
    R, C = block.shape

    def body(in_ref, dep_ref, out_ref, slots, send_sem, recv_sem):
        x, y, c = _place()
        me = 4 * x + 2 * y + c
        slots[me] = in_ref[...]
        copies = []
        for r in range(1, 8):
            fx, fy, fc = (r >> 2) & 1, (r >> 1) & 1, r & 1
            cp = pltpu.make_async_remote_copy(
                src_ref=in_ref, dst_ref=slots.at[me], send_sem=send_sem.at[r - 1], recv_sem=recv_sem.at[r - 1],
                device_id=(_flip(x, fx), _flip(y, fy), _flip(c, fc)), device_id_type=MESH)
            cp.start()
            copies.append(cp)
        for cp in copies:
            cp.wait()
        acc = slots[0]
        for d in range(1, 8):
            acc = acc + slots[d]
        out_ref[...] = acc

    vm = pl.BlockSpec(memory_space=pltpu.VMEM)
    dma = pltpu.SemaphoreType.DMA
    return pl.pallas_call(
        body, name=name, in_specs=[vm, ANY], out_specs=vm, out_shape=jax.ShapeDtypeStruct((R, C), F32),
        scratch_shapes=[pltpu.VMEM((8, R, C), F32), dma((7,)), dma((7,))],
    )(block, dep)


def local_step(x, mem, target, g_mix, g_mem, q_norm_g, k_norm_g, g_mlp, conv_w, h, mem_n, w_in, w_in_dep,
               rest_weights, early_grads, mid_grads, late_grads, last_grads):
    S, D = x.shape
    proj = mm_nn_shard(h, w_in, "proj", dep=w_in_dep, out_dtype=BF16)
    a_conv = conv_fwd(proj, conv_w, D, "conv_fwd")
    o_sb = sb_fwd(proj, D, "sb_fwd")
    w_conv_out, w_sb_out, w_mem_kv, w_x_out, w_out, mlp_dep, mlp_weights = rest_weights(o_sb)
    kv = mm_nn_shard(mem_n, w_mem_kv, "kv", dep=mlp_dep)
    o_x = xa_fwd(proj, kv, q_norm_g, k_norm_g, D, "xa_fwd")
    ys = [mm_nn(a_conv, w_conv_out, "y_conv", out_dtype=BF16), mm_nn(o_sb, w_sb_out, "y_sb", out_dtype=BF16),
          mm_nn(o_x, w_x_out, "y_x", out_dtype=BF16)]
    merged = merge_fwd(proj, ys, D, "merge_fwd")
    x1 = mm_nn(merged, w_out, "x1", res=x)
    h2 = rms_fwd(x1, g_mlp, "rms_mlp")
    w_up, w_down = mlp_weights(h2)
    up, act = mm_nn_shard(h2, w_up, "up", relu2=True)
    x2 = mm_nn(act, w_down, "x2", res=x1)
    dy, dy16, loss_cols = loss_head(x2, target, "loss_head")
    d_up = mm_nt(dy16, w_down, "d_up", up=up, out_dtype=BF16)
    g = {"w_down": mm_tn(act, dy16, "g_w_down")}
    g["w_up"] = mm_tn(h2, d_up, "g_w_up", shard_out=True)
    dh2 = mm_nt_shard(d_up, w_up, "dh2")
    dx1, dx1_16, g["g_mlp"] = rms_bwd(x1, g_mlp, dh2, "rms_mlp_bwd", dres=dy, want16=True)
    g["w_out"] = mm_tn(merged, dx1_16, "g_w_out")
    dm = mm_nt(dx1_16, w_out, "d_merged")
    dy_c, dy_s, dy_x, d_gate = merge_bwd(proj, ys, dm, D, "merge_bwd")
    g["w_conv_out"] = mm_tn(a_conv, dy_c, "g_w_conv_out")
    g["w_sb_out"] = mm_tn(o_sb, dy_s, "g_w_sb_out")
    g["w_x_out"] = mm_tn(o_x, dy_x, "g_w_x_out")
    d_xq, d_kv, g["q_norm_g"], g["k_norm_g"] = xa_bwd(
        proj, kv, q_norm_g, k_norm_g, mm_nt(dy_x, w_x_out, "d_o_x", out_dtype=BF16), D, "xa_bwd")
    g["w_mem_kv"] = mm_tn(mem_n, d_kv, "g_w_mem_kv", shard_out=True)
    g["g_mem"] = rms_bwd(mem, g_mem, mm_nt_shard(d_kv, w_mem_kv, "d_mem_n"), "rms_mem_bwd", want_dx=False)
    dep = early_grads(g)
    d_a_conv = mm_nt(dy_c, w_conv_out, "d_a_conv", dep=dep, out_dtype=BF16)
    d_ch, d_cb, d_cc, g["conv_w"] = conv_bwd(proj, conv_w, d_a_conv, D, "conv_bwd")
    d_o_sb = mm_nt(dy_s, w_sb_out, "d_o_sb", dep=dep, out_dtype=BF16)
    dq, dk, dv = sb_bwd(proj, d_o_sb, D, "sb_bwd", dep=mid_grads([d_ch, d_o_sb]))
    d_proj = jnp.concatenate([d_ch, d_cb, d_cc, dq, dk, dv, d_xq, d_gate], axis=1)
    g["w_in"] = mm_tn(h, d_proj, "g_w_in", shard_out=True)
    dh = mm_nt_shard(d_proj, w_in, "dh", dep=late_grads(g["w_in"]))
    grad_x, g["g_mix"] = rms_bwd(x, g_mix, dh, "rms_mix_bwd", dres=dx1, dep=last_grads(dh))
    return loss_cols, grad_x, g


BIG = ("w_in", "w_conv_out", "w_sb_out", "w_mem_kv", "w_x_out", "w_out", "w_up", "w_down")
REST = BIG[1:]
COL_SHARDED = ("w_in", "w_mem_kv", "w_up")
WEIGHTS = ("g_mix", "g_mem", "w_in", "conv_w", "w_conv_out", "w_sb_out", "q_norm_g", "k_norm_g",
           "w_mem_kv", "w_x_out", "w_out", "g_mlp", "w_up", "w_down")


def _pack_small(D, g_mix, g_mem, g_mlp, q_norm_g, k_norm_g, conv_w, last):
    qk = jnp.concatenate([q_norm_g, k_norm_g, jnp.zeros((1, D - 2 * X_HEAD_DIM), F32)], axis=1)
    cw = jnp.pad(conv_w, ((0, 0), (0, D - conv_w.shape[1])))
    return jnp.concatenate([g_mix, g_mem, g_mlp, qk, cw, last], axis=0)


def kernel(x, mem, g_mix, g_mem, w_in, conv_w, w_conv_out, w_sb_out, q_norm_g, k_norm_g, w_mem_kv, w_x_out, w_out, g_mlp, w_up, w_down, loss_target, m_g_mix, m_g_mem, m_w_in, m_conv_w, m_w_conv_out, m_w_sb_out, m_q_norm_g, m_k_norm_g, m_w_mem_kv, m_w_x_out, m_w_out, m_g_mlp, m_w_up, m_w_down, v_g_mix, v_g_mem, v_w_in, v_conv_w, v_w_conv_out, v_w_sb_out, v_q_norm_g, v_k_norm_g, v_w_mem_kv, v_w_x_out, v_w_out, v_g_mlp, v_w_up, v_w_down):
    S, D = x.shape[1], x.shape[2]
    w = dict(g_mix=g_mix, g_mem=g_mem, w_in=w_in, conv_w=conv_w, w_conv_out=w_conv_out, w_sb_out=w_sb_out,
             q_norm_g=q_norm_g, k_norm_g=k_norm_g, w_mem_kv=w_mem_kv, w_x_out=w_x_out, w_out=w_out,
             g_mlp=g_mlp, w_up=w_up, w_down=w_down)
    m = dict(g_mix=m_g_mix, g_mem=m_g_mem, w_in=m_w_in, conv_w=m_conv_w, w_conv_out=m_w_conv_out,
             w_sb_out=m_w_sb_out, q_norm_g=m_q_norm_g, k_norm_g=m_k_norm_g, w_mem_kv=m_w_mem_kv,
             w_x_out=m_w_x_out, w_out=m_w_out, g_mlp=m_g_mlp, w_up=m_w_up, w_down=m_w_down)
    v = dict(g_mix=v_g_mix, g_mem=v_g_mem, w_in=v_w_in, conv_w=v_conv_w, w_conv_out=v_w_conv_out,
             w_sb_out=v_w_sb_out, q_norm_g=v_q_norm_g, k_norm_g=v_k_norm_g, w_mem_kv=v_w_mem_kv,
             w_x_out=v_w_x_out, w_out=v_w_out, g_mlp=v_g_mlp, w_up=v_w_up, w_down=v_w_down)
    chip = 2 * lax.axis_index("x") + lax.axis_index("y")
    cs = conv_w.shape[2]

    place = jnp.stack([chip, lax.axis_index("c")]).astype(jnp.int32)
    cw_block = jnp.pad(conv_w[0], ((0, 5), (0, 0)))
    handle, token = split_start(_gather_plan, 3, [cast_into_full(w["w_in"][0], place, "cast_w_in")], [], [],
                                "gather_w_in_start")
    rest16 = [cast_into_full(w[k][0], place, "cast_" + k, dep=token) for k in REST]
    h = rms_fwd(x[0], g_mix, "rms_mix", dep=token)
    mem_n = rms_fwd(mem[0], g_mem, "rms_mem", dep=token)
    landed, _ = split_wait(_gather_plan, handle, [*rest16, h, mem_n], "gather_w_in_wait")
    (w_in_full,), cw_all = gather_forward(landed, cw_block, "gather_w_in_forward")
    conv_full = jnp.concatenate([cw_all[p, :3] for p in range(N_CHIP)], axis=1)
    rest_handle, rest_token = split_start(_gather_plan, 3 * len(REST), rest16, [], [w_in_full], "gather_rest_start")

    def layout(k, a):
        return a if k in COL_SHARDED else a.reshape(-1, a.shape[-1])

    def rest_weights(after):
        landed, _ = split_wait(_gather_plan, rest_handle, [after], "gather_rest_wait")
        first = gather_forward(landed[:-2], cw_block, "gather_rest_forward")[0]
        mlp_handle, token = split_start(_forward_plan, 6, landed[-2:], [], [first[0]], "forward_mlp_start")

        def mlp_weights(after):
            both, _ = split_wait(_forward_plan, mlp_handle, [after], "forward_mlp_wait")
            return [layout(k, a) for k, a in zip(REST[-2:], both)]

        return [layout(k, a) for k, a in zip(REST[:-2], first)] + [token, mlp_weights]

    def blocks(k, a):
        return a if k in COL_SHARDED else a.reshape(N_CHIP, -1, a.shape[-1])

    early = {}

    def early_grads(g):
        early["g4"] = [blocks(k, g[k]) for k in REST]
        early["swap"], token = split_start(_exchange_plan, len(REST), early["g4"], _exchange_lands(early["g4"]),
                                           [g["g_mem"]], "exchange_rest_start")
        return token

    def mid_grads(after):
        g4, got = split_wait(_exchange_plan, early["swap"], after, "exchange_rest_wait")
        p16 = [pair_sum(a, b, place, "pair_sum_" + k) for k, a, b in zip(REST, g4, got)]
        early["fly"], token = split_start(_scatter_plan, 3 * len(REST), p16, _scatter_lands(p16), [],
                                          "scatter_rest_start")
        return token

    late = {}

    def late_grads(gw):
        late["swap"], token = split_start(_exchange_plan, 1, [gw], _exchange_lands([gw]), [], "exchange_w_in_start")
        return token

    def last_grads(after):
        (gw,), (got,) = split_wait(_exchange_plan, late["swap"], [after], "exchange_w_in_wait")
        p16 = [pair_sum(gw, got, place, "pair_sum_w_in")]
        late["fly"], token = split_start(_scatter_plan, 3, p16, _scatter_lands(p16), [], "scatter_w_in_start")
        return token

    loss_cols, grad_x, g = local_step(
        x[0], mem[0], loss_target[0], g_mix, g_mem, q_norm_g, k_norm_g, g_mlp, conv_full, h, mem_n,
        w_in_full, rest_token, rest_weights, early_grads, mid_grads, late_grads, last_grads)
    token = g["g_mix"]

    p16_rest, got_rest = split_wait(_scatter_plan, early["fly"], [token], "scatter_rest_wait")
    gsum, delta, new_m, new_v = {}, {}, {}, {}
    halves = [chip_sum(p, b, place, "chip_sum_" + k, dep=token) for k, p, b in zip(REST, p16_rest, got_rest)]
    for k, a in zip(REST, join_halves(halves, "join_halves_rest")):
        gsum[k] = a[None]
        delta[k], new_m[k], new_v[k] = adamw(w[k], gsum[k], m[k], v[k], "adamw_" + k)
    p16_in, got_in = split_wait(_scatter_plan, late["fly"], [new_v[k] for k in REST], "scatter_w_in_wait")

    small = allreduce_small(
        _pack_small(D, g["g_mix"], g["g_mem"], g["g_mlp"], g["q_norm_g"], g["k_norm_g"], g["conv_w"], loss_cols),
        "allreduce_small", dep=got_in[0])
    loss = (0.5 / D) * jnp.sum(small[7])
    gsum.update({"g_mix": small[0:1], "g_mem": small[1:2], "g_mlp": small[2:3],
                 "q_norm_g": small[3:4, :X_HEAD_DIM], "k_norm_g": small[3:4, X_HEAD_DIM:2 * X_HEAD_DIM],
                 "conv_w": lax.dynamic_slice(small[4:7], (0, chip * cs), (3, cs))[None]})
    half_in = chip_sum(p16_in[0], got_in[0], place, "chip_sum_w_in")
    gsum["w_in"] = join_halves([half_in], "join_halves_w_in")[0][None]
    delta["w_in"], new_m["w_in"], new_v["w_in"] = adamw(w["w_in"], gsum["w_in"], m["w_in"], v["w_in"], "adamw_w_in")
    small_names = ("g_mix", "g_mem", "g_mlp", "q_norm_g", "k_norm_g", "conv_w")
    zero_row = jnp.zeros((1, D), F32)
    packed = [_pack_small(D, *[t[k] if k != "conv_w" else t[k][0] for k in small_names], zero_row)
              for t in (w, gsum, m, v)]
    sm = adamw(*packed, "adamw_small")
    for t, block in zip((delta, new_m, new_v), sm):
        t["g_mix"], t["g_mem"], t["g_mlp"] = block[0:1], block[1:2], block[2:3]
        t["q_norm_g"], t["k_norm_g"] = block[3:4, :X_HEAD_DIM], block[3:4, X_HEAD_DIM:2 * X_HEAD_DIM]
        t["conv_w"] = block[4:7, :cs][None]

    return (loss, grad_x[None], *[gsum[k] for k in WEIGHTS], *[delta[k] for k in WEIGHTS],
            *[new_m[k] for k in WEIGHTS], *[new_v[k] for k in WEIGHTS])
```

```python
import functools

import jax
import jax.numpy as jnp
from jax import lax
from jax.experimental import pallas as pl
from jax.experimental.pallas import tpu as pltpu

F32 = jnp.float32
BF16 = jnp.bfloat16
EPS = 1e-6
N_CHIP = 4
SB_HEAD_DIM = 64
X_HEAD_DIM = 256
LANES = 128
VMEM_LIMIT = 56 * 1024 * 1024
ADAM_LR, ADAM_B1, ADAM_B2, ADAM_EPS, ADAM_WD, ADAM_STEP = 0.001, 0.9, 0.999, 1e-8, 0.01, 10
MESH = pl.DeviceIdType.MESH


def _params(*sem):
    return pltpu.CompilerParams(dimension_semantics=sem, vmem_limit_bytes=VMEM_LIMIT)


def _tile(n, pref):
    if n <= pref:
        return n
    t = 1 << (pref.bit_length() - 1)
    while n % t:
        t //= 2
    return t


NN = (((1,), (0,)), ((), ()))
NT = (((1,), (1,)), ((), ()))
TN = (((0,), (0,)), ((), ()))


def _dot(a, b, dims):
    return lax.dot_general(a.astype(BF16), b.astype(BF16), dims, preferred_element_type=F32)


def mm_nn_shard(a, g, name, relu2=False, dep=None, out_dtype=F32):
    M, K = a.shape
    _, _, Ns = g.shape
    tm, tn = _tile(M, 2048), _tile(Ns, 512)
    nb = Ns // tn

    def body(a_ref, b_ref, *o_refs):
        o_refs = o_refs[len(deps):]
        acc = _dot(a_ref[...], b_ref[...], NN)
        if relu2:
            acc = jnp.maximum(acc, 0.0)
            o_refs[1][...] = (acc * acc).astype(BF16)
        o_refs[0][...] = acc.astype(o_refs[0].dtype)

    o_spec = pl.BlockSpec((tm, tn), lambda i, j: (i, j))
    shapes = [jax.ShapeDtypeStruct((M, N_CHIP * Ns), BF16 if relu2 else out_dtype)]
    specs = [o_spec]
    if relu2:
        shapes.append(jax.ShapeDtypeStruct((M, N_CHIP * Ns), BF16))
        specs.append(o_spec)
    deps = [] if dep is None else [dep]
    out = pl.pallas_call(
        body, grid=(M // tm, N_CHIP * nb), name=name,
        in_specs=[pl.BlockSpec((tm, K), lambda i, j: (i, 0)),
                  pl.BlockSpec((None, K, tn), lambda i, j: (j // nb, 0, j % nb))] + [ANY] * len(deps),
        out_specs=specs, out_shape=shapes, compiler_params=_params("parallel", "parallel"),
    )(a, g, *deps)
    return out if relu2 else out[0]


def mm_nn(a, w, name, out_dtype=F32):
    M, K = a.shape
    N = w.shape[1]
    tm, tn = _tile(M, 2048), _tile(N, 512)

    def body(a_ref, b_ref, o_ref):
        o_ref[...] = _dot(a_ref[...], b_ref[...], NN).astype(out_dtype)

    return pl.pallas_call(
        body, grid=(M // tm, N // tn), name=name,
        in_specs=[pl.BlockSpec((tm, K), lambda i, j: (i, 0)), pl.BlockSpec((K, tn), lambda i, j: (0, j))],
        out_specs=pl.BlockSpec((tm, tn), lambda i, j: (i, j)), out_shape=jax.ShapeDtypeStruct((M, N), out_dtype),
        compiler_params=_params("parallel", "parallel"),
    )(a, w)


def mm_nn_loss(a, w, res, target, name):
    M, K = a.shape
    N = w.shape[1]
    tm, tn = _tile(M, 1024), _tile(N, 512)

    def body(a_ref, b_ref, r_ref, t_ref, dy_ref, dy16_ref, l_ref):
        e = (r_ref[...] + _dot(a_ref[...], b_ref[...], NN)) - t_ref[...]
        dy = e * (1.0 / N)
        dy_ref[...] = dy
        dy16_ref[...] = dy.astype(BF16)
        l_ref[...] = jnp.sum(e * e, axis=0, keepdims=True)

    o_spec = pl.BlockSpec((tm, tn), lambda i, j: (i, j))
    return pl.pallas_call(
        body, grid=(M // tm, N // tn), name=name,
        in_specs=[pl.BlockSpec((tm, K), lambda i, j: (i, 0)), pl.BlockSpec((K, tn), lambda i, j: (0, j)),
                  o_spec, o_spec],
        out_specs=[o_spec, o_spec, pl.BlockSpec((None, 1, tn), lambda i, j: (i, 0, j))],
        out_shape=[jax.ShapeDtypeStruct((M, N), F32), jax.ShapeDtypeStruct((M, N), BF16),
                   jax.ShapeDtypeStruct((M // tm, 1, N), F32)],
        compiler_params=_params("parallel", "parallel"),
    )(a, w, res, target)


def mm_nt(a, w, name, up=None, out_dtype=F32, dep=None):
    M, N = a.shape
    R = w.shape[0]
    tm, tr = _tile(M, 2048), _tile(R, 512)

    def body(a_ref, b_ref, *refs):
        acc = _dot(a_ref[...], b_ref[...], NT)
        if up is not None:
            acc = acc * (2.0 * jnp.maximum(refs[0][...].astype(F32), 0.0))
        refs[-1][...] = acc.astype(out_dtype)

    o_spec = pl.BlockSpec((tm, tr), lambda i, j: (i, j))
    ins = [a, w] + ([up] if up is not None else []) + ([dep] if dep is not None else [])
    return pl.pallas_call(
        body, grid=(M // tm, R // tr), name=name,
        in_specs=[pl.BlockSpec((tm, N), lambda i, j: (i, 0)), pl.BlockSpec((tr, N), lambda i, j: (j, 0))]
        + ([o_spec] if up is not None else []) + ([ANY] if dep is not None else []),
        out_specs=o_spec, out_shape=jax.ShapeDtypeStruct((M, R), out_dtype),
        compiler_params=_params("parallel", "parallel"),
    )(*ins)


def mm_nt_shard(a, g, name, out_dtype=F32, dep=None):
    deps = [] if dep is None else [dep]
    M = a.shape[0]
    _, R, Ns = g.shape
    tm, tr, tk = _tile(M, 1024), _tile(R, 1024), _tile(Ns, 2560)
    nb = Ns // tk
    nk = N_CHIP * nb

    def body(a_ref, b_ref, *refs):
        o_ref, acc_ref = refs[len(deps):]
        k = pl.program_id(2)

        @pl.when(k == 0)
        def _():
            acc_ref[...] = jnp.zeros_like(acc_ref)

        acc_ref[...] += _dot(a_ref[...], b_ref[...], NT)

        @pl.when(k == nk - 1)
        def _():
            o_ref[...] = acc_ref[...].astype(out_dtype)

    return pl.pallas_call(
        body, grid=(M // tm, R // tr, nk), name=name,
        in_specs=[pl.BlockSpec((tm, tk), lambda i, j, k: (i, k)),
                  pl.BlockSpec((None, tr, tk), lambda i, j, k: (k // nb, j, k % nb))] + [ANY] * len(deps),
        out_specs=pl.BlockSpec((tm, tr), lambda i, j, k: (i, j)),
        out_shape=jax.ShapeDtypeStruct((M, R), out_dtype),
        scratch_shapes=[pltpu.VMEM((tm, tr), F32)],
        compiler_params=_params("parallel", "parallel", "arbitrary"),
    )(a, g, *deps)


def mm_tn(a, b, name, shard_out=False):
    S, M = a.shape
    N = b.shape[1]
    Ns = N // N_CHIP if shard_out else N
    tm, tn = _tile(M, 1024), _tile(Ns, 512)
    nb = Ns // tn

    def body(a_ref, b_ref, o_ref):
        o_ref[...] = _dot(a_ref[...], b_ref[...], TN)

    if shard_out:
        o_spec = pl.BlockSpec((None, tm, tn), lambda i, j: (j // nb, i, j % nb))
        o_shape = jax.ShapeDtypeStruct((N_CHIP, M, Ns), F32)
    else:
        o_spec = pl.BlockSpec((tm, tn), lambda i, j: (i, j))
        o_shape = jax.ShapeDtypeStruct((M, N), F32)
    return pl.pallas_call(
        body, grid=(M // tm, N // tn), name=name,
        in_specs=[pl.BlockSpec((S, tm), lambda i, j: (0, i)), pl.BlockSpec((S, tn), lambda i, j: (0, j))],
        out_specs=o_spec, out_shape=o_shape, compiler_params=_params("parallel", "parallel"),
    )(a, b)


def rms_fwd(x, g, name, dep=None):
    S, D = x.shape
    tm = _tile(S, 512)
    deps = [] if dep is None else [dep]

    def body(x_ref, g_ref, *refs):
        xv = x_ref[...]
        r = lax.rsqrt(jnp.mean(xv * xv, axis=-1, keepdims=True) + EPS)
        refs[-1][...] = ((xv * r) * g_ref[...]).astype(BF16)

    return pl.pallas_call(
        body, grid=(S // tm,), name=name,
        in_specs=[pl.BlockSpec((tm, D), lambda i: (i, 0)), pl.BlockSpec((1, D), lambda i: (0, 0))]
        + [ANY] * len(deps),
        out_specs=pl.BlockSpec((tm, D), lambda i: (i, 0)),
        out_shape=jax.ShapeDtypeStruct((S, D), BF16), compiler_params=_params("parallel"),
    )(x, g, *deps)


def rms_bwd(x, g, dh, name, dres=None, want_dx=True, want16=False, dep=None):
    S, D = x.shape
    tm = _tile(S, 512)

    def body(x_ref, g_ref, dh_ref, *refs):
        i = pl.program_id(0)
        xv = x_ref[...]
        r = lax.rsqrt(jnp.mean(xv * xv, axis=-1, keepdims=True) + EPS)
        xn = xv * r
        dhv = dh_ref[...].astype(F32)
        gg_ref = refs[-1]

        @pl.when(i == 0)
        def _():
            gg_ref[...] = jnp.zeros_like(gg_ref)

        gg_ref[...] += jnp.sum(dhv * xn, axis=0, keepdims=True)
        if want_dx:
            dxn = dhv * g_ref[...]
            dx = r * (dxn - xn * jnp.mean(dxn * xn, axis=-1, keepdims=True))
            if dres is not None:
                dx = refs[0][...] + dx
            refs[-2][...] = dx.astype(refs[-2].dtype)
            if want16:
                refs[-3][...] = dx

    row = pl.BlockSpec((tm, D), lambda i: (i, 0))
    vec = pl.BlockSpec((1, D), lambda i: (0, 0))
    ins, in_specs = [x, g, dh], [row, vec, row]
    if dres is not None:
        ins.append(dres)
        in_specs.append(row)
    if dep is not None:
        ins.append(dep)
        in_specs.append(ANY)
    shapes, specs = [jax.ShapeDtypeStruct((1, D), F32)], [vec]
    if want_dx:
        if want16:
            shapes.insert(0, jax.ShapeDtypeStruct((S, D), BF16))
            specs.insert(0, row)
        shapes.insert(0, jax.ShapeDtypeStruct((S, D), F32))
        specs.insert(0, row)
    out = pl.pallas_call(body, grid=(S // tm,), name=name, in_specs=in_specs, out_specs=specs,
                         out_shape=shapes, compiler_params=_params("arbitrary"))(*ins)
    return out if want_dx else out[0]


def _shift_down(u, k, row):
    return jnp.where(row >= k, pltpu.roll(u, k, axis=0), 0.0)


def _shift_up(u, k, row):
    S = u.shape[0]
    return jnp.where(row < S - k, pltpu.roll(u, S - k, axis=0), 0.0)


def _conv_specs(S, D, tc):
    nb = D // tc
    col = lambda o: pl.BlockSpec((S, tc), lambda j, o=o: (0, o * nb + j))
    return col, pl.BlockSpec((3, tc), lambda j: (0, j))


def conv_fwd(proj, conv_w, D, name):
    S = proj.shape[0]
    tc = _tile(D, 256)
    col, wspec = _conv_specs(S, D, tc)

    def body(ch_ref, cb_ref, cc_ref, w_ref, a_ref):
        row = lax.broadcasted_iota(jnp.int32, (S, tc), 0)
        u = cc_ref[...].astype(F32) * ch_ref[...].astype(F32)
        w = w_ref[...]
        cv = w[0:1, :] * _shift_down(u, 2, row) + w[1:2, :] * _shift_down(u, 1, row) + w[2:3, :] * u
        a_ref[...] = (cb_ref[...].astype(F32) * cv).astype(BF16)

    return pl.pallas_call(
        body, grid=(D // tc,), name=name, in_specs=[col(0), col(1), col(2), wspec],
        out_specs=pl.BlockSpec((S, tc), lambda j: (0, j)),
        out_shape=jax.ShapeDtypeStruct((S, D), BF16), compiler_params=_params("parallel"),
    )(proj, proj, proj, conv_w)


def conv_bwd(proj, conv_w, da, D, name):
    S = proj.shape[0]
    tc = _tile(D, 256)
    col, wspec = _conv_specs(S, D, tc)
    blk = pl.BlockSpec((S, tc), lambda j: (0, j))

    def body(ch_ref, cb_ref, cc_ref, w_ref, da_ref, dch_ref, dcb_ref, dcc_ref, gw_ref):
        row = lax.broadcasted_iota(jnp.int32, (S, tc), 0)
        ch, cb, cc, dav = [r[...].astype(F32) for r in (ch_ref, cb_ref, cc_ref, da_ref)]
        w = w_ref[...]
        u = cc * ch
        u1, u2 = _shift_down(u, 1, row), _shift_down(u, 2, row)
        cv = w[0:1, :] * u2 + w[1:2, :] * u1 + w[2:3, :] * u
        dcb_ref[...] = (dav * cv).astype(BF16)
        dcv = dav * cb
        gw_ref[0:1, :] = jnp.sum(dcv * u2, axis=0, keepdims=True)
        gw_ref[1:2, :] = jnp.sum(dcv * u1, axis=0, keepdims=True)
        gw_ref[2:3, :] = jnp.sum(dcv * u, axis=0, keepdims=True)
        du = w[2:3, :] * dcv + w[1:2, :] * _shift_up(dcv, 1, row) + w[0:1, :] * _shift_up(dcv, 2, row)
        dcc_ref[...] = (du * ch).astype(BF16)
        dch_ref[...] = (du * cc).astype(BF16)

    act = jax.ShapeDtypeStruct((S, D), BF16)
    return pl.pallas_call(
        body, grid=(D // tc,), name=name, in_specs=[col(0), col(1), col(2), wspec, blk],
        out_specs=[blk, blk, blk, wspec], out_shape=[act, act, act, jax.ShapeDtypeStruct((3, D), F32)],
        compiler_params=_params("parallel"),
    )(proj, proj, proj, conv_w, da)


SB_BQ_FWD = 512
SB_BQ_BWD = 256
SB_BK = 128
SB_GROUP = 4


def _sb_consts(bq):
    lane = lax.broadcasted_iota(jnp.int32, (bq, LANES), 1)
    r = lax.broadcasted_iota(jnp.int32, (SB_BK, SB_BK), 0)
    c = lax.broadcasted_iota(jnp.int32, (SB_BK, SB_BK), 1)
    tri_rev = jnp.where(r > c, 1.0, 0.0).astype(BF16)
    tri_fwd = jnp.where(r < c, 1.0, 0.0).astype(BF16)
    return lane, tri_rev, tri_fwd


def _cumsum2(v, tri):
    hi = v.astype(BF16)
    lo = (v - hi.astype(F32)).astype(BF16)
    part = (lax.dot_general(hi, tri, NN, preferred_element_type=F32)
            + lax.dot_general(lo, tri, NN, preferred_element_type=F32))
    return part, jnp.sum(v, axis=1, keepdims=True)


def _sb_logits(z, past):
    sp = jnp.log(1.0 + jnp.exp(-jnp.abs(z)))
    l = jnp.minimum(z, 0.0) - sp
    m = l - z
    if past is not None:
        m = jnp.where(past, m, 0.0)
    return l, m


def _stack_heads(v, lane):
    return jnp.concatenate([jnp.where(lane < SB_HEAD_DIM, v, 0.0), jnp.where(lane >= SB_HEAD_DIM, v, 0.0)],
                           axis=0).astype(BF16)


def _unstack_heads(v, lane):
    bq = v.shape[0] // 2
    return jnp.where(lane < SB_HEAD_DIM, v[:bq], v[bq:])


def _sb_positions(i, bq):
    r = lax.broadcasted_iota(jnp.int32, (2 * bq, SB_BK), 0)
    trow = i * bq + jnp.where(r >= bq, r - bq, r)
    return trow, lax.broadcasted_iota(jnp.int32, (2 * bq, SB_BK), 1)


def _sb_specs(S, D, bq):
    npair = D // LANES
    qspec = pl.BlockSpec((bq, LANES), lambda p, i: (i, 3 * npair + p))
    kspec = pl.BlockSpec((S, LANES), lambda p, i: (0, 4 * npair + p))
    vspec = pl.BlockSpec((S, LANES), lambda p, i: (0, 5 * npair + p))
    return npair, qspec, kspec, vspec


def sb_fwd(proj, D, name):
    S = proj.shape[0]
    bq = min(SB_BQ_FWD, S)
    nd = bq // SB_BK
    npair, qspec, kspec, vspec = _sb_specs(S, D, bq)
    scale = SB_HEAD_DIM ** -0.5

    def body(q_ref, k_ref, v_ref, o_ref, kb_ref, vb_ref):
        i = pl.program_id(1)

        @pl.when(i == 0)
        def _():
            kb_ref[...] = k_ref[...].astype(BF16)
            vb_ref[...] = v_ref[...].astype(BF16)

        lane, tri_rev, _ = _sb_consts(bq)
        qs = _stack_heads(q_ref[...].astype(F32) * scale, lane)
        trow, scol = _sb_positions(i, bq)

        def steps(j0, carry, n, masked):
            ks = [pl.multiple_of((j0 - t) * SB_BK, SB_BK) for t in range(n)]
            past = [(k + scol) < trow if masked else None for k in ks]
            zs = [lax.dot_general(qs, kb_ref[pl.ds(k, SB_BK), :], NT, preferred_element_type=F32) for k in ks]
            lm = [_sb_logits(z, p) for z, p in zip(zs, past)]
            cs = [_cumsum2(m, tri_rev) for _, m in lm]
            c, acc = carry
            for t in range(n):
                a = jnp.exp(lm[t][0] + (cs[t][0] + c))
                if masked:
                    a = jnp.where(past[t], a, 0.0)
                acc = acc + lax.dot_general(a.astype(BF16), vb_ref[pl.ds(ks[t], SB_BK), :], NN,
                                            preferred_element_type=F32)
                c = c + cs[t][1]
            return c, acc

        carry = (jnp.zeros((2 * bq, 1), F32), jnp.zeros((2 * bq, LANES), F32))
        carry = steps(i * nd + nd - 1, carry, nd, True)
        older = i * nd
        groups = older // SB_GROUP
        carry = lax.fori_loop(
            0, groups, lambda t, cr: steps(older - 1 - t * SB_GROUP, cr, SB_GROUP, False), carry)
        rest = older - groups * SB_GROUP
        carry = lax.fori_loop(0, rest // nd, lambda t, cr: steps(rest - 1 - t * nd, cr, nd, False), carry)
        o_ref[...] = _unstack_heads(carry[1], lane)

    return pl.pallas_call(
        body, grid=(npair, S // bq), name=name, in_specs=[qspec, kspec, vspec],
        out_specs=pl.BlockSpec((bq, LANES), lambda p, i: (i, p)),
        out_shape=jax.ShapeDtypeStruct((S, D), F32),
        scratch_shapes=[pltpu.VMEM((S, LANES), BF16), pltpu.VMEM((S, LANES), BF16)],
        compiler_params=_params("parallel", "arbitrary"),
    )(proj, proj, proj)


def sb_bwd(proj, do, D, name, dep=None):
    S = proj.shape[0]
    bq = min(SB_BQ_BWD, S)
    nd = bq // SB_BK
    nkb = S // SB_BK
    npair, qspec, kspec, vspec = _sb_specs(S, D, bq)
    scale = SB_HEAD_DIM ** -0.5

    def body(q_ref, k_ref, v_ref, do_ref, *refs):
        dq_ref, dk_ref, dv_ref, kb_ref, vb_ref, dk_acc, dv_acc, g_scr, b_scr, a_scr = refs[len(deps):]
        i = pl.program_id(1)

        @pl.when(i == 0)
        def _():
            kb_ref[...] = k_ref[...].astype(BF16)
            vb_ref[...] = v_ref[...].astype(BF16)
            dk_acc[...] = jnp.zeros_like(dk_acc)
            dv_acc[...] = jnp.zeros_like(dv_acc)

        lane, tri_rev, tri_fwd = _sb_consts(bq)
        qs = _stack_heads(q_ref[...].astype(F32) * scale, lane)
        dos = _stack_heads(do_ref[...].astype(F32), lane)
        qs_t, dos_t = qs.T, dos.T
        trow, scol = _sb_positions(i, bq)

        def sweep1(j0, c, n, masked):
            js = [j0 - t for t in range(n)]
            ks = [pl.multiple_of(j * SB_BK, SB_BK) for j in js]
            past = [(k + scol) < trow if masked else None for k in ks]
            zs = [lax.dot_general(qs, kb_ref[pl.ds(k, SB_BK), :], NT, preferred_element_type=F32) for k in ks]
            das = [lax.dot_general(dos, vb_ref[pl.ds(k, SB_BK), :], NT, preferred_element_type=F32) for k in ks]
            lm = [_sb_logits(z, p) for z, p in zip(zs, past)]
            cs = [_cumsum2(m, tri_rev) for _, m in lm]
            for t in range(n):
                b_scr[js[t]] = jnp.exp(lm[t][0]).astype(BF16)
            for t in range(n):
                a = jnp.exp(lm[t][0] + (cs[t][0] + c))
                if masked:
                    a = jnp.where(past[t], a, 0.0)
                g_scr[js[t]] = (das[t] * a).astype(BF16)
                a_scr[js[t]] = a.astype(BF16)
                c = c + cs[t][1]
            return c

        older = i * nd
        groups = older // SB_GROUP
        rest = older - groups * SB_GROUP
        c = jnp.zeros((2 * bq, 1), F32)
        c = sweep1(i * nd + nd - 1, c, nd, True)
        c = lax.fori_loop(0, groups, lambda t, cr: sweep1(older - 1 - t * SB_GROUP, cr, SB_GROUP, False), c)
        lax.fori_loop(0, rest // nd, lambda t, cr: sweep1(rest - 1 - t * nd, cr, nd, False), c)

        def sweep2(j0, carry, n, masked):
            js = [j0 + t for t in range(n)]
            ks = [pl.multiple_of(j * SB_BK, SB_BK) for j in js]
            g16 = [g_scr[j] for j in js]
            gv = [g.astype(F32) for g in g16]
            gs = [(lax.dot_general(g, tri_fwd, NN, preferred_element_type=F32), jnp.sum(v, axis=1, keepdims=True))
                  for g, v in zip(g16, gv)]
            pc, dq = carry
            dzs = []
            for t in range(n):
                dz = gv[t] - b_scr[js[t]].astype(F32) * (gv[t] + (gs[t][0] + pc))
                if masked:
                    dz = jnp.where((ks[t] + scol) < trow, dz, 0.0)
                dzs.append(dz.astype(BF16))
                pc = pc + gs[t][1]
            for t in range(n):
                dq = dq + lax.dot_general(dzs[t], kb_ref[pl.ds(ks[t], SB_BK), :], NN, preferred_element_type=F32)
                dk_acc[js[t]] += lax.dot_general(qs_t, dzs[t], NN, preferred_element_type=F32)
                dv_acc[js[t]] += lax.dot_general(dos_t, a_scr[js[t]], NN, preferred_element_type=F32)
            return pc, dq

        carry = (jnp.zeros((2 * bq, 1), F32), jnp.zeros((2 * bq, LANES), F32))
        carry = lax.fori_loop(0, groups, lambda t, cr: sweep2(t * SB_GROUP, cr, SB_GROUP, False), carry)
        carry = lax.fori_loop(
            0, rest // nd, lambda t, cr: sweep2(groups * SB_GROUP + t * nd, cr, nd, False), carry)
        carry = sweep2(i * nd, carry, nd, True)
        dq_ref[...] = (_unstack_heads(carry[1], lane) * scale).astype(BF16)

        @pl.when(i == pl.num_programs(1) - 1)
        def _():
            for j in range(nkb):
                dk_ref[j * SB_BK:(j + 1) * SB_BK, :] = dk_acc[j].T.astype(BF16)
                dv_ref[j * SB_BK:(j + 1) * SB_BK, :] = dv_acc[j].T.astype(BF16)

    deps = [] if dep is None else [dep]
    full = pl.BlockSpec((S, LANES), lambda p, i: (0, p))
    blk = pl.BlockSpec((bq, LANES), lambda p, i: (i, p))
    act = jax.ShapeDtypeStruct((S, D), BF16)
    return pl.pallas_call(
        body, grid=(npair, S // bq), name=name, in_specs=[qspec, kspec, vspec, blk] + [ANY] * len(deps),
        out_specs=[blk, full, full], out_shape=[act, act, act],
        scratch_shapes=[pltpu.VMEM((S, LANES), BF16), pltpu.VMEM((S, LANES), BF16),
                        pltpu.VMEM((nkb, LANES, SB_BK), F32), pltpu.VMEM((nkb, LANES, SB_BK), F32),
                        pltpu.VMEM((nkb, 2 * bq, SB_BK), BF16), pltpu.VMEM((nkb, 2 * bq, SB_BK), BF16),
                        pltpu.VMEM((nkb, 2 * bq, SB_BK), BF16)],
        compiler_params=_params("parallel", "arbitrary"),
    )(proj, proj, proj, do, *deps)


def _rms_rows(v):
    r = lax.rsqrt(jnp.mean(v * v, axis=-1, keepdims=True) + EPS)
    return v * r, r


def _xa_specs(S, D, M, tq):
    nh = D // X_HEAD_DIM
    qspec = pl.BlockSpec((tq, X_HEAD_DIM), lambda h, i: (i, 6 * nh + h))
    kspec = pl.BlockSpec((M, X_HEAD_DIM), lambda h, i: (0, h))
    vspec = pl.BlockSpec((M, X_HEAD_DIM), lambda h, i: (0, nh + h))
    gspec = pl.BlockSpec((1, X_HEAD_DIM), lambda h, i: (0, 0))
    return nh, qspec, kspec, vspec, gspec


def xa_fwd(proj, kv, gq, gk, D, name):
    S, M = proj.shape[0], kv.shape[0]
    tq = _tile(S, 2048)
    nh, qspec, kspec, vspec, gspec = _xa_specs(S, D, M, tq)
    scale = X_HEAD_DIM ** -0.5

    def body(q_ref, k_ref, v_ref, gq_ref, gk_ref, o_ref):
        qn = _rms_rows(q_ref[...].astype(F32))[0] * gq_ref[...]
        kn = _rms_rows(k_ref[...])[0] * gk_ref[...]
        s = _dot(qn, kn, NT) * scale
        e = jnp.exp(s - jnp.max(s, axis=-1, keepdims=True))
        p = e / jnp.sum(e, axis=-1, keepdims=True)
        o_ref[...] = _dot(p, v_ref[...], NN)

    return pl.pallas_call(
        body, grid=(nh, S // tq), name=name, in_specs=[qspec, kspec, vspec, gspec, gspec],
        out_specs=pl.BlockSpec((tq, X_HEAD_DIM), lambda h, i: (i, h)),
        out_shape=jax.ShapeDtypeStruct((S, D), F32), compiler_params=_params("parallel", "parallel"),
    )(proj, kv, kv, gq, gk)


def xa_bwd(proj, kv, gq, gk, do, D, name):
    S, M = proj.shape[0], kv.shape[0]
    tq = _tile(S, 2048)
    nh, qspec, kspec, vspec, gspec = _xa_specs(S, D, M, tq)
    scale = X_HEAD_DIM ** -0.5

    def body(q_ref, k_ref, v_ref, gq_ref, gk_ref, do_ref, dq_ref, dk_ref, dv_ref, ggq_ref, ggk_ref,
             dkn_acc, dv_acc):
        h, i = pl.program_id(0), pl.program_id(1)

        @pl.when((h == 0) & (i == 0))
        def _():
            ggq_ref[...] = jnp.zeros_like(ggq_ref)
            ggk_ref[...] = jnp.zeros_like(ggk_ref)

        @pl.when(i == 0)
        def _():
            dkn_acc[...] = jnp.zeros_like(dkn_acc)
            dv_acc[...] = jnp.zeros_like(dv_acc)

        gq, gk = gq_ref[...], gk_ref[...]
        qhat, rq = _rms_rows(q_ref[...].astype(F32))
        khat, rk = _rms_rows(k_ref[...])
        qn, kn = qhat * gq, khat * gk
        s = _dot(qn, kn, NT) * scale
        e = jnp.exp(s - jnp.max(s, axis=-1, keepdims=True))
        p = e / jnp.sum(e, axis=-1, keepdims=True)
        dov = do_ref[...]
        dv_acc[...] += _dot(p, dov, TN)
        dp = _dot(dov, v_ref[...], NT)
        ds = (p * (dp - jnp.sum(dp * p, axis=-1, keepdims=True))) * scale
        dqn = _dot(ds, kn, NN)
        dkn_acc[...] += _dot(ds, qn, TN)
        ggq_ref[...] += jnp.sum(dqn * qhat, axis=0, keepdims=True)
        dqh = dqn * gq
        dq_ref[...] = (rq * (dqh - qhat * jnp.mean(dqh * qhat, axis=-1, keepdims=True))).astype(BF16)

        @pl.when(i == pl.num_programs(1) - 1)
        def _():
            dkn = dkn_acc[...]
            ggk_ref[...] += jnp.sum(dkn * khat, axis=0, keepdims=True)
            dkh = dkn * gk
            dk_ref[...] = (rk * (dkh - khat * jnp.mean(dkh * khat, axis=-1, keepdims=True))).astype(BF16)
            dv_ref[...] = dv_acc[...].astype(BF16)

    blk = pl.BlockSpec((tq, X_HEAD_DIM), lambda h, i: (i, h))
    kv_shape = jax.ShapeDtypeStruct((M, 2 * D), BF16)
    gshape = jax.ShapeDtypeStruct((1, X_HEAD_DIM), F32)
    dq, dk, dv, ggq, ggk = pl.pallas_call(
        body, grid=(nh, S // tq), name=name, in_specs=[qspec, kspec, vspec, gspec, gspec, blk],
        out_specs=[blk, kspec, vspec, gspec, gspec],
        out_shape=[jax.ShapeDtypeStruct((S, D), BF16), kv_shape, kv_shape, gshape, gshape],
        scratch_shapes=[pltpu.VMEM((M, X_HEAD_DIM), F32), pltpu.VMEM((M, X_HEAD_DIM), F32)],
        compiler_params=_params("arbitrary", "arbitrary"),
    )(proj, kv, kv, gq, gk, do)
    d_kv = jnp.concatenate([dk[:, :D], dv[:, D:]], axis=1)
    return dq, d_kv, ggq, ggk


def _gate_specs(S, D, tm):
    row = pl.BlockSpec((tm, D), lambda i: (i, 0))
    gate = lambda b: pl.BlockSpec((tm, D), lambda i, b=b: (i, 7 + b))
    return row, gate


def merge_fwd(proj, ys, w_out, x, D, name):
    S = proj.shape[0]
    tm = _tile(S, 512)
    row, gate = _gate_specs(S, D, tm)

    def body(g0, g1, g2, y0, y1, y2, w_ref, x_ref, m_ref, o_ref):
        acc = jax.nn.sigmoid(g0[...].astype(F32)) * y0[...].astype(F32)
        acc = acc + jax.nn.sigmoid(g1[...].astype(F32)) * y1[...].astype(F32)
        acc = acc + jax.nn.sigmoid(g2[...].astype(F32)) * y2[...].astype(F32)
        merged = acc.astype(BF16)
        m_ref[...] = merged
        o_ref[...] = x_ref[...] + _dot(merged, w_ref[...], NN)

    return pl.pallas_call(
        body, grid=(S // tm,), name=name,
        in_specs=[gate(0), gate(1), gate(2), row, row, row, pl.BlockSpec((D, D), lambda i: (0, 0)), row],
        out_specs=[row, row], out_shape=[jax.ShapeDtypeStruct((S, D), BF16), jax.ShapeDtypeStruct((S, D), F32)],
        compiler_params=_params("parallel"),
    )(proj, proj, proj, *ys, w_out, x)


def merge_bwd(proj, ys, dm, D, name):
    S = proj.shape[0]
    tm = _tile(S, 256)
    row, gate = _gate_specs(S, D, tm)

    def body(g0, g1, g2, y0, y1, y2, dm_ref, d0, d1, d2, dg_ref):
        dmv = dm_ref[...]
        for b, (g_ref, y_ref, d_ref) in enumerate(((g0, y0, d0), (g1, y1, d1), (g2, y2, d2))):
            s = jax.nn.sigmoid(g_ref[...].astype(F32))
            d_ref[...] = (dmv * s).astype(BF16)
            dg_ref[:, b * D:(b + 1) * D] = ((dmv * y_ref[...].astype(F32)) * (s * (1.0 - s))).astype(BF16)

    act = jax.ShapeDtypeStruct((S, D), BF16)
    return pl.pallas_call(
        body, grid=(S // tm,), name=name, in_specs=[gate(0), gate(1), gate(2), row, row, row, row],
        out_specs=[row, row, row, pl.BlockSpec((tm, 3 * D), lambda i: (i, 0))],
        out_shape=[act, act, act, jax.ShapeDtypeStruct((S, 3 * D), BF16)], compiler_params=_params("parallel"),
    )(proj, proj, proj, *ys, dm)


def _rows2d(a):
    return a.reshape(-1, a.shape[-1])


def _ew_call(fn, ins, out_dtypes, name):
    R, C = ins[0].shape
    tr = _tile(R, max(8, (1 << 19) // C))
    spec = pl.BlockSpec((tr, C), lambda i: (i, 0))

    def body(*refs):
        outs = fn(*[r[...] for r in refs[:len(ins)]])
        for o_ref, o in zip(refs[len(ins):], outs):
            o_ref[...] = o.astype(o_ref.dtype)

    return pl.pallas_call(
        body, grid=(R // tr,), name=name, in_specs=[spec] * len(ins), out_specs=[spec] * len(out_dtypes),
        out_shape=[jax.ShapeDtypeStruct((R, C), d) for d in out_dtypes], compiler_params=_params("parallel"),
    )(*ins)


def adamw(w, g, m, v, name):
    def fn(w, g, m, v):
        m = ADAM_B1 * m + (1.0 - ADAM_B1) * g
        v = ADAM_B2 * v + (1.0 - ADAM_B2) * (g * g)
        m_hat = m / (1.0 - ADAM_B1 ** ADAM_STEP)
        v_hat = v / (1.0 - ADAM_B2 ** ADAM_STEP)
        return -ADAM_LR * (m_hat / (jnp.sqrt(v_hat) + ADAM_EPS) + ADAM_WD * w), m, v

    shp = w.shape
    outs = _ew_call(fn, [_rows2d(a) for a in (w, g, m, v)], [F32, F32, F32], name)
    return [o.reshape(shp) for o in outs]


def _placed_call(fn, place, grid, ins, in_specs, out_shape, out_specs, name, dep=None):
    n = len(ins)
    deps = [] if dep is None else [dep]

    def body(place_ref, *refs):
        outs = fn(*[r[...] for r in refs[:n]])
        for o_ref, o in zip(refs[n + len(deps):], outs):
            o_ref[...] = o.astype(o_ref.dtype)

    return pl.pallas_call(
        body, name=name, out_shape=out_shape,
        grid_spec=pltpu.PrefetchScalarGridSpec(
            num_scalar_prefetch=1, grid=grid, in_specs=list(in_specs) + [ANY] * len(deps), out_specs=out_specs),
        compiler_params=_params(*["parallel"] * len(grid)),
    )(place, *ins, *deps)


def _row_tile(R, C):
    return _tile(R, max(16, (1 << 19) // C))


def cast_into_full(w, place, name, dep=None):
    R, C = w.shape
    tr = _row_tile(R, C)
    return _placed_call(
        lambda a: (a,), place, (R // tr,), [w], [pl.BlockSpec((tr, C), lambda i, p: (i, 0))],
        [jax.ShapeDtypeStruct((N_CHIP, R, C), BF16)], [pl.BlockSpec((None, tr, C), lambda i, p: (p[0], i, 0))],
        name, dep=dep)[0]


def pair_sum(g4, got, place, name):
    _, hr, C = got.shape
    tr = _row_tile(hr, C)
    nb = hr // tr
    blk = pl.BlockSpec((None, tr, C), lambda s, i, p: (s, i, 0))
    return _placed_call(
        lambda a, b: (a + b,), place, (N_CHIP, nb), [g4, got],
        [pl.BlockSpec((None, tr, C), lambda s, i, p: (s, p[1] * nb + i, 0)), blk],
        [jax.ShapeDtypeStruct(got.shape, BF16)], [blk], name)[0]


def chip_sum(p32, got, place, name, dep=None):
    _, H, C = p32.shape
    tr = _row_tile(H, C)
    nb = H // tr
    peer = lambda j: pl.BlockSpec((None, tr, C), lambda i, p, j=j: (j, i, 0))
    return _placed_call(
        lambda a, b, c, d: (((a.astype(F32) + b.astype(F32)) + c.astype(F32)) + d.astype(F32),), place, (nb,),
        [p32, got, got, got], [pl.BlockSpec((None, tr, C), lambda i, p: (p[0], i, 0)), peer(0), peer(1), peer(2)],
        [jax.ShapeDtypeStruct((2 * H, C), F32)], [pl.BlockSpec((tr, C), lambda i, p: (p[1] * nb + i, 0))],
        name, dep=dep)[0]


ANY = pl.BlockSpec(memory_space=pl.ANY)
CHIP_FLIPS = ((1, 0), (0, 1), (1, 1))


def _place():
    return lax.axis_index("x"), lax.axis_index("y"), lax.axis_index("c")


def _flip(v, f):
    return 1 - v if f else v


def join_halves(fulls, name):
    n = len(fulls)

    def body(*refs):
        outs = refs[n:2 * n]
        send_sem, recv_sem = refs[2 * n:]
        x, y, c = _place()
        copies = []
        for a in range(n):
            hr = outs[a].shape[0] // 2
            half = outs[a].at[pl.ds(c * hr, hr), :]
            cp = pltpu.make_async_remote_copy(
                src_ref=half, dst_ref=half, send_sem=send_sem.at[a], recv_sem=recv_sem.at[a],
                device_id=(x, y, 1 - c), device_id_type=MESH)
            cp.start()
            copies.append(cp)
        for a, cp in enumerate(copies):
            hr = outs[a].shape[0] // 2
            theirs = outs[a].at[pl.ds((1 - c) * hr, hr), :]
            cp.wait_send()
            pltpu.make_async_remote_copy(
                src_ref=theirs, dst_ref=theirs, send_sem=send_sem.at[a], recv_sem=recv_sem.at[a],
                device_id=(x, y, 1 - c), device_id_type=MESH).wait_recv()

    dma = pltpu.SemaphoreType.DMA
    return pl.pallas_call(
        body, name=name, in_specs=[ANY] * n, out_specs=[ANY] * n,
        out_shape=[jax.ShapeDtypeStruct(f.shape, F32) for f in fulls],
        input_output_aliases={a: a for a in range(n)},
        scratch_shapes=[dma((n,)), dma((n,))],
    )(*fulls)


HBM = pl.BlockSpec(memory_space=pltpu.HBM)
SEM = pl.BlockSpec(memory_space=pltpu.SEMAPHORE)
EFFECT = pltpu.SideEffectType.DATAFLOW_SIDE_EFFECTING


def _in_hbm(a):
    return pltpu.with_memory_space_constraint(a, pltpu.HBM)


def _gather_half(ref, chip_idx, core):
    hr = ref.shape[1] // 2
    return ref.at[chip_idx, pl.ds(core * hr, hr), :]


def gather_forward(fulls, small, name):
    n = len(fulls)

    def body(*refs):
        small_in = refs[n]
        outs, small_out = refs[n + 1:2 * n + 1], refs[2 * n + 1]
        send_sem, recv_sem, sm_send, sm_recv, loc_sem = refs[2 * n + 2:]
        x, y, c = _place()
        mine = 2 * x + y
        chips = [(_flip(x, fx), _flip(y, fy)) for fx, fy in CHIP_FLIPS]
        local = pltpu.make_async_copy(small_in, small_out.at[mine], loc_sem)
        local.start()
        copies = []
        for j, (px, py) in enumerate(chips):
            cp = pltpu.make_async_remote_copy(
                src_ref=small_in, dst_ref=small_out.at[mine], send_sem=sm_send.at[j], recv_sem=sm_recv.at[j],
                device_id=(px, py, c), device_id_type=MESH)
            cp.start()
            copies.append(cp)
        for a in range(n):
            for j, (px, py) in enumerate(chips):
                src = _gather_half(outs[a], 2 * px + py, c)
                cp = pltpu.make_async_remote_copy(
                    src_ref=src, dst_ref=src, send_sem=send_sem.at[3 * a + j], recv_sem=recv_sem.at[3 * a + j],
                    device_id=(x, y, 1 - c), device_id_type=MESH)
                cp.start()
                copies.append(cp)
        for a in range(n):
            for j, (px, py) in enumerate(chips):
                dst = _gather_half(outs[a], 2 * px + py, 1 - c)
                pltpu.make_async_remote_copy(
                    src_ref=dst, dst_ref=dst, send_sem=send_sem.at[3 * a + j], recv_sem=recv_sem.at[3 * a + j],
                    device_id=(x, y, 1 - c), device_id_type=MESH).wait_recv()
        for j, (px, py) in enumerate(chips):
            dst = small_out.at[2 * px + py]
            pltpu.make_async_remote_copy(
                src_ref=dst, dst_ref=dst, send_sem=sm_send.at[j], recv_sem=sm_recv.at[j],
                device_id=(px, py, c), device_id_type=MESH).wait_recv()
        for cp in copies:
            cp.wait_send()
        local.wait()

    dma = pltpu.SemaphoreType.DMA
    out = pl.pallas_call(
        body, name=name, in_specs=[ANY] * (n + 1), out_specs=[ANY] * (n + 1),
        out_shape=[jax.ShapeDtypeStruct(f.shape, f.dtype) for f in fulls]
        + [jax.ShapeDtypeStruct((N_CHIP,) + small.shape, small.dtype)],
        input_output_aliases={a: a for a in range(n)},
        scratch_shapes=[dma((3 * n,)), dma((3 * n,)), dma((3,)), dma((3,)), dma],
    )(*fulls, small)
    return out[:n], out[n]


def _gather_plan(fulls, lands):
    x, y, c = _place()
    mine = 2 * x + y
    return [(_gather_half(f, mine, c), _gather_half(f, mine, c), (_flip(x, fx), _flip(y, fy), c))
            for f in fulls for fx, fy in CHIP_FLIPS]


def _scatter_plan(parts, lands):
    x, y, c = _place()
    plan = []
    for p, l in zip(parts, lands):
        for j, (fx, fy) in enumerate(CHIP_FLIPS):
            px, py = _flip(x, fx), _flip(y, fy)
            plan.append((p.at[2 * px + py], l.at[j], (px, py, c)))
    return plan


def _scatter_lands(parts):
    return [(3,) + p.shape[1:] for p in parts]


def _exchange_plan(grads, lands):
    x, y, c = _place()
    plan = []
    for g, l in zip(grads, lands):
        hr = g.shape[1] // 2
        plan.append((g.at[:, pl.ds((1 - c) * hr, hr), :], l, (x, y, 1 - c)))
    return plan


def _exchange_lands(grads):
    return [(N_CHIP, g.shape[1] // 2, g.shape[2]) for g in grads]


def _forward_plan(fulls, lands):
    x, y, c = _place()
    return [(_gather_half(f, 2 * _flip(x, fx) + _flip(y, fy), c), _gather_half(f, 2 * _flip(x, fx) + _flip(y, fy), c),
             (x, y, 1 - c)) for f in fulls for fx, fy in CHIP_FLIPS]


def split_start(plan, copies, srcs, land_shapes, deps, name):
    n, m = len(srcs), len(land_shapes)
    lands = [lax.empty(s, srcs[0].dtype) for s in land_shapes]

    def body(*refs):
        k0 = n + m + len(deps)
        send_sem, recv_sem = refs[k0], refs[k0 + 1]
        thru, token = refs[k0 + 2:k0 + 2 + n + m], refs[k0 + 2 + n + m]
        for k, (src, dst, dev) in enumerate(plan(thru[:n], thru[n:])):
            pltpu.make_async_remote_copy(src_ref=src, dst_ref=dst, send_sem=send_sem.at[k], recv_sem=recv_sem.at[k],
                                         device_id=dev, device_id_type=MESH).start()
        token[...] = jnp.zeros_like(token)

    dma = pltpu.SemaphoreType.DMA
    arrays = list(srcs) + lands
    out = pl.pallas_call(
        body, name=name,
        out_shape=(dma((copies,)), dma((copies,)), *[pltpu.HBM(a.shape, a.dtype) for a in arrays],
                   jax.ShapeDtypeStruct((8, LANES), F32)),
        in_specs=[HBM] * (n + m) + [ANY] * len(deps),
        out_specs=(SEM, SEM, *[HBM] * (n + m), pl.BlockSpec(memory_space=pltpu.VMEM)),
        input_output_aliases={a: 2 + a for a in range(n + m)},
        compiler_params=pltpu.CompilerParams(has_side_effects=EFFECT),
    )(*[_in_hbm(a) for a in arrays], *deps)
    return (out[0], out[1], list(out[2:2 + n]), list(out[2 + n:2 + n + m])), out[2 + n + m]


def split_wait(plan, handle, after, name):
    send_sem, recv_sem, srcs, lands = handle
    n, m = len(srcs), len(lands)

    def body(*refs):
        send_sem, recv_sem = refs[n + m], refs[n + m + 1]
        thru = refs[n + m + 2 + len(after):]
        for k, (src, dst, dev) in enumerate(plan(thru[:n], thru[n:])):
            cp = pltpu.make_async_remote_copy(src_ref=src, dst_ref=dst, send_sem=send_sem.at[k],
                                              recv_sem=recv_sem.at[k], device_id=dev, device_id_type=MESH)
            cp.wait_send()
            cp.wait_recv()

    arrays = list(srcs) + list(lands)
    out = pl.pallas_call(
        body, name=name, out_shape=tuple(pltpu.HBM(a.shape, a.dtype) for a in arrays),
        in_specs=[HBM] * (n + m) + [SEM, SEM] + [ANY] * len(after), out_specs=tuple([HBM] * (n + m)),
        input_output_aliases={a: a for a in range(n + m)},
        compiler_params=pltpu.CompilerParams(has_side_effects=EFFECT),
    )(*arrays, send_sem, recv_sem, *after)
    return list(out[:n]), list(out[n:])


def allreduce_small(block, name, dep):
    R, C = block.shape

    def body(in_ref, dep_ref, out_ref, slots, send_sem, recv_sem):
        x, y, c = _place()
        me = 4 * x + 2 * y + c
        slots[me] = in_ref[...]
        copies = []
        for r in range(1, 8):
            fx, fy, fc = (r >> 2) & 1, (r >> 1) & 1, r & 1
            cp = pltpu.make_async_remote_copy(
                src_ref=in_ref, dst_ref=slots.at[me], send_sem=send_sem.at[r - 1], recv_sem=recv_sem.at[r - 1],
                device_id=(_flip(x, fx), _flip(y, fy), _flip(c, fc)), device_id_type=MESH)
            cp.start()
            copies.append(cp)
        for cp in copies:
            cp.wait()
        acc = slots[0]
        for d in range(1, 8):
            acc = acc + slots[d]
        out_ref[...] = acc

    vm = pl.BlockSpec(memory_space=pltpu.VMEM)
    dma = pltpu.SemaphoreType.DMA
    return pl.pallas_call(
        body, name=name, in_specs=[vm, ANY], out_specs=vm, out_shape=jax.ShapeDtypeStruct((R, C), F32),
        scratch_shapes=[pltpu.VMEM((8, R, C), F32), dma((7,)), dma((7,))],
    )(block, dep)


def local_step(x, mem, target, g_mix, g_mem, q_norm_g, k_norm_g, g_mlp, conv_w, h, mem_n, w_in, w_in_dep,
               rest_weights, early_grads, mid_grads, late_grads, last_grads):
    S, D = x.shape
    proj = mm_nn_shard(h, w_in, "proj", dep=w_in_dep, out_dtype=BF16)
    a_conv = conv_fwd(proj, conv_w, D, "conv_fwd")
    o_sb = sb_fwd(proj, D, "sb_fwd")
    w_conv_out, w_sb_out, w_mem_kv, w_x_out, w_out, mlp_dep, mlp_weights = rest_weights(o_sb)
    kv = mm_nn_shard(mem_n, w_mem_kv, "kv", dep=mlp_dep)
    o_x = xa_fwd(proj, kv, q_norm_g, k_norm_g, D, "xa_fwd")
    ys = [mm_nn(a_conv, w_conv_out, "y_conv", out_dtype=BF16), mm_nn(o_sb, w_sb_out, "y_sb", out_dtype=BF16),
          mm_nn(o_x, w_x_out, "y_x", out_dtype=BF16)]
    merged, x1 = merge_fwd(proj, ys, w_out, x, D, "merge_x1")
    h2 = rms_fwd(x1, g_mlp, "rms_mlp")
    w_up, w_down = mlp_weights(h2)
    up, act = mm_nn_shard(h2, w_up, "up", relu2=True)
    dy, dy16, loss_parts = mm_nn_loss(act, w_down, x1, target, "x2_loss")
    loss_cols = jnp.sum(loss_parts, axis=0)
    d_up = mm_nt(dy16, w_down, "d_up", up=up, out_dtype=BF16)
    g = {"w_down": mm_tn(act, dy16, "g_w_down")}
    g["w_up"] = mm_tn(h2, d_up, "g_w_up", shard_out=True)
    dh2 = mm_nt_shard(d_up, w_up, "dh2")
    dx1, dx1_16, g["g_mlp"] = rms_bwd(x1, g_mlp, dh2, "rms_mlp_bwd", dres=dy, want16=True)
    g["w_out"] = mm_tn(merged, dx1_16, "g_w_out")
    dm = mm_nt(dx1_16, w_out, "d_merged")
    dy_c, dy_s, dy_x, d_gate = merge_bwd(proj, ys, dm, D, "merge_bwd")
    g["w_conv_out"] = mm_tn(a_conv, dy_c, "g_w_conv_out")
    g["w_sb_out"] = mm_tn(o_sb, dy_s, "g_w_sb_out")
    g["w_x_out"] = mm_tn(o_x, dy_x, "g_w_x_out")
    d_xq, d_kv, g["q_norm_g"], g["k_norm_g"] = xa_bwd(
        proj, kv, q_norm_g, k_norm_g, mm_nt(dy_x, w_x_out, "d_o_x", out_dtype=BF16), D, "xa_bwd")
    g["w_mem_kv"] = mm_tn(mem_n, d_kv, "g_w_mem_kv", shard_out=True)
    g["g_mem"] = rms_bwd(mem, g_mem, mm_nt_shard(d_kv, w_mem_kv, "d_mem_n"), "rms_mem_bwd", want_dx=False)
    dep = early_grads(g)
    d_a_conv = mm_nt(dy_c, w_conv_out, "d_a_conv", dep=dep, out_dtype=BF16)
    d_ch, d_cb, d_cc, g["conv_w"] = conv_bwd(proj, conv_w, d_a_conv, D, "conv_bwd")
    d_o_sb = mm_nt(dy_s, w_sb_out, "d_o_sb", dep=dep, out_dtype=BF16)
    dq, dk, dv = sb_bwd(proj, d_o_sb, D, "sb_bwd", dep=mid_grads([d_ch, d_o_sb]))
    d_proj = jnp.concatenate([d_ch, d_cb, d_cc, dq, dk, dv, d_xq, d_gate], axis=1)
    g["w_in"] = mm_tn(h, d_proj, "g_w_in", shard_out=True)
    dh = mm_nt_shard(d_proj, w_in, "dh", dep=late_grads(g["w_in"]))
    grad_x, g["g_mix"] = rms_bwd(x, g_mix, dh, "rms_mix_bwd", dres=dx1, dep=last_grads(dh))
    return loss_cols, grad_x, g


BIG = ("w_in", "w_conv_out", "w_sb_out", "w_mem_kv", "w_x_out", "w_out", "w_up", "w_down")
REST = BIG[1:]
COL_SHARDED = ("w_in", "w_mem_kv", "w_up")
WEIGHTS = ("g_mix", "g_mem", "w_in", "conv_w", "w_conv_out", "w_sb_out", "q_norm_g", "k_norm_g",
           "w_mem_kv", "w_x_out", "w_out", "g_mlp", "w_up", "w_down")


def _pack_small(D, g_mix, g_mem, g_mlp, q_norm_g, k_norm_g, conv_w, last):
    qk = jnp.concatenate([q_norm_g, k_norm_g, jnp.zeros((1, D - 2 * X_HEAD_DIM), F32)], axis=1)
    cw = jnp.pad(conv_w, ((0, 0), (0, D - conv_w.shape[1])))
    return jnp.concatenate([g_mix, g_mem, g_mlp, qk, cw, last], axis=0)


def kernel(x, mem, g_mix, g_mem, w_in, conv_w, w_conv_out, w_sb_out, q_norm_g, k_norm_g, w_mem_kv, w_x_out, w_out, g_mlp, w_up, w_down, loss_target, m_g_mix, m_g_mem, m_w_in, m_conv_w, m_w_conv_out, m_w_sb_out, m_q_norm_g, m_k_norm_g, m_w_mem_kv, m_w_x_out, m_w_out, m_g_mlp, m_w_up, m_w_down, v_g_mix, v_g_mem, v_w_in, v_conv_w, v_w_conv_out, v_w_sb_out, v_q_norm_g, v_k_norm_g, v_w_mem_kv, v_w_x_out, v_w_out, v_g_mlp, v_w_up, v_w_down):
    S, D = x.shape[1], x.shape[2]
    w = dict(g_mix=g_mix, g_mem=g_mem, w_in=w_in, conv_w=conv_w, w_conv_out=w_conv_out, w_sb_out=w_sb_out,
             q_norm_g=q_norm_g, k_norm_g=k_norm_g, w_mem_kv=w_mem_kv, w_x_out=w_x_out, w_out=w_out,
             g_mlp=g_mlp, w_up=w_up, w_down=w_down)
    m = dict(g_mix=m_g_mix, g_mem=m_g_mem, w_in=m_w_in, conv_w=m_conv_w, w_conv_out=m_w_conv_out,
             w_sb_out=m_w_sb_out, q_norm_g=m_q_norm_g, k_norm_g=m_k_norm_g, w_mem_kv=m_w_mem_kv,
             w_x_out=m_w_x_out, w_out=m_w_out, g_mlp=m_g_mlp, w_up=m_w_up, w_down=m_w_down)
    v = dict(g_mix=v_g_mix, g_mem=v_g_mem, w_in=v_w_in, conv_w=v_conv_w, w_conv_out=v_w_conv_out,
             w_sb_out=v_w_sb_out, q_norm_g=v_q_norm_g, k_norm_g=v_k_norm_g, w_mem_kv=v_w_mem_kv,
             w_x_out=v_w_x_out, w_out=v_w_out, g_mlp=v_g_mlp, w_up=v_w_up, w_down=v_w_down)
    chip = 2 * lax.axis_index("x") + lax.axis_index("y")
    cs = conv_w.shape[2]

    place = jnp.stack([chip, lax.axis_index("c")]).astype(jnp.int32)
    cw_block = jnp.pad(conv_w[0], ((0, 5), (0, 0)))
    handle, token = split_start(_gather_plan, 3, [cast_into_full(w["w_in"][0], place, "cast_w_in")], [], [],
                                "gather_w_in_start")
    rest16 = [cast_into_full(w[k][0], place, "cast_" + k, dep=token) for k in REST]
    h = rms_fwd(x[0], g_mix, "rms_mix", dep=token)
    mem_n = rms_fwd(mem[0], g_mem, "rms_mem", dep=token)
    landed, _ = split_wait(_gather_plan, handle, [*rest16, h, mem_n], "gather_w_in_wait")
    (w_in_full,), cw_all = gather_forward(landed, cw_block, "gather_w_in_forward")
    conv_full = jnp.concatenate([cw_all[p, :3] for p in range(N_CHIP)], axis=1)
    rest_handle, rest_token = split_start(_gather_plan, 3 * len(REST), rest16, [], [w_in_full], "gather_rest_start")

    def layout(k, a):
        return a if k in COL_SHARDED else a.reshape(-1, a.shape[-1])

    def rest_weights(after):
        landed, _ = split_wait(_gather_plan, rest_handle, [after], "gather_rest_wait")
        first = gather_forward(landed[:-2], cw_block, "gather_rest_forward")[0]
        mlp_handle, token = split_start(_forward_plan, 6, landed[-2:], [], [first[0]], "forward_mlp_start")

        def mlp_weights(after):
            both, _ = split_wait(_forward_plan, mlp_handle, [after], "forward_mlp_wait")
            return [layout(k, a) for k, a in zip(REST[-2:], both)]

        return [layout(k, a) for k, a in zip(REST[:-2], first)] + [token, mlp_weights]

    def blocks(k, a):
        return a if k in COL_SHARDED else a.reshape(N_CHIP, -1, a.shape[-1])

    early = {}

    def early_grads(g):
        early["g4"] = [blocks(k, g[k]) for k in REST]
        early["swap"], token = split_start(_exchange_plan, len(REST), early["g4"], _exchange_lands(early["g4"]),
                                           [g["g_mem"]], "exchange_rest_start")
        return token

    def mid_grads(after):
        g4, got = split_wait(_exchange_plan, early["swap"], after, "exchange_rest_wait")
        p16 = [pair_sum(a, b, place, "pair_sum_" + k) for k, a, b in zip(REST, g4, got)]
        early["fly"], token = split_start(_scatter_plan, 3 * len(REST), p16, _scatter_lands(p16), [],
                                          "scatter_rest_start")
        return token

    late = {}

    def late_grads(gw):
        late["swap"], token = split_start(_exchange_plan, 1, [gw], _exchange_lands([gw]), [], "exchange_w_in_start")
        return token

    def last_grads(after):
        (gw,), (got,) = split_wait(_exchange_plan, late["swap"], [after], "exchange_w_in_wait")
        p16 = [pair_sum(gw, got, place, "pair_sum_w_in")]
        late["fly"], token = split_start(_scatter_plan, 3, p16, _scatter_lands(p16), [], "scatter_w_in_start")
        return token

    loss_cols, grad_x, g = local_step(
        x[0], mem[0], loss_target[0], g_mix, g_mem, q_norm_g, k_norm_g, g_mlp, conv_full, h, mem_n,
        w_in_full, rest_token, rest_weights, early_grads, mid_grads, late_grads, last_grads)
    token = g["g_mix"]

    p16_rest, got_rest = split_wait(_scatter_plan, early["fly"], [token], "scatter_rest_wait")
    gsum, delta, new_m, new_v = {}, {}, {}, {}
    halves = [chip_sum(p, b, place, "chip_sum_" + k, dep=token) for k, p, b in zip(REST, p16_rest, got_rest)]
    for k, a in zip(REST, join_halves(halves, "join_halves_rest")):
        gsum[k] = a[None]
        delta[k], new_m[k], new_v[k] = adamw(w[k], gsum[k], m[k], v[k], "adamw_" + k)
    p16_in, got_in = split_wait(_scatter_plan, late["fly"], [new_v[k] for k in REST], "scatter_w_in_wait")

    small = allreduce_small(
        _pack_small(D, g["g_mix"], g["g_mem"], g["g_mlp"], g["q_norm_g"], g["k_norm_g"], g["conv_w"], loss_cols),
        "allreduce_small", dep=got_in[0])
    loss = (0.5 / D) * jnp.sum(small[7])
    gsum.update({"g_mix": small[0:1], "g_mem": small[1:2], "g_mlp": small[2:3],
                 "q_norm_g": small[3:4, :X_HEAD_DIM], "k_norm_g": small[3:4, X_HEAD_DIM:2 * X_HEAD_DIM],
                 "conv_w": lax.dynamic_slice(small[4:7], (0, chip * cs), (3, cs))[None]})
    half_in = chip_sum(p16_in[0], got_in[0], place, "chip_sum_w_in")
    gsum["w_in"] = join_halves([half_in], "join_halves_w_in")[0][None]
    delta["w_in"], new_m["w_in"], new_v["w_in"] = adamw(w["w_in"], gsum["w_in"], m["w_in"], v["w_in"], "adamw_w_in")
    small_names = ("g_mix", "g_mem", "g_mlp", "q_norm_g", "k_norm_g", "conv_w")
    zero_row = jnp.zeros((1, D), F32)
    packed = [_pack_small(D, *[t[k] if k != "conv_w" else t[k][0] for k in small_names], zero_row)
              for t in (w, gsum, m, v)]
    sm = adamw(*packed, "adamw_small")
    for t, block in zip((delta, new_m, new_v), sm):
        t["g_mix"], t["g_mem"], t["g_mlp"] = block[0:1], block[1:2], block[2:3]
        t["q_norm_g"], t["k_norm_g"] = block[3:4, :X_HEAD_DIM], block[3:4, X_HEAD_DIM:2 * X_HEAD_DIM]
        t["conv_w"] = block[4:7, :cs][None]

    return (loss, grad_x[None], *[gsum[k] for k in WEIGHTS], *[delta[k] for k in WEIGHTS],
            *[new_m[k] for k in WEIGHTS], *[new_v[k] for k in WEIGHTS])
```

```python
import functools

import jax
import jax.numpy as jnp
from jax import lax
from jax.experimental import pallas as pl
from jax.experimental.pallas import tpu as pltpu

F32 = jnp.float32
BF16 = jnp.bfloat16
EPS = 1e-6
N_CHIP = 4
SB_HEAD_DIM = 64
X_HEAD_DIM = 256
LANES = 128
VMEM_LIMIT = 56 * 1024 * 1024
ADAM_LR, ADAM_B1, ADAM_B2, ADAM_EPS, ADAM_WD, ADAM_STEP = 0.001, 0.9, 0.999, 1e-8, 0.01, 10
MESH = pl.DeviceIdType.MESH


def _params(*sem):
    return pltpu.CompilerParams(dimension_semantics=sem, vmem_limit_bytes=VMEM_LIMIT)


def _tile(n, pref):
    if n <= pref:
        return n
    t = 1 << (pref.bit_length() - 1)
    while n % t:
        t //= 2
    return t


NN = (((1,), (0,)), ((), ()))
NT = (((1,), (1,)), ((), ()))
TN = (((0,), (0,)), ((), ()))


def _dot(a, b, dims):
    return lax.dot_general(a.astype(BF16), b.astype(BF16), dims, preferred_element_type=F32)


def mm_nn_shard(a, g, name, relu2=False, dep=None, out_dtype=F32):
    M, K = a.shape
    _, _, Ns = g.shape
    tm, tn = _tile(M, 2048), _tile(Ns, 512)
    nb = Ns // tn

    def body(a_ref, b_ref, *o_refs):
        o_refs = o_refs[len(deps):]
        acc = _dot(a_ref[...], b_ref[...], NN)
        if relu2:
            acc = jnp.maximum(acc, 0.0)
            o_refs[1][...] = (acc * acc).astype(BF16)
        o_refs[0][...] = acc.astype(o_refs[0].dtype)

    o_spec = pl.BlockSpec((tm, tn), lambda i, j: (i, j))
    shapes = [jax.ShapeDtypeStruct((M, N_CHIP * Ns), BF16 if relu2 else out_dtype)]
    specs = [o_spec]
    if relu2:
        shapes.append(jax.ShapeDtypeStruct((M, N_CHIP * Ns), BF16))
        specs.append(o_spec)
    deps = [] if dep is None else [dep]
    out = pl.pallas_call(
        body, grid=(M // tm, N_CHIP * nb), name=name,
        in_specs=[pl.BlockSpec((tm, K), lambda i, j: (i, 0)),
                  pl.BlockSpec((None, K, tn), lambda i, j: (j // nb, 0, j % nb))] + [ANY] * len(deps),
        out_specs=specs, out_shape=shapes, compiler_params=_params("parallel", "parallel"),
    )(a, g, *deps)
    return out if relu2 else out[0]


def mm_nn(a, w, name, out_dtype=F32):
    M, K = a.shape
    N = w.shape[1]
    tm, tn = _tile(M, 2048), _tile(N, 512)

    def body(a_ref, b_ref, o_ref):
        o_ref[...] = _dot(a_ref[...], b_ref[...], NN).astype(out_dtype)

    return pl.pallas_call(
        body, grid=(M // tm, N // tn), name=name,
        in_specs=[pl.BlockSpec((tm, K), lambda i, j: (i, 0)), pl.BlockSpec((K, tn), lambda i, j: (0, j))],
        out_specs=pl.BlockSpec((tm, tn), lambda i, j: (i, j)), out_shape=jax.ShapeDtypeStruct((M, N), out_dtype),
        compiler_params=_params("parallel", "parallel"),
    )(a, w)


def mm_nn_loss(a, w, res, target, name):
    M, K = a.shape
    N = w.shape[1]
    tm, tn = _tile(M, 1024), _tile(N, 512)

    def body(a_ref, b_ref, r_ref, t_ref, dy_ref, dy16_ref, l_ref):
        e = (r_ref[...] + _dot(a_ref[...], b_ref[...], NN)) - t_ref[...]
        dy = e * (1.0 / N)
        dy_ref[...] = dy
        dy16_ref[...] = dy.astype(BF16)
        l_ref[...] = jnp.sum(e * e, axis=0, keepdims=True)

    o_spec = pl.BlockSpec((tm, tn), lambda i, j: (i, j))
    return pl.pallas_call(
        body, grid=(M // tm, N // tn), name=name,
        in_specs=[pl.BlockSpec((tm, K), lambda i, j: (i, 0)), pl.BlockSpec((K, tn), lambda i, j: (0, j)),
                  o_spec, o_spec],
        out_specs=[o_spec, o_spec, pl.BlockSpec((None, 1, tn), lambda i, j: (i, 0, j))],
        out_shape=[jax.ShapeDtypeStruct((M, N), F32), jax.ShapeDtypeStruct((M, N), BF16),
                   jax.ShapeDtypeStruct((M // tm, 1, N), F32)],
        compiler_params=_params("parallel", "parallel"),
    )(a, w, res, target)


def mm_nt(a, w, name, up=None, out_dtype=F32, dep=None):
    M, N = a.shape
    R = w.shape[0]
    tm, tr = _tile(M, 2048), _tile(R, 512)

    def body(a_ref, b_ref, *refs):
        acc = _dot(a_ref[...], b_ref[...], NT)
        if up is not None:
            acc = acc * (2.0 * jnp.maximum(refs[0][...].astype(F32), 0.0))
        refs[-1][...] = acc.astype(out_dtype)

    o_spec = pl.BlockSpec((tm, tr), lambda i, j: (i, j))
    ins = [a, w] + ([up] if up is not None else []) + ([dep] if dep is not None else [])
    return pl.pallas_call(
        body, grid=(M // tm, R // tr), name=name,
        in_specs=[pl.BlockSpec((tm, N), lambda i, j: (i, 0)), pl.BlockSpec((tr, N), lambda i, j: (j, 0))]
        + ([o_spec] if up is not None else []) + ([ANY] if dep is not None else []),
        out_specs=o_spec, out_shape=jax.ShapeDtypeStruct((M, R), out_dtype),
        compiler_params=_params("parallel", "parallel"),
    )(*ins)


def mm_nt_shard(a, g, name, out_dtype=F32, dep=None):
    deps = [] if dep is None else [dep]
    M = a.shape[0]
    _, R, Ns = g.shape
    tm, tr, tk = _tile(M, 1024), _tile(R, 1024), _tile(Ns, 2560)
    nb = Ns // tk
    nk = N_CHIP * nb

    def body(a_ref, b_ref, *refs):
        o_ref, acc_ref = refs[len(deps):]
        k = pl.program_id(2)

        @pl.when(k == 0)
        def _():
            acc_ref[...] = jnp.zeros_like(acc_ref)

        acc_ref[...] += _dot(a_ref[...], b_ref[...], NT)

        @pl.when(k == nk - 1)
        def _():
            o_ref[...] = acc_ref[...].astype(out_dtype)

    return pl.pallas_call(
        body, grid=(M // tm, R // tr, nk), name=name,
        in_specs=[pl.BlockSpec((tm, tk), lambda i, j, k: (i, k)),
                  pl.BlockSpec((None, tr, tk), lambda i, j, k: (k // nb, j, k % nb))] + [ANY] * len(deps),
        out_specs=pl.BlockSpec((tm, tr), lambda i, j, k: (i, j)),
        out_shape=jax.ShapeDtypeStruct((M, R), out_dtype),
        scratch_shapes=[pltpu.VMEM((tm, tr), F32)],
        compiler_params=_params("parallel", "parallel", "arbitrary"),
    )(a, g, *deps)


def mm_tn(a, b, name, shard_out=False):
    S, M = a.shape
    N = b.shape[1]
    Ns = N // N_CHIP if shard_out else N
    tm, tn = _tile(M, 1024), _tile(Ns, 512)
    nb = Ns // tn

    def body(a_ref, b_ref, o_ref):
        o_ref[...] = _dot(a_ref[...], b_ref[...], TN)

    if shard_out:
        o_spec = pl.BlockSpec((None, tm, tn), lambda i, j: (j // nb, i, j % nb))
        o_shape = jax.ShapeDtypeStruct((N_CHIP, M, Ns), F32)
    else:
        o_spec = pl.BlockSpec((tm, tn), lambda i, j: (i, j))
        o_shape = jax.ShapeDtypeStruct((M, N), F32)
    return pl.pallas_call(
        body, grid=(M // tm, N // tn), name=name,
        in_specs=[pl.BlockSpec((S, tm), lambda i, j: (0, i)), pl.BlockSpec((S, tn), lambda i, j: (0, j))],
        out_specs=o_spec, out_shape=o_shape, compiler_params=_params("parallel", "parallel"),
    )(a, b)


def rms_fwd(x, g, name, dep=None):
    S, D = x.shape
    tm = _tile(S, 512)
    deps = [] if dep is None else [dep]

    def body(x_ref, g_ref, *refs):
        xv = x_ref[...]
        r = lax.rsqrt(jnp.mean(xv * xv, axis=-1, keepdims=True) + EPS)
        refs[-1][...] = ((xv * r) * g_ref[...]).astype(BF16)

    return pl.pallas_call(
        body, grid=(S // tm,), name=name,
        in_specs=[pl.BlockSpec((tm, D), lambda i: (i, 0)), pl.BlockSpec((1, D), lambda i: (0, 0))]
        + [ANY] * len(deps),
        out_specs=pl.BlockSpec((tm, D), lambda i: (i, 0)),
        out_shape=jax.ShapeDtypeStruct((S, D), BF16), compiler_params=_params("parallel"),
    )(x, g, *deps)


def rms_bwd(x, g, dh, name, dres=None, want_dx=True, want16=False, dep=None):
    S, D = x.shape
    tm = _tile(S, 512)

    def body(x_ref, g_ref, dh_ref, *refs):
        i = pl.program_id(0)
        xv = x_ref[...]
        r = lax.rsqrt(jnp.mean(xv * xv, axis=-1, keepdims=True) + EPS)
        xn = xv * r
        dhv = dh_ref[...].astype(F32)
        gg_ref = refs[-1]

        @pl.when(i == 0)
        def _():
            gg_ref[...] = jnp.zeros_like(gg_ref)

        gg_ref[...] += jnp.sum(dhv * xn, axis=0, keepdims=True)
        if want_dx:
            dxn = dhv * g_ref[...]
            dx = r * (dxn - xn * jnp.mean(dxn * xn, axis=-1, keepdims=True))
            if dres is not None:
                dx = refs[0][...] + dx
            refs[-2][...] = dx.astype(refs[-2].dtype)
            if want16:
                refs[-3][...] = dx

    row = pl.BlockSpec((tm, D), lambda i: (i, 0))
    vec = pl.BlockSpec((1, D), lambda i: (0, 0))
    ins, in_specs = [x, g, dh], [row, vec, row]
    if dres is not None:
        ins.append(dres)
        in_specs.append(row)
    if dep is not None:
        ins.append(dep)
        in_specs.append(ANY)
    shapes, specs = [jax.ShapeDtypeStruct((1, D), F32)], [vec]
    if want_dx:
        if want16:
            shapes.insert(0, jax.ShapeDtypeStruct((S, D), BF16))
            specs.insert(0, row)
        shapes.insert(0, jax.ShapeDtypeStruct((S, D), F32))
        specs.insert(0, row)
    out = pl.pallas_call(body, grid=(S // tm,), name=name, in_specs=in_specs, out_specs=specs,
                         out_shape=shapes, compiler_params=_params("arbitrary"))(*ins)
    return out if want_dx else out[0]


def _shift_down(u, k, row):
    return jnp.where(row >= k, pltpu.roll(u, k, axis=0), 0.0)


def _shift_up(u, k, row):
    S = u.shape[0]
    return jnp.where(row < S - k, pltpu.roll(u, S - k, axis=0), 0.0)


def _conv_specs(S, D, tc):
    nb = D // tc
    col = lambda o: pl.BlockSpec((S, tc), lambda j, o=o: (0, o * nb + j))
    return col, pl.BlockSpec((3, tc), lambda j: (0, j))


def conv_fwd(proj, conv_w, D, name):
    S = proj.shape[0]
    tc = _tile(D, 256)
    col, wspec = _conv_specs(S, D, tc)

    def body(ch_ref, cb_ref, cc_ref, w_ref, a_ref):
        row = lax.broadcasted_iota(jnp.int32, (S, tc), 0)
        u = cc_ref[...].astype(F32) * ch_ref[...].astype(F32)
        w = w_ref[...]
        cv = w[0:1, :] * _shift_down(u, 2, row) + w[1:2, :] * _shift_down(u, 1, row) + w[2:3, :] * u
        a_ref[...] = (cb_ref[...].astype(F32) * cv).astype(BF16)

    return pl.pallas_call(
        body, grid=(D // tc,), name=name, in_specs=[col(0), col(1), col(2), wspec],
        out_specs=pl.BlockSpec((S, tc), lambda j: (0, j)),
        out_shape=jax.ShapeDtypeStruct((S, D), BF16), compiler_params=_params("parallel"),
    )(proj, proj, proj, conv_w)


def conv_bwd(proj, conv_w, da, D, name):
    S = proj.shape[0]
    tc = _tile(D, 256)
    col, wspec = _conv_specs(S, D, tc)
    blk = pl.BlockSpec((S, tc), lambda j: (0, j))

    def body(ch_ref, cb_ref, cc_ref, w_ref, da_ref, dch_ref, dcb_ref, dcc_ref, gw_ref):
        row = lax.broadcasted_iota(jnp.int32, (S, tc), 0)
        ch, cb, cc, dav = [r[...].astype(F32) for r in (ch_ref, cb_ref, cc_ref, da_ref)]
        w = w_ref[...]
        u = cc * ch
        u1, u2 = _shift_down(u, 1, row), _shift_down(u, 2, row)
        cv = w[0:1, :] * u2 + w[1:2, :] * u1 + w[2:3, :] * u
        dcb_ref[...] = (dav * cv).astype(BF16)
        dcv = dav * cb
        gw_ref[0:1, :] = jnp.sum(dcv * u2, axis=0, keepdims=True)
        gw_ref[1:2, :] = jnp.sum(dcv * u1, axis=0, keepdims=True)
        gw_ref[2:3, :] = jnp.sum(dcv * u, axis=0, keepdims=True)
        du = w[2:3, :] * dcv + w[1:2, :] * _shift_up(dcv, 1, row) + w[0:1, :] * _shift_up(dcv, 2, row)
        dcc_ref[...] = (du * ch).astype(BF16)
        dch_ref[...] = (du * cc).astype(BF16)

    act = jax.ShapeDtypeStruct((S, D), BF16)
    return pl.pallas_call(
        body, grid=(D // tc,), name=name, in_specs=[col(0), col(1), col(2), wspec, blk],
        out_specs=[blk, blk, blk, wspec], out_shape=[act, act, act, jax.ShapeDtypeStruct((3, D), F32)],
        compiler_params=_params("parallel"),
    )(proj, proj, proj, conv_w, da)


SB_BQ_FWD = 512
SB_BQ_BWD = 256
SB_BK = 128
SB_GROUP = 4


def _sb_consts(bq):
    lane = lax.broadcasted_iota(jnp.int32, (bq, LANES), 1)
    r = lax.broadcasted_iota(jnp.int32, (SB_BK, SB_BK), 0)
    c = lax.broadcasted_iota(jnp.int32, (SB_BK, SB_BK), 1)
    tri_rev = jnp.where(r > c, 1.0, 0.0).astype(BF16)
    tri_fwd = jnp.where(r < c, 1.0, 0.0).astype(BF16)
    return lane, tri_rev, tri_fwd


def _cumsum2(v, tri):
    hi = v.astype(BF16)
    lo = (v - hi.astype(F32)).astype(BF16)
    part = (lax.dot_general(hi, tri, NN, preferred_element_type=F32)
            + lax.dot_general(lo, tri, NN, preferred_element_type=F32))
    return part, jnp.sum(v, axis=1, keepdims=True)


def _sb_logits(z, past):
    sp = jnp.log(1.0 + jnp.exp(-jnp.abs(z)))
    l = jnp.minimum(z, 0.0) - sp
    m = l - z
    if past is not None:
        m = jnp.where(past, m, 0.0)
    return l, m


def _stack_heads(v, lane):
    return jnp.concatenate([jnp.where(lane < SB_HEAD_DIM, v, 0.0), jnp.where(lane >= SB_HEAD_DIM, v, 0.0)],
                           axis=0).astype(BF16)


def _unstack_heads(v, lane):
    bq = v.shape[0] // 2
    return jnp.where(lane < SB_HEAD_DIM, v[:bq], v[bq:])


def _sb_positions(i, bq):
    r = lax.broadcasted_iota(jnp.int32, (2 * bq, SB_BK), 0)
    trow = i * bq + jnp.where(r >= bq, r - bq, r)
    return trow, lax.broadcasted_iota(jnp.int32, (2 * bq, SB_BK), 1)


def _sb_specs(S, D, bq):
    npair = D // LANES
    qspec = pl.BlockSpec((bq, LANES), lambda p, i: (i, 3 * npair + p))
    kspec = pl.BlockSpec((S, LANES), lambda p, i: (0, 4 * npair + p))
    vspec = pl.BlockSpec((S, LANES), lambda p, i: (0, 5 * npair + p))
    return npair, qspec, kspec, vspec


def sb_fwd(proj, D, name):
    S = proj.shape[0]
    bq = min(SB_BQ_FWD, S)
    nd = bq // SB_BK
    npair, qspec, kspec, vspec = _sb_specs(S, D, bq)
    scale = SB_HEAD_DIM ** -0.5

    def body(q_ref, k_ref, v_ref, o_ref, kb_ref, vb_ref):
        i = pl.program_id(1)

        @pl.when(i == 0)
        def _():
            kb_ref[...] = k_ref[...].astype(BF16)
            vb_ref[...] = v_ref[...].astype(BF16)

        lane, tri_rev, _ = _sb_consts(bq)
        qs = _stack_heads(q_ref[...].astype(F32) * scale, lane)
        trow, scol = _sb_positions(i, bq)

        def steps(j0, carry, n, masked):
            ks = [pl.multiple_of((j0 - t) * SB_BK, SB_BK) for t in range(n)]
            past = [(k + scol) < trow if masked else None for k in ks]
            zs = [lax.dot_general(qs, kb_ref[pl.ds(k, SB_BK), :], NT, preferred_element_type=F32) for k in ks]
            lm = [_sb_logits(z, p) for z, p in zip(zs, past)]
            cs = [_cumsum2(m, tri_rev) for _, m in lm]
            c, acc = carry
            for t in range(n):
                a = jnp.exp(lm[t][0] + (cs[t][0] + c))
                if masked:
                    a = jnp.where(past[t], a, 0.0)
                acc = acc + lax.dot_general(a.astype(BF16), vb_ref[pl.ds(ks[t], SB_BK), :], NN,
                                            preferred_element_type=F32)
                c = c + cs[t][1]
            return c, acc

        carry = (jnp.zeros((2 * bq, 1), F32), jnp.zeros((2 * bq, LANES), F32))
        carry = steps(i * nd + nd - 1, carry, nd, True)
        older = i * nd
        groups = older // SB_GROUP
        carry = lax.fori_loop(
            0, groups, lambda t, cr: steps(older - 1 - t * SB_GROUP, cr, SB_GROUP, False), carry)
        rest = older - groups * SB_GROUP
        carry = lax.fori_loop(0, rest // nd, lambda t, cr: steps(rest - 1 - t * nd, cr, nd, False), carry)
        o_ref[...] = _unstack_heads(carry[1], lane)

    return pl.pallas_call(
        body, grid=(npair, S // bq), name=name, in_specs=[qspec, kspec, vspec],
        out_specs=pl.BlockSpec((bq, LANES), lambda p, i: (i, p)),
        out_shape=jax.ShapeDtypeStruct((S, D), F32),
        scratch_shapes=[pltpu.VMEM((S, LANES), BF16), pltpu.VMEM((S, LANES), BF16)],
        compiler_params=_params("parallel", "arbitrary"),
    )(proj, proj, proj)


def sb_bwd(proj, do, D, name, dep=None):
    S = proj.shape[0]
    bq = min(SB_BQ_BWD, S)
    nd = bq // SB_BK
    nkb = S // SB_BK
    npair, qspec, kspec, vspec = _sb_specs(S, D, bq)
    scale = SB_HEAD_DIM ** -0.5

    def body(q_ref, k_ref, v_ref, do_ref, *refs):
        dq_ref, dk_ref, dv_ref, kb_ref, vb_ref, dk_acc, dv_acc, g_scr, b_scr, a_scr = refs[len(deps):]
        i = pl.program_id(1)

        @pl.when(i == 0)
        def _():
            kb_ref[...] = k_ref[...].astype(BF16)
            vb_ref[...] = v_ref[...].astype(BF16)
            dk_acc[...] = jnp.zeros_like(dk_acc)
            dv_acc[...] = jnp.zeros_like(dv_acc)

        lane, tri_rev, tri_fwd = _sb_consts(bq)
        qs = _stack_heads(q_ref[...].astype(F32) * scale, lane)
        dos = _stack_heads(do_ref[...].astype(F32), lane)
        qs_t, dos_t = qs.T, dos.T
        trow, scol = _sb_positions(i, bq)

        def sweep1(j0, c, n, masked):
            js = [j0 - t for t in range(n)]
            ks = [pl.multiple_of(j * SB_BK, SB_BK) for j in js]
            past = [(k + scol) < trow if masked else None for k in ks]
            zs = [lax.dot_general(qs, kb_ref[pl.ds(k, SB_BK), :], NT, preferred_element_type=F32) for k in ks]
            das = [lax.dot_general(dos, vb_ref[pl.ds(k, SB_BK), :], NT, preferred_element_type=F32) for k in ks]
            lm = [_sb_logits(z, p) for z, p in zip(zs, past)]
            cs = [_cumsum2(m, tri_rev) for _, m in lm]
            for t in range(n):
                b_scr[js[t]] = jnp.exp(lm[t][0]).astype(BF16)
            for t in range(n):
                a = jnp.exp(lm[t][0] + (cs[t][0] + c))
                if masked:
                    a = jnp.where(past[t], a, 0.0)
                g_scr[js[t]] = (das[t] * a).astype(BF16)
                a_scr[js[t]] = a.astype(BF16)
                c = c + cs[t][1]
            return c

        older = i * nd
        groups = older // SB_GROUP
        rest = older - groups * SB_GROUP
        c = jnp.zeros((2 * bq, 1), F32)
        c = sweep1(i * nd + nd - 1, c, nd, True)
        c = lax.fori_loop(0, groups, lambda t, cr: sweep1(older - 1 - t * SB_GROUP, cr, SB_GROUP, False), c)
        lax.fori_loop(0, rest // nd, lambda t, cr: sweep1(rest - 1 - t * nd, cr, nd, False), c)

        def sweep2(j0, carry, n, masked):
            js = [j0 + t for t in range(n)]
            ks = [pl.multiple_of(j * SB_BK, SB_BK) for j in js]
            g16 = [g_scr[j] for j in js]
            gv = [g.astype(F32) for g in g16]
            gs = [(lax.dot_general(g, tri_fwd, NN, preferred_element_type=F32), jnp.sum(v, axis=1, keepdims=True))
                  for g, v in zip(g16, gv)]
            pc, dq = carry
            dzs = []
            for t in range(n):
                dz = gv[t] - b_scr[js[t]].astype(F32) * (gv[t] + (gs[t][0] + pc))
                if masked:
                    dz = jnp.where((ks[t] + scol) < trow, dz, 0.0)
                dzs.append(dz.astype(BF16))
                pc = pc + gs[t][1]
            for t in range(n):
                dq = dq + lax.dot_general(dzs[t], kb_ref[pl.ds(ks[t], SB_BK), :], NN, preferred_element_type=F32)
                dk_acc[js[t]] += lax.dot_general(qs_t, dzs[t], NN, preferred_element_type=F32)
                dv_acc[js[t]] += lax.dot_general(dos_t, a_scr[js[t]], NN, preferred_element_type=F32)
            return pc, dq

        carry = (jnp.zeros((2 * bq, 1), F32), jnp.zeros((2 * bq, LANES), F32))
        carry = lax.fori_loop(0, groups, lambda t, cr: sweep2(t * SB_GROUP, cr, SB_GROUP, False), carry)
        carry = lax.fori_loop(
            0, rest // nd, lambda t, cr: sweep2(groups * SB_GROUP + t * nd, cr, nd, False), carry)
        carry = sweep2(i * nd, carry, nd, True)
        dq_ref[...] = (_unstack_heads(carry[1], lane) * scale).astype(BF16)

        @pl.when(i == pl.num_programs(1) - 1)
        def _():
            for j in range(nkb):
                dk_ref[j * SB_BK:(j + 1) * SB_BK, :] = dk_acc[j].T.astype(BF16)
                dv_ref[j * SB_BK:(j + 1) * SB_BK, :] = dv_acc[j].T.astype(BF16)

    deps = [] if dep is None else [dep]
    full = pl.BlockSpec((S, LANES), lambda p, i: (0, p))
    blk = pl.BlockSpec((bq, LANES), lambda p, i: (i, p))
    act = jax.ShapeDtypeStruct((S, D), BF16)
    return pl.pallas_call(
        body, grid=(npair, S // bq), name=name, in_specs=[qspec, kspec, vspec, blk] + [ANY] * len(deps),
        out_specs=[blk, full, full], out_shape=[act, act, act],
        scratch_shapes=[pltpu.VMEM((S, LANES), BF16), pltpu.VMEM((S, LANES), BF16),
                        pltpu.VMEM((nkb, LANES, SB_BK), F32), pltpu.VMEM((nkb, LANES, SB_BK), F32),
                        pltpu.VMEM((nkb, 2 * bq, SB_BK), BF16), pltpu.VMEM((nkb, 2 * bq, SB_BK), BF16),
                        pltpu.VMEM((nkb, 2 * bq, SB_BK), BF16)],
        compiler_params=_params("parallel", "arbitrary"),
    )(proj, proj, proj, do, *deps)


def _rms_rows(v):
    r = lax.rsqrt(jnp.mean(v * v, axis=-1, keepdims=True) + EPS)
    return v * r, r


def _xa_specs(S, D, M, tq):
    nh = D // X_HEAD_DIM
    qspec = pl.BlockSpec((tq, X_HEAD_DIM), lambda h, i: (i, 6 * nh + h))
    kspec = pl.BlockSpec((M, X_HEAD_DIM), lambda h, i: (0, h))
    vspec = pl.BlockSpec((M, X_HEAD_DIM), lambda h, i: (0, nh + h))
    gspec = pl.BlockSpec((1, X_HEAD_DIM), lambda h, i: (0, 0))
    return nh, qspec, kspec, vspec, gspec


def xa_fwd(proj, kv, gq, gk, D, name):
    S, M = proj.shape[0], kv.shape[0]
    tq = _tile(S, 2048)
    nh, qspec, kspec, vspec, gspec = _xa_specs(S, D, M, tq)
    scale = X_HEAD_DIM ** -0.5

    def body(q_ref, k_ref, v_ref, gq_ref, gk_ref, o_ref):
        qn = _rms_rows(q_ref[...].astype(F32))[0] * gq_ref[...]
        kn = _rms_rows(k_ref[...])[0] * gk_ref[...]
        s = _dot(qn, kn, NT) * scale
        e = jnp.exp(s - jnp.max(s, axis=-1, keepdims=True))
        p = e / jnp.sum(e, axis=-1, keepdims=True)
        o_ref[...] = _dot(p, v_ref[...], NN)

    return pl.pallas_call(
        body, grid=(nh, S // tq), name=name, in_specs=[qspec, kspec, vspec, gspec, gspec],
        out_specs=pl.BlockSpec((tq, X_HEAD_DIM), lambda h, i: (i, h)),
        out_shape=jax.ShapeDtypeStruct((S, D), F32), compiler_params=_params("parallel", "parallel"),
    )(proj, kv, kv, gq, gk)


def xa_bwd(proj, kv, gq, gk, do, D, name):
    S, M = proj.shape[0], kv.shape[0]
    tq = _tile(S, 2048)
    nh, qspec, kspec, vspec, gspec = _xa_specs(S, D, M, tq)
    scale = X_HEAD_DIM ** -0.5

    def body(q_ref, k_ref, v_ref, gq_ref, gk_ref, do_ref, dq_ref, dk_ref, dv_ref, ggq_ref, ggk_ref,
             dkn_acc, dv_acc):
        h, i = pl.program_id(0), pl.program_id(1)

        @pl.when((h == 0) & (i == 0))
        def _():
            ggq_ref[...] = jnp.zeros_like(ggq_ref)
            ggk_ref[...] = jnp.zeros_like(ggk_ref)

        @pl.when(i == 0)
        def _():
            dkn_acc[...] = jnp.zeros_like(dkn_acc)
            dv_acc[...] = jnp.zeros_like(dv_acc)

        gq, gk = gq_ref[...], gk_ref[...]
        qhat, rq = _rms_rows(q_ref[...].astype(F32))
        khat, rk = _rms_rows(k_ref[...])
        qn, kn = qhat * gq, khat * gk
        s = _dot(qn, kn, NT) * scale
        e = jnp.exp(s - jnp.max(s, axis=-1, keepdims=True))
        p = e / jnp.sum(e, axis=-1, keepdims=True)
        dov = do_ref[...]
        dv_acc[...] += _dot(p, dov, TN)
        dp = _dot(dov, v_ref[...], NT)
        ds = (p * (dp - jnp.sum(dp * p, axis=-1, keepdims=True))) * scale
        dqn = _dot(ds, kn, NN)
        dkn_acc[...] += _dot(ds, qn, TN)
        ggq_ref[...] += jnp.sum(dqn * qhat, axis=0, keepdims=True)
        dqh = dqn * gq
        dq_ref[...] = (rq * (dqh - qhat * jnp.mean(dqh * qhat, axis=-1, keepdims=True))).astype(BF16)

        @pl.when(i == pl.num_programs(1) - 1)
        def _():
            dkn = dkn_acc[...]
            ggk_ref[...] += jnp.sum(dkn * khat, axis=0, keepdims=True)
            dkh = dkn * gk
            dk_ref[...] = (rk * (dkh - khat * jnp.mean(dkh * khat, axis=-1, keepdims=True))).astype(BF16)
            dv_ref[...] = dv_acc[...].astype(BF16)

    blk = pl.BlockSpec((tq, X_HEAD_DIM), lambda h, i: (i, h))
    kv_shape = jax.ShapeDtypeStruct((M, 2 * D), BF16)
    gshape = jax.ShapeDtypeStruct((1, X_HEAD_DIM), F32)
    dq, dk, dv, ggq, ggk = pl.pallas_call(
        body, grid=(nh, S // tq), name=name, in_specs=[qspec, kspec, vspec, gspec, gspec, blk],
        out_specs=[blk, kspec, vspec, gspec, gspec],
        out_shape=[jax.ShapeDtypeStruct((S, D), BF16), kv_shape, kv_shape, gshape, gshape],
        scratch_shapes=[pltpu.VMEM((M, X_HEAD_DIM), F32), pltpu.VMEM((M, X_HEAD_DIM), F32)],
        compiler_params=_params("arbitrary", "arbitrary"),
    )(proj, kv, kv, gq, gk, do)
    d_kv = jnp.concatenate([dk[:, :D], dv[:, D:]], axis=1)
    return dq, d_kv, ggq, ggk


def _gate_specs(S, D, tm):
    row = pl.BlockSpec((tm, D), lambda i: (i, 0))
    gate = lambda b: pl.BlockSpec((tm, D), lambda i, b=b: (i, 7 + b))
    return row, gate


def merge_fwd(proj, ys, w_out, x, g_next, D, name):
    S = proj.shape[0]
    tm = _tile(S, 512)
    row, gate = _gate_specs(S, D, tm)

    def body(g0, g1, g2, y0, y1, y2, w_ref, x_ref, gn_ref, m_ref, o_ref, h_ref):
        acc = jax.nn.sigmoid(g0[...].astype(F32)) * y0[...].astype(F32)
        acc = acc + jax.nn.sigmoid(g1[...].astype(F32)) * y1[...].astype(F32)
        acc = acc + jax.nn.sigmoid(g2[...].astype(F32)) * y2[...].astype(F32)
        merged = acc.astype(BF16)
        m_ref[...] = merged
        x1 = x_ref[...] + _dot(merged, w_ref[...], NN)
        o_ref[...] = x1
        h_ref[...] = (_rms_rows(x1)[0] * gn_ref[...]).astype(BF16)

    act = jax.ShapeDtypeStruct((S, D), BF16)
    return pl.pallas_call(
        body, grid=(S // tm,), name=name,
        in_specs=[gate(0), gate(1), gate(2), row, row, row, pl.BlockSpec((D, D), lambda i: (0, 0)), row,
                  pl.BlockSpec((1, D), lambda i: (0, 0))],
        out_specs=[row, row, row], out_shape=[act, jax.ShapeDtypeStruct((S, D), F32), act],
        compiler_params=_params("parallel"),
    )(proj, proj, proj, *ys, w_out, x, g_next)


def merge_bwd(proj, ys, dx1, w_out, D, name):
    S = proj.shape[0]
    tm = _tile(S, 512)
    row, gate = _gate_specs(S, D, tm)

    def body(g0, g1, g2, y0, y1, y2, dx_ref, w_ref, d0, d1, d2, dg_ref):
        dmv = _dot(dx_ref[...], w_ref[...], NT)
        for b, (g_ref, y_ref, d_ref) in enumerate(((g0, y0, d0), (g1, y1, d1), (g2, y2, d2))):
            s = jax.nn.sigmoid(g_ref[...].astype(F32))
            d_ref[...] = (dmv * s).astype(BF16)
            dg_ref[:, b * D:(b + 1) * D] = ((dmv * y_ref[...].astype(F32)) * (s * (1.0 - s))).astype(BF16)

    act = jax.ShapeDtypeStruct((S, D), BF16)
    return pl.pallas_call(
        body, grid=(S // tm,), name=name,
        in_specs=[gate(0), gate(1), gate(2), row, row, row, row, pl.BlockSpec((D, D), lambda i: (0, 0))],
        out_specs=[row, row, row, pl.BlockSpec((tm, 3 * D), lambda i: (i, 0))],
        out_shape=[act, act, act, jax.ShapeDtypeStruct((S, 3 * D), BF16)], compiler_params=_params("parallel"),
    )(proj, proj, proj, *ys, dx1, w_out)


def _rows2d(a):
    return a.reshape(-1, a.shape[-1])


def _ew_call(fn, ins, out_dtypes, name):
    R, C = ins[0].shape
    tr = _tile(R, max(8, (1 << 19) // C))
    spec = pl.BlockSpec((tr, C), lambda i: (i, 0))

    def body(*refs):
        outs = fn(*[r[...] for r in refs[:len(ins)]])
        for o_ref, o in zip(refs[len(ins):], outs):
            o_ref[...] = o.astype(o_ref.dtype)

    return pl.pallas_call(
        body, grid=(R // tr,), name=name, in_specs=[spec] * len(ins), out_specs=[spec] * len(out_dtypes),
        out_shape=[jax.ShapeDtypeStruct((R, C), d) for d in out_dtypes], compiler_params=_params("parallel"),
    )(*ins)


def adamw(w, g, m, v, name):
    def fn(w, g, m, v):
        m = ADAM_B1 * m + (1.0 - ADAM_B1) * g
        v = ADAM_B2 * v + (1.0 - ADAM_B2) * (g * g)
        m_hat = m / (1.0 - ADAM_B1 ** ADAM_STEP)
        v_hat = v / (1.0 - ADAM_B2 ** ADAM_STEP)
        return -ADAM_LR * (m_hat / (jnp.sqrt(v_hat) + ADAM_EPS) + ADAM_WD * w), m, v

    shp = w.shape
    outs = _ew_call(fn, [_rows2d(a) for a in (w, g, m, v)], [F32, F32, F32], name)
    return [o.reshape(shp) for o in outs]


def _placed_call(fn, place, grid, ins, in_specs, out_shape, out_specs, name, dep=None):
    n = len(ins)
    deps = [] if dep is None else [dep]

    def body(place_ref, *refs):
        outs = fn(*[r[...] for r in refs[:n]])
        for o_ref, o in zip(refs[n + len(deps):], outs):
            o_ref[...] = o.astype(o_ref.dtype)

    return pl.pallas_call(
        body, name=name, out_shape=out_shape,
        grid_spec=pltpu.PrefetchScalarGridSpec(
            num_scalar_prefetch=1, grid=grid, in_specs=list(in_specs) + [ANY] * len(deps), out_specs=out_specs),
        compiler_params=_params(*["parallel"] * len(grid)),
    )(place, *ins, *deps)


def _row_tile(R, C):
    return _tile(R, max(16, (1 << 19) // C))


def cast_into_full(w, place, name, dep=None):
    R, C = w.shape
    tr = _row_tile(R, C)
    return _placed_call(
        lambda a: (a,), place, (R // tr,), [w], [pl.BlockSpec((tr, C), lambda i, p: (i, 0))],
        [jax.ShapeDtypeStruct((N_CHIP, R, C), BF16)], [pl.BlockSpec((None, tr, C), lambda i, p: (p[0], i, 0))],
        name, dep=dep)[0]


def pair_sum(g4, got, place, name):
    _, hr, C = got.shape
    tr = _row_tile(hr, C)
    nb = hr // tr
    blk = pl.BlockSpec((None, tr, C), lambda s, i, p: (s, i, 0))
    return _placed_call(
        lambda a, b: (a + b,), place, (N_CHIP, nb), [g4, got],
        [pl.BlockSpec((None, tr, C), lambda s, i, p: (s, p[1] * nb + i, 0)), blk],
        [jax.ShapeDtypeStruct(got.shape, BF16)], [blk], name)[0]


def chip_sum(p32, got, place, name, dep=None):
    _, H, C = p32.shape
    tr = _row_tile(H, C)
    nb = H // tr
    peer = lambda j: pl.BlockSpec((None, tr, C), lambda i, p, j=j: (j, i, 0))
    return _placed_call(
        lambda a, b, c, d: (((a.astype(F32) + b.astype(F32)) + c.astype(F32)) + d.astype(F32),), place, (nb,),
        [p32, got, got, got], [pl.BlockSpec((None, tr, C), lambda i, p: (p[0], i, 0)), peer(0), peer(1), peer(2)],
        [jax.ShapeDtypeStruct((2 * H, C), F32)], [pl.BlockSpec((tr, C), lambda i, p: (p[1] * nb + i, 0))],
        name, dep=dep)[0]


ANY = pl.BlockSpec(memory_space=pl.ANY)
CHIP_FLIPS = ((1, 0), (0, 1), (1, 1))


def _place():
    return lax.axis_index("x"), lax.axis_index("y"), lax.axis_index("c")


def _flip(v, f):
    return 1 - v if f else v


def join_halves(fulls, name):
    n = len(fulls)

    def body(*refs):
        outs = refs[n:2 * n]
        send_sem, recv_sem = refs[2 * n:]
        x, y, c = _place()
        copies = []
        for a in range(n):
            hr = outs[a].shape[0] // 2
            half = outs[a].at[pl.ds(c * hr, hr), :]
            cp = pltpu.make_async_remote_copy(
                src_ref=half, dst_ref=half, send_sem=send_sem.at[a], recv_sem=recv_sem.at[a],
                device_id=(x, y, 1 - c), device_id_type=MESH)
            cp.start()
            copies.append(cp)
        for a, cp in enumerate(copies):
            hr = outs[a].shape[0] // 2
            theirs = outs[a].at[pl.ds((1 - c) * hr, hr), :]
            cp.wait_send()
            pltpu.make_async_remote_copy(
                src_ref=theirs, dst_ref=theirs, send_sem=send_sem.at[a], recv_sem=recv_sem.at[a],
                device_id=(x, y, 1 - c), device_id_type=MESH).wait_recv()

    dma = pltpu.SemaphoreType.DMA
    return pl.pallas_call(
        body, name=name, in_specs=[ANY] * n, out_specs=[ANY] * n,
        out_shape=[jax.ShapeDtypeStruct(f.shape, F32) for f in fulls],
        input_output_aliases={a: a for a in range(n)},
        scratch_shapes=[dma((n,)), dma((n,))],
    )(*fulls)


HBM = pl.BlockSpec(memory_space=pltpu.HBM)
SEM = pl.BlockSpec(memory_space=pltpu.SEMAPHORE)
EFFECT = pltpu.SideEffectType.DATAFLOW_SIDE_EFFECTING


def _in_hbm(a):
    return pltpu.with_memory_space_constraint(a, pltpu.HBM)


def _gather_half(ref, chip_idx, core):
    hr = ref.shape[1] // 2
    return ref.at[chip_idx, pl.ds(core * hr, hr), :]


def gather_forward(fulls, small, name):
    n = len(fulls)

    def body(*refs):
        small_in = refs[n]
        outs, small_out = refs[n + 1:2 * n + 1], refs[2 * n + 1]
        send_sem, recv_sem, sm_send, sm_recv, loc_sem = refs[2 * n + 2:]
        x, y, c = _place()
        mine = 2 * x + y
        chips = [(_flip(x, fx), _flip(y, fy)) for fx, fy in CHIP_FLIPS]
        local = pltpu.make_async_copy(small_in, small_out.at[mine], loc_sem)
        local.start()
        copies = []
        for j, (px, py) in enumerate(chips):
            cp = pltpu.make_async_remote_copy(
                src_ref=small_in, dst_ref=small_out.at[mine], send_sem=sm_send.at[j], recv_sem=sm_recv.at[j],
                device_id=(px, py, c), device_id_type=MESH)
            cp.start()
            copies.append(cp)
        for a in range(n):
            for j, (px, py) in enumerate(chips):
                src = _gather_half(outs[a], 2 * px + py, c)
                cp = pltpu.make_async_remote_copy(
                    src_ref=src, dst_ref=src, send_sem=send_sem.at[3 * a + j], recv_sem=recv_sem.at[3 * a + j],
                    device_id=(x, y, 1 - c), device_id_type=MESH)
                cp.start()
                copies.append(cp)
        for a in range(n):
            for j, (px, py) in enumerate(chips):
                dst = _gather_half(outs[a], 2 * px + py, 1 - c)
                pltpu.make_async_remote_copy(
                    src_ref=dst, dst_ref=dst, send_sem=send_sem.at[3 * a + j], recv_sem=recv_sem.at[3 * a + j],
                    device_id=(x, y, 1 - c), device_id_type=MESH).wait_recv()
        for j, (px, py) in enumerate(chips):
            dst = small_out.at[2 * px + py]
            pltpu.make_async_remote_copy(
                src_ref=dst, dst_ref=dst, send_sem=sm_send.at[j], recv_sem=sm_recv.at[j],
                device_id=(px, py, c), device_id_type=MESH).wait_recv()
        for cp in copies:
            cp.wait_send()
        local.wait()

    dma = pltpu.SemaphoreType.DMA
    out = pl.pallas_call(
        body, name=name, in_specs=[ANY] * (n + 1), out_specs=[ANY] * (n + 1),
        out_shape=[jax.ShapeDtypeStruct(f.shape, f.dtype) for f in fulls]
        + [jax.ShapeDtypeStruct((N_CHIP,) + small.shape, small.dtype)],
        input_output_aliases={a: a for a in range(n)},
        scratch_shapes=[dma((3 * n,)), dma((3 * n,)), dma((3,)), dma((3,)), dma],
    )(*fulls, small)
    return out[:n], out[n]


def _gather_plan(fulls, lands):
    x, y, c = _place()
    mine = 2 * x + y
    return [(_gather_half(f, mine, c), _gather_half(f, mine, c), (_flip(x, fx), _flip(y, fy), c))
            for f in fulls for fx, fy in CHIP_FLIPS]


def _scatter_plan(parts, lands):
    x, y, c = _place()
    plan = []
    for p, l in zip(parts, lands):
        for j, (fx, fy) in enumerate(CHIP_FLIPS):
            px, py = _flip(x, fx), _flip(y, fy)
            plan.append((p.at[2 * px + py], l.at[j], (px, py, c)))
    return plan


def _scatter_lands(parts):
    return [(3,) + p.shape[1:] for p in parts]


def _exchange_plan(grads, lands):
    x, y, c = _place()
    plan = []
    for g, l in zip(grads, lands):
        hr = g.shape[1] // 2
        plan.append((g.at[:, pl.ds((1 - c) * hr, hr), :], l, (x, y, 1 - c)))
    return plan


def _exchange_lands(grads):
    return [(N_CHIP, g.shape[1] // 2, g.shape[2]) for g in grads]


def _forward_plan(fulls, lands):
    x, y, c = _place()
    return [(_gather_half(f, 2 * _flip(x, fx) + _flip(y, fy), c), _gather_half(f, 2 * _flip(x, fx) + _flip(y, fy), c),
             (x, y, 1 - c)) for f in fulls for fx, fy in CHIP_FLIPS]


def split_start(plan, copies, srcs, land_shapes, deps, name):
    n, m = len(srcs), len(land_shapes)
    lands = [lax.empty(s, srcs[0].dtype) for s in land_shapes]

    def body(*refs):
        k0 = n + m + len(deps)
        send_sem, recv_sem = refs[k0], refs[k0 + 1]
        thru, token = refs[k0 + 2:k0 + 2 + n + m], refs[k0 + 2 + n + m]
        for k, (src, dst, dev) in enumerate(plan(thru[:n], thru[n:])):
            pltpu.make_async_remote_copy(src_ref=src, dst_ref=dst, send_sem=send_sem.at[k], recv_sem=recv_sem.at[k],
                                         device_id=dev, device_id_type=MESH).start()
        token[...] = jnp.zeros_like(token)

    dma = pltpu.SemaphoreType.DMA
    arrays = list(srcs) + lands
    out = pl.pallas_call(
        body, name=name,
        out_shape=(dma((copies,)), dma((copies,)), *[pltpu.HBM(a.shape, a.dtype) for a in arrays],
                   jax.ShapeDtypeStruct((8, LANES), F32)),
        in_specs=[HBM] * (n + m) + [ANY] * len(deps),
        out_specs=(SEM, SEM, *[HBM] * (n + m), pl.BlockSpec(memory_space=pltpu.VMEM)),
        input_output_aliases={a: 2 + a for a in range(n + m)},
        compiler_params=pltpu.CompilerParams(has_side_effects=EFFECT),
    )(*[_in_hbm(a) for a in arrays], *deps)
    return (out[0], out[1], list(out[2:2 + n]), list(out[2 + n:2 + n + m])), out[2 + n + m]


def split_wait(plan, handle, after, name):
    send_sem, recv_sem, srcs, lands = handle
    n, m = len(srcs), len(lands)

    def body(*refs):
        send_sem, recv_sem = refs[n + m], refs[n + m + 1]
        thru = refs[n + m + 2 + len(after):]
        for k, (src, dst, dev) in enumerate(plan(thru[:n], thru[n:])):
            cp = pltpu.make_async_remote_copy(src_ref=src, dst_ref=dst, send_sem=send_sem.at[k],
                                              recv_sem=recv_sem.at[k], device_id=dev, device_id_type=MESH)
            cp.wait_send()
            cp.wait_recv()

    arrays = list(srcs) + list(lands)
    out = pl.pallas_call(
        body, name=name, out_shape=tuple(pltpu.HBM(a.shape, a.dtype) for a in arrays),
        in_specs=[HBM] * (n + m) + [SEM, SEM] + [ANY] * len(after), out_specs=tuple([HBM] * (n + m)),
        input_output_aliases={a: a for a in range(n + m)},
        compiler_params=pltpu.CompilerParams(has_side_effects=EFFECT),
    )(*arrays, send_sem, recv_sem, *after)
    return list(out[:n]), list(out[n:])


def allreduce_small(block, name, dep):
    R, C = block.shape

    def body(in_ref, dep_ref, out_ref, slots, send_sem, recv_sem):
        x, y, c = _place()
        me = 4 * x + 2 * y + c
        slots[me] = in_ref[...]
        copies = []
        for r in range(1, 8):
            fx, fy, fc = (r >> 2) & 1, (r >> 1) & 1, r & 1
            cp = pltpu.make_async_remote_copy(
                src_ref=in_ref, dst_ref=slots.at[me], send_sem=send_sem.at[r - 1], recv_sem=recv_sem.at[r - 1],
                device_id=(_flip(x, fx), _flip(y, fy), _flip(c, fc)), device_id_type=MESH)
            cp.start()
            copies.append(cp)
        for cp in copies:
            cp.wait()
        acc = slots[0]
        for d in range(1, 8):
            acc = acc + slots[d]
        out_ref[...] = acc

    vm = pl.BlockSpec(memory_space=pltpu.VMEM)
    dma = pltpu.SemaphoreType.DMA
    return pl.pallas_call(
        body, name=name, in_specs=[vm, ANY], out_specs=vm, out_shape=jax.ShapeDtypeStruct((R, C), F32),
        scratch_shapes=[pltpu.VMEM((8, R, C), F32), dma((7,)), dma((7,))],
    )(block, dep)


def local_step(x, mem, target, g_mix, g_mem, q_norm_g, k_norm_g, g_mlp, conv_w, h, mem_n, w_in, w_in_dep,
               rest_weights, early_grads, mid_grads, late_grads, last_grads):
    S, D = x.shape
    proj = mm_nn_shard(h, w_in, "proj", dep=w_in_dep, out_dtype=BF16)
    a_conv = conv_fwd(proj, conv_w, D, "conv_fwd")
    o_sb = sb_fwd(proj, D, "sb_fwd")
    w_conv_out, w_sb_out, w_mem_kv, w_x_out, w_out, mlp_dep, mlp_weights = rest_weights(o_sb)
    kv = mm_nn_shard(mem_n, w_mem_kv, "kv", dep=mlp_dep)
    o_x = xa_fwd(proj, kv, q_norm_g, k_norm_g, D, "xa_fwd")
    ys = [mm_nn(a_conv, w_conv_out, "y_conv", out_dtype=BF16), mm_nn(o_sb, w_sb_out, "y_sb", out_dtype=BF16),
          mm_nn(o_x, w_x_out, "y_x", out_dtype=BF16)]
    merged, x1, h2 = merge_fwd(proj, ys, w_out, x, g_mlp, D, "merge_x1")
    w_up, w_down = mlp_weights(h2)
    up, act = mm_nn_shard(h2, w_up, "up", relu2=True)
    dy, dy16, loss_parts = mm_nn_loss(act, w_down, x1, target, "x2_loss")
    loss_cols = jnp.sum(loss_parts, axis=0)
    d_up = mm_nt(dy16, w_down, "d_up", up=up, out_dtype=BF16)
    g = {"w_down": mm_tn(act, dy16, "g_w_down")}
    g["w_up"] = mm_tn(h2, d_up, "g_w_up", shard_out=True)
    dh2 = mm_nt_shard(d_up, w_up, "dh2")
    dx1, dx1_16, g["g_mlp"] = rms_bwd(x1, g_mlp, dh2, "rms_mlp_bwd", dres=dy, want16=True)
    g["w_out"] = mm_tn(merged, dx1_16, "g_w_out")
    dy_c, dy_s, dy_x, d_gate = merge_bwd(proj, ys, dx1_16, w_out, D, "merge_bwd")
    g["w_conv_out"] = mm_tn(a_conv, dy_c, "g_w_conv_out")
    g["w_sb_out"] = mm_tn(o_sb, dy_s, "g_w_sb_out")
    g["w_x_out"] = mm_tn(o_x, dy_x, "g_w_x_out")
    d_xq, d_kv, g["q_norm_g"], g["k_norm_g"] = xa_bwd(
        proj, kv, q_norm_g, k_norm_g, mm_nt(dy_x, w_x_out, "d_o_x", out_dtype=BF16), D, "xa_bwd")
    g["w_mem_kv"] = mm_tn(mem_n, d_kv, "g_w_mem_kv", shard_out=True)
    g["g_mem"] = rms_bwd(mem, g_mem, mm_nt_shard(d_kv, w_mem_kv, "d_mem_n"), "rms_mem_bwd", want_dx=False)
    dep = early_grads(g)
    d_a_conv = mm_nt(dy_c, w_conv_out, "d_a_conv", dep=dep, out_dtype=BF16)
    d_ch, d_cb, d_cc, g["conv_w"] = conv_bwd(proj, conv_w, d_a_conv, D, "conv_bwd")
    d_o_sb = mm_nt(dy_s, w_sb_out, "d_o_sb", dep=dep, out_dtype=BF16)
    dq, dk, dv = sb_bwd(proj, d_o_sb, D, "sb_bwd", dep=mid_grads([d_ch, d_o_sb]))
    d_proj = jnp.concatenate([d_ch, d_cb, d_cc, dq, dk, dv, d_xq, d_gate], axis=1)
    g["w_in"] = mm_tn(h, d_proj, "g_w_in", shard_out=True)
    dh = mm_nt_shard(d_proj, w_in, "dh", dep=late_grads(g["w_in"]))
    grad_x, g["g_mix"] = rms_bwd(x, g_mix, dh, "rms_mix_bwd", dres=dx1, dep=last_grads(dh))
    return loss_cols, grad_x, g


BIG = ("w_in", "w_conv_out", "w_sb_out", "w_mem_kv", "w_x_out", "w_out", "w_up", "w_down")
REST = BIG[1:]
COL_SHARDED = ("w_in", "w_mem_kv", "w_up")
WEIGHTS = ("g_mix", "g_mem", "w_in", "conv_w", "w_conv_out", "w_sb_out", "q_norm_g", "k_norm_g",
           "w_mem_kv", "w_x_out", "w_out", "g_mlp", "w_up", "w_down")


def _pack_small(D, g_mix, g_mem, g_mlp, q_norm_g, k_norm_g, conv_w, last):
    qk = jnp.concatenate([q_norm_g, k_norm_g, jnp.zeros((1, D - 2 * X_HEAD_DIM), F32)], axis=1)
    cw = jnp.pad(conv_w, ((0, 0), (0, D - conv_w.shape[1])))
    return jnp.concatenate([g_mix, g_mem, g_mlp, qk, cw, last], axis=0)


def kernel(x, mem, g_mix, g_mem, w_in, conv_w, w_conv_out, w_sb_out, q_norm_g, k_norm_g, w_mem_kv, w_x_out, w_out, g_mlp, w_up, w_down, loss_target, m_g_mix, m_g_mem, m_w_in, m_conv_w, m_w_conv_out, m_w_sb_out, m_q_norm_g, m_k_norm_g, m_w_mem_kv, m_w_x_out, m_w_out, m_g_mlp, m_w_up, m_w_down, v_g_mix, v_g_mem, v_w_in, v_conv_w, v_w_conv_out, v_w_sb_out, v_q_norm_g, v_k_norm_g, v_w_mem_kv, v_w_x_out, v_w_out, v_g_mlp, v_w_up, v_w_down):
    S, D = x.shape[1], x.shape[2]
    w = dict(g_mix=g_mix, g_mem=g_mem, w_in=w_in, conv_w=conv_w, w_conv_out=w_conv_out, w_sb_out=w_sb_out,
             q_norm_g=q_norm_g, k_norm_g=k_norm_g, w_mem_kv=w_mem_kv, w_x_out=w_x_out, w_out=w_out,
             g_mlp=g_mlp, w_up=w_up, w_down=w_down)
    m = dict(g_mix=m_g_mix, g_mem=m_g_mem, w_in=m_w_in, conv_w=m_conv_w, w_conv_out=m_w_conv_out,
             w_sb_out=m_w_sb_out, q_norm_g=m_q_norm_g, k_norm_g=m_k_norm_g, w_mem_kv=m_w_mem_kv,
             w_x_out=m_w_x_out, w_out=m_w_out, g_mlp=m_g_mlp, w_up=m_w_up, w_down=m_w_down)
    v = dict(g_mix=v_g_mix, g_mem=v_g_mem, w_in=v_w_in, conv_w=v_conv_w, w_conv_out=v_w_conv_out,
             w_sb_out=v_w_sb_out, q_norm_g=v_q_norm_g, k_norm_g=v_k_norm_g, w_mem_kv=v_w_mem_kv,
             w_x_out=v_w_x_out, w_out=v_w_out, g_mlp=v_g_mlp, w_up=v_w_up, w_down=v_w_down)
    chip = 2 * lax.axis_index("x") + lax.axis_index("y")
    cs = conv_w.shape[2]

    place = jnp.stack([chip, lax.axis_index("c")]).astype(jnp.int32)
    cw_block = jnp.pad(conv_w[0], ((0, 5), (0, 0)))
    handle, token = split_start(_gather_plan, 3, [cast_into_full(w["w_in"][0], place, "cast_w_in")], [], [],
                                "gather_w_in_start")
    rest16 = [cast_into_full(w[k][0], place, "cast_" + k, dep=token) for k in REST]
    h = rms_fwd(x[0], g_mix, "rms_mix", dep=token)
    mem_n = rms_fwd(mem[0], g_mem, "rms_mem", dep=token)
    landed, _ = split_wait(_gather_plan, handle, [*rest16, h, mem_n], "gather_w_in_wait")
    (w_in_full,), cw_all = gather_forward(landed, cw_block, "gather_w_in_forward")
    conv_full = jnp.concatenate([cw_all[p, :3] for p in range(N_CHIP)], axis=1)
    rest_handle, rest_token = split_start(_gather_plan, 3 * len(REST), rest16, [], [w_in_full], "gather_rest_start")

    def layout(k, a):
        return a if k in COL_SHARDED else a.reshape(-1, a.shape[-1])

    def rest_weights(after):
        landed, _ = split_wait(_gather_plan, rest_handle, [after], "gather_rest_wait")
        first = gather_forward(landed[:-2], cw_block, "gather_rest_forward")[0]
        mlp_handle, token = split_start(_forward_plan, 6, landed[-2:], [], [first[0]], "forward_mlp_start")

        def mlp_weights(after):
            both, _ = split_wait(_forward_plan, mlp_handle, [after], "forward_mlp_wait")
            return [layout(k, a) for k, a in zip(REST[-2:], both)]

        return [layout(k, a) for k, a in zip(REST[:-2], first)] + [token, mlp_weights]

    def blocks(k, a):
        return a if k in COL_SHARDED else a.reshape(N_CHIP, -1, a.shape[-1])

    early = {}

    def early_grads(g):
        early["g4"] = [blocks(k, g[k]) for k in REST]
        early["swap"], token = split_start(_exchange_plan, len(REST), early["g4"], _exchange_lands(early["g4"]),
                                           [g["g_mem"]], "exchange_rest_start")
        return token

    def mid_grads(after):
        g4, got = split_wait(_exchange_plan, early["swap"], after, "exchange_rest_wait")
        p16 = [pair_sum(a, b, place, "pair_sum_" + k) for k, a, b in zip(REST, g4, got)]
        early["fly"], token = split_start(_scatter_plan, 3 * len(REST), p16, _scatter_lands(p16), [],
                                          "scatter_rest_start")
        return token

    late = {}

    def late_grads(gw):
        late["swap"], token = split_start(_exchange_plan, 1, [gw], _exchange_lands([gw]), [], "exchange_w_in_start")
        return token

    def last_grads(after):
        (gw,), (got,) = split_wait(_exchange_plan, late["swap"], [after], "exchange_w_in_wait")
        p16 = [pair_sum(gw, got, place, "pair_sum_w_in")]
        late["fly"], token = split_start(_scatter_plan, 3, p16, _scatter_lands(p16), [], "scatter_w_in_start")
        return token

    loss_cols, grad_x, g = local_step(
        x[0], mem[0], loss_target[0], g_mix, g_mem, q_norm_g, k_norm_g, g_mlp, conv_full, h, mem_n,
        w_in_full, rest_token, rest_weights, early_grads, mid_grads, late_grads, last_grads)
    token = g["g_mix"]

    p16_rest, got_rest = split_wait(_scatter_plan, early["fly"], [token], "scatter_rest_wait")
    gsum, delta, new_m, new_v = {}, {}, {}, {}
    halves = [chip_sum(p, b, place, "chip_sum_" + k, dep=token) for k, p, b in zip(REST, p16_rest, got_rest)]
    for k, a in zip(REST, join_halves(halves, "join_halves_rest")):
        gsum[k] = a[None]
        delta[k], new_m[k], new_v[k] = adamw(w[k], gsum[k], m[k], v[k], "adamw_" + k)
    p16_in, got_in = split_wait(_scatter_plan, late["fly"], [new_v[k] for k in REST], "scatter_w_in_wait")

    small = allreduce_small(
        _pack_small(D, g["g_mix"], g["g_mem"], g["g_mlp"], g["q_norm_g"], g["k_norm_g"], g["conv_w"], loss_cols),
        "allreduce_small", dep=got_in[0])
    loss = (0.5 / D) * jnp.sum(small[7])
    gsum.update({"g_mix": small[0:1], "g_mem": small[1:2], "g_mlp": small[2:3],
                 "q_norm_g": small[3:4, :X_HEAD_DIM], "k_norm_g": small[3:4, X_HEAD_DIM:2 * X_HEAD_DIM],
                 "conv_w": lax.dynamic_slice(small[4:7], (0, chip * cs), (3, cs))[None]})
    half_in = chip_sum(p16_in[0], got_in[0], place, "chip_sum_w_in")
    gsum["w_in"] = join_halves([half_in], "join_halves_w_in")[0][None]
    delta["w_in"], new_m["w_in"], new_v["w_in"] = adamw(w["w_in"], gsum["w_in"], m["w_in"], v["w_in"], "adamw_w_in")
    small_names = ("g_mix", "g_mem", "g_mlp", "q_norm_g", "k_norm_g", "conv_w")
    zero_row = jnp.zeros((1, D), F32)
    packed = [_pack_small(D, *[t[k] if k != "conv_w" else t[k][0] for k in small_names], zero_row)
              for t in (w, gsum, m, v)]
    sm = adamw(*packed, "adamw_small")
    for t, block in zip((delta, new_m, new_v), sm):
        t["g_mix"], t["g_mem"], t["g_mlp"] = block[0:1], block[1:2], block[2:3]
        t["q_norm_g"], t["k_norm_g"] = block[3:4, :X_HEAD_DIM], block[3:4, X_HEAD_DIM:2 * X_HEAD_DIM]
        t["conv_w"] = block[4:7, :cs][None]

    return (loss, grad_x[None], *[gsum[k] for k in WEIGHTS], *[delta[k] for k in WEIGHTS],
            *[new_m[k] for k in WEIGHTS], *[new_v[k] for k in WEIGHTS])
```

```python
import jax
import jax.numpy as jnp
from jax import lax
from jax.experimental import pallas as pl
from jax.experimental.pallas import tpu as pltpu

F32 = jnp.float32
BF16 = jnp.bfloat16
EPS = 1e-6
N_CHIP = 4
SB_HEAD_DIM = 64
X_HEAD_DIM = 256
LANES = 128
VMEM_LIMIT = 56 * 1024 * 1024
ADAM_LR, ADAM_B1, ADAM_B2, ADAM_EPS, ADAM_WD, ADAM_STEP = 0.001, 0.9, 0.999, 1e-8, 0.01, 10
MESH = pl.DeviceIdType.MESH


def _params(*sem):
    return pltpu.CompilerParams(dimension_semantics=sem, vmem_limit_bytes=VMEM_LIMIT)


def _tile(n, pref):
    if n <= pref:
        return n
    t = 1 << (pref.bit_length() - 1)
    while n % t:
        t //= 2
    return t


NN = (((1,), (0,)), ((), ()))
NT = (((1,), (1,)), ((), ()))
TN = (((0,), (0,)), ((), ()))


def _dot(a, b, dims):
    return lax.dot_general(a.astype(BF16), b.astype(BF16), dims, preferred_element_type=F32)


def mm_nn_shard(a, g, name, relu2=False, dep=None, out_dtype=F32):
    M, K = a.shape
    _, _, Ns = g.shape
    tm, tn = _tile(M, 2048), _tile(Ns, 512)
    nb = Ns // tn

    def body(a_ref, b_ref, *o_refs):
        o_refs = o_refs[len(deps):]
        acc = _dot(a_ref[...], b_ref[...], NN)
        if relu2:
            acc = jnp.maximum(acc, 0.0)
            o_refs[1][...] = (acc * acc).astype(BF16)
        o_refs[0][...] = acc.astype(o_refs[0].dtype)

    o_spec = pl.BlockSpec((tm, tn), lambda i, j: (i, j))
    shapes = [jax.ShapeDtypeStruct((M, N_CHIP * Ns), BF16 if relu2 else out_dtype)]
    specs = [o_spec]
    if relu2:
        shapes.append(jax.ShapeDtypeStruct((M, N_CHIP * Ns), BF16))
        specs.append(o_spec)
    deps = [] if dep is None else [dep]
    out = pl.pallas_call(
        body, grid=(M // tm, N_CHIP * nb), name=name,
        in_specs=[pl.BlockSpec((tm, K), lambda i, j: (i, 0)),
                  pl.BlockSpec((None, K, tn), lambda i, j: (j // nb, 0, j % nb))] + [ANY] * len(deps),
        out_specs=specs, out_shape=shapes, compiler_params=_params("parallel", "parallel"),
    )(a, g, *deps)
    return out if relu2 else out[0]


def mm_nn_loss(a, w, res, target, name):
    M, K = a.shape
    N = w.shape[1]
    tm, tn = _tile(M, 1024), _tile(N, 512)

    def body(a_ref, b_ref, r_ref, t_ref, dy_ref, dy16_ref, l_ref):
        e = (r_ref[...] + _dot(a_ref[...], b_ref[...], NN)) - t_ref[...]
        dy = e * (1.0 / N)
        dy_ref[...] = dy
        dy16_ref[...] = dy.astype(BF16)
        l_ref[...] = jnp.sum(e * e, axis=0, keepdims=True)

    o_spec = pl.BlockSpec((tm, tn), lambda i, j: (i, j))
    return pl.pallas_call(
        body, grid=(M // tm, N // tn), name=name,
        in_specs=[pl.BlockSpec((tm, K), lambda i, j: (i, 0)), pl.BlockSpec((K, tn), lambda i, j: (0, j)),
                  o_spec, o_spec],
        out_specs=[o_spec, o_spec, pl.BlockSpec((None, 1, tn), lambda i, j: (i, 0, j))],
        out_shape=[jax.ShapeDtypeStruct((M, N), F32), jax.ShapeDtypeStruct((M, N), BF16),
                   jax.ShapeDtypeStruct((M // tm, 1, N), F32)],
        compiler_params=_params("parallel", "parallel"),
    )(a, w, res, target)


def mm_nt(a, w, name, up=None, out_dtype=F32, dep=None):
    M, N = a.shape
    R = w.shape[0]
    tm, tr = _tile(M, 2048), _tile(R, 512)

    def body(a_ref, b_ref, *refs):
        acc = _dot(a_ref[...], b_ref[...], NT)
        if up is not None:
            acc = acc * (2.0 * jnp.maximum(refs[0][...].astype(F32), 0.0))
        refs[-1][...] = acc.astype(out_dtype)

    o_spec = pl.BlockSpec((tm, tr), lambda i, j: (i, j))
    ins = [a, w] + ([up] if up is not None else []) + ([dep] if dep is not None else [])
    return pl.pallas_call(
        body, grid=(M // tm, R // tr), name=name,
        in_specs=[pl.BlockSpec((tm, N), lambda i, j: (i, 0)), pl.BlockSpec((tr, N), lambda i, j: (j, 0))]
        + ([o_spec] if up is not None else []) + ([ANY] if dep is not None else []),
        out_specs=o_spec, out_shape=jax.ShapeDtypeStruct((M, R), out_dtype),
        compiler_params=_params("parallel", "parallel"),
    )(*ins)


def mm_nt_shard(a, g, name, out_dtype=F32, dep=None):
    deps = [] if dep is None else [dep]
    M = a.shape[0]
    _, R, Ns = g.shape
    tm, tr, tk = _tile(M, 1024), _tile(R, 1024), _tile(Ns, 2560)
    nb = Ns // tk
    nk = N_CHIP * nb

    def body(a_ref, b_ref, *refs):
        o_ref, acc_ref = refs[len(deps):]
        k = pl.program_id(2)

        @pl.when(k == 0)
        def _():
            acc_ref[...] = jnp.zeros_like(acc_ref)

        acc_ref[...] += _dot(a_ref[...], b_ref[...], NT)

        @pl.when(k == nk - 1)
        def _():
            o_ref[...] = acc_ref[...].astype(out_dtype)

    return pl.pallas_call(
        body, grid=(M // tm, R // tr, nk), name=name,
        in_specs=[pl.BlockSpec((tm, tk), lambda i, j, k: (i, k)),
                  pl.BlockSpec((None, tr, tk), lambda i, j, k: (k // nb, j, k % nb))] + [ANY] * len(deps),
        out_specs=pl.BlockSpec((tm, tr), lambda i, j, k: (i, j)),
        out_shape=jax.ShapeDtypeStruct((M, R), out_dtype),
        scratch_shapes=[pltpu.VMEM((tm, tr), F32)],
        compiler_params=_params("parallel", "parallel", "arbitrary"),
    )(a, g, *deps)


def mm_tn(a, b, name, shard_out=False):
    S, M = a.shape
    N = b.shape[1]
    Ns = N // N_CHIP if shard_out else N
    tm, tn = _tile(M, 1024), _tile(Ns, 512)
    nb = Ns // tn

    def body(a_ref, b_ref, o_ref):
        o_ref[...] = _dot(a_ref[...], b_ref[...], TN)

    if shard_out:
        o_spec = pl.BlockSpec((None, tm, tn), lambda i, j: (j // nb, i, j % nb))
        o_shape = jax.ShapeDtypeStruct((N_CHIP, M, Ns), F32)
    else:
        o_spec = pl.BlockSpec((tm, tn), lambda i, j: (i, j))
        o_shape = jax.ShapeDtypeStruct((M, N), F32)
    return pl.pallas_call(
        body, grid=(M // tm, N // tn), name=name,
        in_specs=[pl.BlockSpec((S, tm), lambda i, j: (0, i)), pl.BlockSpec((S, tn), lambda i, j: (0, j))],
        out_specs=o_spec, out_shape=o_shape, compiler_params=_params("parallel", "parallel"),
    )(a, b)


def rms_fwd(x, g, name, dep=None):
    S, D = x.shape
    tm = _tile(S, 512)
    deps = [] if dep is None else [dep]

    def body(x_ref, g_ref, *refs):
        xv = x_ref[...]
        r = lax.rsqrt(jnp.mean(xv * xv, axis=-1, keepdims=True) + EPS)
        refs[-1][...] = ((xv * r) * g_ref[...]).astype(BF16)

    return pl.pallas_call(
        body, grid=(S // tm,), name=name,
        in_specs=[pl.BlockSpec((tm, D), lambda i: (i, 0)), pl.BlockSpec((1, D), lambda i: (0, 0))]
        + [ANY] * len(deps),
        out_specs=pl.BlockSpec((tm, D), lambda i: (i, 0)),
        out_shape=jax.ShapeDtypeStruct((S, D), BF16), compiler_params=_params("parallel"),
    )(x, g, *deps)


def rms_bwd(x, g, dh, name, dres=None, want_dx=True, want16=False, dep=None):
    S, D = x.shape
    tm = _tile(S, 512)

    def body(x_ref, g_ref, dh_ref, *refs):
        i = pl.program_id(0)
        xv = x_ref[...]
        r = lax.rsqrt(jnp.mean(xv * xv, axis=-1, keepdims=True) + EPS)
        xn = xv * r
        dhv = dh_ref[...].astype(F32)
        gg_ref = refs[-1]

        @pl.when(i == 0)
        def _():
            gg_ref[...] = jnp.zeros_like(gg_ref)

        gg_ref[...] += jnp.sum(dhv * xn, axis=0, keepdims=True)
        if want_dx:
            dxn = dhv * g_ref[...]
            dx = r * (dxn - xn * jnp.mean(dxn * xn, axis=-1, keepdims=True))
            if dres is not None:
                dx = refs[0][...] + dx
            refs[-2][...] = dx.astype(refs[-2].dtype)
            if want16:
                refs[-3][...] = dx

    row = pl.BlockSpec((tm, D), lambda i: (i, 0))
    vec = pl.BlockSpec((1, D), lambda i: (0, 0))
    ins, in_specs = [x, g, dh], [row, vec, row]
    if dres is not None:
        ins.append(dres)
        in_specs.append(row)
    if dep is not None:
        ins.append(dep)
        in_specs.append(ANY)
    shapes, specs = [jax.ShapeDtypeStruct((1, D), F32)], [vec]
    if want_dx:
        if want16:
            shapes.insert(0, jax.ShapeDtypeStruct((S, D), BF16))
            specs.insert(0, row)
        shapes.insert(0, jax.ShapeDtypeStruct((S, D), F32))
        specs.insert(0, row)
    out = pl.pallas_call(body, grid=(S // tm,), name=name, in_specs=in_specs, out_specs=specs,
                         out_shape=shapes, compiler_params=_params("arbitrary"))(*ins)
    return out if want_dx else out[0]


def _shift_down(u, k, row):
    return jnp.where(row >= k, pltpu.roll(u, k, axis=0), 0.0)


def _shift_up(u, k, row):
    S = u.shape[0]
    return jnp.where(row < S - k, pltpu.roll(u, S - k, axis=0), 0.0)


def _conv_specs(S, D, tc):
    nb = D // tc
    col = lambda o: pl.BlockSpec((S, tc), lambda j, o=o: (0, o * nb + j))
    return col, pl.BlockSpec((3, tc), lambda j: (0, j))


def conv_fwd(proj, conv_w, D, name):
    S = proj.shape[0]
    tc = _tile(D, 256)
    col, wspec = _conv_specs(S, D, tc)

    def body(ch_ref, cb_ref, cc_ref, w_ref, a_ref):
        row = lax.broadcasted_iota(jnp.int32, (S, tc), 0)
        u = cc_ref[...].astype(F32) * ch_ref[...].astype(F32)
        w = w_ref[...]
        cv = w[0:1, :] * _shift_down(u, 2, row) + w[1:2, :] * _shift_down(u, 1, row) + w[2:3, :] * u
        a_ref[...] = (cb_ref[...].astype(F32) * cv).astype(BF16)

    return pl.pallas_call(
        body, grid=(D // tc,), name=name, in_specs=[col(0), col(1), col(2), wspec],
        out_specs=pl.BlockSpec((S, tc), lambda j: (0, j)),
        out_shape=jax.ShapeDtypeStruct((S, D), BF16), compiler_params=_params("parallel"),
    )(proj, proj, proj, conv_w)


def conv_bwd(proj, conv_w, da, D, name):
    S = proj.shape[0]
    tc = _tile(D, 256)
    col, wspec = _conv_specs(S, D, tc)
    blk = pl.BlockSpec((S, tc), lambda j: (0, j))

    def body(ch_ref, cb_ref, cc_ref, w_ref, da_ref, dch_ref, dcb_ref, dcc_ref, gw_ref):
        row = lax.broadcasted_iota(jnp.int32, (S, tc), 0)
        ch, cb, cc, dav = [r[...].astype(F32) for r in (ch_ref, cb_ref, cc_ref, da_ref)]
        w = w_ref[...]
        u = cc * ch
        u1, u2 = _shift_down(u, 1, row), _shift_down(u, 2, row)
        cv = w[0:1, :] * u2 + w[1:2, :] * u1 + w[2:3, :] * u
        dcb_ref[...] = (dav * cv).astype(BF16)
        dcv = dav * cb
        gw_ref[0:1, :] = jnp.sum(dcv * u2, axis=0, keepdims=True)
        gw_ref[1:2, :] = jnp.sum(dcv * u1, axis=0, keepdims=True)
        gw_ref[2:3, :] = jnp.sum(dcv * u, axis=0, keepdims=True)
        du = w[2:3, :] * dcv + w[1:2, :] * _shift_up(dcv, 1, row) + w[0:1, :] * _shift_up(dcv, 2, row)
        dcc_ref[...] = (du * ch).astype(BF16)
        dch_ref[...] = (du * cc).astype(BF16)

    act = jax.ShapeDtypeStruct((S, D), BF16)
    return pl.pallas_call(
        body, grid=(D // tc,), name=name, in_specs=[col(0), col(1), col(2), wspec, blk],
        out_specs=[blk, blk, blk, wspec], out_shape=[act, act, act, jax.ShapeDtypeStruct((3, D), F32)],
        compiler_params=_params("parallel"),
    )(proj, proj, proj, conv_w, da)


SB_BQ_FWD = 512
SB_BQ_BWD = 256
SB_BK = 128
SB_GROUP = 4


def _sb_consts(bq):
    lane = lax.broadcasted_iota(jnp.int32, (bq, LANES), 1)
    r = lax.broadcasted_iota(jnp.int32, (SB_BK, SB_BK), 0)
    c = lax.broadcasted_iota(jnp.int32, (SB_BK, SB_BK), 1)
    tri_rev = jnp.where(r > c, 1.0, 0.0).astype(BF16)
    tri_fwd = jnp.where(r < c, 1.0, 0.0).astype(BF16)
    return lane, tri_rev, tri_fwd


def _cumsum2(v, tri):
    hi = v.astype(BF16)
    lo = (v - hi.astype(F32)).astype(BF16)
    part = (lax.dot_general(hi, tri, NN, preferred_element_type=F32)
            + lax.dot_general(lo, tri, NN, preferred_element_type=F32))
    return part, jnp.sum(v, axis=1, keepdims=True)


def _sb_logits(z, past):
    sp = jnp.log(1.0 + jnp.exp(-jnp.abs(z)))
    l = jnp.minimum(z, 0.0) - sp
    m = l - z
    if past is not None:
        m = jnp.where(past, m, 0.0)
    return l, m


def _stack_heads(v, lane):
    return jnp.concatenate([jnp.where(lane < SB_HEAD_DIM, v, 0.0), jnp.where(lane >= SB_HEAD_DIM, v, 0.0)],
                           axis=0).astype(BF16)


def _unstack_heads(v, lane):
    bq = v.shape[0] // 2
    return jnp.where(lane < SB_HEAD_DIM, v[:bq], v[bq:])


def _sb_positions(i, bq):
    r = lax.broadcasted_iota(jnp.int32, (2 * bq, SB_BK), 0)
    trow = i * bq + jnp.where(r >= bq, r - bq, r)
    return trow, lax.broadcasted_iota(jnp.int32, (2 * bq, SB_BK), 1)


def _sb_specs(S, D, bq):
    npair = D // LANES
    qspec = pl.BlockSpec((bq, LANES), lambda p, i: (i, 3 * npair + p))
    kspec = pl.BlockSpec((S, LANES), lambda p, i: (0, 4 * npair + p))
    vspec = pl.BlockSpec((S, LANES), lambda p, i: (0, 5 * npair + p))
    return npair, qspec, kspec, vspec


def sb_fwd(proj, D, name):
    S = proj.shape[0]
    bq = min(SB_BQ_FWD, S)
    nd = bq // SB_BK
    npair, qspec, kspec, vspec = _sb_specs(S, D, bq)
    scale = SB_HEAD_DIM ** -0.5

    def body(q_ref, k_ref, v_ref, o_ref, kb_ref, vb_ref):
        i = pl.program_id(1)

        @pl.when(i == 0)
        def _():
            kb_ref[...] = k_ref[...].astype(BF16)
            vb_ref[...] = v_ref[...].astype(BF16)

        lane, tri_rev, _ = _sb_consts(bq)
        qs = _stack_heads(q_ref[...].astype(F32) * scale, lane)
        trow, scol = _sb_positions(i, bq)

        def steps(j0, carry, n, masked):
            ks = [pl.multiple_of((j0 - t) * SB_BK, SB_BK) for t in range(n)]
            past = [(k + scol) < trow if masked else None for k in ks]
            zs = [lax.dot_general(qs, kb_ref[pl.ds(k, SB_BK), :], NT, preferred_element_type=F32) for k in ks]
            lm = [_sb_logits(z, p) for z, p in zip(zs, past)]
            cs = [_cumsum2(m, tri_rev) for _, m in lm]
            c, acc = carry
            for t in range(n):
                a = jnp.exp(lm[t][0] + (cs[t][0] + c))
                if masked:
                    a = jnp.where(past[t], a, 0.0)
                acc = acc + lax.dot_general(a.astype(BF16), vb_ref[pl.ds(ks[t], SB_BK), :], NN,
                                            preferred_element_type=F32)
                c = c + cs[t][1]
            return c, acc

        carry = (jnp.zeros((2 * bq, 1), F32), jnp.zeros((2 * bq, LANES), F32))
        carry = steps(i * nd + nd - 1, carry, nd, True)
        older = i * nd
        groups = older // SB_GROUP
        carry = lax.fori_loop(
            0, groups, lambda t, cr: steps(older - 1 - t * SB_GROUP, cr, SB_GROUP, False), carry)
        rest = older - groups * SB_GROUP
        carry = lax.fori_loop(0, rest // nd, lambda t, cr: steps(rest - 1 - t * nd, cr, nd, False), carry)
        o_ref[...] = _unstack_heads(carry[1], lane)

    return pl.pallas_call(
        body, grid=(npair, S // bq), name=name, in_specs=[qspec, kspec, vspec],
        out_specs=pl.BlockSpec((bq, LANES), lambda p, i: (i, p)),
        out_shape=jax.ShapeDtypeStruct((S, D), F32),
        scratch_shapes=[pltpu.VMEM((S, LANES), BF16), pltpu.VMEM((S, LANES), BF16)],
        compiler_params=_params("parallel", "arbitrary"),
    )(proj, proj, proj)


def sb_bwd(proj, do, D, name, dep=None):
    S = proj.shape[0]
    bq = min(SB_BQ_BWD, S)
    nd = bq // SB_BK
    nkb = S // SB_BK
    npair, qspec, kspec, vspec = _sb_specs(S, D, bq)
    scale = SB_HEAD_DIM ** -0.5

    def body(q_ref, k_ref, v_ref, do_ref, *refs):
        dq_ref, dk_ref, dv_ref, kb_ref, vb_ref, dk_acc, dv_acc, g_scr, b_scr, a_scr = refs[len(deps):]
        i = pl.program_id(1)

        @pl.when(i == 0)
        def _():
            kb_ref[...] = k_ref[...].astype(BF16)
            vb_ref[...] = v_ref[...].astype(BF16)
            dk_acc[...] = jnp.zeros_like(dk_acc)
            dv_acc[...] = jnp.zeros_like(dv_acc)

        lane, tri_rev, tri_fwd = _sb_consts(bq)
        qs = _stack_heads(q_ref[...].astype(F32) * scale, lane)
        dos = _stack_heads(do_ref[...].astype(F32), lane)
        qs_t, dos_t = qs.T, dos.T
        trow, scol = _sb_positions(i, bq)

        def sweep1(j0, c, n, masked):
            js = [j0 - t for t in range(n)]
            ks = [pl.multiple_of(j * SB_BK, SB_BK) for j in js]
            past = [(k + scol) < trow if masked else None for k in ks]
            zs = [lax.dot_general(qs, kb_ref[pl.ds(k, SB_BK), :], NT, preferred_element_type=F32) for k in ks]
            das = [lax.dot_general(dos, vb_ref[pl.ds(k, SB_BK), :], NT, preferred_element_type=F32) for k in ks]
            lm = [_sb_logits(z, p) for z, p in zip(zs, past)]
            cs = [_cumsum2(m, tri_rev) for _, m in lm]
            for t in range(n):
                b_scr[js[t]] = jnp.exp(lm[t][0]).astype(BF16)
            for t in range(n):
                a = jnp.exp(lm[t][0] + (cs[t][0] + c))
                if masked:
                    a = jnp.where(past[t], a, 0.0)
                g_scr[js[t]] = (das[t] * a).astype(BF16)
                a_scr[js[t]] = a.astype(BF16)
                c = c + cs[t][1]
            return c

        older = i * nd
        groups = older // SB_GROUP
        rest = older - groups * SB_GROUP
        c = jnp.zeros((2 * bq, 1), F32)
        c = sweep1(i * nd + nd - 1, c, nd, True)
        c = lax.fori_loop(0, groups, lambda t, cr: sweep1(older - 1 - t * SB_GROUP, cr, SB_GROUP, False), c)
        lax.fori_loop(0, rest // nd, lambda t, cr: sweep1(rest - 1 - t * nd, cr, nd, False), c)

        def sweep2(j0, carry, n, masked):
            js = [j0 + t for t in range(n)]
            ks = [pl.multiple_of(j * SB_BK, SB_BK) for j in js]
            g16 = [g_scr[j] for j in js]
            gv = [g.astype(F32) for g in g16]
            gs = [(lax.dot_general(g, tri_fwd, NN, preferred_element_type=F32), jnp.sum(v, axis=1, keepdims=True))
                  for g, v in zip(g16, gv)]
            pc, dq = carry
            dzs = []
            for t in range(n):
                dz = gv[t] - b_scr[js[t]].astype(F32) * (gv[t] + (gs[t][0] + pc))
                if masked:
                    dz = jnp.where((ks[t] + scol) < trow, dz, 0.0)
                dzs.append(dz.astype(BF16))
                pc = pc + gs[t][1]
            for t in range(n):
                dq = dq + lax.dot_general(dzs[t], kb_ref[pl.ds(ks[t], SB_BK), :], NN, preferred_element_type=F32)
                dk_acc[js[t]] += lax.dot_general(qs_t, dzs[t], NN, preferred_element_type=F32)
                dv_acc[js[t]] += lax.dot_general(dos_t, a_scr[js[t]], NN, preferred_element_type=F32)
            return pc, dq

        carry = (jnp.zeros((2 * bq, 1), F32), jnp.zeros((2 * bq, LANES), F32))
        carry = lax.fori_loop(0, groups, lambda t, cr: sweep2(t * SB_GROUP, cr, SB_GROUP, False), carry)
        carry = lax.fori_loop(
            0, rest // nd, lambda t, cr: sweep2(groups * SB_GROUP + t * nd, cr, nd, False), carry)
        carry = sweep2(i * nd, carry, nd, True)
        dq_ref[...] = (_unstack_heads(carry[1], lane) * scale).astype(BF16)

        @pl.when(i == pl.num_programs(1) - 1)
        def _():
            for j in range(nkb):
                dk_ref[j * SB_BK:(j + 1) * SB_BK, :] = dk_acc[j].T.astype(BF16)
                dv_ref[j * SB_BK:(j + 1) * SB_BK, :] = dv_acc[j].T.astype(BF16)

    deps = [] if dep is None else [dep]
    full = pl.BlockSpec((S, LANES), lambda p, i: (0, p))
    blk = pl.BlockSpec((bq, LANES), lambda p, i: (i, p))
    act = jax.ShapeDtypeStruct((S, D), BF16)
    return pl.pallas_call(
        body, grid=(npair, S // bq), name=name, in_specs=[qspec, kspec, vspec, blk] + [ANY] * len(deps),
        out_specs=[blk, full, full], out_shape=[act, act, act],
        scratch_shapes=[pltpu.VMEM((S, LANES), BF16), pltpu.VMEM((S, LANES), BF16),
                        pltpu.VMEM((nkb, LANES, SB_BK), F32), pltpu.VMEM((nkb, LANES, SB_BK), F32),
                        pltpu.VMEM((nkb, 2 * bq, SB_BK), BF16), pltpu.VMEM((nkb, 2 * bq, SB_BK), BF16),
                        pltpu.VMEM((nkb, 2 * bq, SB_BK), BF16)],
        compiler_params=_params("parallel", "arbitrary"),
    )(proj, proj, proj, do, *deps)


def _rms_rows(v):
    r = lax.rsqrt(jnp.mean(v * v, axis=-1, keepdims=True) + EPS)
    return v * r, r


def _xa_specs(S, D, M, tq):
    nh = D // X_HEAD_DIM
    qspec = pl.BlockSpec((tq, X_HEAD_DIM), lambda h, i: (i, 6 * nh + h))
    kspec = pl.BlockSpec((M, X_HEAD_DIM), lambda h, i: (0, h))
    vspec = pl.BlockSpec((M, X_HEAD_DIM), lambda h, i: (0, nh + h))
    gspec = pl.BlockSpec((1, X_HEAD_DIM), lambda h, i: (0, 0))
    return nh, qspec, kspec, vspec, gspec


def xa_fwd(proj, kv, gq, gk, D, name):
    S, M = proj.shape[0], kv.shape[0]
    tq = _tile(S, 2048)
    nh, qspec, kspec, vspec, gspec = _xa_specs(S, D, M, tq)
    scale = X_HEAD_DIM ** -0.5

    def body(q_ref, k_ref, v_ref, gq_ref, gk_ref, o_ref):
        qn = _rms_rows(q_ref[...].astype(F32))[0] * gq_ref[...]
        kn = _rms_rows(k_ref[...])[0] * gk_ref[...]
        s = _dot(qn, kn, NT) * scale
        e = jnp.exp(s - jnp.max(s, axis=-1, keepdims=True))
        p = e / jnp.sum(e, axis=-1, keepdims=True)
        o_ref[...] = _dot(p, v_ref[...], NN)

    return pl.pallas_call(
        body, grid=(nh, S // tq), name=name, in_specs=[qspec, kspec, vspec, gspec, gspec],
        out_specs=pl.BlockSpec((tq, X_HEAD_DIM), lambda h, i: (i, h)),
        out_shape=jax.ShapeDtypeStruct((S, D), F32), compiler_params=_params("parallel", "parallel"),
    )(proj, kv, kv, gq, gk)


def xa_bwd(proj, kv, gq, gk, do, D, name):
    S, M = proj.shape[0], kv.shape[0]
    tq = _tile(S, 2048)
    nh, qspec, kspec, vspec, gspec = _xa_specs(S, D, M, tq)
    scale = X_HEAD_DIM ** -0.5

    def body(q_ref, k_ref, v_ref, gq_ref, gk_ref, do_ref, dq_ref, dk_ref, dv_ref, ggq_ref, ggk_ref,
             dkn_acc, dv_acc):
        h, i = pl.program_id(0), pl.program_id(1)

        @pl.when((h == 0) & (i == 0))
        def _():
            ggq_ref[...] = jnp.zeros_like(ggq_ref)
            ggk_ref[...] = jnp.zeros_like(ggk_ref)

        @pl.when(i == 0)
        def _():
            dkn_acc[...] = jnp.zeros_like(dkn_acc)
            dv_acc[...] = jnp.zeros_like(dv_acc)

        gq, gk = gq_ref[...], gk_ref[...]
        qhat, rq = _rms_rows(q_ref[...].astype(F32))
        khat, rk = _rms_rows(k_ref[...])
        qn, kn = qhat * gq, khat * gk
        s = _dot(qn, kn, NT) * scale
        e = jnp.exp(s - jnp.max(s, axis=-1, keepdims=True))
        p = e / jnp.sum(e, axis=-1, keepdims=True)
        dov = do_ref[...]
        dv_acc[...] += _dot(p, dov, TN)
        dp = _dot(dov, v_ref[...], NT)
        ds = (p * (dp - jnp.sum(dp * p, axis=-1, keepdims=True))) * scale
        dqn = _dot(ds, kn, NN)
        dkn_acc[...] += _dot(ds, qn, TN)
        ggq_ref[...] += jnp.sum(dqn * qhat, axis=0, keepdims=True)
        dqh = dqn * gq
        dq_ref[...] = (rq * (dqh - qhat * jnp.mean(dqh * qhat, axis=-1, keepdims=True))).astype(BF16)

        @pl.when(i == pl.num_programs(1) - 1)
        def _():
            dkn = dkn_acc[...]
            ggk_ref[...] += jnp.sum(dkn * khat, axis=0, keepdims=True)
            dkh = dkn * gk
            dk_ref[...] = (rk * (dkh - khat * jnp.mean(dkh * khat, axis=-1, keepdims=True))).astype(BF16)
            dv_ref[...] = dv_acc[...].astype(BF16)

    blk = pl.BlockSpec((tq, X_HEAD_DIM), lambda h, i: (i, h))
    kv_shape = jax.ShapeDtypeStruct((M, 2 * D), BF16)
    gshape = jax.ShapeDtypeStruct((1, X_HEAD_DIM), F32)
    dq, dk, dv, ggq, ggk = pl.pallas_call(
        body, grid=(nh, S // tq), name=name, in_specs=[qspec, kspec, vspec, gspec, gspec, blk],
        out_specs=[blk, kspec, vspec, gspec, gspec],
        out_shape=[jax.ShapeDtypeStruct((S, D), BF16), kv_shape, kv_shape, gshape, gshape],
        scratch_shapes=[pltpu.VMEM((M, X_HEAD_DIM), F32), pltpu.VMEM((M, X_HEAD_DIM), F32)],
        compiler_params=_params("arbitrary", "arbitrary"),
    )(proj, kv, kv, gq, gk, do)
    d_kv = jnp.concatenate([dk[:, :D], dv[:, D:]], axis=1)
    return dq, d_kv, ggq, ggk


def _gate_specs(S, D, tm):
    row = pl.BlockSpec((tm, D), lambda i: (i, 0))
    gate = lambda b: pl.BlockSpec((tm, D), lambda i, b=b: (i, 7 + b))
    return row, gate


def merge_fwd(proj, branches, w_branches, w_out, x, g_next, D, name):
    S = proj.shape[0]
    tm = _tile(S, 256)
    row, gate = _gate_specs(S, D, tm)
    mat = pl.BlockSpec((D, D), lambda i: (0, 0))

    def body(g0, g1, g2, b0, b1, b2, w0, w1, w2, w_ref, x_ref, gn_ref, y0, y1, y2, m_ref, o_ref, h_ref):
        acc = None
        for g_ref, b_ref, wb_ref, y_ref in ((g0, b0, w0, y0), (g1, b1, w1, y1), (g2, b2, w2, y2)):
            y = _dot(b_ref[...], wb_ref[...], NN).astype(BF16)
            y_ref[...] = y
            term = jax.nn.sigmoid(g_ref[...].astype(F32)) * y.astype(F32)
            acc = term if acc is None else acc + term
        merged = acc.astype(BF16)
        m_ref[...] = merged
        x1 = x_ref[...] + _dot(merged, w_ref[...], NN)
        o_ref[...] = x1
        h_ref[...] = (_rms_rows(x1)[0] * gn_ref[...]).astype(BF16)

    act = jax.ShapeDtypeStruct((S, D), BF16)
    out = pl.pallas_call(
        body, grid=(S // tm,), name=name,
        in_specs=[gate(0), gate(1), gate(2), row, row, row, mat, mat, mat, mat, row,
                  pl.BlockSpec((1, D), lambda i: (0, 0))],
        out_specs=[row] * 6, out_shape=[act, act, act, act, jax.ShapeDtypeStruct((S, D), F32), act],
        compiler_params=_params("parallel"),
    )(proj, proj, proj, *branches, *w_branches, w_out, x, g_next)
    return out[:3], out[3], out[4], out[5]


def merge_bwd(proj, ys, dx1, w_out, D, name):
    S = proj.shape[0]
    tm = _tile(S, 512)
    row, gate = _gate_specs(S, D, tm)

    def body(g0, g1, g2, y0, y1, y2, dx_ref, w_ref, d0, d1, d2, dg_ref):
        dmv = _dot(dx_ref[...], w_ref[...], NT)
        for b, (g_ref, y_ref, d_ref) in enumerate(((g0, y0, d0), (g1, y1, d1), (g2, y2, d2))):
            s = jax.nn.sigmoid(g_ref[...].astype(F32))
            d_ref[...] = (dmv * s).astype(BF16)
            dg_ref[:, b * D:(b + 1) * D] = ((dmv * y_ref[...].astype(F32)) * (s * (1.0 - s))).astype(BF16)

    act = jax.ShapeDtypeStruct((S, D), BF16)
    return pl.pallas_call(
        body, grid=(S // tm,), name=name,
        in_specs=[gate(0), gate(1), gate(2), row, row, row, row, pl.BlockSpec((D, D), lambda i: (0, 0))],
        out_specs=[row, row, row, pl.BlockSpec((tm, 3 * D), lambda i: (i, 0))],
        out_shape=[act, act, act, jax.ShapeDtypeStruct((S, 3 * D), BF16)], compiler_params=_params("parallel"),
    )(proj, proj, proj, *ys, dx1, w_out)


def _rows2d(a):
    return a.reshape(-1, a.shape[-1])


def _ew_call(fn, ins, out_dtypes, name):
    R, C = ins[0].shape
    tr = _tile(R, max(8, (1 << 19) // C))
    spec = pl.BlockSpec((tr, C), lambda i: (i, 0))

    def body(*refs):
        outs = fn(*[r[...] for r in refs[:len(ins)]])
        for o_ref, o in zip(refs[len(ins):], outs):
            o_ref[...] = o.astype(o_ref.dtype)

    return pl.pallas_call(
        body, grid=(R // tr,), name=name, in_specs=[spec] * len(ins), out_specs=[spec] * len(out_dtypes),
        out_shape=[jax.ShapeDtypeStruct((R, C), d) for d in out_dtypes], compiler_params=_params("parallel"),
    )(*ins)


def adamw(w, g, m, v, name):
    def fn(w, g, m, v):
        m = ADAM_B1 * m + (1.0 - ADAM_B1) * g
        v = ADAM_B2 * v + (1.0 - ADAM_B2) * (g * g)
        m_hat = m / (1.0 - ADAM_B1 ** ADAM_STEP)
        v_hat = v / (1.0 - ADAM_B2 ** ADAM_STEP)
        return -ADAM_LR * (m_hat / (jnp.sqrt(v_hat) + ADAM_EPS) + ADAM_WD * w), m, v

    shp = w.shape
    outs = _ew_call(fn, [_rows2d(a) for a in (w, g, m, v)], [F32, F32, F32], name)
    return [o.reshape(shp) for o in outs]


def _placed_call(fn, place, grid, ins, in_specs, out_shape, out_specs, name, dep=None):
    n = len(ins)
    deps = [] if dep is None else [dep]

    def body(place_ref, *refs):
        outs = fn(*[r[...] for r in refs[:n]])
        for o_ref, o in zip(refs[n + len(deps):], outs):
            o_ref[...] = o.astype(o_ref.dtype)

    return pl.pallas_call(
        body, name=name, out_shape=out_shape,
        grid_spec=pltpu.PrefetchScalarGridSpec(
            num_scalar_prefetch=1, grid=grid, in_specs=list(in_specs) + [ANY] * len(deps), out_specs=out_specs),
        compiler_params=_params(*["parallel"] * len(grid)),
    )(place, *ins, *deps)


def _row_tile(R, C):
    return _tile(R, max(16, (1 << 19) // C))


def cast_into_full(w, place, name, dep=None):
    R, C = w.shape
    tr = _row_tile(R, C)
    return _placed_call(
        lambda a: (a,), place, (R // tr,), [w], [pl.BlockSpec((tr, C), lambda i, p: (i, 0))],
        [jax.ShapeDtypeStruct((N_CHIP, R, C), BF16)], [pl.BlockSpec((None, tr, C), lambda i, p: (p[0], i, 0))],
        name, dep=dep)[0]


def pair_sum(g4, got, place, name):
    _, hr, C = got.shape
    tr = _row_tile(hr, C)
    nb = hr // tr
    blk = pl.BlockSpec((None, tr, C), lambda s, i, p: (s, i, 0))
    return _placed_call(
        lambda a, b: (a + b,), place, (N_CHIP, nb), [g4, got],
        [pl.BlockSpec((None, tr, C), lambda s, i, p: (s, p[1] * nb + i, 0)), blk],
        [jax.ShapeDtypeStruct(got.shape, BF16)], [blk], name)[0]


def chip_sum(p32, got, place, name, dep=None):
    _, H, C = p32.shape
    tr = _row_tile(H, C)
    nb = H // tr
    peer = lambda j: pl.BlockSpec((None, tr, C), lambda i, p, j=j: (j, i, 0))
    return _placed_call(
        lambda a, b, c, d: (((a.astype(F32) + b.astype(F32)) + c.astype(F32)) + d.astype(F32),), place, (nb,),
        [p32, got, got, got], [pl.BlockSpec((None, tr, C), lambda i, p: (p[0], i, 0)), peer(0), peer(1), peer(2)],
        [jax.ShapeDtypeStruct((2 * H, C), F32)], [pl.BlockSpec((tr, C), lambda i, p: (p[1] * nb + i, 0))],
        name, dep=dep)[0]


ANY = pl.BlockSpec(memory_space=pl.ANY)
CHIP_FLIPS = ((1, 0), (0, 1), (1, 1))


def _place():
    return lax.axis_index("x"), lax.axis_index("y"), lax.axis_index("c")


def _flip(v, f):
    return 1 - v if f else v


def join_halves(fulls, name):
    n = len(fulls)

    def body(*refs):
        outs = refs[n:2 * n]
        send_sem, recv_sem = refs[2 * n:]
        x, y, c = _place()
        copies = []
        for a in range(n):
            hr = outs[a].shape[0] // 2
            half = outs[a].at[pl.ds(c * hr, hr), :]
            cp = pltpu.make_async_remote_copy(
                src_ref=half, dst_ref=half, send_sem=send_sem.at[a], recv_sem=recv_sem.at[a],
                device_id=(x, y, 1 - c), device_id_type=MESH)
            cp.start()
            copies.append(cp)
        for a, cp in enumerate(copies):
            hr = outs[a].shape[0] // 2
            theirs = outs[a].at[pl.ds((1 - c) * hr, hr), :]
            cp.wait_send()
            pltpu.make_async_remote_copy(
                src_ref=theirs, dst_ref=theirs, send_sem=send_sem.at[a], recv_sem=recv_sem.at[a],
                device_id=(x, y, 1 - c), device_id_type=MESH).wait_recv()

    dma = pltpu.SemaphoreType.DMA
    return pl.pallas_call(
        body, name=name, in_specs=[ANY] * n, out_specs=[ANY] * n,
        out_shape=[jax.ShapeDtypeStruct(f.shape, F32) for f in fulls],
        input_output_aliases={a: a for a in range(n)},
        scratch_shapes=[dma((n,)), dma((n,))],
    )(*fulls)


HBM = pl.BlockSpec(memory_space=pltpu.HBM)
SEM = pl.BlockSpec(memory_space=pltpu.SEMAPHORE)
EFFECT = pltpu.SideEffectType.DATAFLOW_SIDE_EFFECTING


def _in_hbm(a):
    return pltpu.with_memory_space_constraint(a, pltpu.HBM)


def _gather_half(ref, chip_idx, core):
    hr = ref.shape[1] // 2
    return ref.at[chip_idx, pl.ds(core * hr, hr), :]


def gather_forward(fulls, small, name):
    n = len(fulls)

    def body(*refs):
        small_in = refs[n]
        outs, small_out = refs[n + 1:2 * n + 1], refs[2 * n + 1]
        send_sem, recv_sem, sm_send, sm_recv, loc_sem = refs[2 * n + 2:]
        x, y, c = _place()
        mine = 2 * x + y
        chips = [(_flip(x, fx), _flip(y, fy)) for fx, fy in CHIP_FLIPS]
        local = pltpu.make_async_copy(small_in, small_out.at[mine], loc_sem)
        local.start()
        copies = []
        for j, (px, py) in enumerate(chips):
            cp = pltpu.make_async_remote_copy(
                src_ref=small_in, dst_ref=small_out.at[mine], send_sem=sm_send.at[j], recv_sem=sm_recv.at[j],
                device_id=(px, py, c), device_id_type=MESH)
            cp.start()
            copies.append(cp)
        for a in range(n):
            for j, (px, py) in enumerate(chips):
                src = _gather_half(outs[a], 2 * px + py, c)
                cp = pltpu.make_async_remote_copy(
                    src_ref=src, dst_ref=src, send_sem=send_sem.at[3 * a + j], recv_sem=recv_sem.at[3 * a + j],
                    device_id=(x, y, 1 - c), device_id_type=MESH)
                cp.start()
                copies.append(cp)
        for a in range(n):
            for j, (px, py) in enumerate(chips):
                dst = _gather_half(outs[a], 2 * px + py, 1 - c)
                pltpu.make_async_remote_copy(
                    src_ref=dst, dst_ref=dst, send_sem=send_sem.at[3 * a + j], recv_sem=recv_sem.at[3 * a + j],
                    device_id=(x, y, 1 - c), device_id_type=MESH).wait_recv()
        for j, (px, py) in enumerate(chips):
            dst = small_out.at[2 * px + py]
            pltpu.make_async_remote_copy(
                src_ref=dst, dst_ref=dst, send_sem=sm_send.at[j], recv_sem=sm_recv.at[j],
                device_id=(px, py, c), device_id_type=MESH).wait_recv()
        for cp in copies:
            cp.wait_send()
        local.wait()

    dma = pltpu.SemaphoreType.DMA
    out = pl.pallas_call(
        body, name=name, in_specs=[ANY] * (n + 1), out_specs=[ANY] * (n + 1),
        out_shape=[jax.ShapeDtypeStruct(f.shape, f.dtype) for f in fulls]
        + [jax.ShapeDtypeStruct((N_CHIP,) + small.shape, small.dtype)],
        input_output_aliases={a: a for a in range(n)},
        scratch_shapes=[dma((3 * n,)), dma((3 * n,)), dma((3,)), dma((3,)), dma],
    )(*fulls, small)
    return out[:n], out[n]


def _gather_plan(fulls, lands):
    x, y, c = _place()
    mine = 2 * x + y
    return [(_gather_half(f, mine, c), _gather_half(f, mine, c), (_flip(x, fx), _flip(y, fy), c))
            for f in fulls for fx, fy in CHIP_FLIPS]


def _scatter_plan(parts, lands):
    x, y, c = _place()
    plan = []
    for p, l in zip(parts, lands):
        for j, (fx, fy) in enumerate(CHIP_FLIPS):
            px, py = _flip(x, fx), _flip(y, fy)
            plan.append((p.at[2 * px + py], l.at[j], (px, py, c)))
    return plan


def _scatter_lands(parts):
    return [(3,) + p.shape[1:] for p in parts]


def _exchange_plan(grads, lands):
    x, y, c = _place()
    plan = []
    for g, l in zip(grads, lands):
        hr = g.shape[1] // 2
        plan.append((g.at[:, pl.ds((1 - c) * hr, hr), :], l, (x, y, 1 - c)))
    return plan


def _exchange_lands(grads):
    return [(N_CHIP, g.shape[1] // 2, g.shape[2]) for g in grads]


def _forward_plan(fulls, lands):
    x, y, c = _place()
    return [(_gather_half(f, 2 * _flip(x, fx) + _flip(y, fy), c), _gather_half(f, 2 * _flip(x, fx) + _flip(y, fy), c),
             (x, y, 1 - c)) for f in fulls for fx, fy in CHIP_FLIPS]


def split_start(plan, copies, srcs, land_shapes, deps, name):
    n, m = len(srcs), len(land_shapes)
    lands = [lax.empty(s, srcs[0].dtype) for s in land_shapes]

    def body(*refs):
        k0 = n + m + len(deps)
        send_sem, recv_sem = refs[k0], refs[k0 + 1]
        thru, token = refs[k0 + 2:k0 + 2 + n + m], refs[k0 + 2 + n + m]
        for k, (src, dst, dev) in enumerate(plan(thru[:n], thru[n:])):
            pltpu.make_async_remote_copy(src_ref=src, dst_ref=dst, send_sem=send_sem.at[k], recv_sem=recv_sem.at[k],
                                         device_id=dev, device_id_type=MESH).start()
        token[...] = jnp.zeros_like(token)

    dma = pltpu.SemaphoreType.DMA
    arrays = list(srcs) + lands
    out = pl.pallas_call(
        body, name=name,
        out_shape=(dma((copies,)), dma((copies,)), *[pltpu.HBM(a.shape, a.dtype) for a in arrays],
                   jax.ShapeDtypeStruct((8, LANES), F32)),
        in_specs=[HBM] * (n + m) + [ANY] * len(deps),
        out_specs=(SEM, SEM, *[HBM] * (n + m), pl.BlockSpec(memory_space=pltpu.VMEM)),
        input_output_aliases={a: 2 + a for a in range(n + m)},
        compiler_params=pltpu.CompilerParams(has_side_effects=EFFECT),
    )(*[_in_hbm(a) for a in arrays], *deps)
    return (out[0], out[1], list(out[2:2 + n]), list(out[2 + n:2 + n + m])), out[2 + n + m]


def split_wait(plan, handle, after, name):
    send_sem, recv_sem, srcs, lands = handle
    n, m = len(srcs), len(lands)

    def body(*refs):
        send_sem, recv_sem = refs[n + m], refs[n + m + 1]
        thru = refs[n + m + 2 + len(after):]
        for k, (src, dst, dev) in enumerate(plan(thru[:n], thru[n:])):
            cp = pltpu.make_async_remote_copy(src_ref=src, dst_ref=dst, send_sem=send_sem.at[k],
                                              recv_sem=recv_sem.at[k], device_id=dev, device_id_type=MESH)
            cp.wait_send()
            cp.wait_recv()

    arrays = list(srcs) + list(lands)
    out = pl.pallas_call(
        body, name=name, out_shape=tuple(pltpu.HBM(a.shape, a.dtype) for a in arrays),
        in_specs=[HBM] * (n + m) + [SEM, SEM] + [ANY] * len(after), out_specs=tuple([HBM] * (n + m)),
        input_output_aliases={a: a for a in range(n + m)},
        compiler_params=pltpu.CompilerParams(has_side_effects=EFFECT),
    )(*arrays, send_sem, recv_sem, *after)
    return list(out[:n]), list(out[n:])


def allreduce_small(block, name, dep):
    R, C = block.shape

    def body(in_ref, dep_ref, out_ref, slots, send_sem, recv_sem):
        x, y, c = _place()
        me = 4 * x + 2 * y + c
        slots[me] = in_ref[...]
        copies = []
        for r in range(1, 8):
            fx, fy, fc = (r >> 2) & 1, (r >> 1) & 1, r & 1
            cp = pltpu.make_async_remote_copy(
                src_ref=in_ref, dst_ref=slots.at[me], send_sem=send_sem.at[r - 1], recv_sem=recv_sem.at[r - 1],
                device_id=(_flip(x, fx), _flip(y, fy), _flip(c, fc)), device_id_type=MESH)
            cp.start()
            copies.append(cp)
        for cp in copies:
            cp.wait()
        acc = slots[0]
        for d in range(1, 8):
            acc = acc + slots[d]
        out_ref[...] = acc

    vm = pl.BlockSpec(memory_space=pltpu.VMEM)
    dma = pltpu.SemaphoreType.DMA
    return pl.pallas_call(
        body, name=name, in_specs=[vm, ANY], out_specs=vm, out_shape=jax.ShapeDtypeStruct((R, C), F32),
        scratch_shapes=[pltpu.VMEM((8, R, C), F32), dma((7,)), dma((7,))],
    )(block, dep)


def local_step(x, mem, target, g_mix, g_mem, q_norm_g, k_norm_g, g_mlp, conv_w, h, mem_n, w_in, w_in_dep,
               rest_weights, early_grads, mid_grads, late_grads, last_grads):
    S, D = x.shape
    proj = mm_nn_shard(h, w_in, "proj", dep=w_in_dep, out_dtype=BF16)
    a_conv = conv_fwd(proj, conv_w, D, "conv_fwd")
    o_sb = sb_fwd(proj, D, "sb_fwd")
    w_conv_out, w_sb_out, w_mem_kv, w_x_out, w_out, mlp_dep, mlp_weights = rest_weights(o_sb)
    kv = mm_nn_shard(mem_n, w_mem_kv, "kv", dep=mlp_dep)
    o_x = xa_fwd(proj, kv, q_norm_g, k_norm_g, D, "xa_fwd")
    ys, merged, x1, h2 = merge_fwd(proj, (a_conv, o_sb, o_x), (w_conv_out, w_sb_out, w_x_out), w_out, x, g_mlp, D,
                                   "merge_x1")
    w_up, w_down = mlp_weights(h2)
    up, act = mm_nn_shard(h2, w_up, "up", relu2=True)
    dy, dy16, loss_parts = mm_nn_loss(act, w_down, x1, target, "x2_loss")
    loss_cols = jnp.sum(loss_parts, axis=0)
    d_up = mm_nt(dy16, w_down, "d_up", up=up, out_dtype=BF16)
    g = {"w_down": mm_tn(act, dy16, "g_w_down")}
    g["w_up"] = mm_tn(h2, d_up, "g_w_up", shard_out=True)
    dh2 = mm_nt_shard(d_up, w_up, "dh2")
    dx1, dx1_16, g["g_mlp"] = rms_bwd(x1, g_mlp, dh2, "rms_mlp_bwd", dres=dy, want16=True)
    g["w_out"] = mm_tn(merged, dx1_16, "g_w_out")
    dy_c, dy_s, dy_x, d_gate = merge_bwd(proj, ys, dx1_16, w_out, D, "merge_bwd")
    g["w_conv_out"] = mm_tn(a_conv, dy_c, "g_w_conv_out")
    g["w_sb_out"] = mm_tn(o_sb, dy_s, "g_w_sb_out")
    g["w_x_out"] = mm_tn(o_x, dy_x, "g_w_x_out")
    d_xq, d_kv, g["q_norm_g"], g["k_norm_g"] = xa_bwd(
        proj, kv, q_norm_g, k_norm_g, mm_nt(dy_x, w_x_out, "d_o_x", out_dtype=BF16), D, "xa_bwd")
    g["w_mem_kv"] = mm_tn(mem_n, d_kv, "g_w_mem_kv", shard_out=True)
    g["g_mem"] = rms_bwd(mem, g_mem, mm_nt_shard(d_kv, w_mem_kv, "d_mem_n"), "rms_mem_bwd", want_dx=False)
    dep = early_grads(g)
    d_a_conv = mm_nt(dy_c, w_conv_out, "d_a_conv", dep=dep, out_dtype=BF16)
    d_ch, d_cb, d_cc, g["conv_w"] = conv_bwd(proj, conv_w, d_a_conv, D, "conv_bwd")
    d_o_sb = mm_nt(dy_s, w_sb_out, "d_o_sb", dep=dep, out_dtype=BF16)
    dq, dk, dv = sb_bwd(proj, d_o_sb, D, "sb_bwd", dep=mid_grads([d_ch, d_o_sb]))
    d_proj = jnp.concatenate([d_ch, d_cb, d_cc, dq, dk, dv, d_xq, d_gate], axis=1)
    g["w_in"] = mm_tn(h, d_proj, "g_w_in", shard_out=True)
    dh = mm_nt_shard(d_proj, w_in, "dh", dep=late_grads(g["w_in"]))
    grad_x, g["g_mix"] = rms_bwd(x, g_mix, dh, "rms_mix_bwd", dres=dx1, dep=last_grads(dh))
    return loss_cols, grad_x, g


BIG = ("w_in", "w_conv_out", "w_sb_out", "w_mem_kv", "w_x_out", "w_out", "w_up", "w_down")
REST = BIG[1:]
COL_SHARDED = ("w_in", "w_mem_kv", "w_up")
WEIGHTS = ("g_mix", "g_mem", "w_in", "conv_w", "w_conv_out", "w_sb_out", "q_norm_g", "k_norm_g",
           "w_mem_kv", "w_x_out", "w_out", "g_mlp", "w_up", "w_down")


def _pack_small(D, g_mix, g_mem, g_mlp, q_norm_g, k_norm_g, conv_w, last):
    qk = jnp.concatenate([q_norm_g, k_norm_g, jnp.zeros((1, D - 2 * X_HEAD_DIM), F32)], axis=1)
    cw = jnp.pad(conv_w, ((0, 0), (0, D - conv_w.shape[1])))
    return jnp.concatenate([g_mix, g_mem, g_mlp, qk, cw, last], axis=0)


def kernel(x, mem, g_mix, g_mem, w_in, conv_w, w_conv_out, w_sb_out, q_norm_g, k_norm_g, w_mem_kv, w_x_out, w_out, g_mlp, w_up, w_down, loss_target, m_g_mix, m_g_mem, m_w_in, m_conv_w, m_w_conv_out, m_w_sb_out, m_q_norm_g, m_k_norm_g, m_w_mem_kv, m_w_x_out, m_w_out, m_g_mlp, m_w_up, m_w_down, v_g_mix, v_g_mem, v_w_in, v_conv_w, v_w_conv_out, v_w_sb_out, v_q_norm_g, v_k_norm_g, v_w_mem_kv, v_w_x_out, v_w_out, v_g_mlp, v_w_up, v_w_down):
    S, D = x.shape[1], x.shape[2]
    w = dict(g_mix=g_mix, g_mem=g_mem, w_in=w_in, conv_w=conv_w, w_conv_out=w_conv_out, w_sb_out=w_sb_out,
             q_norm_g=q_norm_g, k_norm_g=k_norm_g, w_mem_kv=w_mem_kv, w_x_out=w_x_out, w_out=w_out,
             g_mlp=g_mlp, w_up=w_up, w_down=w_down)
    m = dict(g_mix=m_g_mix, g_mem=m_g_mem, w_in=m_w_in, conv_w=m_conv_w, w_conv_out=m_w_conv_out,
             w_sb_out=m_w_sb_out, q_norm_g=m_q_norm_g, k_norm_g=m_k_norm_g, w_mem_kv=m_w_mem_kv,
             w_x_out=m_w_x_out, w_out=m_w_out, g_mlp=m_g_mlp, w_up=m_w_up, w_down=m_w_down)
    v = dict(g_mix=v_g_mix, g_mem=v_g_mem, w_in=v_w_in, conv_w=v_conv_w, w_conv_out=v_w_conv_out,
             w_sb_out=v_w_sb_out, q_norm_g=v_q_norm_g, k_norm_g=v_k_norm_g, w_mem_kv=v_w_mem_kv,
             w_x_out=v_w_x_out, w_out=v_w_out, g_mlp=v_g_mlp, w_up=v_w_up, w_down=v_w_down)
    chip = 2 * lax.axis_index("x") + lax.axis_index("y")
    cs = conv_w.shape[2]

    place = jnp.stack([chip, lax.axis_index("c")]).astype(jnp.int32)
    cw_block = jnp.pad(conv_w[0], ((0, 5), (0, 0)))
    handle, token = split_start(_gather_plan, 3, [cast_into_full(w["w_in"][0], place, "cast_w_in")], [], [],
                                "gather_w_in_start")
    rest16 = [cast_into_full(w[k][0], place, "cast_" + k, dep=token) for k in REST]
    h = rms_fwd(x[0], g_mix, "rms_mix", dep=token)
    mem_n = rms_fwd(mem[0], g_mem, "rms_mem", dep=token)
    landed, _ = split_wait(_gather_plan, handle, [*rest16, h, mem_n], "gather_w_in_wait")
    (w_in_full,), cw_all = gather_forward(landed, cw_block, "gather_w_in_forward")
    conv_full = jnp.concatenate([cw_all[p, :3] for p in range(N_CHIP)], axis=1)
    rest_handle, rest_token = split_start(_gather_plan, 3 * len(REST), rest16, [], [w_in_full], "gather_rest_start")

    def layout(k, a):
        return a if k in COL_SHARDED else a.reshape(-1, a.shape[-1])

    def rest_weights(after):
        landed, _ = split_wait(_gather_plan, rest_handle, [after], "gather_rest_wait")
        first = gather_forward(landed[:-2], cw_block, "gather_rest_forward")[0]
        mlp_handle, token = split_start(_forward_plan, 6, landed[-2:], [], [first[0]], "forward_mlp_start")

        def mlp_weights(after):
            both, _ = split_wait(_forward_plan, mlp_handle, [after], "forward_mlp_wait")
            return [layout(k, a) for k, a in zip(REST[-2:], both)]

        return [layout(k, a) for k, a in zip(REST[:-2], first)] + [token, mlp_weights]

    def blocks(k, a):
        return a if k in COL_SHARDED else a.reshape(N_CHIP, -1, a.shape[-1])

    early = {}

    def early_grads(g):
        early["g4"] = [blocks(k, g[k]) for k in REST]
        early["swap"], token = split_start(_exchange_plan, len(REST), early["g4"], _exchange_lands(early["g4"]),
                                           [g["g_mem"]], "exchange_rest_start")
        return token

    def mid_grads(after):
        g4, got = split_wait(_exchange_plan, early["swap"], after, "exchange_rest_wait")
        p16 = [pair_sum(a, b, place, "pair_sum_" + k) for k, a, b in zip(REST, g4, got)]
        early["fly"], token = split_start(_scatter_plan, 3 * len(REST), p16, _scatter_lands(p16), [],
                                          "scatter_rest_start")
        return token

    late = {}

    def late_grads(gw):
        late["swap"], token = split_start(_exchange_plan, 1, [gw], _exchange_lands([gw]), [], "exchange_w_in_start")
        return token

    def last_grads(after):
        (gw,), (got,) = split_wait(_exchange_plan, late["swap"], [after], "exchange_w_in_wait")
        p16 = [pair_sum(gw, got, place, "pair_sum_w_in")]
        late["fly"], token = split_start(_scatter_plan, 3, p16, _scatter_lands(p16), [], "scatter_w_in_start")
        return token

    loss_cols, grad_x, g = local_step(
        x[0], mem[0], loss_target[0], g_mix, g_mem, q_norm_g, k_norm_g, g_mlp, conv_full, h, mem_n,
        w_in_full, rest_token, rest_weights, early_grads, mid_grads, late_grads, last_grads)
    token = g["g_mix"]

    p16_rest, got_rest = split_wait(_scatter_plan, early["fly"], [token], "scatter_rest_wait")
    gsum, delta, new_m, new_v = {}, {}, {}, {}
    halves = [chip_sum(p, b, place, "chip_sum_" + k, dep=token) for k, p, b in zip(REST, p16_rest, got_rest)]
    for k, a in zip(REST, join_halves(halves, "join_halves_rest")):
        gsum[k] = a[None]
        delta[k], new_m[k], new_v[k] = adamw(w[k], gsum[k], m[k], v[k], "adamw_" + k)
    p16_in, got_in = split_wait(_scatter_plan, late["fly"], [new_v[k] for k in REST], "scatter_w_in_wait")

    small = allreduce_small(
        _pack_small(D, g["g_mix"], g["g_mem"], g["g_mlp"], g["q_norm_g"], g["k_norm_g"], g["conv_w"], loss_cols),
        "allreduce_small", dep=got_in[0])
    loss = (0.5 / D) * jnp.sum(small[7])
    gsum.update({"g_mix": small[0:1], "g_mem": small[1:2], "g_mlp": small[2:3],
                 "q_norm_g": small[3:4, :X_HEAD_DIM], "k_norm_g": small[3:4, X_HEAD_DIM:2 * X_HEAD_DIM],
                 "conv_w": lax.dynamic_slice(small[4:7], (0, chip * cs), (3, cs))[None]})
    half_in = chip_sum(p16_in[0], got_in[0], place, "chip_sum_w_in")
    gsum["w_in"] = join_halves([half_in], "join_halves_w_in")[0][None]
    delta["w_in"], new_m["w_in"], new_v["w_in"] = adamw(w["w_in"], gsum["w_in"], m["w_in"], v["w_in"], "adamw_w_in")
    small_names = ("g_mix", "g_mem", "g_mlp", "q_norm_g", "k_norm_g", "conv_w")
    zero_row = jnp.zeros((1, D), F32)
    packed = [_pack_small(D, *[t[k] if k != "conv_w" else t[k][0] for k in small_names], zero_row)
              for t in (w, gsum, m, v)]
    sm = adamw(*packed, "adamw_small")
    for t, block in zip((delta, new_m, new_v), sm):
        t["g_mix"], t["g_mem"], t["g_mlp"] = block[0:1], block[1:2], block[2:3]
        t["q_norm_g"], t["k_norm_g"] = block[3:4, :X_HEAD_DIM], block[3:4, X_HEAD_DIM:2 * X_HEAD_DIM]
        t["conv_w"] = block[4:7, :cs][None]

    return (loss, grad_x[None], *[gsum[k] for k in WEIGHTS], *[delta[k] for k in WEIGHTS],
            *[new_m[k] for k in WEIGHTS], *[new_v[k] for k in WEIGHTS])
```

```python
import jax
import jax.numpy as jnp
from jax import lax
from jax.experimental import pallas as pl
from jax.experimental.pallas import tpu as pltpu

F32 = jnp.float32
BF16 = jnp.bfloat16
EPS = 1e-6
N_CHIP = 4
SB_HEAD_DIM = 64
X_HEAD_DIM = 256
LANES = 128
VMEM_LIMIT = 56 * 1024 * 1024
ADAM_LR, ADAM_B1, ADAM_B2, ADAM_EPS, ADAM_WD, ADAM_STEP = 0.001, 0.9, 0.999, 1e-8, 0.01, 10
MESH = pl.DeviceIdType.MESH


def _params(*sem):
    return pltpu.CompilerParams(dimension_semantics=sem, vmem_limit_bytes=VMEM_LIMIT)


def _tile(n, pref):
    if n <= pref:
        return n
    t = 1 << (pref.bit_length() - 1)
    while n % t:
        t //= 2
    return t


NN = (((1,), (0,)), ((), ()))
NT = (((1,), (1,)), ((), ()))
TN = (((0,), (0,)), ((), ()))


def _dot(a, b, dims):
    return lax.dot_general(a.astype(BF16), b.astype(BF16), dims, preferred_element_type=F32)


def mm_nn_shard(a, g, name, relu2=False, dep=None, out_dtype=F32):
    M, K = a.shape
    _, _, Ns = g.shape
    tm, tn = _tile(M, 2048), _tile(Ns, 512)
    nb = Ns // tn

    def body(a_ref, b_ref, *o_refs):
        o_refs = o_refs[len(deps):]
        acc = _dot(a_ref[...], b_ref[...], NN)
        if relu2:
            acc = jnp.maximum(acc, 0.0)
            o_refs[1][...] = (acc * acc).astype(BF16)
        o_refs[0][...] = acc.astype(o_refs[0].dtype)

    o_spec = pl.BlockSpec((tm, tn), lambda i, j: (i, j))
    shapes = [jax.ShapeDtypeStruct((M, N_CHIP * Ns), BF16 if relu2 else out_dtype)]
    specs = [o_spec]
    if relu2:
        shapes.append(jax.ShapeDtypeStruct((M, N_CHIP * Ns), BF16))
        specs.append(o_spec)
    deps = [] if dep is None else [dep]
    out = pl.pallas_call(
        body, grid=(M // tm, N_CHIP * nb), name=name,
        in_specs=[pl.BlockSpec((tm, K), lambda i, j: (i, 0)),
                  pl.BlockSpec((None, K, tn), lambda i, j: (j // nb, 0, j % nb))] + [ANY] * len(deps),
        out_specs=specs, out_shape=shapes, compiler_params=_params("parallel", "parallel"),
    )(a, g, *deps)
    return out if relu2 else out[0]


def mm_nn_loss(a, w, res, target, name):
    M, K = a.shape
    N = w.shape[1]
    tm, tn = _tile(M, 1024), _tile(N, 512)

    def body(a_ref, b_ref, r_ref, t_ref, dy_ref, dy16_ref, l_ref):
        e = (r_ref[...] + _dot(a_ref[...], b_ref[...], NN)) - t_ref[...]
        dy = e * (1.0 / N)
        dy_ref[...] = dy
        dy16_ref[...] = dy.astype(BF16)
        l_ref[...] = jnp.sum(e * e, axis=0, keepdims=True)

    o_spec = pl.BlockSpec((tm, tn), lambda i, j: (i, j))
    return pl.pallas_call(
        body, grid=(M // tm, N // tn), name=name,
        in_specs=[pl.BlockSpec((tm, K), lambda i, j: (i, 0)), pl.BlockSpec((K, tn), lambda i, j: (0, j)),
                  o_spec, o_spec],
        out_specs=[o_spec, o_spec, pl.BlockSpec((None, 1, tn), lambda i, j: (i, 0, j))],
        out_shape=[jax.ShapeDtypeStruct((M, N), F32), jax.ShapeDtypeStruct((M, N), BF16),
                   jax.ShapeDtypeStruct((M // tm, 1, N), F32)],
        compiler_params=_params("parallel", "parallel"),
    )(a, w, res, target)


def mm_nt(a, w, name, up=None, out_dtype=F32, dep=None):
    M, N = a.shape
    R = w.shape[0]
    tm, tr = _tile(M, 2048), _tile(R, 512)

    def body(a_ref, b_ref, *refs):
        acc = _dot(a_ref[...], b_ref[...], NT)
        if up is not None:
            acc = acc * (2.0 * jnp.maximum(refs[0][...].astype(F32), 0.0))
        refs[-1][...] = acc.astype(out_dtype)

    o_spec = pl.BlockSpec((tm, tr), lambda i, j: (i, j))
    ins = [a, w] + ([up] if up is not None else []) + ([dep] if dep is not None else [])
    return pl.pallas_call(
        body, grid=(M // tm, R // tr), name=name,
        in_specs=[pl.BlockSpec((tm, N), lambda i, j: (i, 0)), pl.BlockSpec((tr, N), lambda i, j: (j, 0))]
        + ([o_spec] if up is not None else []) + ([ANY] if dep is not None else []),
        out_specs=o_spec, out_shape=jax.ShapeDtypeStruct((M, R), out_dtype),
        compiler_params=_params("parallel", "parallel"),
    )(*ins)


def mm_nt_shard(a, g, name, out_dtype=F32, dep=None):
    deps = [] if dep is None else [dep]
    M = a.shape[0]
    _, R, Ns = g.shape
    tm, tr, tk = _tile(M, 1024), _tile(R, 1024), _tile(Ns, 2560)
    nb = Ns // tk
    nk = N_CHIP * nb

    def body(a_ref, b_ref, *refs):
        o_ref, acc_ref = refs[len(deps):]
        k = pl.program_id(2)

        @pl.when(k == 0)
        def _():
            acc_ref[...] = jnp.zeros_like(acc_ref)

        acc_ref[...] += _dot(a_ref[...], b_ref[...], NT)

        @pl.when(k == nk - 1)
        def _():
            o_ref[...] = acc_ref[...].astype(out_dtype)

    return pl.pallas_call(
        body, grid=(M // tm, R // tr, nk), name=name,
        in_specs=[pl.BlockSpec((tm, tk), lambda i, j, k: (i, k)),
                  pl.BlockSpec((None, tr, tk), lambda i, j, k: (k // nb, j, k % nb))] + [ANY] * len(deps),
        out_specs=pl.BlockSpec((tm, tr), lambda i, j, k: (i, j)),
        out_shape=jax.ShapeDtypeStruct((M, R), out_dtype),
        scratch_shapes=[pltpu.VMEM((tm, tr), F32)],
        compiler_params=_params("parallel", "parallel", "arbitrary"),
    )(a, g, *deps)


def mm_tn(a, b, name, shard_out=False):
    S, M = a.shape
    N = b.shape[1]
    Ns = N // N_CHIP if shard_out else N
    tm, tn = _tile(M, 1024), _tile(Ns, 512)
    nb = Ns // tn

    def body(a_ref, b_ref, o_ref):
        o_ref[...] = _dot(a_ref[...], b_ref[...], TN)

    if shard_out:
        o_spec = pl.BlockSpec((None, tm, tn), lambda i, j: (j // nb, i, j % nb))
        o_shape = jax.ShapeDtypeStruct((N_CHIP, M, Ns), F32)
    else:
        o_spec = pl.BlockSpec((tm, tn), lambda i, j: (i, j))
        o_shape = jax.ShapeDtypeStruct((M, N), F32)
    return pl.pallas_call(
        body, grid=(M // tm, N // tn), name=name,
        in_specs=[pl.BlockSpec((S, tm), lambda i, j: (0, i)), pl.BlockSpec((S, tn), lambda i, j: (0, j))],
        out_specs=o_spec, out_shape=o_shape, compiler_params=_params("parallel", "parallel"),
    )(a, b)


def rms_fwd(x, g, name, dep=None):
    S, D = x.shape
    tm = _tile(S, 512)
    deps = [] if dep is None else [dep]

    def body(x_ref, g_ref, *refs):
        xv = x_ref[...]
        r = lax.rsqrt(jnp.mean(xv * xv, axis=-1, keepdims=True) + EPS)
        refs[-1][...] = ((xv * r) * g_ref[...]).astype(BF16)

    return pl.pallas_call(
        body, grid=(S // tm,), name=name,
        in_specs=[pl.BlockSpec((tm, D), lambda i: (i, 0)), pl.BlockSpec((1, D), lambda i: (0, 0))]
        + [ANY] * len(deps),
        out_specs=pl.BlockSpec((tm, D), lambda i: (i, 0)),
        out_shape=jax.ShapeDtypeStruct((S, D), BF16), compiler_params=_params("parallel"),
    )(x, g, *deps)


def rms_bwd(x, g, dh, name, dres=None, want_dx=True, want16=False, dep=None):
    S, D = x.shape
    tm = _tile(S, 512)

    def body(x_ref, g_ref, dh_ref, *refs):
        i = pl.program_id(0)
        xv = x_ref[...]
        r = lax.rsqrt(jnp.mean(xv * xv, axis=-1, keepdims=True) + EPS)
        xn = xv * r
        dhv = dh_ref[...].astype(F32)
        gg_ref = refs[-1]

        @pl.when(i == 0)
        def _():
            gg_ref[...] = jnp.zeros_like(gg_ref)

        gg_ref[...] += jnp.sum(dhv * xn, axis=0, keepdims=True)
        if want_dx:
            dxn = dhv * g_ref[...]
            dx = r * (dxn - xn * jnp.mean(dxn * xn, axis=-1, keepdims=True))
            if dres is not None:
                dx = refs[0][...] + dx
            refs[-2][...] = dx.astype(refs[-2].dtype)
            if want16:
                refs[-3][...] = dx

    row = pl.BlockSpec((tm, D), lambda i: (i, 0))
    vec = pl.BlockSpec((1, D), lambda i: (0, 0))
    ins, in_specs = [x, g, dh], [row, vec, row]
    if dres is not None:
        ins.append(dres)
        in_specs.append(row)
    if dep is not None:
        ins.append(dep)
        in_specs.append(ANY)
    shapes, specs = [jax.ShapeDtypeStruct((1, D), F32)], [vec]
    if want_dx:
        if want16:
            shapes.insert(0, jax.ShapeDtypeStruct((S, D), BF16))
            specs.insert(0, row)
        shapes.insert(0, jax.ShapeDtypeStruct((S, D), F32))
        specs.insert(0, row)
    out = pl.pallas_call(body, grid=(S // tm,), name=name, in_specs=in_specs, out_specs=specs,
                         out_shape=shapes, compiler_params=_params("arbitrary"))(*ins)
    return out if want_dx else out[0]


def _shift_down(u, k, row):
    return jnp.where(row >= k, pltpu.roll(u, k, axis=0), 0.0)


def _shift_up(u, k, row):
    S = u.shape[0]
    return jnp.where(row < S - k, pltpu.roll(u, S - k, axis=0), 0.0)


def _conv_specs(S, D, tc):
    nb = D // tc
    col = lambda o: pl.BlockSpec((S, tc), lambda j, o=o: (0, o * nb + j))
    return col, pl.BlockSpec((3, tc), lambda j: (0, j))


def conv_fwd(proj, conv_w, D, name):
    S = proj.shape[0]
    tc = _tile(D, 256)
    col, wspec = _conv_specs(S, D, tc)

    def body(ch_ref, cb_ref, cc_ref, w_ref, a_ref):
        row = lax.broadcasted_iota(jnp.int32, (S, tc), 0)
        u = cc_ref[...].astype(F32) * ch_ref[...].astype(F32)
        w = w_ref[...]
        cv = w[0:1, :] * _shift_down(u, 2, row) + w[1:2, :] * _shift_down(u, 1, row) + w[2:3, :] * u
        a_ref[...] = (cb_ref[...].astype(F32) * cv).astype(BF16)

    return pl.pallas_call(
        body, grid=(D // tc,), name=name, in_specs=[col(0), col(1), col(2), wspec],
        out_specs=pl.BlockSpec((S, tc), lambda j: (0, j)),
        out_shape=jax.ShapeDtypeStruct((S, D), BF16), compiler_params=_params("parallel"),
    )(proj, proj, proj, conv_w)


def conv_bwd(proj, conv_w, da, D, name, dep):
    S = proj.shape[0]
    tc = _tile(D, 256)
    col, wspec = _conv_specs(S, D, tc)
    blk = pl.BlockSpec((S, tc), lambda j: (0, j))

    def body(ch_ref, cb_ref, cc_ref, w_ref, da_ref, dep_ref, dch_ref, dcb_ref, dcc_ref, gw_ref):
        row = lax.broadcasted_iota(jnp.int32, (S, tc), 0)
        ch, cb, cc, dav = [r[...].astype(F32) for r in (ch_ref, cb_ref, cc_ref, da_ref)]
        w = w_ref[...]
        u = cc * ch
        u1, u2 = _shift_down(u, 1, row), _shift_down(u, 2, row)
        cv = w[0:1, :] * u2 + w[1:2, :] * u1 + w[2:3, :] * u
        dcb_ref[...] = (dav * cv).astype(BF16)
        dcv = dav * cb
        gw_ref[0:1, :] = jnp.sum(dcv * u2, axis=0, keepdims=True)
        gw_ref[1:2, :] = jnp.sum(dcv * u1, axis=0, keepdims=True)
        gw_ref[2:3, :] = jnp.sum(dcv * u, axis=0, keepdims=True)
        du = w[2:3, :] * dcv + w[1:2, :] * _shift_up(dcv, 1, row) + w[0:1, :] * _shift_up(dcv, 2, row)
        dcc_ref[...] = (du * ch).astype(BF16)
        dch_ref[...] = (du * cc).astype(BF16)

    act = jax.ShapeDtypeStruct((S, D), BF16)
    return pl.pallas_call(
        body, grid=(D // tc,), name=name, in_specs=[col(0), col(1), col(2), wspec, blk, ANY],
        out_specs=[blk, blk, blk, wspec], out_shape=[act, act, act, jax.ShapeDtypeStruct((3, D), F32)],
        compiler_params=_params("parallel"),
    )(proj, proj, proj, conv_w, da, dep)


SB_BQ_FWD = 512
SB_BQ_BWD = 256
SB_BK = 128
SB_GROUP = 4


def _sb_consts(bq):
    lane = lax.broadcasted_iota(jnp.int32, (bq, LANES), 1)
    r = lax.broadcasted_iota(jnp.int32, (SB_BK, SB_BK), 0)
    c = lax.broadcasted_iota(jnp.int32, (SB_BK, SB_BK), 1)
    tri_rev = jnp.where(r > c, 1.0, 0.0).astype(BF16)
    tri_fwd = jnp.where(r < c, 1.0, 0.0).astype(BF16)
    return lane, tri_rev, tri_fwd


def _cumsum2(v, tri):
    hi = v.astype(BF16)
    lo = (v - hi.astype(F32)).astype(BF16)
    part = (lax.dot_general(hi, tri, NN, preferred_element_type=F32)
            + lax.dot_general(lo, tri, NN, preferred_element_type=F32))
    return part, jnp.sum(v, axis=1, keepdims=True)


def _sb_logits(z, past):
    sp = jnp.log(1.0 + jnp.exp(-jnp.abs(z)))
    l = jnp.minimum(z, 0.0) - sp
    m = l - z
    if past is not None:
        m = jnp.where(past, m, 0.0)
    return l, m


def _stack_heads(v, lane):
    return jnp.concatenate([jnp.where(lane < SB_HEAD_DIM, v, 0.0), jnp.where(lane >= SB_HEAD_DIM, v, 0.0)],
                           axis=0).astype(BF16)


def _unstack_heads(v, lane):
    bq = v.shape[0] // 2
    return jnp.where(lane < SB_HEAD_DIM, v[:bq], v[bq:])


def _sb_positions(i, bq):
    r = lax.broadcasted_iota(jnp.int32, (2 * bq, SB_BK), 0)
    trow = i * bq + jnp.where(r >= bq, r - bq, r)
    return trow, lax.broadcasted_iota(jnp.int32, (2 * bq, SB_BK), 1)


def _sb_specs(S, D, bq):
    npair = D // LANES
    qspec = pl.BlockSpec((bq, LANES), lambda p, i: (i, 3 * npair + p))
    kspec = pl.BlockSpec((S, LANES), lambda p, i: (0, 4 * npair + p))
    vspec = pl.BlockSpec((S, LANES), lambda p, i: (0, 5 * npair + p))
    return npair, qspec, kspec, vspec


def sb_fwd(proj, D, name):
    S = proj.shape[0]
    bq = min(SB_BQ_FWD, S)
    nd = bq // SB_BK
    npair, qspec, kspec, vspec = _sb_specs(S, D, bq)
    scale = SB_HEAD_DIM ** -0.5

    def body(q_ref, k_ref, v_ref, o_ref, kb_ref, vb_ref):
        i = pl.program_id(1)

        @pl.when(i == 0)
        def _():
            kb_ref[...] = k_ref[...].astype(BF16)
            vb_ref[...] = v_ref[...].astype(BF16)

        lane, tri_rev, _ = _sb_consts(bq)
        qs = _stack_heads(q_ref[...].astype(F32) * scale, lane)
        trow, scol = _sb_positions(i, bq)

        def steps(j0, carry, n, masked):
            ks = [pl.multiple_of((j0 - t) * SB_BK, SB_BK) for t in range(n)]
            past = [(k + scol) < trow if masked else None for k in ks]
            zs = [lax.dot_general(qs, kb_ref[pl.ds(k, SB_BK), :], NT, preferred_element_type=F32) for k in ks]
            lm = [_sb_logits(z, p) for z, p in zip(zs, past)]
            cs = [_cumsum2(m, tri_rev) for _, m in lm]
            c, acc = carry
            for t in range(n):
                a = jnp.exp(lm[t][0] + (cs[t][0] + c))
                if masked:
                    a = jnp.where(past[t], a, 0.0)
                acc = acc + lax.dot_general(a.astype(BF16), vb_ref[pl.ds(ks[t], SB_BK), :], NN,
                                            preferred_element_type=F32)
                c = c + cs[t][1]
            return c, acc

        carry = (jnp.zeros((2 * bq, 1), F32), jnp.zeros((2 * bq, LANES), F32))
        carry = steps(i * nd + nd - 1, carry, nd, True)
        older = i * nd
        groups = older // SB_GROUP
        carry = lax.fori_loop(
            0, groups, lambda t, cr: steps(older - 1 - t * SB_GROUP, cr, SB_GROUP, False), carry)
        rest = older - groups * SB_GROUP
        carry = lax.fori_loop(0, rest // nd, lambda t, cr: steps(rest - 1 - t * nd, cr, nd, False), carry)
        o_ref[...] = _unstack_heads(carry[1], lane)

    return pl.pallas_call(
        body, grid=(npair, S // bq), name=name, in_specs=[qspec, kspec, vspec],
        out_specs=pl.BlockSpec((bq, LANES), lambda p, i: (i, p)),
        out_shape=jax.ShapeDtypeStruct((S, D), F32),
        scratch_shapes=[pltpu.VMEM((S, LANES), BF16), pltpu.VMEM((S, LANES), BF16)],
        compiler_params=_params("parallel", "arbitrary"),
    )(proj, proj, proj)


def sb_bwd(proj, do, D, name, dep=None):
    S = proj.shape[0]
    bq = min(SB_BQ_BWD, S)
    nd = bq // SB_BK
    nkb = S // SB_BK
    npair, qspec, kspec, vspec = _sb_specs(S, D, bq)
    scale = SB_HEAD_DIM ** -0.5

    def body(q_ref, k_ref, v_ref, do_ref, *refs):
        dq_ref, dk_ref, dv_ref, kb_ref, vb_ref, dk_acc, dv_acc, g_scr, b_scr, a_scr = refs[len(deps):]
        i = pl.program_id(1)

        @pl.when(i == 0)
        def _():
            kb_ref[...] = k_ref[...].astype(BF16)
            vb_ref[...] = v_ref[...].astype(BF16)
            dk_acc[...] = jnp.zeros_like(dk_acc)
            dv_acc[...] = jnp.zeros_like(dv_acc)

        lane, tri_rev, tri_fwd = _sb_consts(bq)
        qs = _stack_heads(q_ref[...].astype(F32) * scale, lane)
        dos = _stack_heads(do_ref[...].astype(F32), lane)
        qs_t, dos_t = qs.T, dos.T
        trow, scol = _sb_positions(i, bq)

        def sweep1(j0, c, n, masked):
            js = [j0 - t for t in range(n)]
            ks = [pl.multiple_of(j * SB_BK, SB_BK) for j in js]
            past = [(k + scol) < trow if masked else None for k in ks]
            zs = [lax.dot_general(qs, kb_ref[pl.ds(k, SB_BK), :], NT, preferred_element_type=F32) for k in ks]
            das = [lax.dot_general(dos, vb_ref[pl.ds(k, SB_BK), :], NT, preferred_element_type=F32) for k in ks]
            lm = [_sb_logits(z, p) for z, p in zip(zs, past)]
            cs = [_cumsum2(m, tri_rev) for _, m in lm]
            for t in range(n):
                b_scr[js[t]] = jnp.exp(lm[t][0]).astype(BF16)
            for t in range(n):
                a = jnp.exp(lm[t][0] + (cs[t][0] + c))
                if masked:
                    a = jnp.where(past[t], a, 0.0)
                g_scr[js[t]] = (das[t] * a).astype(BF16)
                a_scr[js[t]] = a.astype(BF16)
                c = c + cs[t][1]
            return c

        older = i * nd
        groups = older // SB_GROUP
        rest = older - groups * SB_GROUP
        c = jnp.zeros((2 * bq, 1), F32)
        c = sweep1(i * nd + nd - 1, c, nd, True)
        c = lax.fori_loop(0, groups, lambda t, cr: sweep1(older - 1 - t * SB_GROUP, cr, SB_GROUP, False), c)
        lax.fori_loop(0, rest // nd, lambda t, cr: sweep1(rest - 1 - t * nd, cr, nd, False), c)

        def sweep2(j0, carry, n, masked):
            js = [j0 + t for t in range(n)]
            ks = [pl.multiple_of(j * SB_BK, SB_BK) for j in js]
            g16 = [g_scr[j] for j in js]
            gv = [g.astype(F32) for g in g16]
            gs = [(lax.dot_general(g, tri_fwd, NN, preferred_element_type=F32), jnp.sum(v, axis=1, keepdims=True))
                  for g, v in zip(g16, gv)]
            pc, dq = carry
            dzs = []
            for t in range(n):
                dz = gv[t] - b_scr[js[t]].astype(F32) * (gv[t] + (gs[t][0] + pc))
                if masked:
                    dz = jnp.where((ks[t] + scol) < trow, dz, 0.0)
                dzs.append(dz.astype(BF16))
                pc = pc + gs[t][1]
            for t in range(n):
                dq = dq + lax.dot_general(dzs[t], kb_ref[pl.ds(ks[t], SB_BK), :], NN, preferred_element_type=F32)
                dk_acc[js[t]] += lax.dot_general(qs_t, dzs[t], NN, preferred_element_type=F32)
                dv_acc[js[t]] += lax.dot_general(dos_t, a_scr[js[t]], NN, preferred_element_type=F32)
            return pc, dq

        carry = (jnp.zeros((2 * bq, 1), F32), jnp.zeros((2 * bq, LANES), F32))
        carry = lax.fori_loop(0, groups, lambda t, cr: sweep2(t * SB_GROUP, cr, SB_GROUP, False), carry)
        carry = lax.fori_loop(
            0, rest // nd, lambda t, cr: sweep2(groups * SB_GROUP + t * nd, cr, nd, False), carry)
        carry = sweep2(i * nd, carry, nd, True)
        dq_ref[...] = (_unstack_heads(carry[1], lane) * scale).astype(BF16)

        @pl.when(i == pl.num_programs(1) - 1)
        def _():
            for j in range(nkb):
                dk_ref[j * SB_BK:(j + 1) * SB_BK, :] = dk_acc[j].T.astype(BF16)
                dv_ref[j * SB_BK:(j + 1) * SB_BK, :] = dv_acc[j].T.astype(BF16)

    deps = [] if dep is None else [dep]
    full = pl.BlockSpec((S, LANES), lambda p, i: (0, p))
    blk = pl.BlockSpec((bq, LANES), lambda p, i: (i, p))
    act = jax.ShapeDtypeStruct((S, D), BF16)
    return pl.pallas_call(
        body, grid=(npair, S // bq), name=name, in_specs=[qspec, kspec, vspec, blk] + [ANY] * len(deps),
        out_specs=[blk, full, full], out_shape=[act, act, act],
        scratch_shapes=[pltpu.VMEM((S, LANES), BF16), pltpu.VMEM((S, LANES), BF16),
                        pltpu.VMEM((nkb, LANES, SB_BK), F32), pltpu.VMEM((nkb, LANES, SB_BK), F32),
                        pltpu.VMEM((nkb, 2 * bq, SB_BK), BF16), pltpu.VMEM((nkb, 2 * bq, SB_BK), BF16),
                        pltpu.VMEM((nkb, 2 * bq, SB_BK), BF16)],
        compiler_params=_params("parallel", "arbitrary"),
    )(proj, proj, proj, do, *deps)


def _rms_rows(v):
    r = lax.rsqrt(jnp.mean(v * v, axis=-1, keepdims=True) + EPS)
    return v * r, r


def _xa_specs(S, D, M, tq):
    nh = D // X_HEAD_DIM
    qspec = pl.BlockSpec((tq, X_HEAD_DIM), lambda h, i: (i, 6 * nh + h))
    kspec = pl.BlockSpec((M, X_HEAD_DIM), lambda h, i: (0, h))
    vspec = pl.BlockSpec((M, X_HEAD_DIM), lambda h, i: (0, nh + h))
    gspec = pl.BlockSpec((1, X_HEAD_DIM), lambda h, i: (0, 0))
    return nh, qspec, kspec, vspec, gspec


def xa_fwd(proj, kv, gq, gk, D, name):
    S, M = proj.shape[0], kv.shape[0]
    tq = _tile(S, 2048)
    nh, qspec, kspec, vspec, gspec = _xa_specs(S, D, M, tq)
    scale = X_HEAD_DIM ** -0.5

    def body(q_ref, k_ref, v_ref, gq_ref, gk_ref, o_ref):
        qn = _rms_rows(q_ref[...].astype(F32))[0] * gq_ref[...]
        kn = _rms_rows(k_ref[...])[0] * gk_ref[...]
        s = _dot(qn, kn, NT) * scale
        e = jnp.exp(s - jnp.max(s, axis=-1, keepdims=True))
        p = e / jnp.sum(e, axis=-1, keepdims=True)
        o_ref[...] = _dot(p, v_ref[...], NN)

    return pl.pallas_call(
        body, grid=(nh, S // tq), name=name, in_specs=[qspec, kspec, vspec, gspec, gspec],
        out_specs=pl.BlockSpec((tq, X_HEAD_DIM), lambda h, i: (i, h)),
        out_shape=jax.ShapeDtypeStruct((S, D), F32), compiler_params=_params("parallel", "parallel"),
    )(proj, kv, kv, gq, gk)


def xa_bwd(proj, kv, gq, gk, do, D, name):
    S, M = proj.shape[0], kv.shape[0]
    tq = _tile(S, 2048)
    nh, qspec, kspec, vspec, gspec = _xa_specs(S, D, M, tq)
    scale = X_HEAD_DIM ** -0.5

    def body(q_ref, k_ref, v_ref, gq_ref, gk_ref, do_ref, dq_ref, dk_ref, dv_ref, ggq_ref, ggk_ref,
             dkn_acc, dv_acc):
        h, i = pl.program_id(0), pl.program_id(1)

        @pl.when((h == 0) & (i == 0))
        def _():
            ggq_ref[...] = jnp.zeros_like(ggq_ref)
            ggk_ref[...] = jnp.zeros_like(ggk_ref)

        @pl.when(i == 0)
        def _():
            dkn_acc[...] = jnp.zeros_like(dkn_acc)
            dv_acc[...] = jnp.zeros_like(dv_acc)

        gq, gk = gq_ref[...], gk_ref[...]
        qhat, rq = _rms_rows(q_ref[...].astype(F32))
        khat, rk = _rms_rows(k_ref[...])
        qn, kn = qhat * gq, khat * gk
        s = _dot(qn, kn, NT) * scale
        e = jnp.exp(s - jnp.max(s, axis=-1, keepdims=True))
        p = e / jnp.sum(e, axis=-1, keepdims=True)
        dov = do_ref[...]
        dv_acc[...] += _dot(p, dov, TN)
        dp = _dot(dov, v_ref[...], NT)
        ds = (p * (dp - jnp.sum(dp * p, axis=-1, keepdims=True))) * scale
        dqn = _dot(ds, kn, NN)
        dkn_acc[...] += _dot(ds, qn, TN)
        ggq_ref[...] += jnp.sum(dqn * qhat, axis=0, keepdims=True)
        dqh = dqn * gq
        dq_ref[...] = (rq * (dqh - qhat * jnp.mean(dqh * qhat, axis=-1, keepdims=True))).astype(BF16)

        @pl.when(i == pl.num_programs(1) - 1)
        def _():
            dkn = dkn_acc[...]
            ggk_ref[...] += jnp.sum(dkn * khat, axis=0, keepdims=True)
            dkh = dkn * gk
            dk_ref[...] = (rk * (dkh - khat * jnp.mean(dkh * khat, axis=-1, keepdims=True))).astype(BF16)
            dv_ref[...] = dv_acc[...].astype(BF16)

    blk = pl.BlockSpec((tq, X_HEAD_DIM), lambda h, i: (i, h))
    kv_shape = jax.ShapeDtypeStruct((M, 2 * D), BF16)
    gshape = jax.ShapeDtypeStruct((1, X_HEAD_DIM), F32)
    dq, dk, dv, ggq, ggk = pl.pallas_call(
        body, grid=(nh, S // tq), name=name, in_specs=[qspec, kspec, vspec, gspec, gspec, blk],
        out_specs=[blk, kspec, vspec, gspec, gspec],
        out_shape=[jax.ShapeDtypeStruct((S, D), BF16), kv_shape, kv_shape, gshape, gshape],
        scratch_shapes=[pltpu.VMEM((M, X_HEAD_DIM), F32), pltpu.VMEM((M, X_HEAD_DIM), F32)],
        compiler_params=_params("arbitrary", "arbitrary"),
    )(proj, kv, kv, gq, gk, do)
    d_kv = jnp.concatenate([dk[:, :D], dv[:, D:]], axis=1)
    return dq, d_kv, ggq, ggk


def _gate_specs(S, D, tm):
    row = pl.BlockSpec((tm, D), lambda i: (i, 0))
    gate = lambda b: pl.BlockSpec((tm, D), lambda i, b=b: (i, 7 + b))
    return row, gate


def merge_fwd(proj, branches, w_branches, w_out, x, g_next, D, name):
    S = proj.shape[0]
    tm = _tile(S, 256)
    row, gate = _gate_specs(S, D, tm)
    mat = pl.BlockSpec((D, D), lambda i: (0, 0))

    def body(g0, g1, g2, b0, b1, b2, w0, w1, w2, w_ref, x_ref, gn_ref, y0, y1, y2, m_ref, o_ref, h_ref):
        acc = None
        for g_ref, b_ref, wb_ref, y_ref in ((g0, b0, w0, y0), (g1, b1, w1, y1), (g2, b2, w2, y2)):
            y = _dot(b_ref[...], wb_ref[...], NN).astype(BF16)
            y_ref[...] = y
            term = jax.nn.sigmoid(g_ref[...].astype(F32)) * y.astype(F32)
            acc = term if acc is None else acc + term
        merged = acc.astype(BF16)
        m_ref[...] = merged
        x1 = x_ref[...] + _dot(merged, w_ref[...], NN)
        o_ref[...] = x1
        h_ref[...] = (_rms_rows(x1)[0] * gn_ref[...]).astype(BF16)

    act = jax.ShapeDtypeStruct((S, D), BF16)
    out = pl.pallas_call(
        body, grid=(S // tm,), name=name,
        in_specs=[gate(0), gate(1), gate(2), row, row, row, mat, mat, mat, mat, row,
                  pl.BlockSpec((1, D), lambda i: (0, 0))],
        out_specs=[row] * 6, out_shape=[act, act, act, act, jax.ShapeDtypeStruct((S, D), F32), act],
        compiler_params=_params("parallel"),
    )(proj, proj, proj, *branches, *w_branches, w_out, x, g_next)
    return out[:3], out[3], out[4], out[5]


def merge_bwd(proj, ys, dx1, w_out, w_branches, D, name, dep):
    S = proj.shape[0]
    tm = _tile(S, 256)
    row, gate = _gate_specs(S, D, tm)
    mat = pl.BlockSpec((D, D), lambda i: (0, 0))

    def body(g0, g1, g2, y0, y1, y2, dx_ref, w_ref, w0, w1, w2, dep_ref, d0, d1, d2, p0, p1, p2, dg_ref):
        dmv = _dot(dx_ref[...], w_ref[...], NT)
        for b, (g_ref, y_ref, wb_ref, d_ref, p_ref) in enumerate(
                ((g0, y0, w0, d0, p0), (g1, y1, w1, d1, p1), (g2, y2, w2, d2, p2))):
            s = jax.nn.sigmoid(g_ref[...].astype(F32))
            dy = (dmv * s).astype(BF16)
            d_ref[...] = dy
            p_ref[...] = _dot(dy, wb_ref[...], NT).astype(BF16)
            dg_ref[:, b * D:(b + 1) * D] = ((dmv * y_ref[...].astype(F32)) * (s * (1.0 - s))).astype(BF16)

    act = jax.ShapeDtypeStruct((S, D), BF16)
    out = pl.pallas_call(
        body, grid=(S // tm,), name=name,
        in_specs=[gate(0), gate(1), gate(2), row, row, row, row, mat, mat, mat, mat, ANY],
        out_specs=[row] * 6 + [pl.BlockSpec((tm, 3 * D), lambda i: (i, 0))],
        out_shape=[act] * 6 + [jax.ShapeDtypeStruct((S, 3 * D), BF16)], compiler_params=_params("parallel"),
    )(proj, proj, proj, *ys, dx1, w_out, *w_branches, dep)
    return out[:3], out[3:6], out[6]


def _rows2d(a):
    return a.reshape(-1, a.shape[-1])


def _ew_call(fn, ins, out_dtypes, name):
    R, C = ins[0].shape
    tr = _tile(R, max(8, (1 << 19) // C))
    spec = pl.BlockSpec((tr, C), lambda i: (i, 0))

    def body(*refs):
        outs = fn(*[r[...] for r in refs[:len(ins)]])
        for o_ref, o in zip(refs[len(ins):], outs):
            o_ref[...] = o.astype(o_ref.dtype)

    return pl.pallas_call(
        body, grid=(R // tr,), name=name, in_specs=[spec] * len(ins), out_specs=[spec] * len(out_dtypes),
        out_shape=[jax.ShapeDtypeStruct((R, C), d) for d in out_dtypes], compiler_params=_params("parallel"),
    )(*ins)


def adamw(w, g, m, v, name):
    def fn(w, g, m, v):
        m = ADAM_B1 * m + (1.0 - ADAM_B1) * g
        v = ADAM_B2 * v + (1.0 - ADAM_B2) * (g * g)
        m_hat = m / (1.0 - ADAM_B1 ** ADAM_STEP)
        v_hat = v / (1.0 - ADAM_B2 ** ADAM_STEP)
        return -ADAM_LR * (m_hat / (jnp.sqrt(v_hat) + ADAM_EPS) + ADAM_WD * w), m, v

    shp = w.shape
    outs = _ew_call(fn, [_rows2d(a) for a in (w, g, m, v)], [F32, F32, F32], name)
    return [o.reshape(shp) for o in outs]


def _placed_call(fn, place, grid, ins, in_specs, out_shape, out_specs, name, dep=None):
    n = len(ins)
    deps = [] if dep is None else [dep]

    def body(place_ref, *refs):
        outs = fn(*[r[...] for r in refs[:n]])
        for o_ref, o in zip(refs[n + len(deps):], outs):
            o_ref[...] = o.astype(o_ref.dtype)

    return pl.pallas_call(
        body, name=name, out_shape=out_shape,
        grid_spec=pltpu.PrefetchScalarGridSpec(
            num_scalar_prefetch=1, grid=grid, in_specs=list(in_specs) + [ANY] * len(deps), out_specs=out_specs),
        compiler_params=_params(*["parallel"] * len(grid)),
    )(place, *ins, *deps)


def _row_tile(R, C):
    return _tile(R, max(16, (1 << 19) // C))


def cast_into_full(w, place, name, dep=None):
    R, C = w.shape
    tr = _row_tile(R, C)
    return _placed_call(
        lambda a: (a,), place, (R // tr,), [w], [pl.BlockSpec((tr, C), lambda i, p: (i, 0))],
        [jax.ShapeDtypeStruct((N_CHIP, R, C), BF16)], [pl.BlockSpec((None, tr, C), lambda i, p: (p[0], i, 0))],
        name, dep=dep)[0]


def pair_sum(g4, got, place, name):
    _, hr, C = got.shape
    tr = _row_tile(hr, C)
    nb = hr // tr
    blk = pl.BlockSpec((None, tr, C), lambda s, i, p: (s, i, 0))
    return _placed_call(
        lambda a, b: (a + b,), place, (N_CHIP, nb), [g4, got],
        [pl.BlockSpec((None, tr, C), lambda s, i, p: (s, p[1] * nb + i, 0)), blk],
        [jax.ShapeDtypeStruct(got.shape, BF16)], [blk], name)[0]


def chip_sum(p32, got, place, name, dep=None):
    _, H, C = p32.shape
    tr = _row_tile(H, C)
    nb = H // tr
    peer = lambda j: pl.BlockSpec((None, tr, C), lambda i, p, j=j: (j, i, 0))
    return _placed_call(
        lambda a, b, c, d: (((a.astype(F32) + b.astype(F32)) + c.astype(F32)) + d.astype(F32),), place, (nb,),
        [p32, got, got, got], [pl.BlockSpec((None, tr, C), lambda i, p: (p[0], i, 0)), peer(0), peer(1), peer(2)],
        [jax.ShapeDtypeStruct((2 * H, C), F32)], [pl.BlockSpec((tr, C), lambda i, p: (p[1] * nb + i, 0))],
        name, dep=dep)[0]


ANY = pl.BlockSpec(memory_space=pl.ANY)
CHIP_FLIPS = ((1, 0), (0, 1), (1, 1))


def _place():
    return lax.axis_index("x"), lax.axis_index("y"), lax.axis_index("c")


def _flip(v, f):
    return 1 - v if f else v


def join_halves(fulls, name):
    n = len(fulls)

    def body(*refs):
        outs = refs[n:2 * n]
        send_sem, recv_sem = refs[2 * n:]
        x, y, c = _place()
        copies = []
        for a in range(n):
            hr = outs[a].shape[0] // 2
            half = outs[a].at[pl.ds(c * hr, hr), :]
            cp = pltpu.make_async_remote_copy(
                src_ref=half, dst_ref=half, send_sem=send_sem.at[a], recv_sem=recv_sem.at[a],
                device_id=(x, y, 1 - c), device_id_type=MESH)
            cp.start()
            copies.append(cp)
        for a, cp in enumerate(copies):
            hr = outs[a].shape[0] // 2
            theirs = outs[a].at[pl.ds((1 - c) * hr, hr), :]
            cp.wait_send()
            pltpu.make_async_remote_copy(
                src_ref=theirs, dst_ref=theirs, send_sem=send_sem.at[a], recv_sem=recv_sem.at[a],
                device_id=(x, y, 1 - c), device_id_type=MESH).wait_recv()

    dma = pltpu.SemaphoreType.DMA
    return pl.pallas_call(
        body, name=name, in_specs=[ANY] * n, out_specs=[ANY] * n,
        out_shape=[jax.ShapeDtypeStruct(f.shape, F32) for f in fulls],
        input_output_aliases={a: a for a in range(n)},
        scratch_shapes=[dma((n,)), dma((n,))],
    )(*fulls)


HBM = pl.BlockSpec(memory_space=pltpu.HBM)
SEM = pl.BlockSpec(memory_space=pltpu.SEMAPHORE)
EFFECT = pltpu.SideEffectType.DATAFLOW_SIDE_EFFECTING


def _in_hbm(a):
    return pltpu.with_memory_space_constraint(a, pltpu.HBM)


def _gather_half(ref, chip_idx, core):
    hr = ref.shape[1] // 2
    return ref.at[chip_idx, pl.ds(core * hr, hr), :]


def gather_forward(fulls, small, name):
    n = len(fulls)

    def body(*refs):
        small_in = refs[n]
        outs, small_out = refs[n + 1:2 * n + 1], refs[2 * n + 1]
        send_sem, recv_sem, sm_send, sm_recv, loc_sem = refs[2 * n + 2:]
        x, y, c = _place()
        mine = 2 * x + y
        chips = [(_flip(x, fx), _flip(y, fy)) for fx, fy in CHIP_FLIPS]
        local = pltpu.make_async_copy(small_in, small_out.at[mine], loc_sem)
        local.start()
        copies = []
        for j, (px, py) in enumerate(chips):
            cp = pltpu.make_async_remote_copy(
                src_ref=small_in, dst_ref=small_out.at[mine], send_sem=sm_send.at[j], recv_sem=sm_recv.at[j],
                device_id=(px, py, c), device_id_type=MESH)
            cp.start()
            copies.append(cp)
        for a in range(n):
            for j, (px, py) in enumerate(chips):
                src = _gather_half(outs[a], 2 * px + py, c)
                cp = pltpu.make_async_remote_copy(
                    src_ref=src, dst_ref=src, send_sem=send_sem.at[3 * a + j], recv_sem=recv_sem.at[3 * a + j],
                    device_id=(x, y, 1 - c), device_id_type=MESH)
                cp.start()
                copies.append(cp)
        for a in range(n):
            for j, (px, py) in enumerate(chips):
                dst = _gather_half(outs[a], 2 * px + py, 1 - c)
                pltpu.make_async_remote_copy(
                    src_ref=dst, dst_ref=dst, send_sem=send_sem.at[3 * a + j], recv_sem=recv_sem.at[3 * a + j],
                    device_id=(x, y, 1 - c), device_id_type=MESH).wait_recv()
        for j, (px, py) in enumerate(chips):
            dst = small_out.at[2 * px + py]
            pltpu.make_async_remote_copy(
                src_ref=dst, dst_ref=dst, send_sem=sm_send.at[j], recv_sem=sm_recv.at[j],
                device_id=(px, py, c), device_id_type=MESH).wait_recv()
        for cp in copies:
            cp.wait_send()
        local.wait()

    dma = pltpu.SemaphoreType.DMA
    out = pl.pallas_call(
        body, name=name, in_specs=[ANY] * (n + 1), out_specs=[ANY] * (n + 1),
        out_shape=[jax.ShapeDtypeStruct(f.shape, f.dtype) for f in fulls]
        + [jax.ShapeDtypeStruct((N_CHIP,) + small.shape, small.dtype)],
        input_output_aliases={a: a for a in range(n)},
        scratch_shapes=[dma((3 * n,)), dma((3 * n,)), dma((3,)), dma((3,)), dma],
    )(*fulls, small)
    return out[:n], out[n]


def _gather_plan(fulls, lands):
    x, y, c = _place()
    mine = 2 * x + y
    return [(_gather_half(f, mine, c), _gather_half(f, mine, c), (_flip(x, fx), _flip(y, fy), c))
            for f in fulls for fx, fy in CHIP_FLIPS]


def _scatter_plan(parts, lands):
    x, y, c = _place()
    plan = []
    for p, l in zip(parts, lands):
        for j, (fx, fy) in enumerate(CHIP_FLIPS):
            px, py = _flip(x, fx), _flip(y, fy)
            plan.append((p.at[2 * px + py], l.at[j], (px, py, c)))
    return plan


def _scatter_lands(parts):
    return [(3,) + p.shape[1:] for p in parts]


def _exchange_plan(grads, lands):
    x, y, c = _place()
    plan = []
    for g, l in zip(grads, lands):
        hr = g.shape[1] // 2
        plan.append((g.at[:, pl.ds((1 - c) * hr, hr), :], l, (x, y, 1 - c)))
    return plan


def _exchange_lands(grads):
    return [(N_CHIP, g.shape[1] // 2, g.shape[2]) for g in grads]


def _forward_plan(fulls, lands):
    x, y, c = _place()
    return [(_gather_half(f, 2 * _flip(x, fx) + _flip(y, fy), c), _gather_half(f, 2 * _flip(x, fx) + _flip(y, fy), c),
             (x, y, 1 - c)) for f in fulls for fx, fy in CHIP_FLIPS]


def split_start(plan, copies, srcs, land_shapes, deps, name):
    n, m = len(srcs), len(land_shapes)
    lands = [lax.empty(s, srcs[0].dtype) for s in land_shapes]

    def body(*refs):
        k0 = n + m + len(deps)
        send_sem, recv_sem = refs[k0], refs[k0 + 1]
        thru, token = refs[k0 + 2:k0 + 2 + n + m], refs[k0 + 2 + n + m]
        for k, (src, dst, dev) in enumerate(plan(thru[:n], thru[n:])):
            pltpu.make_async_remote_copy(src_ref=src, dst_ref=dst, send_sem=send_sem.at[k], recv_sem=recv_sem.at[k],
                                         device_id=dev, device_id_type=MESH).start()
        token[...] = jnp.zeros_like(token)

    dma = pltpu.SemaphoreType.DMA
    arrays = list(srcs) + lands
    out = pl.pallas_call(
        body, name=name,
        out_shape=(dma((copies,)), dma((copies,)), *[pltpu.HBM(a.shape, a.dtype) for a in arrays],
                   jax.ShapeDtypeStruct((8, LANES), F32)),
        in_specs=[HBM] * (n + m) + [ANY] * len(deps),
        out_specs=(SEM, SEM, *[HBM] * (n + m), pl.BlockSpec(memory_space=pltpu.VMEM)),
        input_output_aliases={a: 2 + a for a in range(n + m)},
        compiler_params=pltpu.CompilerParams(has_side_effects=EFFECT),
    )(*[_in_hbm(a) for a in arrays], *deps)
    return (out[0], out[1], list(out[2:2 + n]), list(out[2 + n:2 + n + m])), out[2 + n + m]


def split_wait(plan, handle, after, name):
    send_sem, recv_sem, srcs, lands = handle
    n, m = len(srcs), len(lands)

    def body(*refs):
        send_sem, recv_sem = refs[n + m], refs[n + m + 1]
        thru = refs[n + m + 2 + len(after):]
        for k, (src, dst, dev) in enumerate(plan(thru[:n], thru[n:])):
            cp = pltpu.make_async_remote_copy(src_ref=src, dst_ref=dst, send_sem=send_sem.at[k],
                                              recv_sem=recv_sem.at[k], device_id=dev, device_id_type=MESH)
            cp.wait_send()
            cp.wait_recv()

    arrays = list(srcs) + list(lands)
    out = pl.pallas_call(
        body, name=name, out_shape=tuple(pltpu.HBM(a.shape, a.dtype) for a in arrays),
        in_specs=[HBM] * (n + m) + [SEM, SEM] + [ANY] * len(after), out_specs=tuple([HBM] * (n + m)),
        input_output_aliases={a: a for a in range(n + m)},
        compiler_params=pltpu.CompilerParams(has_side_effects=EFFECT),
    )(*arrays, send_sem, recv_sem, *after)
    return list(out[:n]), list(out[n:])


def allreduce_small(block, name, dep):
    R, C = block.shape

    def body(in_ref, dep_ref, out_ref, slots, send_sem, recv_sem):
        x, y, c = _place()
        me = 4 * x + 2 * y + c
        slots[me] = in_ref[...]
        copies = []
        for r in range(1, 8):
            fx, fy, fc = (r >> 2) & 1, (r >> 1) & 1, r & 1
            cp = pltpu.make_async_remote_copy(
                src_ref=in_ref, dst_ref=slots.at[me], send_sem=send_sem.at[r - 1], recv_sem=recv_sem.at[r - 1],
                device_id=(_flip(x, fx), _flip(y, fy), _flip(c, fc)), device_id_type=MESH)
            cp.start()
            copies.append(cp)
        for cp in copies:
            cp.wait()
        acc = slots[0]
        for d in range(1, 8):
            acc = acc + slots[d]
        out_ref[...] = acc

    vm = pl.BlockSpec(memory_space=pltpu.VMEM)
    dma = pltpu.SemaphoreType.DMA
    return pl.pallas_call(
        body, name=name, in_specs=[vm, ANY], out_specs=vm, out_shape=jax.ShapeDtypeStruct((R, C), F32),
        scratch_shapes=[pltpu.VMEM((8, R, C), F32), dma((7,)), dma((7,))],
    )(block, dep)


def local_step(x, mem, target, g_mix, g_mem, q_norm_g, k_norm_g, g_mlp, conv_w, h, mem_n, w_in, w_in_dep,
               rest_weights, first_grads, early_grads, mid_grads, late_grads, last_grads):
    S, D = x.shape
    proj = mm_nn_shard(h, w_in, "proj", dep=w_in_dep, out_dtype=BF16)
    a_conv = conv_fwd(proj, conv_w, D, "conv_fwd")
    o_sb = sb_fwd(proj, D, "sb_fwd")
    w_conv_out, w_sb_out, w_mem_kv, w_x_out, w_out, mlp_dep, mlp_weights = rest_weights(o_sb)
    kv = mm_nn_shard(mem_n, w_mem_kv, "kv", dep=mlp_dep)
    o_x = xa_fwd(proj, kv, q_norm_g, k_norm_g, D, "xa_fwd")
    ys, merged, x1, h2 = merge_fwd(proj, (a_conv, o_sb, o_x), (w_conv_out, w_sb_out, w_x_out), w_out, x, g_mlp, D,
                                   "merge_x1")
    w_up, w_down = mlp_weights(h2)
    up, act = mm_nn_shard(h2, w_up, "up", relu2=True)
    dy, dy16, loss_parts = mm_nn_loss(act, w_down, x1, target, "x2_loss")
    loss_cols = jnp.sum(loss_parts, axis=0)
    d_up = mm_nt(dy16, w_down, "d_up", up=up, out_dtype=BF16)
    g = {"w_down": mm_tn(act, dy16, "g_w_down")}
    g["w_up"] = mm_tn(h2, d_up, "g_w_up", shard_out=True)
    dh2 = mm_nt_shard(d_up, w_up, "dh2")
    dx1, dx1_16, g["g_mlp"] = rms_bwd(x1, g_mlp, dh2, "rms_mlp_bwd", dres=dy, want16=True)
    g["w_out"] = mm_tn(merged, dx1_16, "g_w_out")
    (dy_c, dy_s, dy_x), (d_a_conv, d_o_sb, d_o_x), d_gate = merge_bwd(
        proj, ys, dx1_16, w_out, (w_conv_out, w_sb_out, w_x_out), D, "merge_bwd", dep=first_grads(g))
    g["w_conv_out"] = mm_tn(a_conv, dy_c, "g_w_conv_out")
    g["w_sb_out"] = mm_tn(o_sb, dy_s, "g_w_sb_out")
    g["w_x_out"] = mm_tn(o_x, dy_x, "g_w_x_out")
    d_xq, d_kv, g["q_norm_g"], g["k_norm_g"] = xa_bwd(proj, kv, q_norm_g, k_norm_g, d_o_x, D, "xa_bwd")
    g["w_mem_kv"] = mm_tn(mem_n, d_kv, "g_w_mem_kv", shard_out=True)
    g["g_mem"] = rms_bwd(mem, g_mem, mm_nt_shard(d_kv, w_mem_kv, "d_mem_n"), "rms_mem_bwd", want_dx=False)
    d_ch, d_cb, d_cc, g["conv_w"] = conv_bwd(proj, conv_w, d_a_conv, D, "conv_bwd", dep=early_grads(g))
    dq, dk, dv = sb_bwd(proj, d_o_sb, D, "sb_bwd", dep=mid_grads([d_ch]))
    d_proj = jnp.concatenate([d_ch, d_cb, d_cc, dq, dk, dv, d_xq, d_gate], axis=1)
    g["w_in"] = mm_tn(h, d_proj, "g_w_in", shard_out=True)
    dh = mm_nt_shard(d_proj, w_in, "dh", dep=late_grads(g["w_in"]))
    grad_x, g["g_mix"] = rms_bwd(x, g_mix, dh, "rms_mix_bwd", dres=dx1, dep=last_grads(dh))
    return loss_cols, grad_x, g


BIG = ("w_in", "w_conv_out", "w_sb_out", "w_mem_kv", "w_x_out", "w_out", "w_up", "w_down")
REST = BIG[1:]
MLP_SIDE = ("w_out", "w_up", "w_down")
MIXER_SIDE = tuple(k for k in REST if k not in MLP_SIDE)
COL_SHARDED = ("w_in", "w_mem_kv", "w_up")
WEIGHTS = ("g_mix", "g_mem", "w_in", "conv_w", "w_conv_out", "w_sb_out", "q_norm_g", "k_norm_g",
           "w_mem_kv", "w_x_out", "w_out", "g_mlp", "w_up", "w_down")


def _pack_small(D, g_mix, g_mem, g_mlp, q_norm_g, k_norm_g, conv_w, last):
    qk = jnp.concatenate([q_norm_g, k_norm_g, jnp.zeros((1, D - 2 * X_HEAD_DIM), F32)], axis=1)
    cw = jnp.pad(conv_w, ((0, 0), (0, D - conv_w.shape[1])))
    return jnp.concatenate([g_mix, g_mem, g_mlp, qk, cw, last], axis=0)


def kernel(x, mem, g_mix, g_mem, w_in, conv_w, w_conv_out, w_sb_out, q_norm_g, k_norm_g, w_mem_kv, w_x_out, w_out, g_mlp, w_up, w_down, loss_target, m_g_mix, m_g_mem, m_w_in, m_conv_w, m_w_conv_out, m_w_sb_out, m_q_norm_g, m_k_norm_g, m_w_mem_kv, m_w_x_out, m_w_out, m_g_mlp, m_w_up, m_w_down, v_g_mix, v_g_mem, v_w_in, v_conv_w, v_w_conv_out, v_w_sb_out, v_q_norm_g, v_k_norm_g, v_w_mem_kv, v_w_x_out, v_w_out, v_g_mlp, v_w_up, v_w_down):
    S, D = x.shape[1], x.shape[2]
    w = dict(g_mix=g_mix, g_mem=g_mem, w_in=w_in, conv_w=conv_w, w_conv_out=w_conv_out, w_sb_out=w_sb_out,
             q_norm_g=q_norm_g, k_norm_g=k_norm_g, w_mem_kv=w_mem_kv, w_x_out=w_x_out, w_out=w_out,
             g_mlp=g_mlp, w_up=w_up, w_down=w_down)
    m = dict(g_mix=m_g_mix, g_mem=m_g_mem, w_in=m_w_in, conv_w=m_conv_w, w_conv_out=m_w_conv_out,
             w_sb_out=m_w_sb_out, q_norm_g=m_q_norm_g, k_norm_g=m_k_norm_g, w_mem_kv=m_w_mem_kv,
             w_x_out=m_w_x_out, w_out=m_w_out, g_mlp=m_g_mlp, w_up=m_w_up, w_down=m_w_down)
    v = dict(g_mix=v_g_mix, g_mem=v_g_mem, w_in=v_w_in, conv_w=v_conv_w, w_conv_out=v_w_conv_out,
             w_sb_out=v_w_sb_out, q_norm_g=v_q_norm_g, k_norm_g=v_k_norm_g, w_mem_kv=v_w_mem_kv,
             w_x_out=v_w_x_out, w_out=v_w_out, g_mlp=v_g_mlp, w_up=v_w_up, w_down=v_w_down)
    chip = 2 * lax.axis_index("x") + lax.axis_index("y")
    cs = conv_w.shape[2]

    place = jnp.stack([chip, lax.axis_index("c")]).astype(jnp.int32)
    cw_block = jnp.pad(conv_w[0], ((0, 5), (0, 0)))
    handle, token = split_start(_gather_plan, 3, [cast_into_full(w["w_in"][0], place, "cast_w_in")], [], [],
                                "gather_w_in_start")
    rest16 = [cast_into_full(w[k][0], place, "cast_" + k, dep=token) for k in REST]
    h = rms_fwd(x[0], g_mix, "rms_mix", dep=token)
    mem_n = rms_fwd(mem[0], g_mem, "rms_mem", dep=token)
    landed, _ = split_wait(_gather_plan, handle, [*rest16, h, mem_n], "gather_w_in_wait")
    (w_in_full,), cw_all = gather_forward(landed, cw_block, "gather_w_in_forward")
    conv_full = jnp.concatenate([cw_all[p, :3] for p in range(N_CHIP)], axis=1)
    rest_handle, rest_token = split_start(_gather_plan, 3 * len(REST), rest16, [], [w_in_full], "gather_rest_start")

    def layout(k, a):
        return a if k in COL_SHARDED else a.reshape(-1, a.shape[-1])

    def rest_weights(after):
        landed, _ = split_wait(_gather_plan, rest_handle, [after], "gather_rest_wait")
        first = gather_forward(landed[:-2], cw_block, "gather_rest_forward")[0]
        mlp_handle, token = split_start(_forward_plan, 6, landed[-2:], [], [first[0]], "forward_mlp_start")

        def mlp_weights(after):
            both, _ = split_wait(_forward_plan, mlp_handle, [after], "forward_mlp_wait")
            return [layout(k, a) for k, a in zip(REST[-2:], both)]

        return [layout(k, a) for k, a in zip(REST[:-2], first)] + [token, mlp_weights]

    def blocks(k, a):
        return a if k in COL_SHARDED else a.reshape(N_CHIP, -1, a.shape[-1])

    early = {}

    def swap_start(names, g, dep, name):
        g4 = [blocks(k, g[k]) for k in names]
        early[names], token = split_start(_exchange_plan, len(names), g4, _exchange_lands(g4), [dep], name)
        return token

    def first_grads(g):
        return swap_start(MLP_SIDE, g, g["g_mlp"], "exchange_mlp_start")

    def early_grads(g):
        return swap_start(MIXER_SIDE, g, g["g_mem"], "exchange_mixers_start")

    def mid_grads(after):
        pairs = {}
        for names, name in ((MLP_SIDE, "exchange_mlp_wait"), (MIXER_SIDE, "exchange_mixers_wait")):
            g4, got = split_wait(_exchange_plan, early[names], after, name)
            pairs.update(zip(names, zip(g4, got)))
        p16 = [pair_sum(*pairs[k], place, "pair_sum_" + k) for k in REST]
        early["fly"], token = split_start(_scatter_plan, 3 * len(REST), p16, _scatter_lands(p16), [],
                                          "scatter_rest_start")
        return token

    late = {}

    def late_grads(gw):
        late["swap"], token = split_start(_exchange_plan, 1, [gw], _exchange_lands([gw]), [], "exchange_w_in_start")
        return token

    def last_grads(after):
        (gw,), (got,) = split_wait(_exchange_plan, late["swap"], [after], "exchange_w_in_wait")
        p16 = [pair_sum(gw, got, place, "pair_sum_w_in")]
        late["fly"], token = split_start(_scatter_plan, 3, p16, _scatter_lands(p16), [], "scatter_w_in_start")
        return token

    loss_cols, grad_x, g = local_step(
        x[0], mem[0], loss_target[0], g_mix, g_mem, q_norm_g, k_norm_g, g_mlp, conv_full, h, mem_n,
        w_in_full, rest_token, rest_weights, first_grads, early_grads, mid_grads, late_grads, last_grads)
    token = g["g_mix"]

    p16_rest, got_rest = split_wait(_scatter_plan, early["fly"], [token], "scatter_rest_wait")
    gsum, delta, new_m, new_v = {}, {}, {}, {}
    halves = [chip_sum(p, b, place, "chip_sum_" + k, dep=token) for k, p, b in zip(REST, p16_rest, got_rest)]
    for k, a in zip(REST, join_halves(halves, "join_halves_rest")):
        gsum[k] = a[None]
        delta[k], new_m[k], new_v[k] = adamw(w[k], gsum[k], m[k], v[k], "adamw_" + k)
    p16_in, got_in = split_wait(_scatter_plan, late["fly"], [new_v[k] for k in REST], "scatter_w_in_wait")

    small = allreduce_small(
        _pack_small(D, g["g_mix"], g["g_mem"], g["g_mlp"], g["q_norm_g"], g["k_norm_g"], g["conv_w"], loss_cols),
        "allreduce_small", dep=got_in[0])
    loss = (0.5 / D) * jnp.sum(small[7])
    gsum.update({"g_mix": small[0:1], "g_mem": small[1:2], "g_mlp": small[2:3],
                 "q_norm_g": small[3:4, :X_HEAD_DIM], "k_norm_g": small[3:4, X_HEAD_DIM:2 * X_HEAD_DIM],
                 "conv_w": lax.dynamic_slice(small[4:7], (0, chip * cs), (3, cs))[None]})
    half_in = chip_sum(p16_in[0], got_in[0], place, "chip_sum_w_in")
    gsum["w_in"] = join_halves([half_in], "join_halves_w_in")[0][None]
    delta["w_in"], new_m["w_in"], new_v["w_in"] = adamw(w["w_in"], gsum["w_in"], m["w_in"], v["w_in"], "adamw_w_in")
    small_names = ("g_mix", "g_mem", "g_mlp", "q_norm_g", "k_norm_g", "conv_w")
    zero_row = jnp.zeros((1, D), F32)
    packed = [_pack_small(D, *[t[k] if k != "conv_w" else t[k][0] for k in small_names], zero_row)
              for t in (w, gsum, m, v)]
    sm = adamw(*packed, "adamw_small")
    for t, block in zip((delta, new_m, new_v), sm):
        t["g_mix"], t["g_mem"], t["g_mlp"] = block[0:1], block[1:2], block[2:3]
        t["q_norm_g"], t["k_norm_g"] = block[3:4, :X_HEAD_DIM], block[3:4, X_HEAD_DIM:2 * X_HEAD_DIM]
        t["conv_w"] = block[4:7, :cs][None]

    return (loss, grad_x[None], *[gsum[k] for k in WEIGHTS], *[delta[k] for k in WEIGHTS],
            *[new_m[k] for k in WEIGHTS], *[new_v[k] for k in WEIGHTS])
```

```python
import jax
import jax.numpy as jnp
from jax import lax
from jax.experimental import pallas as pl
from jax.experimental.pallas import tpu as pltpu

F32 = jnp.float32
BF16 = jnp.bfloat16
EPS = 1e-6
N_CHIP = 4
SB_HEAD_DIM = 64
X_HEAD_DIM = 256
LANES = 128
VMEM_LIMIT = 56 * 1024 * 1024
ROW_BLOCK_ELEMS = 1 << 19
ADAM_LR, ADAM_B1, ADAM_B2, ADAM_EPS, ADAM_WD, ADAM_STEP = 0.001, 0.9, 0.999, 1e-8, 0.01, 10
MESH = pl.DeviceIdType.MESH


def _params(*sem):
    return pltpu.CompilerParams(dimension_semantics=sem, vmem_limit_bytes=VMEM_LIMIT)


def _tile(n, pref):
    if n <= pref:
        return n
    t = 1 << (pref.bit_length() - 1)
    while n % t:
        t //= 2
    return t


NN = (((1,), (0,)), ((), ()))
NT = (((1,), (1,)), ((), ()))
TN = (((0,), (0,)), ((), ()))


def _dot(a, b, dims):
    return lax.dot_general(a.astype(BF16), b.astype(BF16), dims, preferred_element_type=F32)


def mm_nn_shard(a, g, name, relu2=False, dep=None, out_dtype=F32):
    M, K = a.shape
    _, _, Ns = g.shape
    tm, tn = _tile(M, 2048), _tile(Ns, 512)
    nb = Ns // tn

    def body(a_ref, b_ref, *o_refs):
        o_refs = o_refs[len(deps):]
        acc = _dot(a_ref[...], b_ref[...], NN)
        if relu2:
            acc = jnp.maximum(acc, 0.0)
            o_refs[1][...] = (acc * acc).astype(BF16)
        o_refs[0][...] = acc.astype(o_refs[0].dtype)

    o_spec = pl.BlockSpec((tm, tn), lambda i, j: (i, j))
    shapes = [jax.ShapeDtypeStruct((M, N_CHIP * Ns), BF16 if relu2 else out_dtype)]
    specs = [o_spec]
    if relu2:
        shapes.append(jax.ShapeDtypeStruct((M, N_CHIP * Ns), BF16))
        specs.append(o_spec)
    deps = [] if dep is None else [dep]
    out = pl.pallas_call(
        body, grid=(M // tm, N_CHIP * nb), name=name,
        in_specs=[pl.BlockSpec((tm, K), lambda i, j: (i, 0)),
                  pl.BlockSpec((None, K, tn), lambda i, j: (j // nb, 0, j % nb))] + [ANY] * len(deps),
        out_specs=specs, out_shape=shapes, compiler_params=_params("parallel", "parallel"),
    )(a, g, *deps)
    return out if relu2 else out[0]


def mm_nn_loss(a, w, res, target, name):
    M, K = a.shape
    N = w.shape[1]
    tm, tn = _tile(M, 1024), _tile(N, 512)

    def body(a_ref, b_ref, r_ref, t_ref, dy_ref, dy16_ref, l_ref):
        e = (r_ref[...] + _dot(a_ref[...], b_ref[...], NN)) - t_ref[...]
        dy = e * (1.0 / N)
        dy_ref[...] = dy
        dy16_ref[...] = dy.astype(BF16)
        l_ref[...] = jnp.sum(e * e, axis=0, keepdims=True)

    o_spec = pl.BlockSpec((tm, tn), lambda i, j: (i, j))
    return pl.pallas_call(
        body, grid=(M // tm, N // tn), name=name,
        in_specs=[pl.BlockSpec((tm, K), lambda i, j: (i, 0)), pl.BlockSpec((K, tn), lambda i, j: (0, j)),
                  o_spec, o_spec],
        out_specs=[o_spec, o_spec, pl.BlockSpec((None, 1, tn), lambda i, j: (i, 0, j))],
        out_shape=[jax.ShapeDtypeStruct((M, N), F32), jax.ShapeDtypeStruct((M, N), BF16),
                   jax.ShapeDtypeStruct((M // tm, 1, N), F32)],
        compiler_params=_params("parallel", "parallel"),
    )(a, w, res, target)


def mm_nt_relu2(a, w, up, name):
    M, N = a.shape
    R = w.shape[0]
    tm, tr = _tile(M, 2048), _tile(R, 512)

    def body(a_ref, b_ref, up_ref, o_ref):
        acc = _dot(a_ref[...], b_ref[...], NT)
        o_ref[...] = (acc * (2.0 * jnp.maximum(up_ref[...].astype(F32), 0.0))).astype(BF16)

    o_spec = pl.BlockSpec((tm, tr), lambda i, j: (i, j))
    return pl.pallas_call(
        body, grid=(M // tm, R // tr), name=name,
        in_specs=[pl.BlockSpec((tm, N), lambda i, j: (i, 0)), pl.BlockSpec((tr, N), lambda i, j: (j, 0)), o_spec],
        out_specs=o_spec, out_shape=jax.ShapeDtypeStruct((M, R), BF16),
        compiler_params=_params("parallel", "parallel"),
    )(a, w, up)


def mm_nt_shard(a, g, name, out_dtype=F32, dep=None):
    deps = [] if dep is None else [dep]
    M = a.shape[0]
    _, R, Ns = g.shape
    tm, tr, tk = _tile(M, 1024), _tile(R, 1024), _tile(Ns, 2560)
    nb = Ns // tk
    nk = N_CHIP * nb

    def body(a_ref, b_ref, *refs):
        o_ref, acc_ref = refs[len(deps):]
        k = pl.program_id(2)

        @pl.when(k == 0)
        def _():
            acc_ref[...] = jnp.zeros_like(acc_ref)

        acc_ref[...] += _dot(a_ref[...], b_ref[...], NT)

        @pl.when(k == nk - 1)
        def _():
            o_ref[...] = acc_ref[...].astype(out_dtype)

    return pl.pallas_call(
        body, grid=(M // tm, R // tr, nk), name=name,
        in_specs=[pl.BlockSpec((tm, tk), lambda i, j, k: (i, k)),
                  pl.BlockSpec((None, tr, tk), lambda i, j, k: (k // nb, j, k % nb))] + [ANY] * len(deps),
        out_specs=pl.BlockSpec((tm, tr), lambda i, j, k: (i, j)),
        out_shape=jax.ShapeDtypeStruct((M, R), out_dtype),
        scratch_shapes=[pltpu.VMEM((tm, tr), F32)],
        compiler_params=_params("parallel", "parallel", "arbitrary"),
    )(a, g, *deps)


def mm_tn(a, b, name, shard_out=False):
    S, M = a.shape
    N = b.shape[1]
    Ns = N // N_CHIP if shard_out else N
    tm, tn = _tile(M, 1024), _tile(Ns, 512)
    nb = Ns // tn

    def body(a_ref, b_ref, o_ref):
        o_ref[...] = _dot(a_ref[...], b_ref[...], TN)

    if shard_out:
        o_spec = pl.BlockSpec((None, tm, tn), lambda i, j: (j // nb, i, j % nb))
        o_shape = jax.ShapeDtypeStruct((N_CHIP, M, Ns), F32)
    else:
        o_spec = pl.BlockSpec((tm, tn), lambda i, j: (i, j))
        o_shape = jax.ShapeDtypeStruct((M, N), F32)
    return pl.pallas_call(
        body, grid=(M // tm, N // tn), name=name,
        in_specs=[pl.BlockSpec((S, tm), lambda i, j: (0, i)), pl.BlockSpec((S, tn), lambda i, j: (0, j))],
        out_specs=o_spec, out_shape=o_shape, compiler_params=_params("parallel", "parallel"),
    )(a, b)


def rms_fwd(x, g, name, dep=None):
    S, D = x.shape
    tm = _tile(S, 512)
    deps = [] if dep is None else [dep]

    def body(x_ref, g_ref, *refs):
        xv = x_ref[...]
        r = lax.rsqrt(jnp.mean(xv * xv, axis=-1, keepdims=True) + EPS)
        refs[-1][...] = ((xv * r) * g_ref[...]).astype(BF16)

    return pl.pallas_call(
        body, grid=(S // tm,), name=name,
        in_specs=[pl.BlockSpec((tm, D), lambda i: (i, 0)), pl.BlockSpec((1, D), lambda i: (0, 0))]
        + [ANY] * len(deps),
        out_specs=pl.BlockSpec((tm, D), lambda i: (i, 0)),
        out_shape=jax.ShapeDtypeStruct((S, D), BF16), compiler_params=_params("parallel"),
    )(x, g, *deps)


def rms_bwd(x, g, dh, name, dres=None, want_dx=True, want16=False, dep=None):
    S, D = x.shape
    tm = _tile(S, 512)

    def body(x_ref, g_ref, dh_ref, *refs):
        i = pl.program_id(0)
        xv = x_ref[...]
        r = lax.rsqrt(jnp.mean(xv * xv, axis=-1, keepdims=True) + EPS)
        xn = xv * r
        dhv = dh_ref[...].astype(F32)
        gg_ref = refs[-1]

        @pl.when(i == 0)
        def _():
            gg_ref[...] = jnp.zeros_like(gg_ref)

        gg_ref[...] += jnp.sum(dhv * xn, axis=0, keepdims=True)
        if want_dx:
            dxn = dhv * g_ref[...]
            dx = r * (dxn - xn * jnp.mean(dxn * xn, axis=-1, keepdims=True))
            if dres is not None:
                dx = refs[0][...] + dx
            refs[-2][...] = dx.astype(refs[-2].dtype)
            if want16:
                refs[-3][...] = dx

    row = pl.BlockSpec((tm, D), lambda i: (i, 0))
    vec = pl.BlockSpec((1, D), lambda i: (0, 0))
    ins, in_specs = [x, g, dh], [row, vec, row]
    if dres is not None:
        ins.append(dres)
        in_specs.append(row)
    if dep is not None:
        ins.append(dep)
        in_specs.append(ANY)
    shapes, specs = [jax.ShapeDtypeStruct((1, D), F32)], [vec]
    if want_dx:
        if want16:
            shapes.insert(0, jax.ShapeDtypeStruct((S, D), BF16))
            specs.insert(0, row)
        shapes.insert(0, jax.ShapeDtypeStruct((S, D), F32))
        specs.insert(0, row)
    out = pl.pallas_call(body, grid=(S // tm,), name=name, in_specs=in_specs, out_specs=specs,
                         out_shape=shapes, compiler_params=_params("arbitrary"))(*ins)
    return out if want_dx else out[0]


def _shift_down(u, k, row):
    return jnp.where(row >= k, pltpu.roll(u, k, axis=0), 0.0)


def _shift_up(u, k, row):
    S = u.shape[0]
    return jnp.where(row < S - k, pltpu.roll(u, S - k, axis=0), 0.0)


def _conv_specs(S, D, tc):
    nb = D // tc
    col = lambda o: pl.BlockSpec((S, tc), lambda j, o=o: (0, o * nb + j))
    return col, pl.BlockSpec((3, tc), lambda j: (0, j))


def conv_fwd(proj, conv_w, D, name):
    S = proj.shape[0]
    tc = _tile(D, 256)
    col, wspec = _conv_specs(S, D, tc)

    def body(ch_ref, cb_ref, cc_ref, w_ref, a_ref):
        row = lax.broadcasted_iota(jnp.int32, (S, tc), 0)
        u = cc_ref[...].astype(F32) * ch_ref[...].astype(F32)
        w = w_ref[...]
        cv = w[0:1, :] * _shift_down(u, 2, row) + w[1:2, :] * _shift_down(u, 1, row) + w[2:3, :] * u
        a_ref[...] = (cb_ref[...].astype(F32) * cv).astype(BF16)

    return pl.pallas_call(
        body, grid=(D // tc,), name=name, in_specs=[col(0), col(1), col(2), wspec],
        out_specs=pl.BlockSpec((S, tc), lambda j: (0, j)),
        out_shape=jax.ShapeDtypeStruct((S, D), BF16), compiler_params=_params("parallel"),
    )(proj, proj, proj, conv_w)


def conv_bwd(proj, conv_w, da, D, name, dep):
    S = proj.shape[0]
    tc = _tile(D, 256)
    col, wspec = _conv_specs(S, D, tc)
    blk = pl.BlockSpec((S, tc), lambda j: (0, j))

    def body(ch_ref, cb_ref, cc_ref, w_ref, da_ref, dep_ref, dch_ref, dcb_ref, dcc_ref, gw_ref):
        row = lax.broadcasted_iota(jnp.int32, (S, tc), 0)
        ch, cb, cc, dav = [r[...].astype(F32) for r in (ch_ref, cb_ref, cc_ref, da_ref)]
        w = w_ref[...]
        u = cc * ch
        u1, u2 = _shift_down(u, 1, row), _shift_down(u, 2, row)
        cv = w[0:1, :] * u2 + w[1:2, :] * u1 + w[2:3, :] * u
        dcb_ref[...] = (dav * cv).astype(BF16)
        dcv = dav * cb
        gw_ref[0:1, :] = jnp.sum(dcv * u2, axis=0, keepdims=True)
        gw_ref[1:2, :] = jnp.sum(dcv * u1, axis=0, keepdims=True)
        gw_ref[2:3, :] = jnp.sum(dcv * u, axis=0, keepdims=True)
        du = w[2:3, :] * dcv + w[1:2, :] * _shift_up(dcv, 1, row) + w[0:1, :] * _shift_up(dcv, 2, row)
        dcc_ref[...] = (du * ch).astype(BF16)
        dch_ref[...] = (du * cc).astype(BF16)

    act = jax.ShapeDtypeStruct((S, D), BF16)
    return pl.pallas_call(
        body, grid=(D // tc,), name=name, in_specs=[col(0), col(1), col(2), wspec, blk, ANY],
        out_specs=[blk, blk, blk, wspec], out_shape=[act, act, act, jax.ShapeDtypeStruct((3, D), F32)],
        compiler_params=_params("parallel"),
    )(proj, proj, proj, conv_w, da, dep)


SB_BQ_FWD = 512
SB_BQ_BWD = 256
SB_BK = 128
SB_GROUP = 4


def _sb_consts(bq):
    lane = lax.broadcasted_iota(jnp.int32, (bq, LANES), 1)
    r = lax.broadcasted_iota(jnp.int32, (SB_BK, SB_BK), 0)
    c = lax.broadcasted_iota(jnp.int32, (SB_BK, SB_BK), 1)
    tri_rev = jnp.where(r > c, 1.0, 0.0).astype(BF16)
    tri_fwd = jnp.where(r < c, 1.0, 0.0).astype(BF16)
    return lane, tri_rev, tri_fwd


def _cumsum2(v, tri):
    hi = v.astype(BF16)
    lo = (v - hi.astype(F32)).astype(BF16)
    part = (lax.dot_general(hi, tri, NN, preferred_element_type=F32)
            + lax.dot_general(lo, tri, NN, preferred_element_type=F32))
    return part, jnp.sum(v, axis=1, keepdims=True)


def _sb_logits(z, past):
    sp = jnp.log(1.0 + jnp.exp(-jnp.abs(z)))
    l = jnp.minimum(z, 0.0) - sp
    m = l - z
    if past is not None:
        m = jnp.where(past, m, 0.0)
    return l, m


def _stack_heads(v, lane):
    return jnp.concatenate([jnp.where(lane < SB_HEAD_DIM, v, 0.0), jnp.where(lane >= SB_HEAD_DIM, v, 0.0)],
                           axis=0).astype(BF16)


def _unstack_heads(v, lane):
    bq = v.shape[0] // 2
    return jnp.where(lane < SB_HEAD_DIM, v[:bq], v[bq:])


def _sb_positions(i, bq):
    r = lax.broadcasted_iota(jnp.int32, (2 * bq, SB_BK), 0)
    trow = i * bq + jnp.where(r >= bq, r - bq, r)
    return trow, lax.broadcasted_iota(jnp.int32, (2 * bq, SB_BK), 1)


def _sb_specs(S, D, bq):
    npair = D // LANES
    qspec = pl.BlockSpec((bq, LANES), lambda p, i: (i, 3 * npair + p))
    kspec = pl.BlockSpec((S, LANES), lambda p, i: (0, 4 * npair + p))
    vspec = pl.BlockSpec((S, LANES), lambda p, i: (0, 5 * npair + p))
    return npair, qspec, kspec, vspec


def sb_fwd(proj, D, name):
    S = proj.shape[0]
    bq = min(SB_BQ_FWD, S)
    nd = bq // SB_BK
    npair, qspec, kspec, vspec = _sb_specs(S, D, bq)
    scale = SB_HEAD_DIM ** -0.5

    def body(q_ref, k_ref, v_ref, o_ref, kb_ref, vb_ref):
        i = pl.program_id(1)

        @pl.when(i == 0)
        def _():
            kb_ref[...] = k_ref[...].astype(BF16)
            vb_ref[...] = v_ref[...].astype(BF16)

        lane, tri_rev, _ = _sb_consts(bq)
        qs = _stack_heads(q_ref[...].astype(F32) * scale, lane)
        trow, scol = _sb_positions(i, bq)

        def steps(j0, carry, n, masked):
            ks = [pl.multiple_of((j0 - t) * SB_BK, SB_BK) for t in range(n)]
            past = [(k + scol) < trow if masked else None for k in ks]
            zs = [lax.dot_general(qs, kb_ref[pl.ds(k, SB_BK), :], NT, preferred_element_type=F32) for k in ks]
            lm = [_sb_logits(z, p) for z, p in zip(zs, past)]
            cs = [_cumsum2(m, tri_rev) for _, m in lm]
            c, acc = carry
            for t in range(n):
                a = jnp.exp(lm[t][0] + (cs[t][0] + c))
                if masked:
                    a = jnp.where(past[t], a, 0.0)
                acc = acc + lax.dot_general(a.astype(BF16), vb_ref[pl.ds(ks[t], SB_BK), :], NN,
                                            preferred_element_type=F32)
                c = c + cs[t][1]
            return c, acc

        carry = (jnp.zeros((2 * bq, 1), F32), jnp.zeros((2 * bq, LANES), F32))
        carry = steps(i * nd + nd - 1, carry, nd, True)
        older = i * nd
        groups = older // SB_GROUP
        carry = lax.fori_loop(
            0, groups, lambda t, cr: steps(older - 1 - t * SB_GROUP, cr, SB_GROUP, False), carry)
        rest = older - groups * SB_GROUP
        carry = lax.fori_loop(0, rest // nd, lambda t, cr: steps(rest - 1 - t * nd, cr, nd, False), carry)
        o_ref[...] = _unstack_heads(carry[1], lane).astype(BF16)

    return pl.pallas_call(
        body, grid=(npair, S // bq), name=name, in_specs=[qspec, kspec, vspec],
        out_specs=pl.BlockSpec((bq, LANES), lambda p, i: (i, p)),
        out_shape=jax.ShapeDtypeStruct((S, D), BF16),
        scratch_shapes=[pltpu.VMEM((S, LANES), BF16), pltpu.VMEM((S, LANES), BF16)],
        compiler_params=_params("parallel", "arbitrary"),
    )(proj, proj, proj)


def sb_bwd(proj, do, D, name, dep=None):
    S = proj.shape[0]
    bq = min(SB_BQ_BWD, S)
    nd = bq // SB_BK
    nkb = S // SB_BK
    npair, qspec, kspec, vspec = _sb_specs(S, D, bq)
    scale = SB_HEAD_DIM ** -0.5

    def body(q_ref, k_ref, v_ref, do_ref, *refs):
        dq_ref, dk_ref, dv_ref, kb_ref, vb_ref, dk_acc, dv_acc, g_scr, b_scr, a_scr = refs[len(deps):]
        i = pl.program_id(1)

        @pl.when(i == 0)
        def _():
            kb_ref[...] = k_ref[...].astype(BF16)
            vb_ref[...] = v_ref[...].astype(BF16)
            dk_acc[...] = jnp.zeros_like(dk_acc)
            dv_acc[...] = jnp.zeros_like(dv_acc)

        lane, tri_rev, tri_fwd = _sb_consts(bq)
        qs = _stack_heads(q_ref[...].astype(F32) * scale, lane)
        dos = _stack_heads(do_ref[...].astype(F32), lane)
        qs_t, dos_t = qs.T, dos.T
        trow, scol = _sb_positions(i, bq)

        def sweep1(j0, c, n, masked):
            js = [j0 - t for t in range(n)]
            ks = [pl.multiple_of(j * SB_BK, SB_BK) for j in js]
            past = [(k + scol) < trow if masked else None for k in ks]
            zs = [lax.dot_general(qs, kb_ref[pl.ds(k, SB_BK), :], NT, preferred_element_type=F32) for k in ks]
            das = [lax.dot_general(dos, vb_ref[pl.ds(k, SB_BK), :], NT, preferred_element_type=F32) for k in ks]
            lm = [_sb_logits(z, p) for z, p in zip(zs, past)]
            cs = [_cumsum2(m, tri_rev) for _, m in lm]
            for t in range(n):
                b_scr[js[t]] = jnp.exp(lm[t][0]).astype(BF16)
            for t in range(n):
                a = jnp.exp(lm[t][0] + (cs[t][0] + c))
                if masked:
                    a = jnp.where(past[t], a, 0.0)
                g_scr[js[t]] = (das[t] * a).astype(BF16)
                a_scr[js[t]] = a.astype(BF16)
                c = c + cs[t][1]
            return c

        older = i * nd
        groups = older // SB_GROUP
        rest = older - groups * SB_GROUP
        c = jnp.zeros((2 * bq, 1), F32)
        c = sweep1(i * nd + nd - 1, c, nd, True)
        c = lax.fori_loop(0, groups, lambda t, cr: sweep1(older - 1 - t * SB_GROUP, cr, SB_GROUP, False), c)
        lax.fori_loop(0, rest // nd, lambda t, cr: sweep1(rest - 1 - t * nd, cr, nd, False), c)

        def sweep2(j0, carry, n, masked):
            js = [j0 + t for t in range(n)]
            ks = [pl.multiple_of(j * SB_BK, SB_BK) for j in js]
            g16 = [g_scr[j] for j in js]
            gv = [g.astype(F32) for g in g16]
            gs = [(lax.dot_general(g, tri_fwd, NN, preferred_element_type=F32), jnp.sum(v, axis=1, keepdims=True))
                  for g, v in zip(g16, gv)]
            pc, dq = carry
            dzs = []
            for t in range(n):
                dz = gv[t] - b_scr[js[t]].astype(F32) * (gv[t] + (gs[t][0] + pc))
                if masked:
                    dz = jnp.where((ks[t] + scol) < trow, dz, 0.0)
                dzs.append(dz.astype(BF16))
                pc = pc + gs[t][1]
            for t in range(n):
                dq = dq + lax.dot_general(dzs[t], kb_ref[pl.ds(ks[t], SB_BK), :], NN, preferred_element_type=F32)
                dk_acc[js[t]] += lax.dot_general(qs_t, dzs[t], NN, preferred_element_type=F32)
                dv_acc[js[t]] += lax.dot_general(dos_t, a_scr[js[t]], NN, preferred_element_type=F32)
            return pc, dq

        carry = (jnp.zeros((2 * bq, 1), F32), jnp.zeros((2 * bq, LANES), F32))
        carry = lax.fori_loop(0, groups, lambda t, cr: sweep2(t * SB_GROUP, cr, SB_GROUP, False), carry)
        carry = lax.fori_loop(
            0, rest // nd, lambda t, cr: sweep2(groups * SB_GROUP + t * nd, cr, nd, False), carry)
        carry = sweep2(i * nd, carry, nd, True)
        dq_ref[...] = (_unstack_heads(carry[1], lane) * scale).astype(BF16)

        @pl.when(i == pl.num_programs(1) - 1)
        def _():
            for j in range(nkb):
                dk_ref[j * SB_BK:(j + 1) * SB_BK, :] = dk_acc[j].T.astype(BF16)
                dv_ref[j * SB_BK:(j + 1) * SB_BK, :] = dv_acc[j].T.astype(BF16)

    deps = [] if dep is None else [dep]
    full = pl.BlockSpec((S, LANES), lambda p, i: (0, p))
    blk = pl.BlockSpec((bq, LANES), lambda p, i: (i, p))
    act = jax.ShapeDtypeStruct((S, D), BF16)
    return pl.pallas_call(
        body, grid=(npair, S // bq), name=name, in_specs=[qspec, kspec, vspec, blk] + [ANY] * len(deps),
        out_specs=[blk, full, full], out_shape=[act, act, act],
        scratch_shapes=[pltpu.VMEM((S, LANES), BF16), pltpu.VMEM((S, LANES), BF16),
                        pltpu.VMEM((nkb, LANES, SB_BK), F32), pltpu.VMEM((nkb, LANES, SB_BK), F32),
                        pltpu.VMEM((nkb, 2 * bq, SB_BK), BF16), pltpu.VMEM((nkb, 2 * bq, SB_BK), BF16),
                        pltpu.VMEM((nkb, 2 * bq, SB_BK), BF16)],
        compiler_params=_params("parallel", "arbitrary"),
    )(proj, proj, proj, do, *deps)


def _rms_rows(v):
    r = lax.rsqrt(jnp.mean(v * v, axis=-1, keepdims=True) + EPS)
    return v * r, r


def _xa_specs(S, D, M, tq):
    nh = D // X_HEAD_DIM
    qspec = pl.BlockSpec((tq, X_HEAD_DIM), lambda h, i: (i, 6 * nh + h))
    kspec = pl.BlockSpec((M, X_HEAD_DIM), lambda h, i: (0, h))
    vspec = pl.BlockSpec((M, X_HEAD_DIM), lambda h, i: (0, nh + h))
    gspec = pl.BlockSpec((1, X_HEAD_DIM), lambda h, i: (0, 0))
    return nh, qspec, kspec, vspec, gspec


def xa_fwd(proj, kv, gq, gk, D, name):
    S, M = proj.shape[0], kv.shape[0]
    tq = _tile(S, 2048)
    nh, qspec, kspec, vspec, gspec = _xa_specs(S, D, M, tq)
    scale = X_HEAD_DIM ** -0.5

    def body(q_ref, k_ref, v_ref, gq_ref, gk_ref, o_ref):
        qn = _rms_rows(q_ref[...].astype(F32))[0] * gq_ref[...]
        kn = _rms_rows(k_ref[...])[0] * gk_ref[...]
        s = _dot(qn, kn, NT) * scale
        e = jnp.exp(s - jnp.max(s, axis=-1, keepdims=True))
        p = e / jnp.sum(e, axis=-1, keepdims=True)
        o_ref[...] = _dot(p, v_ref[...], NN).astype(BF16)

    return pl.pallas_call(
        body, grid=(nh, S // tq), name=name, in_specs=[qspec, kspec, vspec, gspec, gspec],
        out_specs=pl.BlockSpec((tq, X_HEAD_DIM), lambda h, i: (i, h)),
        out_shape=jax.ShapeDtypeStruct((S, D), BF16), compiler_params=_params("parallel", "parallel"),
    )(proj, kv, kv, gq, gk)


def xa_bwd(proj, kv, gq, gk, do, D, name):
    S, M = proj.shape[0], kv.shape[0]
    tq = _tile(S, 2048)
    nh, qspec, kspec, vspec, gspec = _xa_specs(S, D, M, tq)
    scale = X_HEAD_DIM ** -0.5

    def body(q_ref, k_ref, v_ref, gq_ref, gk_ref, do_ref, dq_ref, dk_ref, dv_ref, ggq_ref, ggk_ref,
             dkn_acc, dv_acc):
        h, i = pl.program_id(0), pl.program_id(1)

        @pl.when((h == 0) & (i == 0))
        def _():
            ggq_ref[...] = jnp.zeros_like(ggq_ref)
            ggk_ref[...] = jnp.zeros_like(ggk_ref)

        @pl.when(i == 0)
        def _():
            dkn_acc[...] = jnp.zeros_like(dkn_acc)
            dv_acc[...] = jnp.zeros_like(dv_acc)

        gq, gk = gq_ref[...], gk_ref[...]
        qhat, rq = _rms_rows(q_ref[...].astype(F32))
        khat, rk = _rms_rows(k_ref[...])
        qn, kn = qhat * gq, khat * gk
        s = _dot(qn, kn, NT) * scale
        e = jnp.exp(s - jnp.max(s, axis=-1, keepdims=True))
        p = e / jnp.sum(e, axis=-1, keepdims=True)
        dov = do_ref[...]
        dv_acc[...] += _dot(p, dov, TN)
        dp = _dot(dov, v_ref[...], NT)
        ds = (p * (dp - jnp.sum(dp * p, axis=-1, keepdims=True))) * scale
        dqn = _dot(ds, kn, NN)
        dkn_acc[...] += _dot(ds, qn, TN)
        ggq_ref[...] += jnp.sum(dqn * qhat, axis=0, keepdims=True)
        dqh = dqn * gq
        dq_ref[...] = (rq * (dqh - qhat * jnp.mean(dqh * qhat, axis=-1, keepdims=True))).astype(BF16)

        @pl.when(i == pl.num_programs(1) - 1)
        def _():
            dkn = dkn_acc[...]
            ggk_ref[...] += jnp.sum(dkn * khat, axis=0, keepdims=True)
            dkh = dkn * gk
            dk_ref[...] = (rk * (dkh - khat * jnp.mean(dkh * khat, axis=-1, keepdims=True))).astype(BF16)
            dv_ref[...] = dv_acc[...].astype(BF16)

    blk = pl.BlockSpec((tq, X_HEAD_DIM), lambda h, i: (i, h))
    kv_shape = jax.ShapeDtypeStruct((M, 2 * D), BF16)
    gshape = jax.ShapeDtypeStruct((1, X_HEAD_DIM), F32)
    dq, dk, dv, ggq, ggk = pl.pallas_call(
        body, grid=(nh, S // tq), name=name, in_specs=[qspec, kspec, vspec, gspec, gspec, blk],
        out_specs=[blk, kspec, vspec, gspec, gspec],
        out_shape=[jax.ShapeDtypeStruct((S, D), BF16), kv_shape, kv_shape, gshape, gshape],
        scratch_shapes=[pltpu.VMEM((M, X_HEAD_DIM), F32), pltpu.VMEM((M, X_HEAD_DIM), F32)],
        compiler_params=_params("arbitrary", "arbitrary"),
    )(proj, kv, kv, gq, gk, do)
    d_kv = jnp.concatenate([dk[:, :D], dv[:, D:]], axis=1)
    return dq, d_kv, ggq, ggk


def _gate_specs(S, D, tm):
    row = pl.BlockSpec((tm, D), lambda i: (i, 0))
    gate = lambda b: pl.BlockSpec((tm, D), lambda i, b=b: (i, 7 + b))
    return row, gate


def merge_fwd(proj, branches, w_branches, w_out, x, g_next, D, name):
    S = proj.shape[0]
    tm = _tile(S, 256)
    row, gate = _gate_specs(S, D, tm)
    mat = pl.BlockSpec((D, D), lambda i: (0, 0))

    def body(g0, g1, g2, b0, b1, b2, w0, w1, w2, w_ref, x_ref, gn_ref, y0, y1, y2, m_ref, o_ref, h_ref):
        acc = None
        for g_ref, b_ref, wb_ref, y_ref in ((g0, b0, w0, y0), (g1, b1, w1, y1), (g2, b2, w2, y2)):
            y = _dot(b_ref[...], wb_ref[...], NN).astype(BF16)
            y_ref[...] = y
            term = jax.nn.sigmoid(g_ref[...].astype(F32)) * y.astype(F32)
            acc = term if acc is None else acc + term
        merged = acc.astype(BF16)
        m_ref[...] = merged
        x1 = x_ref[...] + _dot(merged, w_ref[...], NN)
        o_ref[...] = x1
        h_ref[...] = (_rms_rows(x1)[0] * gn_ref[...]).astype(BF16)

    act = jax.ShapeDtypeStruct((S, D), BF16)
    out = pl.pallas_call(
        body, grid=(S // tm,), name=name,
        in_specs=[gate(0), gate(1), gate(2), row, row, row, mat, mat, mat, mat, row,
                  pl.BlockSpec((1, D), lambda i: (0, 0))],
        out_specs=[row] * 6, out_shape=[act, act, act, act, jax.ShapeDtypeStruct((S, D), F32), act],
        compiler_params=_params("parallel"),
    )(proj, proj, proj, *branches, *w_branches, w_out, x, g_next)
    return out[:3], out[3], out[4], out[5]


def merge_bwd(proj, ys, dx1, w_out, w_branches, D, name, dep):
    S = proj.shape[0]
    tm = _tile(S, 256)
    row, gate = _gate_specs(S, D, tm)
    mat = pl.BlockSpec((D, D), lambda i: (0, 0))

    def body(g0, g1, g2, y0, y1, y2, dx_ref, w_ref, w0, w1, w2, dep_ref, d0, d1, d2, p0, p1, p2, dg_ref):
        dmv = _dot(dx_ref[...], w_ref[...], NT)
        for b, (g_ref, y_ref, wb_ref, d_ref, p_ref) in enumerate(
                ((g0, y0, w0, d0, p0), (g1, y1, w1, d1, p1), (g2, y2, w2, d2, p2))):
            s = jax.nn.sigmoid(g_ref[...].astype(F32))
            dy = (dmv * s).astype(BF16)
            d_ref[...] = dy
            p_ref[...] = _dot(dy, wb_ref[...], NT).astype(BF16)
            dg_ref[:, b * D:(b + 1) * D] = ((dmv * y_ref[...].astype(F32)) * (s * (1.0 - s))).astype(BF16)

    act = jax.ShapeDtypeStruct((S, D), BF16)
    out = pl.pallas_call(
        body, grid=(S // tm,), name=name,
        in_specs=[gate(0), gate(1), gate(2), row, row, row, row, mat, mat, mat, mat, ANY],
        out_specs=[row] * 6 + [pl.BlockSpec((tm, 3 * D), lambda i: (i, 0))],
        out_shape=[act] * 6 + [jax.ShapeDtypeStruct((S, 3 * D), BF16)], compiler_params=_params("parallel"),
    )(proj, proj, proj, *ys, dx1, w_out, *w_branches, dep)
    return out[:3], out[3:6], out[6]


def _rows2d(a):
    return a.reshape(-1, a.shape[-1])


def _ew_call(fn, ins, out_dtypes, name):
    R, C = ins[0].shape
    tr = _tile(R, max(8, ROW_BLOCK_ELEMS // C))
    spec = pl.BlockSpec((tr, C), lambda i: (i, 0))

    def body(*refs):
        outs = fn(*[r[...] for r in refs[:len(ins)]])
        for o_ref, o in zip(refs[len(ins):], outs):
            o_ref[...] = o.astype(o_ref.dtype)

    return pl.pallas_call(
        body, grid=(R // tr,), name=name, in_specs=[spec] * len(ins), out_specs=[spec] * len(out_dtypes),
        out_shape=[jax.ShapeDtypeStruct((R, C), d) for d in out_dtypes], compiler_params=_params("parallel"),
    )(*ins)


def adamw(w, g, m, v, name):
    def fn(w, g, m, v):
        m = ADAM_B1 * m + (1.0 - ADAM_B1) * g
        v = ADAM_B2 * v + (1.0 - ADAM_B2) * (g * g)
        m_hat = m / (1.0 - ADAM_B1 ** ADAM_STEP)
        v_hat = v / (1.0 - ADAM_B2 ** ADAM_STEP)
        return -ADAM_LR * (m_hat / (jnp.sqrt(v_hat) + ADAM_EPS) + ADAM_WD * w), m, v

    shp = w.shape
    outs = _ew_call(fn, [_rows2d(a) for a in (w, g, m, v)], [F32, F32, F32], name)
    return [o.reshape(shp) for o in outs]


def _placed_call(fn, place, grid, ins, in_specs, out_shape, out_specs, name, dep=None):
    n = len(ins)
    deps = [] if dep is None else [dep]

    def body(place_ref, *refs):
        outs = fn(*[r[...] for r in refs[:n]])
        for o_ref, o in zip(refs[n + len(deps):], outs):
            o_ref[...] = o.astype(o_ref.dtype)

    return pl.pallas_call(
        body, name=name, out_shape=out_shape,
        grid_spec=pltpu.PrefetchScalarGridSpec(
            num_scalar_prefetch=1, grid=grid, in_specs=list(in_specs) + [ANY] * len(deps), out_specs=out_specs),
        compiler_params=_params(*["parallel"] * len(grid)),
    )(place, *ins, *deps)


def _row_tile(R, C):
    return _tile(R, max(16, ROW_BLOCK_ELEMS // C))


def cast_into_full(w, place, name, dep=None):
    R, C = w.shape
    tr = _row_tile(R, C)
    return _placed_call(
        lambda a: (a,), place, (R // tr,), [w], [pl.BlockSpec((tr, C), lambda i, p: (i, 0))],
        [jax.ShapeDtypeStruct((N_CHIP, R, C), BF16)], [pl.BlockSpec((None, tr, C), lambda i, p: (p[0], i, 0))],
        name, dep=dep)[0]


def pair_sum(g4, got, place, name):
    _, hr, C = got.shape
    tr = _row_tile(hr, C)
    nb = hr // tr
    blk = pl.BlockSpec((None, tr, C), lambda s, i, p: (s, i, 0))
    return _placed_call(
        lambda a, b: (a + b,), place, (N_CHIP, nb), [g4, got],
        [pl.BlockSpec((None, tr, C), lambda s, i, p: (s, p[1] * nb + i, 0)), blk],
        [jax.ShapeDtypeStruct(got.shape, BF16)], [blk], name)[0]


def chip_sum(p32, got, place, name, dep=None):
    _, H, C = p32.shape
    tr = _row_tile(H, C)
    nb = H // tr
    peer = lambda j: pl.BlockSpec((None, tr, C), lambda i, p, j=j: (j, i, 0))
    return _placed_call(
        lambda a, b, c, d: (((a.astype(F32) + b.astype(F32)) + c.astype(F32)) + d.astype(F32),), place, (nb,),
        [p32, got, got, got], [pl.BlockSpec((None, tr, C), lambda i, p: (p[0], i, 0)), peer(0), peer(1), peer(2)],
        [jax.ShapeDtypeStruct((2 * H, C), F32)], [pl.BlockSpec((tr, C), lambda i, p: (p[1] * nb + i, 0))],
        name, dep=dep)[0]


ANY = pl.BlockSpec(memory_space=pl.ANY)
CHIP_FLIPS = ((1, 0), (0, 1), (1, 1))


def _place():
    return lax.axis_index("x"), lax.axis_index("y"), lax.axis_index("c")


def _flip(v, f):
    return 1 - v if f else v


def join_halves(fulls, name):
    n = len(fulls)

    def body(*refs):
        outs = refs[n:2 * n]
        send_sem, recv_sem = refs[2 * n:]
        x, y, c = _place()
        copies = []
        for a in range(n):
            hr = outs[a].shape[0] // 2
            half = outs[a].at[pl.ds(c * hr, hr), :]
            cp = pltpu.make_async_remote_copy(
                src_ref=half, dst_ref=half, send_sem=send_sem.at[a], recv_sem=recv_sem.at[a],
                device_id=(x, y, 1 - c), device_id_type=MESH)
            cp.start()
            copies.append(cp)
        for a, cp in enumerate(copies):
            hr = outs[a].shape[0] // 2
            theirs = outs[a].at[pl.ds((1 - c) * hr, hr), :]
            cp.wait_send()
            pltpu.make_async_remote_copy(
                src_ref=theirs, dst_ref=theirs, send_sem=send_sem.at[a], recv_sem=recv_sem.at[a],
                device_id=(x, y, 1 - c), device_id_type=MESH).wait_recv()

    dma = pltpu.SemaphoreType.DMA
    return pl.pallas_call(
        body, name=name, in_specs=[ANY] * n, out_specs=[ANY] * n,
        out_shape=[jax.ShapeDtypeStruct(f.shape, F32) for f in fulls],
        input_output_aliases={a: a for a in range(n)},
        scratch_shapes=[dma((n,)), dma((n,))],
    )(*fulls)


HBM = pl.BlockSpec(memory_space=pltpu.HBM)
SEM = pl.BlockSpec(memory_space=pltpu.SEMAPHORE)
EFFECT = pltpu.SideEffectType.DATAFLOW_SIDE_EFFECTING


def _in_hbm(a):
    return pltpu.with_memory_space_constraint(a, pltpu.HBM)


def _gather_half(ref, chip_idx, core):
    hr = ref.shape[1] // 2
    return ref.at[chip_idx, pl.ds(core * hr, hr), :]


def gather_forward(fulls, small, name):
    n = len(fulls)

    def body(*refs):
        small_in = refs[n]
        outs, small_out = refs[n + 1:2 * n + 1], refs[2 * n + 1]
        send_sem, recv_sem, sm_send, sm_recv, loc_sem = refs[2 * n + 2:]
        x, y, c = _place()
        mine = 2 * x + y
        chips = [(_flip(x, fx), _flip(y, fy)) for fx, fy in CHIP_FLIPS]
        local = pltpu.make_async_copy(small_in, small_out.at[mine], loc_sem)
        local.start()
        copies = []
        for j, (px, py) in enumerate(chips):
            cp = pltpu.make_async_remote_copy(
                src_ref=small_in, dst_ref=small_out.at[mine], send_sem=sm_send.at[j], recv_sem=sm_recv.at[j],
                device_id=(px, py, c), device_id_type=MESH)
            cp.start()
            copies.append(cp)
        for a in range(n):
            for j, (px, py) in enumerate(chips):
                src = _gather_half(outs[a], 2 * px + py, c)
                cp = pltpu.make_async_remote_copy(
                    src_ref=src, dst_ref=src, send_sem=send_sem.at[3 * a + j], recv_sem=recv_sem.at[3 * a + j],
                    device_id=(x, y, 1 - c), device_id_type=MESH)
                cp.start()
                copies.append(cp)
        for a in range(n):
            for j, (px, py) in enumerate(chips):
                dst = _gather_half(outs[a], 2 * px + py, 1 - c)
                pltpu.make_async_remote_copy(
                    src_ref=dst, dst_ref=dst, send_sem=send_sem.at[3 * a + j], recv_sem=recv_sem.at[3 * a + j],
                    device_id=(x, y, 1 - c), device_id_type=MESH).wait_recv()
        for j, (px, py) in enumerate(chips):
            dst = small_out.at[2 * px + py]
            pltpu.make_async_remote_copy(
                src_ref=dst, dst_ref=dst, send_sem=sm_send.at[j], recv_sem=sm_recv.at[j],
                device_id=(px, py, c), device_id_type=MESH).wait_recv()
        for cp in copies:
            cp.wait_send()
        local.wait()

    dma = pltpu.SemaphoreType.DMA
    out = pl.pallas_call(
        body, name=name, in_specs=[ANY] * (n + 1), out_specs=[ANY] * (n + 1),
        out_shape=[jax.ShapeDtypeStruct(f.shape, f.dtype) for f in fulls]
        + [jax.ShapeDtypeStruct((N_CHIP,) + small.shape, small.dtype)],
        input_output_aliases={a: a for a in range(n)},
        scratch_shapes=[dma((3 * n,)), dma((3 * n,)), dma((3,)), dma((3,)), dma],
    )(*fulls, small)
    return out[:n], out[n]


def _gather_plan(fulls, lands):
    x, y, c = _place()
    mine = 2 * x + y
    return [(_gather_half(f, mine, c), _gather_half(f, mine, c), (_flip(x, fx), _flip(y, fy), c))
            for f in fulls for fx, fy in CHIP_FLIPS]


def _scatter_plan(parts, lands):
    x, y, c = _place()
    plan = []
    for p, l in zip(parts, lands):
        for j, (fx, fy) in enumerate(CHIP_FLIPS):
            px, py = _flip(x, fx), _flip(y, fy)
            plan.append((p.at[2 * px + py], l.at[j], (px, py, c)))
    return plan


def _scatter_lands(parts):
    return [(3,) + p.shape[1:] for p in parts]


def _exchange_plan(grads, lands):
    x, y, c = _place()
    plan = []
    for g, l in zip(grads, lands):
        hr = g.shape[1] // 2
        plan.append((g.at[:, pl.ds((1 - c) * hr, hr), :], l, (x, y, 1 - c)))
    return plan


def _exchange_lands(grads):
    return [(N_CHIP, g.shape[1] // 2, g.shape[2]) for g in grads]


def _forward_plan(fulls, lands):
    x, y, c = _place()
    return [(_gather_half(f, 2 * _flip(x, fx) + _flip(y, fy), c), _gather_half(f, 2 * _flip(x, fx) + _flip(y, fy), c),
             (x, y, 1 - c)) for f in fulls for fx, fy in CHIP_FLIPS]


def split_start(plan, copies, srcs, land_shapes, deps, name):
    n, m = len(srcs), len(land_shapes)
    lands = [lax.empty(s, srcs[0].dtype) for s in land_shapes]

    def body(*refs):
        k0 = n + m + len(deps)
        send_sem, recv_sem = refs[k0], refs[k0 + 1]
        thru, token = refs[k0 + 2:k0 + 2 + n + m], refs[k0 + 2 + n + m]
        for k, (src, dst, dev) in enumerate(plan(thru[:n], thru[n:])):
            pltpu.make_async_remote_copy(src_ref=src, dst_ref=dst, send_sem=send_sem.at[k], recv_sem=recv_sem.at[k],
                                         device_id=dev, device_id_type=MESH).start()
        token[...] = jnp.zeros_like(token)

    dma = pltpu.SemaphoreType.DMA
    arrays = list(srcs) + lands
    out = pl.pallas_call(
        body, name=name,
        out_shape=(dma((copies,)), dma((copies,)), *[pltpu.HBM(a.shape, a.dtype) for a in arrays],
                   jax.ShapeDtypeStruct((8, LANES), F32)),
        in_specs=[HBM] * (n + m) + [ANY] * len(deps),
        out_specs=(SEM, SEM, *[HBM] * (n + m), pl.BlockSpec(memory_space=pltpu.VMEM)),
        input_output_aliases={a: 2 + a for a in range(n + m)},
        compiler_params=pltpu.CompilerParams(has_side_effects=EFFECT),
    )(*[_in_hbm(a) for a in arrays], *deps)
    return (out[0], out[1], list(out[2:2 + n]), list(out[2 + n:2 + n + m])), out[2 + n + m]


def split_wait(plan, handle, after, name):
    send_sem, recv_sem, srcs, lands = handle
    n, m = len(srcs), len(lands)

    def body(*refs):
        send_sem, recv_sem = refs[n + m], refs[n + m + 1]
        thru = refs[n + m + 2 + len(after):]
        for k, (src, dst, dev) in enumerate(plan(thru[:n], thru[n:])):
            cp = pltpu.make_async_remote_copy(src_ref=src, dst_ref=dst, send_sem=send_sem.at[k],
                                              recv_sem=recv_sem.at[k], device_id=dev, device_id_type=MESH)
            cp.wait_send()
            cp.wait_recv()

    arrays = list(srcs) + list(lands)
    out = pl.pallas_call(
        body, name=name, out_shape=tuple(pltpu.HBM(a.shape, a.dtype) for a in arrays),
        in_specs=[HBM] * (n + m) + [SEM, SEM] + [ANY] * len(after), out_specs=tuple([HBM] * (n + m)),
        input_output_aliases={a: a for a in range(n + m)},
        compiler_params=pltpu.CompilerParams(has_side_effects=EFFECT),
    )(*arrays, send_sem, recv_sem, *after)
    return list(out[:n]), list(out[n:])


def allreduce_small(block, name, dep):
    R, C = block.shape

    def body(in_ref, dep_ref, out_ref, slots, send_sem, recv_sem):
        x, y, c = _place()
        me = 4 * x + 2 * y + c
        slots[me] = in_ref[...]
        copies = []
        for r in range(1, 8):
            fx, fy, fc = (r >> 2) & 1, (r >> 1) & 1, r & 1
            cp = pltpu.make_async_remote_copy(
                src_ref=in_ref, dst_ref=slots.at[me], send_sem=send_sem.at[r - 1], recv_sem=recv_sem.at[r - 1],
                device_id=(_flip(x, fx), _flip(y, fy), _flip(c, fc)), device_id_type=MESH)
            cp.start()
            copies.append(cp)
        for cp in copies:
            cp.wait()
        acc = slots[0]
        for d in range(1, 8):
            acc = acc + slots[d]
        out_ref[...] = acc

    vm = pl.BlockSpec(memory_space=pltpu.VMEM)
    dma = pltpu.SemaphoreType.DMA
    return pl.pallas_call(
        body, name=name, in_specs=[vm, ANY], out_specs=vm, out_shape=jax.ShapeDtypeStruct((R, C), F32),
        scratch_shapes=[pltpu.VMEM((8, R, C), F32), dma((7,)), dma((7,))],
    )(block, dep)


def local_step(x, mem, target, g_mix, g_mem, q_norm_g, k_norm_g, g_mlp, conv_w, h, mem_n, w_in, w_in_dep,
               rest_weights, first_grads, early_grads, mid_grads, late_grads, last_grads):
    S, D = x.shape
    proj = mm_nn_shard(h, w_in, "proj", dep=w_in_dep, out_dtype=BF16)
    a_conv = conv_fwd(proj, conv_w, D, "conv_fwd")
    o_sb = sb_fwd(proj, D, "sb_fwd")
    w_conv_out, w_sb_out, w_mem_kv, w_x_out, w_out, mlp_dep, mlp_weights = rest_weights(o_sb)
    kv = mm_nn_shard(mem_n, w_mem_kv, "kv", dep=mlp_dep)
    o_x = xa_fwd(proj, kv, q_norm_g, k_norm_g, D, "xa_fwd")
    ys, merged, x1, h2 = merge_fwd(proj, (a_conv, o_sb, o_x), (w_conv_out, w_sb_out, w_x_out), w_out, x, g_mlp, D,
                                   "merge_x1")
    w_up, w_down = mlp_weights(h2)
    up, act = mm_nn_shard(h2, w_up, "up", relu2=True)
    dy, dy16, loss_parts = mm_nn_loss(act, w_down, x1, target, "x2_loss")
    loss_cols = jnp.sum(loss_parts, axis=0)
    d_up = mm_nt_relu2(dy16, w_down, up, "d_up")
    g = {"w_down": mm_tn(act, dy16, "g_w_down")}
    g["w_up"] = mm_tn(h2, d_up, "g_w_up", shard_out=True)
    dh2 = mm_nt_shard(d_up, w_up, "dh2")
    dx1, dx1_16, g["g_mlp"] = rms_bwd(x1, g_mlp, dh2, "rms_mlp_bwd", dres=dy, want16=True)
    g["w_out"] = mm_tn(merged, dx1_16, "g_w_out")
    (dy_c, dy_s, dy_x), (d_a_conv, d_o_sb, d_o_x), d_gate = merge_bwd(
        proj, ys, dx1_16, w_out, (w_conv_out, w_sb_out, w_x_out), D, "merge_bwd", dep=first_grads(g))
    g["w_conv_out"] = mm_tn(a_conv, dy_c, "g_w_conv_out")
    g["w_sb_out"] = mm_tn(o_sb, dy_s, "g_w_sb_out")
    g["w_x_out"] = mm_tn(o_x, dy_x, "g_w_x_out")
    d_xq, d_kv, g["q_norm_g"], g["k_norm_g"] = xa_bwd(proj, kv, q_norm_g, k_norm_g, d_o_x, D, "xa_bwd")
    g["w_mem_kv"] = mm_tn(mem_n, d_kv, "g_w_mem_kv", shard_out=True)
    g["g_mem"] = rms_bwd(mem, g_mem, mm_nt_shard(d_kv, w_mem_kv, "d_mem_n"), "rms_mem_bwd", want_dx=False)
    d_ch, d_cb, d_cc, g["conv_w"] = conv_bwd(proj, conv_w, d_a_conv, D, "conv_bwd", dep=early_grads(g))
    dq, dk, dv = sb_bwd(proj, d_o_sb, D, "sb_bwd", dep=mid_grads([d_ch]))
    d_proj = jnp.concatenate([d_ch, d_cb, d_cc, dq, dk, dv, d_xq, d_gate], axis=1)
    g["w_in"] = mm_tn(h, d_proj, "g_w_in", shard_out=True)
    dh = mm_nt_shard(d_proj, w_in, "dh", dep=late_grads(g["w_in"]))
    grad_x, g["g_mix"] = rms_bwd(x, g_mix, dh, "rms_mix_bwd", dres=dx1, dep=last_grads(dh))
    return loss_cols, grad_x, g


BIG = ("w_in", "w_conv_out", "w_sb_out", "w_mem_kv", "w_x_out", "w_out", "w_up", "w_down")
REST = BIG[1:]
MLP_SIDE = ("w_out", "w_up", "w_down")
MIXER_SIDE = tuple(k for k in REST if k not in MLP_SIDE)
COL_SHARDED = ("w_in", "w_mem_kv", "w_up")
WEIGHTS = ("g_mix", "g_mem", "w_in", "conv_w", "w_conv_out", "w_sb_out", "q_norm_g", "k_norm_g",
           "w_mem_kv", "w_x_out", "w_out", "g_mlp", "w_up", "w_down")


def _pack_small(D, g_mix, g_mem, g_mlp, q_norm_g, k_norm_g, conv_w, last):
    qk = jnp.concatenate([q_norm_g, k_norm_g, jnp.zeros((1, D - 2 * X_HEAD_DIM), F32)], axis=1)
    cw = jnp.pad(conv_w, ((0, 0), (0, D - conv_w.shape[1])))
    return jnp.concatenate([g_mix, g_mem, g_mlp, qk, cw, last], axis=0)


def kernel(x, mem, g_mix, g_mem, w_in, conv_w, w_conv_out, w_sb_out, q_norm_g, k_norm_g, w_mem_kv, w_x_out, w_out, g_mlp, w_up, w_down, loss_target, m_g_mix, m_g_mem, m_w_in, m_conv_w, m_w_conv_out, m_w_sb_out, m_q_norm_g, m_k_norm_g, m_w_mem_kv, m_w_x_out, m_w_out, m_g_mlp, m_w_up, m_w_down, v_g_mix, v_g_mem, v_w_in, v_conv_w, v_w_conv_out, v_w_sb_out, v_q_norm_g, v_k_norm_g, v_w_mem_kv, v_w_x_out, v_w_out, v_g_mlp, v_w_up, v_w_down):
    S, D = x.shape[1], x.shape[2]
    w = dict(g_mix=g_mix, g_mem=g_mem, w_in=w_in, conv_w=conv_w, w_conv_out=w_conv_out, w_sb_out=w_sb_out,
             q_norm_g=q_norm_g, k_norm_g=k_norm_g, w_mem_kv=w_mem_kv, w_x_out=w_x_out, w_out=w_out,
             g_mlp=g_mlp, w_up=w_up, w_down=w_down)
    m = dict(g_mix=m_g_mix, g_mem=m_g_mem, w_in=m_w_in, conv_w=m_conv_w, w_conv_out=m_w_conv_out,
             w_sb_out=m_w_sb_out, q_norm_g=m_q_norm_g, k_norm_g=m_k_norm_g, w_mem_kv=m_w_mem_kv,
             w_x_out=m_w_x_out, w_out=m_w_out, g_mlp=m_g_mlp, w_up=m_w_up, w_down=m_w_down)
    v = dict(g_mix=v_g_mix, g_mem=v_g_mem, w_in=v_w_in, conv_w=v_conv_w, w_conv_out=v_w_conv_out,
             w_sb_out=v_w_sb_out, q_norm_g=v_q_norm_g, k_norm_g=v_k_norm_g, w_mem_kv=v_w_mem_kv,
             w_x_out=v_w_x_out, w_out=v_w_out, g_mlp=v_g_mlp, w_up=v_w_up, w_down=v_w_down)
    chip = 2 * lax.axis_index("x") + lax.axis_index("y")
    cs = conv_w.shape[2]

    place = jnp.stack([chip, lax.axis_index("c")]).astype(jnp.int32)
    cw_block = jnp.pad(conv_w[0], ((0, 5), (0, 0)))
    handle, token = split_start(_gather_plan, 3, [cast_into_full(w["w_in"][0], place, "cast_w_in")], [], [],
                                "gather_w_in_start")
    rest16 = [cast_into_full(w[k][0], place, "cast_" + k, dep=token) for k in REST]
    h = rms_fwd(x[0], g_mix, "rms_mix", dep=token)
    mem_n = rms_fwd(mem[0], g_mem, "rms_mem", dep=token)
    landed, _ = split_wait(_gather_plan, handle, [*rest16, h, mem_n], "gather_w_in_wait")
    (w_in_full,), cw_all = gather_forward(landed, cw_block, "gather_w_in_forward")
    conv_full = jnp.concatenate([cw_all[p, :3] for p in range(N_CHIP)], axis=1)
    rest_handle, rest_token = split_start(_gather_plan, 3 * len(REST), rest16, [], [w_in_full], "gather_rest_start")

    def layout(k, a):
        return a if k in COL_SHARDED else a.reshape(-1, a.shape[-1])

    def rest_weights(after):
        landed, _ = split_wait(_gather_plan, rest_handle, [after], "gather_rest_wait")
        first = gather_forward(landed[:-2], cw_block, "gather_rest_forward")[0]
        mlp_handle, token = split_start(_forward_plan, 6, landed[-2:], [], [first[0]], "forward_mlp_start")

        def mlp_weights(after):
            both, _ = split_wait(_forward_plan, mlp_handle, [after], "forward_mlp_wait")
            return [layout(k, a) for k, a in zip(REST[-2:], both)]

        return [layout(k, a) for k, a in zip(REST[:-2], first)] + [token, mlp_weights]

    def blocks(k, a):
        return a if k in COL_SHARDED else a.reshape(N_CHIP, -1, a.shape[-1])

    early = {}

    def swap_start(names, g, dep, name):
        g4 = [blocks(k, g[k]) for k in names]
        early[names], token = split_start(_exchange_plan, len(names), g4, _exchange_lands(g4), [dep], name)
        return token

    def first_grads(g):
        return swap_start(MLP_SIDE, g, g["g_mlp"], "exchange_mlp_start")

    def early_grads(g):
        return swap_start(MIXER_SIDE, g, g["g_mem"], "exchange_mixers_start")

    def mid_grads(after):
        pairs = {}
        for names, name in ((MLP_SIDE, "exchange_mlp_wait"), (MIXER_SIDE, "exchange_mixers_wait")):
            g4, got = split_wait(_exchange_plan, early[names], after, name)
            pairs.update(zip(names, zip(g4, got)))
        p16 = [pair_sum(*pairs[k], place, "pair_sum_" + k) for k in REST]
        early["fly"], token = split_start(_scatter_plan, 3 * len(REST), p16, _scatter_lands(p16), [],
                                          "scatter_rest_start")
        return token

    late = {}

    def late_grads(gw):
        late["swap"], token = split_start(_exchange_plan, 1, [gw], _exchange_lands([gw]), [], "exchange_w_in_start")
        return token

    def last_grads(after):
        (gw,), (got,) = split_wait(_exchange_plan, late["swap"], [after], "exchange_w_in_wait")
        p16 = [pair_sum(gw, got, place, "pair_sum_w_in")]
        late["fly"], token = split_start(_scatter_plan, 3, p16, _scatter_lands(p16), [], "scatter_w_in_start")
        return token

    loss_cols, grad_x, g = local_step(
        x[0], mem[0], loss_target[0], g_mix, g_mem, q_norm_g, k_norm_g, g_mlp, conv_full, h, mem_n,
        w_in_full, rest_token, rest_weights, first_grads, early_grads, mid_grads, late_grads, last_grads)
    token = g["g_mix"]

    p16_rest, got_rest = split_wait(_scatter_plan, early["fly"], [token], "scatter_rest_wait")
    gsum, delta, new_m, new_v = {}, {}, {}, {}
    halves = [chip_sum(p, b, place, "chip_sum_" + k, dep=token) for k, p, b in zip(REST, p16_rest, got_rest)]
    for k, a in zip(REST, join_halves(halves, "join_halves_rest")):
        gsum[k] = a[None]
        delta[k], new_m[k], new_v[k] = adamw(w[k], gsum[k], m[k], v[k], "adamw_" + k)
    p16_in, got_in = split_wait(_scatter_plan, late["fly"], [new_v[k] for k in REST], "scatter_w_in_wait")

    small = allreduce_small(
        _pack_small(D, g["g_mix"], g["g_mem"], g["g_mlp"], g["q_norm_g"], g["k_norm_g"], g["conv_w"], loss_cols),
        "allreduce_small", dep=got_in[0])
    loss = (0.5 / D) * jnp.sum(small[7])
    gsum.update({"g_mix": small[0:1], "g_mem": small[1:2], "g_mlp": small[2:3],
                 "q_norm_g": small[3:4, :X_HEAD_DIM], "k_norm_g": small[3:4, X_HEAD_DIM:2 * X_HEAD_DIM],
                 "conv_w": lax.dynamic_slice(small[4:7], (0, chip * cs), (3, cs))[None]})
    half_in = chip_sum(p16_in[0], got_in[0], place, "chip_sum_w_in")
    gsum["w_in"] = join_halves([half_in], "join_halves_w_in")[0][None]
    delta["w_in"], new_m["w_in"], new_v["w_in"] = adamw(w["w_in"], gsum["w_in"], m["w_in"], v["w_in"], "adamw_w_in")
    small_names = ("g_mix", "g_mem", "g_mlp", "q_norm_g", "k_norm_g", "conv_w")
    zero_row = jnp.zeros((1, D), F32)
    packed = [_pack_small(D, *[t[k] if k != "conv_w" else t[k][0] for k in small_names], zero_row)
              for t in (w, gsum, m, v)]
    sm = adamw(*packed, "adamw_small")
    for t, block in zip((delta, new_m, new_v), sm):
        t["g_mix"], t["g_mem"], t["g_mlp"] = block[0:1], block[1:2], block[2:3]
        t["q_norm_g"], t["k_norm_g"] = block[3:4, :X_HEAD_DIM], block[3:4, X_HEAD_DIM:2 * X_HEAD_DIM]
        t["conv_w"] = block[4:7, :cs][None]

    return (loss, grad_x[None], *[gsum[k] for k in WEIGHTS], *[delta[k] for k in WEIGHTS],
            *[new_m[k] for k in WEIGHTS], *[new_v[k] for k in WEIGHTS])
```

```python
import jax
import jax.numpy as jnp
from jax import lax
from jax.experimental import pallas as pl
from jax.experimental.pallas import tpu as pltpu

F32 = jnp.float32
BF16 = jnp.bfloat16
EPS = 1e-6
N_CHIP = 4
SB_HEAD_DIM = 64
X_HEAD_DIM = 256
LANES = 128
VMEM_LIMIT = 56 * 1024 * 1024
ROW_BLOCK_ELEMS = 1 << 19
ADAM_LR, ADAM_B1, ADAM_B2, ADAM_EPS, ADAM_WD, ADAM_STEP = 0.001, 0.9, 0.999, 1e-8, 0.01, 10
MESH = pl.DeviceIdType.MESH


def _params(*sem):
    return pltpu.CompilerParams(dimension_semantics=sem, vmem_limit_bytes=VMEM_LIMIT)


def _tile(n, pref):
    if n <= pref:
        return n
    t = 1 << (pref.bit_length() - 1)
    while n % t:
        t //= 2
    return t


NN = (((1,), (0,)), ((), ()))
NT = (((1,), (1,)), ((), ()))
TN = (((0,), (0,)), ((), ()))


def _dot(a, b, dims):
    return lax.dot_general(a.astype(BF16), b.astype(BF16), dims, preferred_element_type=F32)


def mm_nn_shard(a, g, name, relu2=False, dep=None, out_dtype=F32):
    M, K = a.shape
    _, _, Ns = g.shape
    tm, tn = _tile(M, 2048), _tile(Ns, 512)
    nb = Ns // tn

    def body(a_ref, b_ref, *o_refs):
        o_refs = o_refs[len(deps):]
        acc = _dot(a_ref[...], b_ref[...], NN)
        if relu2:
            acc = jnp.maximum(acc, 0.0)
            o_refs[1][...] = (acc * acc).astype(BF16)
        o_refs[0][...] = acc.astype(o_refs[0].dtype)

    o_spec = pl.BlockSpec((tm, tn), lambda i, j: (i, j))
    shapes = [jax.ShapeDtypeStruct((M, N_CHIP * Ns), BF16 if relu2 else out_dtype)]
    specs = [o_spec]
    if relu2:
        shapes.append(jax.ShapeDtypeStruct((M, N_CHIP * Ns), BF16))
        specs.append(o_spec)
    deps = [] if dep is None else [dep]
    out = pl.pallas_call(
        body, grid=(M // tm, N_CHIP * nb), name=name,
        in_specs=[pl.BlockSpec((tm, K), lambda i, j: (i, 0)),
                  pl.BlockSpec((None, K, tn), lambda i, j: (j // nb, 0, j % nb))] + [ANY] * len(deps),
        out_specs=specs, out_shape=shapes, compiler_params=_params("parallel", "parallel"),
    )(a, g, *deps)
    return out if relu2 else out[0]


def proj_own(a, w_own, place, name, dep):
    M, K = a.shape
    Ns = w_own.shape[1]
    tm, tn = _tile(M, 2048), _tile(Ns, 512)
    nb = Ns // tn

    def body(place_ref, a_ref, b_ref, dep_ref, o_ref):
        o_ref[...] = _dot(a_ref[...], b_ref[...], NN).astype(BF16)

    return pl.pallas_call(
        body, name=name, out_shape=jax.ShapeDtypeStruct((M, N_CHIP * Ns), BF16),
        grid_spec=pltpu.PrefetchScalarGridSpec(
            num_scalar_prefetch=1, grid=(M // tm, nb),
            in_specs=[pl.BlockSpec((tm, K), lambda i, j, p: (i, 0)), pl.BlockSpec((K, tn), lambda i, j, p: (0, j)), ANY],
            out_specs=pl.BlockSpec((tm, tn), lambda i, j, p: (i, p[0] * nb + j))),
        compiler_params=_params("parallel", "parallel"),
    )(place, a, w_own, dep)


def proj_rest(a, g, part, place, name, dep):
    M, K = a.shape
    _, _, Ns = g.shape
    tm, tn = _tile(M, 2048), _tile(Ns, 512)
    nb = Ns // tn
    chip = lambda j, p: (p[0] + 1 + j // nb) % N_CHIP

    def body(place_ref, a_ref, b_ref, part_ref, dep_ref, o_ref):
        o_ref[...] = _dot(a_ref[...], b_ref[...], NN).astype(BF16)

    return pl.pallas_call(
        body, name=name, out_shape=jax.ShapeDtypeStruct(part.shape, BF16),
        grid_spec=pltpu.PrefetchScalarGridSpec(
            num_scalar_prefetch=1, grid=(M // tm, (N_CHIP - 1) * nb),
            in_specs=[pl.BlockSpec((tm, K), lambda i, j, p: (i, 0)),
                      pl.BlockSpec((None, K, tn), lambda i, j, p: (chip(j, p), 0, j % nb)), ANY, ANY],
            out_specs=pl.BlockSpec((tm, tn), lambda i, j, p: (i, chip(j, p) * nb + j % nb))),
        input_output_aliases={3: 0},
        compiler_params=_params("parallel", "parallel"),
    )(place, a, g, part, dep)


def mm_nn_loss(a, w, res, target, name):
    M, K = a.shape
    N = w.shape[1]
    tm, tn = _tile(M, 1024), _tile(N, 512)

    def body(a_ref, b_ref, r_ref, t_ref, dy_ref, dy16_ref, l_ref):
        e = (r_ref[...] + _dot(a_ref[...], b_ref[...], NN)) - t_ref[...]
        dy = e * (1.0 / N)
        dy_ref[...] = dy
        dy16_ref[...] = dy.astype(BF16)
        l_ref[...] = jnp.sum(e * e, axis=0, keepdims=True)

    o_spec = pl.BlockSpec((tm, tn), lambda i, j: (i, j))
    return pl.pallas_call(
        body, grid=(M // tm, N // tn), name=name,
        in_specs=[pl.BlockSpec((tm, K), lambda i, j: (i, 0)), pl.BlockSpec((K, tn), lambda i, j: (0, j)),
                  o_spec, o_spec],
        out_specs=[o_spec, o_spec, pl.BlockSpec((None, 1, tn), lambda i, j: (i, 0, j))],
        out_shape=[jax.ShapeDtypeStruct((M, N), F32), jax.ShapeDtypeStruct((M, N), BF16),
                   jax.ShapeDtypeStruct((M // tm, 1, N), F32)],
        compiler_params=_params("parallel", "parallel"),
    )(a, w, res, target)


def mm_nt_relu2(a, w, up, name):
    M, N = a.shape
    R = w.shape[0]
    tm, tr = _tile(M, 2048), _tile(R, 512)

    def body(a_ref, b_ref, up_ref, o_ref):
        acc = _dot(a_ref[...], b_ref[...], NT)
        o_ref[...] = (acc * (2.0 * jnp.maximum(up_ref[...].astype(F32), 0.0))).astype(BF16)

    o_spec = pl.BlockSpec((tm, tr), lambda i, j: (i, j))
    return pl.pallas_call(
        body, grid=(M // tm, R // tr), name=name,
        in_specs=[pl.BlockSpec((tm, N), lambda i, j: (i, 0)), pl.BlockSpec((tr, N), lambda i, j: (j, 0)), o_spec],
        out_specs=o_spec, out_shape=jax.ShapeDtypeStruct((M, R), BF16),
        compiler_params=_params("parallel", "parallel"),
    )(a, w, up)


def mm_nt_shard(a, g, name, out_dtype=F32, dep=None):
    deps = [] if dep is None else [dep]
    M = a.shape[0]
    _, R, Ns = g.shape
    tm, tr, tk = _tile(M, 1024), _tile(R, 1024), _tile(Ns, 2560)
    nb = Ns // tk
    nk = N_CHIP * nb

    def body(a_ref, b_ref, *refs):
        o_ref, acc_ref = refs[len(deps):]
        k = pl.program_id(2)

        @pl.when(k == 0)
        def _():
            acc_ref[...] = jnp.zeros_like(acc_ref)

        acc_ref[...] += _dot(a_ref[...], b_ref[...], NT)

        @pl.when(k == nk - 1)
        def _():
            o_ref[...] = acc_ref[...].astype(out_dtype)

    return pl.pallas_call(
        body, grid=(M // tm, R // tr, nk), name=name,
        in_specs=[pl.BlockSpec((tm, tk), lambda i, j, k: (i, k)),
                  pl.BlockSpec((None, tr, tk), lambda i, j, k: (k // nb, j, k % nb))] + [ANY] * len(deps),
        out_specs=pl.BlockSpec((tm, tr), lambda i, j, k: (i, j)),
        out_shape=jax.ShapeDtypeStruct((M, R), out_dtype),
        scratch_shapes=[pltpu.VMEM((tm, tr), F32)],
        compiler_params=_params("parallel", "parallel", "arbitrary"),
    )(a, g, *deps)


def mm_tn(a, b, name, shard_out=False):
    S, M = a.shape
    N = b.shape[1]
    Ns = N // N_CHIP if shard_out else N
    tm, tn = _tile(M, 1024), _tile(Ns, 512)
    nb = Ns // tn

    def body(a_ref, b_ref, o_ref):
        o_ref[...] = _dot(a_ref[...], b_ref[...], TN)

    if shard_out:
        o_spec = pl.BlockSpec((None, tm, tn), lambda i, j: (j // nb, i, j % nb))
        o_shape = jax.ShapeDtypeStruct((N_CHIP, M, Ns), F32)
    else:
        o_spec = pl.BlockSpec((tm, tn), lambda i, j: (i, j))
        o_shape = jax.ShapeDtypeStruct((M, N), F32)
    return pl.pallas_call(
        body, grid=(M // tm, N // tn), name=name,
        in_specs=[pl.BlockSpec((S, tm), lambda i, j: (0, i)), pl.BlockSpec((S, tn), lambda i, j: (0, j))],
        out_specs=o_spec, out_shape=o_shape, compiler_params=_params("parallel", "parallel"),
    )(a, b)


def rms_fwd(x, g, name, dep=None):
    S, D = x.shape
    tm = _tile(S, 512)
    deps = [] if dep is None else [dep]

    def body(x_ref, g_ref, *refs):
        xv = x_ref[...]
        r = lax.rsqrt(jnp.mean(xv * xv, axis=-1, keepdims=True) + EPS)
        refs[-1][...] = ((xv * r) * g_ref[...]).astype(BF16)

    return pl.pallas_call(
        body, grid=(S // tm,), name=name,
        in_specs=[pl.BlockSpec((tm, D), lambda i: (i, 0)), pl.BlockSpec((1, D), lambda i: (0, 0))]
        + [ANY] * len(deps),
        out_specs=pl.BlockSpec((tm, D), lambda i: (i, 0)),
        out_shape=jax.ShapeDtypeStruct((S, D), BF16), compiler_params=_params("parallel"),
    )(x, g, *deps)


def rms_bwd(x, g, dh, name, dres=None, want_dx=True, want16=False, dep=None):
    S, D = x.shape
    tm = _tile(S, 512)

    def body(x_ref, g_ref, dh_ref, *refs):
        i = pl.program_id(0)
        xv = x_ref[...]
        r = lax.rsqrt(jnp.mean(xv * xv, axis=-1, keepdims=True) + EPS)
        xn = xv * r
        dhv = dh_ref[...].astype(F32)
        gg_ref = refs[-1]

        @pl.when(i == 0)
        def _():
            gg_ref[...] = jnp.zeros_like(gg_ref)

        gg_ref[...] += jnp.sum(dhv * xn, axis=0, keepdims=True)
        if want_dx:
            dxn = dhv * g_ref[...]
            dx = r * (dxn - xn * jnp.mean(dxn * xn, axis=-1, keepdims=True))
            if dres is not None:
                dx = refs[0][...] + dx
            refs[-2][...] = dx.astype(refs[-2].dtype)
            if want16:
                refs[-3][...] = dx

    row = pl.BlockSpec((tm, D), lambda i: (i, 0))
    vec = pl.BlockSpec((1, D), lambda i: (0, 0))
    ins, in_specs = [x, g, dh], [row, vec, row]
    if dres is not None:
        ins.append(dres)
        in_specs.append(row)
    if dep is not None:
        ins.append(dep)
        in_specs.append(ANY)
    shapes, specs = [jax.ShapeDtypeStruct((1, D), F32)], [vec]
    if want_dx:
        if want16:
            shapes.insert(0, jax.ShapeDtypeStruct((S, D), BF16))
            specs.insert(0, row)
        shapes.insert(0, jax.ShapeDtypeStruct((S, D), F32))
        specs.insert(0, row)
    out = pl.pallas_call(body, grid=(S // tm,), name=name, in_specs=in_specs, out_specs=specs,
                         out_shape=shapes, compiler_params=_params("arbitrary"))(*ins)
    return out if want_dx else out[0]


def _shift_down(u, k, row):
    return jnp.where(row >= k, pltpu.roll(u, k, axis=0), 0.0)


def _shift_up(u, k, row):
    S = u.shape[0]
    return jnp.where(row < S - k, pltpu.roll(u, S - k, axis=0), 0.0)


def _conv_specs(S, D, tc):
    nb = D // tc
    col = lambda o: pl.BlockSpec((S, tc), lambda j, o=o: (0, o * nb + j))
    return col, pl.BlockSpec((3, tc), lambda j: (0, j))


def conv_fwd(proj, conv_w, D, name):
    S = proj.shape[0]
    tc = _tile(D, 256)
    col, wspec = _conv_specs(S, D, tc)

    def body(ch_ref, cb_ref, cc_ref, w_ref, a_ref):
        row = lax.broadcasted_iota(jnp.int32, (S, tc), 0)
        u = cc_ref[...].astype(F32) * ch_ref[...].astype(F32)
        w = w_ref[...]
        cv = w[0:1, :] * _shift_down(u, 2, row) + w[1:2, :] * _shift_down(u, 1, row) + w[2:3, :] * u
        a_ref[...] = (cb_ref[...].astype(F32) * cv).astype(BF16)

    return pl.pallas_call(
        body, grid=(D // tc,), name=name, in_specs=[col(0), col(1), col(2), wspec],
        out_specs=pl.BlockSpec((S, tc), lambda j: (0, j)),
        out_shape=jax.ShapeDtypeStruct((S, D), BF16), compiler_params=_params("parallel"),
    )(proj, proj, proj, conv_w)


def conv_bwd(proj, conv_w, da, D, name, dep):
    S = proj.shape[0]
    tc = _tile(D, 256)
    col, wspec = _conv_specs(S, D, tc)
    blk = pl.BlockSpec((S, tc), lambda j: (0, j))

    def body(ch_ref, cb_ref, cc_ref, w_ref, da_ref, dep_ref, dch_ref, dcb_ref, dcc_ref, gw_ref):
        row = lax.broadcasted_iota(jnp.int32, (S, tc), 0)
        ch, cb, cc, dav = [r[...].astype(F32) for r in (ch_ref, cb_ref, cc_ref, da_ref)]
        w = w_ref[...]
        u = cc * ch
        u1, u2 = _shift_down(u, 1, row), _shift_down(u, 2, row)
        cv = w[0:1, :] * u2 + w[1:2, :] * u1 + w[2:3, :] * u
        dcb_ref[...] = (dav * cv).astype(BF16)
        dcv = dav * cb
        gw_ref[0:1, :] = jnp.sum(dcv * u2, axis=0, keepdims=True)
        gw_ref[1:2, :] = jnp.sum(dcv * u1, axis=0, keepdims=True)
        gw_ref[2:3, :] = jnp.sum(dcv * u, axis=0, keepdims=True)
        du = w[2:3, :] * dcv + w[1:2, :] * _shift_up(dcv, 1, row) + w[0:1, :] * _shift_up(dcv, 2, row)
        dcc_ref[...] = (du * ch).astype(BF16)
        dch_ref[...] = (du * cc).astype(BF16)

    act = jax.ShapeDtypeStruct((S, D), BF16)
    return pl.pallas_call(
        body, grid=(D // tc,), name=name, in_specs=[col(0), col(1), col(2), wspec, blk, ANY],
        out_specs=[blk, blk, blk, wspec], out_shape=[act, act, act, jax.ShapeDtypeStruct((3, D), F32)],
        compiler_params=_params("parallel"),
    )(proj, proj, proj, conv_w, da, dep)


SB_BQ_FWD = 512
SB_BQ_BWD = 256
SB_BK = 128
SB_GROUP = 4


def _sb_consts(bq):
    lane = lax.broadcasted_iota(jnp.int32, (bq, LANES), 1)
    r = lax.broadcasted_iota(jnp.int32, (SB_BK, SB_BK), 0)
    c = lax.broadcasted_iota(jnp.int32, (SB_BK, SB_BK), 1)
    tri_rev = jnp.where(r > c, 1.0, 0.0).astype(BF16)
    tri_fwd = jnp.where(r < c, 1.0, 0.0).astype(BF16)
    return lane, tri_rev, tri_fwd


def _cumsum2(v, tri):
    hi = v.astype(BF16)
    lo = (v - hi.astype(F32)).astype(BF16)
    part = (lax.dot_general(hi, tri, NN, preferred_element_type=F32)
            + lax.dot_general(lo, tri, NN, preferred_element_type=F32))
    return part, jnp.sum(v, axis=1, keepdims=True)


def _sb_logits(z, past):
    sp = jnp.log(1.0 + jnp.exp(-jnp.abs(z)))
    l = jnp.minimum(z, 0.0) - sp
    m = l - z
    if past is not None:
        m = jnp.where(past, m, 0.0)
    return l, m


def _stack_heads(v, lane):
    return jnp.concatenate([jnp.where(lane < SB_HEAD_DIM, v, 0.0), jnp.where(lane >= SB_HEAD_DIM, v, 0.0)],
                           axis=0).astype(BF16)


def _unstack_heads(v, lane):
    bq = v.shape[0] // 2
    return jnp.where(lane < SB_HEAD_DIM, v[:bq], v[bq:])


def _sb_positions(i, bq):
    r = lax.broadcasted_iota(jnp.int32, (2 * bq, SB_BK), 0)
    trow = i * bq + jnp.where(r >= bq, r - bq, r)
    return trow, lax.broadcasted_iota(jnp.int32, (2 * bq, SB_BK), 1)


def _sb_specs(S, D, bq):
    npair = D // LANES
    qspec = pl.BlockSpec((bq, LANES), lambda p, i: (i, 3 * npair + p))
    kspec = pl.BlockSpec((S, LANES), lambda p, i: (0, 4 * npair + p))
    vspec = pl.BlockSpec((S, LANES), lambda p, i: (0, 5 * npair + p))
    return npair, qspec, kspec, vspec


def sb_fwd(proj, D, name):
    S = proj.shape[0]
    bq = min(SB_BQ_FWD, S)
    nd = bq // SB_BK
    npair, qspec, kspec, vspec = _sb_specs(S, D, bq)
    scale = SB_HEAD_DIM ** -0.5

    def body(q_ref, k_ref, v_ref, o_ref, kb_ref, vb_ref):
        i = pl.program_id(1)

        @pl.when(i == 0)
        def _():
            kb_ref[...] = k_ref[...].astype(BF16)
            vb_ref[...] = v_ref[...].astype(BF16)

        lane, tri_rev, _ = _sb_consts(bq)
        qs = _stack_heads(q_ref[...].astype(F32) * scale, lane)
        trow, scol = _sb_positions(i, bq)

        def steps(j0, carry, n, masked):
            ks = [pl.multiple_of((j0 - t) * SB_BK, SB_BK) for t in range(n)]
            past = [(k + scol) < trow if masked else None for k in ks]
            zs = [lax.dot_general(qs, kb_ref[pl.ds(k, SB_BK), :], NT, preferred_element_type=F32) for k in ks]
            lm = [_sb_logits(z, p) for z, p in zip(zs, past)]
            cs = [_cumsum2(m, tri_rev) for _, m in lm]
            c, acc = carry
            for t in range(n):
                a = jnp.exp(lm[t][0] + (cs[t][0] + c))
                if masked:
                    a = jnp.where(past[t], a, 0.0)
                acc = acc + lax.dot_general(a.astype(BF16), vb_ref[pl.ds(ks[t], SB_BK), :], NN,
                                            preferred_element_type=F32)
                c = c + cs[t][1]
            return c, acc

        carry = (jnp.zeros((2 * bq, 1), F32), jnp.zeros((2 * bq, LANES), F32))
        carry = steps(i * nd + nd - 1, carry, nd, True)
        older = i * nd
        groups = older // SB_GROUP
        carry = lax.fori_loop(
            0, groups, lambda t, cr: steps(older - 1 - t * SB_GROUP, cr, SB_GROUP, False), carry)
        rest = older - groups * SB_GROUP
        carry = lax.fori_loop(0, rest // nd, lambda t, cr: steps(rest - 1 - t * nd, cr, nd, False), carry)
        o_ref[...] = _unstack_heads(carry[1], lane).astype(BF16)

    return pl.pallas_call(
        body, grid=(npair, S // bq), name=name, in_specs=[qspec, kspec, vspec],
        out_specs=pl.BlockSpec((bq, LANES), lambda p, i: (i, p)),
        out_shape=jax.ShapeDtypeStruct((S, D), BF16),
        scratch_shapes=[pltpu.VMEM((S, LANES), BF16), pltpu.VMEM((S, LANES), BF16)],
        compiler_params=_params("parallel", "arbitrary"),
    )(proj, proj, proj)


def sb_bwd(proj, do, D, name, dep=None):
    S = proj.shape[0]
    bq = min(SB_BQ_BWD, S)
    nd = bq // SB_BK
    nkb = S // SB_BK
    npair, qspec, kspec, vspec = _sb_specs(S, D, bq)
    scale = SB_HEAD_DIM ** -0.5

    def body(q_ref, k_ref, v_ref, do_ref, *refs):
        dq_ref, dk_ref, dv_ref, kb_ref, vb_ref, dk_acc, dv_acc, g_scr, b_scr, a_scr = refs[len(deps):]
        i = pl.program_id(1)

        @pl.when(i == 0)
        def _():
            kb_ref[...] = k_ref[...].astype(BF16)
            vb_ref[...] = v_ref[...].astype(BF16)
            dk_acc[...] = jnp.zeros_like(dk_acc)
            dv_acc[...] = jnp.zeros_like(dv_acc)

        lane, tri_rev, tri_fwd = _sb_consts(bq)
        qs = _stack_heads(q_ref[...].astype(F32) * scale, lane)
        dos = _stack_heads(do_ref[...].astype(F32), lane)
        qs_t, dos_t = qs.T, dos.T
        trow, scol = _sb_positions(i, bq)

        def sweep1(j0, c, n, masked):
            js = [j0 - t for t in range(n)]
            ks = [pl.multiple_of(j * SB_BK, SB_BK) for j in js]
            past = [(k + scol) < trow if masked else None for k in ks]
            zs = [lax.dot_general(qs, kb_ref[pl.ds(k, SB_BK), :], NT, preferred_element_type=F32) for k in ks]
            das = [lax.dot_general(dos, vb_ref[pl.ds(k, SB_BK), :], NT, preferred_element_type=F32) for k in ks]
            lm = [_sb_logits(z, p) for z, p in zip(zs, past)]
            cs = [_cumsum2(m, tri_rev) for _, m in lm]
            for t in range(n):
                b_scr[js[t]] = jnp.exp(lm[t][0]).astype(BF16)
            for t in range(n):
                a = jnp.exp(lm[t][0] + (cs[t][0] + c))
                if masked:
                    a = jnp.where(past[t], a, 0.0)
                g_scr[js[t]] = (das[t] * a).astype(BF16)
                a_scr[js[t]] = a.astype(BF16)
                c = c + cs[t][1]
            return c

        older = i * nd
        groups = older // SB_GROUP
        rest = older - groups * SB_GROUP
        c = jnp.zeros((2 * bq, 1), F32)
        c = sweep1(i * nd + nd - 1, c, nd, True)
        c = lax.fori_loop(0, groups, lambda t, cr: sweep1(older - 1 - t * SB_GROUP, cr, SB_GROUP, False), c)
        lax.fori_loop(0, rest // nd, lambda t, cr: sweep1(rest - 1 - t * nd, cr, nd, False), c)

        def sweep2(j0, carry, n, masked):
            js = [j0 + t for t in range(n)]
            ks = [pl.multiple_of(j * SB_BK, SB_BK) for j in js]
            g16 = [g_scr[j] for j in js]
            gv = [g.astype(F32) for g in g16]
            gs = [(lax.dot_general(g, tri_fwd, NN, preferred_element_type=F32), jnp.sum(v, axis=1, keepdims=True))
                  for g, v in zip(g16, gv)]
            pc, dq = carry
            dzs = []
            for t in range(n):
                dz = gv[t] - b_scr[js[t]].astype(F32) * (gv[t] + (gs[t][0] + pc))
                if masked:
                    dz = jnp.where((ks[t] + scol) < trow, dz, 0.0)
                dzs.append(dz.astype(BF16))
                pc = pc + gs[t][1]
            for t in range(n):
                dq = dq + lax.dot_general(dzs[t], kb_ref[pl.ds(ks[t], SB_BK), :], NN, preferred_element_type=F32)
                dk_acc[js[t]] += lax.dot_general(qs_t, dzs[t], NN, preferred_element_type=F32)
                dv_acc[js[t]] += lax.dot_general(dos_t, a_scr[js[t]], NN, preferred_element_type=F32)
            return pc, dq

        carry = (jnp.zeros((2 * bq, 1), F32), jnp.zeros((2 * bq, LANES), F32))
        carry = lax.fori_loop(0, groups, lambda t, cr: sweep2(t * SB_GROUP, cr, SB_GROUP, False), carry)
        carry = lax.fori_loop(
            0, rest // nd, lambda t, cr: sweep2(groups * SB_GROUP + t * nd, cr, nd, False), carry)
        carry = sweep2(i * nd, carry, nd, True)
        dq_ref[...] = (_unstack_heads(carry[1], lane) * scale).astype(BF16)

        @pl.when(i == pl.num_programs(1) - 1)
        def _():
            for j in range(nkb):
                dk_ref[j * SB_BK:(j + 1) * SB_BK, :] = dk_acc[j].T.astype(BF16)
                dv_ref[j * SB_BK:(j + 1) * SB_BK, :] = dv_acc[j].T.astype(BF16)

    deps = [] if dep is None else [dep]
    full = pl.BlockSpec((S, LANES), lambda p, i: (0, p))
    blk = pl.BlockSpec((bq, LANES), lambda p, i: (i, p))
    act = jax.ShapeDtypeStruct((S, D), BF16)
    return pl.pallas_call(
        body, grid=(npair, S // bq), name=name, in_specs=[qspec, kspec, vspec, blk] + [ANY] * len(deps),
        out_specs=[blk, full, full], out_shape=[act, act, act],
        scratch_shapes=[pltpu.VMEM((S, LANES), BF16), pltpu.VMEM((S, LANES), BF16),
                        pltpu.VMEM((nkb, LANES, SB_BK), F32), pltpu.VMEM((nkb, LANES, SB_BK), F32),
                        pltpu.VMEM((nkb, 2 * bq, SB_BK), BF16), pltpu.VMEM((nkb, 2 * bq, SB_BK), BF16),
                        pltpu.VMEM((nkb, 2 * bq, SB_BK), BF16)],
        compiler_params=_params("parallel", "arbitrary"),
    )(proj, proj, proj, do, *deps)


def _rms_rows(v):
    r = lax.rsqrt(jnp.mean(v * v, axis=-1, keepdims=True) + EPS)
    return v * r, r


def _xa_specs(S, D, M, tq):
    nh = D // X_HEAD_DIM
    qspec = pl.BlockSpec((tq, X_HEAD_DIM), lambda h, i: (i, 6 * nh + h))
    kspec = pl.BlockSpec((M, X_HEAD_DIM), lambda h, i: (0, h))
    vspec = pl.BlockSpec((M, X_HEAD_DIM), lambda h, i: (0, nh + h))
    gspec = pl.BlockSpec((1, X_HEAD_DIM), lambda h, i: (0, 0))
    return nh, qspec, kspec, vspec, gspec


def xa_fwd(proj, kv, gq, gk, D, name):
    S, M = proj.shape[0], kv.shape[0]
    tq = _tile(S, 2048)
    nh, qspec, kspec, vspec, gspec = _xa_specs(S, D, M, tq)
    scale = X_HEAD_DIM ** -0.5

    def body(q_ref, k_ref, v_ref, gq_ref, gk_ref, o_ref):
        qn = _rms_rows(q_ref[...].astype(F32))[0] * gq_ref[...]
        kn = _rms_rows(k_ref[...])[0] * gk_ref[...]
        s = _dot(qn, kn, NT) * scale
        e = jnp.exp(s - jnp.max(s, axis=-1, keepdims=True))
        p = e / jnp.sum(e, axis=-1, keepdims=True)
        o_ref[...] = _dot(p, v_ref[...], NN).astype(BF16)

    return pl.pallas_call(
        body, grid=(nh, S // tq), name=name, in_specs=[qspec, kspec, vspec, gspec, gspec],
        out_specs=pl.BlockSpec((tq, X_HEAD_DIM), lambda h, i: (i, h)),
        out_shape=jax.ShapeDtypeStruct((S, D), BF16), compiler_params=_params("parallel", "parallel"),
    )(proj, kv, kv, gq, gk)


def xa_bwd(proj, kv, gq, gk, do, D, name):
    S, M = proj.shape[0], kv.shape[0]
    tq = _tile(S, 2048)
    nh, qspec, kspec, vspec, gspec = _xa_specs(S, D, M, tq)
    scale = X_HEAD_DIM ** -0.5

    def body(q_ref, k_ref, v_ref, gq_ref, gk_ref, do_ref, dq_ref, dk_ref, dv_ref, ggq_ref, ggk_ref,
             dkn_acc, dv_acc):
        h, i = pl.program_id(0), pl.program_id(1)

        @pl.when((h == 0) & (i == 0))
        def _():
            ggq_ref[...] = jnp.zeros_like(ggq_ref)
            ggk_ref[...] = jnp.zeros_like(ggk_ref)

        @pl.when(i == 0)
        def _():
            dkn_acc[...] = jnp.zeros_like(dkn_acc)
            dv_acc[...] = jnp.zeros_like(dv_acc)

        gq, gk = gq_ref[...], gk_ref[...]
        qhat, rq = _rms_rows(q_ref[...].astype(F32))
        khat, rk = _rms_rows(k_ref[...])
        qn, kn = qhat * gq, khat * gk
        s = _dot(qn, kn, NT) * scale
        e = jnp.exp(s - jnp.max(s, axis=-1, keepdims=True))
        p = e / jnp.sum(e, axis=-1, keepdims=True)
        dov = do_ref[...]
        dv_acc[...] += _dot(p, dov, TN)
        dp = _dot(dov, v_ref[...], NT)
        ds = (p * (dp - jnp.sum(dp * p, axis=-1, keepdims=True))) * scale
        dqn = _dot(ds, kn, NN)
        dkn_acc[...] += _dot(ds, qn, TN)
        ggq_ref[...] += jnp.sum(dqn * qhat, axis=0, keepdims=True)
        dqh = dqn * gq
        dq_ref[...] = (rq * (dqh - qhat * jnp.mean(dqh * qhat, axis=-1, keepdims=True))).astype(BF16)

        @pl.when(i == pl.num_programs(1) - 1)
        def _():
            dkn = dkn_acc[...]
            ggk_ref[...] += jnp.sum(dkn * khat, axis=0, keepdims=True)
            dkh = dkn * gk
            dk_ref[...] = (rk * (dkh - khat * jnp.mean(dkh * khat, axis=-1, keepdims=True))).astype(BF16)
            dv_ref[...] = dv_acc[...].astype(BF16)

    blk = pl.BlockSpec((tq, X_HEAD_DIM), lambda h, i: (i, h))
    kv_shape = jax.ShapeDtypeStruct((M, 2 * D), BF16)
    gshape = jax.ShapeDtypeStruct((1, X_HEAD_DIM), F32)
    dq, dk, dv, ggq, ggk = pl.pallas_call(
        body, grid=(nh, S // tq), name=name, in_specs=[qspec, kspec, vspec, gspec, gspec, blk],
        out_specs=[blk, kspec, vspec, gspec, gspec],
        out_shape=[jax.ShapeDtypeStruct((S, D), BF16), kv_shape, kv_shape, gshape, gshape],
        scratch_shapes=[pltpu.VMEM((M, X_HEAD_DIM), F32), pltpu.VMEM((M, X_HEAD_DIM), F32)],
        compiler_params=_params("arbitrary", "arbitrary"),
    )(proj, kv, kv, gq, gk, do)
    d_kv = jnp.concatenate([dk[:, :D], dv[:, D:]], axis=1)
    return dq, d_kv, ggq, ggk


def _gate_specs(S, D, tm):
    row = pl.BlockSpec((tm, D), lambda i: (i, 0))
    gate = lambda b: pl.BlockSpec((tm, D), lambda i, b=b: (i, 7 + b))
    return row, gate


def merge_fwd(proj, branches, w_branches, w_out, x, g_next, D, name):
    S = proj.shape[0]
    tm = _tile(S, 256)
    row, gate = _gate_specs(S, D, tm)
    mat = pl.BlockSpec((D, D), lambda i: (0, 0))

    def body(g0, g1, g2, b0, b1, b2, w0, w1, w2, w_ref, x_ref, gn_ref, y0, y1, y2, m_ref, o_ref, h_ref):
        acc = None
        for g_ref, b_ref, wb_ref, y_ref in ((g0, b0, w0, y0), (g1, b1, w1, y1), (g2, b2, w2, y2)):
            y = _dot(b_ref[...], wb_ref[...], NN).astype(BF16)
            y_ref[...] = y
            term = jax.nn.sigmoid(g_ref[...].astype(F32)) * y.astype(F32)
            acc = term if acc is None else acc + term
        merged = acc.astype(BF16)
        m_ref[...] = merged
        x1 = x_ref[...] + _dot(merged, w_ref[...], NN)
        o_ref[...] = x1
        h_ref[...] = (_rms_rows(x1)[0] * gn_ref[...]).astype(BF16)

    act = jax.ShapeDtypeStruct((S, D), BF16)
    out = pl.pallas_call(
        body, grid=(S // tm,), name=name,
        in_specs=[gate(0), gate(1), gate(2), row, row, row, mat, mat, mat, mat, row,
                  pl.BlockSpec((1, D), lambda i: (0, 0))],
        out_specs=[row] * 6, out_shape=[act, act, act, act, jax.ShapeDtypeStruct((S, D), F32), act],
        compiler_params=_params("parallel"),
    )(proj, proj, proj, *branches, *w_branches, w_out, x, g_next)
    return out[:3], out[3], out[4], out[5]


def merge_bwd(proj, ys, dx1, w_out, w_branches, D, name, dep):
    S = proj.shape[0]
    tm = _tile(S, 256)
    row, gate = _gate_specs(S, D, tm)
    mat = pl.BlockSpec((D, D), lambda i: (0, 0))

    def body(g0, g1, g2, y0, y1, y2, dx_ref, w_ref, w0, w1, w2, dep_ref, d0, d1, d2, p0, p1, p2, dg_ref):
        dmv = _dot(dx_ref[...], w_ref[...], NT)
        for b, (g_ref, y_ref, wb_ref, d_ref, p_ref) in enumerate(
                ((g0, y0, w0, d0, p0), (g1, y1, w1, d1, p1), (g2, y2, w2, d2, p2))):
            s = jax.nn.sigmoid(g_ref[...].astype(F32))
            dy = (dmv * s).astype(BF16)
            d_ref[...] = dy
            p_ref[...] = _dot(dy, wb_ref[...], NT).astype(BF16)
            dg_ref[:, b * D:(b + 1) * D] = ((dmv * y_ref[...].astype(F32)) * (s * (1.0 - s))).astype(BF16)

    act = jax.ShapeDtypeStruct((S, D), BF16)
    out = pl.pallas_call(
        body, grid=(S // tm,), name=name,
        in_specs=[gate(0), gate(1), gate(2), row, row, row, row, mat, mat, mat, mat, ANY],
        out_specs=[row] * 6 + [pl.BlockSpec((tm, 3 * D), lambda i: (i, 0))],
        out_shape=[act] * 6 + [jax.ShapeDtypeStruct((S, 3 * D), BF16)], compiler_params=_params("parallel"),
    )(proj, proj, proj, *ys, dx1, w_out, *w_branches, dep)
    return out[:3], out[3:6], out[6]


def _rows2d(a):
    return a.reshape(-1, a.shape[-1])


def _ew_call(fn, ins, out_dtypes, name):
    R, C = ins[0].shape
    tr = _tile(R, max(8, ROW_BLOCK_ELEMS // C))
    spec = pl.BlockSpec((tr, C), lambda i: (i, 0))

    def body(*refs):
        outs = fn(*[r[...] for r in refs[:len(ins)]])
        for o_ref, o in zip(refs[len(ins):], outs):
            o_ref[...] = o.astype(o_ref.dtype)

    return pl.pallas_call(
        body, grid=(R // tr,), name=name, in_specs=[spec] * len(ins), out_specs=[spec] * len(out_dtypes),
        out_shape=[jax.ShapeDtypeStruct((R, C), d) for d in out_dtypes], compiler_params=_params("parallel"),
    )(*ins)


def adamw(w, g, m, v, name):
    def fn(w, g, m, v):
        m = ADAM_B1 * m + (1.0 - ADAM_B1) * g
        v = ADAM_B2 * v + (1.0 - ADAM_B2) * (g * g)
        m_hat = m / (1.0 - ADAM_B1 ** ADAM_STEP)
        v_hat = v / (1.0 - ADAM_B2 ** ADAM_STEP)
        return -ADAM_LR * (m_hat / (jnp.sqrt(v_hat) + ADAM_EPS) + ADAM_WD * w), m, v

    shp = w.shape
    outs = _ew_call(fn, [_rows2d(a) for a in (w, g, m, v)], [F32, F32, F32], name)
    return [o.reshape(shp) for o in outs]


def _placed_call(fn, place, grid, ins, in_specs, out_shape, out_specs, name, dep=None):
    n = len(ins)
    deps = [] if dep is None else [dep]

    def body(place_ref, *refs):
        outs = fn(*[r[...] for r in refs[:n]])
        for o_ref, o in zip(refs[n + len(deps):], outs):
            o_ref[...] = o.astype(o_ref.dtype)

    return pl.pallas_call(
        body, name=name, out_shape=out_shape,
        grid_spec=pltpu.PrefetchScalarGridSpec(
            num_scalar_prefetch=1, grid=grid, in_specs=list(in_specs) + [ANY] * len(deps), out_specs=out_specs),
        compiler_params=_params(*["parallel"] * len(grid)),
    )(place, *ins, *deps)


def _row_tile(R, C):
    return _tile(R, max(16, ROW_BLOCK_ELEMS // C))


def cast_into_full(w, place, name, dep=None):
    R, C = w.shape
    tr = _row_tile(R, C)
    return _placed_call(
        lambda a: (a,), place, (R // tr,), [w], [pl.BlockSpec((tr, C), lambda i, p: (i, 0))],
        [jax.ShapeDtypeStruct((N_CHIP, R, C), BF16)], [pl.BlockSpec((None, tr, C), lambda i, p: (p[0], i, 0))],
        name, dep=dep)[0]


def pair_sum(g4, got, place, name):
    _, hr, C = got.shape
    tr = _row_tile(hr, C)
    nb = hr // tr
    blk = pl.BlockSpec((None, tr, C), lambda s, i, p: (s, i, 0))
    return _placed_call(
        lambda a, b: (a + b,), place, (N_CHIP, nb), [g4, got],
        [pl.BlockSpec((None, tr, C), lambda s, i, p: (s, p[1] * nb + i, 0)), blk],
        [jax.ShapeDtypeStruct(got.shape, BF16)], [blk], name)[0]


def chip_sum(p32, got, place, name, dep=None):
    _, H, C = p32.shape
    tr = _row_tile(H, C)
    nb = H // tr
    peer = lambda j: pl.BlockSpec((None, tr, C), lambda i, p, j=j: (j, i, 0))
    return _placed_call(
        lambda a, b, c, d: (((a.astype(F32) + b.astype(F32)) + c.astype(F32)) + d.astype(F32),), place, (nb,),
        [p32, got, got, got], [pl.BlockSpec((None, tr, C), lambda i, p: (p[0], i, 0)), peer(0), peer(1), peer(2)],
        [jax.ShapeDtypeStruct((2 * H, C), F32)], [pl.BlockSpec((tr, C), lambda i, p: (p[1] * nb + i, 0))],
        name, dep=dep)[0]


ANY = pl.BlockSpec(memory_space=pl.ANY)
CHIP_FLIPS = ((1, 0), (0, 1), (1, 1))


def _place():
    return lax.axis_index("x"), lax.axis_index("y"), lax.axis_index("c")


def _flip(v, f):
    return 1 - v if f else v


def join_halves(fulls, name):
    n = len(fulls)

    def body(*refs):
        outs = refs[n:2 * n]
        send_sem, recv_sem = refs[2 * n:]
        x, y, c = _place()
        copies = []
        for a in range(n):
            hr = outs[a].shape[0] // 2
            half = outs[a].at[pl.ds(c * hr, hr), :]
            cp = pltpu.make_async_remote_copy(
                src_ref=half, dst_ref=half, send_sem=send_sem.at[a], recv_sem=recv_sem.at[a],
                device_id=(x, y, 1 - c), device_id_type=MESH)
            cp.start()
            copies.append(cp)
        for a, cp in enumerate(copies):
            hr = outs[a].shape[0] // 2
            theirs = outs[a].at[pl.ds((1 - c) * hr, hr), :]
            cp.wait_send()
            pltpu.make_async_remote_copy(
                src_ref=theirs, dst_ref=theirs, send_sem=send_sem.at[a], recv_sem=recv_sem.at[a],
                device_id=(x, y, 1 - c), device_id_type=MESH).wait_recv()

    dma = pltpu.SemaphoreType.DMA
    return pl.pallas_call(
        body, name=name, in_specs=[ANY] * n, out_specs=[ANY] * n,
        out_shape=[jax.ShapeDtypeStruct(f.shape, F32) for f in fulls],
        input_output_aliases={a: a for a in range(n)},
        scratch_shapes=[dma((n,)), dma((n,))],
    )(*fulls)


HBM = pl.BlockSpec(memory_space=pltpu.HBM)
SEM = pl.BlockSpec(memory_space=pltpu.SEMAPHORE)
EFFECT = pltpu.SideEffectType.DATAFLOW_SIDE_EFFECTING


def _in_hbm(a):
    return pltpu.with_memory_space_constraint(a, pltpu.HBM)


def _gather_half(ref, chip_idx, core):
    hr = ref.shape[1] // 2
    return ref.at[chip_idx, pl.ds(core * hr, hr), :]


def gather_forward(fulls, small, name):
    n = len(fulls)

    def body(*refs):
        small_in = refs[n]
        outs, small_out = refs[n + 1:2 * n + 1], refs[2 * n + 1]
        send_sem, recv_sem, sm_send, sm_recv, loc_sem = refs[2 * n + 2:]
        x, y, c = _place()
        mine = 2 * x + y
        chips = [(_flip(x, fx), _flip(y, fy)) for fx, fy in CHIP_FLIPS]
        local = pltpu.make_async_copy(small_in, small_out.at[mine], loc_sem)
        local.start()
        copies = []
        for j, (px, py) in enumerate(chips):
            cp = pltpu.make_async_remote_copy(
                src_ref=small_in, dst_ref=small_out.at[mine], send_sem=sm_send.at[j], recv_sem=sm_recv.at[j],
                device_id=(px, py, c), device_id_type=MESH)
            cp.start()
            copies.append(cp)
        for a in range(n):
            for j, (px, py) in enumerate(chips):
                src = _gather_half(outs[a], 2 * px + py, c)
                cp = pltpu.make_async_remote_copy(
                    src_ref=src, dst_ref=src, send_sem=send_sem.at[3 * a + j], recv_sem=recv_sem.at[3 * a + j],
                    device_id=(x, y, 1 - c), device_id_type=MESH)
                cp.start()
                copies.append(cp)
        for a in range(n):
            for j, (px, py) in enumerate(chips):
                dst = _gather_half(outs[a], 2 * px + py, 1 - c)
                pltpu.make_async_remote_copy(
                    src_ref=dst, dst_ref=dst, send_sem=send_sem.at[3 * a + j], recv_sem=recv_sem.at[3 * a + j],
                    device_id=(x, y, 1 - c), device_id_type=MESH).wait_recv()
        for j, (px, py) in enumerate(chips):
            dst = small_out.at[2 * px + py]
            pltpu.make_async_remote_copy(
                src_ref=dst, dst_ref=dst, send_sem=sm_send.at[j], recv_sem=sm_recv.at[j],
                device_id=(px, py, c), device_id_type=MESH).wait_recv()
        for cp in copies:
            cp.wait_send()
        local.wait()

    dma = pltpu.SemaphoreType.DMA
    out = pl.pallas_call(
        body, name=name, in_specs=[ANY] * (n + 1), out_specs=[ANY] * (n + 1),
        out_shape=[jax.ShapeDtypeStruct(f.shape, f.dtype) for f in fulls]
        + [jax.ShapeDtypeStruct((N_CHIP,) + small.shape, small.dtype)],
        input_output_aliases={a: a for a in range(n)},
        scratch_shapes=[dma((3 * n,)), dma((3 * n,)), dma((3,)), dma((3,)), dma],
    )(*fulls, small)
    return out[:n], out[n]


def _gather_plan(fulls, lands):
    x, y, c = _place()
    mine = 2 * x + y
    return [(_gather_half(f, mine, c), _gather_half(f, mine, c), (_flip(x, fx), _flip(y, fy), c))
            for f in fulls for fx, fy in CHIP_FLIPS]


def _scatter_plan(parts, lands):
    x, y, c = _place()
    plan = []
    for p, l in zip(parts, lands):
        for j, (fx, fy) in enumerate(CHIP_FLIPS):
            px, py = _flip(x, fx), _flip(y, fy)
            plan.append((p.at[2 * px + py], l.at[j], (px, py, c)))
    return plan


def _scatter_lands(parts):
    return [(3,) + p.shape[1:] for p in parts]


def _exchange_plan(grads, lands):
    x, y, c = _place()
    plan = []
    for g, l in zip(grads, lands):
        hr = g.shape[1] // 2
        plan.append((g.at[:, pl.ds((1 - c) * hr, hr), :], l, (x, y, 1 - c)))
    return plan


def _exchange_lands(grads):
    return [(N_CHIP, g.shape[1] // 2, g.shape[2]) for g in grads]


def _forward_plan(fulls, lands):
    x, y, c = _place()
    return [(_gather_half(f, 2 * _flip(x, fx) + _flip(y, fy), c), _gather_half(f, 2 * _flip(x, fx) + _flip(y, fy), c),
             (x, y, 1 - c)) for f in fulls for fx, fy in CHIP_FLIPS]


def split_start(plan, copies, srcs, land_shapes, deps, name):
    n, m = len(srcs), len(land_shapes)
    lands = [lax.empty(s, srcs[0].dtype) for s in land_shapes]

    def body(*refs):
        k0 = n + m + len(deps)
        send_sem, recv_sem = refs[k0], refs[k0 + 1]
        thru, token = refs[k0 + 2:k0 + 2 + n + m], refs[k0 + 2 + n + m]
        for k, (src, dst, dev) in enumerate(plan(thru[:n], thru[n:])):
            pltpu.make_async_remote_copy(src_ref=src, dst_ref=dst, send_sem=send_sem.at[k], recv_sem=recv_sem.at[k],
                                         device_id=dev, device_id_type=MESH).start()
        token[...] = jnp.zeros_like(token)

    dma = pltpu.SemaphoreType.DMA
    arrays = list(srcs) + lands
    out = pl.pallas_call(
        body, name=name,
        out_shape=(dma((copies,)), dma((copies,)), *[pltpu.HBM(a.shape, a.dtype) for a in arrays],
                   jax.ShapeDtypeStruct((8, LANES), F32)),
        in_specs=[HBM] * (n + m) + [ANY] * len(deps),
        out_specs=(SEM, SEM, *[HBM] * (n + m), pl.BlockSpec(memory_space=pltpu.VMEM)),
        input_output_aliases={a: 2 + a for a in range(n + m)},
        compiler_params=pltpu.CompilerParams(has_side_effects=EFFECT),
    )(*[_in_hbm(a) for a in arrays], *deps)
    return (out[0], out[1], list(out[2:2 + n]), list(out[2 + n:2 + n + m])), out[2 + n + m]


def split_wait(plan, handle, after, name):
    send_sem, recv_sem, srcs, lands = handle
    n, m = len(srcs), len(lands)

    def body(*refs):
        send_sem, recv_sem = refs[n + m], refs[n + m + 1]
        thru = refs[n + m + 2 + len(after):]
        for k, (src, dst, dev) in enumerate(plan(thru[:n], thru[n:])):
            cp = pltpu.make_async_remote_copy(src_ref=src, dst_ref=dst, send_sem=send_sem.at[k],
                                              recv_sem=recv_sem.at[k], device_id=dev, device_id_type=MESH)
            cp.wait_send()
            cp.wait_recv()

    arrays = list(srcs) + list(lands)
    out = pl.pallas_call(
        body, name=name, out_shape=tuple(pltpu.HBM(a.shape, a.dtype) for a in arrays),
        in_specs=[HBM] * (n + m) + [SEM, SEM] + [ANY] * len(after), out_specs=tuple([HBM] * (n + m)),
        input_output_aliases={a: a for a in range(n + m)},
        compiler_params=pltpu.CompilerParams(has_side_effects=EFFECT),
    )(*arrays, send_sem, recv_sem, *after)
    return list(out[:n]), list(out[n:])


def allreduce_small(block, name, dep):
    R, C = block.shape

    def body(in_ref, dep_ref, out_ref, slots, send_sem, recv_sem):
        x, y, c = _place()
        me = 4 * x + 2 * y + c
        slots[me] = in_ref[...]
        copies = []
        for r in range(1, 8):
            fx, fy, fc = (r >> 2) & 1, (r >> 1) & 1, r & 1
            cp = pltpu.make_async_remote_copy(
                src_ref=in_ref, dst_ref=slots.at[me], send_sem=send_sem.at[r - 1], recv_sem=recv_sem.at[r - 1],
                device_id=(_flip(x, fx), _flip(y, fy), _flip(c, fc)), device_id_type=MESH)
            cp.start()
            copies.append(cp)
        for cp in copies:
            cp.wait()
        acc = slots[0]
        for d in range(1, 8):
            acc = acc + slots[d]
        out_ref[...] = acc

    vm = pl.BlockSpec(memory_space=pltpu.VMEM)
    dma = pltpu.SemaphoreType.DMA
    return pl.pallas_call(
        body, name=name, in_specs=[vm, ANY], out_specs=vm, out_shape=jax.ShapeDtypeStruct((R, C), F32),
        scratch_shapes=[pltpu.VMEM((8, R, C), F32), dma((7,)), dma((7,))],
    )(block, dep)


def local_step(x, mem, target, g_mix, g_mem, q_norm_g, k_norm_g, g_mlp, conv_w, h, mem_n, proj_part, place, w_in, w_in_dep,
               rest_weights, first_grads, early_grads, mid_grads, late_grads, last_grads):
    S, D = x.shape
    proj = proj_rest(h, w_in, proj_part, place, "proj", dep=w_in_dep)
    a_conv = conv_fwd(proj, conv_w, D, "conv_fwd")
    o_sb = sb_fwd(proj, D, "sb_fwd")
    w_conv_out, w_sb_out, w_mem_kv, w_x_out, w_out, mlp_dep, mlp_weights = rest_weights(o_sb)
    kv = mm_nn_shard(mem_n, w_mem_kv, "kv", dep=mlp_dep)
    o_x = xa_fwd(proj, kv, q_norm_g, k_norm_g, D, "xa_fwd")
    ys, merged, x1, h2 = merge_fwd(proj, (a_conv, o_sb, o_x), (w_conv_out, w_sb_out, w_x_out), w_out, x, g_mlp, D,
                                   "merge_x1")
    w_up, w_down = mlp_weights(h2)
    up, act = mm_nn_shard(h2, w_up, "up", relu2=True)
    dy, dy16, loss_parts = mm_nn_loss(act, w_down, x1, target, "x2_loss")
    loss_cols = jnp.sum(loss_parts, axis=0)
    d_up = mm_nt_relu2(dy16, w_down, up, "d_up")
    g = {"w_down": mm_tn(act, dy16, "g_w_down")}
    g["w_up"] = mm_tn(h2, d_up, "g_w_up", shard_out=True)
    dh2 = mm_nt_shard(d_up, w_up, "dh2")
    dx1, dx1_16, g["g_mlp"] = rms_bwd(x1, g_mlp, dh2, "rms_mlp_bwd", dres=dy, want16=True)
    g["w_out"] = mm_tn(merged, dx1_16, "g_w_out")
    (dy_c, dy_s, dy_x), (d_a_conv, d_o_sb, d_o_x), d_gate = merge_bwd(
        proj, ys, dx1_16, w_out, (w_conv_out, w_sb_out, w_x_out), D, "merge_bwd", dep=first_grads(g))
    g["w_conv_out"] = mm_tn(a_conv, dy_c, "g_w_conv_out")
    g["w_sb_out"] = mm_tn(o_sb, dy_s, "g_w_sb_out")
    g["w_x_out"] = mm_tn(o_x, dy_x, "g_w_x_out")
    d_xq, d_kv, g["q_norm_g"], g["k_norm_g"] = xa_bwd(proj, kv, q_norm_g, k_norm_g, d_o_x, D, "xa_bwd")
    g["w_mem_kv"] = mm_tn(mem_n, d_kv, "g_w_mem_kv", shard_out=True)
    g["g_mem"] = rms_bwd(mem, g_mem, mm_nt_shard(d_kv, w_mem_kv, "d_mem_n"), "rms_mem_bwd", want_dx=False)
    d_ch, d_cb, d_cc, g["conv_w"] = conv_bwd(proj, conv_w, d_a_conv, D, "conv_bwd", dep=early_grads(g))
    dq, dk, dv = sb_bwd(proj, d_o_sb, D, "sb_bwd", dep=mid_grads([d_ch]))
    d_proj = jnp.concatenate([d_ch, d_cb, d_cc, dq, dk, dv, d_xq, d_gate], axis=1)
    g["w_in"] = mm_tn(h, d_proj, "g_w_in", shard_out=True)
    dh = mm_nt_shard(d_proj, w_in, "dh", dep=late_grads(g["w_in"]))
    grad_x, g["g_mix"] = rms_bwd(x, g_mix, dh, "rms_mix_bwd", dres=dx1, dep=last_grads(dh))
    return loss_cols, grad_x, g


BIG = ("w_in", "w_conv_out", "w_sb_out", "w_mem_kv", "w_x_out", "w_out", "w_up", "w_down")
REST = BIG[1:]
MLP_SIDE = ("w_out", "w_up", "w_down")
MIXER_SIDE = tuple(k for k in REST if k not in MLP_SIDE)
COL_SHARDED = ("w_in", "w_mem_kv", "w_up")
WEIGHTS = ("g_mix", "g_mem", "w_in", "conv_w", "w_conv_out", "w_sb_out", "q_norm_g", "k_norm_g",
           "w_mem_kv", "w_x_out", "w_out", "g_mlp", "w_up", "w_down")


def _pack_small(D, g_mix, g_mem, g_mlp, q_norm_g, k_norm_g, conv_w, last):
    qk = jnp.concatenate([q_norm_g, k_norm_g, jnp.zeros((1, D - 2 * X_HEAD_DIM), F32)], axis=1)
    cw = jnp.pad(conv_w, ((0, 0), (0, D - conv_w.shape[1])))
    return jnp.concatenate([g_mix, g_mem, g_mlp, qk, cw, last], axis=0)


def kernel(x, mem, g_mix, g_mem, w_in, conv_w, w_conv_out, w_sb_out, q_norm_g, k_norm_g, w_mem_kv, w_x_out, w_out, g_mlp, w_up, w_down, loss_target, m_g_mix, m_g_mem, m_w_in, m_conv_w, m_w_conv_out, m_w_sb_out, m_q_norm_g, m_k_norm_g, m_w_mem_kv, m_w_x_out, m_w_out, m_g_mlp, m_w_up, m_w_down, v_g_mix, v_g_mem, v_w_in, v_conv_w, v_w_conv_out, v_w_sb_out, v_q_norm_g, v_k_norm_g, v_w_mem_kv, v_w_x_out, v_w_out, v_g_mlp, v_w_up, v_w_down):
    S, D = x.shape[1], x.shape[2]
    w = dict(g_mix=g_mix, g_mem=g_mem, w_in=w_in, conv_w=conv_w, w_conv_out=w_conv_out, w_sb_out=w_sb_out,
             q_norm_g=q_norm_g, k_norm_g=k_norm_g, w_mem_kv=w_mem_kv, w_x_out=w_x_out, w_out=w_out,
             g_mlp=g_mlp, w_up=w_up, w_down=w_down)
    m = dict(g_mix=m_g_mix, g_mem=m_g_mem, w_in=m_w_in, conv_w=m_conv_w, w_conv_out=m_w_conv_out,
             w_sb_out=m_w_sb_out, q_norm_g=m_q_norm_g, k_norm_g=m_k_norm_g, w_mem_kv=m_w_mem_kv,
             w_x_out=m_w_x_out, w_out=m_w_out, g_mlp=m_g_mlp, w_up=m_w_up, w_down=m_w_down)
    v = dict(g_mix=v_g_mix, g_mem=v_g_mem, w_in=v_w_in, conv_w=v_conv_w, w_conv_out=v_w_conv_out,
             w_sb_out=v_w_sb_out, q_norm_g=v_q_norm_g, k_norm_g=v_k_norm_g, w_mem_kv=v_w_mem_kv,
             w_x_out=v_w_x_out, w_out=v_w_out, g_mlp=v_g_mlp, w_up=v_w_up, w_down=v_w_down)
    chip = 2 * lax.axis_index("x") + lax.axis_index("y")
    cs = conv_w.shape[2]

    place = jnp.stack([chip, lax.axis_index("c")]).astype(jnp.int32)
    cw_block = jnp.pad(conv_w[0], ((0, 5), (0, 0)))
    handle, token = split_start(_gather_plan, 3, [cast_into_full(w["w_in"][0], place, "cast_w_in")], [], [],
                                "gather_w_in_start")
    rest16 = [cast_into_full(w[k][0], place, "cast_" + k, dep=token) for k in REST]
    h = rms_fwd(x[0], g_mix, "rms_mix", dep=token)
    mem_n = rms_fwd(mem[0], g_mem, "rms_mem", dep=token)
    proj_part = proj_own(h, w["w_in"][0], place, "proj_own", dep=token)
    landed, _ = split_wait(_gather_plan, handle, [*rest16, mem_n, proj_part], "gather_w_in_wait")
    (w_in_full,), cw_all = gather_forward(landed, cw_block, "gather_w_in_forward")
    conv_full = jnp.concatenate([cw_all[p, :3] for p in range(N_CHIP)], axis=1)
    rest_handle, rest_token = split_start(_gather_plan, 3 * len(REST), rest16, [], [w_in_full], "gather_rest_start")

    def layout(k, a):
        return a if k in COL_SHARDED else a.reshape(-1, a.shape[-1])

    def rest_weights(after):
        landed, _ = split_wait(_gather_plan, rest_handle, [after], "gather_rest_wait")
        first = gather_forward(landed[:-2], cw_block, "gather_rest_forward")[0]
        mlp_handle, token = split_start(_forward_plan, 6, landed[-2:], [], [first[0]], "forward_mlp_start")

        def mlp_weights(after):
            both, _ = split_wait(_forward_plan, mlp_handle, [after], "forward_mlp_wait")
            return [layout(k, a) for k, a in zip(REST[-2:], both)]

        return [layout(k, a) for k, a in zip(REST[:-2], first)] + [token, mlp_weights]

    def blocks(k, a):
        return a if k in COL_SHARDED else a.reshape(N_CHIP, -1, a.shape[-1])

    early = {}

    def swap_start(names, g, dep, name):
        g4 = [blocks(k, g[k]) for k in names]
        early[names], token = split_start(_exchange_plan, len(names), g4, _exchange_lands(g4), [dep], name)
        return token

    def first_grads(g):
        return swap_start(MLP_SIDE, g, g["g_mlp"], "exchange_mlp_start")

    def early_grads(g):
        return swap_start(MIXER_SIDE, g, g["g_mem"], "exchange_mixers_start")

    def mid_grads(after):
        pairs = {}
        for names, name in ((MLP_SIDE, "exchange_mlp_wait"), (MIXER_SIDE, "exchange_mixers_wait")):
            g4, got = split_wait(_exchange_plan, early[names], after, name)
            pairs.update(zip(names, zip(g4, got)))
        p16 = [pair_sum(*pairs[k], place, "pair_sum_" + k) for k in REST]
        early["fly"], token = split_start(_scatter_plan, 3 * len(REST), p16, _scatter_lands(p16), [],
                                          "scatter_rest_start")
        return token

    late = {}

    def late_grads(gw):
        late["swap"], token = split_start(_exchange_plan, 1, [gw], _exchange_lands([gw]), [], "exchange_w_in_start")
        return token

    def last_grads(after):
        (gw,), (got,) = split_wait(_exchange_plan, late["swap"], [after], "exchange_w_in_wait")
        p16 = [pair_sum(gw, got, place, "pair_sum_w_in")]
        late["fly"], token = split_start(_scatter_plan, 3, p16, _scatter_lands(p16), [], "scatter_w_in_start")
        return token

    loss_cols, grad_x, g = local_step(
        x[0], mem[0], loss_target[0], g_mix, g_mem, q_norm_g, k_norm_g, g_mlp, conv_full, h, mem_n, proj_part, place,
        w_in_full, rest_token, rest_weights, first_grads, early_grads, mid_grads, late_grads, last_grads)
    token = g["g_mix"]

    p16_rest, got_rest = split_wait(_scatter_plan, early["fly"], [token], "scatter_rest_wait")
    gsum, delta, new_m, new_v = {}, {}, {}, {}
    halves = [chip_sum(p, b, place, "chip_sum_" + k, dep=token) for k, p, b in zip(REST, p16_rest, got_rest)]
    for k, a in zip(REST, join_halves(halves, "join_halves_rest")):
        gsum[k] = a[None]
        delta[k], new_m[k], new_v[k] = adamw(w[k], gsum[k], m[k], v[k], "adamw_" + k)
    p16_in, got_in = split_wait(_scatter_plan, late["fly"], [new_v[k] for k in REST], "scatter_w_in_wait")

    small = allreduce_small(
        _pack_small(D, g["g_mix"], g["g_mem"], g["g_mlp"], g["q_norm_g"], g["k_norm_g"], g["conv_w"], loss_cols),
        "allreduce_small", dep=got_in[0])
    loss = (0.5 / D) * jnp.sum(small[7])
    gsum.update({"g_mix": small[0:1], "g_mem": small[1:2], "g_mlp": small[2:3],
                 "q_norm_g": small[3:4, :X_HEAD_DIM], "k_norm_g": small[3:4, X_HEAD_DIM:2 * X_HEAD_DIM],
                 "conv_w": lax.dynamic_slice(small[4:7], (0, chip * cs), (3, cs))[None]})
    half_in = chip_sum(p16_in[0], got_in[0], place, "chip_sum_w_in")
    gsum["w_in"] = join_halves([half_in], "join_halves_w_in")[0][None]
    delta["w_in"], new_m["w_in"], new_v["w_in"] = adamw(w["w_in"], gsum["w_in"], m["w_in"], v["w_in"], "adamw_w_in")
    small_names = ("g_mix", "g_mem", "g_mlp", "q_norm_g", "k_norm_g", "conv_w")
    zero_row = jnp.zeros((1, D), F32)
    packed = [_pack_small(D, *[t[k] if k != "conv_w" else t[k][0] for k in small_names], zero_row)
              for t in (w, gsum, m, v)]
    sm = adamw(*packed, "adamw_small")
    for t, block in zip((delta, new_m, new_v), sm):
        t["g_mix"], t["g_mem"], t["g_mlp"] = block[0:1], block[1:2], block[2:3]
        t["q_norm_g"], t["k_norm_g"] = block[3:4, :X_HEAD_DIM], block[3:4, X_HEAD_DIM:2 * X_HEAD_DIM]
        t["conv_w"] = block[4:7, :cs][None]

    return (loss, grad_x[None], *[gsum[k] for k in WEIGHTS], *[delta[k] for k in WEIGHTS],
            *[new_m[k] for k in WEIGHTS], *[new_v[k] for k in WEIGHTS])
```

```python
import jax
import jax.numpy as jnp
from jax import lax
from jax.experimental import pallas as pl
from jax.experimental.pallas import tpu as pltpu

F32 = jnp.float32
BF16 = jnp.bfloat16
EPS = 1e-6
N_CHIP = 4
SB_HEAD_DIM = 64
X_HEAD_DIM = 256
LANES = 128
VMEM_LIMIT = 56 * 1024 * 1024
ROW_BLOCK_ELEMS = 1 << 19
ADAM_LR, ADAM_B1, ADAM_B2, ADAM_EPS, ADAM_WD, ADAM_STEP = 0.001, 0.9, 0.999, 1e-8, 0.01, 10
MESH = pl.DeviceIdType.MESH


def _params(*sem):
    return pltpu.CompilerParams(dimension_semantics=sem, vmem_limit_bytes=VMEM_LIMIT)


def _tile(n, pref):
    if n <= pref:
        return n
    t = 1 << (pref.bit_length() - 1)
    while n % t:
        t //= 2
    return t


NN = (((1,), (0,)), ((), ()))
NT = (((1,), (1,)), ((), ()))
TN = (((0,), (0,)), ((), ()))


def _dot(a, b, dims):
    return lax.dot_general(a.astype(BF16), b.astype(BF16), dims, preferred_element_type=F32)


def mm_nn_shard(a, g, name, relu2=False, dep=None, out_dtype=F32):
    M, K = a.shape
    _, _, Ns = g.shape
    tm, tn = _tile(M, 2048), _tile(Ns, 512)
    nb = Ns // tn

    def body(a_ref, b_ref, *o_refs):
        o_refs = o_refs[len(deps):]
        acc = _dot(a_ref[...], b_ref[...], NN)
        if relu2:
            acc = jnp.maximum(acc, 0.0)
            o_refs[1][...] = (acc * acc).astype(BF16)
        o_refs[0][...] = acc.astype(o_refs[0].dtype)

    o_spec = pl.BlockSpec((tm, tn), lambda i, j: (i, j))
    shapes = [jax.ShapeDtypeStruct((M, N_CHIP * Ns), BF16 if relu2 else out_dtype)]
    specs = [o_spec]
    if relu2:
        shapes.append(jax.ShapeDtypeStruct((M, N_CHIP * Ns), BF16))
        specs.append(o_spec)
    deps = [] if dep is None else [dep]
    out = pl.pallas_call(
        body, grid=(M // tm, N_CHIP * nb), name=name,
        in_specs=[pl.BlockSpec((tm, K), lambda i, j: (i, 0)),
                  pl.BlockSpec((None, K, tn), lambda i, j: (j // nb, 0, j % nb))] + [ANY] * len(deps),
        out_specs=specs, out_shape=shapes, compiler_params=_params("parallel", "parallel"),
    )(a, g, *deps)
    return out if relu2 else out[0]


def proj_own(a, w_own, place, name, dep):
    M, K = a.shape
    Ns = w_own.shape[1]
    tm, tn = _tile(M, 2048), _tile(Ns, 512)
    nb = Ns // tn

    def body(place_ref, a_ref, b_ref, dep_ref, o_ref):
        o_ref[...] = _dot(a_ref[...], b_ref[...], NN).astype(BF16)

    return pl.pallas_call(
        body, name=name, out_shape=jax.ShapeDtypeStruct((M, N_CHIP * Ns), BF16),
        grid_spec=pltpu.PrefetchScalarGridSpec(
            num_scalar_prefetch=1, grid=(M // tm, nb),
            in_specs=[pl.BlockSpec((tm, K), lambda i, j, p: (i, 0)), pl.BlockSpec((K, tn), lambda i, j, p: (0, j)), ANY],
            out_specs=pl.BlockSpec((tm, tn), lambda i, j, p: (i, p[0] * nb + j))),
        compiler_params=_params("parallel", "parallel"),
    )(place, a, w_own, dep)


def proj_rest(a, g, part, place, name, dep):
    M, K = a.shape
    _, _, Ns = g.shape
    tm, tn = _tile(M, 2048), _tile(Ns, 512)
    nb = Ns // tn
    chip = lambda j, p: (p[0] + 1 + j // nb) % N_CHIP

    def body(place_ref, a_ref, b_ref, part_ref, dep_ref, o_ref):
        o_ref[...] = _dot(a_ref[...], b_ref[...], NN).astype(BF16)

    return pl.pallas_call(
        body, name=name, out_shape=jax.ShapeDtypeStruct(part.shape, BF16),
        grid_spec=pltpu.PrefetchScalarGridSpec(
            num_scalar_prefetch=1, grid=(M // tm, (N_CHIP - 1) * nb),
            in_specs=[pl.BlockSpec((tm, K), lambda i, j, p: (i, 0)),
                      pl.BlockSpec((None, K, tn), lambda i, j, p: (chip(j, p), 0, j % nb)), ANY, ANY],
            out_specs=pl.BlockSpec((tm, tn), lambda i, j, p: (i, chip(j, p) * nb + j % nb))),
        input_output_aliases={3: 0},
        compiler_params=_params("parallel", "parallel"),
    )(place, a, g, part, dep)


def mm_nn_loss(a, w, res, target, name):
    M, K = a.shape
    N = w.shape[1]
    tm, tn = _tile(M, 1024), _tile(N, 512)

    def body(a_ref, b_ref, r_ref, t_ref, dy_ref, dy16_ref, l_ref):
        e = (r_ref[...] + _dot(a_ref[...], b_ref[...], NN)) - t_ref[...]
        dy = e * (1.0 / N)
        dy_ref[...] = dy
        dy16_ref[...] = dy.astype(BF16)
        l_ref[...] = jnp.sum(e * e, axis=0, keepdims=True)

    o_spec = pl.BlockSpec((tm, tn), lambda i, j: (i, j))
    return pl.pallas_call(
        body, grid=(M // tm, N // tn), name=name,
        in_specs=[pl.BlockSpec((tm, K), lambda i, j: (i, 0)), pl.BlockSpec((K, tn), lambda i, j: (0, j)),
                  o_spec, o_spec],
        out_specs=[o_spec, o_spec, pl.BlockSpec((None, 1, tn), lambda i, j: (i, 0, j))],
        out_shape=[jax.ShapeDtypeStruct((M, N), F32), jax.ShapeDtypeStruct((M, N), BF16),
                   jax.ShapeDtypeStruct((M // tm, 1, N), F32)],
        compiler_params=_params("parallel", "parallel"),
    )(a, w, res, target)


def mm_nt_relu2(a, w, up, name):
    M, N = a.shape
    R = w.shape[0]
    tm, tr = _tile(M, 2048), _tile(R, 512)

    def body(a_ref, b_ref, up_ref, o_ref):
        acc = _dot(a_ref[...], b_ref[...], NT)
        o_ref[...] = (acc * (2.0 * jnp.maximum(up_ref[...].astype(F32), 0.0))).astype(BF16)

    o_spec = pl.BlockSpec((tm, tr), lambda i, j: (i, j))
    return pl.pallas_call(
        body, grid=(M // tm, R // tr), name=name,
        in_specs=[pl.BlockSpec((tm, N), lambda i, j: (i, 0)), pl.BlockSpec((tr, N), lambda i, j: (j, 0)), o_spec],
        out_specs=o_spec, out_shape=jax.ShapeDtypeStruct((M, R), BF16),
        compiler_params=_params("parallel", "parallel"),
    )(a, w, up)


def mm_nt_shard(a, g, name, out_dtype=F32, dep=None):
    deps = [] if dep is None else [dep]
    M = a.shape[0]
    _, R, Ns = g.shape
    tm, tr, tk = _tile(M, 1024), _tile(R, 1024), _tile(Ns, 2560)
    nb = Ns // tk
    nk = N_CHIP * nb

    def body(a_ref, b_ref, *refs):
        o_ref, acc_ref = refs[len(deps):]
        k = pl.program_id(2)

        @pl.when(k == 0)
        def _():
            acc_ref[...] = jnp.zeros_like(acc_ref)

        acc_ref[...] += _dot(a_ref[...], b_ref[...], NT)

        @pl.when(k == nk - 1)
        def _():
            o_ref[...] = acc_ref[...].astype(out_dtype)

    return pl.pallas_call(
        body, grid=(M // tm, R // tr, nk), name=name,
        in_specs=[pl.BlockSpec((tm, tk), lambda i, j, k: (i, k)),
                  pl.BlockSpec((None, tr, tk), lambda i, j, k: (k // nb, j, k % nb))] + [ANY] * len(deps),
        out_specs=pl.BlockSpec((tm, tr), lambda i, j, k: (i, j)),
        out_shape=jax.ShapeDtypeStruct((M, R), out_dtype),
        scratch_shapes=[pltpu.VMEM((tm, tr), F32)],
        compiler_params=_params("parallel", "parallel", "arbitrary"),
    )(a, g, *deps)


def mm_tn(a, b, name, shard_out=False):
    S, M = a.shape
    N = b.shape[1]
    Ns = N // N_CHIP if shard_out else N
    tm, tn = _tile(M, 1024), _tile(Ns, 512)
    nb = Ns // tn

    def body(a_ref, b_ref, o_ref):
        o_ref[...] = _dot(a_ref[...], b_ref[...], TN)

    if shard_out:
        o_spec = pl.BlockSpec((None, tm, tn), lambda i, j: (j // nb, i, j % nb))
        o_shape = jax.ShapeDtypeStruct((N_CHIP, M, Ns), F32)
    else:
        o_spec = pl.BlockSpec((tm, tn), lambda i, j: (i, j))
        o_shape = jax.ShapeDtypeStruct((M, N), F32)
    return pl.pallas_call(
        body, grid=(M // tm, N // tn), name=name,
        in_specs=[pl.BlockSpec((S, tm), lambda i, j: (0, i)), pl.BlockSpec((S, tn), lambda i, j: (0, j))],
        out_specs=o_spec, out_shape=o_shape, compiler_params=_params("parallel", "parallel"),
    )(a, b)


def rms_fwd(x, g, name, dep=None):
    S, D = x.shape
    tm = _tile(S, 512)
    deps = [] if dep is None else [dep]

    def body(x_ref, g_ref, *refs):
        xv = x_ref[...]
        r = lax.rsqrt(jnp.mean(xv * xv, axis=-1, keepdims=True) + EPS)
        refs[-1][...] = ((xv * r) * g_ref[...]).astype(BF16)

    return pl.pallas_call(
        body, grid=(S // tm,), name=name,
        in_specs=[pl.BlockSpec((tm, D), lambda i: (i, 0)), pl.BlockSpec((1, D), lambda i: (0, 0))]
        + [ANY] * len(deps),
        out_specs=pl.BlockSpec((tm, D), lambda i: (i, 0)),
        out_shape=jax.ShapeDtypeStruct((S, D), BF16), compiler_params=_params("parallel"),
    )(x, g, *deps)


def rms_bwd(x, g, dh, name, dres=None, want_dx=True, want16=False, dep=None):
    S, D = x.shape
    tm = _tile(S, 512)

    def body(x_ref, g_ref, dh_ref, *refs):
        i = pl.program_id(0)
        xv = x_ref[...]
        r = lax.rsqrt(jnp.mean(xv * xv, axis=-1, keepdims=True) + EPS)
        xn = xv * r
        dhv = dh_ref[...].astype(F32)
        gg_ref = refs[-1]

        @pl.when(i == 0)
        def _():
            gg_ref[...] = jnp.zeros_like(gg_ref)

        gg_ref[...] += jnp.sum(dhv * xn, axis=0, keepdims=True)
        if want_dx:
            dxn = dhv * g_ref[...]
            dx = r * (dxn - xn * jnp.mean(dxn * xn, axis=-1, keepdims=True))
            if dres is not None:
                dx = refs[0][...] + dx
            refs[-2][...] = dx.astype(refs[-2].dtype)
            if want16:
                refs[-3][...] = dx

    row = pl.BlockSpec((tm, D), lambda i: (i, 0))
    vec = pl.BlockSpec((1, D), lambda i: (0, 0))
    ins, in_specs = [x, g, dh], [row, vec, row]
    if dres is not None:
        ins.append(dres)
        in_specs.append(row)
    if dep is not None:
        ins.append(dep)
        in_specs.append(ANY)
    shapes, specs = [jax.ShapeDtypeStruct((1, D), F32)], [vec]
    if want_dx:
        if want16:
            shapes.insert(0, jax.ShapeDtypeStruct((S, D), BF16))
            specs.insert(0, row)
        shapes.insert(0, jax.ShapeDtypeStruct((S, D), F32))
        specs.insert(0, row)
    out = pl.pallas_call(body, grid=(S // tm,), name=name, in_specs=in_specs, out_specs=specs,
                         out_shape=shapes, compiler_params=_params("arbitrary"))(*ins)
    return out if want_dx else out[0]


def _shift_down(u, k, row):
    return jnp.where(row >= k, pltpu.roll(u, k, axis=0), 0.0)


def _shift_up(u, k, row):
    S = u.shape[0]
    return jnp.where(row < S - k, pltpu.roll(u, S - k, axis=0), 0.0)


def _conv_specs(S, D, tc):
    nb = D // tc
    col = lambda o: pl.BlockSpec((S, tc), lambda j, o=o: (0, o * nb + j))
    return col, pl.BlockSpec((3, tc), lambda j: (0, j))


def conv_fwd(proj, conv_w, D, name):
    S = proj.shape[0]
    tc = _tile(D, 256)
    col, wspec = _conv_specs(S, D, tc)

    def body(ch_ref, cb_ref, cc_ref, w_ref, a_ref):
        row = lax.broadcasted_iota(jnp.int32, (S, tc), 0)
        u = cc_ref[...].astype(F32) * ch_ref[...].astype(F32)
        w = w_ref[...]
        cv = w[0:1, :] * _shift_down(u, 2, row) + w[1:2, :] * _shift_down(u, 1, row) + w[2:3, :] * u
        a_ref[...] = (cb_ref[...].astype(F32) * cv).astype(BF16)

    return pl.pallas_call(
        body, grid=(D // tc,), name=name, in_specs=[col(0), col(1), col(2), wspec],
        out_specs=pl.BlockSpec((S, tc), lambda j: (0, j)),
        out_shape=jax.ShapeDtypeStruct((S, D), BF16), compiler_params=_params("parallel"),
    )(proj, proj, proj, conv_w)


def conv_bwd(proj, conv_w, da, D, name, dep):
    S = proj.shape[0]
    tc = _tile(D, 256)
    col, wspec = _conv_specs(S, D, tc)
    blk = pl.BlockSpec((S, tc), lambda j: (0, j))

    def body(ch_ref, cb_ref, cc_ref, w_ref, da_ref, dep_ref, dch_ref, dcb_ref, dcc_ref, gw_ref):
        row = lax.broadcasted_iota(jnp.int32, (S, tc), 0)
        ch, cb, cc, dav = [r[...].astype(F32) for r in (ch_ref, cb_ref, cc_ref, da_ref)]
        w = w_ref[...]
        u = cc * ch
        u1, u2 = _shift_down(u, 1, row), _shift_down(u, 2, row)
        cv = w[0:1, :] * u2 + w[1:2, :] * u1 + w[2:3, :] * u
        dcb_ref[...] = (dav * cv).astype(BF16)
        dcv = dav * cb
        gw_ref[0:1, :] = jnp.sum(dcv * u2, axis=0, keepdims=True)
        gw_ref[1:2, :] = jnp.sum(dcv * u1, axis=0, keepdims=True)
        gw_ref[2:3, :] = jnp.sum(dcv * u, axis=0, keepdims=True)
        du = w[2:3, :] * dcv + w[1:2, :] * _shift_up(dcv, 1, row) + w[0:1, :] * _shift_up(dcv, 2, row)
        dcc_ref[...] = (du * ch).astype(BF16)
        dch_ref[...] = (du * cc).astype(BF16)

    act = jax.ShapeDtypeStruct((S, D), BF16)
    return pl.pallas_call(
        body, grid=(D // tc,), name=name, in_specs=[col(0), col(1), col(2), wspec, blk, ANY],
        out_specs=[blk, blk, blk, wspec], out_shape=[act, act, act, jax.ShapeDtypeStruct((3, D), F32)],
        compiler_params=_params("parallel"),
    )(proj, proj, proj, conv_w, da, dep)


SB_BQ_FWD = 512
SB_BQ_BWD = 256
SB_BK = 128
SB_GROUP = 4


def _sb_consts(bq):
    lane = lax.broadcasted_iota(jnp.int32, (bq, LANES), 1)
    r = lax.broadcasted_iota(jnp.int32, (SB_BK, SB_BK), 0)
    c = lax.broadcasted_iota(jnp.int32, (SB_BK, SB_BK), 1)
    tri_rev = jnp.where(r > c, 1.0, 0.0).astype(BF16)
    tri_fwd = jnp.where(r < c, 1.0, 0.0).astype(BF16)
    return lane, tri_rev, tri_fwd


def _cumsum2(v, tri):
    hi = v.astype(BF16)
    lo = (v - hi.astype(F32)).astype(BF16)
    part = (lax.dot_general(hi, tri, NN, preferred_element_type=F32)
            + lax.dot_general(lo, tri, NN, preferred_element_type=F32))
    return part, jnp.sum(v, axis=1, keepdims=True)


def _sb_logits(z, past):
    sp = jnp.log(1.0 + jnp.exp(-jnp.abs(z)))
    l = jnp.minimum(z, 0.0) - sp
    m = l - z
    if past is not None:
        m = jnp.where(past, m, 0.0)
    return l, m


def _stack_heads(v, lane):
    return jnp.concatenate([jnp.where(lane < SB_HEAD_DIM, v, 0.0), jnp.where(lane >= SB_HEAD_DIM, v, 0.0)],
                           axis=0).astype(BF16)


def _unstack_heads(v, lane):
    bq = v.shape[0] // 2
    return jnp.where(lane < SB_HEAD_DIM, v[:bq], v[bq:])


def _sb_positions(i, bq):
    r = lax.broadcasted_iota(jnp.int32, (2 * bq, SB_BK), 0)
    trow = i * bq + jnp.where(r >= bq, r - bq, r)
    return trow, lax.broadcasted_iota(jnp.int32, (2 * bq, SB_BK), 1)


def _sb_specs(S, D, bq):
    npair = D // LANES
    qspec = pl.BlockSpec((bq, LANES), lambda p, i: (i, 3 * npair + p))
    kspec = pl.BlockSpec((S, LANES), lambda p, i: (0, 4 * npair + p))
    vspec = pl.BlockSpec((S, LANES), lambda p, i: (0, 5 * npair + p))
    return npair, qspec, kspec, vspec


def sb_fwd(proj, D, name):
    S = proj.shape[0]
    bq = min(SB_BQ_FWD, S)
    nd = bq // SB_BK
    npair, qspec, kspec, vspec = _sb_specs(S, D, bq)
    scale = SB_HEAD_DIM ** -0.5

    def body(q_ref, k_ref, v_ref, o_ref, kb_ref, vb_ref):
        i = pl.program_id(1)

        @pl.when(i == 0)
        def _():
            kb_ref[...] = k_ref[...].astype(BF16)
            vb_ref[...] = v_ref[...].astype(BF16)

        lane, tri_rev, _ = _sb_consts(bq)
        qs = _stack_heads(q_ref[...].astype(F32) * scale, lane)
        trow, scol = _sb_positions(i, bq)

        def steps(j0, carry, n, masked):
            ks = [pl.multiple_of((j0 - t) * SB_BK, SB_BK) for t in range(n)]
            past = [(k + scol) < trow if masked else None for k in ks]
            zs = [lax.dot_general(qs, kb_ref[pl.ds(k, SB_BK), :], NT, preferred_element_type=F32) for k in ks]
            lm = [_sb_logits(z, p) for z, p in zip(zs, past)]
            cs = [_cumsum2(m, tri_rev) for _, m in lm]
            c, acc = carry
            for t in range(n):
                a = jnp.exp(lm[t][0] + (cs[t][0] + c))
                if masked:
                    a = jnp.where(past[t], a, 0.0)
                acc = acc + lax.dot_general(a.astype(BF16), vb_ref[pl.ds(ks[t], SB_BK), :], NN,
                                            preferred_element_type=F32)
                c = c + cs[t][1]
            return c, acc

        carry = (jnp.zeros((2 * bq, 1), F32), jnp.zeros((2 * bq, LANES), F32))
        carry = steps(i * nd + nd - 1, carry, nd, True)
        older = i * nd
        groups = older // SB_GROUP
        carry = lax.fori_loop(
            0, groups, lambda t, cr: steps(older - 1 - t * SB_GROUP, cr, SB_GROUP, False), carry)
        rest = older - groups * SB_GROUP
        carry = lax.fori_loop(0, rest // nd, lambda t, cr: steps(rest - 1 - t * nd, cr, nd, False), carry)
        o_ref[...] = _unstack_heads(carry[1], lane).astype(BF16)

    return pl.pallas_call(
        body, grid=(npair, S // bq), name=name, in_specs=[qspec, kspec, vspec],
        out_specs=pl.BlockSpec((bq, LANES), lambda p, i: (i, p)),
        out_shape=jax.ShapeDtypeStruct((S, D), BF16),
        scratch_shapes=[pltpu.VMEM((S, LANES), BF16), pltpu.VMEM((S, LANES), BF16)],
        compiler_params=_params("parallel", "arbitrary"),
    )(proj, proj, proj)


def sb_bwd(proj, do, D, name, dep=None):
    S = proj.shape[0]
    bq = min(SB_BQ_BWD, S)
    nd = bq // SB_BK
    nkb = S // SB_BK
    npair, qspec, kspec, vspec = _sb_specs(S, D, bq)
    scale = SB_HEAD_DIM ** -0.5

    def body(q_ref, k_ref, v_ref, do_ref, *refs):
        dq_ref, dk_ref, dv_ref, kb_ref, vb_ref, dk_acc, dv_acc, g_scr, b_scr, a_scr = refs[len(deps):]
        i = pl.program_id(1)

        @pl.when(i == 0)
        def _():
            kb_ref[...] = k_ref[...].astype(BF16)
            vb_ref[...] = v_ref[...].astype(BF16)
            dk_acc[...] = jnp.zeros_like(dk_acc)
            dv_acc[...] = jnp.zeros_like(dv_acc)

        lane, tri_rev, tri_fwd = _sb_consts(bq)
        qs = _stack_heads(q_ref[...].astype(F32) * scale, lane)
        dos = _stack_heads(do_ref[...].astype(F32), lane)
        qs_t, dos_t = qs.T, dos.T
        trow, scol = _sb_positions(i, bq)

        def sweep1(j0, c, n, masked):
            js = [j0 - t for t in range(n)]
            ks = [pl.multiple_of(j * SB_BK, SB_BK) for j in js]
            past = [(k + scol) < trow if masked else None for k in ks]
            zs = [lax.dot_general(qs, kb_ref[pl.ds(k, SB_BK), :], NT, preferred_element_type=F32) for k in ks]
            das = [lax.dot_general(dos, vb_ref[pl.ds(k, SB_BK), :], NT, preferred_element_type=F32) for k in ks]
            lm = [_sb_logits(z, p) for z, p in zip(zs, past)]
            cs = [_cumsum2(m, tri_rev) for _, m in lm]
            for t in range(n):
                b_scr[js[t]] = jnp.exp(lm[t][0]).astype(BF16)
            for t in range(n):
                a = jnp.exp(lm[t][0] + (cs[t][0] + c))
                if masked:
                    a = jnp.where(past[t], a, 0.0)
                g_scr[js[t]] = (das[t] * a).astype(BF16)
                a_scr[js[t]] = a.astype(BF16)
                c = c + cs[t][1]
            return c

        older = i * nd
        groups = older // SB_GROUP
        rest = older - groups * SB_GROUP
        c = jnp.zeros((2 * bq, 1), F32)
        c = sweep1(i * nd + nd - 1, c, nd, True)
        c = lax.fori_loop(0, groups, lambda t, cr: sweep1(older - 1 - t * SB_GROUP, cr, SB_GROUP, False), c)
        lax.fori_loop(0, rest // nd, lambda t, cr: sweep1(rest - 1 - t * nd, cr, nd, False), c)

        def sweep2(j0, carry, n, masked):
            js = [j0 + t for t in range(n)]
            ks = [pl.multiple_of(j * SB_BK, SB_BK) for j in js]
            g16 = [g_scr[j] for j in js]
            gv = [g.astype(F32) for g in g16]
            gs = [(lax.dot_general(g, tri_fwd, NN, preferred_element_type=F32), jnp.sum(v, axis=1, keepdims=True))
                  for g, v in zip(g16, gv)]
            pc, dq = carry
            dzs = []
            for t in range(n):
                dz = gv[t] - b_scr[js[t]].astype(F32) * (gv[t] + (gs[t][0] + pc))
                if masked:
                    dz = jnp.where((ks[t] + scol) < trow, dz, 0.0)
                dzs.append(dz.astype(BF16))
                pc = pc + gs[t][1]
            for t in range(n):
                dq = dq + lax.dot_general(dzs[t], kb_ref[pl.ds(ks[t], SB_BK), :], NN, preferred_element_type=F32)
                dk_acc[js[t]] += lax.dot_general(qs_t, dzs[t], NN, preferred_element_type=F32)
                dv_acc[js[t]] += lax.dot_general(dos_t, a_scr[js[t]], NN, preferred_element_type=F32)
            return pc, dq

        carry = (jnp.zeros((2 * bq, 1), F32), jnp.zeros((2 * bq, LANES), F32))
        carry = lax.fori_loop(0, groups, lambda t, cr: sweep2(t * SB_GROUP, cr, SB_GROUP, False), carry)
        carry = lax.fori_loop(
            0, rest // nd, lambda t, cr: sweep2(groups * SB_GROUP + t * nd, cr, nd, False), carry)
        carry = sweep2(i * nd, carry, nd, True)
        dq_ref[...] = (_unstack_heads(carry[1], lane) * scale).astype(BF16)

        @pl.when(i == pl.num_programs(1) - 1)
        def _():
            for j in range(nkb):
                dk_ref[j * SB_BK:(j + 1) * SB_BK, :] = dk_acc[j].T.astype(BF16)
                dv_ref[j * SB_BK:(j + 1) * SB_BK, :] = dv_acc[j].T.astype(BF16)

    deps = [] if dep is None else [dep]
    full = pl.BlockSpec((S, LANES), lambda p, i: (0, p))
    blk = pl.BlockSpec((bq, LANES), lambda p, i: (i, p))
    act = jax.ShapeDtypeStruct((S, D), BF16)
    return pl.pallas_call(
        body, grid=(npair, S // bq), name=name, in_specs=[qspec, kspec, vspec, blk] + [ANY] * len(deps),
        out_specs=[blk, full, full], out_shape=[act, act, act],
        scratch_shapes=[pltpu.VMEM((S, LANES), BF16), pltpu.VMEM((S, LANES), BF16),
                        pltpu.VMEM((nkb, LANES, SB_BK), F32), pltpu.VMEM((nkb, LANES, SB_BK), F32),
                        pltpu.VMEM((nkb, 2 * bq, SB_BK), BF16), pltpu.VMEM((nkb, 2 * bq, SB_BK), BF16),
                        pltpu.VMEM((nkb, 2 * bq, SB_BK), BF16)],
        compiler_params=_params("parallel", "arbitrary"),
    )(proj, proj, proj, do, *deps)


def _rms_rows(v):
    r = lax.rsqrt(jnp.mean(v * v, axis=-1, keepdims=True) + EPS)
    return v * r, r


def _xa_specs(S, D, M, tq):
    nh = D // X_HEAD_DIM
    qspec = pl.BlockSpec((tq, X_HEAD_DIM), lambda h, i: (i, 6 * nh + h))
    kspec = pl.BlockSpec((M, X_HEAD_DIM), lambda h, i: (0, h))
    vspec = pl.BlockSpec((M, X_HEAD_DIM), lambda h, i: (0, nh + h))
    gspec = pl.BlockSpec((1, X_HEAD_DIM), lambda h, i: (0, 0))
    return nh, qspec, kspec, vspec, gspec


def xa_fwd(proj, kv, gq, gk, D, name):
    S, M = proj.shape[0], kv.shape[0]
    tq = _tile(S, 2048)
    nh, qspec, kspec, vspec, gspec = _xa_specs(S, D, M, tq)
    scale = X_HEAD_DIM ** -0.5

    def body(q_ref, k_ref, v_ref, gq_ref, gk_ref, o_ref):
        qn = _rms_rows(q_ref[...].astype(F32))[0] * gq_ref[...]
        kn = _rms_rows(k_ref[...])[0] * gk_ref[...]
        s = _dot(qn, kn, NT) * scale
        e = jnp.exp(s - jnp.max(s, axis=-1, keepdims=True))
        p = e / jnp.sum(e, axis=-1, keepdims=True)
        o_ref[...] = _dot(p, v_ref[...], NN).astype(BF16)

    return pl.pallas_call(
        body, grid=(nh, S // tq), name=name, in_specs=[qspec, kspec, vspec, gspec, gspec],
        out_specs=pl.BlockSpec((tq, X_HEAD_DIM), lambda h, i: (i, h)),
        out_shape=jax.ShapeDtypeStruct((S, D), BF16), compiler_params=_params("parallel", "parallel"),
    )(proj, kv, kv, gq, gk)


def xa_bwd(proj, kv, gq, gk, do, D, name):
    S, M = proj.shape[0], kv.shape[0]
    tq = _tile(S, 2048)
    nh, qspec, kspec, vspec, gspec = _xa_specs(S, D, M, tq)
    scale = X_HEAD_DIM ** -0.5

    def body(q_ref, k_ref, v_ref, gq_ref, gk_ref, do_ref, dq_ref, dk_ref, dv_ref, ggq_ref, ggk_ref,
             dkn_acc, dv_acc):
        h, i = pl.program_id(0), pl.program_id(1)

        @pl.when((h == 0) & (i == 0))
        def _():
            ggq_ref[...] = jnp.zeros_like(ggq_ref)
            ggk_ref[...] = jnp.zeros_like(ggk_ref)

        @pl.when(i == 0)
        def _():
            dkn_acc[...] = jnp.zeros_like(dkn_acc)
            dv_acc[...] = jnp.zeros_like(dv_acc)

        gq, gk = gq_ref[...], gk_ref[...]
        qhat, rq = _rms_rows(q_ref[...].astype(F32))
        khat, rk = _rms_rows(k_ref[...])
        qn, kn = qhat * gq, khat * gk
        s = _dot(qn, kn, NT) * scale
        e = jnp.exp(s - jnp.max(s, axis=-1, keepdims=True))
        p = e / jnp.sum(e, axis=-1, keepdims=True)
        dov = do_ref[...]
        dv_acc[...] += _dot(p, dov, TN)
        dp = _dot(dov, v_ref[...], NT)
        ds = (p * (dp - jnp.sum(dp * p, axis=-1, keepdims=True))) * scale
        dqn = _dot(ds, kn, NN)
        dkn_acc[...] += _dot(ds, qn, TN)
        ggq_ref[...] += jnp.sum(dqn * qhat, axis=0, keepdims=True)
        dqh = dqn * gq
        dq_ref[...] = (rq * (dqh - qhat * jnp.mean(dqh * qhat, axis=-1, keepdims=True))).astype(BF16)

        @pl.when(i == pl.num_programs(1) - 1)
        def _():
            dkn = dkn_acc[...]
            ggk_ref[...] += jnp.sum(dkn * khat, axis=0, keepdims=True)
            dkh = dkn * gk
            dk_ref[...] = (rk * (dkh - khat * jnp.mean(dkh * khat, axis=-1, keepdims=True))).astype(BF16)
            dv_ref[...] = dv_acc[...].astype(BF16)

    blk = pl.BlockSpec((tq, X_HEAD_DIM), lambda h, i: (i, h))
    kv_shape = jax.ShapeDtypeStruct((M, 2 * D), BF16)
    gshape = jax.ShapeDtypeStruct((1, X_HEAD_DIM), F32)
    dq, dk, dv, ggq, ggk = pl.pallas_call(
        body, grid=(nh, S // tq), name=name, in_specs=[qspec, kspec, vspec, gspec, gspec, blk],
        out_specs=[blk, kspec, vspec, gspec, gspec],
        out_shape=[jax.ShapeDtypeStruct((S, D), BF16), kv_shape, kv_shape, gshape, gshape],
        scratch_shapes=[pltpu.VMEM((M, X_HEAD_DIM), F32), pltpu.VMEM((M, X_HEAD_DIM), F32)],
        compiler_params=_params("arbitrary", "arbitrary"),
    )(proj, kv, kv, gq, gk, do)
    d_kv = jnp.concatenate([dk[:, :D], dv[:, D:]], axis=1)
    return dq, d_kv, ggq, ggk


def _gate_specs(S, D, tm):
    row = pl.BlockSpec((tm, D), lambda i: (i, 0))
    gate = lambda b: pl.BlockSpec((tm, D), lambda i, b=b: (i, 7 + b))
    return row, gate


def merge_fwd(proj, branches, w_branches, w_out, x, g_next, D, name):
    S = proj.shape[0]
    tm = _tile(S, 256)
    row, gate = _gate_specs(S, D, tm)
    mat = pl.BlockSpec((D, D), lambda i: (0, 0))

    def body(g0, g1, g2, b0, b1, b2, w0, w1, w2, w_ref, x_ref, gn_ref, y0, y1, y2, m_ref, o_ref, h_ref):
        acc = None
        for g_ref, b_ref, wb_ref, y_ref in ((g0, b0, w0, y0), (g1, b1, w1, y1), (g2, b2, w2, y2)):
            y = _dot(b_ref[...], wb_ref[...], NN).astype(BF16)
            y_ref[...] = y
            term = jax.nn.sigmoid(g_ref[...].astype(F32)) * y.astype(F32)
            acc = term if acc is None else acc + term
        merged = acc.astype(BF16)
        m_ref[...] = merged
        x1 = x_ref[...] + _dot(merged, w_ref[...], NN)
        o_ref[...] = x1
        h_ref[...] = (_rms_rows(x1)[0] * gn_ref[...]).astype(BF16)

    act = jax.ShapeDtypeStruct((S, D), BF16)
    out = pl.pallas_call(
        body, grid=(S // tm,), name=name,
        in_specs=[gate(0), gate(1), gate(2), row, row, row, mat, mat, mat, mat, row,
                  pl.BlockSpec((1, D), lambda i: (0, 0))],
        out_specs=[row] * 6, out_shape=[act, act, act, act, jax.ShapeDtypeStruct((S, D), F32), act],
        compiler_params=_params("parallel"),
    )(proj, proj, proj, *branches, *w_branches, w_out, x, g_next)
    return out[:3], out[3], out[4], out[5]


def merge_bwd(proj, ys, dx1, w_out, w_branches, D, name, dep):
    S = proj.shape[0]
    tm = _tile(S, 256)
    row, gate = _gate_specs(S, D, tm)
    mat = pl.BlockSpec((D, D), lambda i: (0, 0))

    def body(g0, g1, g2, y0, y1, y2, dx_ref, w_ref, w0, w1, w2, dep_ref, d0, d1, d2, p0, p1, p2, dg_ref):
        dmv = _dot(dx_ref[...], w_ref[...], NT)
        for b, (g_ref, y_ref, wb_ref, d_ref, p_ref) in enumerate(
                ((g0, y0, w0, d0, p0), (g1, y1, w1, d1, p1), (g2, y2, w2, d2, p2))):
            s = jax.nn.sigmoid(g_ref[...].astype(F32))
            dy = (dmv * s).astype(BF16)
            d_ref[...] = dy
            p_ref[...] = _dot(dy, wb_ref[...], NT).astype(BF16)
            dg_ref[:, b * D:(b + 1) * D] = ((dmv * y_ref[...].astype(F32)) * (s * (1.0 - s))).astype(BF16)

    act = jax.ShapeDtypeStruct((S, D), BF16)
    out = pl.pallas_call(
        body, grid=(S // tm,), name=name,
        in_specs=[gate(0), gate(1), gate(2), row, row, row, row, mat, mat, mat, mat, ANY],
        out_specs=[row] * 6 + [pl.BlockSpec((tm, 3 * D), lambda i: (i, 0))],
        out_shape=[act] * 6 + [jax.ShapeDtypeStruct((S, 3 * D), BF16)], compiler_params=_params("parallel"),
    )(proj, proj, proj, *ys, dx1, w_out, *w_branches, dep)
    return out[:3], out[3:6], out[6]


def _rows2d(a):
    return a.reshape(-1, a.shape[-1])


def _ew_call(fn, ins, out_dtypes, name):
    R, C = ins[0].shape
    tr = _tile(R, max(8, ROW_BLOCK_ELEMS // C))
    spec = pl.BlockSpec((tr, C), lambda i: (i, 0))

    def body(*refs):
        outs = fn(*[r[...] for r in refs[:len(ins)]])
        for o_ref, o in zip(refs[len(ins):], outs):
            o_ref[...] = o.astype(o_ref.dtype)

    return pl.pallas_call(
        body, grid=(R // tr,), name=name, in_specs=[spec] * len(ins), out_specs=[spec] * len(out_dtypes),
        out_shape=[jax.ShapeDtypeStruct((R, C), d) for d in out_dtypes], compiler_params=_params("parallel"),
    )(*ins)


def adamw(w, g, m, v, name):
    def fn(w, g, m, v):
        m = ADAM_B1 * m + (1.0 - ADAM_B1) * g
        v = ADAM_B2 * v + (1.0 - ADAM_B2) * (g * g)
        m_hat = m / (1.0 - ADAM_B1 ** ADAM_STEP)
        v_hat = v / (1.0 - ADAM_B2 ** ADAM_STEP)
        return -ADAM_LR * (m_hat / (jnp.sqrt(v_hat) + ADAM_EPS) + ADAM_WD * w), m, v

    shp = w.shape
    outs = _ew_call(fn, [_rows2d(a) for a in (w, g, m, v)], [F32, F32, F32], name)
    return [o.reshape(shp) for o in outs]


def _placed_call(fn, place, grid, ins, in_specs, out_shape, out_specs, name, dep=None):
    n = len(ins)
    deps = [] if dep is None else [dep]

    def body(place_ref, *refs):
        outs = fn(*[r[...] for r in refs[:n]])
        for o_ref, o in zip(refs[n + len(deps):], outs):
            o_ref[...] = o.astype(o_ref.dtype)

    return pl.pallas_call(
        body, name=name, out_shape=out_shape,
        grid_spec=pltpu.PrefetchScalarGridSpec(
            num_scalar_prefetch=1, grid=grid, in_specs=list(in_specs) + [ANY] * len(deps), out_specs=out_specs),
        compiler_params=_params(*["parallel"] * len(grid)),
    )(place, *ins, *deps)


def _row_tile(R, C):
    return _tile(R, max(16, ROW_BLOCK_ELEMS // C))


def cast_into_full(w, place, name, dep=None):
    R, C = w.shape
    tr = _row_tile(R, C)
    return _placed_call(
        lambda a: (a,), place, (R // tr,), [w], [pl.BlockSpec((tr, C), lambda i, p: (i, 0))],
        [jax.ShapeDtypeStruct((N_CHIP, R, C), BF16)], [pl.BlockSpec((None, tr, C), lambda i, p: (p[0], i, 0))],
        name, dep=dep)[0]


def pair_sum(g4, got, place, name):
    _, hr, C = got.shape
    tr = _row_tile(hr, C)
    nb = hr // tr
    blk = pl.BlockSpec((None, tr, C), lambda s, i, p: (s, i, 0))
    return _placed_call(
        lambda a, b: (a + b,), place, (N_CHIP, nb), [g4, got],
        [pl.BlockSpec((None, tr, C), lambda s, i, p: (s, p[1] * nb + i, 0)), blk],
        [jax.ShapeDtypeStruct(got.shape, BF16)], [blk], name)[0]


def chip_sum(p32, got, place, name, dep=None):
    _, H, C = p32.shape
    tr = _row_tile(H, C)
    nb = H // tr
    peer = lambda j: pl.BlockSpec((None, tr, C), lambda i, p, j=j: (j, i, 0))
    return _placed_call(
        lambda a, b, c, d: (((a.astype(F32) + b.astype(F32)) + c.astype(F32)) + d.astype(F32),), place, (nb,),
        [p32, got, got, got], [pl.BlockSpec((None, tr, C), lambda i, p: (p[0], i, 0)), peer(0), peer(1), peer(2)],
        [jax.ShapeDtypeStruct((2 * H, C), F32)], [pl.BlockSpec((tr, C), lambda i, p: (p[1] * nb + i, 0))],
        name, dep=dep)[0]


ANY = pl.BlockSpec(memory_space=pl.ANY)
CHIP_FLIPS = ((1, 0), (0, 1), (1, 1))


def _place():
    return lax.axis_index("x"), lax.axis_index("y"), lax.axis_index("c")


def _flip(v, f):
    return 1 - v if f else v


def join_halves(fulls, name):
    n = len(fulls)

    def body(*refs):
        outs = refs[n:2 * n]
        send_sem, recv_sem = refs[2 * n:]
        x, y, c = _place()
        copies = []
        for a in range(n):
            hr = outs[a].shape[0] // 2
            half = outs[a].at[pl.ds(c * hr, hr), :]
            cp = pltpu.make_async_remote_copy(
                src_ref=half, dst_ref=half, send_sem=send_sem.at[a], recv_sem=recv_sem.at[a],
                device_id=(x, y, 1 - c), device_id_type=MESH)
            cp.start()
            copies.append(cp)
        for a, cp in enumerate(copies):
            hr = outs[a].shape[0] // 2
            theirs = outs[a].at[pl.ds((1 - c) * hr, hr), :]
            cp.wait_send()
            pltpu.make_async_remote_copy(
                src_ref=theirs, dst_ref=theirs, send_sem=send_sem.at[a], recv_sem=recv_sem.at[a],
                device_id=(x, y, 1 - c), device_id_type=MESH).wait_recv()

    dma = pltpu.SemaphoreType.DMA
    return pl.pallas_call(
        body, name=name, in_specs=[ANY] * n, out_specs=[ANY] * n,
        out_shape=[jax.ShapeDtypeStruct(f.shape, F32) for f in fulls],
        input_output_aliases={a: a for a in range(n)},
        scratch_shapes=[dma((n,)), dma((n,))],
    )(*fulls)


HBM = pl.BlockSpec(memory_space=pltpu.HBM)
SEM = pl.BlockSpec(memory_space=pltpu.SEMAPHORE)
EFFECT = pltpu.SideEffectType.DATAFLOW_SIDE_EFFECTING


def _in_hbm(a):
    return pltpu.with_memory_space_constraint(a, pltpu.HBM)


def _gather_half(ref, chip_idx, core):
    hr = ref.shape[1] // 2
    return ref.at[chip_idx, pl.ds(core * hr, hr), :]


def gather_forward(fulls, small, name):
    n = len(fulls)

    def body(*refs):
        small_in = refs[n]
        outs, small_out = refs[n + 1:2 * n + 1], refs[2 * n + 1]
        send_sem, recv_sem, sm_send, sm_recv, loc_sem = refs[2 * n + 2:]
        x, y, c = _place()
        mine = 2 * x + y
        chips = [(_flip(x, fx), _flip(y, fy)) for fx, fy in CHIP_FLIPS]
        local = pltpu.make_async_copy(small_in, small_out.at[mine], loc_sem)
        local.start()
        copies = []
        for j, (px, py) in enumerate(chips):
            cp = pltpu.make_async_remote_copy(
                src_ref=small_in, dst_ref=small_out.at[mine], send_sem=sm_send.at[j], recv_sem=sm_recv.at[j],
                device_id=(px, py, c), device_id_type=MESH)
            cp.start()
            copies.append(cp)
        for a in range(n):
            for j, (px, py) in enumerate(chips):
                src = _gather_half(outs[a], 2 * px + py, c)
                cp = pltpu.make_async_remote_copy(
                    src_ref=src, dst_ref=src, send_sem=send_sem.at[3 * a + j], recv_sem=recv_sem.at[3 * a + j],
                    device_id=(x, y, 1 - c), device_id_type=MESH)
                cp.start()
                copies.append(cp)
        for a in range(n):
            for j, (px, py) in enumerate(chips):
                dst = _gather_half(outs[a], 2 * px + py, 1 - c)
                pltpu.make_async_remote_copy(
                    src_ref=dst, dst_ref=dst, send_sem=send_sem.at[3 * a + j], recv_sem=recv_sem.at[3 * a + j],
                    device_id=(x, y, 1 - c), device_id_type=MESH).wait_recv()
        for j, (px, py) in enumerate(chips):
            dst = small_out.at[2 * px + py]
            pltpu.make_async_remote_copy(
                src_ref=dst, dst_ref=dst, send_sem=sm_send.at[j], recv_sem=sm_recv.at[j],
                device_id=(px, py, c), device_id_type=MESH).wait_recv()
        for cp in copies:
            cp.wait_send()
        local.wait()

    dma = pltpu.SemaphoreType.DMA
    out = pl.pallas_call(
        body, name=name, in_specs=[ANY] * (n + 1), out_specs=[ANY] * (n + 1),
        out_shape=[jax.ShapeDtypeStruct(f.shape, f.dtype) for f in fulls]
        + [jax.ShapeDtypeStruct((N_CHIP,) + small.shape, small.dtype)],
        input_output_aliases={a: a for a in range(n)},
        scratch_shapes=[dma((3 * n,)), dma((3 * n,)), dma((3,)), dma((3,)), dma],
    )(*fulls, small)
    return out[:n], out[n]


def _gather_plan(fulls, lands):
    x, y, c = _place()
    mine = 2 * x + y
    return [(_gather_half(f, mine, c), _gather_half(f, mine, c), (_flip(x, fx), _flip(y, fy), c))
            for f in fulls for fx, fy in CHIP_FLIPS]


def _scatter_plan(parts, lands):
    x, y, c = _place()
    plan = []
    for p, l in zip(parts, lands):
        for j, (fx, fy) in enumerate(CHIP_FLIPS):
            px, py = _flip(x, fx), _flip(y, fy)
            plan.append((p.at[2 * px + py], l.at[j], (px, py, c)))
    return plan


def _scatter_lands(parts):
    return [(3,) + p.shape[1:] for p in parts]


def _exchange_plan(grads, lands):
    x, y, c = _place()
    plan = []
    for g, l in zip(grads, lands):
        hr = g.shape[1] // 2
        plan.append((g.at[:, pl.ds((1 - c) * hr, hr), :], l, (x, y, 1 - c)))
    return plan


def _exchange_lands(grads):
    return [(N_CHIP, g.shape[1] // 2, g.shape[2]) for g in grads]


def _join_plan(fulls, lands):
    x, y, c = _place()
    plan = []
    for f in fulls:
        hr = f.shape[0] // 2
        half = f.at[pl.ds(c * hr, hr), :]
        plan.append((half, half, (x, y, 1 - c)))
    return plan


def _forward_plan(fulls, lands):
    x, y, c = _place()
    return [(_gather_half(f, 2 * _flip(x, fx) + _flip(y, fy), c), _gather_half(f, 2 * _flip(x, fx) + _flip(y, fy), c),
             (x, y, 1 - c)) for f in fulls for fx, fy in CHIP_FLIPS]


def split_start(plan, copies, srcs, land_shapes, deps, name):
    n, m = len(srcs), len(land_shapes)
    lands = [lax.empty(s, srcs[0].dtype) for s in land_shapes]

    def body(*refs):
        k0 = n + m + len(deps)
        send_sem, recv_sem = refs[k0], refs[k0 + 1]
        thru, token = refs[k0 + 2:k0 + 2 + n + m], refs[k0 + 2 + n + m]
        for k, (src, dst, dev) in enumerate(plan(thru[:n], thru[n:])):
            pltpu.make_async_remote_copy(src_ref=src, dst_ref=dst, send_sem=send_sem.at[k], recv_sem=recv_sem.at[k],
                                         device_id=dev, device_id_type=MESH).start()
        token[...] = jnp.zeros_like(token)

    dma = pltpu.SemaphoreType.DMA
    arrays = list(srcs) + lands
    out = pl.pallas_call(
        body, name=name,
        out_shape=(dma((copies,)), dma((copies,)), *[pltpu.HBM(a.shape, a.dtype) for a in arrays],
                   jax.ShapeDtypeStruct((8, LANES), F32)),
        in_specs=[HBM] * (n + m) + [ANY] * len(deps),
        out_specs=(SEM, SEM, *[HBM] * (n + m), pl.BlockSpec(memory_space=pltpu.VMEM)),
        input_output_aliases={a: 2 + a for a in range(n + m)},
        compiler_params=pltpu.CompilerParams(has_side_effects=EFFECT),
    )(*[_in_hbm(a) for a in arrays], *deps)
    return (out[0], out[1], list(out[2:2 + n]), list(out[2 + n:2 + n + m])), out[2 + n + m]


def split_wait(plan, handle, after, name):
    send_sem, recv_sem, srcs, lands = handle
    n, m = len(srcs), len(lands)

    def body(*refs):
        send_sem, recv_sem = refs[n + m], refs[n + m + 1]
        thru = refs[n + m + 2 + len(after):]
        for k, (src, dst, dev) in enumerate(plan(thru[:n], thru[n:])):
            cp = pltpu.make_async_remote_copy(src_ref=src, dst_ref=dst, send_sem=send_sem.at[k],
                                              recv_sem=recv_sem.at[k], device_id=dev, device_id_type=MESH)
            cp.wait_send()
            cp.wait_recv()

    arrays = list(srcs) + list(lands)
    out = pl.pallas_call(
        body, name=name, out_shape=tuple(pltpu.HBM(a.shape, a.dtype) for a in arrays),
        in_specs=[HBM] * (n + m) + [SEM, SEM] + [ANY] * len(after), out_specs=tuple([HBM] * (n + m)),
        input_output_aliases={a: a for a in range(n + m)},
        compiler_params=pltpu.CompilerParams(has_side_effects=EFFECT),
    )(*arrays, send_sem, recv_sem, *after)
    return list(out[:n]), list(out[n:])


def allreduce_small(block, name, dep):
    R, C = block.shape

    def body(in_ref, dep_ref, out_ref, slots, send_sem, recv_sem):
        x, y, c = _place()
        me = 4 * x + 2 * y + c
        slots[me] = in_ref[...]
        copies = []
        for r in range(1, 8):
            fx, fy, fc = (r >> 2) & 1, (r >> 1) & 1, r & 1
            cp = pltpu.make_async_remote_copy(
                src_ref=in_ref, dst_ref=slots.at[me], send_sem=send_sem.at[r - 1], recv_sem=recv_sem.at[r - 1],
                device_id=(_flip(x, fx), _flip(y, fy), _flip(c, fc)), device_id_type=MESH)
            cp.start()
            copies.append(cp)
        for cp in copies:
            cp.wait()
        acc = slots[0]
        for d in range(1, 8):
            acc = acc + slots[d]
        out_ref[...] = acc

    vm = pl.BlockSpec(memory_space=pltpu.VMEM)
    dma = pltpu.SemaphoreType.DMA
    return pl.pallas_call(
        body, name=name, in_specs=[vm, ANY], out_specs=vm, out_shape=jax.ShapeDtypeStruct((R, C), F32),
        scratch_shapes=[pltpu.VMEM((8, R, C), F32), dma((7,)), dma((7,))],
    )(block, dep)


def local_step(x, mem, target, g_mix, g_mem, q_norm_g, k_norm_g, g_mlp, conv_w, h, mem_n, proj_part, place, w_in, w_in_dep,
               rest_weights, first_grads, early_grads, mid_grads, late_grads, last_grads):
    S, D = x.shape
    proj = proj_rest(h, w_in, proj_part, place, "proj", dep=w_in_dep)
    a_conv = conv_fwd(proj, conv_w, D, "conv_fwd")
    o_sb = sb_fwd(proj, D, "sb_fwd")
    w_conv_out, w_sb_out, w_mem_kv, w_x_out, w_out, mlp_dep, mlp_weights = rest_weights(o_sb)
    kv = mm_nn_shard(mem_n, w_mem_kv, "kv", dep=mlp_dep)
    o_x = xa_fwd(proj, kv, q_norm_g, k_norm_g, D, "xa_fwd")
    ys, merged, x1, h2 = merge_fwd(proj, (a_conv, o_sb, o_x), (w_conv_out, w_sb_out, w_x_out), w_out, x, g_mlp, D,
                                   "merge_x1")
    w_up, w_down = mlp_weights(h2)
    up, act = mm_nn_shard(h2, w_up, "up", relu2=True)
    dy, dy16, loss_parts = mm_nn_loss(act, w_down, x1, target, "x2_loss")
    loss_cols = jnp.sum(loss_parts, axis=0)
    d_up = mm_nt_relu2(dy16, w_down, up, "d_up")
    g = {"w_down": mm_tn(act, dy16, "g_w_down")}
    g["w_up"] = mm_tn(h2, d_up, "g_w_up", shard_out=True)
    dh2 = mm_nt_shard(d_up, w_up, "dh2")
    dx1, dx1_16, g["g_mlp"] = rms_bwd(x1, g_mlp, dh2, "rms_mlp_bwd", dres=dy, want16=True)
    g["w_out"] = mm_tn(merged, dx1_16, "g_w_out")
    (dy_c, dy_s, dy_x), (d_a_conv, d_o_sb, d_o_x), d_gate = merge_bwd(
        proj, ys, dx1_16, w_out, (w_conv_out, w_sb_out, w_x_out), D, "merge_bwd", dep=first_grads(g))
    g["w_conv_out"] = mm_tn(a_conv, dy_c, "g_w_conv_out")
    g["w_sb_out"] = mm_tn(o_sb, dy_s, "g_w_sb_out")
    g["w_x_out"] = mm_tn(o_x, dy_x, "g_w_x_out")
    d_xq, d_kv, g["q_norm_g"], g["k_norm_g"] = xa_bwd(proj, kv, q_norm_g, k_norm_g, d_o_x, D, "xa_bwd")
    g["w_mem_kv"] = mm_tn(mem_n, d_kv, "g_w_mem_kv", shard_out=True)
    g["g_mem"] = rms_bwd(mem, g_mem, mm_nt_shard(d_kv, w_mem_kv, "d_mem_n"), "rms_mem_bwd", want_dx=False)
    d_ch, d_cb, d_cc, g["conv_w"] = conv_bwd(proj, conv_w, d_a_conv, D, "conv_bwd", dep=early_grads(g))
    dq, dk, dv = sb_bwd(proj, d_o_sb, D, "sb_bwd", dep=mid_grads([d_ch]))
    d_proj = jnp.concatenate([d_ch, d_cb, d_cc, dq, dk, dv, d_xq, d_gate], axis=1)
    g["w_in"] = mm_tn(h, d_proj, "g_w_in", shard_out=True)
    dh = mm_nt_shard(d_proj, w_in, "dh", dep=late_grads(g["w_in"]))
    grad_x, g["g_mix"] = rms_bwd(x, g_mix, dh, "rms_mix_bwd", dres=dx1, dep=last_grads(dh))
    return loss_cols, grad_x, g


BIG = ("w_in", "w_conv_out", "w_sb_out", "w_mem_kv", "w_x_out", "w_out", "w_up", "w_down")
REST = BIG[1:]
MLP_SIDE = ("w_out", "w_up", "w_down")
MIXER_SIDE = tuple(k for k in REST if k not in MLP_SIDE)
COL_SHARDED = ("w_in", "w_mem_kv", "w_up")
WEIGHTS = ("g_mix", "g_mem", "w_in", "conv_w", "w_conv_out", "w_sb_out", "q_norm_g", "k_norm_g",
           "w_mem_kv", "w_x_out", "w_out", "g_mlp", "w_up", "w_down")


def _pack_small(D, g_mix, g_mem, g_mlp, q_norm_g, k_norm_g, conv_w, last):
    qk = jnp.concatenate([q_norm_g, k_norm_g, jnp.zeros((1, D - 2 * X_HEAD_DIM), F32)], axis=1)
    cw = jnp.pad(conv_w, ((0, 0), (0, D - conv_w.shape[1])))
    return jnp.concatenate([g_mix, g_mem, g_mlp, qk, cw, last], axis=0)


def kernel(x, mem, g_mix, g_mem, w_in, conv_w, w_conv_out, w_sb_out, q_norm_g, k_norm_g, w_mem_kv, w_x_out, w_out, g_mlp, w_up, w_down, loss_target, m_g_mix, m_g_mem, m_w_in, m_conv_w, m_w_conv_out, m_w_sb_out, m_q_norm_g, m_k_norm_g, m_w_mem_kv, m_w_x_out, m_w_out, m_g_mlp, m_w_up, m_w_down, v_g_mix, v_g_mem, v_w_in, v_conv_w, v_w_conv_out, v_w_sb_out, v_q_norm_g, v_k_norm_g, v_w_mem_kv, v_w_x_out, v_w_out, v_g_mlp, v_w_up, v_w_down):
    S, D = x.shape[1], x.shape[2]
    w = dict(g_mix=g_mix, g_mem=g_mem, w_in=w_in, conv_w=conv_w, w_conv_out=w_conv_out, w_sb_out=w_sb_out,
             q_norm_g=q_norm_g, k_norm_g=k_norm_g, w_mem_kv=w_mem_kv, w_x_out=w_x_out, w_out=w_out,
             g_mlp=g_mlp, w_up=w_up, w_down=w_down)
    m = dict(g_mix=m_g_mix, g_mem=m_g_mem, w_in=m_w_in, conv_w=m_conv_w, w_conv_out=m_w_conv_out,
             w_sb_out=m_w_sb_out, q_norm_g=m_q_norm_g, k_norm_g=m_k_norm_g, w_mem_kv=m_w_mem_kv,
             w_x_out=m_w_x_out, w_out=m_w_out, g_mlp=m_g_mlp, w_up=m_w_up, w_down=m_w_down)
    v = dict(g_mix=v_g_mix, g_mem=v_g_mem, w_in=v_w_in, conv_w=v_conv_w, w_conv_out=v_w_conv_out,
             w_sb_out=v_w_sb_out, q_norm_g=v_q_norm_g, k_norm_g=v_k_norm_g, w_mem_kv=v_w_mem_kv,
             w_x_out=v_w_x_out, w_out=v_w_out, g_mlp=v_g_mlp, w_up=v_w_up, w_down=v_w_down)
    chip = 2 * lax.axis_index("x") + lax.axis_index("y")
    cs = conv_w.shape[2]

    place = jnp.stack([chip, lax.axis_index("c")]).astype(jnp.int32)
    cw_block = jnp.pad(conv_w[0], ((0, 5), (0, 0)))
    handle, token = split_start(_gather_plan, 3, [cast_into_full(w["w_in"][0], place, "cast_w_in")], [], [],
                                "gather_w_in_start")
    rest16 = [cast_into_full(w[k][0], place, "cast_" + k, dep=token) for k in REST]
    h = rms_fwd(x[0], g_mix, "rms_mix", dep=token)
    mem_n = rms_fwd(mem[0], g_mem, "rms_mem", dep=token)
    proj_part = proj_own(h, w["w_in"][0], place, "proj_own", dep=token)
    landed, _ = split_wait(_gather_plan, handle, [*rest16, mem_n, proj_part], "gather_w_in_wait")
    (w_in_full,), cw_all = gather_forward(landed, cw_block, "gather_w_in_forward")
    conv_full = jnp.concatenate([cw_all[p, :3] for p in range(N_CHIP)], axis=1)
    rest_handle, rest_token = split_start(_gather_plan, 3 * len(REST), rest16, [], [w_in_full], "gather_rest_start")

    def layout(k, a):
        return a if k in COL_SHARDED else a.reshape(-1, a.shape[-1])

    def rest_weights(after):
        landed, _ = split_wait(_gather_plan, rest_handle, [after], "gather_rest_wait")
        first = gather_forward(landed[:-2], cw_block, "gather_rest_forward")[0]
        mlp_handle, token = split_start(_forward_plan, 6, landed[-2:], [], [first[0]], "forward_mlp_start")

        def mlp_weights(after):
            both, _ = split_wait(_forward_plan, mlp_handle, [after], "forward_mlp_wait")
            return [layout(k, a) for k, a in zip(REST[-2:], both)]

        return [layout(k, a) for k, a in zip(REST[:-2], first)] + [token, mlp_weights]

    def blocks(k, a):
        return a if k in COL_SHARDED else a.reshape(N_CHIP, -1, a.shape[-1])

    early = {}

    def swap_start(names, g, dep, name):
        g4 = [blocks(k, g[k]) for k in names]
        early[names], token = split_start(_exchange_plan, len(names), g4, _exchange_lands(g4), [dep], name)
        return token

    def first_grads(g):
        return swap_start(MLP_SIDE, g, g["g_mlp"], "exchange_mlp_start")

    def early_grads(g):
        return swap_start(MIXER_SIDE, g, g["g_mem"], "exchange_mixers_start")

    def mid_grads(after):
        pairs = {}
        for names, name in ((MLP_SIDE, "exchange_mlp_wait"), (MIXER_SIDE, "exchange_mixers_wait")):
            g4, got = split_wait(_exchange_plan, early[names], after, name)
            pairs.update(zip(names, zip(g4, got)))
        p16 = [pair_sum(*pairs[k], place, "pair_sum_" + k) for k in REST]
        early["fly"], token = split_start(_scatter_plan, 3 * len(REST), p16, _scatter_lands(p16), [],
                                          "scatter_rest_start")
        return token

    late = {}

    def late_grads(gw):
        late["swap"], token = split_start(_exchange_plan, 1, [gw], _exchange_lands([gw]), [], "exchange_w_in_start")
        return token

    def last_grads(after):
        (gw,), (got,) = split_wait(_exchange_plan, late["swap"], [after], "exchange_w_in_wait")
        p16 = [pair_sum(gw, got, place, "pair_sum_w_in")]
        late["fly"], token = split_start(_scatter_plan, 3, p16, _scatter_lands(p16), [], "scatter_w_in_start")
        return token

    loss_cols, grad_x, g = local_step(
        x[0], mem[0], loss_target[0], g_mix, g_mem, q_norm_g, k_norm_g, g_mlp, conv_full, h, mem_n, proj_part, place,
        w_in_full, rest_token, rest_weights, first_grads, early_grads, mid_grads, late_grads, last_grads)
    token = g["g_mix"]

    p16_rest, got_rest = split_wait(_scatter_plan, early["fly"], [token], "scatter_rest_wait")
    gsum, delta, new_m, new_v = {}, {}, {}, {}
    halves = [chip_sum(p, b, place, "chip_sum_" + k, dep=token) for k, p, b in zip(REST, p16_rest, got_rest)]
    for k, a in zip(REST, join_halves(halves, "join_halves_rest")):
        gsum[k] = a[None]
        delta[k], new_m[k], new_v[k] = adamw(w[k], gsum[k], m[k], v[k], "adamw_" + k)
    p16_in, got_in = split_wait(_scatter_plan, late["fly"], [new_v[k] for k in REST], "scatter_w_in_wait")

    half_in = chip_sum(p16_in[0], got_in[0], place, "chip_sum_w_in")
    join_in, join_token = split_start(_join_plan, 1, [half_in], [], [], "join_w_in_start")

    small = allreduce_small(
        _pack_small(D, g["g_mix"], g["g_mem"], g["g_mlp"], g["q_norm_g"], g["k_norm_g"], g["conv_w"], loss_cols),
        "allreduce_small", dep=join_token)
    loss = (0.5 / D) * jnp.sum(small[7])
    gsum.update({"g_mix": small[0:1], "g_mem": small[1:2], "g_mlp": small[2:3],
                 "q_norm_g": small[3:4, :X_HEAD_DIM], "k_norm_g": small[3:4, X_HEAD_DIM:2 * X_HEAD_DIM],
                 "conv_w": lax.dynamic_slice(small[4:7], (0, chip * cs), (3, cs))[None]})
    small_names = ("g_mix", "g_mem", "g_mlp", "q_norm_g", "k_norm_g", "conv_w")
    zero_row = jnp.zeros((1, D), F32)
    packed = [_pack_small(D, *[t[k] if k != "conv_w" else t[k][0] for k in small_names], zero_row)
              for t in (w, gsum, m, v)]
    sm = adamw(*packed, "adamw_small")
    for t, block in zip((delta, new_m, new_v), sm):
        t["g_mix"], t["g_mem"], t["g_mlp"] = block[0:1], block[1:2], block[2:3]
        t["q_norm_g"], t["k_norm_g"] = block[3:4, :X_HEAD_DIM], block[3:4, X_HEAD_DIM:2 * X_HEAD_DIM]
        t["conv_w"] = block[4:7, :cs][None]
    gsum["w_in"] = split_wait(_join_plan, join_in, [sm[2]], "join_w_in_wait")[0][0][None]
    delta["w_in"], new_m["w_in"], new_v["w_in"] = adamw(w["w_in"], gsum["w_in"], m["w_in"], v["w_in"], "adamw_w_in")

    return (loss, grad_x[None], *[gsum[k] for k in WEIGHTS], *[delta[k] for k in WEIGHTS],
            *[new_m[k] for k in WEIGHTS], *[new_v[k] for k in WEIGHTS])
```
